```python
import math
import jax, jax.numpy as jnp
from jax import lax
import numpy as np

D_MODEL = 2048
BATCH = 8
SEQ = 8192
DEPTH = 1

SWA_Q_HEADS = 16
SWA_KV_HEADS = 2
SWA_HEAD_DIM = 64
SWA_WINDOW = 128
GDN_HEADS = 4
GDN_HEAD_DIM = 128
GDN_CONV = 4
GDN_CHUNK = 64
N_MEM = 256
XA_HEADS = 4
XA_HEAD_DIM = 128
D_FF = 4 * D_MODEL
N_BRANCH = 3
RMS_EPS = 1e-6
L2_EPS = 1e-6

SWA_Q_W = SWA_Q_HEADS * SWA_HEAD_DIM
SWA_KV_W = SWA_KV_HEADS * SWA_HEAD_DIM
GDN_W = GDN_HEADS * GDN_HEAD_DIM
XA_W = XA_HEADS * XA_HEAD_DIM
IN_SPLITS = (SWA_Q_W, SWA_KV_W, SWA_KV_W, 3 * GDN_W, GDN_HEADS, GDN_HEADS, GDN_W, XA_W, N_BRANCH * D_MODEL)
IN_WIDTH = sum(IN_SPLITS)

kernel_name = 'hybrid_swa_sink_gdn_memxattn_relu2_block'


def rms_norm(x, g):
    xf = x.astype(jnp.float32)
    y = xf * lax.rsqrt(jnp.mean(xf * xf, axis=-1, keepdims=True) + RMS_EPS)
    return (y * g.astype(jnp.float32)).astype(x.dtype)


def l2_norm(x):
    return x * lax.rsqrt(jnp.sum(x * x, axis=-1, keepdims=True) + L2_EPS)


def split_cols(t, sizes):
    idx, acc = [], 0
    for s in sizes[:-1]:
        acc += s
        idx.append(acc)
    return jnp.split(t, idx, axis=-1)


def sliding_window_attention(q, k, v, sinks):
    B, S, HQ, hd = q.shape
    HKV = k.shape[2]
    G = HQ // HKV
    W = SWA_WINDOW
    nb = S // W
    qb = q.reshape(B, nb, W, HKV, G, hd)
    kb = k.reshape(B, nb, W, HKV, hd)
    vb = v.reshape(B, nb, W, HKV, hd)

    def with_prev(t):
        prev = jnp.pad(t, ((0, 0), (1, 0), (0, 0), (0, 0), (0, 0)))[:, :-1]
        return jnp.concatenate([prev, t], axis=2)

    kc, vc = with_prev(kb), with_prev(vb)
    s = jnp.einsum('bnqhgd,bnkhd->bnhgqk', qb, kc).astype(jnp.float32) * (hd ** -0.5)
    qi = jnp.arange(W)[:, None]
    kj = jnp.arange(2 * W)[None, :]
    band = (kj > qi) & (kj <= qi + W)
    not_pad = (jnp.arange(nb)[:, None, None] > 0) | (kj >= W)[None]
    mask = band[None] & not_pad
    s = jnp.where(mask[None, :, None, None], s, -jnp.inf)
    sk = sinks.astype(jnp.float32).reshape(HKV, G)[None, None, :, :, None, None]
    m = jnp.maximum(jnp.max(s, axis=-1, keepdims=True), sk)
    p = jnp.exp(s - m)
    denom = jnp.sum(p, axis=-1, keepdims=True) + jnp.exp(sk - m)
    pr = (p / denom).astype(v.dtype)
    o = jnp.einsum('bnhgqk,bnkhd->bnqhgd', pr, vc)
    return o.reshape(B, S, HQ * hd)


def causal_depthwise_conv(x, w):
    K, C = w.shape
    return lax.conv_general_dilated(
        x, w.reshape(K, 1, C), window_strides=(1,), padding=[(K - 1, 0)],
        dimension_numbers=('NWC', 'WIO', 'NWC'), feature_group_count=C)


def chunked_gated_delta_rule(q, k, v, g, beta):
    B, S, H, dk = q.shape
    dv = v.shape[-1]
    C = GDN_CHUNK
    N = S // C

    def to_chunks(t):
        return t.reshape(B, N, C, H, -1).transpose(1, 0, 3, 2, 4)

    qc, kc, vc = to_chunks(q), to_chunks(k), to_chunks(v)
    gc = g.reshape(B, N, C, H).transpose(1, 0, 3, 2)
    bc = beta.reshape(B, N, C, H).transpose(1, 0, 3, 2)
    gcum = jnp.cumsum(gc, axis=-1)
    causal = jnp.tril(jnp.ones((C, C), dtype=bool))
    strict = jnp.tril(jnp.ones((C, C), dtype=bool), k=-1)
    decay = jnp.exp(jnp.where(causal, gcum[..., :, None] - gcum[..., None, :], -jnp.inf))
    kk = jnp.einsum('nbhcd,nbhed->nbhce', kc, kc)
    lower = jnp.where(strict, bc[..., :, None] * kk * decay, 0.0)
    a_mat = jnp.eye(C, dtype=q.dtype) + lower
    rhs = jnp.concatenate([vc * bc[..., None], kc * (bc * jnp.exp(gcum))[..., None]], axis=-1)
    sol = lax.linalg.triangular_solve(a_mat, rhs, left_side=True, lower=True, unit_diagonal=True)
    u, w = sol[..., :dv], sol[..., dv:]
    qk = jnp.einsum('nbhcd,nbhed->nbhce', qc, kc) * decay
    q_dec = qc * jnp.exp(gcum)[..., None]
    k_dec = kc * jnp.exp(gcum[..., -1:] - gcum)[..., None]
    g_last = jnp.exp(gcum[..., -1])

    def step(state, inp):
        qk_i, qd_i, kd_i, u_i, w_i, gl_i = inp
        v_new = u_i - jnp.einsum('bhcd,bhde->bhce', w_i, state)
        o = jnp.einsum('bhcd,bhde->bhce', qd_i, state) + jnp.einsum('bhce,bhef->bhcf', qk_i, v_new)
        state = state * gl_i[..., None, None] + jnp.einsum('bhcd,bhce->bhde', kd_i, v_new)
        return state, o

    state0 = jnp.zeros((B, H, dk, dv), dtype=q.dtype)
    _, o = lax.scan(step, state0, (qk, q_dec, k_dec, u, w, g_last))
    return o.transpose(1, 0, 3, 2, 4).reshape(B, S, H, dv)


def gated_deltanet(qkv, a, b, z, conv_w, a_log, dt_bias, norm_w):
    B, S, _ = qkv.shape
    H, dh = GDN_HEADS, GDN_HEAD_DIM
    f32 = jnp.float32
    qkv = jax.nn.silu(causal_depthwise_conv(qkv, conv_w))
    q, k, v = jnp.split(qkv, 3, axis=-1)
    q = l2_norm(q.reshape(B, S, H, dh).astype(f32)) * (dh ** -0.5)
    k = l2_norm(k.reshape(B, S, H, dh).astype(f32))
    v = v.reshape(B, S, H, dh).astype(f32)
    beta = jax.nn.sigmoid(b.astype(f32))
    g = -jnp.exp(a_log.astype(f32)) * jax.nn.softplus(a.astype(f32) + dt_bias.astype(f32))
    o = chunked_gated_delta_rule(q, k, v, g, beta)
    o = rms_norm(o, norm_w) * jax.nn.silu(z.reshape(B, S, H, dh).astype(f32))
    return o.reshape(B, S, H * dh).astype(qkv.dtype)


def memory_cross_attention(q, mkv):
    B, S, _ = q.shape
    q = q.reshape(B, S, XA_HEADS, XA_HEAD_DIM)
    mk, mv = jnp.split(mkv, 2, axis=-1)
    mk = mk.reshape(B, N_MEM, XA_HEADS, XA_HEAD_DIM)
    mv = mv.reshape(B, N_MEM, XA_HEADS, XA_HEAD_DIM)
    s = jnp.einsum('bshd,bmhd->bhsm', q, mk).astype(jnp.float32) * (XA_HEAD_DIM ** -0.5)
    p = jax.nn.softmax(s, axis=-1).astype(mv.dtype)
    return jnp.einsum('bhsm,bmhd->bshd', p, mv).reshape(B, S, XA_W)


def _fwd_setup_inputs(seed: int = 0) -> dict:
    key = jax.random.key(seed)
    ks = jax.random.split(key, 20)
    f32 = jnp.float32
    L, D = DEPTH, D_MODEL

    def nrm(k, shape, scale):
        return jax.random.normal(k, shape, f32) * scale

    x = nrm(ks[0], (BATCH, SEQ, D), 1.0)
    mem = nrm(ks[1], (BATCH, N_MEM, D), 1.0)
    g_mix = 1.0 + nrm(ks[2], (L, D), 0.02)
    w_in = nrm(ks[3], (L, D, IN_WIDTH), D ** -0.5)
    sinks = nrm(ks[4], (L, SWA_Q_HEADS), 0.5)
    conv_w = nrm(ks[5], (L, GDN_CONV, 3 * GDN_W), GDN_CONV ** -0.5)
    a_log = jnp.log(jax.random.uniform(ks[6], (L, GDN_HEADS), f32, 1.0, 16.0))
    dt = jnp.exp(jax.random.uniform(ks[7], (L, GDN_HEADS), f32, math.log(1e-3), math.log(1e-1)))
    dt_bias = dt + jnp.log(-jnp.expm1(-dt))
    gdn_norm_w = 1.0 + nrm(ks[8], (L, GDN_HEAD_DIM), 0.02)
    g_mem = 1.0 + nrm(ks[9], (L, D), 0.02)
    w_mem_kv = nrm(ks[10], (L, D, 2 * XA_W), D ** -0.5)
    w_swa_up = nrm(ks[11], (L, SWA_Q_W, D), SWA_Q_W ** -0.5)
    w_gdn_up = nrm(ks[12], (L, GDN_W, D), GDN_W ** -0.5)
    w_xa_up = nrm(ks[13], (L, XA_W, D), XA_W ** -0.5)
    w_out = nrm(ks[14], (L, D, D), D ** -0.5)
    g_mlp = 1.0 + nrm(ks[15], (L, D), 0.02)
    w_mlp_in = nrm(ks[16], (L, D, D_FF), D ** -0.5)
    w_mlp_out = nrm(ks[17], (L, D_FF, D), D_FF ** -0.5)
    g_final = 1.0 + nrm(ks[18], (D,), 0.02)
    return {'x': x, 'mem': mem, 'g_mix': g_mix, 'w_in': w_in, 'sinks': sinks, 'conv_w': conv_w,
            'a_log': a_log, 'dt_bias': dt_bias, 'gdn_norm_w': gdn_norm_w, 'g_mem': g_mem,
            'w_mem_kv': w_mem_kv, 'w_swa_up': w_swa_up, 'w_gdn_up': w_gdn_up, 'w_xa_up': w_xa_up,
            'w_out': w_out, 'g_mlp': g_mlp, 'w_mlp_in': w_mlp_in, 'w_mlp_out': w_mlp_out,
            'g_final': g_final}


def _fwd_reference(x, mem, g_mix, w_in, sinks, conv_w, a_log, dt_bias, gdn_norm_w, g_mem, w_mem_kv,
              w_swa_up, w_gdn_up, w_xa_up, w_out, g_mlp, w_mlp_in, w_mlp_out, g_final):
    B, S, D = x.shape
    h = x
    for l in range(DEPTH):
        n = rms_norm(h, g_mix[l])
        p = n @ w_in[l]
        q_a, k_a, v_a, qkv_b, a_b, b_b, z_b, q_c, gate_logits = split_cols(p, IN_SPLITS)
        y_a = sliding_window_attention(
            q_a.reshape(B, S, SWA_Q_HEADS, SWA_HEAD_DIM),
            k_a.reshape(B, S, SWA_KV_HEADS, SWA_HEAD_DIM),
            v_a.reshape(B, S, SWA_KV_HEADS, SWA_HEAD_DIM), sinks[l])
        y_b = gated_deltanet(qkv_b, a_b, b_b, z_b, conv_w[l], a_log[l], dt_bias[l], gdn_norm_w[l])
        mkv = rms_norm(mem, g_mem[l]) @ w_mem_kv[l]
        y_c = memory_cross_attention(q_c, mkv)
        g_a, g_b, g_c = jnp.split(jax.nn.sigmoid(gate_logits), N_BRANCH, axis=-1)
        merged = g_a * (y_a @ w_swa_up[l]) + g_b * (y_b @ w_gdn_up[l]) + g_c * (y_c @ w_xa_up[l])
        h = h + merged @ w_out[l]
        u = rms_norm(h, g_mlp[l]) @ w_mlp_in[l]
        h = h + jnp.square(jax.nn.relu(u)) @ w_mlp_out[l]
    return rms_norm(h, g_final)


import jax as _jax
import jax.numpy as _jnp

TWIN_FORMAT = 'train_step'
FWD_PARAMS = ['x', 'mem', 'g_mix', 'w_in', 'sinks', 'conv_w', 'a_log', 'dt_bias', 'gdn_norm_w', 'g_mem', 'w_mem_kv', 'w_swa_up', 'w_gdn_up', 'w_xa_up', 'w_out', 'g_mlp', 'w_mlp_in', 'w_mlp_out', 'g_final']
TWIN_WEIGHTS = ['g_mix', 'w_in', 'sinks', 'conv_w', 'a_log', 'dt_bias', 'gdn_norm_w', 'g_mem', 'w_mem_kv', 'w_swa_up', 'w_gdn_up', 'w_xa_up', 'w_out', 'g_mlp', 'w_mlp_in', 'w_mlp_out', 'g_final']
TWIN_DIFF_INPUT = 'x'
TWIN_INPUTS = ['x', 'mem', 'g_mix', 'w_in', 'sinks', 'conv_w', 'a_log', 'dt_bias', 'gdn_norm_w', 'g_mem', 'w_mem_kv', 'w_swa_up', 'w_gdn_up', 'w_xa_up', 'w_out', 'g_mlp', 'w_mlp_in', 'w_mlp_out', 'g_final', 'loss_target', 'm_g_mix', 'm_w_in', 'm_sinks', 'm_conv_w', 'm_a_log', 'm_dt_bias', 'm_gdn_norm_w', 'm_g_mem', 'm_w_mem_kv', 'm_w_swa_up', 'm_w_gdn_up', 'm_w_xa_up', 'm_w_out', 'm_g_mlp', 'm_w_mlp_in', 'm_w_mlp_out', 'm_g_final', 'v_g_mix', 'v_w_in', 'v_sinks', 'v_conv_w', 'v_a_log', 'v_dt_bias', 'v_gdn_norm_w', 'v_g_mem', 'v_w_mem_kv', 'v_w_swa_up', 'v_w_gdn_up', 'v_w_xa_up', 'v_w_out', 'v_g_mlp', 'v_w_mlp_in', 'v_w_mlp_out', 'v_g_final']
TWIN_OUTPUTS = ['loss', 'grad_x', 'grad_g_mix', 'grad_w_in', 'grad_sinks', 'grad_conv_w', 'grad_a_log', 'grad_dt_bias', 'grad_gdn_norm_w', 'grad_g_mem', 'grad_w_mem_kv', 'grad_w_swa_up', 'grad_w_gdn_up', 'grad_w_xa_up', 'grad_w_out', 'grad_g_mlp', 'grad_w_mlp_in', 'grad_w_mlp_out', 'grad_g_final', 'delta_g_mix', 'delta_w_in', 'delta_sinks', 'delta_conv_w', 'delta_a_log', 'delta_dt_bias', 'delta_gdn_norm_w', 'delta_g_mem', 'delta_w_mem_kv', 'delta_w_swa_up', 'delta_w_gdn_up', 'delta_w_xa_up', 'delta_w_out', 'delta_g_mlp', 'delta_w_mlp_in', 'delta_w_mlp_out', 'delta_g_final', 'new_m_g_mix', 'new_m_w_in', 'new_m_sinks', 'new_m_conv_w', 'new_m_a_log', 'new_m_dt_bias', 'new_m_gdn_norm_w', 'new_m_g_mem', 'new_m_w_mem_kv', 'new_m_w_swa_up', 'new_m_w_gdn_up', 'new_m_w_xa_up', 'new_m_w_out', 'new_m_g_mlp', 'new_m_w_mlp_in', 'new_m_w_mlp_out', 'new_m_g_final', 'new_v_g_mix', 'new_v_w_in', 'new_v_sinks', 'new_v_conv_w', 'new_v_a_log', 'new_v_dt_bias', 'new_v_gdn_norm_w', 'new_v_g_mem', 'new_v_w_mem_kv', 'new_v_w_swa_up', 'new_v_w_gdn_up', 'new_v_w_xa_up', 'new_v_w_out', 'new_v_g_mlp', 'new_v_w_mlp_in', 'new_v_w_mlp_out', 'new_v_g_final']
TWIN_LEAF_KINDS = {'loss': 'loss', 'grad_x': 'grad_x', 'grad_g_mix': 'grad_w', 'grad_w_in': 'grad_w', 'grad_sinks': 'grad_w', 'grad_conv_w': 'grad_w', 'grad_a_log': 'grad_w', 'grad_dt_bias': 'grad_w', 'grad_gdn_norm_w': 'grad_w', 'grad_g_mem': 'grad_w', 'grad_w_mem_kv': 'grad_w', 'grad_w_swa_up': 'grad_w', 'grad_w_gdn_up': 'grad_w', 'grad_w_xa_up': 'grad_w', 'grad_w_out': 'grad_w', 'grad_g_mlp': 'grad_w', 'grad_w_mlp_in': 'grad_w', 'grad_w_mlp_out': 'grad_w', 'grad_g_final': 'grad_w', 'delta_g_mix': 'delta_w', 'delta_w_in': 'delta_w', 'delta_sinks': 'delta_w', 'delta_conv_w': 'delta_w', 'delta_a_log': 'delta_w', 'delta_dt_bias': 'delta_w', 'delta_gdn_norm_w': 'delta_w', 'delta_g_mem': 'delta_w', 'delta_w_mem_kv': 'delta_w', 'delta_w_swa_up': 'delta_w', 'delta_w_gdn_up': 'delta_w', 'delta_w_xa_up': 'delta_w', 'delta_w_out': 'delta_w', 'delta_g_mlp': 'delta_w', 'delta_w_mlp_in': 'delta_w', 'delta_w_mlp_out': 'delta_w', 'delta_g_final': 'delta_w', 'new_m_g_mix': 'new_m', 'new_m_w_in': 'new_m', 'new_m_sinks': 'new_m', 'new_m_conv_w': 'new_m', 'new_m_a_log': 'new_m', 'new_m_dt_bias': 'new_m', 'new_m_gdn_norm_w': 'new_m', 'new_m_g_mem': 'new_m', 'new_m_w_mem_kv': 'new_m', 'new_m_w_swa_up': 'new_m', 'new_m_w_gdn_up': 'new_m', 'new_m_w_xa_up': 'new_m', 'new_m_w_out': 'new_m', 'new_m_g_mlp': 'new_m', 'new_m_w_mlp_in': 'new_m', 'new_m_w_mlp_out': 'new_m', 'new_m_g_final': 'new_m', 'new_v_g_mix': 'new_v', 'new_v_w_in': 'new_v', 'new_v_sinks': 'new_v', 'new_v_conv_w': 'new_v', 'new_v_a_log': 'new_v', 'new_v_dt_bias': 'new_v', 'new_v_gdn_norm_w': 'new_v', 'new_v_g_mem': 'new_v', 'new_v_w_mem_kv': 'new_v', 'new_v_w_swa_up': 'new_v', 'new_v_w_gdn_up': 'new_v', 'new_v_w_xa_up': 'new_v', 'new_v_w_out': 'new_v', 'new_v_g_mlp': 'new_v', 'new_v_w_mlp_in': 'new_v', 'new_v_w_mlp_out': 'new_v', 'new_v_g_final': 'new_v'}


def _forward(args):
    return _fwd_reference(*[args[k] for k in FWD_PARAMS])


def _output_shape():
    def fwd():
        inp = _fwd_setup_inputs(0)
        return _fwd_reference(*[inp[k] for k in FWD_PARAMS])
    out = _jax.eval_shape(fwd)
    return out.shape, out.dtype

N_MICROBATCH = 1
ADAM_LR = 0.001
ADAM_B1 = 0.9
ADAM_B2 = 0.999
ADAM_EPS = 1e-08
ADAM_WD = 0.01
ADAM_STEP = 10
PER_EXAMPLE_BATCH_AXIS = {'x': 0, 'mem': 0, 'loss_target': 0}
SHARED_INPUTS = []
_WEIGHT_DTYPES = {'g_mix': _jnp.float32, 'w_in': _jnp.float32, 'sinks': _jnp.float32, 'conv_w': _jnp.float32, 'a_log': _jnp.float32, 'dt_bias': _jnp.float32, 'gdn_norm_w': _jnp.float32, 'g_mem': _jnp.float32, 'w_mem_kv': _jnp.float32, 'w_swa_up': _jnp.float32, 'w_gdn_up': _jnp.float32, 'w_xa_up': _jnp.float32, 'w_out': _jnp.float32, 'g_mlp': _jnp.float32, 'w_mlp_in': _jnp.float32, 'w_mlp_out': _jnp.float32, 'g_final': _jnp.float32}
MOMENT_SCALE = {'g_mix': 7.501677e-02, 'w_in': 3.385951e-02, 'sinks': 1.735814e-02, 'conv_w': 7.085770e-02, 'a_log': 7.913221e-01, 'dt_bias': 7.800373e-01, 'gdn_norm_w': 1.785274e-01, 'g_mem': 1.073498e-02, 'w_mem_kv': 1.467182e-02, 'w_swa_up': 1.432564e-02, 'w_gdn_up': 4.266437e-02, 'w_xa_up': 7.389357e-03, 'w_out': 4.415521e-02, 'g_mlp': 1.105810e-01, 'w_mlp_in': 5.488921e-02, 'w_mlp_out': 1.155662e-01, 'g_final': 3.235105e+01}


def _to_microbatches(a, axis):
    t = _jnp.moveaxis(a, axis, 0)
    t = t.reshape((N_MICROBATCH, t.shape[0] // N_MICROBATCH) + t.shape[1:])
    return _jnp.moveaxis(t, 1, axis + 1)


def setup_inputs(seed: int = 0) -> dict:
    inp = _fwd_setup_inputs(seed)
    key = _jax.random.fold_in(_jax.random.key(seed), 7919)
    shape, _ = _output_shape()
    out = dict(inp)
    out["loss_target"] = _jax.random.normal(_jax.random.fold_in(key, 0), shape, _jnp.float32)
    for i, name in enumerate(TWIN_WEIGHTS):
        w = inp[name].astype(_jnp.float32)
        if MOMENT_SCALE is None:
            s = _jnp.sqrt(_jnp.mean(_jnp.square(w)) + 1e-30)
        else:
            s = MOMENT_SCALE[name]
        km, kv = _jax.random.split(_jax.random.fold_in(key, i + 1))
        out[name] = w
        out["m_" + name] = s * _jax.random.normal(km, w.shape, _jnp.float32)
        out["v_" + name] = (s * s) * _jax.random.uniform(kv, w.shape, _jnp.float32, 0.5, 1.5)
    if N_MICROBATCH > 1:
        for name, axis in PER_EXAMPLE_BATCH_AXIS.items():
            out[name] = _to_microbatches(out[name], axis)
    return {'x': out['x'], 'mem': out['mem'], 'g_mix': out['g_mix'], 'w_in': out['w_in'], 'sinks': out['sinks'], 'conv_w': out['conv_w'], 'a_log': out['a_log'], 'dt_bias': out['dt_bias'], 'gdn_norm_w': out['gdn_norm_w'], 'g_mem': out['g_mem'], 'w_mem_kv': out['w_mem_kv'], 'w_swa_up': out['w_swa_up'], 'w_gdn_up': out['w_gdn_up'], 'w_xa_up': out['w_xa_up'], 'w_out': out['w_out'], 'g_mlp': out['g_mlp'], 'w_mlp_in': out['w_mlp_in'], 'w_mlp_out': out['w_mlp_out'], 'g_final': out['g_final'], 'loss_target': out['loss_target'], 'm_g_mix': out['m_g_mix'], 'm_w_in': out['m_w_in'], 'm_sinks': out['m_sinks'], 'm_conv_w': out['m_conv_w'], 'm_a_log': out['m_a_log'], 'm_dt_bias': out['m_dt_bias'], 'm_gdn_norm_w': out['m_gdn_norm_w'], 'm_g_mem': out['m_g_mem'], 'm_w_mem_kv': out['m_w_mem_kv'], 'm_w_swa_up': out['m_w_swa_up'], 'm_w_gdn_up': out['m_w_gdn_up'], 'm_w_xa_up': out['m_w_xa_up'], 'm_w_out': out['m_w_out'], 'm_g_mlp': out['m_g_mlp'], 'm_w_mlp_in': out['m_w_mlp_in'], 'm_w_mlp_out': out['m_w_mlp_out'], 'm_g_final': out['m_g_final'], 'v_g_mix': out['v_g_mix'], 'v_w_in': out['v_w_in'], 'v_sinks': out['v_sinks'], 'v_conv_w': out['v_conv_w'], 'v_a_log': out['v_a_log'], 'v_dt_bias': out['v_dt_bias'], 'v_gdn_norm_w': out['v_gdn_norm_w'], 'v_g_mem': out['v_g_mem'], 'v_w_mem_kv': out['v_w_mem_kv'], 'v_w_swa_up': out['v_w_swa_up'], 'v_w_gdn_up': out['v_w_gdn_up'], 'v_w_xa_up': out['v_w_xa_up'], 'v_w_out': out['v_w_out'], 'v_g_mlp': out['v_g_mlp'], 'v_w_mlp_in': out['v_w_mlp_in'], 'v_w_mlp_out': out['v_w_mlp_out'], 'v_g_final': out['v_g_final']}


def _loss(weights, diff, rest, loss_target):
    with _jax.named_scope("forward"):
        args = {**rest, TWIN_DIFF_INPUT: diff, **{k: w.astype(_WEIGHT_DTYPES[k]) for k, w in weights.items()}}
        y = _forward(args)
    with _jax.named_scope("loss_head"):
        err = _jnp.square(y.astype(_jnp.float32) - loss_target)
        return 0.5 * _jnp.sum(_jnp.mean(err, axis=-1)) if err.ndim else 0.5 * err


def _adamw(w, g, m, v):
    m = ADAM_B1 * m + (1.0 - ADAM_B1) * g
    v = ADAM_B2 * v + (1.0 - ADAM_B2) * _jnp.square(g)
    m_hat = m / (1.0 - ADAM_B1 ** ADAM_STEP)
    v_hat = v / (1.0 - ADAM_B2 ** ADAM_STEP)
    delta = -ADAM_LR * (m_hat / (_jnp.sqrt(v_hat) + ADAM_EPS) + ADAM_WD * w)
    return delta, m, v


def reference(x, mem, g_mix, w_in, sinks, conv_w, a_log, dt_bias, gdn_norm_w, g_mem, w_mem_kv, w_swa_up, w_gdn_up, w_xa_up, w_out, g_mlp, w_mlp_in, w_mlp_out, g_final, loss_target, m_g_mix, m_w_in, m_sinks, m_conv_w, m_a_log, m_dt_bias, m_gdn_norm_w, m_g_mem, m_w_mem_kv, m_w_swa_up, m_w_gdn_up, m_w_xa_up, m_w_out, m_g_mlp, m_w_mlp_in, m_w_mlp_out, m_g_final, v_g_mix, v_w_in, v_sinks, v_conv_w, v_a_log, v_dt_bias, v_gdn_norm_w, v_g_mem, v_w_mem_kv, v_w_swa_up, v_w_gdn_up, v_w_xa_up, v_w_out, v_g_mlp, v_w_mlp_in, v_w_mlp_out, v_g_final):
    given = dict(x=x, mem=mem, g_mix=g_mix, w_in=w_in, sinks=sinks, conv_w=conv_w, a_log=a_log, dt_bias=dt_bias, gdn_norm_w=gdn_norm_w, g_mem=g_mem, w_mem_kv=w_mem_kv, w_swa_up=w_swa_up, w_gdn_up=w_gdn_up, w_xa_up=w_xa_up, w_out=w_out, g_mlp=g_mlp, w_mlp_in=w_mlp_in, w_mlp_out=w_mlp_out, g_final=g_final, loss_target=loss_target, m_g_mix=m_g_mix, m_w_in=m_w_in, m_sinks=m_sinks, m_conv_w=m_conv_w, m_a_log=m_a_log, m_dt_bias=m_dt_bias, m_gdn_norm_w=m_gdn_norm_w, m_g_mem=m_g_mem, m_w_mem_kv=m_w_mem_kv, m_w_swa_up=m_w_swa_up, m_w_gdn_up=m_w_gdn_up, m_w_xa_up=m_w_xa_up, m_w_out=m_w_out, m_g_mlp=m_g_mlp, m_w_mlp_in=m_w_mlp_in, m_w_mlp_out=m_w_mlp_out, m_g_final=m_g_final, v_g_mix=v_g_mix, v_w_in=v_w_in, v_sinks=v_sinks, v_conv_w=v_conv_w, v_a_log=v_a_log, v_dt_bias=v_dt_bias, v_gdn_norm_w=v_gdn_norm_w, v_g_mem=v_g_mem, v_w_mem_kv=v_w_mem_kv, v_w_swa_up=v_w_swa_up, v_w_gdn_up=v_w_gdn_up, v_w_xa_up=v_w_xa_up, v_w_out=v_w_out, v_g_mlp=v_g_mlp, v_w_mlp_in=v_w_mlp_in, v_w_mlp_out=v_w_mlp_out, v_g_final=v_g_final)
    weights = {n: given[n] for n in TWIN_WEIGHTS}
    shared = {n: given[n] for n in SHARED_INPUTS}
    per_example = {n: given[n] for n in ['x', 'mem']}
    grad_fn = _jax.value_and_grad(_loss, argnums=(0, 1))

    def one_microbatch(ex, loss_target):
        ex = dict(ex)
        diff = ex.pop(TWIN_DIFF_INPUT)
        return grad_fn(weights, diff, {**shared, **ex}, loss_target)

    if N_MICROBATCH == 1:
        loss, (grad_w, grad_x) = one_microbatch(per_example, given["loss_target"])
    else:
        def body(carry, xs):
            loss_sum, grad_sum = carry
            l_k, (gw_k, gx_k) = one_microbatch(xs[0], xs[1])
            with _jax.named_scope("update"):
                return (loss_sum + l_k, _jax.tree.map(_jnp.add, grad_sum, gw_k)), gx_k

        init = (_jnp.zeros((), _jnp.float32), _jax.tree.map(_jnp.zeros_like, weights))
        (loss, grad_w), grad_x = _jax.lax.scan(body, init, (per_example, given["loss_target"]))
    with _jax.named_scope("update"):
        delta_w, new_m, new_v = {}, {}, {}
        for n in TWIN_WEIGHTS:
            delta_w[n], new_m[n], new_v[n] = _adamw(weights[n], grad_w[n], given["m_" + n], given["v_" + n])
    return (loss, grad_x, *[grad_w[n] for n in TWIN_WEIGHTS], *[delta_w[n] for n in TWIN_WEIGHTS],
            *[new_m[n] for n in TWIN_WEIGHTS], *[new_v[n] for n in TWIN_WEIGHTS])
```

```python
import functools
import math

import jax
import jax.numpy as jnp
from jax import lax
from jax.experimental import pallas as pl
from jax.experimental.pallas import tpu as pltpu

F32, BF16 = jnp.float32, jnp.bfloat16
SDS = jax.ShapeDtypeStruct
MESH = pl.DeviceIdType.MESH
ANY = pl.BlockSpec(memory_space=pl.ANY)

SWA_HQ, SWA_HKV, SWA_HD, SWA_W = 16, 2, 64, 128
SWA_G = SWA_HQ // SWA_HKV
GDN_H, GDN_D, GDN_CONV, GDN_C = 4, 128, 4, 64
XA_H, XA_D = 4, 128
Q_W = SWA_HQ * SWA_HD
KV_W = SWA_HKV * SWA_HD
GDN_W = GDN_H * GDN_D
XA_W = XA_H * XA_D
RMS_EPS = 1e-6
L2_EPS = 1e-6
NEG = -1e30
N_DEV = 8
LANE = 128

ADAM_LR, ADAM_B1, ADAM_B2, ADAM_EPS, ADAM_WD, ADAM_STEP = 0.001, 0.9, 0.999, 1e-08, 0.01, 10

VMEM_BIG = 56 * 1024 * 1024


def _cp(sem, vmem=VMEM_BIG):
    return pltpu.CompilerParams(dimension_semantics=sem, vmem_limit_bytes=vmem)


def _div(a, b):
    assert a % b == 0, (a, b)
    return a // b


def _tile(n, t):
    t = min(t, n)
    assert n % t == 0, (n, t)
    return t


def _sigmoid(x):
    return 1.0 / (1.0 + jnp.exp(-x))


def _silu(x):
    return x * _sigmoid(x)


def _softplus(x):
    return jnp.maximum(x, 0.0) + jnp.log1p(jnp.exp(-jnp.abs(x)))


def _dot(a, b, dims, prec=None):
    return lax.dot_general(a, b, (dims, ((), ())), precision=prec, preferred_element_type=F32)


NN = ((1,), (0,))
NT = ((1,), (1,))
TN = ((0,), (0,))
HI = lax.Precision.HIGHEST


class Layout:
    def __init__(self, d):
        self.d = d
        self.g = 0
        self.q = 3 * d
        self.qkv = self.q + Q_W
        self.z = self.qkv + 3 * GDN_W
        self.qc = self.z + GDN_W
        self.k = self.qc + XA_W
        self.v = self.k + KV_W
        self.ab = self.v + KV_W
        self.end = self.ab + LANE
        self.pw = -(-self.end // 1024) * 1024
        self.lq, self.lk, self.lv, self.lqkv = 0, Q_W, Q_W + KV_W, Q_W + 2 * KV_W
        self.la = self.lqkv + 3 * GDN_W
        self.lz = self.la + 2 * GDN_H
        self.lqc = self.lz + GDN_W
        self.lg = self.lqc + XA_W
        self.lw = self.lg + 3 * d

    def pad_weight(self, w):
        k = w.shape[0]
        parts = [w[:, self.lg:self.lw], w[:, self.lq:self.lk], w[:, self.lqkv:self.la], w[:, self.lz:self.lqc],
                 w[:, self.lqc:self.lg], w[:, self.lk:self.lv], w[:, self.lv:self.lqkv], w[:, self.la:self.lz],
                 jnp.zeros((k, self.pw - self.ab - 2 * GDN_H), w.dtype)]
        return jnp.concatenate(parts, axis=1)

    def unpad_weight(self, w):
        parts = [w[:, self.q:self.qkv], w[:, self.k:self.v], w[:, self.v:self.ab], w[:, self.qkv:self.z],
                 w[:, self.ab:self.ab + 2 * GDN_H], w[:, self.z:self.qc], w[:, self.qc:self.k], w[:, self.g:self.q]]
        return jnp.concatenate(parts, axis=1)


def _position():
    return lax.axis_index("x"), lax.axis_index("y"), lax.axis_index("c")


def all_gather(arrs, name):
    n = len(arrs)

    def body(*refs):
        ins, outs = refs[:n], refs[n:2 * n]
        send_sems, recv_sems, local_sems = refs[2 * n:]
        x, y, c = _position()
        me, sibling = (x, y, c), (x, y, 1 - c)
        chips = [(1 - x, y), (x, 1 - y), (1 - x, 1 - y)]

        def blk(o, p):
            return o.at[4 * p[0] + 2 * p[1] + p[2]]

        def copy(i, k, block, to, src=None):
            return pltpu.make_async_remote_copy(
                src_ref=blk(outs[i], block) if src is None else src, dst_ref=blk(outs[i], block),
                send_sem=send_sems.at[i, k], recv_sem=recv_sems.at[i, k], device_id=to, device_id_type=MESH)

        started = []
        for i in range(n):
            mine = pltpu.make_async_copy(ins[i], blk(outs[i], me), local_sems.at[i])
            mine.start()
            first = [copy(i, 0, me, sibling, src=ins[i])]
            first += [copy(i, 1 + j, me, (*chip, c), src=ins[i]) for j, chip in enumerate(chips)]
            for cp in first:
                cp.start()
            started += [mine.wait] + [cp.wait_send for cp in first]
        for i in range(n):
            for j, chip in enumerate(chips):
                copy(i, 1 + j, (*chip, c), me).wait_recv()
                fwd = copy(i, 4 + j, (*chip, c), sibling)
                fwd.start()
                started.append(fwd.wait_send)
        for i in range(n):
            copy(i, 0, sibling, me).wait_recv()
            for j, chip in enumerate(chips):
                copy(i, 4 + j, (*chip, 1 - c), me).wait_recv()
        for w in started:
            w()

    return pl.pallas_call(
        body, name=name,
        out_shape=[SDS((N_DEV,) + a.shape, a.dtype) for a in arrs],
        in_specs=[ANY] * n, out_specs=[ANY] * n,
        scratch_shapes=[pltpu.SemaphoreType.DMA((n, 7)), pltpu.SemaphoreType.DMA((n, 7)), pltpu.SemaphoreType.DMA((n,))],
    )(*arrs)


def pair_exchange(arrs, name):
    n = len(arrs)

    def body(*refs):
        ins, outs = refs[:n], refs[n:2 * n]
        send_sems, recv_sems = refs[2 * n:]
        x, y, c = _position()
        copies = [pltpu.make_async_remote_copy(src_ref=ins[i], dst_ref=outs[i], send_sem=send_sems.at[i],
                                               recv_sem=recv_sems.at[i], device_id=(x, y, 1 - c), device_id_type=MESH)
                  for i in range(n)]
        for cp in copies:
            cp.start()
        for cp in copies:
            cp.wait()

    return pl.pallas_call(
        body, name=name, out_shape=[SDS(a.shape, a.dtype) for a in arrs],
        in_specs=[ANY] * n, out_specs=[ANY] * n,
        scratch_shapes=[pltpu.SemaphoreType.DMA((n,)), pltpu.SemaphoreType.DMA((n,))],
    )(*arrs)


def chip_exchange(arrs, name):
    n = len(arrs)

    def body(*refs):
        ins, outs = refs[:n], refs[n:2 * n]
        send_sems, recv_sems, local_sems = refs[2 * n:]
        x, y, c = _position()
        my_chip = 2 * x + y
        chips = [(1 - x, y), (x, 1 - y), (1 - x, 1 - y)]
        waits = []
        for i in range(n):
            mine = pltpu.make_async_copy(ins[i].at[my_chip], outs[i].at[my_chip], local_sems.at[i])
            mine.start()
            waits.append(mine.wait)
            for k, (px, py) in enumerate(chips):
                cp = pltpu.make_async_remote_copy(
                    src_ref=ins[i].at[2 * px + py], dst_ref=outs[i].at[my_chip],
                    send_sem=send_sems.at[i, k], recv_sem=recv_sems.at[i, k], device_id=(px, py, c), device_id_type=MESH)
                cp.start()
                waits.append(cp.wait_send)
        for i in range(n):
            for k, (px, py) in enumerate(chips):
                pltpu.make_async_remote_copy(
                    src_ref=ins[i].at[my_chip], dst_ref=outs[i].at[2 * px + py],
                    send_sem=send_sems.at[i, k], recv_sem=recv_sems.at[i, k], device_id=(px, py, c),
                    device_id_type=MESH).wait_recv()
        for w in waits:
            w()

    return pl.pallas_call(
        body, name=name, out_shape=[SDS(a.shape, a.dtype) for a in arrs],
        in_specs=[ANY] * n, out_specs=[ANY] * n,
        scratch_shapes=[pltpu.SemaphoreType.DMA((n, 3)), pltpu.SemaphoreType.DMA((n, 3)), pltpu.SemaphoreType.DMA((n,))],
    )(*arrs)


def add_bf16(a, b, name):
    p, r, c = a.shape
    tr = _tile(r, 256)

    def body(a_ref, b_ref, o_ref):
        o_ref[...] = (a_ref[...].astype(F32) + b_ref[...].astype(F32)).astype(BF16)

    spec = pl.BlockSpec((1, tr, c), lambda j, i: (j, i, 0))
    return pl.pallas_call(body, name=name, grid=(p, r // tr), in_specs=[spec, spec], out_specs=spec,
                          out_shape=SDS(a.shape, BF16), compiler_params=_cp(("parallel", "parallel")))(a, b)


def adamw(parts, w, m, v, name):
    p, r, c = parts.shape
    tr = _tile(r, 128 if c > 1024 else 256)

    def body(p_ref, w_ref, m_ref, v_ref, g_out, d_out, m_out, v_out):
        g = p_ref[0].astype(F32)
        for j in range(1, p):
            g = g + p_ref[j].astype(F32)
        mn = ADAM_B1 * m_ref[...] + (1.0 - ADAM_B1) * g
        vn = ADAM_B2 * v_ref[...] + (1.0 - ADAM_B2) * jnp.square(g)
        m_hat = mn / (1.0 - ADAM_B1 ** ADAM_STEP)
        v_hat = vn / (1.0 - ADAM_B2 ** ADAM_STEP)
        g_out[...] = g
        d_out[...] = -ADAM_LR * (m_hat / (jnp.sqrt(v_hat) + ADAM_EPS) + ADAM_WD * w_ref[...])
        m_out[...] = mn
        v_out[...] = vn

    spec = pl.BlockSpec((tr, c), lambda i: (i, 0))
    return pl.pallas_call(
        body, name=name, grid=(r // tr,),
        in_specs=[pl.BlockSpec((p, tr, c), lambda i: (0, i, 0)), spec, spec, spec],
        out_specs=[spec] * 4, out_shape=[SDS((r, c), F32)] * 4, compiler_params=_cp(("parallel",)))(parts, w, m, v)


def matmul(a, b, *, mode, out_dtype, name, tm=1024, tn=1024, tk=512, a_relu2=False, resid=None, relu2_grad_of=None):
    if mode == "nn":
        (m, k), (k2, n) = a.shape, b.shape
    elif mode == "nt":
        (m, k), (n, k2) = a.shape, b.shape
    else:
        (k, m), (k2, n) = a.shape, b.shape
    assert k == k2, (a.shape, b.shape, mode)
    tm, tn, tk = _tile(m, tm), _tile(n, tn), _tile(k, tk)
    nk = k // tk
    dims = {"nn": NN, "nt": NT, "tn": TN}[mode]
    extras = [e for e in (resid, relu2_grad_of) if e is not None]

    def body(*refs):
        a_ref, b_ref = refs[:2]
        e_refs = refs[2:2 + len(extras)]
        o_ref = refs[2 + len(extras)]
        acc_ref = refs[-1] if nk > 1 else None

        def operands():
            av = a_ref[...]
            if a_relu2:
                av = jnp.square(jnp.maximum(av.astype(F32), 0.0))
            return av.astype(BF16), b_ref[...].astype(BF16)

        def finish(r):
            e = list(e_refs)
            if resid is not None:
                r = r + e.pop(0)[...]
            if relu2_grad_of is not None:
                r = r * (2.0 * jnp.maximum(e.pop(0)[...], 0.0))
            o_ref[...] = r.astype(out_dtype)

        if nk == 1:
            av, bv = operands()
            finish(_dot(av, bv, dims))
        else:
            kk = pl.program_id(2)

            @pl.when(kk == 0)
            def _():
                acc_ref[...] = jnp.zeros_like(acc_ref)

            av, bv = operands()
            acc_ref[...] += _dot(av, bv, dims)

            @pl.when(kk == nk - 1)
            def _():
                finish(acc_ref[...])

    a_spec = {"nn": pl.BlockSpec((tm, tk), lambda i, j, kk: (i, kk)),
              "nt": pl.BlockSpec((tm, tk), lambda i, j, kk: (i, kk)),
              "tn": pl.BlockSpec((tk, tm), lambda i, j, kk: (kk, i))}[mode]
    b_spec = {"nn": pl.BlockSpec((tk, tn), lambda i, j, kk: (kk, j)),
              "nt": pl.BlockSpec((tn, tk), lambda i, j, kk: (j, kk)),
              "tn": pl.BlockSpec((tk, tn), lambda i, j, kk: (kk, j))}[mode]
    o_spec = pl.BlockSpec((tm, tn), lambda i, j, kk: (i, j))
    return pl.pallas_call(
        body, name=name, grid=(m // tm, n // tn, nk),
        in_specs=[a_spec, b_spec] + [o_spec] * len(extras), out_specs=o_spec, out_shape=SDS((m, n), out_dtype),
        scratch_shapes=[pltpu.VMEM((tm, tn), F32)] if nk > 1 else [],
        compiler_params=_cp(("parallel", "parallel", "arbitrary")))(a, b, *extras)


def rmsnorm_fwd(x, g, name):
    s, d = x.shape
    tm = _tile(s, 256)

    def body(x_ref, g_ref, o_ref):
        xv = x_ref[...]
        r = lax.rsqrt(jnp.mean(xv * xv, axis=-1, keepdims=True) + RMS_EPS)
        o_ref[...] = (xv * r * g_ref[...]).astype(BF16)

    row = pl.BlockSpec((tm, d), lambda i: (i, 0))
    return pl.pallas_call(body, name=name, grid=(s // tm,), in_specs=[row, pl.BlockSpec((1, d), lambda i: (0, 0))],
                          out_specs=row, out_shape=SDS((s, d), BF16), compiler_params=_cp(("parallel",)))(x, g)


def _rms_bwd_rows(xv, gv, dy):
    r = lax.rsqrt(jnp.mean(xv * xv, axis=-1, keepdims=True) + RMS_EPS)
    xh = xv * r
    dxh = dy * gv
    dx = r * (dxh - xh * jnp.mean(dxh * xh, axis=-1, keepdims=True))
    return dx, jnp.sum(dy * xh, axis=0, keepdims=True)


def rmsnorm_bwd(x, g, dn, resid, name):
    s, d = x.shape
    tm = _tile(s, 256)
    has_r = resid is not None

    def body(*refs):
        x_ref, g_ref, dn_ref = refs[:3]
        dx_ref, dg_ref = refs[-2:]
        dx, part = _rms_bwd_rows(x_ref[...], g_ref[...], dn_ref[...].astype(F32))
        if has_r:
            dx = dx + refs[3][...]
        dx_ref[...] = dx

        @pl.when(pl.program_id(0) == 0)
        def _():
            dg_ref[...] = jnp.zeros_like(dg_ref)

        dg_ref[...] += part

    row = pl.BlockSpec((tm, d), lambda i: (i, 0))
    vec = pl.BlockSpec((1, d), lambda i: (0, 0))
    ins = [x, g, dn] + ([resid] if has_r else [])
    return pl.pallas_call(body, name=name, grid=(s // tm,), in_specs=[row, vec, row] + ([row] if has_r else []),
                          out_specs=[row, vec], out_shape=[SDS((s, d), F32), SDS((1, d), F32)],
                          compiler_params=_cp(("arbitrary",)))(*ins)


def final_norm_loss(h, g, tgt, name):
    s, d = h.shape
    tm = _tile(s, 256)

    def body(h_ref, g_ref, t_ref, dh_ref, dg_ref, l_ref):
        xv, gv = h_ref[...], g_ref[...]
        r = lax.rsqrt(jnp.mean(xv * xv, axis=-1, keepdims=True) + RMS_EPS)
        e = xv * r * gv - t_ref[...]
        lpart = 0.5 * jnp.sum(jnp.mean(e * e, axis=-1, keepdims=True), axis=0, keepdims=True)
        dx, part = _rms_bwd_rows(xv, gv, e * (1.0 / d))
        dh_ref[...] = dx

        @pl.when(pl.program_id(0) == 0)
        def _():
            dg_ref[...] = jnp.zeros_like(dg_ref)
            l_ref[...] = jnp.zeros_like(l_ref)

        dg_ref[...] += part
        l_ref[...] += jnp.broadcast_to(lpart, l_ref.shape)

    row = pl.BlockSpec((tm, d), lambda i: (i, 0))
    vec = pl.BlockSpec((1, d), lambda i: (0, 0))
    lsp = pl.BlockSpec((1, LANE), lambda i: (0, 0))
    return pl.pallas_call(body, name=name, grid=(s // tm,), in_specs=[row, vec, row], out_specs=[row, vec, lsp],
                          out_shape=[SDS((s, d), F32), SDS((1, d), F32), SDS((1, LANE), F32)],
                          compiler_params=_cp(("arbitrary",)))(h, g, tgt)


def merge(p_all, ya, yb, yc, wa, wb, wc, dm, lay, name):
    s, d = ya.shape[0], lay.d
    tm, tn = _tile(s, 512), _tile(d, 512)
    nj = d // tn
    bwd = dm is not None

    def body(*refs):
        ga, gb, gc, ya_r, yb_r, yc_r, wa_r, wb_r, wc_r = refs[:9]
        ts = [_dot(y[...], w[...], NN) for y, w in ((ya_r, wa_r), (yb_r, wb_r), (yc_r, wc_r))]
        gs = [_sigmoid(g[...]) for g in (ga, gb, gc)]
        if not bwd:
            refs[9][...] = (gs[0] * ts[0] + gs[1] * ts[1] + gs[2] * ts[2]).astype(BF16)
        else:
            dmv = refs[9][...]
            for i in range(3):
                refs[10 + i][...] = (dmv * ts[i] * (gs[i] * (1.0 - gs[i]))).astype(BF16)
                refs[13 + i][...] = (dmv * gs[i]).astype(BF16)

    gate_specs = [pl.BlockSpec((tm, tn), lambda i, j, b=b: (i, b * nj + j)) for b in range(3)]
    y_specs = [pl.BlockSpec((tm, y.shape[1]), lambda i, j: (i, 0)) for y in (ya, yb, yc)]
    w_specs = [pl.BlockSpec((w.shape[0], tn), lambda i, j: (0, j)) for w in (wa, wb, wc)]
    o_spec = pl.BlockSpec((tm, tn), lambda i, j: (i, j))
    n_out = 6 if bwd else 1
    out = pl.pallas_call(
        body, name=name, grid=(s // tm, nj),
        in_specs=gate_specs + y_specs + w_specs + ([o_spec] if bwd else []),
        out_specs=[o_spec] * n_out, out_shape=[SDS((s, d), BF16)] * n_out,
        compiler_params=_cp(("parallel", "parallel")))(p_all, p_all, p_all, ya, yb, yc, wa, wb, wc, *([dm] if bwd else []))
    return out if bwd else out[0]


def _swa_probs(q, kc, sink, first):
    s = _dot(q, kc, NT) * (SWA_HD ** -0.5)
    qi = lax.broadcasted_iota(jnp.int32, s.shape, 0)
    kj = lax.broadcasted_iota(jnp.int32, s.shape, 1)
    mask = (kj > qi) & (kj <= qi + SWA_W) & ((kj >= SWA_W) | jnp.logical_not(first))
    s = jnp.where(mask, s, NEG)
    m = jnp.maximum(jnp.max(s, axis=-1, keepdims=True), sink)
    p = jnp.exp(s - m)
    es = jnp.exp(sink - m)
    denom = jnp.sum(p, axis=-1, keepdims=True) + es
    return p / denom, es / denom


def _swa_specs(lay):
    w = SWA_W
    q_spec = pl.BlockSpec((w, Q_W), lambda n: (n, _div(lay.q, Q_W)))
    cur = lambda off: pl.BlockSpec((w, KV_W), lambda n: (n, _div(off, KV_W)))
    prev = lambda off: pl.BlockSpec((w, KV_W), lambda n: (jnp.maximum(n - 1, 0), _div(off, KV_W)))
    return q_spec, cur(lay.k), prev(lay.k), cur(lay.v), prev(lay.v)


def swa_fwd(p_all, sinks, lay, name):
    s = p_all.shape[0]
    nb = _div(s, SWA_W)

    def body(q_ref, kc_ref, kp_ref, vc_ref, vp_ref, sk_ref, o_ref):
        first = pl.program_id(0) == 0
        for h in range(SWA_HKV):
            hs = pl.ds(h * SWA_HD, SWA_HD)
            kc = jnp.concatenate([kp_ref[:, hs], kc_ref[:, hs]], axis=0).astype(BF16)
            vc = jnp.concatenate([vp_ref[:, hs], vc_ref[:, hs]], axis=0).astype(BF16)
            for g in range(SWA_G):
                hh = h * SWA_G + g
                cols = pl.ds(hh * SWA_HD, SWA_HD)
                p, _ = _swa_probs(q_ref[:, cols].astype(BF16), kc, sk_ref[0:1, hh:hh + 1], first)
                o_ref[:, cols] = _dot(p.astype(BF16), vc, NN).astype(BF16)

    q_spec, kc_s, kp_s, vc_s, vp_s = _swa_specs(lay)
    return pl.pallas_call(
        body, name=name, grid=(nb,),
        in_specs=[q_spec, kc_s, kp_s, vc_s, vp_s, pl.BlockSpec(sinks.shape, lambda n: (0, 0))],
        out_specs=pl.BlockSpec((SWA_W, Q_W), lambda n: (n, 0)), out_shape=SDS((s, Q_W), BF16),
        compiler_params=_cp(("parallel",)))(p_all, p_all, p_all, p_all, p_all, sinks)


def swa_bwd(p_all, sinks, dy, lay, name):
    s = p_all.shape[0]
    nb = _div(s, SWA_W)
    w = SWA_W

    def body(q_ref, kc_ref, kp_ref, vc_ref, vp_ref, sk_ref, do_ref, dq_ref, dk_ref, dv_ref, ds_ref, kcar, vcar):
        n = pl.program_id(0)
        first = n == 0

        @pl.when(first)
        def _():
            kcar[...] = jnp.zeros_like(kcar)
            vcar[...] = jnp.zeros_like(vcar)
            ds_ref[...] = jnp.zeros_like(ds_ref)

        @pl.when(n < nb)
        def _():
            lane = lax.broadcasted_iota(jnp.int32, (1, LANE), 1)
            dsink = jnp.zeros((1, LANE), F32)
            for h in range(SWA_HKV):
                hs = pl.ds(h * SWA_HD, SWA_HD)
                kc = jnp.concatenate([kp_ref[:, hs], kc_ref[:, hs]], axis=0).astype(BF16)
                vc = jnp.concatenate([vp_ref[:, hs], vc_ref[:, hs]], axis=0).astype(BF16)
                dkc = jnp.zeros((2 * w, SWA_HD), F32)
                dvc = jnp.zeros((2 * w, SWA_HD), F32)
                for g in range(SWA_G):
                    hh = h * SWA_G + g
                    cols = pl.ds(hh * SWA_HD, SWA_HD)
                    q = q_ref[:, cols].astype(BF16)
                    do = do_ref[:, cols].astype(BF16)
                    p, ps = _swa_probs(q, kc, sk_ref[0:1, hh:hh + 1], first)
                    dp = _dot(do, vc, NT)
                    dvc = dvc + _dot(p.astype(BF16), do, TN)
                    rs = jnp.sum(dp * p, axis=-1, keepdims=True)
                    dsb = (p * (dp - rs) * (SWA_HD ** -0.5)).astype(BF16)
                    dq_ref[:, cols] = _dot(dsb, kc, NN).astype(BF16)
                    dkc = dkc + _dot(dsb, q, TN)
                    dsink = dsink + jnp.where(lane == hh, -jnp.sum(ps * rs, axis=0, keepdims=True), 0.0)
                dk_ref[:, hs] = (kcar[:, hs] + dkc[:w]).astype(BF16)
                dv_ref[:, hs] = (vcar[:, hs] + dvc[:w]).astype(BF16)
                kcar[:, hs] = dkc[w:]
                vcar[:, hs] = dvc[w:]
            ds_ref[...] += dsink

        @pl.when(n == nb)
        def _():
            dk_ref[...] = kcar[...].astype(BF16)
            dv_ref[...] = vcar[...].astype(BF16)

    last = nb - 1
    q_spec = pl.BlockSpec((w, Q_W), lambda n: (jnp.minimum(n, last), _div(lay.q, Q_W)))
    cur = lambda off: pl.BlockSpec((w, KV_W), lambda n: (jnp.minimum(n, last), _div(off, KV_W)))
    prev = lambda off: pl.BlockSpec((w, KV_W), lambda n: (jnp.clip(n - 1, 0, last), _div(off, KV_W)))
    row = pl.BlockSpec((w, Q_W), lambda n: (jnp.minimum(n, last), 0))
    kv_out = pl.BlockSpec((w, KV_W), lambda n: (jnp.maximum(n - 1, 0), 0))
    return pl.pallas_call(
        body, name=name, grid=(nb + 1,),
        in_specs=[q_spec, cur(lay.k), prev(lay.k), cur(lay.v), prev(lay.v), pl.BlockSpec(sinks.shape, lambda n: (0, 0)), row],
        out_specs=[row, kv_out, kv_out, pl.BlockSpec((1, LANE), lambda n: (0, 0))],
        out_shape=[SDS((s, Q_W), BF16), SDS((s, KV_W), BF16), SDS((s, KV_W), BF16), SDS((1, LANE), F32)],
        scratch_shapes=[pltpu.VMEM((w, KV_W), F32), pltpu.VMEM((w, KV_W), F32)],
        compiler_params=_cp(("arbitrary",)))(p_all, p_all, p_all, p_all, p_all, sinks, dy)


def _xa_probs(q, mk):
    s = _dot(q, mk, NT) * (XA_D ** -0.5)
    p = jnp.exp(s - jnp.max(s, axis=-1, keepdims=True))
    return p / jnp.sum(p, axis=-1, keepdims=True)


def xattn_fwd(p_all, mkv, lay, name):
    s, nm = p_all.shape[0], mkv.shape[0]
    tm = _tile(s, 512)

    def body(q_ref, mkv_ref, o_ref):
        for h in range(XA_H):
            cols = pl.ds(h * XA_D, XA_D)
            p = _xa_probs(q_ref[:, cols].astype(BF16), mkv_ref[:, cols])
            o_ref[:, cols] = _dot(p.astype(BF16), mkv_ref[:, pl.ds(XA_W + h * XA_D, XA_D)], NN).astype(BF16)

    return pl.pallas_call(
        body, name=name, grid=(s // tm,),
        in_specs=[pl.BlockSpec((tm, XA_W), lambda i: (i, _div(lay.qc, XA_W))), pl.BlockSpec((nm, 2 * XA_W), lambda i: (0, 0))],
        out_specs=pl.BlockSpec((tm, XA_W), lambda i: (i, 0)), out_shape=SDS((s, XA_W), BF16),
        compiler_params=_cp(("parallel",)))(p_all, mkv)


def xattn_bwd(p_all, mkv, dy, lay, name):
    s, nm = p_all.shape[0], mkv.shape[0]
    tm = _tile(s, 512)

    def body(q_ref, mkv_ref, do_ref, dq_ref, dmkv_ref):
        @pl.when(pl.program_id(0) == 0)
        def _():
            dmkv_ref[...] = jnp.zeros_like(dmkv_ref)

        for h in range(XA_H):
            cols = pl.ds(h * XA_D, XA_D)
            vcols = pl.ds(XA_W + h * XA_D, XA_D)
            q = q_ref[:, cols].astype(BF16)
            do = do_ref[:, cols].astype(BF16)
            p = _xa_probs(q, mkv_ref[:, cols])
            dp = _dot(do, mkv_ref[:, vcols], NT)
            dmkv_ref[:, vcols] += _dot(p.astype(BF16), do, TN)
            dsb = (p * (dp - jnp.sum(dp * p, axis=-1, keepdims=True)) * (XA_D ** -0.5)).astype(BF16)
            dq_ref[:, cols] = _dot(dsb, mkv_ref[:, cols], NN).astype(BF16)
            dmkv_ref[:, cols] += _dot(dsb, q, TN)

    row = pl.BlockSpec((tm, XA_W), lambda i: (i, 0))
    full = pl.BlockSpec((nm, 2 * XA_W), lambda i: (0, 0))
    return pl.pallas_call(
        body, name=name, grid=(s // tm,),
        in_specs=[pl.BlockSpec((tm, XA_W), lambda i: (i, _div(lay.qc, XA_W))), full, row],
        out_specs=[row, full], out_shape=[SDS((s, XA_W), BF16), SDS((nm, 2 * XA_W), F32)],
        compiler_params=_cp(("arbitrary",)))(p_all, mkv, dy)


def _shift_down(cur, prev8, s):
    cat = jnp.concatenate([prev8, cur[0:8]], axis=0)
    return pltpu.roll(cur, s, axis=0), pltpu.roll(cat, s, axis=0)[8:16]


def _shift_up(cur, next8, s):
    tm = cur.shape[0]
    cat = jnp.concatenate([cur[tm - 8:tm], next8], axis=0)
    return pltpu.roll(cur, tm - s, axis=0), pltpu.roll(cat, 16 - s, axis=0)[0:8]


def gdn_conv_fwd(p_all, conv_w, lay, name):
    s = p_all.shape[0]
    tm = _tile(s, 512)
    c0 = _div(lay.qkv, GDN_W)

    def body(x_ref, prev_ref, w_ref, o_ref):
        cur = x_ref[...]
        prev8 = jnp.where(pl.program_id(1) > 0, prev_ref[...], 0.0)
        main = w_ref[GDN_CONV - 1:GDN_CONV, :] * cur
        top = w_ref[GDN_CONV - 1:GDN_CONV, :] * cur[0:8]
        for sft in range(1, GDN_CONV):
            wi = w_ref[GDN_CONV - 1 - sft:GDN_CONV - sft, :]
            a, b = _shift_down(cur, prev8, sft)
            main = main + wi * a
            top = top + wi * b
        o_ref[...] = main
        o_ref[0:8, :] = top

    return pl.pallas_call(
        body, name=name, grid=(3, s // tm),
        in_specs=[pl.BlockSpec((tm, GDN_W), lambda c, i: (i, c0 + c)),
                  pl.BlockSpec((8, GDN_W), lambda c, i: (jnp.maximum(i * (tm // 8) - 1, 0), c0 + c)),
                  pl.BlockSpec((GDN_CONV, GDN_W), lambda c, i: (0, c))],
        out_specs=pl.BlockSpec((tm, GDN_W), lambda c, i: (i, c)), out_shape=SDS((s, 3 * GDN_W), F32),
        compiler_params=_cp(("parallel", "parallel")))(p_all, p_all, conv_w)


def gdn_conv_bwd(p_all, conv_w, dxc, lay, name):
    s = p_all.shape[0]
    tm = _tile(s, 512)
    c0 = _div(lay.qkv, GDN_W)
    nt = s // tm

    def body(x_ref, prev_ref, d_ref, next_ref, w_ref, dx_ref, dw_ref):
        i = pl.program_id(1)
        cur, d = x_ref[...], d_ref[...]
        prev8 = jnp.where(i > 0, prev_ref[...], 0.0)
        next8 = jnp.where(i < nt - 1, next_ref[...], 0.0)
        row = lax.broadcasted_iota(jnp.int32, (tm, 1), 0)
        main = w_ref[GDN_CONV - 1:GDN_CONV, :] * d
        bot = w_ref[GDN_CONV - 1:GDN_CONV, :] * d[tm - 8:tm]
        dws = [jnp.sum(d * cur, axis=0, keepdims=True)]
        for sft in range(1, GDN_CONV):
            wi = w_ref[GDN_CONV - 1 - sft:GDN_CONV - sft, :]
            a, b = _shift_up(d, next8, sft)
            main = main + wi * a
            bot = bot + wi * b
            xa, xb = _shift_down(cur, prev8, sft)
            dws.append(jnp.sum(jnp.where(row >= 8, d * xa, 0.0), axis=0, keepdims=True)
                       + jnp.sum(d[0:8] * xb, axis=0, keepdims=True))
        dx_ref[...] = main.astype(BF16)
        dx_ref[tm - 8:tm, :] = bot.astype(BF16)

        @pl.when(i == 0)
        def _():
            dw_ref[...] = jnp.zeros_like(dw_ref)

        for sft in range(GDN_CONV):
            dw_ref[GDN_CONV - 1 - sft:GDN_CONV - sft, :] += dws[sft]

    return pl.pallas_call(
        body, name=name, grid=(3, nt),
        in_specs=[pl.BlockSpec((tm, GDN_W), lambda c, i: (i, c0 + c)),
                  pl.BlockSpec((8, GDN_W), lambda c, i: (jnp.maximum(i * (tm // 8) - 1, 0), c0 + c)),
                  pl.BlockSpec((tm, GDN_W), lambda c, i: (i, c)),
                  pl.BlockSpec((8, GDN_W), lambda c, i: (jnp.minimum((i + 1) * (tm // 8), s // 8 - 1), c)),
                  pl.BlockSpec((GDN_CONV, GDN_W), lambda c, i: (0, c))],
        out_specs=[pl.BlockSpec((tm, GDN_W), lambda c, i: (i, c)), pl.BlockSpec((GDN_CONV, GDN_W), lambda c, i: (0, c))],
        out_shape=[SDS((s, 3 * GDN_W), BF16), SDS((GDN_CONV, 3 * GDN_W), F32)],
        compiler_params=_cp(("parallel", "arbitrary")))(p_all, p_all, dxc, dxc, conv_w)


def _gdn_chunk(xq, xk, xv, ab, gp):
    c = GDN_C
    lane = lax.broadcasted_iota(jnp.int32, (c, LANE), 1)
    row = lax.broadcasted_iota(jnp.int32, (c, c), 0)
    col = lax.broadcasted_iota(jnp.int32, (c, c), 1)
    g_tile = -jnp.exp(gp[0:1, :]) * _softplus(ab + gp[1:2, :])
    b_tile = _sigmoid(ab)
    tri = (row >= col).astype(F32)
    gcum = _dot(tri, g_tile, NN, HI)
    gcum_t = gcum.T
    eye = (row == col).astype(F32)
    out = []
    for h in range(GDN_H):
        hs = slice(h * GDN_D, (h + 1) * GDN_D)
        q, k, v = _silu(xq[:, hs]), _silu(xk[:, hs]), _silu(xv[:, hs])
        q = q * lax.rsqrt(jnp.sum(q * q, axis=-1, keepdims=True) + L2_EPS) * (GDN_D ** -0.5)
        k = k * lax.rsqrt(jnp.sum(k * k, axis=-1, keepdims=True) + L2_EPS)
        gc = jnp.sum(jnp.where(lane == h, gcum, 0.0), axis=1, keepdims=True)
        gr = gcum_t[h:h + 1, :]
        beta = jnp.sum(jnp.where(lane == GDN_H + h, b_tile, 0.0), axis=1, keepdims=True)
        decay = jnp.exp(jnp.where(row >= col, gc - gr, NEG))
        kb = k.astype(BF16)
        kk = _dot(kb, kb, NT)
        x = -jnp.where(row > col, beta * kk * decay, 0.0)
        ainv = eye + x
        pw = x
        for _ in range(5):
            pw = _dot(pw, pw, NN, HI)
            ainv = ainv + _dot(ainv, pw, NN, HI)
        eg = jnp.exp(gc)
        u = _dot(ainv, v * beta, NN, HI)
        w = _dot(ainv, k * (beta * eg), NN, HI)
        qk = _dot(q.astype(BF16), kb, NT) * decay
        gl = gc[c - 1:c, :]
        out.append((u, w, q * eg, k * jnp.exp(gl - gc), qk, jnp.exp(gl)))
    return out


def _gdn_pre_specs(lay):
    c = GDN_C
    xspec = lambda j: pl.BlockSpec((c, GDN_W), lambda n, j=j: (n, j))
    return [xspec(0), xspec(1), xspec(2), pl.BlockSpec((c, LANE), lambda n: (n, _div(lay.ab, LANE))),
            pl.BlockSpec((8, LANE), lambda n: (0, 0))]


def gdn_pre_fwd(xc, p_all, gp, lay, name):
    s = xc.shape[0]
    n = _div(s, GDN_C)
    c = GDN_C

    def body(xq, xk, xv, ab, gp_ref, u_ref, w_ref, qd_ref, kd_ref, qk_ref, gl_ref):
        heads = _gdn_chunk(xq[...], xk[...], xv[...], ab[...], gp_ref[...])
        lane = lax.broadcasted_iota(jnp.int32, (1, LANE), 1)
        gl_row = jnp.zeros((1, LANE), F32)
        for h, (u, w, qd, kd, qk, gl) in enumerate(heads):
            hs = pl.ds(h * GDN_D, GDN_D)
            u_ref[:, hs] = u
            w_ref[:, hs] = w.astype(BF16)
            qd_ref[:, hs] = qd.astype(BF16)
            kd_ref[:, hs] = kd.astype(BF16)
            qk_ref[:, pl.ds(h * c, c)] = qk.astype(BF16)
            gl_row = gl_row + jnp.where(lane == h, gl, 0.0)
        gl_ref[0] = gl_row

    row = pl.BlockSpec((c, GDN_W), lambda n: (n, 0))
    return pl.pallas_call(
        body, name=name, grid=(n,), in_specs=_gdn_pre_specs(lay),
        out_specs=[row, row, row, row, pl.BlockSpec((c, GDN_H * c), lambda n: (n, 0)), pl.BlockSpec((1, 1, LANE), lambda n: (n, 0, 0))],
        out_shape=[SDS((s, GDN_W), F32), SDS((s, GDN_W), BF16), SDS((s, GDN_W), BF16), SDS((s, GDN_W), BF16),
                   SDS((s, GDN_H * c), BF16), SDS((n, 1, LANE), F32)],
        compiler_params=_cp(("parallel",)))(xc, xc, xc, p_all, gp)


def gdn_pre_bwd(xc, p_all, gp, du, dw, dqd, dkd, dqk, dgl, lay, name):
    s = xc.shape[0]
    n = _div(s, GDN_C)
    c = GDN_C

    def body(xq, xk, xv, ab, gp_ref, du_r, dw_r, dqd_r, dkd_r, dqk_r, dgl_r, dxc_ref, dab_ref, dgp_ref):
        _, vjp = jax.vjp(_gdn_chunk, xq[...], xk[...], xv[...], ab[...], gp_ref[...])
        lane = lax.broadcasted_iota(jnp.int32, (1, LANE), 1)
        cts = []
        for h in range(GDN_H):
            hs = pl.ds(h * GDN_D, GDN_D)
            dgl_h = jnp.sum(jnp.where(lane == h, dgl_r[0], 0.0), axis=1, keepdims=True)
            cts.append((du_r[:, hs], dw_r[:, hs], dqd_r[:, hs], dkd_r[:, hs], dqk_r[:, pl.ds(h * c, c)], dgl_h))
        dq, dk, dv, dab, dgp = vjp(cts)
        dxc_ref[:, pl.ds(0, GDN_W)] = dq
        dxc_ref[:, pl.ds(GDN_W, GDN_W)] = dk
        dxc_ref[:, pl.ds(2 * GDN_W, GDN_W)] = dv
        dab_ref[...] = dab.astype(BF16)

        @pl.when(pl.program_id(0) == 0)
        def _():
            dgp_ref[...] = jnp.zeros_like(dgp_ref)

        dgp_ref[...] += dgp

    row = pl.BlockSpec((c, GDN_W), lambda n: (n, 0))
    return pl.pallas_call(
        body, name=name, grid=(n,),
        in_specs=_gdn_pre_specs(lay) + [row, row, row, row, pl.BlockSpec((c, GDN_H * c), lambda n: (n, 0)),
                                        pl.BlockSpec((1, 1, LANE), lambda n: (n, 0, 0))],
        out_specs=[pl.BlockSpec((c, 3 * GDN_W), lambda n: (n, 0)), pl.BlockSpec((c, LANE), lambda n: (n, 0)),
                   pl.BlockSpec((8, LANE), lambda n: (0, 0))],
        out_shape=[SDS((s, 3 * GDN_W), F32), SDS((s, LANE), BF16), SDS((8, LANE), F32)],
        compiler_params=_cp(("arbitrary",)))(xc, xc, xc, p_all, gp, du, dw, dqd, dkd, dqk, dgl)


def _lane_scalar(row, h):
    lane = lax.broadcasted_iota(jnp.int32, row.shape, 1)
    return jnp.sum(jnp.where(lane == h, row, 0.0), axis=1, keepdims=True)


def gdn_scan_fwd(u, w, qd, kd, qk, gl, name):
    s = u.shape[0]
    c = GDN_C
    n = _div(s, c)

    def body(u_r, w_r, qd_r, kd_r, qk_r, gl_r, o_ref, s_ref, st):
        @pl.when(pl.program_id(0) == 0)
        def _():
            st[...] = jnp.zeros_like(st)

        s_ref[0] = st[...]
        for h in range(GDN_H):
            hs = pl.ds(h * GDN_D, GDN_D)
            sh = st[hs, :]
            shb = sh.astype(BF16)
            v_new = u_r[:, hs] - _dot(w_r[:, hs], shb, NN)
            vb = v_new.astype(BF16)
            o_ref[:, hs] = _dot(qd_r[:, hs], shb, NN) + _dot(qk_r[:, pl.ds(h * c, c)], vb, NN)
            st[hs, :] = sh * _lane_scalar(gl_r[0], h) + _dot(kd_r[:, hs], vb, TN)

    row = pl.BlockSpec((c, GDN_W), lambda i: (i, 0))
    return pl.pallas_call(
        body, name=name, grid=(n,),
        in_specs=[row, row, row, row, pl.BlockSpec((c, GDN_H * c), lambda i: (i, 0)), pl.BlockSpec((1, 1, LANE), lambda i: (i, 0, 0))],
        out_specs=[row, pl.BlockSpec((1, GDN_W, GDN_D), lambda i: (i, 0, 0))],
        out_shape=[SDS((s, GDN_W), F32), SDS((n, GDN_W, GDN_D), F32)],
        scratch_shapes=[pltpu.VMEM((GDN_W, GDN_D), F32)],
        compiler_params=_cp(("arbitrary",)))(u, w, qd, kd, qk, gl)


def gdn_scan_bwd(u, w, qd, kd, qk, gl, states, do, name):
    s = u.shape[0]
    c = GDN_C
    n = _div(s, c)

    def body(u_r, w_r, qd_r, kd_r, qk_r, gl_r, s_r, do_r, du_o, dw_o, dqd_o, dkd_o, dqk_o, dgl_o, dst):
        @pl.when(pl.program_id(0) == 0)
        def _():
            dst[...] = jnp.zeros_like(dst)

        lane = lax.broadcasted_iota(jnp.int32, (1, LANE), 1)
        dgl_row = jnp.zeros((1, LANE), F32)
        for h in range(GDN_H):
            hs = pl.ds(h * GDN_D, GDN_D)
            qs = pl.ds(h * c, c)
            sh = s_r[0, hs, :]
            shb = sh.astype(BF16)
            ds_out = dst[hs, :]
            dsb = ds_out.astype(BF16)
            dob = do_r[:, hs].astype(BF16)
            wv, qdv, kdv, qkv = w_r[:, hs], qd_r[:, hs], kd_r[:, hs], qk_r[:, qs]
            v_new = u_r[:, hs] - _dot(wv, shb, NN)
            vb = v_new.astype(BF16)
            dv = _dot(qkv, dob, TN) + _dot(kdv, dsb, NN)
            dvb = dv.astype(BF16)
            du_o[:, hs] = dv
            dw_o[:, hs] = -_dot(dvb, shb, NT)
            dqd_o[:, hs] = _dot(dob, shb, NT)
            dkd_o[:, hs] = _dot(vb, dsb, NT)
            dqk_o[:, qs] = _dot(dob, vb, NT)
            dgl_row = dgl_row + jnp.where(lane == h, jnp.sum(jnp.sum(ds_out * sh, axis=1, keepdims=True), axis=0, keepdims=True), 0.0)
            dst[hs, :] = ds_out * _lane_scalar(gl_r[0], h) + _dot(qdv, dob, TN) - _dot(wv, dvb, TN)
        dgl_o[0] = dgl_row

    rev = lambda i: n - 1 - i
    row = pl.BlockSpec((c, GDN_W), lambda i: (rev(i), 0))
    qks = pl.BlockSpec((c, GDN_H * c), lambda i: (rev(i), 0))
    gls = pl.BlockSpec((1, 1, LANE), lambda i: (rev(i), 0, 0))
    return pl.pallas_call(
        body, name=name, grid=(n,),
        in_specs=[row, row, row, row, qks, gls, pl.BlockSpec((1, GDN_W, GDN_D), lambda i: (rev(i), 0, 0)), row],
        out_specs=[row, row, row, row, qks, gls],
        out_shape=[SDS((s, GDN_W), F32)] * 4 + [SDS((s, GDN_H * c), F32), SDS((n, 1, LANE), F32)],
        scratch_shapes=[pltpu.VMEM((GDN_W, GDN_D), F32)],
        compiler_params=_cp(("arbitrary",)))(u, w, qd, kd, qk, gl, states, do)


def _gdn_out_rows(o, z, nw):
    outs = []
    for h in range(GDN_H):
        hs = slice(h * GDN_D, (h + 1) * GDN_D)
        oh = o[:, hs]
        y = oh * lax.rsqrt(jnp.mean(oh * oh, axis=-1, keepdims=True) + RMS_EPS) * nw
        outs.append(y * _silu(z[:, hs]))
    return jnp.concatenate(outs, axis=1)


def gdn_out(o, p_all, nw, dy, lay, name):
    s = o.shape[0]
    tm = _tile(s, 512)
    bwd = dy is not None

    def body(*refs):
        o_r, z_r, nw_r = refs[:3]
        if not bwd:
            refs[3][...] = _gdn_out_rows(o_r[...], z_r[...], nw_r[...]).astype(BF16)
            return
        dy_r, do_o, dz_o, dnw_o = refs[3:]
        _, vjp = jax.vjp(_gdn_out_rows, o_r[...], z_r[...], nw_r[...])
        d_o, d_z, d_nw = vjp(dy_r[...].astype(F32))
        do_o[...] = d_o
        dz_o[...] = d_z.astype(BF16)

        @pl.when(pl.program_id(0) == 0)
        def _():
            dnw_o[...] = jnp.zeros_like(dnw_o)

        dnw_o[...] += d_nw

    row = pl.BlockSpec((tm, GDN_W), lambda i: (i, 0))
    zs = pl.BlockSpec((tm, GDN_W), lambda i: (i, _div(lay.z, GDN_W)))
    nws = pl.BlockSpec((1, GDN_D), lambda i: (0, 0))
    if not bwd:
        return pl.pallas_call(body, name=name, grid=(s // tm,), in_specs=[row, zs, nws], out_specs=row,
                              out_shape=SDS((s, GDN_W), BF16), compiler_params=_cp(("parallel",)))(o, p_all, nw)
    return pl.pallas_call(body, name=name, grid=(s // tm,), in_specs=[row, zs, nws, row], out_specs=[row, row, nws],
                          out_shape=[SDS((s, GDN_W), F32), SDS((s, GDN_W), BF16), SDS((1, GDN_D), F32)],
                          compiler_params=_cp(("arbitrary",)))(o, p_all, nw, dy)


def _cols_to_full(g):
    n, k, c = g.shape
    return g.transpose(1, 0, 2).reshape(k, n * c)


def _full_to_cols(w):
    k, nc = w.shape
    return w.reshape(k, N_DEV, nc // N_DEV).transpose(1, 0, 2)


def _rows_to_blocks(w):
    return w.reshape(N_DEV, w.shape[0] // N_DEV, w.shape[1])


def _pack_small(parts, rows):
    flat = jnp.concatenate([jnp.pad(p.reshape(-1), (0, -p.size % LANE)) for p in parts])
    return jnp.pad(flat, (0, rows * LANE - flat.size)).reshape(rows, LANE)


def kernel(x, mem, g_mix, w_in, sinks, conv_w, a_log, dt_bias, gdn_norm_w, g_mem, w_mem_kv, w_swa_up, w_gdn_up, w_xa_up, w_out, g_mlp, w_mlp_in, w_mlp_out, g_final, loss_target, m_g_mix, m_w_in, m_sinks, m_conv_w, m_a_log, m_dt_bias, m_gdn_norm_w, m_g_mem, m_w_mem_kv, m_w_swa_up, m_w_gdn_up, m_w_xa_up, m_w_out, m_g_mlp, m_w_mlp_in, m_w_mlp_out, m_g_final, v_g_mix, v_w_in, v_sinks, v_conv_w, v_a_log, v_dt_bias, v_gdn_norm_w, v_g_mem, v_w_mem_kv, v_w_swa_up, v_w_gdn_up, v_w_xa_up, v_w_out, v_g_mlp, v_w_mlp_in, v_w_mlp_out, v_g_final):
    xs, ms, tgt = x[0], mem[0], loss_target[0]
    s, d = xs.shape
    lay = Layout(d)
    px, py, pc = _position()
    dev = 4 * px + 2 * py + pc

    big = [w_in[0], w_mem_kv[0], w_swa_up[0], w_gdn_up[0], w_xa_up[0], w_out[0], w_mlp_in[0], w_mlp_out[0]]
    gathered = all_gather([w.astype(BF16) for w in big] + [conv_w[0]], "gather_weights")
    W_in = lay.pad_weight(_cols_to_full(gathered[0]))
    W_mkv = gathered[1].reshape(-1, gathered[1].shape[2])
    W_sup, W_gup, W_xup = (_cols_to_full(g) for g in gathered[2:5])
    W_out = gathered[5].reshape(-1, d)
    W_m1 = _cols_to_full(gathered[6])
    W_m2 = gathered[7].reshape(-1, d)
    convw = _cols_to_full(gathered[8])
    gp = jnp.zeros((8, LANE), F32).at[0, :GDN_H].set(a_log[0]).at[1, :GDN_H].set(dt_bias[0])

    n1 = rmsnorm_fwd(xs, g_mix, "norm_mix")
    p_all = matmul(n1, W_in, mode="nn", out_dtype=F32, name="proj_in", tm=2048, tn=512, tk=d)
    y_a = swa_fwd(p_all, sinks, lay, "swa_fwd")
    xc = gdn_conv_fwd(p_all, convw, lay, "gdn_conv_fwd")
    u, gw, gqd, gkd, gqk, ggl = gdn_pre_fwd(xc, p_all, gp, lay, "gdn_pre_fwd")
    o_b, states = gdn_scan_fwd(u, gw, gqd, gkd, gqk, ggl, "gdn_scan_fwd")
    y_b = gdn_out(o_b, p_all, gdn_norm_w, None, lay, "gdn_out_fwd")
    nm = rmsnorm_fwd(ms, g_mem, "norm_mem")
    mkv = matmul(nm, W_mkv, mode="nn", out_dtype=BF16, name="proj_mem", tk=d)
    y_c = xattn_fwd(p_all, mkv, lay, "xattn_fwd")
    merged = merge(p_all, y_a, y_b, y_c, W_sup, W_gup, W_xup, None, lay, "merge_fwd")
    h1 = matmul(merged, W_out, mode="nn", out_dtype=F32, name="proj_out", tm=2048, tn=512, tk=d, resid=xs)
    n2 = rmsnorm_fwd(h1, g_mlp, "norm_mlp")
    uu = matmul(n2, W_m1, mode="nn", out_dtype=F32, name="mlp_in", tm=2048, tn=512, tk=d)
    h2 = matmul(uu, W_m2, mode="nn", out_dtype=F32, name="mlp_out", tm=1024, tn=2048, tk=512, a_relu2=True, resid=h1)
    dh2, dg_final, lrow = final_norm_loss(h2, g_final.reshape(1, d), tgt, "final_loss")
    loss = lax.psum(lrow[0, 0], ("x", "y", "c"))

    du = matmul(dh2, W_m2, mode="nt", out_dtype=BF16, name="mlp_out_dx", tm=2048, tn=512, tk=d, relu2_grad_of=uu)
    dW_m2 = matmul(uu, dh2, mode="tn", out_dtype=BF16, name="mlp_out_dw", tm=1024, tn=2048, tk=512, a_relu2=True)
    dn2 = matmul(du, W_m1, mode="nt", out_dtype=F32, name="mlp_in_dx", tm=1024, tn=2048, tk=1024)
    dW_m1 = matmul(n2, du, mode="tn", out_dtype=BF16, name="mlp_in_dw", tm=2048, tn=1024, tk=512)
    dh1, dg_mlp = rmsnorm_bwd(h1, g_mlp, dn2, dh2, "norm_mlp_bwd")

    dmerged = matmul(dh1, W_out, mode="nt", out_dtype=F32, name="proj_out_dx", tm=2048, tn=512, tk=d)
    dW_out = matmul(merged, dh1, mode="tn", out_dtype=BF16, name="proj_out_dw", tm=2048, tn=1024, tk=512)
    dga, dgb, dgc, dta, dtb, dtc = merge(p_all, y_a, y_b, y_c, W_sup, W_gup, W_xup, dmerged, lay, "merge_bwd")
    dy_a = matmul(dta, W_sup, mode="nt", out_dtype=BF16, name="swa_up_dx", tk=d)
    dy_b = matmul(dtb, W_gup, mode="nt", out_dtype=BF16, name="gdn_up_dx", tk=d)
    dy_c = matmul(dtc, W_xup, mode="nt", out_dtype=BF16, name="xa_up_dx", tk=d)
    dW_sup = matmul(y_a, dta, mode="tn", out_dtype=BF16, name="swa_up_dw", tn=2048)
    dW_gup = matmul(y_b, dtb, mode="tn", out_dtype=BF16, name="gdn_up_dw", tn=2048)
    dW_xup = matmul(y_c, dtc, mode="tn", out_dtype=BF16, name="xa_up_dw", tn=2048)

    dq_a, dk_a, dv_a, dsinks = swa_bwd(p_all, sinks, dy_a, lay, "swa_bwd")
    dq_c, dmkv = xattn_bwd(p_all, mkv, dy_c, lay, "xattn_bwd")
    dW_mkv = matmul(nm, dmkv, mode="tn", out_dtype=BF16, name="proj_mem_dw", tk=256)
    dnm = matmul(dmkv, W_mkv, mode="nt", out_dtype=F32, name="proj_mem_dx", tk=1024)
    _, dg_mem = rmsnorm_bwd(ms, g_mem, dnm, None, "norm_mem_bwd")

    do_b, dz, dnorm_w = gdn_out(o_b, p_all, gdn_norm_w, dy_b, lay, "gdn_out_bwd")
    du_g, dw_g, dqd_g, dkd_g, dqk_g, dgl_g = gdn_scan_bwd(u, gw, gqd, gkd, gqk, ggl, states, do_b, "gdn_scan_bwd")
    dxc, dab, dgp = gdn_pre_bwd(xc, p_all, gp, du_g, dw_g, dqd_g, dkd_g, dqk_g, dgl_g, lay, "gdn_pre_bwd")
    dqkv, dconv = gdn_conv_bwd(p_all, convw, dxc, lay, "gdn_conv_bwd")

    dp_all = jnp.concatenate([dga, dgb, dgc, dq_a, dqkv, dz, dq_c, dk_a, dv_a, dab,
                              jnp.zeros((s, lay.pw - lay.end), BF16)], axis=1)
    dn1 = matmul(dp_all, W_in, mode="nt", out_dtype=F32, name="proj_in_dx", tm=1024, tn=2048, tk=1024)
    dW_in = matmul(n1, dp_all, mode="tn", out_dtype=BF16, name="proj_in_dw", tm=2048, tn=1024, tk=512)
    grad_x, dg_mix = rmsnorm_bwd(xs, g_mix, dn1, dh1, "norm_mix_bwd")

    blocks = [_full_to_cols(lay.unpad_weight(dW_in)), _rows_to_blocks(dW_mkv), _full_to_cols(dW_sup), _full_to_cols(dW_gup),
              _full_to_cols(dW_xup), _rows_to_blocks(dW_out), _full_to_cols(dW_m1), _rows_to_blocks(dW_m2)]
    split = [b.reshape((4, 2) + b.shape[1:]) for b in blocks]
    own = [lax.dynamic_index_in_dim(b, pc, axis=1, keepdims=False) for b in split]
    sib = [lax.dynamic_index_in_dim(b, 1 - pc, axis=1, keepdims=False) for b in split]
    from_sib = pair_exchange(sib, "grads_pair_exchange")
    chip_sum = [add_bf16(a, b, "grads_pair_add_%d" % i) for i, (a, b) in enumerate(zip(own, from_sib))]
    parts = chip_exchange(chip_sum, "grads_chip_exchange")

    shard_names = [(w_in, m_w_in, v_w_in), (w_mem_kv, m_w_mem_kv, v_w_mem_kv), (w_swa_up, m_w_swa_up, v_w_swa_up),
                   (w_gdn_up, m_w_gdn_up, v_w_gdn_up), (w_xa_up, m_w_xa_up, v_w_xa_up), (w_out, m_w_out, v_w_out),
                   (w_mlp_in, m_w_mlp_in, v_w_mlp_in), (w_mlp_out, m_w_mlp_out, v_w_mlp_out)]
    big_res = [adamw(p, w[0], m[0], v[0], "adamw_%d" % i) for i, (p, (w, m, v)) in enumerate(zip(parts, shard_names))]

    smalls = [(g_mix, m_g_mix, v_g_mix, dg_mix), (sinks, m_sinks, v_sinks, dsinks[:, :SWA_HQ]),
              (a_log, m_a_log, v_a_log, dgp[0:1, :GDN_H]), (dt_bias, m_dt_bias, v_dt_bias, dgp[1:2, :GDN_H]),
              (gdn_norm_w, m_gdn_norm_w, v_gdn_norm_w, dnorm_w), (g_mem, m_g_mem, v_g_mem, dg_mem),
              (g_mlp, m_g_mlp, v_g_mlp, dg_mlp), (g_final, m_g_final, v_g_final, dg_final)]
    sizes = [-(-t[0].size // LANE) * LANE for t in smalls] + [GDN_CONV * 3 * GDN_W]
    rows = -(-sum(sizes) // (8 * LANE)) * 8
    csh = conv_w.shape[2]

    def conv_place(a):
        return lax.dynamic_update_slice(jnp.zeros((GDN_CONV, 3 * GDN_W), F32), a[0], (0, dev * csh))

    g_pack = _pack_small([t[3] for t in smalls] + [dconv], rows)
    w_pack = _pack_small([t[0] for t in smalls] + [conv_place(conv_w)], rows)
    m_pack = _pack_small([t[1] for t in smalls] + [conv_place(m_conv_w)], rows)
    v_pack = _pack_small([t[2] for t in smalls] + [conv_place(v_conv_w)], rows)
    g_all = all_gather([g_pack], "gather_small_grads")[0]
    small_res = adamw(g_all, w_pack, m_pack, v_pack, "adamw_small")

    def unpack(arr):
        flat = arr.reshape(-1)
        outs, off = [], 0
        for t, sz in zip(smalls, sizes[:-1]):
            outs.append(flat[off:off + t[0].size].reshape(t[0].shape))
            off += sz
        cw = flat[off:off + sizes[-1]].reshape(GDN_CONV, 3 * GDN_W)
        outs.append(lax.dynamic_slice(cw, (0, dev * csh), (GDN_CONV, csh))[None])
        return outs

    sg, sd, sm, sv = (unpack(a) for a in small_res)
    bg, bd, bm, bv = ([r[i][None] for r in big_res] for i in range(4))

    def ordered(sm_, bg_):
        return [sm_[0], bg_[0], sm_[1], sm_[8], sm_[2], sm_[3], sm_[4], sm_[5], bg_[1], bg_[2], bg_[3], bg_[4], bg_[5],
                sm_[6], bg_[6], bg_[7], sm_[7]]

    return (loss, grad_x[None], *ordered(sg, bg), *ordered(sd, bd), *ordered(sm, bm), *ordered(sv, bv))
```

```python
import functools
import math

import jax
import jax.numpy as jnp
from jax import lax
from jax.experimental import pallas as pl
from jax.experimental.pallas import tpu as pltpu

F32, BF16 = jnp.float32, jnp.bfloat16
SDS = jax.ShapeDtypeStruct
MESH = pl.DeviceIdType.MESH
ANY = pl.BlockSpec(memory_space=pl.ANY)

SWA_HQ, SWA_HKV, SWA_HD, SWA_W = 16, 2, 64, 128
SWA_G = SWA_HQ // SWA_HKV
GDN_H, GDN_D, GDN_CONV, GDN_C = 4, 128, 4, 64
XA_H, XA_D = 4, 128
Q_W = SWA_HQ * SWA_HD
KV_W = SWA_HKV * SWA_HD
GDN_W = GDN_H * GDN_D
XA_W = XA_H * XA_D
RMS_EPS = 1e-6
L2_EPS = 1e-6
NEG = -1e30
N_DEV = 8
LANE = 128

ADAM_LR, ADAM_B1, ADAM_B2, ADAM_EPS, ADAM_WD, ADAM_STEP = 0.001, 0.9, 0.999, 1e-08, 0.01, 10

VMEM_BIG = 56 * 1024 * 1024


def _cp(sem, vmem=VMEM_BIG):
    return pltpu.CompilerParams(dimension_semantics=sem, vmem_limit_bytes=vmem)


def _div(a, b):
    assert a % b == 0, (a, b)
    return a // b


def _tile(n, t):
    t = min(t, n)
    assert n % t == 0, (n, t)
    return t


def _sigmoid(x):
    return 1.0 / (1.0 + jnp.exp(-x))


def _silu(x):
    return x * _sigmoid(x)


def _softplus(x):
    return jnp.maximum(x, 0.0) + jnp.log1p(jnp.exp(-jnp.abs(x)))


def _dot(a, b, dims, prec=None):
    return lax.dot_general(a, b, (dims, ((), ())), precision=prec, preferred_element_type=F32)


NN = ((1,), (0,))
NT = ((1,), (1,))
TN = ((0,), (0,))
HI = lax.Precision.HIGHEST


def _bdot_plain(a, b, dims):
    return _dot(a.astype(BF16), b.astype(BF16), dims)


@functools.partial(jax.custom_vjp, nondiff_argnums=(2,))
def _bdot_vjp(a, b, dims):
    return _bdot_plain(a, b, dims)


def _bdot_vjp_fwd(a, b, dims):
    return _bdot_plain(a, b, dims), (a, b)


def _bdot_vjp_bwd(dims, res, ct):
    a, b = res
    if dims == NN:
        return _bdot_plain(ct, b, NT), _bdot_plain(a, ct, TN)
    assert dims == NT, dims
    return _bdot_plain(ct, b, NN), _bdot_plain(ct, a, TN)


_bdot_vjp.defvjp(_bdot_vjp_fwd, _bdot_vjp_bwd)


class Layout:
    def __init__(self, d):
        self.d = d
        self.g = 0
        self.q = 3 * d
        self.qkv = self.q + Q_W
        self.z = self.qkv + 3 * GDN_W
        self.qc = self.z + GDN_W
        self.k = self.qc + XA_W
        self.v = self.k + KV_W
        self.ab = self.v + KV_W
        self.end = self.ab + LANE
        self.pw = -(-self.end // 1024) * 1024
        self.lq, self.lk, self.lv, self.lqkv = 0, Q_W, Q_W + KV_W, Q_W + 2 * KV_W
        self.la = self.lqkv + 3 * GDN_W
        self.lz = self.la + 2 * GDN_H
        self.lqc = self.lz + GDN_W
        self.lg = self.lqc + XA_W
        self.lw = self.lg + 3 * d

    def pad_weight(self, w):
        k = w.shape[0]
        parts = [w[:, self.lg:self.lw], w[:, self.lq:self.lk], w[:, self.lqkv:self.la], w[:, self.lz:self.lqc],
                 w[:, self.lqc:self.lg], w[:, self.lk:self.lv], w[:, self.lv:self.lqkv], w[:, self.la:self.lz],
                 jnp.zeros((k, self.pw - self.ab - 2 * GDN_H), w.dtype)]
        return jnp.concatenate(parts, axis=1)

    def unpad_weight(self, w):
        parts = [w[:, self.q:self.qkv], w[:, self.k:self.v], w[:, self.v:self.ab], w[:, self.qkv:self.z],
                 w[:, self.ab:self.ab + 2 * GDN_H], w[:, self.z:self.qc], w[:, self.qc:self.k], w[:, self.g:self.q]]
        return jnp.concatenate(parts, axis=1)


def _position():
    return lax.axis_index("x"), lax.axis_index("y"), lax.axis_index("c")


def all_gather(arrs, name):
    n = len(arrs)

    def body(*refs):
        ins, outs = refs[:n], refs[n:2 * n]
        send_sems, recv_sems, local_sems = refs[2 * n:]
        x, y, c = _position()
        me, sibling = (x, y, c), (x, y, 1 - c)
        chips = [(1 - x, y), (x, 1 - y), (1 - x, 1 - y)]

        def blk(o, p):
            return o.at[4 * p[0] + 2 * p[1] + p[2]]

        def copy(i, k, block, to, src=None):
            return pltpu.make_async_remote_copy(
                src_ref=blk(outs[i], block) if src is None else src, dst_ref=blk(outs[i], block),
                send_sem=send_sems.at[i, k], recv_sem=recv_sems.at[i, k], device_id=to, device_id_type=MESH)

        started = []
        for i in range(n):
            mine = pltpu.make_async_copy(ins[i], blk(outs[i], me), local_sems.at[i])
            mine.start()
            first = [copy(i, 0, me, sibling, src=ins[i])]
            first += [copy(i, 1 + j, me, (*chip, c), src=ins[i]) for j, chip in enumerate(chips)]
            for cp in first:
                cp.start()
            started += [mine.wait] + [cp.wait_send for cp in first]
        for i in range(n):
            for j, chip in enumerate(chips):
                copy(i, 1 + j, (*chip, c), me).wait_recv()
                fwd = copy(i, 4 + j, (*chip, c), sibling)
                fwd.start()
                started.append(fwd.wait_send)
        for i in range(n):
            copy(i, 0, sibling, me).wait_recv()
            for j, chip in enumerate(chips):
                copy(i, 4 + j, (*chip, 1 - c), me).wait_recv()
        for w in started:
            w()

    return pl.pallas_call(
        body, name=name,
        out_shape=[SDS((N_DEV,) + a.shape, a.dtype) for a in arrs],
        in_specs=[ANY] * n, out_specs=[ANY] * n,
        scratch_shapes=[pltpu.SemaphoreType.DMA((n, 7)), pltpu.SemaphoreType.DMA((n, 7)), pltpu.SemaphoreType.DMA((n,))],
    )(*arrs)


def pair_exchange(arrs, name):
    n = len(arrs)

    def body(*refs):
        ins, outs = refs[:n], refs[n:2 * n]
        send_sems, recv_sems = refs[2 * n:]
        x, y, c = _position()
        copies = [pltpu.make_async_remote_copy(src_ref=ins[i], dst_ref=outs[i], send_sem=send_sems.at[i],
                                               recv_sem=recv_sems.at[i], device_id=(x, y, 1 - c), device_id_type=MESH)
                  for i in range(n)]
        for cp in copies:
            cp.start()
        for cp in copies:
            cp.wait()

    return pl.pallas_call(
        body, name=name, out_shape=[SDS(a.shape, a.dtype) for a in arrs],
        in_specs=[ANY] * n, out_specs=[ANY] * n,
        scratch_shapes=[pltpu.SemaphoreType.DMA((n,)), pltpu.SemaphoreType.DMA((n,))],
    )(*arrs)


def chip_exchange(arrs, name):
    n = len(arrs)

    def body(*refs):
        ins, outs = refs[:n], refs[n:2 * n]
        send_sems, recv_sems, local_sems = refs[2 * n:]
        x, y, c = _position()
        my_chip = 2 * x + y
        chips = [(1 - x, y), (x, 1 - y), (1 - x, 1 - y)]
        waits = []
        for i in range(n):
            mine = pltpu.make_async_copy(ins[i].at[my_chip], outs[i].at[my_chip], local_sems.at[i])
            mine.start()
            waits.append(mine.wait)
            for k, (px, py) in enumerate(chips):
                cp = pltpu.make_async_remote_copy(
                    src_ref=ins[i].at[2 * px + py], dst_ref=outs[i].at[my_chip],
                    send_sem=send_sems.at[i, k], recv_sem=recv_sems.at[i, k], device_id=(px, py, c), device_id_type=MESH)
                cp.start()
                waits.append(cp.wait_send)
        for i in range(n):
            for k, (px, py) in enumerate(chips):
                pltpu.make_async_remote_copy(
                    src_ref=ins[i].at[my_chip], dst_ref=outs[i].at[2 * px + py],
                    send_sem=send_sems.at[i, k], recv_sem=recv_sems.at[i, k], device_id=(px, py, c),
                    device_id_type=MESH).wait_recv()
        for w in waits:
            w()

    return pl.pallas_call(
        body, name=name, out_shape=[SDS(a.shape, a.dtype) for a in arrs],
        in_specs=[ANY] * n, out_specs=[ANY] * n,
        scratch_shapes=[pltpu.SemaphoreType.DMA((n, 3)), pltpu.SemaphoreType.DMA((n, 3)), pltpu.SemaphoreType.DMA((n,))],
    )(*arrs)


def add_bf16(a, b, name):
    p, r, c = a.shape
    tr = _tile(r, 256)

    def body(a_ref, b_ref, o_ref):
        o_ref[...] = (a_ref[...].astype(F32) + b_ref[...].astype(F32)).astype(BF16)

    spec = pl.BlockSpec((1, tr, c), lambda j, i: (j, i, 0))
    return pl.pallas_call(body, name=name, grid=(p, r // tr), in_specs=[spec, spec], out_specs=spec,
                          out_shape=SDS(a.shape, BF16), compiler_params=_cp(("parallel", "parallel")))(a, b)


def adamw(parts, w, m, v, name):
    p, r, c = parts.shape
    tr = _tile(r, 128 if c > 1024 else 256)

    def body(p_ref, w_ref, m_ref, v_ref, g_out, d_out, m_out, v_out):
        g = p_ref[0].astype(F32)
        for j in range(1, p):
            g = g + p_ref[j].astype(F32)
        mn = ADAM_B1 * m_ref[...] + (1.0 - ADAM_B1) * g
        vn = ADAM_B2 * v_ref[...] + (1.0 - ADAM_B2) * jnp.square(g)
        m_hat = mn / (1.0 - ADAM_B1 ** ADAM_STEP)
        v_hat = vn / (1.0 - ADAM_B2 ** ADAM_STEP)
        g_out[...] = g
        d_out[...] = -ADAM_LR * (m_hat / (jnp.sqrt(v_hat) + ADAM_EPS) + ADAM_WD * w_ref[...])
        m_out[...] = mn
        v_out[...] = vn

    spec = pl.BlockSpec((tr, c), lambda i: (i, 0))
    return pl.pallas_call(
        body, name=name, grid=(r // tr,),
        in_specs=[pl.BlockSpec((p, tr, c), lambda i: (0, i, 0)), spec, spec, spec],
        out_specs=[spec] * 4, out_shape=[SDS((r, c), F32)] * 4, compiler_params=_cp(("parallel",)))(parts, w, m, v)


def matmul(a, b, *, mode, out_dtype, name, tm=1024, tn=1024, tk=512, a_relu2=False, resid=None, relu2_grad_of=None):
    if mode == "nn":
        (m, k), (k2, n) = a.shape, b.shape
    elif mode == "nt":
        (m, k), (n, k2) = a.shape, b.shape
    else:
        (k, m), (k2, n) = a.shape, b.shape
    assert k == k2, (a.shape, b.shape, mode)
    tm, tn, tk = _tile(m, tm), _tile(n, tn), _tile(k, tk)
    nk = k // tk
    dims = {"nn": NN, "nt": NT, "tn": TN}[mode]
    extras = [e for e in (resid, relu2_grad_of) if e is not None]

    def body(*refs):
        a_ref, b_ref = refs[:2]
        e_refs = refs[2:2 + len(extras)]
        o_ref = refs[2 + len(extras)]
        acc_ref = refs[-1] if nk > 1 else None

        def operands():
            av = a_ref[...]
            if a_relu2:
                av = jnp.square(jnp.maximum(av.astype(F32), 0.0))
            return av.astype(BF16), b_ref[...].astype(BF16)

        def finish(r):
            e = list(e_refs)
            if resid is not None:
                r = r + e.pop(0)[...]
            if relu2_grad_of is not None:
                r = r * (2.0 * jnp.maximum(e.pop(0)[...], 0.0))
            o_ref[...] = r.astype(out_dtype)

        if nk == 1:
            av, bv = operands()
            finish(_dot(av, bv, dims))
        else:
            kk = pl.program_id(2)

            @pl.when(kk == 0)
            def _():
                acc_ref[...] = jnp.zeros_like(acc_ref)

            av, bv = operands()
            acc_ref[...] += _dot(av, bv, dims)

            @pl.when(kk == nk - 1)
            def _():
                finish(acc_ref[...])

    a_spec = {"nn": pl.BlockSpec((tm, tk), lambda i, j, kk: (i, kk)),
              "nt": pl.BlockSpec((tm, tk), lambda i, j, kk: (i, kk)),
              "tn": pl.BlockSpec((tk, tm), lambda i, j, kk: (kk, i))}[mode]
    b_spec = {"nn": pl.BlockSpec((tk, tn), lambda i, j, kk: (kk, j)),
              "nt": pl.BlockSpec((tn, tk), lambda i, j, kk: (j, kk)),
              "tn": pl.BlockSpec((tk, tn), lambda i, j, kk: (kk, j))}[mode]
    o_spec = pl.BlockSpec((tm, tn), lambda i, j, kk: (i, j))
    return pl.pallas_call(
        body, name=name, grid=(m // tm, n // tn, nk),
        in_specs=[a_spec, b_spec] + [o_spec] * len(extras), out_specs=o_spec, out_shape=SDS((m, n), out_dtype),
        scratch_shapes=[pltpu.VMEM((tm, tn), F32)] if nk > 1 else [],
        compiler_params=_cp(("parallel", "parallel", "arbitrary")))(a, b, *extras)


def rmsnorm_fwd(x, g, name):
    s, d = x.shape
    tm = _tile(s, 256)

    def body(x_ref, g_ref, o_ref):
        xv = x_ref[...]
        r = lax.rsqrt(jnp.mean(xv * xv, axis=-1, keepdims=True) + RMS_EPS)
        o_ref[...] = (xv * r * g_ref[...]).astype(BF16)

    row = pl.BlockSpec((tm, d), lambda i: (i, 0))
    return pl.pallas_call(body, name=name, grid=(s // tm,), in_specs=[row, pl.BlockSpec((1, d), lambda i: (0, 0))],
                          out_specs=row, out_shape=SDS((s, d), BF16), compiler_params=_cp(("parallel",)))(x, g)


def _rms_bwd_rows(xv, gv, dy):
    r = lax.rsqrt(jnp.mean(xv * xv, axis=-1, keepdims=True) + RMS_EPS)
    xh = xv * r
    dxh = dy * gv
    dx = r * (dxh - xh * jnp.mean(dxh * xh, axis=-1, keepdims=True))
    return dx, jnp.sum(dy * xh, axis=0, keepdims=True)


def rmsnorm_bwd(x, g, dn, resid, name):
    s, d = x.shape
    tm = _tile(s, 256)
    has_r = resid is not None

    def body(*refs):
        x_ref, g_ref, dn_ref = refs[:3]
        dx_ref, dg_ref = refs[-2:]
        dx, part = _rms_bwd_rows(x_ref[...], g_ref[...], dn_ref[...].astype(F32))
        if has_r:
            dx = dx + refs[3][...]
        dx_ref[...] = dx

        @pl.when(pl.program_id(0) == 0)
        def _():
            dg_ref[...] = jnp.zeros_like(dg_ref)

        dg_ref[...] += part

    row = pl.BlockSpec((tm, d), lambda i: (i, 0))
    vec = pl.BlockSpec((1, d), lambda i: (0, 0))
    ins = [x, g, dn] + ([resid] if has_r else [])
    return pl.pallas_call(body, name=name, grid=(s // tm,), in_specs=[row, vec, row] + ([row] if has_r else []),
                          out_specs=[row, vec], out_shape=[SDS((s, d), F32), SDS((1, d), F32)],
                          compiler_params=_cp(("arbitrary",)))(*ins)


def final_norm_loss(h, g, tgt, name):
    s, d = h.shape
    tm = _tile(s, 256)

    def body(h_ref, g_ref, t_ref, dh_ref, dg_ref, l_ref):
        xv, gv = h_ref[...], g_ref[...]
        r = lax.rsqrt(jnp.mean(xv * xv, axis=-1, keepdims=True) + RMS_EPS)
        e = xv * r * gv - t_ref[...]
        lpart = 0.5 * jnp.sum(jnp.mean(e * e, axis=-1, keepdims=True), axis=0, keepdims=True)
        dx, part = _rms_bwd_rows(xv, gv, e * (1.0 / d))
        dh_ref[...] = dx

        @pl.when(pl.program_id(0) == 0)
        def _():
            dg_ref[...] = jnp.zeros_like(dg_ref)
            l_ref[...] = jnp.zeros_like(l_ref)

        dg_ref[...] += part
        l_ref[...] += jnp.broadcast_to(lpart, l_ref.shape)

    row = pl.BlockSpec((tm, d), lambda i: (i, 0))
    vec = pl.BlockSpec((1, d), lambda i: (0, 0))
    lsp = pl.BlockSpec((1, LANE), lambda i: (0, 0))
    return pl.pallas_call(body, name=name, grid=(s // tm,), in_specs=[row, vec, row], out_specs=[row, vec, lsp],
                          out_shape=[SDS((s, d), F32), SDS((1, d), F32), SDS((1, LANE), F32)],
                          compiler_params=_cp(("arbitrary",)))(h, g, tgt)


def merge(p_all, ya, yb, yc, wa, wb, wc, dm, lay, name):
    s, d = ya.shape[0], lay.d
    tm, tn = _tile(s, 512), _tile(d, 512)
    nj = d // tn
    bwd = dm is not None

    def body(*refs):
        ga, gb, gc, ya_r, yb_r, yc_r, wa_r, wb_r, wc_r = refs[:9]
        ts = [_dot(y[...], w[...], NN) for y, w in ((ya_r, wa_r), (yb_r, wb_r), (yc_r, wc_r))]
        gs = [_sigmoid(g[...]) for g in (ga, gb, gc)]
        if not bwd:
            refs[9][...] = (gs[0] * ts[0] + gs[1] * ts[1] + gs[2] * ts[2]).astype(BF16)
        else:
            dmv = refs[9][...]
            for i in range(3):
                refs[10 + i][...] = (dmv * ts[i] * (gs[i] * (1.0 - gs[i]))).astype(BF16)
                refs[13 + i][...] = (dmv * gs[i]).astype(BF16)

    gate_specs = [pl.BlockSpec((tm, tn), lambda i, j, b=b: (i, b * nj + j)) for b in range(3)]
    y_specs = [pl.BlockSpec((tm, y.shape[1]), lambda i, j: (i, 0)) for y in (ya, yb, yc)]
    w_specs = [pl.BlockSpec((w.shape[0], tn), lambda i, j: (0, j)) for w in (wa, wb, wc)]
    o_spec = pl.BlockSpec((tm, tn), lambda i, j: (i, j))
    n_out = 6 if bwd else 1
    out = pl.pallas_call(
        body, name=name, grid=(s // tm, nj),
        in_specs=gate_specs + y_specs + w_specs + ([o_spec] if bwd else []),
        out_specs=[o_spec] * n_out, out_shape=[SDS((s, d), BF16)] * n_out,
        compiler_params=_cp(("parallel", "parallel")))(p_all, p_all, p_all, ya, yb, yc, wa, wb, wc, *([dm] if bwd else []))
    return out if bwd else out[0]


def _swa_probs(q, kc, sink, first):
    s = _dot(q, kc, NT) * (SWA_HD ** -0.5)
    qi = lax.broadcasted_iota(jnp.int32, s.shape, 0) % SWA_W
    kj = lax.broadcasted_iota(jnp.int32, s.shape, 1)
    mask = (kj > qi) & (kj <= qi + SWA_W) & ((kj >= SWA_W) | jnp.logical_not(first))
    s = jnp.where(mask, s, NEG)
    m = jnp.maximum(jnp.max(s, axis=-1, keepdims=True), sink)
    p = jnp.exp(s - m)
    es = jnp.exp(sink - m)
    denom = jnp.sum(p, axis=-1, keepdims=True) + es
    return p / denom, es / denom


def _swa_stack(ref, h):
    return jnp.concatenate([ref[:, pl.ds((h * SWA_G + g) * SWA_HD, SWA_HD)] for g in range(SWA_G)], axis=0)


def _swa_sink_col(sk_ref, h):
    head = lax.broadcasted_iota(jnp.int32, (SWA_G * SWA_W, 1), 0) // SWA_W
    col = jnp.zeros((SWA_G * SWA_W, 1), F32)
    for g in range(SWA_G):
        hh = h * SWA_G + g
        col = jnp.where(head == g, sk_ref[0:1, hh:hh + 1], col)
    return col


def _swa_specs(lay):
    w = SWA_W
    q_spec = pl.BlockSpec((w, Q_W), lambda n: (n, _div(lay.q, Q_W)))
    cur = lambda off: pl.BlockSpec((w, KV_W), lambda n: (n, _div(off, KV_W)))
    prev = lambda off: pl.BlockSpec((w, KV_W), lambda n: (jnp.maximum(n - 1, 0), _div(off, KV_W)))
    return q_spec, cur(lay.k), prev(lay.k), cur(lay.v), prev(lay.v)


def swa_fwd(p_all, sinks, lay, name):
    s = p_all.shape[0]
    nb = _div(s, SWA_W)

    def body(q_ref, kc_ref, kp_ref, vc_ref, vp_ref, sk_ref, o_ref):
        first = pl.program_id(0) == 0
        for h in range(SWA_HKV):
            hs = pl.ds(h * SWA_HD, SWA_HD)
            kc = jnp.concatenate([kp_ref[:, hs], kc_ref[:, hs]], axis=0).astype(BF16)
            vc = jnp.concatenate([vp_ref[:, hs], vc_ref[:, hs]], axis=0).astype(BF16)
            q = _swa_stack(q_ref, h).astype(BF16)
            p, _ = _swa_probs(q, kc, _swa_sink_col(sk_ref, h), first)
            o = _dot(p.astype(BF16), vc, NN).astype(BF16)
            for g in range(SWA_G):
                o_ref[:, pl.ds((h * SWA_G + g) * SWA_HD, SWA_HD)] = o[g * SWA_W:(g + 1) * SWA_W]

    q_spec, kc_s, kp_s, vc_s, vp_s = _swa_specs(lay)
    return pl.pallas_call(
        body, name=name, grid=(nb,),
        in_specs=[q_spec, kc_s, kp_s, vc_s, vp_s, pl.BlockSpec(sinks.shape, lambda n: (0, 0))],
        out_specs=pl.BlockSpec((SWA_W, Q_W), lambda n: (n, 0)), out_shape=SDS((s, Q_W), BF16),
        compiler_params=_cp(("parallel",)))(p_all, p_all, p_all, p_all, p_all, sinks)


def swa_bwd(p_all, sinks, dy, lay, name):
    s = p_all.shape[0]
    nb = _div(s, SWA_W)
    w = SWA_W

    def body(q_ref, kc_ref, kp_ref, vc_ref, vp_ref, sk_ref, do_ref, dq_ref, dk_ref, dv_ref, ds_ref, kcar, vcar):
        n = pl.program_id(0)
        first = n == 0

        @pl.when(first)
        def _():
            kcar[...] = jnp.zeros_like(kcar)
            vcar[...] = jnp.zeros_like(vcar)
            ds_ref[...] = jnp.zeros_like(ds_ref)

        @pl.when(n < nb)
        def _():
            lane = lax.broadcasted_iota(jnp.int32, (1, LANE), 1)
            dsink = jnp.zeros((1, LANE), F32)
            for h in range(SWA_HKV):
                hs = pl.ds(h * SWA_HD, SWA_HD)
                kc = jnp.concatenate([kp_ref[:, hs], kc_ref[:, hs]], axis=0).astype(BF16)
                vc = jnp.concatenate([vp_ref[:, hs], vc_ref[:, hs]], axis=0).astype(BF16)
                q = _swa_stack(q_ref, h).astype(BF16)
                do = _swa_stack(do_ref, h).astype(BF16)
                p, ps = _swa_probs(q, kc, _swa_sink_col(sk_ref, h), first)
                dp = _dot(do, vc, NT)
                dvc = _dot(p.astype(BF16), do, TN)
                rs = jnp.sum(dp * p, axis=-1, keepdims=True)
                dsb = (p * (dp - rs) * (SWA_HD ** -0.5)).astype(BF16)
                dq = _dot(dsb, kc, NN).astype(BF16)
                dkc = _dot(dsb, q, TN)
                psr = ps * rs
                for g in range(SWA_G):
                    hh = h * SWA_G + g
                    dq_ref[:, pl.ds(hh * SWA_HD, SWA_HD)] = dq[g * w:(g + 1) * w]
                    dsink = dsink + jnp.where(lane == hh, -jnp.sum(psr[g * w:(g + 1) * w], axis=0, keepdims=True), 0.0)
                dk_ref[:, hs] = (kcar[:, hs] + dkc[:w]).astype(BF16)
                dv_ref[:, hs] = (vcar[:, hs] + dvc[:w]).astype(BF16)
                kcar[:, hs] = dkc[w:]
                vcar[:, hs] = dvc[w:]
            ds_ref[...] += dsink

        @pl.when(n == nb)
        def _():
            dk_ref[...] = kcar[...].astype(BF16)
            dv_ref[...] = vcar[...].astype(BF16)

    last = nb - 1
    q_spec = pl.BlockSpec((w, Q_W), lambda n: (jnp.minimum(n, last), _div(lay.q, Q_W)))
    cur = lambda off: pl.BlockSpec((w, KV_W), lambda n: (jnp.minimum(n, last), _div(off, KV_W)))
    prev = lambda off: pl.BlockSpec((w, KV_W), lambda n: (jnp.clip(n - 1, 0, last), _div(off, KV_W)))
    row = pl.BlockSpec((w, Q_W), lambda n: (jnp.minimum(n, last), 0))
    kv_out = pl.BlockSpec((w, KV_W), lambda n: (jnp.maximum(n - 1, 0), 0))
    return pl.pallas_call(
        body, name=name, grid=(nb + 1,),
        in_specs=[q_spec, cur(lay.k), prev(lay.k), cur(lay.v), prev(lay.v), pl.BlockSpec(sinks.shape, lambda n: (0, 0)), row],
        out_specs=[row, kv_out, kv_out, pl.BlockSpec((1, LANE), lambda n: (0, 0))],
        out_shape=[SDS((s, Q_W), BF16), SDS((s, KV_W), BF16), SDS((s, KV_W), BF16), SDS((1, LANE), F32)],
        scratch_shapes=[pltpu.VMEM((w, KV_W), F32), pltpu.VMEM((w, KV_W), F32)],
        compiler_params=_cp(("arbitrary",)))(p_all, p_all, p_all, p_all, p_all, sinks, dy)


def _xa_probs(q, mk):
    s = _dot(q, mk, NT) * (XA_D ** -0.5)
    p = jnp.exp(s - jnp.max(s, axis=-1, keepdims=True))
    return p / jnp.sum(p, axis=-1, keepdims=True)


def xattn_fwd(p_all, mkv, lay, name):
    s, nm = p_all.shape[0], mkv.shape[0]
    tm = _tile(s, 512)

    def body(q_ref, mkv_ref, o_ref):
        for h in range(XA_H):
            cols = pl.ds(h * XA_D, XA_D)
            p = _xa_probs(q_ref[:, cols].astype(BF16), mkv_ref[:, cols])
            o_ref[:, cols] = _dot(p.astype(BF16), mkv_ref[:, pl.ds(XA_W + h * XA_D, XA_D)], NN).astype(BF16)

    return pl.pallas_call(
        body, name=name, grid=(s // tm,),
        in_specs=[pl.BlockSpec((tm, XA_W), lambda i: (i, _div(lay.qc, XA_W))), pl.BlockSpec((nm, 2 * XA_W), lambda i: (0, 0))],
        out_specs=pl.BlockSpec((tm, XA_W), lambda i: (i, 0)), out_shape=SDS((s, XA_W), BF16),
        compiler_params=_cp(("parallel",)))(p_all, mkv)


def xattn_bwd(p_all, mkv, dy, lay, name):
    s, nm = p_all.shape[0], mkv.shape[0]
    tm = _tile(s, 512)

    def body(q_ref, mkv_ref, do_ref, dq_ref, dmkv_ref):
        @pl.when(pl.program_id(0) == 0)
        def _():
            dmkv_ref[...] = jnp.zeros_like(dmkv_ref)

        for h in range(XA_H):
            cols = pl.ds(h * XA_D, XA_D)
            vcols = pl.ds(XA_W + h * XA_D, XA_D)
            q = q_ref[:, cols].astype(BF16)
            do = do_ref[:, cols].astype(BF16)
            p = _xa_probs(q, mkv_ref[:, cols])
            dp = _dot(do, mkv_ref[:, vcols], NT)
            dmkv_ref[:, vcols] += _dot(p.astype(BF16), do, TN)
            dsb = (p * (dp - jnp.sum(dp * p, axis=-1, keepdims=True)) * (XA_D ** -0.5)).astype(BF16)
            dq_ref[:, cols] = _dot(dsb, mkv_ref[:, cols], NN).astype(BF16)
            dmkv_ref[:, cols] += _dot(dsb, q, TN)

    row = pl.BlockSpec((tm, XA_W), lambda i: (i, 0))
    full = pl.BlockSpec((nm, 2 * XA_W), lambda i: (0, 0))
    return pl.pallas_call(
        body, name=name, grid=(s // tm,),
        in_specs=[pl.BlockSpec((tm, XA_W), lambda i: (i, _div(lay.qc, XA_W))), full, row],
        out_specs=[row, full], out_shape=[SDS((s, XA_W), BF16), SDS((nm, 2 * XA_W), F32)],
        compiler_params=_cp(("arbitrary",)))(p_all, mkv, dy)


def _shift_down(cur, prev8, s):
    cat = jnp.concatenate([prev8, cur[0:8]], axis=0)
    return pltpu.roll(cur, s, axis=0), pltpu.roll(cat, s, axis=0)[8:16]


def _shift_up(cur, next8, s):
    tm = cur.shape[0]
    cat = jnp.concatenate([cur[tm - 8:tm], next8], axis=0)
    return pltpu.roll(cur, tm - s, axis=0), pltpu.roll(cat, 16 - s, axis=0)[0:8]


def gdn_conv_fwd(p_all, conv_w, lay, name):
    s = p_all.shape[0]
    tm = _tile(s, 512)
    c0 = _div(lay.qkv, GDN_W)

    def body(x_ref, prev_ref, w_ref, o_ref):
        cur = x_ref[...]
        prev8 = jnp.where(pl.program_id(1) > 0, prev_ref[...], 0.0)
        main = w_ref[GDN_CONV - 1:GDN_CONV, :] * cur
        top = w_ref[GDN_CONV - 1:GDN_CONV, :] * cur[0:8]
        for sft in range(1, GDN_CONV):
            wi = w_ref[GDN_CONV - 1 - sft:GDN_CONV - sft, :]
            a, b = _shift_down(cur, prev8, sft)
            main = main + wi * a
            top = top + wi * b
        o_ref[...] = main
        o_ref[0:8, :] = top

    return pl.pallas_call(
        body, name=name, grid=(3, s // tm),
        in_specs=[pl.BlockSpec((tm, GDN_W), lambda c, i: (i, c0 + c)),
                  pl.BlockSpec((8, GDN_W), lambda c, i: (jnp.maximum(i * (tm // 8) - 1, 0), c0 + c)),
                  pl.BlockSpec((GDN_CONV, GDN_W), lambda c, i: (0, c))],
        out_specs=pl.BlockSpec((tm, GDN_W), lambda c, i: (i, c)), out_shape=SDS((s, 3 * GDN_W), F32),
        compiler_params=_cp(("parallel", "parallel")))(p_all, p_all, conv_w)


def gdn_conv_bwd(p_all, conv_w, dxc, lay, name):
    s = p_all.shape[0]
    tm = _tile(s, 512)
    c0 = _div(lay.qkv, GDN_W)
    nt = s // tm

    def body(x_ref, prev_ref, d_ref, next_ref, w_ref, dx_ref, dw_ref):
        i = pl.program_id(1)
        cur, d = x_ref[...], d_ref[...]
        prev8 = jnp.where(i > 0, prev_ref[...], 0.0)
        next8 = jnp.where(i < nt - 1, next_ref[...], 0.0)
        row = lax.broadcasted_iota(jnp.int32, (tm, 1), 0)
        main = w_ref[GDN_CONV - 1:GDN_CONV, :] * d
        bot = w_ref[GDN_CONV - 1:GDN_CONV, :] * d[tm - 8:tm]
        dws = [jnp.sum(d * cur, axis=0, keepdims=True)]
        for sft in range(1, GDN_CONV):
            wi = w_ref[GDN_CONV - 1 - sft:GDN_CONV - sft, :]
            a, b = _shift_up(d, next8, sft)
            main = main + wi * a
            bot = bot + wi * b
            xa, xb = _shift_down(cur, prev8, sft)
            dws.append(jnp.sum(jnp.where(row >= 8, d * xa, 0.0), axis=0, keepdims=True)
                       + jnp.sum(d[0:8] * xb, axis=0, keepdims=True))
        dx_ref[...] = main.astype(BF16)
        dx_ref[tm - 8:tm, :] = bot.astype(BF16)

        @pl.when(i == 0)
        def _():
            dw_ref[...] = jnp.zeros_like(dw_ref)

        for sft in range(GDN_CONV):
            dw_ref[GDN_CONV - 1 - sft:GDN_CONV - sft, :] += dws[sft]

    return pl.pallas_call(
        body, name=name, grid=(3, nt),
        in_specs=[pl.BlockSpec((tm, GDN_W), lambda c, i: (i, c0 + c)),
                  pl.BlockSpec((8, GDN_W), lambda c, i: (jnp.maximum(i * (tm // 8) - 1, 0), c0 + c)),
                  pl.BlockSpec((tm, GDN_W), lambda c, i: (i, c)),
                  pl.BlockSpec((8, GDN_W), lambda c, i: (jnp.minimum((i + 1) * (tm // 8), s // 8 - 1), c)),
                  pl.BlockSpec((GDN_CONV, GDN_W), lambda c, i: (0, c))],
        out_specs=[pl.BlockSpec((tm, GDN_W), lambda c, i: (i, c)), pl.BlockSpec((GDN_CONV, GDN_W), lambda c, i: (0, c))],
        out_shape=[SDS((s, 3 * GDN_W), BF16), SDS((GDN_CONV, 3 * GDN_W), F32)],
        compiler_params=_cp(("parallel", "arbitrary")))(p_all, p_all, dxc, dxc, conv_w)


def _gdn_chunk(xq, xk, xv, ab, gp, bdot=_bdot_plain):
    c = GDN_C
    nc = xq.shape[0] // c
    lane = lax.broadcasted_iota(jnp.int32, (c, LANE), 1)
    row = lax.broadcasted_iota(jnp.int32, (c, c), 0)
    col = lax.broadcasted_iota(jnp.int32, (c, c), 1)
    g_tile = -jnp.exp(gp[0:1, :]) * _softplus(ab + gp[1:2, :])
    b_tile = _sigmoid(ab)
    tri = (row >= col).astype(F32)
    qa, ka, va = _silu(xq), _silu(xk), _silu(xv)
    items = []
    for ci in range(nc):
        rs = slice(ci * c, (ci + 1) * c)
        gcum = _dot(tri, g_tile[rs], NN, HI)
        gcum_t = gcum.T
        for h in range(GDN_H):
            hs = slice(h * GDN_D, (h + 1) * GDN_D)
            q, k, v = qa[rs, hs], ka[rs, hs], va[rs, hs]
            q = q * lax.rsqrt(jnp.sum(q * q, axis=-1, keepdims=True) + L2_EPS) * (GDN_D ** -0.5)
            k = k * lax.rsqrt(jnp.sum(k * k, axis=-1, keepdims=True) + L2_EPS)
            gc = jnp.sum(jnp.where(lane == h, gcum, 0.0), axis=1, keepdims=True)
            beta = jnp.sum(jnp.where(lane == GDN_H + h, b_tile[rs], 0.0), axis=1, keepdims=True)
            decay = jnp.exp(jnp.where(row >= col, gc - gcum_t[h:h + 1, :], NEG))
            items.append((q, k, v, gc, beta, decay))
    kks = [bdot(k, k, NT) for (_, k, _, _, _, _) in items]
    pws = [-jnp.where(row > col, it[4] * kk * it[5], 0.0) for it, kk in zip(items, kks)]
    nns = list(pws)
    for _ in range(5):
        pws = [bdot(p, p, NN) for p in pws]
        nns = [n + p + bdot(n, p, NN) for n, p in zip(nns, pws)]
    qks = [bdot(q, k, NT) for (q, k, _, _, _, _) in items]
    out = []
    for (q, k, v, gc, beta, decay), n, qk in zip(items, nns, qks):
        eg = jnp.exp(gc)
        vb = v * beta
        kbe = k * (beta * eg)
        gl = gc[c - 1:c, :]
        out.append((vb + bdot(n, vb, NN), kbe + bdot(n, kbe, NN), q * eg, k * jnp.exp(gl - gc), qk * decay, jnp.exp(gl)))
    return [out[ci * GDN_H:(ci + 1) * GDN_H] for ci in range(nc)]


GDN_CPS = 4


def _gdn_pre_specs(lay, t):
    xspec = lambda j: pl.BlockSpec((t, GDN_W), lambda n, j=j: (n, j))
    return [xspec(0), xspec(1), xspec(2), pl.BlockSpec((t, LANE), lambda n: (n, _div(lay.ab, LANE))),
            pl.BlockSpec((8, LANE), lambda n: (0, 0))]


def gdn_pre_fwd(xc, p_all, gp, lay, name):
    s = xc.shape[0]
    c = GDN_C
    n = _div(s, c)
    cps = _tile(n, GDN_CPS)
    t = cps * c

    def body(xq, xk, xv, ab, gp_ref, u_ref, w_ref, qd_ref, kd_ref, qk_ref, gl_ref):
        lane = lax.broadcasted_iota(jnp.int32, (1, LANE), 1)
        chunks = _gdn_chunk(xq[...], xk[...], xv[...], ab[...], gp_ref[...])
        for ci, heads in enumerate(chunks):
            rs = pl.ds(ci * c, c)
            gl_row = jnp.zeros((1, LANE), F32)
            for h, (u, w, qd, kd, qk, gl) in enumerate(heads):
                hs = pl.ds(h * GDN_D, GDN_D)
                u_ref[rs, hs] = u
                w_ref[rs, hs] = w.astype(BF16)
                qd_ref[rs, hs] = qd.astype(BF16)
                kd_ref[rs, hs] = kd.astype(BF16)
                qk_ref[rs, pl.ds(h * c, c)] = qk.astype(BF16)
                gl_row = gl_row + jnp.where(lane == h, gl, 0.0)
            gl_ref[ci] = gl_row

    row = pl.BlockSpec((t, GDN_W), lambda n: (n, 0))
    return pl.pallas_call(
        body, name=name, grid=(n // cps,), in_specs=_gdn_pre_specs(lay, t),
        out_specs=[row, row, row, row, pl.BlockSpec((t, GDN_H * c), lambda n: (n, 0)), pl.BlockSpec((cps, 1, LANE), lambda n: (n, 0, 0))],
        out_shape=[SDS((s, GDN_W), F32), SDS((s, GDN_W), BF16), SDS((s, GDN_W), BF16), SDS((s, GDN_W), BF16),
                   SDS((s, GDN_H * c), BF16), SDS((n, 1, LANE), F32)],
        compiler_params=_cp(("parallel",)))(xc, xc, xc, p_all, gp)


def gdn_pre_bwd(xc, p_all, gp, du, dw, dqd, dkd, dqk, dgl, lay, name):
    s = xc.shape[0]
    c = GDN_C
    n = _div(s, c)
    cps = _tile(n, GDN_CPS)
    t = cps * c
    chunk = functools.partial(_gdn_chunk, bdot=_bdot_vjp)

    def body(xq, xk, xv, ab, gp_ref, du_r, dw_r, dqd_r, dkd_r, dqk_r, dgl_r, dxc_ref, dab_ref, dgp_ref):
        lane = lax.broadcasted_iota(jnp.int32, (1, LANE), 1)
        _, vjp = jax.vjp(chunk, xq[...], xk[...], xv[...], ab[...], gp_ref[...])
        cts = []
        for ci in range(cps):
            rs = pl.ds(ci * c, c)
            heads = []
            for h in range(GDN_H):
                hs = pl.ds(h * GDN_D, GDN_D)
                dgl_h = jnp.sum(jnp.where(lane == h, dgl_r[ci], 0.0), axis=1, keepdims=True)
                heads.append((du_r[rs, hs], dw_r[rs, hs], dqd_r[rs, hs], dkd_r[rs, hs], dqk_r[rs, pl.ds(h * c, c)], dgl_h))
            cts.append(heads)
        dq, dk, dv, dab, dgp = vjp(cts)
        dxc_ref[:, pl.ds(0, GDN_W)] = dq
        dxc_ref[:, pl.ds(GDN_W, GDN_W)] = dk
        dxc_ref[:, pl.ds(2 * GDN_W, GDN_W)] = dv
        dab_ref[...] = dab.astype(BF16)

        @pl.when(pl.program_id(0) == 0)
        def _():
            dgp_ref[...] = jnp.zeros_like(dgp_ref)

        dgp_ref[...] += dgp

    row = pl.BlockSpec((t, GDN_W), lambda n: (n, 0))
    return pl.pallas_call(
        body, name=name, grid=(n // cps,),
        in_specs=_gdn_pre_specs(lay, t) + [row, row, row, row, pl.BlockSpec((t, GDN_H * c), lambda n: (n, 0)),
                                           pl.BlockSpec((cps, 1, LANE), lambda n: (n, 0, 0))],
        out_specs=[pl.BlockSpec((t, 3 * GDN_W), lambda n: (n, 0)), pl.BlockSpec((t, LANE), lambda n: (n, 0)),
                   pl.BlockSpec((8, LANE), lambda n: (0, 0))],
        out_shape=[SDS((s, 3 * GDN_W), F32), SDS((s, LANE), BF16), SDS((8, LANE), F32)],
        compiler_params=_cp(("arbitrary",)))(xc, xc, xc, p_all, gp, du, dw, dqd, dkd, dqk, dgl)


def _lane_scalar(row, h):
    lane = lax.broadcasted_iota(jnp.int32, row.shape, 1)
    return jnp.sum(jnp.where(lane == h, row, 0.0), axis=1, keepdims=True)


def gdn_scan_fwd(u, w, qd, kd, qk, gl, name):
    s = u.shape[0]
    c = GDN_C
    n = _div(s, c)

    def body(u_r, w_r, qd_r, kd_r, qk_r, gl_r, o_ref, s_ref, st):
        @pl.when(pl.program_id(0) == 0)
        def _():
            st[...] = jnp.zeros_like(st)

        s_ref[0] = st[...]
        for h in range(GDN_H):
            hs = pl.ds(h * GDN_D, GDN_D)
            sh = st[hs, :]
            shb = sh.astype(BF16)
            v_new = u_r[:, hs] - _dot(w_r[:, hs], shb, NN)
            vb = v_new.astype(BF16)
            o_ref[:, hs] = _dot(qd_r[:, hs], shb, NN) + _dot(qk_r[:, pl.ds(h * c, c)], vb, NN)
            st[hs, :] = sh * _lane_scalar(gl_r[0], h) + _dot(kd_r[:, hs], vb, TN)

    row = pl.BlockSpec((c, GDN_W), lambda i: (i, 0))
    return pl.pallas_call(
        body, name=name, grid=(n,),
        in_specs=[row, row, row, row, pl.BlockSpec((c, GDN_H * c), lambda i: (i, 0)), pl.BlockSpec((1, 1, LANE), lambda i: (i, 0, 0))],
        out_specs=[row, pl.BlockSpec((1, GDN_W, GDN_D), lambda i: (i, 0, 0))],
        out_shape=[SDS((s, GDN_W), F32), SDS((n, GDN_W, GDN_D), F32)],
        scratch_shapes=[pltpu.VMEM((GDN_W, GDN_D), F32)],
        compiler_params=_cp(("arbitrary",)))(u, w, qd, kd, qk, gl)


def gdn_scan_bwd(u, w, qd, kd, qk, gl, states, do, name):
    s = u.shape[0]
    c = GDN_C
    n = _div(s, c)

    def body(u_r, w_r, qd_r, kd_r, qk_r, gl_r, s_r, do_r, du_o, dw_o, dqd_o, dkd_o, dqk_o, dgl_o, dst):
        @pl.when(pl.program_id(0) == 0)
        def _():
            dst[...] = jnp.zeros_like(dst)

        lane = lax.broadcasted_iota(jnp.int32, (1, LANE), 1)
        dgl_row = jnp.zeros((1, LANE), F32)
        for h in range(GDN_H):
            hs = pl.ds(h * GDN_D, GDN_D)
            qs = pl.ds(h * c, c)
            sh = s_r[0, hs, :]
            shb = sh.astype(BF16)
            ds_out = dst[hs, :]
            dsb = ds_out.astype(BF16)
            dob = do_r[:, hs].astype(BF16)
            wv, qdv, kdv, qkv = w_r[:, hs], qd_r[:, hs], kd_r[:, hs], qk_r[:, qs]
            v_new = u_r[:, hs] - _dot(wv, shb, NN)
            vb = v_new.astype(BF16)
            dv = _dot(qkv, dob, TN) + _dot(kdv, dsb, NN)
            dvb = dv.astype(BF16)
            du_o[:, hs] = dv
            dw_o[:, hs] = -_dot(dvb, shb, NT)
            dqd_o[:, hs] = _dot(dob, shb, NT)
            dkd_o[:, hs] = _dot(vb, dsb, NT)
            dqk_o[:, qs] = _dot(dob, vb, NT)
            dgl_row = dgl_row + jnp.where(lane == h, jnp.sum(jnp.sum(ds_out * sh, axis=1, keepdims=True), axis=0, keepdims=True), 0.0)
            dst[hs, :] = ds_out * _lane_scalar(gl_r[0], h) + _dot(qdv, dob, TN) - _dot(wv, dvb, TN)
        dgl_o[0] = dgl_row

    rev = lambda i: n - 1 - i
    row = pl.BlockSpec((c, GDN_W), lambda i: (rev(i), 0))
    qks = pl.BlockSpec((c, GDN_H * c), lambda i: (rev(i), 0))
    gls = pl.BlockSpec((1, 1, LANE), lambda i: (rev(i), 0, 0))
    return pl.pallas_call(
        body, name=name, grid=(n,),
        in_specs=[row, row, row, row, qks, gls, pl.BlockSpec((1, GDN_W, GDN_D), lambda i: (rev(i), 0, 0)), row],
        out_specs=[row, row, row, row, qks, gls],
        out_shape=[SDS((s, GDN_W), F32)] * 4 + [SDS((s, GDN_H * c), F32), SDS((n, 1, LANE), F32)],
        scratch_shapes=[pltpu.VMEM((GDN_W, GDN_D), F32)],
        compiler_params=_cp(("arbitrary",)))(u, w, qd, kd, qk, gl, states, do)


def _gdn_out_rows(o, z, nw):
    outs = []
    for h in range(GDN_H):
        hs = slice(h * GDN_D, (h + 1) * GDN_D)
        oh = o[:, hs]
        y = oh * lax.rsqrt(jnp.mean(oh * oh, axis=-1, keepdims=True) + RMS_EPS) * nw
        outs.append(y * _silu(z[:, hs]))
    return jnp.concatenate(outs, axis=1)


def gdn_out(o, p_all, nw, dy, lay, name):
    s = o.shape[0]
    tm = _tile(s, 512)
    bwd = dy is not None

    def body(*refs):
        o_r, z_r, nw_r = refs[:3]
        if not bwd:
            refs[3][...] = _gdn_out_rows(o_r[...], z_r[...], nw_r[...]).astype(BF16)
            return
        dy_r, do_o, dz_o, dnw_o = refs[3:]
        _, vjp = jax.vjp(_gdn_out_rows, o_r[...], z_r[...], nw_r[...])
        d_o, d_z, d_nw = vjp(dy_r[...].astype(F32))
        do_o[...] = d_o
        dz_o[...] = d_z.astype(BF16)

        @pl.when(pl.program_id(0) == 0)
        def _():
            dnw_o[...] = jnp.zeros_like(dnw_o)

        dnw_o[...] += d_nw

    row = pl.BlockSpec((tm, GDN_W), lambda i: (i, 0))
    zs = pl.BlockSpec((tm, GDN_W), lambda i: (i, _div(lay.z, GDN_W)))
    nws = pl.BlockSpec((1, GDN_D), lambda i: (0, 0))
    if not bwd:
        return pl.pallas_call(body, name=name, grid=(s // tm,), in_specs=[row, zs, nws], out_specs=row,
                              out_shape=SDS((s, GDN_W), BF16), compiler_params=_cp(("parallel",)))(o, p_all, nw)
    return pl.pallas_call(body, name=name, grid=(s // tm,), in_specs=[row, zs, nws, row], out_specs=[row, row, nws],
                          out_shape=[SDS((s, GDN_W), F32), SDS((s, GDN_W), BF16), SDS((1, GDN_D), F32)],
                          compiler_params=_cp(("arbitrary",)))(o, p_all, nw, dy)


def _cols_to_full(g):
    n, k, c = g.shape
    return g.transpose(1, 0, 2).reshape(k, n * c)


def _full_to_cols(w):
    k, nc = w.shape
    return w.reshape(k, N_DEV, nc // N_DEV).transpose(1, 0, 2)


def _rows_to_blocks(w):
    return w.reshape(N_DEV, w.shape[0] // N_DEV, w.shape[1])


def _pack_small(parts, rows):
    flat = jnp.concatenate([jnp.pad(p.reshape(-1), (0, -p.size % LANE)) for p in parts])
    return jnp.pad(flat, (0, rows * LANE - flat.size)).reshape(rows, LANE)


def kernel(x, mem, g_mix, w_in, sinks, conv_w, a_log, dt_bias, gdn_norm_w, g_mem, w_mem_kv, w_swa_up, w_gdn_up, w_xa_up, w_out, g_mlp, w_mlp_in, w_mlp_out, g_final, loss_target, m_g_mix, m_w_in, m_sinks, m_conv_w, m_a_log, m_dt_bias, m_gdn_norm_w, m_g_mem, m_w_mem_kv, m_w_swa_up, m_w_gdn_up, m_w_xa_up, m_w_out, m_g_mlp, m_w_mlp_in, m_w_mlp_out, m_g_final, v_g_mix, v_w_in, v_sinks, v_conv_w, v_a_log, v_dt_bias, v_gdn_norm_w, v_g_mem, v_w_mem_kv, v_w_swa_up, v_w_gdn_up, v_w_xa_up, v_w_out, v_g_mlp, v_w_mlp_in, v_w_mlp_out, v_g_final):
    xs, ms, tgt = x[0], mem[0], loss_target[0]
    s, d = xs.shape
    lay = Layout(d)
    px, py, pc = _position()
    dev = 4 * px + 2 * py + pc

    big = [w_in[0], w_mem_kv[0], w_swa_up[0], w_gdn_up[0], w_xa_up[0], w_out[0], w_mlp_in[0], w_mlp_out[0]]
    gathered = all_gather([w.astype(BF16) for w in big] + [conv_w[0]], "gather_weights")
    W_in = lay.pad_weight(_cols_to_full(gathered[0]))
    W_mkv = gathered[1].reshape(-1, gathered[1].shape[2])
    W_sup, W_gup, W_xup = (_cols_to_full(g) for g in gathered[2:5])
    W_out = gathered[5].reshape(-1, d)
    W_m1 = _cols_to_full(gathered[6])
    W_m2 = gathered[7].reshape(-1, d)
    convw = _cols_to_full(gathered[8])
    gp = jnp.zeros((8, LANE), F32).at[0, :GDN_H].set(a_log[0]).at[1, :GDN_H].set(dt_bias[0])

    n1 = rmsnorm_fwd(xs, g_mix, "norm_mix")
    p_all = matmul(n1, W_in, mode="nn", out_dtype=F32, name="proj_in", tm=2048, tn=512, tk=d)
    y_a = swa_fwd(p_all, sinks, lay, "swa_fwd")
    xc = gdn_conv_fwd(p_all, convw, lay, "gdn_conv_fwd")
    u, gw, gqd, gkd, gqk, ggl = gdn_pre_fwd(xc, p_all, gp, lay, "gdn_pre_fwd")
    o_b, states = gdn_scan_fwd(u, gw, gqd, gkd, gqk, ggl, "gdn_scan_fwd")
    y_b = gdn_out(o_b, p_all, gdn_norm_w, None, lay, "gdn_out_fwd")
    nm = rmsnorm_fwd(ms, g_mem, "norm_mem")
    mkv = matmul(nm, W_mkv, mode="nn", out_dtype=BF16, name="proj_mem", tk=d)
    y_c = xattn_fwd(p_all, mkv, lay, "xattn_fwd")
    merged = merge(p_all, y_a, y_b, y_c, W_sup, W_gup, W_xup, None, lay, "merge_fwd")
    h1 = matmul(merged, W_out, mode="nn", out_dtype=F32, name="proj_out", tm=2048, tn=512, tk=d, resid=xs)
    n2 = rmsnorm_fwd(h1, g_mlp, "norm_mlp")
    uu = matmul(n2, W_m1, mode="nn", out_dtype=F32, name="mlp_in", tm=2048, tn=512, tk=d)
    h2 = matmul(uu, W_m2, mode="nn", out_dtype=F32, name="mlp_out", tm=1024, tn=2048, tk=512, a_relu2=True, resid=h1)
    dh2, dg_final, lrow = final_norm_loss(h2, g_final.reshape(1, d), tgt, "final_loss")
    loss = lax.psum(lrow[0, 0], ("x", "y", "c"))

    du = matmul(dh2, W_m2, mode="nt", out_dtype=BF16, name="mlp_out_dx", tm=2048, tn=512, tk=d, relu2_grad_of=uu)
    dW_m2 = matmul(uu, dh2, mode="tn", out_dtype=BF16, name="mlp_out_dw", tm=1024, tn=2048, tk=512, a_relu2=True)
    dn2 = matmul(du, W_m1, mode="nt", out_dtype=F32, name="mlp_in_dx", tm=1024, tn=2048, tk=1024)
    dW_m1 = matmul(n2, du, mode="tn", out_dtype=BF16, name="mlp_in_dw", tm=2048, tn=1024, tk=512)
    dh1, dg_mlp = rmsnorm_bwd(h1, g_mlp, dn2, dh2, "norm_mlp_bwd")

    dmerged = matmul(dh1, W_out, mode="nt", out_dtype=F32, name="proj_out_dx", tm=2048, tn=512, tk=d)
    dW_out = matmul(merged, dh1, mode="tn", out_dtype=BF16, name="proj_out_dw", tm=2048, tn=1024, tk=512)
    dga, dgb, dgc, dta, dtb, dtc = merge(p_all, y_a, y_b, y_c, W_sup, W_gup, W_xup, dmerged, lay, "merge_bwd")
    dy_a = matmul(dta, W_sup, mode="nt", out_dtype=BF16, name="swa_up_dx", tk=d)
    dy_b = matmul(dtb, W_gup, mode="nt", out_dtype=BF16, name="gdn_up_dx", tk=d)
    dy_c = matmul(dtc, W_xup, mode="nt", out_dtype=BF16, name="xa_up_dx", tk=d)
    dW_sup = matmul(y_a, dta, mode="tn", out_dtype=BF16, name="swa_up_dw", tn=2048)
    dW_gup = matmul(y_b, dtb, mode="tn", out_dtype=BF16, name="gdn_up_dw", tn=2048)
    dW_xup = matmul(y_c, dtc, mode="tn", out_dtype=BF16, name="xa_up_dw", tn=2048)

    dq_a, dk_a, dv_a, dsinks = swa_bwd(p_all, sinks, dy_a, lay, "swa_bwd")
    dq_c, dmkv = xattn_bwd(p_all, mkv, dy_c, lay, "xattn_bwd")
    dW_mkv = matmul(nm, dmkv, mode="tn", out_dtype=BF16, name="proj_mem_dw", tk=256)
    dnm = matmul(dmkv, W_mkv, mode="nt", out_dtype=F32, name="proj_mem_dx", tk=1024)
    _, dg_mem = rmsnorm_bwd(ms, g_mem, dnm, None, "norm_mem_bwd")

    do_b, dz, dnorm_w = gdn_out(o_b, p_all, gdn_norm_w, dy_b, lay, "gdn_out_bwd")
    du_g, dw_g, dqd_g, dkd_g, dqk_g, dgl_g = gdn_scan_bwd(u, gw, gqd, gkd, gqk, ggl, states, do_b, "gdn_scan_bwd")
    dxc, dab, dgp = gdn_pre_bwd(xc, p_all, gp, du_g, dw_g, dqd_g, dkd_g, dqk_g, dgl_g, lay, "gdn_pre_bwd")
    dqkv, dconv = gdn_conv_bwd(p_all, convw, dxc, lay, "gdn_conv_bwd")

    dp_all = jnp.concatenate([dga, dgb, dgc, dq_a, dqkv, dz, dq_c, dk_a, dv_a, dab,
                              jnp.zeros((s, lay.pw - lay.end), BF16)], axis=1)
    dn1 = matmul(dp_all, W_in, mode="nt", out_dtype=F32, name="proj_in_dx", tm=1024, tn=2048, tk=1024)
    dW_in = matmul(n1, dp_all, mode="tn", out_dtype=BF16, name="proj_in_dw", tm=2048, tn=1024, tk=512)
    grad_x, dg_mix = rmsnorm_bwd(xs, g_mix, dn1, dh1, "norm_mix_bwd")

    blocks = [_full_to_cols(lay.unpad_weight(dW_in)), _rows_to_blocks(dW_mkv), _full_to_cols(dW_sup), _full_to_cols(dW_gup),
              _full_to_cols(dW_xup), _rows_to_blocks(dW_out), _full_to_cols(dW_m1), _rows_to_blocks(dW_m2)]
    split = [b.reshape((4, 2) + b.shape[1:]) for b in blocks]
    own = [lax.dynamic_index_in_dim(b, pc, axis=1, keepdims=False) for b in split]
    sib = [lax.dynamic_index_in_dim(b, 1 - pc, axis=1, keepdims=False) for b in split]
    from_sib = pair_exchange(sib, "grads_pair_exchange")
    chip_sum = [add_bf16(a, b, "grads_pair_add_%d" % i) for i, (a, b) in enumerate(zip(own, from_sib))]
    parts = chip_exchange(chip_sum, "grads_chip_exchange")

    shard_names = [(w_in, m_w_in, v_w_in), (w_mem_kv, m_w_mem_kv, v_w_mem_kv), (w_swa_up, m_w_swa_up, v_w_swa_up),
                   (w_gdn_up, m_w_gdn_up, v_w_gdn_up), (w_xa_up, m_w_xa_up, v_w_xa_up), (w_out, m_w_out, v_w_out),
                   (w_mlp_in, m_w_mlp_in, v_w_mlp_in), (w_mlp_out, m_w_mlp_out, v_w_mlp_out)]
    big_res = [adamw(p, w[0], m[0], v[0], "adamw_%d" % i) for i, (p, (w, m, v)) in enumerate(zip(parts, shard_names))]

    smalls = [(g_mix, m_g_mix, v_g_mix, dg_mix), (sinks, m_sinks, v_sinks, dsinks[:, :SWA_HQ]),
              (a_log, m_a_log, v_a_log, dgp[0:1, :GDN_H]), (dt_bias, m_dt_bias, v_dt_bias, dgp[1:2, :GDN_H]),
              (gdn_norm_w, m_gdn_norm_w, v_gdn_norm_w, dnorm_w), (g_mem, m_g_mem, v_g_mem, dg_mem),
              (g_mlp, m_g_mlp, v_g_mlp, dg_mlp), (g_final, m_g_final, v_g_final, dg_final)]
    sizes = [-(-t[0].size // LANE) * LANE for t in smalls] + [GDN_CONV * 3 * GDN_W]
    rows = -(-sum(sizes) // (8 * LANE)) * 8
    csh = conv_w.shape[2]

    def conv_place(a):
        return lax.dynamic_update_slice(jnp.zeros((GDN_CONV, 3 * GDN_W), F32), a[0], (0, dev * csh))

    g_pack = _pack_small([t[3] for t in smalls] + [dconv], rows)
    w_pack = _pack_small([t[0] for t in smalls] + [conv_place(conv_w)], rows)
    m_pack = _pack_small([t[1] for t in smalls] + [conv_place(m_conv_w)], rows)
    v_pack = _pack_small([t[2] for t in smalls] + [conv_place(v_conv_w)], rows)
    g_all = all_gather([g_pack], "gather_small_grads")[0]
    small_res = adamw(g_all, w_pack, m_pack, v_pack, "adamw_small")

    def unpack(arr):
        flat = arr.reshape(-1)
        outs, off = [], 0
        for t, sz in zip(smalls, sizes[:-1]):
            outs.append(flat[off:off + t[0].size].reshape(t[0].shape))
            off += sz
        cw = flat[off:off + sizes[-1]].reshape(GDN_CONV, 3 * GDN_W)
        outs.append(lax.dynamic_slice(cw, (0, dev * csh), (GDN_CONV, csh))[None])
        return outs

    sg, sd, sm, sv = (unpack(a) for a in small_res)
    bg, bd, bm, bv = ([r[i][None] for r in big_res] for i in range(4))

    def ordered(sm_, bg_):
        return [sm_[0], bg_[0], sm_[1], sm_[8], sm_[2], sm_[3], sm_[4], sm_[5], bg_[1], bg_[2], bg_[3], bg_[4], bg_[5],
                sm_[6], bg_[6], bg_[7], sm_[7]]

    return (loss, grad_x[None], *ordered(sg, bg), *ordered(sd, bd), *ordered(sm, bm), *ordered(sv, bv))
```

```python
import functools
import math

import jax
import jax.numpy as jnp
from jax import lax
from jax.experimental import pallas as pl
from jax.experimental.pallas import tpu as pltpu

F32, BF16 = jnp.float32, jnp.bfloat16
SDS = jax.ShapeDtypeStruct
MESH = pl.DeviceIdType.MESH
ANY = pl.BlockSpec(memory_space=pl.ANY)

SWA_HQ, SWA_HKV, SWA_HD, SWA_W = 16, 2, 64, 128
SWA_G = SWA_HQ // SWA_HKV
GDN_H, GDN_D, GDN_CONV, GDN_C = 4, 128, 4, 64
XA_H, XA_D = 4, 128
Q_W = SWA_HQ * SWA_HD
KV_W = SWA_HKV * SWA_HD
GDN_W = GDN_H * GDN_D
XA_W = XA_H * XA_D
RMS_EPS = 1e-6
L2_EPS = 1e-6
NEG = -1e30
N_DEV = 8
LANE = 128

ADAM_LR, ADAM_B1, ADAM_B2, ADAM_EPS, ADAM_WD, ADAM_STEP = 0.001, 0.9, 0.999, 1e-08, 0.01, 10

VMEM_BIG = 56 * 1024 * 1024


def _cp(sem, vmem=VMEM_BIG):
    return pltpu.CompilerParams(dimension_semantics=sem, vmem_limit_bytes=vmem)


def _div(a, b):
    assert a % b == 0, (a, b)
    return a // b


def _tile(n, t):
    t = min(t, n)
    assert n % t == 0, (n, t)
    return t


def _sigmoid(x):
    return 1.0 / (1.0 + jnp.exp(-x))


def _silu(x):
    return x * _sigmoid(x)


def _softplus(x):
    return jnp.maximum(x, 0.0) + jnp.log1p(jnp.exp(-jnp.abs(x)))


def _dot(a, b, dims, prec=None):
    return lax.dot_general(a, b, (dims, ((), ())), precision=prec, preferred_element_type=F32)


NN = ((1,), (0,))
NT = ((1,), (1,))
TN = ((0,), (0,))
HI = lax.Precision.HIGHEST


def _bdot_plain(a, b, dims):
    return _dot(a.astype(BF16), b.astype(BF16), dims)


@functools.partial(jax.custom_vjp, nondiff_argnums=(2,))
def _bdot_vjp(a, b, dims):
    return _bdot_plain(a, b, dims)


def _bdot_vjp_fwd(a, b, dims):
    return _bdot_plain(a, b, dims), (a, b)


def _bdot_vjp_bwd(dims, res, ct):
    a, b = res
    if dims == NN:
        return _bdot_plain(ct, b, NT), _bdot_plain(a, ct, TN)
    assert dims == NT, dims
    return _bdot_plain(ct, b, NN), _bdot_plain(ct, a, TN)


_bdot_vjp.defvjp(_bdot_vjp_fwd, _bdot_vjp_bwd)


class Layout:
    def __init__(self, d):
        self.d = d
        self.g = 0
        self.q = 3 * d
        self.qkv = self.q + Q_W
        self.z = self.qkv + 3 * GDN_W
        self.qc = self.z + GDN_W
        self.k = self.qc + XA_W
        self.v = self.k + KV_W
        self.ab = self.v + KV_W
        self.end = self.ab + LANE
        self.pw = -(-self.end // 1024) * 1024
        self.lq, self.lk, self.lv, self.lqkv = 0, Q_W, Q_W + KV_W, Q_W + 2 * KV_W
        self.la = self.lqkv + 3 * GDN_W
        self.lz = self.la + 2 * GDN_H
        self.lqc = self.lz + GDN_W
        self.lg = self.lqc + XA_W
        self.lw = self.lg + 3 * d

    def pieces(self):
        segs = [(self.lq, self.lk, self.q), (self.lk, self.lv, self.k), (self.lv, self.lqkv, self.v),
                (self.lqkv, self.la, self.qkv), (self.la, self.lz, self.ab), (self.lz, self.lqc, self.z),
                (self.lqc, self.lg, self.qc), (self.lg, self.lw, self.g)]
        cw = _div(self.lw, N_DEV)
        out = []
        for dev in range(N_DEV):
            lo, hi = dev * cw, (dev + 1) * cw
            for ls, le, ps in segs:
                s, e = max(lo, ls), min(hi, le)
                if s < e:
                    out.append((dev, s - lo, ps + s - ls, e - s))
        return out


def pad_w_in(g, lay):
    nd, k, cw = g.shape
    tr = _tile(k, 256)
    tail = lay.ab + 2 * GDN_H

    def body(g_ref, o_ref):
        o_ref[:, pl.ds(tail, lay.pw - tail)] = jnp.zeros((tr, lay.pw - tail), o_ref.dtype)
        for dev, so, po, ln in lay.pieces():
            o_ref[:, pl.ds(po, ln)] = g_ref[dev, :, pl.ds(so, ln)]

    return pl.pallas_call(
        body, name="pad_w_in", grid=(k // tr,), in_specs=[pl.BlockSpec((nd, tr, cw), lambda i: (0, i, 0))],
        out_specs=pl.BlockSpec((tr, lay.pw), lambda i: (i, 0)), out_shape=SDS((k, lay.pw), g.dtype),
        compiler_params=_cp(("parallel",)))(g)


def unpad_dw_in(dw, lay):
    k = dw.shape[0]
    cw = _div(lay.lw, N_DEV)
    tr = _tile(k, 256)

    def body(d_ref, o_ref):
        for dev, so, po, ln in lay.pieces():
            o_ref[dev, :, pl.ds(so, ln)] = d_ref[:, pl.ds(po, ln)]

    return pl.pallas_call(
        body, name="unpad_dw_in", grid=(k // tr,), in_specs=[pl.BlockSpec((tr, lay.pw), lambda i: (i, 0))],
        out_specs=pl.BlockSpec((N_DEV, tr, cw), lambda i: (0, i, 0)), out_shape=SDS((N_DEV, k, cw), dw.dtype),
        compiler_params=_cp(("parallel",)))(dw)


def _position():
    return lax.axis_index("x"), lax.axis_index("y"), lax.axis_index("c")


def all_gather(arrs, name):
    n = len(arrs)

    def body(*refs):
        ins, outs = refs[:n], refs[n:2 * n]
        send_sems, recv_sems, local_sems = refs[2 * n:]
        x, y, c = _position()
        me, sibling = (x, y, c), (x, y, 1 - c)
        chips = [(1 - x, y), (x, 1 - y), (1 - x, 1 - y)]

        def blk(o, p):
            return o.at[4 * p[0] + 2 * p[1] + p[2]]

        def copy(i, k, block, to, src=None):
            return pltpu.make_async_remote_copy(
                src_ref=blk(outs[i], block) if src is None else src, dst_ref=blk(outs[i], block),
                send_sem=send_sems.at[i, k], recv_sem=recv_sems.at[i, k], device_id=to, device_id_type=MESH)

        started = []
        for i in range(n):
            mine = pltpu.make_async_copy(ins[i], blk(outs[i], me), local_sems.at[i])
            mine.start()
            first = [copy(i, 0, me, sibling, src=ins[i])]
            first += [copy(i, 1 + j, me, (*chip, c), src=ins[i]) for j, chip in enumerate(chips)]
            for cp in first:
                cp.start()
            started += [mine.wait] + [cp.wait_send for cp in first]
        for i in range(n):
            for j, chip in enumerate(chips):
                copy(i, 1 + j, (*chip, c), me).wait_recv()
                fwd = copy(i, 4 + j, (*chip, c), sibling)
                fwd.start()
                started.append(fwd.wait_send)
        for i in range(n):
            copy(i, 0, sibling, me).wait_recv()
            for j, chip in enumerate(chips):
                copy(i, 4 + j, (*chip, 1 - c), me).wait_recv()
        for w in started:
            w()

    return pl.pallas_call(
        body, name=name,
        out_shape=[SDS((N_DEV,) + a.shape, a.dtype) for a in arrs],
        in_specs=[ANY] * n, out_specs=[ANY] * n,
        scratch_shapes=[pltpu.SemaphoreType.DMA((n, 7)), pltpu.SemaphoreType.DMA((n, 7)), pltpu.SemaphoreType.DMA((n,))],
    )(*arrs)


class GatherJob:
    def __init__(self, arrs):
        self.ins = list(arrs)
        n = len(arrs)
        self.out_shapes = [SDS((N_DEV,) + a.shape, a.dtype) for a in arrs]
        self.scratch = [pltpu.SemaphoreType.DMA((n, 7)), pltpu.SemaphoreType.DMA((n, 7)), pltpu.SemaphoreType.DMA((n,))]

    def _ctx(self, outs, sems):
        send_sems, recv_sems, _ = sems
        x, y, c = _position()

        def blk(o, p):
            return o.at[4 * p[0] + 2 * p[1] + p[2]]

        def copy(i, k, block, to, src=None):
            return pltpu.make_async_remote_copy(
                src_ref=blk(outs[i], block) if src is None else src, dst_ref=blk(outs[i], block),
                send_sem=send_sems.at[i, k], recv_sem=recv_sems.at[i, k], device_id=to, device_id_type=MESH)

        return (x, y, c), (x, y, 1 - c), [(1 - x, y), (x, 1 - y), (1 - x, 1 - y)], blk, copy

    def start(self, ins, outs, sems):
        me, sibling, chips, blk, copy = self._ctx(outs, sems)
        for i in range(len(ins)):
            pltpu.make_async_copy(ins[i], blk(outs[i], me), sems[2].at[i]).start()
            copy(i, 0, me, sibling, src=ins[i]).start()
            for j, chip in enumerate(chips):
                copy(i, 1 + j, me, (*chip, me[2]), src=ins[i]).start()

    def mid(self, ins, outs, sems):
        me, sibling, chips, blk, copy = self._ctx(outs, sems)
        for i in range(len(ins)):
            for j, chip in enumerate(chips):
                copy(i, 1 + j, (*chip, me[2]), me).wait_recv()
                copy(i, 4 + j, (*chip, me[2]), sibling).start()

    def finish(self, ins, outs, sems):
        me, sibling, chips, blk, copy = self._ctx(outs, sems)
        for i in range(len(ins)):
            copy(i, 0, sibling, me).wait_recv()
            for j, chip in enumerate(chips):
                copy(i, 4 + j, (*chip, 1 - me[2]), me).wait_recv()
        for i in range(len(ins)):
            pltpu.make_async_copy(ins[i], blk(outs[i], me), sems[2].at[i]).wait()
            copy(i, 0, me, sibling, src=ins[i]).wait_send()
            for j, chip in enumerate(chips):
                copy(i, 1 + j, me, (*chip, me[2]), src=ins[i]).wait_send()
                copy(i, 4 + j, (*chip, me[2]), sibling).wait_send()


class ChipExchangeJob:
    mid = None

    def __init__(self, arrs):
        self.ins = list(arrs)
        n = len(arrs)
        self.out_shapes = [SDS(a.shape, a.dtype) for a in arrs]
        self.scratch = [pltpu.SemaphoreType.DMA((n, 3)), pltpu.SemaphoreType.DMA((n, 3)), pltpu.SemaphoreType.DMA((n,))]

    def _copies(self, ins, outs, sems, i, arrivals):
        send_sems, recv_sems, local_sems = sems
        x, y, c = _position()
        my_chip = 2 * x + y
        chips = [(1 - x, y), (x, 1 - y), (1 - x, 1 - y)]
        if arrivals:
            return [pltpu.make_async_remote_copy(
                src_ref=ins[i].at[my_chip], dst_ref=outs[i].at[2 * px + py], send_sem=send_sems.at[i, k],
                recv_sem=recv_sems.at[i, k], device_id=(px, py, c), device_id_type=MESH) for k, (px, py) in enumerate(chips)]
        local = pltpu.make_async_copy(ins[i].at[my_chip], outs[i].at[my_chip], local_sems.at[i])
        return local, [pltpu.make_async_remote_copy(
            src_ref=ins[i].at[2 * px + py], dst_ref=outs[i].at[my_chip], send_sem=send_sems.at[i, k],
            recv_sem=recv_sems.at[i, k], device_id=(px, py, c), device_id_type=MESH) for k, (px, py) in enumerate(chips)]

    def start(self, ins, outs, sems):
        for i in range(len(ins)):
            local, remote = self._copies(ins, outs, sems, i, False)
            local.start()
            for cp in remote:
                cp.start()

    def finish(self, ins, outs, sems):
        for i in range(len(ins)):
            for cp in self._copies(ins, outs, sems, i, True):
                cp.wait_recv()
            local, remote = self._copies(ins, outs, sems, i, False)
            for cp in remote:
                cp.wait_send()
            local.wait()


def pair_exchange(blocks, name):
    n = len(blocks)

    def body(*refs):
        ins, outs = refs[:n], refs[n:2 * n]
        send_sems, recv_sems = refs[2 * n:]
        x, y, c = _position()
        copies = [pltpu.make_async_remote_copy(src_ref=ins[i].at[2 * j + 1 - c], dst_ref=outs[i].at[j], send_sem=send_sems.at[i, j],
                                               recv_sem=recv_sems.at[i, j], device_id=(x, y, 1 - c), device_id_type=MESH)
                  for i in range(n) for j in range(4)]
        for cp in copies:
            cp.start()
        for cp in copies:
            cp.wait()

    return pl.pallas_call(
        body, name=name, out_shape=[SDS((4,) + a.shape[1:], a.dtype) for a in blocks],
        in_specs=[ANY] * n, out_specs=[ANY] * n,
        scratch_shapes=[pltpu.SemaphoreType.DMA((n, 4)), pltpu.SemaphoreType.DMA((n, 4))],
    )(*blocks)


def pair_add(blocks, other, name):
    _, r, c = blocks.shape
    tr = _tile(r, 256)
    parity = lax.axis_index("c").astype(jnp.int32).reshape(1)

    def body(par_ref, a_ref, b_ref, o_ref):
        o_ref[...] = (a_ref[...].astype(F32) + b_ref[...].astype(F32)).astype(BF16)

    spec = pl.BlockSpec((1, tr, c), lambda j, i, par: (j, i, 0))
    own = pl.BlockSpec((1, tr, c), lambda j, i, par: (2 * j + par[0], i, 0))
    return pl.pallas_call(
        body, name=name, out_shape=SDS(other.shape, BF16),
        grid_spec=pltpu.PrefetchScalarGridSpec(num_scalar_prefetch=1, grid=(4, r // tr), in_specs=[own, spec], out_specs=spec),
        compiler_params=_cp(("parallel", "parallel")))(parity, blocks, other)


def adamw(parts, w, m, v, name):
    p, r, c = parts.shape
    tr = _tile(r, 128 if c > 1024 else 256)

    def body(p_ref, w_ref, m_ref, v_ref, g_out, d_out, m_out, v_out):
        g = p_ref[0].astype(F32)
        for j in range(1, p):
            g = g + p_ref[j].astype(F32)
        mn = ADAM_B1 * m_ref[...] + (1.0 - ADAM_B1) * g
        vn = ADAM_B2 * v_ref[...] + (1.0 - ADAM_B2) * jnp.square(g)
        m_hat = mn / (1.0 - ADAM_B1 ** ADAM_STEP)
        v_hat = vn / (1.0 - ADAM_B2 ** ADAM_STEP)
        g_out[...] = g
        d_out[...] = -ADAM_LR * (m_hat / (jnp.sqrt(v_hat) + ADAM_EPS) + ADAM_WD * w_ref[...])
        m_out[...] = mn
        v_out[...] = vn

    spec = pl.BlockSpec((tr, c), lambda i: (i, 0))
    return pl.pallas_call(
        body, name=name, grid=(r // tr,),
        in_specs=[pl.BlockSpec((p, tr, c), lambda i: (0, i, 0)), spec, spec, spec],
        out_specs=[spec] * 4, out_shape=[SDS((r, c), F32)] * 4, compiler_params=_cp(("parallel",)))(parts, w, m, v)


def matmul(a, b, *, mode, out_dtype, name, tm=1024, tn=1024, tk=512, a_relu2=False, resid=None, relu2_grad_of=None,
           b_cols=False, out_cols=False, side=None):
    if b_cols:
        nb, brows, bc = b.shape
        bshape = (brows, nb * bc)
    else:
        bshape = b.shape
    if mode == "nn":
        (m, k), (k2, n) = a.shape, bshape
    elif mode == "nt":
        (m, k), (n, k2) = a.shape, bshape
    else:
        (k, m), (k2, n) = a.shape, bshape
    assert k == k2, (a.shape, b.shape, mode)
    tm, tn, tk = _tile(m, tm), _tile(n, tn), _tile(k, tk)
    if b_cols and mode == "nn":
        tn = _tile(bc, tn)
    if b_cols and mode == "nt":
        tk = _tile(bc, tk)
    if out_cols:
        oc = _div(n, N_DEV)
        tn = _tile(oc, tn)
    nk = k // tk
    ni, nj = m // tm, n // tn
    dims = {"nn": NN, "nt": NT, "tn": TN}[mode]
    extras = [e for e in (resid, relu2_grad_of) if e is not None]
    n_side = len(side.ins) if side is not None else 0

    def body(*refs):
        a_ref, b_ref = refs[:2]
        e_refs = refs[2:2 + len(extras)]
        o_ref = refs[2 + len(extras) + n_side]
        acc_ref = refs[3 + len(extras) + 2 * n_side] if nk > 1 else None
        if side is not None:
            s_ins = refs[2 + len(extras):2 + len(extras) + n_side]
            s_outs = refs[3 + len(extras) + n_side:3 + len(extras) + 2 * n_side]
            s_sems = refs[len(refs) - len(side.scratch):]
            step = (pl.program_id(0) * nj + pl.program_id(1)) * nk + pl.program_id(2)
            pl.when(step == 0)(lambda: side.start(s_ins, s_outs, s_sems))
            if side.mid is not None:
                pl.when(step == (ni * nj * nk) // 2)(lambda: side.mid(s_ins, s_outs, s_sems))

        def operands():
            av = a_ref[...]
            if a_relu2:
                av = jnp.square(jnp.maximum(av.astype(F32), 0.0))
            return av.astype(BF16), b_ref[...].astype(BF16)

        def finish(r):
            e = list(e_refs)
            if resid is not None:
                r = r + e.pop(0)[...]
            if relu2_grad_of is not None:
                r = r * (2.0 * jnp.maximum(e.pop(0)[...], 0.0))
            o_ref[...] = r.astype(out_dtype)

        if nk == 1:
            av, bv = operands()
            finish(_dot(av, bv, dims))
        else:
            kk = pl.program_id(2)

            @pl.when(kk == 0)
            def _():
                acc_ref[...] = jnp.zeros_like(acc_ref)

            av, bv = operands()
            acc_ref[...] += _dot(av, bv, dims)

            @pl.when(kk == nk - 1)
            def _():
                finish(acc_ref[...])

        if side is not None:
            pl.when(step == ni * nj * nk - 1)(lambda: side.finish(s_ins, s_outs, s_sems))

    a_spec = {"nn": pl.BlockSpec((tm, tk), lambda i, j, kk: (i, kk)),
              "nt": pl.BlockSpec((tm, tk), lambda i, j, kk: (i, kk)),
              "tn": pl.BlockSpec((tk, tm), lambda i, j, kk: (kk, i))}[mode]
    if not b_cols:
        b_spec = {"nn": pl.BlockSpec((tk, tn), lambda i, j, kk: (kk, j)),
                  "nt": pl.BlockSpec((tn, tk), lambda i, j, kk: (j, kk)),
                  "tn": pl.BlockSpec((tk, tn), lambda i, j, kk: (kk, j))}[mode]
    elif mode == "nn":
        per = bc // tn
        b_spec = pl.BlockSpec((None, tk, tn), lambda i, j, kk: (j // per, kk, j % per))
    else:
        assert mode == "nt", mode
        per = bc // tk
        b_spec = pl.BlockSpec((None, tn, tk), lambda i, j, kk: (kk // per, j, kk % per))
    e_spec = pl.BlockSpec((tm, tn), lambda i, j, kk: (i, j))
    if out_cols:
        pero = oc // tn
        o_spec = pl.BlockSpec((None, tm, tn), lambda i, j, kk: (j // pero, i, j % pero))
        o_shape = SDS((N_DEV, m, oc), out_dtype)
    else:
        o_spec, o_shape = e_spec, SDS((m, n), out_dtype)
    res = pl.pallas_call(
        body, name=name, grid=(ni, nj, nk),
        in_specs=[a_spec, b_spec] + [e_spec] * len(extras) + [ANY] * n_side,
        out_specs=[o_spec] + [ANY] * n_side, out_shape=[o_shape] + (side.out_shapes if side is not None else []),
        scratch_shapes=([pltpu.VMEM((tm, tn), F32)] if nk > 1 else []) + (side.scratch if side is not None else []),
        compiler_params=_cp(("arbitrary", "arbitrary", "arbitrary")))(a, b, *extras, *(side.ins if side is not None else []))
    return res if side is not None else res[0]


def rmsnorm_fwd(x, g, name):
    s, d = x.shape
    tm = _tile(s, 256)

    def body(x_ref, g_ref, o_ref):
        xv = x_ref[...]
        r = lax.rsqrt(jnp.mean(xv * xv, axis=-1, keepdims=True) + RMS_EPS)
        o_ref[...] = (xv * r * g_ref[...]).astype(BF16)

    row = pl.BlockSpec((tm, d), lambda i: (i, 0))
    return pl.pallas_call(body, name=name, grid=(s // tm,), in_specs=[row, pl.BlockSpec((1, d), lambda i: (0, 0))],
                          out_specs=row, out_shape=SDS((s, d), BF16), compiler_params=_cp(("parallel",)))(x, g)


def _rms_bwd_rows(xv, gv, dy):
    r = lax.rsqrt(jnp.mean(xv * xv, axis=-1, keepdims=True) + RMS_EPS)
    xh = xv * r
    dxh = dy * gv
    dx = r * (dxh - xh * jnp.mean(dxh * xh, axis=-1, keepdims=True))
    return dx, jnp.sum(dy * xh, axis=0, keepdims=True)


def rmsnorm_bwd(x, g, dn, resid, name):
    s, d = x.shape
    tm = _tile(s, 256)
    has_r = resid is not None

    def body(*refs):
        x_ref, g_ref, dn_ref = refs[:3]
        dx_ref, dg_ref = refs[-2:]
        dx, part = _rms_bwd_rows(x_ref[...], g_ref[...], dn_ref[...].astype(F32))
        if has_r:
            dx = dx + refs[3][...]
        dx_ref[...] = dx

        @pl.when(pl.program_id(0) == 0)
        def _():
            dg_ref[...] = jnp.zeros_like(dg_ref)

        dg_ref[...] += part

    row = pl.BlockSpec((tm, d), lambda i: (i, 0))
    vec = pl.BlockSpec((1, d), lambda i: (0, 0))
    ins = [x, g, dn] + ([resid] if has_r else [])
    return pl.pallas_call(body, name=name, grid=(s // tm,), in_specs=[row, vec, row] + ([row] if has_r else []),
                          out_specs=[row, vec], out_shape=[SDS((s, d), F32), SDS((1, d), F32)],
                          compiler_params=_cp(("arbitrary",)))(*ins)


def final_norm_loss(h, g, tgt, name):
    s, d = h.shape
    tm = _tile(s, 256)

    def body(h_ref, g_ref, t_ref, dh_ref, dg_ref, l_ref):
        xv, gv = h_ref[...], g_ref[...]
        r = lax.rsqrt(jnp.mean(xv * xv, axis=-1, keepdims=True) + RMS_EPS)
        e = xv * r * gv - t_ref[...]
        lpart = 0.5 * jnp.sum(jnp.mean(e * e, axis=-1, keepdims=True), axis=0, keepdims=True)
        dx, part = _rms_bwd_rows(xv, gv, e * (1.0 / d))
        dh_ref[...] = dx

        @pl.when(pl.program_id(0) == 0)
        def _():
            dg_ref[...] = jnp.zeros_like(dg_ref)
            l_ref[...] = jnp.zeros_like(l_ref)

        dg_ref[...] += part
        l_ref[...] += jnp.broadcast_to(lpart, l_ref.shape)

    row = pl.BlockSpec((tm, d), lambda i: (i, 0))
    vec = pl.BlockSpec((1, d), lambda i: (0, 0))
    lsp = pl.BlockSpec((1, LANE), lambda i: (0, 0))
    return pl.pallas_call(body, name=name, grid=(s // tm,), in_specs=[row, vec, row], out_specs=[row, vec, lsp],
                          out_shape=[SDS((s, d), F32), SDS((1, d), F32), SDS((1, LANE), F32)],
                          compiler_params=_cp(("arbitrary",)))(h, g, tgt)


def merge(p_all, ya, yb, yc, wa, wb, wc, dm, lay, name):
    s, d = ya.shape[0], lay.d
    wcols = wa.shape[2]
    tm, tn = _tile(s, 512), _tile(wcols, 512)
    nj, per = d // tn, wcols // tn
    bwd = dm is not None

    def body(*refs):
        ga, gb, gc, ya_r, yb_r, yc_r, wa_r, wb_r, wc_r = refs[:9]
        ts = [_dot(y[...], w[...], NN) for y, w in ((ya_r, wa_r), (yb_r, wb_r), (yc_r, wc_r))]
        gs = [_sigmoid(g[...]) for g in (ga, gb, gc)]
        if not bwd:
            refs[9][...] = (gs[0] * ts[0] + gs[1] * ts[1] + gs[2] * ts[2]).astype(BF16)
        else:
            dmv = refs[9][...]
            for i in range(3):
                refs[10 + i][...] = (dmv * ts[i] * (gs[i] * (1.0 - gs[i]))).astype(BF16)
                refs[13 + i][...] = (dmv * gs[i]).astype(BF16)

    gate_specs = [pl.BlockSpec((tm, tn), lambda i, j, b=b: (i, b * nj + j)) for b in range(3)]
    y_specs = [pl.BlockSpec((tm, y.shape[1]), lambda i, j: (i, 0)) for y in (ya, yb, yc)]
    w_specs = [pl.BlockSpec((None, w.shape[1], tn), lambda i, j: (j // per, 0, j % per)) for w in (wa, wb, wc)]
    o_spec = pl.BlockSpec((tm, tn), lambda i, j: (i, j))
    n_out = 6 if bwd else 1
    out = pl.pallas_call(
        body, name=name, grid=(s // tm, nj),
        in_specs=gate_specs + y_specs + w_specs + ([o_spec] if bwd else []),
        out_specs=[o_spec] * n_out, out_shape=[SDS((s, d), BF16)] * n_out,
        compiler_params=_cp(("parallel", "parallel")))(p_all, p_all, p_all, ya, yb, yc, wa, wb, wc, *([dm] if bwd else []))
    return out if bwd else out[0]


def _swa_probs(q, kc, sink, first):
    s = _dot(q, kc, NT) * (SWA_HD ** -0.5)
    qi = lax.broadcasted_iota(jnp.int32, s.shape, 0) % SWA_W
    kj = lax.broadcasted_iota(jnp.int32, s.shape, 1)
    mask = (kj > qi) & (kj <= qi + SWA_W) & ((kj >= SWA_W) | jnp.logical_not(first))
    s = jnp.where(mask, s, NEG)
    m = jnp.maximum(jnp.max(s, axis=-1, keepdims=True), sink)
    p = jnp.exp(s - m)
    es = jnp.exp(sink - m)
    denom = jnp.sum(p, axis=-1, keepdims=True) + es
    return p / denom, es / denom


def _swa_stack(ref, h):
    return jnp.concatenate([ref[:, pl.ds((h * SWA_G + g) * SWA_HD, SWA_HD)] for g in range(SWA_G)], axis=0)


def _swa_sink_col(sk_ref, h):
    head = lax.broadcasted_iota(jnp.int32, (SWA_G * SWA_W, 1), 0) // SWA_W
    col = jnp.zeros((SWA_G * SWA_W, 1), F32)
    for g in range(SWA_G):
        hh = h * SWA_G + g
        col = jnp.where(head == g, sk_ref[0:1, hh:hh + 1], col)
    return col


def _swa_specs(lay):
    w = SWA_W
    q_spec = pl.BlockSpec((w, Q_W), lambda n: (n, _div(lay.q, Q_W)))
    cur = lambda off: pl.BlockSpec((w, KV_W), lambda n: (n, _div(off, KV_W)))
    prev = lambda off: pl.BlockSpec((w, KV_W), lambda n: (jnp.maximum(n - 1, 0), _div(off, KV_W)))
    return q_spec, cur(lay.k), prev(lay.k), cur(lay.v), prev(lay.v)


def swa_fwd(p_all, sinks, lay, name):
    s = p_all.shape[0]
    nb = _div(s, SWA_W)

    def body(q_ref, kc_ref, kp_ref, vc_ref, vp_ref, sk_ref, o_ref):
        first = pl.program_id(0) == 0
        for h in range(SWA_HKV):
            hs = pl.ds(h * SWA_HD, SWA_HD)
            kc = jnp.concatenate([kp_ref[:, hs], kc_ref[:, hs]], axis=0).astype(BF16)
            vc = jnp.concatenate([vp_ref[:, hs], vc_ref[:, hs]], axis=0).astype(BF16)
            q = _swa_stack(q_ref, h).astype(BF16)
            p, _ = _swa_probs(q, kc, _swa_sink_col(sk_ref, h), first)
            o = _dot(p.astype(BF16), vc, NN).astype(BF16)
            for g in range(SWA_G):
                o_ref[:, pl.ds((h * SWA_G + g) * SWA_HD, SWA_HD)] = o[g * SWA_W:(g + 1) * SWA_W]

    q_spec, kc_s, kp_s, vc_s, vp_s = _swa_specs(lay)
    return pl.pallas_call(
        body, name=name, grid=(nb,),
        in_specs=[q_spec, kc_s, kp_s, vc_s, vp_s, pl.BlockSpec(sinks.shape, lambda n: (0, 0))],
        out_specs=pl.BlockSpec((SWA_W, Q_W), lambda n: (n, 0)), out_shape=SDS((s, Q_W), BF16),
        compiler_params=_cp(("parallel",)))(p_all, p_all, p_all, p_all, p_all, sinks)


def swa_bwd(p_all, sinks, dy, lay, name):
    s = p_all.shape[0]
    nb = _div(s, SWA_W)
    w = SWA_W

    def body(q_ref, kc_ref, kp_ref, vc_ref, vp_ref, sk_ref, do_ref, dq_ref, dk_ref, dv_ref, ds_ref, kcar, vcar):
        n = pl.program_id(0)
        first = n == 0

        @pl.when(first)
        def _():
            kcar[...] = jnp.zeros_like(kcar)
            vcar[...] = jnp.zeros_like(vcar)
            ds_ref[...] = jnp.zeros_like(ds_ref)

        @pl.when(n < nb)
        def _():
            lane = lax.broadcasted_iota(jnp.int32, (1, LANE), 1)
            dsink = jnp.zeros((1, LANE), F32)
            for h in range(SWA_HKV):
                hs = pl.ds(h * SWA_HD, SWA_HD)
                kc = jnp.concatenate([kp_ref[:, hs], kc_ref[:, hs]], axis=0).astype(BF16)
                vc = jnp.concatenate([vp_ref[:, hs], vc_ref[:, hs]], axis=0).astype(BF16)
                q = _swa_stack(q_ref, h).astype(BF16)
                do = _swa_stack(do_ref, h).astype(BF16)
                p, ps = _swa_probs(q, kc, _swa_sink_col(sk_ref, h), first)
                dp = _dot(do, vc, NT)
                dvc = _dot(p.astype(BF16), do, TN)
                rs = jnp.sum(dp * p, axis=-1, keepdims=True)
                dsb = (p * (dp - rs) * (SWA_HD ** -0.5)).astype(BF16)
                dq = _dot(dsb, kc, NN).astype(BF16)
                dkc = _dot(dsb, q, TN)
                psr = ps * rs
                for g in range(SWA_G):
                    hh = h * SWA_G + g
                    dq_ref[:, pl.ds(hh * SWA_HD, SWA_HD)] = dq[g * w:(g + 1) * w]
                    dsink = dsink + jnp.where(lane == hh, -jnp.sum(psr[g * w:(g + 1) * w], axis=0, keepdims=True), 0.0)
                dk_ref[:, hs] = (kcar[:, hs] + dkc[:w]).astype(BF16)
                dv_ref[:, hs] = (vcar[:, hs] + dvc[:w]).astype(BF16)
                kcar[:, hs] = dkc[w:]
                vcar[:, hs] = dvc[w:]
            ds_ref[...] += dsink

        @pl.when(n == nb)
        def _():
            dk_ref[...] = kcar[...].astype(BF16)
            dv_ref[...] = vcar[...].astype(BF16)

    last = nb - 1
    q_spec = pl.BlockSpec((w, Q_W), lambda n: (jnp.minimum(n, last), _div(lay.q, Q_W)))
    cur = lambda off: pl.BlockSpec((w, KV_W), lambda n: (jnp.minimum(n, last), _div(off, KV_W)))
    prev = lambda off: pl.BlockSpec((w, KV_W), lambda n: (jnp.clip(n - 1, 0, last), _div(off, KV_W)))
    row = pl.BlockSpec((w, Q_W), lambda n: (jnp.minimum(n, last), 0))
    kv_out = pl.BlockSpec((w, KV_W), lambda n: (jnp.maximum(n - 1, 0), 0))
    return pl.pallas_call(
        body, name=name, grid=(nb + 1,),
        in_specs=[q_spec, cur(lay.k), prev(lay.k), cur(lay.v), prev(lay.v), pl.BlockSpec(sinks.shape, lambda n: (0, 0)), row],
        out_specs=[row, kv_out, kv_out, pl.BlockSpec((1, LANE), lambda n: (0, 0))],
        out_shape=[SDS((s, Q_W), BF16), SDS((s, KV_W), BF16), SDS((s, KV_W), BF16), SDS((1, LANE), F32)],
        scratch_shapes=[pltpu.VMEM((w, KV_W), F32), pltpu.VMEM((w, KV_W), F32)],
        compiler_params=_cp(("arbitrary",)))(p_all, p_all, p_all, p_all, p_all, sinks, dy)


def _xa_probs(q, mk):
    s = _dot(q, mk, NT) * (XA_D ** -0.5)
    p = jnp.exp(s - jnp.max(s, axis=-1, keepdims=True))
    return p / jnp.sum(p, axis=-1, keepdims=True)


def xattn_fwd(p_all, mkv, lay, name):
    s, nm = p_all.shape[0], mkv.shape[0]
    tm = _tile(s, 512)

    def body(q_ref, mkv_ref, o_ref):
        for h in range(XA_H):
            cols = pl.ds(h * XA_D, XA_D)
            p = _xa_probs(q_ref[:, cols].astype(BF16), mkv_ref[:, cols])
            o_ref[:, cols] = _dot(p.astype(BF16), mkv_ref[:, pl.ds(XA_W + h * XA_D, XA_D)], NN).astype(BF16)

    return pl.pallas_call(
        body, name=name, grid=(s // tm,),
        in_specs=[pl.BlockSpec((tm, XA_W), lambda i: (i, _div(lay.qc, XA_W))), pl.BlockSpec((nm, 2 * XA_W), lambda i: (0, 0))],
        out_specs=pl.BlockSpec((tm, XA_W), lambda i: (i, 0)), out_shape=SDS((s, XA_W), BF16),
        compiler_params=_cp(("parallel",)))(p_all, mkv)


def xattn_bwd(p_all, mkv, dy, lay, name):
    s, nm = p_all.shape[0], mkv.shape[0]
    tm = _tile(s, 512)

    def body(q_ref, mkv_ref, do_ref, dq_ref, dmkv_ref):
        @pl.when(pl.program_id(0) == 0)
        def _():
            dmkv_ref[...] = jnp.zeros_like(dmkv_ref)

        for h in range(XA_H):
            cols = pl.ds(h * XA_D, XA_D)
            vcols = pl.ds(XA_W + h * XA_D, XA_D)
            q = q_ref[:, cols].astype(BF16)
            do = do_ref[:, cols].astype(BF16)
            p = _xa_probs(q, mkv_ref[:, cols])
            dp = _dot(do, mkv_ref[:, vcols], NT)
            dmkv_ref[:, vcols] += _dot(p.astype(BF16), do, TN)
            dsb = (p * (dp - jnp.sum(dp * p, axis=-1, keepdims=True)) * (XA_D ** -0.5)).astype(BF16)
            dq_ref[:, cols] = _dot(dsb, mkv_ref[:, cols], NN).astype(BF16)
            dmkv_ref[:, cols] += _dot(dsb, q, TN)

    row = pl.BlockSpec((tm, XA_W), lambda i: (i, 0))
    full = pl.BlockSpec((nm, 2 * XA_W), lambda i: (0, 0))
    return pl.pallas_call(
        body, name=name, grid=(s // tm,),
        in_specs=[pl.BlockSpec((tm, XA_W), lambda i: (i, _div(lay.qc, XA_W))), full, row],
        out_specs=[row, full], out_shape=[SDS((s, XA_W), BF16), SDS((nm, 2 * XA_W), F32)],
        compiler_params=_cp(("arbitrary",)))(p_all, mkv, dy)


def _shift_down(cur, prev8, s):
    cat = jnp.concatenate([prev8, cur[0:8]], axis=0)
    return pltpu.roll(cur, s, axis=0), pltpu.roll(cat, s, axis=0)[8:16]


def _shift_up(cur, next8, s):
    tm = cur.shape[0]
    cat = jnp.concatenate([cur[tm - 8:tm], next8], axis=0)
    return pltpu.roll(cur, tm - s, axis=0), pltpu.roll(cat, 16 - s, axis=0)[0:8]


def gdn_conv_fwd(p_all, conv_w, lay, name):
    s = p_all.shape[0]
    tm = _tile(s, 512)
    c0 = _div(lay.qkv, GDN_W)

    def body(x_ref, prev_ref, w_ref, o_ref):
        cur = x_ref[...]
        prev8 = jnp.where(pl.program_id(1) > 0, prev_ref[...], 0.0)
        main = w_ref[GDN_CONV - 1:GDN_CONV, :] * cur
        top = w_ref[GDN_CONV - 1:GDN_CONV, :] * cur[0:8]
        for sft in range(1, GDN_CONV):
            wi = w_ref[GDN_CONV - 1 - sft:GDN_CONV - sft, :]
            a, b = _shift_down(cur, prev8, sft)
            main = main + wi * a
            top = top + wi * b
        o_ref[...] = main
        o_ref[0:8, :] = top

    return pl.pallas_call(
        body, name=name, grid=(3, s // tm),
        in_specs=[pl.BlockSpec((tm, GDN_W), lambda c, i: (i, c0 + c)),
                  pl.BlockSpec((8, GDN_W), lambda c, i: (jnp.maximum(i * (tm // 8) - 1, 0), c0 + c)),
                  pl.BlockSpec((GDN_CONV, GDN_W), lambda c, i: (0, c))],
        out_specs=pl.BlockSpec((tm, GDN_W), lambda c, i: (i, c)), out_shape=SDS((s, 3 * GDN_W), F32),
        compiler_params=_cp(("parallel", "parallel")))(p_all, p_all, conv_w)


def gdn_conv_bwd(p_all, conv_w, dxc, lay, name):
    s = p_all.shape[0]
    tm = _tile(s, 512)
    c0 = _div(lay.qkv, GDN_W)
    nt = s // tm

    def body(x_ref, prev_ref, d_ref, next_ref, w_ref, dx_ref, dw_ref):
        i = pl.program_id(1)
        cur, d = x_ref[...], d_ref[...]
        prev8 = jnp.where(i > 0, prev_ref[...], 0.0)
        next8 = jnp.where(i < nt - 1, next_ref[...], 0.0)
        row = lax.broadcasted_iota(jnp.int32, (tm, 1), 0)
        main = w_ref[GDN_CONV - 1:GDN_CONV, :] * d
        bot = w_ref[GDN_CONV - 1:GDN_CONV, :] * d[tm - 8:tm]
        dws = [jnp.sum(d * cur, axis=0, keepdims=True)]
        for sft in range(1, GDN_CONV):
            wi = w_ref[GDN_CONV - 1 - sft:GDN_CONV - sft, :]
            a, b = _shift_up(d, next8, sft)
            main = main + wi * a
            bot = bot + wi * b
            xa, xb = _shift_down(cur, prev8, sft)
            dws.append(jnp.sum(jnp.where(row >= 8, d * xa, 0.0), axis=0, keepdims=True)
                       + jnp.sum(d[0:8] * xb, axis=0, keepdims=True))
        dx_ref[...] = main.astype(BF16)
        dx_ref[tm - 8:tm, :] = bot.astype(BF16)

        @pl.when(i == 0)
        def _():
            dw_ref[...] = jnp.zeros_like(dw_ref)

        for sft in range(GDN_CONV):
            dw_ref[GDN_CONV - 1 - sft:GDN_CONV - sft, :] += dws[sft]

    return pl.pallas_call(
        body, name=name, grid=(3, nt),
        in_specs=[pl.BlockSpec((tm, GDN_W), lambda c, i: (i, c0 + c)),
                  pl.BlockSpec((8, GDN_W), lambda c, i: (jnp.maximum(i * (tm // 8) - 1, 0), c0 + c)),
                  pl.BlockSpec((tm, GDN_W), lambda c, i: (i, c)),
                  pl.BlockSpec((8, GDN_W), lambda c, i: (jnp.minimum((i + 1) * (tm // 8), s // 8 - 1), c)),
                  pl.BlockSpec((GDN_CONV, GDN_W), lambda c, i: (0, c))],
        out_specs=[pl.BlockSpec((tm, GDN_W), lambda c, i: (i, c)), pl.BlockSpec((GDN_CONV, GDN_W), lambda c, i: (0, c))],
        out_shape=[SDS((s, 3 * GDN_W), BF16), SDS((GDN_CONV, 3 * GDN_W), F32)],
        compiler_params=_cp(("parallel", "arbitrary")))(p_all, p_all, dxc, dxc, conv_w)


def _gdn_chunk(xq, xk, xv, ab, gp, bdot=_bdot_plain):
    c = GDN_C
    nc = xq.shape[0] // c
    lane = lax.broadcasted_iota(jnp.int32, (c, LANE), 1)
    row = lax.broadcasted_iota(jnp.int32, (c, c), 0)
    col = lax.broadcasted_iota(jnp.int32, (c, c), 1)
    g_tile = -jnp.exp(gp[0:1, :]) * _softplus(ab + gp[1:2, :])
    b_tile = _sigmoid(ab)
    tri = (row >= col).astype(F32)
    qa, ka, va = _silu(xq), _silu(xk), _silu(xv)
    items = []
    for ci in range(nc):
        rs = slice(ci * c, (ci + 1) * c)
        gcum = _dot(tri, g_tile[rs], NN, HI)
        gcum_t = gcum.T
        for h in range(GDN_H):
            hs = slice(h * GDN_D, (h + 1) * GDN_D)
            q, k, v = qa[rs, hs], ka[rs, hs], va[rs, hs]
            q = q * lax.rsqrt(jnp.sum(q * q, axis=-1, keepdims=True) + L2_EPS) * (GDN_D ** -0.5)
            k = k * lax.rsqrt(jnp.sum(k * k, axis=-1, keepdims=True) + L2_EPS)
            gc = jnp.sum(jnp.where(lane == h, gcum, 0.0), axis=1, keepdims=True)
            beta = jnp.sum(jnp.where(lane == GDN_H + h, b_tile[rs], 0.0), axis=1, keepdims=True)
            decay = jnp.exp(jnp.where(row >= col, gc - gcum_t[h:h + 1, :], NEG))
            items.append((q, k, v, gc, beta, decay))
    kks = [bdot(k, k, NT) for (_, k, _, _, _, _) in items]
    pws = [-jnp.where(row > col, it[4] * kk * it[5], 0.0) for it, kk in zip(items, kks)]
    nns = list(pws)
    for _ in range(5):
        pws = [bdot(p, p, NN) for p in pws]
        nns = [n + p + bdot(n, p, NN) for n, p in zip(nns, pws)]
    qks = [bdot(q, k, NT) for (q, k, _, _, _, _) in items]
    out = []
    for (q, k, v, gc, beta, decay), n, qk in zip(items, nns, qks):
        eg = jnp.exp(gc)
        vb = v * beta
        kbe = k * (beta * eg)
        gl = gc[c - 1:c, :]
        out.append((vb + bdot(n, vb, NN), kbe + bdot(n, kbe, NN), q * eg, k * jnp.exp(gl - gc), qk * decay, jnp.exp(gl)))
    return [out[ci * GDN_H:(ci + 1) * GDN_H] for ci in range(nc)]


GDN_CPS = 4


def _gdn_pre_specs(lay, t):
    xspec = lambda j: pl.BlockSpec((t, GDN_W), lambda n, j=j: (n, j))
    return [xspec(0), xspec(1), xspec(2), pl.BlockSpec((t, LANE), lambda n: (n, _div(lay.ab, LANE))),
            pl.BlockSpec((8, LANE), lambda n: (0, 0))]


def gdn_pre_fwd(xc, p_all, gp, lay, name):
    s = xc.shape[0]
    c = GDN_C
    n = _div(s, c)
    cps = _tile(n, GDN_CPS)
    t = cps * c

    def body(xq, xk, xv, ab, gp_ref, u_ref, w_ref, qd_ref, kd_ref, qk_ref, gl_ref):
        lane = lax.broadcasted_iota(jnp.int32, (1, LANE), 1)
        chunks = _gdn_chunk(xq[...], xk[...], xv[...], ab[...], gp_ref[...])
        for ci, heads in enumerate(chunks):
            rs = pl.ds(ci * c, c)
            gl_row = jnp.zeros((1, LANE), F32)
            for h, (u, w, qd, kd, qk, gl) in enumerate(heads):
                hs = pl.ds(h * GDN_D, GDN_D)
                u_ref[rs, hs] = u
                w_ref[rs, hs] = w.astype(BF16)
                qd_ref[rs, hs] = qd.astype(BF16)
                kd_ref[rs, hs] = kd.astype(BF16)
                qk_ref[rs, pl.ds(h * c, c)] = qk.astype(BF16)
                gl_row = gl_row + jnp.where(lane == h, gl, 0.0)
            gl_ref[ci] = gl_row

    row = pl.BlockSpec((t, GDN_W), lambda n: (n, 0))
    return pl.pallas_call(
        body, name=name, grid=(n // cps,), in_specs=_gdn_pre_specs(lay, t),
        out_specs=[row, row, row, row, pl.BlockSpec((t, GDN_H * c), lambda n: (n, 0)), pl.BlockSpec((cps, 1, LANE), lambda n: (n, 0, 0))],
        out_shape=[SDS((s, GDN_W), F32), SDS((s, GDN_W), BF16), SDS((s, GDN_W), BF16), SDS((s, GDN_W), BF16),
                   SDS((s, GDN_H * c), BF16), SDS((n, 1, LANE), F32)],
        compiler_params=_cp(("parallel",)))(xc, xc, xc, p_all, gp)


def gdn_pre_bwd(xc, p_all, gp, du, dw, dqd, dkd, dqk, dgl, lay, name):
    s = xc.shape[0]
    c = GDN_C
    n = _div(s, c)
    cps = _tile(n, GDN_CPS)
    t = cps * c
    chunk = functools.partial(_gdn_chunk, bdot=_bdot_vjp)

    def body(xq, xk, xv, ab, gp_ref, du_r, dw_r, dqd_r, dkd_r, dqk_r, dgl_r, dxc_ref, dab_ref, dgp_ref):
        lane = lax.broadcasted_iota(jnp.int32, (1, LANE), 1)
        _, vjp = jax.vjp(chunk, xq[...], xk[...], xv[...], ab[...], gp_ref[...])
        cts = []
        for ci in range(cps):
            rs = pl.ds(ci * c, c)
            heads = []
            for h in range(GDN_H):
                hs = pl.ds(h * GDN_D, GDN_D)
                dgl_h = jnp.sum(jnp.where(lane == h, dgl_r[ci], 0.0), axis=1, keepdims=True)
                heads.append((du_r[rs, hs], dw_r[rs, hs], dqd_r[rs, hs], dkd_r[rs, hs], dqk_r[rs, pl.ds(h * c, c)], dgl_h))
            cts.append(heads)
        dq, dk, dv, dab, dgp = vjp(cts)
        dxc_ref[:, pl.ds(0, GDN_W)] = dq
        dxc_ref[:, pl.ds(GDN_W, GDN_W)] = dk
        dxc_ref[:, pl.ds(2 * GDN_W, GDN_W)] = dv
        dab_ref[...] = dab.astype(BF16)

        @pl.when(pl.program_id(0) == 0)
        def _():
            dgp_ref[...] = jnp.zeros_like(dgp_ref)

        dgp_ref[...] += dgp

    row = pl.BlockSpec((t, GDN_W), lambda n: (n, 0))
    return pl.pallas_call(
        body, name=name, grid=(n // cps,),
        in_specs=_gdn_pre_specs(lay, t) + [row, row, row, row, pl.BlockSpec((t, GDN_H * c), lambda n: (n, 0)),
                                           pl.BlockSpec((cps, 1, LANE), lambda n: (n, 0, 0))],
        out_specs=[pl.BlockSpec((t, 3 * GDN_W), lambda n: (n, 0)), pl.BlockSpec((t, LANE), lambda n: (n, 0)),
                   pl.BlockSpec((8, LANE), lambda n: (0, 0))],
        out_shape=[SDS((s, 3 * GDN_W), F32), SDS((s, LANE), BF16), SDS((8, LANE), F32)],
        compiler_params=_cp(("arbitrary",)))(xc, xc, xc, p_all, gp, du, dw, dqd, dkd, dqk, dgl)


def _lane_scalar(row, h):
    lane = lax.broadcasted_iota(jnp.int32, row.shape, 1)
    return jnp.sum(jnp.where(lane == h, row, 0.0), axis=1, keepdims=True)


def gdn_scan_fwd(u, w, qd, kd, qk, gl, name):
    s = u.shape[0]
    c = GDN_C
    n = _div(s, c)

    def body(u_r, w_r, qd_r, kd_r, qk_r, gl_r, o_ref, s_ref, st):
        @pl.when(pl.program_id(0) == 0)
        def _():
            st[...] = jnp.zeros_like(st)

        s_ref[0] = st[...]
        for h in range(GDN_H):
            hs = pl.ds(h * GDN_D, GDN_D)
            sh = st[hs, :]
            shb = sh.astype(BF16)
            v_new = u_r[:, hs] - _dot(w_r[:, hs], shb, NN)
            vb = v_new.astype(BF16)
            o_ref[:, hs] = _dot(qd_r[:, hs], shb, NN) + _dot(qk_r[:, pl.ds(h * c, c)], vb, NN)
            st[hs, :] = sh * _lane_scalar(gl_r[0], h) + _dot(kd_r[:, hs], vb, TN)

    row = pl.BlockSpec((c, GDN_W), lambda i: (i, 0))
    return pl.pallas_call(
        body, name=name, grid=(n,),
        in_specs=[row, row, row, row, pl.BlockSpec((c, GDN_H * c), lambda i: (i, 0)), pl.BlockSpec((1, 1, LANE), lambda i: (i, 0, 0))],
        out_specs=[row, pl.BlockSpec((1, GDN_W, GDN_D), lambda i: (i, 0, 0))],
        out_shape=[SDS((s, GDN_W), F32), SDS((n, GDN_W, GDN_D), F32)],
        scratch_shapes=[pltpu.VMEM((GDN_W, GDN_D), F32)],
        compiler_params=_cp(("arbitrary",)))(u, w, qd, kd, qk, gl)


def gdn_scan_bwd(u, w, qd, kd, qk, gl, states, do, name):
    s = u.shape[0]
    c = GDN_C
    n = _div(s, c)

    def body(u_r, w_r, qd_r, kd_r, qk_r, gl_r, s_r, do_r, du_o, dw_o, dqd_o, dkd_o, dqk_o, dgl_o, dst):
        @pl.when(pl.program_id(0) == 0)
        def _():
            dst[...] = jnp.zeros_like(dst)

        lane = lax.broadcasted_iota(jnp.int32, (1, LANE), 1)
        dgl_row = jnp.zeros((1, LANE), F32)
        for h in range(GDN_H):
            hs = pl.ds(h * GDN_D, GDN_D)
            qs = pl.ds(h * c, c)
            sh = s_r[0, hs, :]
            shb = sh.astype(BF16)
            ds_out = dst[hs, :]
            dsb = ds_out.astype(BF16)
            dob = do_r[:, hs].astype(BF16)
            wv, qdv, kdv, qkv = w_r[:, hs], qd_r[:, hs], kd_r[:, hs], qk_r[:, qs]
            v_new = u_r[:, hs] - _dot(wv, shb, NN)
            vb = v_new.astype(BF16)
            dv = _dot(qkv, dob, TN) + _dot(kdv, dsb, NN)
            dvb = dv.astype(BF16)
            du_o[:, hs] = dv
            dw_o[:, hs] = -_dot(dvb, shb, NT)
            dqd_o[:, hs] = _dot(dob, shb, NT)
            dkd_o[:, hs] = _dot(vb, dsb, NT)
            dqk_o[:, qs] = _dot(dob, vb, NT)
            dgl_row = dgl_row + jnp.where(lane == h, jnp.sum(jnp.sum(ds_out * sh, axis=1, keepdims=True), axis=0, keepdims=True), 0.0)
            dst[hs, :] = ds_out * _lane_scalar(gl_r[0], h) + _dot(qdv, dob, TN) - _dot(wv, dvb, TN)
        dgl_o[0] = dgl_row

    rev = lambda i: n - 1 - i
    row = pl.BlockSpec((c, GDN_W), lambda i: (rev(i), 0))
    qks = pl.BlockSpec((c, GDN_H * c), lambda i: (rev(i), 0))
    gls = pl.BlockSpec((1, 1, LANE), lambda i: (rev(i), 0, 0))
    return pl.pallas_call(
        body, name=name, grid=(n,),
        in_specs=[row, row, row, row, qks, gls, pl.BlockSpec((1, GDN_W, GDN_D), lambda i: (rev(i), 0, 0)), row],
        out_specs=[row, row, row, row, qks, gls],
        out_shape=[SDS((s, GDN_W), F32)] * 4 + [SDS((s, GDN_H * c), F32), SDS((n, 1, LANE), F32)],
        scratch_shapes=[pltpu.VMEM((GDN_W, GDN_D), F32)],
        compiler_params=_cp(("arbitrary",)))(u, w, qd, kd, qk, gl, states, do)


def _gdn_out_rows(o, z, nw):
    outs = []
    for h in range(GDN_H):
        hs = slice(h * GDN_D, (h + 1) * GDN_D)
        oh = o[:, hs]
        y = oh * lax.rsqrt(jnp.mean(oh * oh, axis=-1, keepdims=True) + RMS_EPS) * nw
        outs.append(y * _silu(z[:, hs]))
    return jnp.concatenate(outs, axis=1)


def gdn_out(o, p_all, nw, dy, lay, name):
    s = o.shape[0]
    tm = _tile(s, 512)
    bwd = dy is not None

    def body(*refs):
        o_r, z_r, nw_r = refs[:3]
        if not bwd:
            refs[3][...] = _gdn_out_rows(o_r[...], z_r[...], nw_r[...]).astype(BF16)
            return
        dy_r, do_o, dz_o, dnw_o = refs[3:]
        _, vjp = jax.vjp(_gdn_out_rows, o_r[...], z_r[...], nw_r[...])
        d_o, d_z, d_nw = vjp(dy_r[...].astype(F32))
        do_o[...] = d_o
        dz_o[...] = d_z.astype(BF16)

        @pl.when(pl.program_id(0) == 0)
        def _():
            dnw_o[...] = jnp.zeros_like(dnw_o)

        dnw_o[...] += d_nw

    row = pl.BlockSpec((tm, GDN_W), lambda i: (i, 0))
    zs = pl.BlockSpec((tm, GDN_W), lambda i: (i, _div(lay.z, GDN_W)))
    nws = pl.BlockSpec((1, GDN_D), lambda i: (0, 0))
    if not bwd:
        return pl.pallas_call(body, name=name, grid=(s // tm,), in_specs=[row, zs, nws], out_specs=row,
                              out_shape=SDS((s, GDN_W), BF16), compiler_params=_cp(("parallel",)))(o, p_all, nw)
    return pl.pallas_call(body, name=name, grid=(s // tm,), in_specs=[row, zs, nws, row], out_specs=[row, row, nws],
                          out_shape=[SDS((s, GDN_W), F32), SDS((s, GDN_W), BF16), SDS((1, GDN_D), F32)],
                          compiler_params=_cp(("arbitrary",)))(o, p_all, nw, dy)


def _cols_to_full(g):
    n, k, c = g.shape
    return g.transpose(1, 0, 2).reshape(k, n * c)


def _rows_to_blocks(w):
    return w.reshape(N_DEV, w.shape[0] // N_DEV, w.shape[1])


def _pack_small(parts, rows):
    flat = jnp.concatenate([jnp.pad(p.reshape(-1), (0, -p.size % LANE)) for p in parts])
    return jnp.pad(flat, (0, rows * LANE - flat.size)).reshape(rows, LANE)


def kernel(x, mem, g_mix, w_in, sinks, conv_w, a_log, dt_bias, gdn_norm_w, g_mem, w_mem_kv, w_swa_up, w_gdn_up, w_xa_up, w_out, g_mlp, w_mlp_in, w_mlp_out, g_final, loss_target, m_g_mix, m_w_in, m_sinks, m_conv_w, m_a_log, m_dt_bias, m_gdn_norm_w, m_g_mem, m_w_mem_kv, m_w_swa_up, m_w_gdn_up, m_w_xa_up, m_w_out, m_g_mlp, m_w_mlp_in, m_w_mlp_out, m_g_final, v_g_mix, v_w_in, v_sinks, v_conv_w, v_a_log, v_dt_bias, v_gdn_norm_w, v_g_mem, v_w_mem_kv, v_w_swa_up, v_w_gdn_up, v_w_xa_up, v_w_out, v_g_mlp, v_w_mlp_in, v_w_mlp_out, v_g_final):
    xs, ms, tgt = x[0], mem[0], loss_target[0]
    s, d = xs.shape
    lay = Layout(d)
    px, py, pc = _position()
    dev = 4 * px + 2 * py + pc

    g_in, g_conv = all_gather([w_in[0].astype(BF16), conv_w[0]], "gather_w_in")
    W_in = pad_w_in(g_in, lay)
    convw = _cols_to_full(g_conv)
    gp = jnp.zeros((8, LANE), F32).at[0, :GDN_H].set(a_log[0]).at[1, :GDN_H].set(dt_bias[0])
    later = [w_mem_kv[0], w_swa_up[0], w_gdn_up[0], w_xa_up[0], w_out[0], w_mlp_in[0], w_mlp_out[0]]

    n1 = rmsnorm_fwd(xs, g_mix, "norm_mix")
    p_all, g_mkv, W_sup, W_gup, W_xup, g_out, W_m1, g_m2 = matmul(
        n1, W_in, mode="nn", out_dtype=F32, name="proj_in", tm=2048, tn=512, tk=d,
        side=GatherJob([w.astype(BF16) for w in later]))
    W_mkv = g_mkv.reshape(-1, g_mkv.shape[2])
    W_out = g_out.reshape(-1, d)
    W_m2 = g_m2.reshape(-1, d)
    y_a = swa_fwd(p_all, sinks, lay, "swa_fwd")
    xc = gdn_conv_fwd(p_all, convw, lay, "gdn_conv_fwd")
    u, gw, gqd, gkd, gqk, ggl = gdn_pre_fwd(xc, p_all, gp, lay, "gdn_pre_fwd")
    o_b, states = gdn_scan_fwd(u, gw, gqd, gkd, gqk, ggl, "gdn_scan_fwd")
    y_b = gdn_out(o_b, p_all, gdn_norm_w, None, lay, "gdn_out_fwd")
    nm = rmsnorm_fwd(ms, g_mem, "norm_mem")
    mkv = matmul(nm, W_mkv, mode="nn", out_dtype=BF16, name="proj_mem", tk=d)
    y_c = xattn_fwd(p_all, mkv, lay, "xattn_fwd")
    merged = merge(p_all, y_a, y_b, y_c, W_sup, W_gup, W_xup, None, lay, "merge_fwd")
    h1 = matmul(merged, W_out, mode="nn", out_dtype=F32, name="proj_out", tm=2048, tn=512, tk=d, resid=xs)
    n2 = rmsnorm_fwd(h1, g_mlp, "norm_mlp")
    uu = matmul(n2, W_m1, mode="nn", out_dtype=F32, name="mlp_in", tm=2048, tn=512, tk=d, b_cols=True)
    h2 = matmul(uu, W_m2, mode="nn", out_dtype=F32, name="mlp_out", tm=1024, tn=2048, tk=512, a_relu2=True, resid=h1)
    dh2, dg_final, lrow = final_norm_loss(h2, g_final.reshape(1, d), tgt, "final_loss")
    loss = lax.psum(lrow[0, 0], ("x", "y", "c"))

    du = matmul(dh2, W_m2, mode="nt", out_dtype=BF16, name="mlp_out_dx", tm=2048, tn=512, tk=d, relu2_grad_of=uu)
    dW_m2 = matmul(uu, dh2, mode="tn", out_dtype=BF16, name="mlp_out_dw", tm=1024, tn=2048, tk=512, a_relu2=True)
    dn2 = matmul(du, W_m1, mode="nt", out_dtype=F32, name="mlp_in_dx", tm=1024, tn=2048, tk=1024, b_cols=True)
    dW_m1 = matmul(n2, du, mode="tn", out_dtype=BF16, name="mlp_in_dw", tm=2048, tn=1024, tk=512, out_cols=True)
    dh1, dg_mlp = rmsnorm_bwd(h1, g_mlp, dn2, dh2, "norm_mlp_bwd")

    dmerged = matmul(dh1, W_out, mode="nt", out_dtype=F32, name="proj_out_dx", tm=2048, tn=512, tk=d)
    dW_out = matmul(merged, dh1, mode="tn", out_dtype=BF16, name="proj_out_dw", tm=2048, tn=1024, tk=512)
    dga, dgb, dgc, dta, dtb, dtc = merge(p_all, y_a, y_b, y_c, W_sup, W_gup, W_xup, dmerged, lay, "merge_bwd")
    dy_a = matmul(dta, W_sup, mode="nt", out_dtype=BF16, name="swa_up_dx", tm=2048, tk=d, b_cols=True)
    dy_b = matmul(dtb, W_gup, mode="nt", out_dtype=BF16, name="gdn_up_dx", tm=2048, tk=d, b_cols=True)
    dy_c = matmul(dtc, W_xup, mode="nt", out_dtype=BF16, name="xa_up_dx", tm=2048, tk=d, b_cols=True)
    dW_sup = matmul(y_a, dta, mode="tn", out_dtype=BF16, name="swa_up_dw", tk=1024, out_cols=True)
    dW_gup = matmul(y_b, dtb, mode="tn", out_dtype=BF16, name="gdn_up_dw", tk=1024, out_cols=True)
    dW_xup = matmul(y_c, dtc, mode="tn", out_dtype=BF16, name="xa_up_dw", tk=1024, out_cols=True)

    dq_a, dk_a, dv_a, dsinks = swa_bwd(p_all, sinks, dy_a, lay, "swa_bwd")
    dq_c, dmkv = xattn_bwd(p_all, mkv, dy_c, lay, "xattn_bwd")
    dW_mkv = matmul(nm, dmkv, mode="tn", out_dtype=BF16, name="proj_mem_dw", tk=256)
    dnm = matmul(dmkv, W_mkv, mode="nt", out_dtype=F32, name="proj_mem_dx", tk=1024)
    _, dg_mem = rmsnorm_bwd(ms, g_mem, dnm, None, "norm_mem_bwd")

    do_b, dz, dnorm_w = gdn_out(o_b, p_all, gdn_norm_w, dy_b, lay, "gdn_out_bwd")
    du_g, dw_g, dqd_g, dkd_g, dqk_g, dgl_g = gdn_scan_bwd(u, gw, gqd, gkd, gqk, ggl, states, do_b, "gdn_scan_bwd")
    dxc, dab, dgp = gdn_pre_bwd(xc, p_all, gp, du_g, dw_g, dqd_g, dkd_g, dqk_g, dgl_g, lay, "gdn_pre_bwd")
    dqkv, dconv = gdn_conv_bwd(p_all, convw, dxc, lay, "gdn_conv_bwd")

    dp_all = jnp.concatenate([dga, dgb, dgc, dq_a, dqkv, dz, dq_c, dk_a, dv_a, dab,
                              jnp.zeros((s, lay.pw - lay.end), BF16)], axis=1)
    def pair_stage(blocks, tag):
        from_sib = pair_exchange(blocks, "grads_pair_exchange_" + tag)
        return [pair_add(b, o, "grads_pair_add_%s%d" % (tag, i)) for i, (b, o) in enumerate(zip(blocks, from_sib))]

    early = pair_stage([_rows_to_blocks(dW_mkv), dW_sup, dW_gup, dW_xup, _rows_to_blocks(dW_out), dW_m1, _rows_to_blocks(dW_m2)], "a")
    dW_in, *parts_early = matmul(n1, dp_all, mode="tn", out_dtype=BF16, name="proj_in_dw", tm=2048, tn=1024, tk=512,
                                 side=ChipExchangeJob(early))
    late = pair_stage([unpad_dw_in(dW_in, lay)], "b")
    dn1, parts_in = matmul(dp_all, W_in, mode="nt", out_dtype=F32, name="proj_in_dx", tm=1024, tn=2048, tk=1024,
                           side=ChipExchangeJob(late))
    grad_x, dg_mix = rmsnorm_bwd(xs, g_mix, dn1, dh1, "norm_mix_bwd")
    parts = [parts_in] + parts_early

    shard_names = [(w_in, m_w_in, v_w_in), (w_mem_kv, m_w_mem_kv, v_w_mem_kv), (w_swa_up, m_w_swa_up, v_w_swa_up),
                   (w_gdn_up, m_w_gdn_up, v_w_gdn_up), (w_xa_up, m_w_xa_up, v_w_xa_up), (w_out, m_w_out, v_w_out),
                   (w_mlp_in, m_w_mlp_in, v_w_mlp_in), (w_mlp_out, m_w_mlp_out, v_w_mlp_out)]
    big_res = [adamw(p, w[0], m[0], v[0], "adamw_%d" % i) for i, (p, (w, m, v)) in enumerate(zip(parts, shard_names))]

    smalls = [(g_mix, m_g_mix, v_g_mix, dg_mix), (sinks, m_sinks, v_sinks, dsinks[:, :SWA_HQ]),
              (a_log, m_a_log, v_a_log, dgp[0:1, :GDN_H]), (dt_bias, m_dt_bias, v_dt_bias, dgp[1:2, :GDN_H]),
              (gdn_norm_w, m_gdn_norm_w, v_gdn_norm_w, dnorm_w), (g_mem, m_g_mem, v_g_mem, dg_mem),
              (g_mlp, m_g_mlp, v_g_mlp, dg_mlp), (g_final, m_g_final, v_g_final, dg_final)]
    sizes = [-(-t[0].size // LANE) * LANE for t in smalls] + [GDN_CONV * 3 * GDN_W]
    rows = -(-sum(sizes) // (8 * LANE)) * 8
    csh = conv_w.shape[2]

    def conv_place(a):
        return lax.dynamic_update_slice(jnp.zeros((GDN_CONV, 3 * GDN_W), F32), a[0], (0, dev * csh))

    g_pack = _pack_small([t[3] for t in smalls] + [dconv], rows)
    w_pack = _pack_small([t[0] for t in smalls] + [conv_place(conv_w)], rows)
    m_pack = _pack_small([t[1] for t in smalls] + [conv_place(m_conv_w)], rows)
    v_pack = _pack_small([t[2] for t in smalls] + [conv_place(v_conv_w)], rows)
    g_all = all_gather([g_pack], "gather_small_grads")[0]
    small_res = adamw(g_all, w_pack, m_pack, v_pack, "adamw_small")

    def unpack(arr):
        flat = arr.reshape(-1)
        outs, off = [], 0
        for t, sz in zip(smalls, sizes[:-1]):
            outs.append(flat[off:off + t[0].size].reshape(t[0].shape))
            off += sz
        cw = flat[off:off + sizes[-1]].reshape(GDN_CONV, 3 * GDN_W)
        outs.append(lax.dynamic_slice(cw, (0, dev * csh), (GDN_CONV, csh))[None])
        return outs

    sg, sd, sm, sv = (unpack(a) for a in small_res)
    bg, bd, bm, bv = ([r[i][None] for r in big_res] for i in range(4))

    def ordered(sm_, bg_):
        return [sm_[0], bg_[0], sm_[1], sm_[8], sm_[2], sm_[3], sm_[4], sm_[5], bg_[1], bg_[2], bg_[3], bg_[4], bg_[5],
                sm_[6], bg_[6], bg_[7], sm_[7]]

    return (loss, grad_x[None], *ordered(sg, bg), *ordered(sd, bd), *ordered(sm, bm), *ordered(sv, bv))
```

```python
import functools
import math

import jax
import jax.numpy as jnp
from jax import lax
from jax.experimental import pallas as pl
from jax.experimental.pallas import tpu as pltpu

F32, BF16 = jnp.float32, jnp.bfloat16
SDS = jax.ShapeDtypeStruct
MESH = pl.DeviceIdType.MESH
ANY = pl.BlockSpec(memory_space=pl.ANY)

SWA_HQ, SWA_HKV, SWA_HD, SWA_W = 16, 2, 64, 128
SWA_G = SWA_HQ // SWA_HKV
GDN_H, GDN_D, GDN_CONV, GDN_C = 4, 128, 4, 64
XA_H, XA_D = 4, 128
Q_W = SWA_HQ * SWA_HD
KV_W = SWA_HKV * SWA_HD
GDN_W = GDN_H * GDN_D
XA_W = XA_H * XA_D
RMS_EPS = 1e-6
L2_EPS = 1e-6
NEG = -1e30
N_DEV = 8
LANE = 128

ADAM_LR, ADAM_B1, ADAM_B2, ADAM_EPS, ADAM_WD, ADAM_STEP = 0.001, 0.9, 0.999, 1e-08, 0.01, 10

VMEM_BIG = 56 * 1024 * 1024


def _cp(sem, vmem=VMEM_BIG):
    return pltpu.CompilerParams(dimension_semantics=sem, vmem_limit_bytes=vmem)


def _div(a, b):
    assert a % b == 0, (a, b)
    return a // b


def _tile(n, t):
    t = min(t, n)
    assert n % t == 0, (n, t)
    return t


def _sigmoid(x):
    return 1.0 / (1.0 + jnp.exp(-x))


def _silu(x):
    return x * _sigmoid(x)


def _softplus(x):
    return jnp.maximum(x, 0.0) + jnp.log1p(jnp.exp(-jnp.abs(x)))


def _dot(a, b, dims, prec=None):
    return lax.dot_general(a, b, (dims, ((), ())), precision=prec, preferred_element_type=F32)


NN = ((1,), (0,))
NT = ((1,), (1,))
TN = ((0,), (0,))
HI = lax.Precision.HIGHEST


def _bdot_plain(a, b, dims):
    return _dot(a.astype(BF16), b.astype(BF16), dims)


@functools.partial(jax.custom_vjp, nondiff_argnums=(2,))
def _bdot_vjp(a, b, dims):
    return _bdot_plain(a, b, dims)


def _bdot_vjp_fwd(a, b, dims):
    return _bdot_plain(a, b, dims), (a, b)


def _bdot_vjp_bwd(dims, res, ct):
    a, b = res
    if dims == NN:
        return _bdot_plain(ct, b, NT), _bdot_plain(a, ct, TN)
    assert dims == NT, dims
    return _bdot_plain(ct, b, NN), _bdot_plain(ct, a, TN)


_bdot_vjp.defvjp(_bdot_vjp_fwd, _bdot_vjp_bwd)


class Layout:
    def __init__(self, d):
        self.d = d
        self.g = 0
        self.q = 3 * d
        self.qkv = self.q + Q_W
        self.z = self.qkv + 3 * GDN_W
        self.qc = self.z + GDN_W
        self.k = self.qc + XA_W
        self.v = self.k + KV_W
        self.ab = self.v + KV_W
        self.end = self.ab + LANE
        self.pw = -(-self.end // 1024) * 1024
        self.lq, self.lk, self.lv, self.lqkv = 0, Q_W, Q_W + KV_W, Q_W + 2 * KV_W
        self.la = self.lqkv + 3 * GDN_W
        self.lz = self.la + 2 * GDN_H
        self.lqc = self.lz + GDN_W
        self.lg = self.lqc + XA_W
        self.lw = self.lg + 3 * d

    def pieces(self):
        segs = [(self.lq, self.lk, self.q), (self.lk, self.lv, self.k), (self.lv, self.lqkv, self.v),
                (self.lqkv, self.la, self.qkv), (self.la, self.lz, self.ab), (self.lz, self.lqc, self.z),
                (self.lqc, self.lg, self.qc), (self.lg, self.lw, self.g)]
        cw = _div(self.lw, N_DEV)
        out = []
        for dev in range(N_DEV):
            lo, hi = dev * cw, (dev + 1) * cw
            for ls, le, ps in segs:
                s, e = max(lo, ls), min(hi, le)
                if s < e:
                    out.append((dev, s - lo, ps + s - ls, e - s))
        return out


def pad_w_in(g, lay):
    nd, k, cw = g.shape
    tr = _tile(k, 256)
    tail = lay.ab + 2 * GDN_H

    def body(g_ref, o_ref):
        o_ref[:, pl.ds(tail, lay.pw - tail)] = jnp.zeros((tr, lay.pw - tail), o_ref.dtype)
        for dev, so, po, ln in lay.pieces():
            o_ref[:, pl.ds(po, ln)] = g_ref[dev, :, pl.ds(so, ln)]

    return pl.pallas_call(
        body, name="pad_w_in", grid=(k // tr,), in_specs=[pl.BlockSpec((nd, tr, cw), lambda i: (0, i, 0))],
        out_specs=pl.BlockSpec((tr, lay.pw), lambda i: (i, 0)), out_shape=SDS((k, lay.pw), g.dtype),
        compiler_params=_cp(("parallel",)))(g)


def unpad_dw_in(dw, lay):
    k = dw.shape[0]
    cw = _div(lay.lw, N_DEV)
    tr = _tile(k, 256)

    def body(d_ref, o_ref):
        for dev, so, po, ln in lay.pieces():
            o_ref[dev, :, pl.ds(so, ln)] = d_ref[:, pl.ds(po, ln)]

    return pl.pallas_call(
        body, name="unpad_dw_in", grid=(k // tr,), in_specs=[pl.BlockSpec((tr, lay.pw), lambda i: (i, 0))],
        out_specs=pl.BlockSpec((N_DEV, tr, cw), lambda i: (0, i, 0)), out_shape=SDS((N_DEV, k, cw), dw.dtype),
        compiler_params=_cp(("parallel",)))(dw)


def _position():
    return lax.axis_index("x"), lax.axis_index("y"), lax.axis_index("c")


def all_gather(arrs, name):
    n = len(arrs)

    def body(*refs):
        ins, outs = refs[:n], refs[n:2 * n]
        send_sems, recv_sems, local_sems = refs[2 * n:]
        x, y, c = _position()
        me, sibling = (x, y, c), (x, y, 1 - c)
        chips = [(1 - x, y), (x, 1 - y), (1 - x, 1 - y)]

        def blk(o, p):
            return o.at[4 * p[0] + 2 * p[1] + p[2]]

        def copy(i, k, block, to, src=None):
            return pltpu.make_async_remote_copy(
                src_ref=blk(outs[i], block) if src is None else src, dst_ref=blk(outs[i], block),
                send_sem=send_sems.at[i, k], recv_sem=recv_sems.at[i, k], device_id=to, device_id_type=MESH)

        started = []
        for i in range(n):
            mine = pltpu.make_async_copy(ins[i], blk(outs[i], me), local_sems.at[i])
            mine.start()
            first = [copy(i, 0, me, sibling, src=ins[i])]
            first += [copy(i, 1 + j, me, (*chip, c), src=ins[i]) for j, chip in enumerate(chips)]
            for cp in first:
                cp.start()
            started += [mine.wait] + [cp.wait_send for cp in first]
        for i in range(n):
            for j, chip in enumerate(chips):
                copy(i, 1 + j, (*chip, c), me).wait_recv()
                fwd = copy(i, 4 + j, (*chip, c), sibling)
                fwd.start()
                started.append(fwd.wait_send)
        for i in range(n):
            copy(i, 0, sibling, me).wait_recv()
            for j, chip in enumerate(chips):
                copy(i, 4 + j, (*chip, 1 - c), me).wait_recv()
        for w in started:
            w()

    return pl.pallas_call(
        body, name=name,
        out_shape=[SDS((N_DEV,) + a.shape, a.dtype) for a in arrs],
        in_specs=[ANY] * n, out_specs=[ANY] * n,
        scratch_shapes=[pltpu.SemaphoreType.DMA((n, 7)), pltpu.SemaphoreType.DMA((n, 7)), pltpu.SemaphoreType.DMA((n,))],
    )(*arrs)


class GatherJob:
    def __init__(self, arrs):
        self.ins = list(arrs)
        n = len(arrs)
        self.out_shapes = [SDS((N_DEV,) + a.shape, a.dtype) for a in arrs]
        self.scratch = [pltpu.SemaphoreType.DMA((n, 7)), pltpu.SemaphoreType.DMA((n, 7)), pltpu.SemaphoreType.DMA((n,))]

    def _ctx(self, outs, sems):
        send_sems, recv_sems, _ = sems
        x, y, c = _position()

        def blk(o, p):
            return o.at[4 * p[0] + 2 * p[1] + p[2]]

        def copy(i, k, block, to, src=None):
            return pltpu.make_async_remote_copy(
                src_ref=blk(outs[i], block) if src is None else src, dst_ref=blk(outs[i], block),
                send_sem=send_sems.at[i, k], recv_sem=recv_sems.at[i, k], device_id=to, device_id_type=MESH)

        return (x, y, c), (x, y, 1 - c), [(1 - x, y), (x, 1 - y), (1 - x, 1 - y)], blk, copy

    def start(self, ins, outs, sems):
        me, sibling, chips, blk, copy = self._ctx(outs, sems)
        for i in range(len(ins)):
            pltpu.make_async_copy(ins[i], blk(outs[i], me), sems[2].at[i]).start()
            copy(i, 0, me, sibling, src=ins[i]).start()
            for j, chip in enumerate(chips):
                copy(i, 1 + j, me, (*chip, me[2]), src=ins[i]).start()

    def mid(self, ins, outs, sems):
        me, sibling, chips, blk, copy = self._ctx(outs, sems)
        for i in range(len(ins)):
            for j, chip in enumerate(chips):
                copy(i, 1 + j, (*chip, me[2]), me).wait_recv()
                copy(i, 4 + j, (*chip, me[2]), sibling).start()

    def finish(self, ins, outs, sems):
        me, sibling, chips, blk, copy = self._ctx(outs, sems)
        for i in range(len(ins)):
            copy(i, 0, sibling, me).wait_recv()
            for j, chip in enumerate(chips):
                copy(i, 4 + j, (*chip, 1 - me[2]), me).wait_recv()
        for i in range(len(ins)):
            pltpu.make_async_copy(ins[i], blk(outs[i], me), sems[2].at[i]).wait()
            copy(i, 0, me, sibling, src=ins[i]).wait_send()
            for j, chip in enumerate(chips):
                copy(i, 1 + j, me, (*chip, me[2]), src=ins[i]).wait_send()
                copy(i, 4 + j, (*chip, me[2]), sibling).wait_send()


class ChipExchangeJob:
    mid = None

    def __init__(self, arrs):
        self.ins = list(arrs)
        n = len(arrs)
        self.out_shapes = [SDS(a.shape, a.dtype) for a in arrs]
        self.scratch = [pltpu.SemaphoreType.DMA((n, 3)), pltpu.SemaphoreType.DMA((n, 3)), pltpu.SemaphoreType.DMA((n,))]

    def _copies(self, ins, outs, sems, i, arrivals):
        send_sems, recv_sems, local_sems = sems
        x, y, c = _position()
        my_chip = 2 * x + y
        chips = [(1 - x, y), (x, 1 - y), (1 - x, 1 - y)]
        if arrivals:
            return [pltpu.make_async_remote_copy(
                src_ref=ins[i].at[my_chip], dst_ref=outs[i].at[2 * px + py], send_sem=send_sems.at[i, k],
                recv_sem=recv_sems.at[i, k], device_id=(px, py, c), device_id_type=MESH) for k, (px, py) in enumerate(chips)]
        local = pltpu.make_async_copy(ins[i].at[my_chip], outs[i].at[my_chip], local_sems.at[i])
        return local, [pltpu.make_async_remote_copy(
            src_ref=ins[i].at[2 * px + py], dst_ref=outs[i].at[my_chip], send_sem=send_sems.at[i, k],
            recv_sem=recv_sems.at[i, k], device_id=(px, py, c), device_id_type=MESH) for k, (px, py) in enumerate(chips)]

    def start(self, ins, outs, sems):
        for i in range(len(ins)):
            local, remote = self._copies(ins, outs, sems, i, False)
            local.start()
            for cp in remote:
                cp.start()

    def finish(self, ins, outs, sems):
        for i in range(len(ins)):
            for cp in self._copies(ins, outs, sems, i, True):
                cp.wait_recv()
            local, remote = self._copies(ins, outs, sems, i, False)
            for cp in remote:
                cp.wait_send()
            local.wait()


def pair_exchange(blocks, name):
    n = len(blocks)

    def body(*refs):
        ins, outs = refs[:n], refs[n:2 * n]
        send_sems, recv_sems = refs[2 * n:]
        x, y, c = _position()
        copies = [pltpu.make_async_remote_copy(src_ref=ins[i].at[2 * j + 1 - c], dst_ref=outs[i].at[j], send_sem=send_sems.at[i, j],
                                               recv_sem=recv_sems.at[i, j], device_id=(x, y, 1 - c), device_id_type=MESH)
                  for i in range(n) for j in range(4)]
        for cp in copies:
            cp.start()
        for cp in copies:
            cp.wait()

    return pl.pallas_call(
        body, name=name, out_shape=[SDS((4,) + a.shape[1:], a.dtype) for a in blocks],
        in_specs=[ANY] * n, out_specs=[ANY] * n,
        scratch_shapes=[pltpu.SemaphoreType.DMA((n, 4)), pltpu.SemaphoreType.DMA((n, 4))],
    )(*blocks)


def pair_add(blocks, other, name):
    _, r, c = blocks.shape
    tr = _tile(r, 256)
    parity = lax.axis_index("c").astype(jnp.int32).reshape(1)

    def body(par_ref, a_ref, b_ref, o_ref):
        o_ref[...] = (a_ref[...].astype(F32) + b_ref[...].astype(F32)).astype(BF16)

    spec = pl.BlockSpec((1, tr, c), lambda j, i, par: (j, i, 0))
    own = pl.BlockSpec((1, tr, c), lambda j, i, par: (2 * j + par[0], i, 0))
    return pl.pallas_call(
        body, name=name, out_shape=SDS(other.shape, BF16),
        grid_spec=pltpu.PrefetchScalarGridSpec(num_scalar_prefetch=1, grid=(4, r // tr), in_specs=[own, spec], out_specs=spec),
        compiler_params=_cp(("parallel", "parallel")))(parity, blocks, other)


def adamw(parts, w, m, v, name):
    p, r, c = parts.shape
    tr = _tile(r, 128 if c > 1024 else 256)

    def body(p_ref, w_ref, m_ref, v_ref, g_out, d_out, m_out, v_out):
        g = p_ref[0].astype(F32)
        for j in range(1, p):
            g = g + p_ref[j].astype(F32)
        mn = ADAM_B1 * m_ref[...] + (1.0 - ADAM_B1) * g
        vn = ADAM_B2 * v_ref[...] + (1.0 - ADAM_B2) * jnp.square(g)
        m_hat = mn / (1.0 - ADAM_B1 ** ADAM_STEP)
        v_hat = vn / (1.0 - ADAM_B2 ** ADAM_STEP)
        g_out[...] = g
        d_out[...] = -ADAM_LR * (m_hat / (jnp.sqrt(v_hat) + ADAM_EPS) + ADAM_WD * w_ref[...])
        m_out[...] = mn
        v_out[...] = vn

    spec = pl.BlockSpec((tr, c), lambda i: (i, 0))
    return pl.pallas_call(
        body, name=name, grid=(r // tr,),
        in_specs=[pl.BlockSpec((p, tr, c), lambda i: (0, i, 0)), spec, spec, spec],
        out_specs=[spec] * 4, out_shape=[SDS((r, c), F32)] * 4, compiler_params=_cp(("parallel",)))(parts, w, m, v)


def matmul(a, b, *, mode, out_dtype, name, tm=1024, tn=1024, tk=512, a_relu2=False, resid=None, relu2_grad_of=None,
           b_cols=False, out_cols=False, side=None):
    if b_cols:
        nb, brows, bc = b.shape
        bshape = (brows, nb * bc)
    else:
        bshape = b.shape
    if mode == "nn":
        (m, k), (k2, n) = a.shape, bshape
    elif mode == "nt":
        (m, k), (n, k2) = a.shape, bshape
    else:
        (k, m), (k2, n) = a.shape, bshape
    assert k == k2, (a.shape, b.shape, mode)
    tm, tn, tk = _tile(m, tm), _tile(n, tn), _tile(k, tk)
    if b_cols and mode == "nn":
        tn = _tile(bc, tn)
    if b_cols and mode == "nt":
        tk = _tile(bc, tk)
    if out_cols:
        oc = _div(n, N_DEV)
        tn = _tile(oc, tn)
    nk = k // tk
    ni, nj = m // tm, n // tn
    dims = {"nn": NN, "nt": NT, "tn": TN}[mode]
    extras = [e for e in (resid, relu2_grad_of) if e is not None]
    n_side = len(side.ins) if side is not None else 0

    def body(*refs):
        a_ref, b_ref = refs[:2]
        e_refs = refs[2:2 + len(extras)]
        o_ref = refs[2 + len(extras) + n_side]
        acc_ref = refs[3 + len(extras) + 2 * n_side] if nk > 1 else None
        if side is not None:
            s_ins = refs[2 + len(extras):2 + len(extras) + n_side]
            s_outs = refs[3 + len(extras) + n_side:3 + len(extras) + 2 * n_side]
            s_sems = refs[len(refs) - len(side.scratch):]
            step = (pl.program_id(0) * nj + pl.program_id(1)) * nk + pl.program_id(2)
            pl.when(step == 0)(lambda: side.start(s_ins, s_outs, s_sems))
            if side.mid is not None:
                pl.when(step == (ni * nj * nk) // 2)(lambda: side.mid(s_ins, s_outs, s_sems))

        def operands():
            av = a_ref[...]
            if a_relu2:
                av = jnp.square(jnp.maximum(av.astype(F32), 0.0))
            return av.astype(BF16), b_ref[...].astype(BF16)

        def finish(r):
            e = list(e_refs)
            if resid is not None:
                r = r + e.pop(0)[...]
            if relu2_grad_of is not None:
                r = r * (2.0 * jnp.maximum(e.pop(0)[...], 0.0))
            o_ref[...] = r.astype(out_dtype)

        if nk == 1:
            av, bv = operands()
            finish(_dot(av, bv, dims))
        else:
            kk = pl.program_id(2)

            @pl.when(kk == 0)
            def _():
                acc_ref[...] = jnp.zeros_like(acc_ref)

            av, bv = operands()
            acc_ref[...] += _dot(av, bv, dims)

            @pl.when(kk == nk - 1)
            def _():
                finish(acc_ref[...])

        if side is not None:
            pl.when(step == ni * nj * nk - 1)(lambda: side.finish(s_ins, s_outs, s_sems))

    a_spec = {"nn": pl.BlockSpec((tm, tk), lambda i, j, kk: (i, kk)),
              "nt": pl.BlockSpec((tm, tk), lambda i, j, kk: (i, kk)),
              "tn": pl.BlockSpec((tk, tm), lambda i, j, kk: (kk, i))}[mode]
    if not b_cols:
        b_spec = {"nn": pl.BlockSpec((tk, tn), lambda i, j, kk: (kk, j)),
                  "nt": pl.BlockSpec((tn, tk), lambda i, j, kk: (j, kk)),
                  "tn": pl.BlockSpec((tk, tn), lambda i, j, kk: (kk, j))}[mode]
    elif mode == "nn":
        per = bc // tn
        b_spec = pl.BlockSpec((None, tk, tn), lambda i, j, kk: (j // per, kk, j % per))
    else:
        assert mode == "nt", mode
        per = bc // tk
        b_spec = pl.BlockSpec((None, tn, tk), lambda i, j, kk: (kk // per, j, kk % per))
    e_spec = pl.BlockSpec((tm, tn), lambda i, j, kk: (i, j))
    if out_cols:
        pero = oc // tn
        o_spec = pl.BlockSpec((None, tm, tn), lambda i, j, kk: (j // pero, i, j % pero))
        o_shape = SDS((N_DEV, m, oc), out_dtype)
    else:
        o_spec, o_shape = e_spec, SDS((m, n), out_dtype)
    res = pl.pallas_call(
        body, name=name, grid=(ni, nj, nk),
        in_specs=[a_spec, b_spec] + [e_spec] * len(extras) + [ANY] * n_side,
        out_specs=[o_spec] + [ANY] * n_side, out_shape=[o_shape] + (side.out_shapes if side is not None else []),
        scratch_shapes=([pltpu.VMEM((tm, tn), F32)] if nk > 1 else []) + (side.scratch if side is not None else []),
        compiler_params=_cp(("arbitrary", "arbitrary", "arbitrary")))(a, b, *extras, *(side.ins if side is not None else []))
    return res if side is not None else res[0]


def rmsnorm_fwd(x, g, name):
    s, d = x.shape
    tm = _tile(s, 256)

    def body(x_ref, g_ref, o_ref):
        xv = x_ref[...]
        r = lax.rsqrt(jnp.mean(xv * xv, axis=-1, keepdims=True) + RMS_EPS)
        o_ref[...] = (xv * r * g_ref[...]).astype(BF16)

    row = pl.BlockSpec((tm, d), lambda i: (i, 0))
    return pl.pallas_call(body, name=name, grid=(s // tm,), in_specs=[row, pl.BlockSpec((1, d), lambda i: (0, 0))],
                          out_specs=row, out_shape=SDS((s, d), BF16), compiler_params=_cp(("parallel",)))(x, g)


def _rms_bwd_rows(xv, gv, dy):
    r = lax.rsqrt(jnp.mean(xv * xv, axis=-1, keepdims=True) + RMS_EPS)
    xh = xv * r
    dxh = dy * gv
    dx = r * (dxh - xh * jnp.mean(dxh * xh, axis=-1, keepdims=True))
    return dx, jnp.sum(dy * xh, axis=0, keepdims=True)


def rmsnorm_bwd(x, g, dn, resid, name):
    s, d = x.shape
    tm = _tile(s, 256)
    has_r = resid is not None

    def body(*refs):
        x_ref, g_ref, dn_ref = refs[:3]
        dx_ref, dg_ref = refs[-2:]
        dx, part = _rms_bwd_rows(x_ref[...], g_ref[...], dn_ref[...].astype(F32))
        if has_r:
            dx = dx + refs[3][...]
        dx_ref[...] = dx

        @pl.when(pl.program_id(0) == 0)
        def _():
            dg_ref[...] = jnp.zeros_like(dg_ref)

        dg_ref[...] += part

    row = pl.BlockSpec((tm, d), lambda i: (i, 0))
    vec = pl.BlockSpec((1, d), lambda i: (0, 0))
    ins = [x, g, dn] + ([resid] if has_r else [])
    return pl.pallas_call(body, name=name, grid=(s // tm,), in_specs=[row, vec, row] + ([row] if has_r else []),
                          out_specs=[row, vec], out_shape=[SDS((s, d), F32), SDS((1, d), F32)],
                          compiler_params=_cp(("arbitrary",)))(*ins)


def final_norm_loss(h, g, tgt, name):
    s, d = h.shape
    tm = _tile(s, 256)

    def body(h_ref, g_ref, t_ref, dh_ref, dg_ref, l_ref):
        xv, gv = h_ref[...], g_ref[...]
        r = lax.rsqrt(jnp.mean(xv * xv, axis=-1, keepdims=True) + RMS_EPS)
        e = xv * r * gv - t_ref[...]
        lpart = 0.5 * jnp.sum(jnp.mean(e * e, axis=-1, keepdims=True), axis=0, keepdims=True)
        dx, part = _rms_bwd_rows(xv, gv, e * (1.0 / d))
        dh_ref[...] = dx

        @pl.when(pl.program_id(0) == 0)
        def _():
            dg_ref[...] = jnp.zeros_like(dg_ref)
            l_ref[...] = jnp.zeros_like(l_ref)

        dg_ref[...] += part
        l_ref[...] += jnp.broadcast_to(lpart, l_ref.shape)

    row = pl.BlockSpec((tm, d), lambda i: (i, 0))
    vec = pl.BlockSpec((1, d), lambda i: (0, 0))
    lsp = pl.BlockSpec((1, LANE), lambda i: (0, 0))
    return pl.pallas_call(body, name=name, grid=(s // tm,), in_specs=[row, vec, row], out_specs=[row, vec, lsp],
                          out_shape=[SDS((s, d), F32), SDS((1, d), F32), SDS((1, LANE), F32)],
                          compiler_params=_cp(("arbitrary",)))(h, g, tgt)


def merge(p_all, ya, yb, yc, wa, wb, wc, dm, lay, name):
    s, d = ya.shape[0], lay.d
    wcols = wa.shape[2]
    tm, tn = _tile(s, 512), _tile(wcols, 512)
    nj, per = d // tn, wcols // tn
    bwd = dm is not None

    def body(*refs):
        ga, gb, gc, ya_r, yb_r, yc_r, wa_r, wb_r, wc_r = refs[:9]
        ts = [_dot(y[...], w[...], NN) for y, w in ((ya_r, wa_r), (yb_r, wb_r), (yc_r, wc_r))]
        gs = [_sigmoid(g[...]) for g in (ga, gb, gc)]
        if not bwd:
            refs[9][...] = (gs[0] * ts[0] + gs[1] * ts[1] + gs[2] * ts[2]).astype(BF16)
        else:
            dmv = refs[9][...]
            for i in range(3):
                refs[10 + i][...] = (dmv * ts[i] * (gs[i] * (1.0 - gs[i]))).astype(BF16)
                refs[13 + i][...] = (dmv * gs[i]).astype(BF16)

    gate_specs = [pl.BlockSpec((tm, tn), lambda i, j, b=b: (i, b * nj + j)) for b in range(3)]
    y_specs = [pl.BlockSpec((tm, y.shape[1]), lambda i, j: (i, 0)) for y in (ya, yb, yc)]
    w_specs = [pl.BlockSpec((None, w.shape[1], tn), lambda i, j: (j // per, 0, j % per)) for w in (wa, wb, wc)]
    o_spec = pl.BlockSpec((tm, tn), lambda i, j: (i, j))
    n_out = 6 if bwd else 1
    out = pl.pallas_call(
        body, name=name, grid=(s // tm, nj),
        in_specs=gate_specs + y_specs + w_specs + ([o_spec] if bwd else []),
        out_specs=[o_spec] * n_out, out_shape=[SDS((s, d), BF16)] * n_out,
        compiler_params=_cp(("parallel", "parallel")))(p_all, p_all, p_all, ya, yb, yc, wa, wb, wc, *([dm] if bwd else []))
    return out if bwd else out[0]


SWA_PAIRS = SWA_G // 2


def _swa_probs(q, kc, sink, first):
    s = _dot(q, kc, NT) * (SWA_HD ** -0.5)
    qi = lax.broadcasted_iota(jnp.int32, s.shape, 0) % SWA_W
    kj = lax.broadcasted_iota(jnp.int32, s.shape, 1)
    mask = (kj > qi) & (kj <= qi + SWA_W) & ((kj >= SWA_W) | jnp.logical_not(first))
    s = jnp.where(mask, s, NEG)
    m = jnp.maximum(jnp.max(s, axis=-1, keepdims=True), sink)
    p = jnp.exp(s - m)
    es = jnp.exp(sink - m)
    denom = jnp.sum(p, axis=-1, keepdims=True) + es
    return p / denom, es / denom


def _swa_stack(ref, h):
    return jnp.concatenate([ref[:, pl.ds((h * SWA_PAIRS + p) * LANE, LANE)] for p in range(SWA_PAIRS)], axis=0)


def _swa_unstack(ref, h, val):
    for p in range(SWA_PAIRS):
        ref[:, pl.ds((h * SWA_PAIRS + p) * LANE, LANE)] = val[p * SWA_W:(p + 1) * SWA_W]


def _swa_sink_col(sk_ref, h, second):
    pair = lax.broadcasted_iota(jnp.int32, (SWA_PAIRS * SWA_W, 1), 0) // SWA_W
    col = jnp.zeros((SWA_PAIRS * SWA_W, 1), F32)
    for p in range(SWA_PAIRS):
        hh = h * SWA_G + 2 * p + second
        col = jnp.where(pair == p, sk_ref[0:1, hh:hh + 1], col)
    return col


def _swa_kv_tiles(cur_ref, prev_ref, h):
    t = jnp.concatenate([prev_ref[...], cur_ref[...]], axis=0)
    lane = lax.broadcasted_iota(jnp.int32, t.shape, 1)
    moved = pltpu.roll(t, SWA_HD, axis=1)
    low, high = (t, moved) if h == 0 else (moved, t)
    return jnp.where(lane < SWA_HD, low, 0.0).astype(BF16), jnp.where(lane >= SWA_HD, high, 0.0).astype(BF16)


def _swa_kv_grad(g_low, g_high, h):
    lane = lax.broadcasted_iota(jnp.int32, g_low.shape, 1)
    if h == 0:
        return jnp.where(lane < SWA_HD, g_low + pltpu.roll(g_high, SWA_HD, axis=1), 0.0)
    return jnp.where(lane >= SWA_HD, pltpu.roll(g_low, SWA_HD, axis=1) + g_high, 0.0)


def _swa_specs(lay):
    w = SWA_W
    q_spec = pl.BlockSpec((w, Q_W), lambda n: (n, _div(lay.q, Q_W)))
    cur = lambda off: pl.BlockSpec((w, KV_W), lambda n: (n, _div(off, KV_W)))
    prev = lambda off: pl.BlockSpec((w, KV_W), lambda n: (jnp.maximum(n - 1, 0), _div(off, KV_W)))
    return q_spec, cur(lay.k), prev(lay.k), cur(lay.v), prev(lay.v)


def swa_fwd(p_all, sinks, lay, name):
    s = p_all.shape[0]
    nb = _div(s, SWA_W)

    def body(q_ref, kc_ref, kp_ref, vc_ref, vp_ref, sk_ref, o_ref):
        first = pl.program_id(0) == 0
        for h in range(SWA_HKV):
            ks = _swa_kv_tiles(kc_ref, kp_ref, h)
            vs = _swa_kv_tiles(vc_ref, vp_ref, h)
            q = _swa_stack(q_ref, h).astype(BF16)
            ps = [_swa_probs(q, ks[e], _swa_sink_col(sk_ref, h, e), first)[0] for e in range(2)]
            o = _dot(ps[0].astype(BF16), vs[0], NN) + _dot(ps[1].astype(BF16), vs[1], NN)
            _swa_unstack(o_ref, h, o.astype(BF16))

    q_spec, kc_s, kp_s, vc_s, vp_s = _swa_specs(lay)
    return pl.pallas_call(
        body, name=name, grid=(nb,),
        in_specs=[q_spec, kc_s, kp_s, vc_s, vp_s, pl.BlockSpec(sinks.shape, lambda n: (0, 0))],
        out_specs=pl.BlockSpec((SWA_W, Q_W), lambda n: (n, 0)), out_shape=SDS((s, Q_W), BF16),
        compiler_params=_cp(("parallel",)))(p_all, p_all, p_all, p_all, p_all, sinks)


def swa_bwd(p_all, sinks, dy, lay, name):
    s = p_all.shape[0]
    nb = _div(s, SWA_W)
    w = SWA_W

    def body(q_ref, kc_ref, kp_ref, vc_ref, vp_ref, sk_ref, do_ref, dq_ref, dk_ref, dv_ref, ds_ref, kcar, vcar):
        n = pl.program_id(0)
        first = n == 0

        @pl.when(first)
        def _():
            kcar[...] = jnp.zeros_like(kcar)
            vcar[...] = jnp.zeros_like(vcar)
            ds_ref[...] = jnp.zeros_like(ds_ref)

        @pl.when(n < nb)
        def _():
            lane = lax.broadcasted_iota(jnp.int32, (1, LANE), 1)
            dsink = jnp.zeros((1, LANE), F32)
            dk_tile = jnp.zeros((2 * w, KV_W), F32)
            dv_tile = jnp.zeros((2 * w, KV_W), F32)
            for h in range(SWA_HKV):
                ks = _swa_kv_tiles(kc_ref, kp_ref, h)
                vs = _swa_kv_tiles(vc_ref, vp_ref, h)
                q = _swa_stack(q_ref, h).astype(BF16)
                do = _swa_stack(do_ref, h).astype(BF16)
                dq = jnp.zeros((SWA_PAIRS * w, LANE), F32)
                dks, dvs = [], []
                for e in range(2):
                    p, psink = _swa_probs(q, ks[e], _swa_sink_col(sk_ref, h, e), first)
                    dp = _dot(do, vs[e], NT)
                    dvs.append(_dot(p.astype(BF16), do, TN))
                    rs = jnp.sum(dp * p, axis=-1, keepdims=True)
                    dsb = (p * (dp - rs) * (SWA_HD ** -0.5)).astype(BF16)
                    dq = dq + _dot(dsb, ks[e], NN)
                    dks.append(_dot(dsb, q, TN))
                    psr = psink * rs
                    for pr in range(SWA_PAIRS):
                        hh = h * SWA_G + 2 * pr + e
                        dsink = dsink + jnp.where(lane == hh, -jnp.sum(psr[pr * w:(pr + 1) * w], axis=0, keepdims=True), 0.0)
                _swa_unstack(dq_ref, h, dq.astype(BF16))
                dk_tile = dk_tile + _swa_kv_grad(dks[0], dks[1], h)
                dv_tile = dv_tile + _swa_kv_grad(dvs[0], dvs[1], h)
            dk_ref[...] = (kcar[...] + dk_tile[:w]).astype(BF16)
            dv_ref[...] = (vcar[...] + dv_tile[:w]).astype(BF16)
            kcar[...] = dk_tile[w:]
            vcar[...] = dv_tile[w:]
            ds_ref[...] += dsink

        @pl.when(n == nb)
        def _():
            dk_ref[...] = kcar[...].astype(BF16)
            dv_ref[...] = vcar[...].astype(BF16)

    last = nb - 1
    q_spec = pl.BlockSpec((w, Q_W), lambda n: (jnp.minimum(n, last), _div(lay.q, Q_W)))
    cur = lambda off: pl.BlockSpec((w, KV_W), lambda n: (jnp.minimum(n, last), _div(off, KV_W)))
    prev = lambda off: pl.BlockSpec((w, KV_W), lambda n: (jnp.clip(n - 1, 0, last), _div(off, KV_W)))
    row = pl.BlockSpec((w, Q_W), lambda n: (jnp.minimum(n, last), 0))
    kv_out = pl.BlockSpec((w, KV_W), lambda n: (jnp.maximum(n - 1, 0), 0))
    return pl.pallas_call(
        body, name=name, grid=(nb + 1,),
        in_specs=[q_spec, cur(lay.k), prev(lay.k), cur(lay.v), prev(lay.v), pl.BlockSpec(sinks.shape, lambda n: (0, 0)), row],
        out_specs=[row, kv_out, kv_out, pl.BlockSpec((1, LANE), lambda n: (0, 0))],
        out_shape=[SDS((s, Q_W), BF16), SDS((s, KV_W), BF16), SDS((s, KV_W), BF16), SDS((1, LANE), F32)],
        scratch_shapes=[pltpu.VMEM((w, KV_W), F32), pltpu.VMEM((w, KV_W), F32)],
        compiler_params=_cp(("arbitrary",)))(p_all, p_all, p_all, p_all, p_all, sinks, dy)


def _xa_probs(q, mk):
    s = _dot(q, mk, NT) * (XA_D ** -0.5)
    p = jnp.exp(s - jnp.max(s, axis=-1, keepdims=True))
    return p / jnp.sum(p, axis=-1, keepdims=True)


def xattn_fwd(p_all, mkv, lay, name):
    s, nm = p_all.shape[0], mkv.shape[0]
    tm = _tile(s, 512)

    def body(q_ref, mkv_ref, o_ref):
        for h in range(XA_H):
            cols = pl.ds(h * XA_D, XA_D)
            p = _xa_probs(q_ref[:, cols].astype(BF16), mkv_ref[:, cols])
            o_ref[:, cols] = _dot(p.astype(BF16), mkv_ref[:, pl.ds(XA_W + h * XA_D, XA_D)], NN).astype(BF16)

    return pl.pallas_call(
        body, name=name, grid=(s // tm,),
        in_specs=[pl.BlockSpec((tm, XA_W), lambda i: (i, _div(lay.qc, XA_W))), pl.BlockSpec((nm, 2 * XA_W), lambda i: (0, 0))],
        out_specs=pl.BlockSpec((tm, XA_W), lambda i: (i, 0)), out_shape=SDS((s, XA_W), BF16),
        compiler_params=_cp(("parallel",)))(p_all, mkv)


def xattn_bwd(p_all, mkv, dy, lay, name):
    s, nm = p_all.shape[0], mkv.shape[0]
    tm = _tile(s, 512)

    def body(q_ref, mkv_ref, do_ref, dq_ref, dmkv_ref):
        @pl.when(pl.program_id(0) == 0)
        def _():
            dmkv_ref[...] = jnp.zeros_like(dmkv_ref)

        for h in range(XA_H):
            cols = pl.ds(h * XA_D, XA_D)
            vcols = pl.ds(XA_W + h * XA_D, XA_D)
            q = q_ref[:, cols].astype(BF16)
            do = do_ref[:, cols].astype(BF16)
            p = _xa_probs(q, mkv_ref[:, cols])
            dp = _dot(do, mkv_ref[:, vcols], NT)
            dmkv_ref[:, vcols] += _dot(p.astype(BF16), do, TN)
            dsb = (p * (dp - jnp.sum(dp * p, axis=-1, keepdims=True)) * (XA_D ** -0.5)).astype(BF16)
            dq_ref[:, cols] = _dot(dsb, mkv_ref[:, cols], NN).astype(BF16)
            dmkv_ref[:, cols] += _dot(dsb, q, TN)

    row = pl.BlockSpec((tm, XA_W), lambda i: (i, 0))
    full = pl.BlockSpec((nm, 2 * XA_W), lambda i: (0, 0))
    return pl.pallas_call(
        body, name=name, grid=(s // tm,),
        in_specs=[pl.BlockSpec((tm, XA_W), lambda i: (i, _div(lay.qc, XA_W))), full, row],
        out_specs=[row, full], out_shape=[SDS((s, XA_W), BF16), SDS((nm, 2 * XA_W), F32)],
        compiler_params=_cp(("arbitrary",)))(p_all, mkv, dy)


def _shift_down(cur, prev8, s):
    cat = jnp.concatenate([prev8, cur[0:8]], axis=0)
    return pltpu.roll(cur, s, axis=0), pltpu.roll(cat, s, axis=0)[8:16]


def _shift_up(cur, next8, s):
    tm = cur.shape[0]
    cat = jnp.concatenate([cur[tm - 8:tm], next8], axis=0)
    return pltpu.roll(cur, tm - s, axis=0), pltpu.roll(cat, 16 - s, axis=0)[0:8]


def gdn_conv_fwd(p_all, conv_w, lay, name):
    s = p_all.shape[0]
    tm = _tile(s, 512)
    c0 = _div(lay.qkv, GDN_W)

    def body(x_ref, prev_ref, w_ref, o_ref):
        cur = x_ref[...]
        prev8 = jnp.where(pl.program_id(1) > 0, prev_ref[...], 0.0)
        main = w_ref[GDN_CONV - 1:GDN_CONV, :] * cur
        top = w_ref[GDN_CONV - 1:GDN_CONV, :] * cur[0:8]
        for sft in range(1, GDN_CONV):
            wi = w_ref[GDN_CONV - 1 - sft:GDN_CONV - sft, :]
            a, b = _shift_down(cur, prev8, sft)
            main = main + wi * a
            top = top + wi * b
        o_ref[...] = main
        o_ref[0:8, :] = top

    return pl.pallas_call(
        body, name=name, grid=(3, s // tm),
        in_specs=[pl.BlockSpec((tm, GDN_W), lambda c, i: (i, c0 + c)),
                  pl.BlockSpec((8, GDN_W), lambda c, i: (jnp.maximum(i * (tm // 8) - 1, 0), c0 + c)),
                  pl.BlockSpec((GDN_CONV, GDN_W), lambda c, i: (0, c))],
        out_specs=pl.BlockSpec((tm, GDN_W), lambda c, i: (i, c)), out_shape=SDS((s, 3 * GDN_W), F32),
        compiler_params=_cp(("parallel", "parallel")))(p_all, p_all, conv_w)


def gdn_conv_bwd(p_all, conv_w, dxc, lay, name):
    s = p_all.shape[0]
    tm = _tile(s, 512)
    c0 = _div(lay.qkv, GDN_W)
    nt = s // tm

    def body(x_ref, prev_ref, d_ref, next_ref, w_ref, dx_ref, dw_ref):
        i = pl.program_id(1)
        cur, d = x_ref[...], d_ref[...]
        prev8 = jnp.where(i > 0, prev_ref[...], 0.0)
        next8 = jnp.where(i < nt - 1, next_ref[...], 0.0)
        row = lax.broadcasted_iota(jnp.int32, (tm, 1), 0)
        main = w_ref[GDN_CONV - 1:GDN_CONV, :] * d
        bot = w_ref[GDN_CONV - 1:GDN_CONV, :] * d[tm - 8:tm]
        dws = [jnp.sum(d * cur, axis=0, keepdims=True)]
        for sft in range(1, GDN_CONV):
            wi = w_ref[GDN_CONV - 1 - sft:GDN_CONV - sft, :]
            a, b = _shift_up(d, next8, sft)
            main = main + wi * a
            bot = bot + wi * b
            xa, xb = _shift_down(cur, prev8, sft)
            dws.append(jnp.sum(jnp.where(row >= 8, d * xa, 0.0), axis=0, keepdims=True)
                       + jnp.sum(d[0:8] * xb, axis=0, keepdims=True))
        dx_ref[...] = main.astype(BF16)
        dx_ref[tm - 8:tm, :] = bot.astype(BF16)

        @pl.when(i == 0)
        def _():
            dw_ref[...] = jnp.zeros_like(dw_ref)

        for sft in range(GDN_CONV):
            dw_ref[GDN_CONV - 1 - sft:GDN_CONV - sft, :] += dws[sft]

    return pl.pallas_call(
        body, name=name, grid=(3, nt),
        in_specs=[pl.BlockSpec((tm, GDN_W), lambda c, i: (i, c0 + c)),
                  pl.BlockSpec((8, GDN_W), lambda c, i: (jnp.maximum(i * (tm // 8) - 1, 0), c0 + c)),
                  pl.BlockSpec((tm, GDN_W), lambda c, i: (i, c)),
                  pl.BlockSpec((8, GDN_W), lambda c, i: (jnp.minimum((i + 1) * (tm // 8), s // 8 - 1), c)),
                  pl.BlockSpec((GDN_CONV, GDN_W), lambda c, i: (0, c))],
        out_specs=[pl.BlockSpec((tm, GDN_W), lambda c, i: (i, c)), pl.BlockSpec((GDN_CONV, GDN_W), lambda c, i: (0, c))],
        out_shape=[SDS((s, 3 * GDN_W), BF16), SDS((GDN_CONV, 3 * GDN_W), F32)],
        compiler_params=_cp(("parallel", "arbitrary")))(p_all, p_all, dxc, dxc, conv_w)


def _gdn_chunk(xq, xk, xv, ab, gp, bdot=_bdot_plain):
    c = GDN_C
    nc = xq.shape[0] // c
    lane = lax.broadcasted_iota(jnp.int32, (c, LANE), 1)
    row = lax.broadcasted_iota(jnp.int32, (c, c), 0)
    col = lax.broadcasted_iota(jnp.int32, (c, c), 1)
    g_tile = -jnp.exp(gp[0:1, :]) * _softplus(ab + gp[1:2, :])
    b_tile = _sigmoid(ab)
    tri = (row >= col).astype(F32)
    qa, ka, va = _silu(xq), _silu(xk), _silu(xv)
    items = []
    for ci in range(nc):
        rs = slice(ci * c, (ci + 1) * c)
        gcum = _dot(tri, g_tile[rs], NN, HI)
        gcum_t = gcum.T
        for h in range(GDN_H):
            hs = slice(h * GDN_D, (h + 1) * GDN_D)
            q, k, v = qa[rs, hs], ka[rs, hs], va[rs, hs]
            q = q * lax.rsqrt(jnp.sum(q * q, axis=-1, keepdims=True) + L2_EPS) * (GDN_D ** -0.5)
            k = k * lax.rsqrt(jnp.sum(k * k, axis=-1, keepdims=True) + L2_EPS)
            gc = jnp.sum(jnp.where(lane == h, gcum, 0.0), axis=1, keepdims=True)
            beta = jnp.sum(jnp.where(lane == GDN_H + h, b_tile[rs], 0.0), axis=1, keepdims=True)
            decay = jnp.exp(jnp.where(row >= col, gc - gcum_t[h:h + 1, :], NEG))
            items.append((q, k, v, gc, beta, decay))
    kks = [bdot(k, k, NT) for (_, k, _, _, _, _) in items]
    pws = [-jnp.where(row > col, it[4] * kk * it[5], 0.0) for it, kk in zip(items, kks)]
    nns = list(pws)
    for _ in range(5):
        pws = [bdot(p, p, NN) for p in pws]
        nns = [n + p + bdot(n, p, NN) for n, p in zip(nns, pws)]
    qks = [bdot(q, k, NT) for (q, k, _, _, _, _) in items]
    out = []
    for (q, k, v, gc, beta, decay), n, qk in zip(items, nns, qks):
        eg = jnp.exp(gc)
        vb = v * beta
        kbe = k * (beta * eg)
        gl = gc[c - 1:c, :]
        out.append((vb + bdot(n, vb, NN), kbe + bdot(n, kbe, NN), q * eg, k * jnp.exp(gl - gc), qk * decay, jnp.exp(gl)))
    return [out[ci * GDN_H:(ci + 1) * GDN_H] for ci in range(nc)]


GDN_CPS = 4


def _gdn_pre_specs(lay, t):
    xspec = lambda j: pl.BlockSpec((t, GDN_W), lambda n, j=j: (n, j))
    return [xspec(0), xspec(1), xspec(2), pl.BlockSpec((t, LANE), lambda n: (n, _div(lay.ab, LANE))),
            pl.BlockSpec((8, LANE), lambda n: (0, 0))]


def gdn_pre_fwd(xc, p_all, gp, lay, name):
    s = xc.shape[0]
    c = GDN_C
    n = _div(s, c)
    cps = _tile(n, GDN_CPS)
    t = cps * c

    def body(xq, xk, xv, ab, gp_ref, u_ref, w_ref, qd_ref, kd_ref, qk_ref, gl_ref):
        lane = lax.broadcasted_iota(jnp.int32, (1, LANE), 1)
        chunks = _gdn_chunk(xq[...], xk[...], xv[...], ab[...], gp_ref[...])
        for ci, heads in enumerate(chunks):
            rs = pl.ds(ci * c, c)
            gl_row = jnp.zeros((1, LANE), F32)
            for h, (u, w, qd, kd, qk, gl) in enumerate(heads):
                hs = pl.ds(h * GDN_D, GDN_D)
                u_ref[rs, hs] = u
                w_ref[rs, hs] = w.astype(BF16)
                qd_ref[rs, hs] = qd.astype(BF16)
                kd_ref[rs, hs] = kd.astype(BF16)
                qk_ref[rs, pl.ds(h * c, c)] = qk.astype(BF16)
                gl_row = gl_row + jnp.where(lane == h, gl, 0.0)
            gl_ref[ci] = gl_row

    row = pl.BlockSpec((t, GDN_W), lambda n: (n, 0))
    return pl.pallas_call(
        body, name=name, grid=(n // cps,), in_specs=_gdn_pre_specs(lay, t),
        out_specs=[row, row, row, row, pl.BlockSpec((t, GDN_H * c), lambda n: (n, 0)), pl.BlockSpec((cps, 1, LANE), lambda n: (n, 0, 0))],
        out_shape=[SDS((s, GDN_W), F32), SDS((s, GDN_W), BF16), SDS((s, GDN_W), BF16), SDS((s, GDN_W), BF16),
                   SDS((s, GDN_H * c), BF16), SDS((n, 1, LANE), F32)],
        compiler_params=_cp(("parallel",)))(xc, xc, xc, p_all, gp)


def gdn_pre_bwd(xc, p_all, gp, du, dw, dqd, dkd, dqk, dgl, lay, name):
    s = xc.shape[0]
    c = GDN_C
    n = _div(s, c)
    cps = _tile(n, GDN_CPS)
    t = cps * c
    chunk = functools.partial(_gdn_chunk, bdot=_bdot_vjp)

    def body(xq, xk, xv, ab, gp_ref, du_r, dw_r, dqd_r, dkd_r, dqk_r, dgl_r, dxc_ref, dab_ref, dgp_ref):
        lane = lax.broadcasted_iota(jnp.int32, (1, LANE), 1)
        _, vjp = jax.vjp(chunk, xq[...], xk[...], xv[...], ab[...], gp_ref[...])
        cts = []
        for ci in range(cps):
            rs = pl.ds(ci * c, c)
            heads = []
            for h in range(GDN_H):
                hs = pl.ds(h * GDN_D, GDN_D)
                dgl_h = jnp.sum(jnp.where(lane == h, dgl_r[ci], 0.0), axis=1, keepdims=True)
                heads.append((du_r[rs, hs], dw_r[rs, hs], dqd_r[rs, hs], dkd_r[rs, hs], dqk_r[rs, pl.ds(h * c, c)], dgl_h))
            cts.append(heads)
        dq, dk, dv, dab, dgp = vjp(cts)
        dxc_ref[:, pl.ds(0, GDN_W)] = dq
        dxc_ref[:, pl.ds(GDN_W, GDN_W)] = dk
        dxc_ref[:, pl.ds(2 * GDN_W, GDN_W)] = dv
        dab_ref[...] = dab.astype(BF16)

        @pl.when(pl.program_id(0) == 0)
        def _():
            dgp_ref[...] = jnp.zeros_like(dgp_ref)

        dgp_ref[...] += dgp

    row = pl.BlockSpec((t, GDN_W), lambda n: (n, 0))
    return pl.pallas_call(
        body, name=name, grid=(n // cps,),
        in_specs=_gdn_pre_specs(lay, t) + [row, row, row, row, pl.BlockSpec((t, GDN_H * c), lambda n: (n, 0)),
                                           pl.BlockSpec((cps, 1, LANE), lambda n: (n, 0, 0))],
        out_specs=[pl.BlockSpec((t, 3 * GDN_W), lambda n: (n, 0)), pl.BlockSpec((t, LANE), lambda n: (n, 0)),
                   pl.BlockSpec((8, LANE), lambda n: (0, 0))],
        out_shape=[SDS((s, 3 * GDN_W), F32), SDS((s, LANE), BF16), SDS((8, LANE), F32)],
        compiler_params=_cp(("arbitrary",)))(xc, xc, xc, p_all, gp, du, dw, dqd, dkd, dqk, dgl)


def _lane_scalar(row, h):
    lane = lax.broadcasted_iota(jnp.int32, row.shape, 1)
    return jnp.sum(jnp.where(lane == h, row, 0.0), axis=1, keepdims=True)


def gdn_scan_fwd(u, w, qd, kd, qk, gl, name):
    s = u.shape[0]
    c = GDN_C
    n = _div(s, c)

    def body(u_r, w_r, qd_r, kd_r, qk_r, gl_r, o_ref, s_ref, st):
        @pl.when(pl.program_id(0) == 0)
        def _():
            st[...] = jnp.zeros_like(st)

        s_ref[0] = st[...]
        heads = range(GDN_H)
        hs = [pl.ds(h * GDN_D, GDN_D) for h in heads]
        sh = [st[hs[h], :] for h in heads]
        shb = [x.astype(BF16) for x in sh]
        ws = [_dot(w_r[:, hs[h]], shb[h], NN) for h in heads]
        qs = [_dot(qd_r[:, hs[h]], shb[h], NN) for h in heads]
        vb = [(u_r[:, hs[h]] - ws[h]).astype(BF16) for h in heads]
        ov = [_dot(qk_r[:, pl.ds(h * c, c)], vb[h], NN) for h in heads]
        kv = [_dot(kd_r[:, hs[h]], vb[h], TN) for h in heads]
        for h in heads:
            o_ref[:, hs[h]] = qs[h] + ov[h]
            st[hs[h], :] = sh[h] * _lane_scalar(gl_r[0], h) + kv[h]

    row = pl.BlockSpec((c, GDN_W), lambda i: (i, 0))
    return pl.pallas_call(
        body, name=name, grid=(n,),
        in_specs=[row, row, row, row, pl.BlockSpec((c, GDN_H * c), lambda i: (i, 0)), pl.BlockSpec((1, 1, LANE), lambda i: (i, 0, 0))],
        out_specs=[row, pl.BlockSpec((1, GDN_W, GDN_D), lambda i: (i, 0, 0))],
        out_shape=[SDS((s, GDN_W), F32), SDS((n, GDN_W, GDN_D), F32)],
        scratch_shapes=[pltpu.VMEM((GDN_W, GDN_D), F32)],
        compiler_params=_cp(("arbitrary",)))(u, w, qd, kd, qk, gl)


def gdn_scan_bwd(u, w, qd, kd, qk, gl, states, do, name):
    s = u.shape[0]
    c = GDN_C
    n = _div(s, c)

    def body(u_r, w_r, qd_r, kd_r, qk_r, gl_r, s_r, do_r, du_o, dw_o, dqd_o, dkd_o, dqk_o, dgl_o, dst):
        @pl.when(pl.program_id(0) == 0)
        def _():
            dst[...] = jnp.zeros_like(dst)

        lane = lax.broadcasted_iota(jnp.int32, (1, LANE), 1)
        heads = range(GDN_H)
        hs = [pl.ds(h * GDN_D, GDN_D) for h in heads]
        qs = [pl.ds(h * c, c) for h in heads]
        sh = [s_r[0, hs[h], :] for h in heads]
        shb = [x.astype(BF16) for x in sh]
        ds_out = [dst[hs[h], :] for h in heads]
        dsb = [x.astype(BF16) for x in ds_out]
        dob = [do_r[:, hs[h]].astype(BF16) for h in heads]
        ws = [_dot(w_r[:, hs[h]], shb[h], NN) for h in heads]
        dv1 = [_dot(qk_r[:, qs[h]], dob[h], TN) for h in heads]
        dv2 = [_dot(kd_r[:, hs[h]], dsb[h], NN) for h in heads]
        dqd = [_dot(dob[h], shb[h], NT) for h in heads]
        dsq = [_dot(qd_r[:, hs[h]], dob[h], TN) for h in heads]
        vb = [(u_r[:, hs[h]] - ws[h]).astype(BF16) for h in heads]
        dv = [dv1[h] + dv2[h] for h in heads]
        dvb = [x.astype(BF16) for x in dv]
        dw = [_dot(dvb[h], shb[h], NT) for h in heads]
        dkd = [_dot(vb[h], dsb[h], NT) for h in heads]
        dqk = [_dot(dob[h], vb[h], NT) for h in heads]
        dsw = [_dot(w_r[:, hs[h]], dvb[h], TN) for h in heads]
        dgl_row = jnp.zeros((1, LANE), F32)
        for h in heads:
            du_o[:, hs[h]] = dv[h]
            dw_o[:, hs[h]] = -dw[h]
            dqd_o[:, hs[h]] = dqd[h]
            dkd_o[:, hs[h]] = dkd[h]
            dqk_o[:, qs[h]] = dqk[h]
            dgl_row = dgl_row + jnp.where(lane == h, jnp.sum(jnp.sum(ds_out[h] * sh[h], axis=1, keepdims=True), axis=0, keepdims=True), 0.0)
            dst[hs[h], :] = ds_out[h] * _lane_scalar(gl_r[0], h) + dsq[h] - dsw[h]
        dgl_o[0] = dgl_row

    rev = lambda i: n - 1 - i
    row = pl.BlockSpec((c, GDN_W), lambda i: (rev(i), 0))
    qks = pl.BlockSpec((c, GDN_H * c), lambda i: (rev(i), 0))
    gls = pl.BlockSpec((1, 1, LANE), lambda i: (rev(i), 0, 0))
    return pl.pallas_call(
        body, name=name, grid=(n,),
        in_specs=[row, row, row, row, qks, gls, pl.BlockSpec((1, GDN_W, GDN_D), lambda i: (rev(i), 0, 0)), row],
        out_specs=[row, row, row, row, qks, gls],
        out_shape=[SDS((s, GDN_W), F32)] * 4 + [SDS((s, GDN_H * c), F32), SDS((n, 1, LANE), F32)],
        scratch_shapes=[pltpu.VMEM((GDN_W, GDN_D), F32)],
        compiler_params=_cp(("arbitrary",)))(u, w, qd, kd, qk, gl, states, do)


def _gdn_out_rows(o, z, nw):
    outs = []
    for h in range(GDN_H):
        hs = slice(h * GDN_D, (h + 1) * GDN_D)
        oh = o[:, hs]
        y = oh * lax.rsqrt(jnp.mean(oh * oh, axis=-1, keepdims=True) + RMS_EPS) * nw
        outs.append(y * _silu(z[:, hs]))
    return jnp.concatenate(outs, axis=1)


def gdn_out(o, p_all, nw, dy, lay, name):
    s = o.shape[0]
    tm = _tile(s, 512)
    bwd = dy is not None

    def body(*refs):
        o_r, z_r, nw_r = refs[:3]
        if not bwd:
            refs[3][...] = _gdn_out_rows(o_r[...], z_r[...], nw_r[...]).astype(BF16)
            return
        dy_r, do_o, dz_o, dnw_o = refs[3:]
        _, vjp = jax.vjp(_gdn_out_rows, o_r[...], z_r[...], nw_r[...])
        d_o, d_z, d_nw = vjp(dy_r[...].astype(F32))
        do_o[...] = d_o
        dz_o[...] = d_z.astype(BF16)

        @pl.when(pl.program_id(0) == 0)
        def _():
            dnw_o[...] = jnp.zeros_like(dnw_o)

        dnw_o[...] += d_nw

    row = pl.BlockSpec((tm, GDN_W), lambda i: (i, 0))
    zs = pl.BlockSpec((tm, GDN_W), lambda i: (i, _div(lay.z, GDN_W)))
    nws = pl.BlockSpec((1, GDN_D), lambda i: (0, 0))
    if not bwd:
        return pl.pallas_call(body, name=name, grid=(s // tm,), in_specs=[row, zs, nws], out_specs=row,
                              out_shape=SDS((s, GDN_W), BF16), compiler_params=_cp(("parallel",)))(o, p_all, nw)
    return pl.pallas_call(body, name=name, grid=(s // tm,), in_specs=[row, zs, nws, row], out_specs=[row, row, nws],
                          out_shape=[SDS((s, GDN_W), F32), SDS((s, GDN_W), BF16), SDS((1, GDN_D), F32)],
                          compiler_params=_cp(("arbitrary",)))(o, p_all, nw, dy)


def _cols_to_full(g):
    n, k, c = g.shape
    return g.transpose(1, 0, 2).reshape(k, n * c)


def _rows_to_blocks(w):
    return w.reshape(N_DEV, w.shape[0] // N_DEV, w.shape[1])


def _pack_small(parts, rows):
    flat = jnp.concatenate([jnp.pad(p.reshape(-1), (0, -p.size % LANE)) for p in parts])
    return jnp.pad(flat, (0, rows * LANE - flat.size)).reshape(rows, LANE)


def kernel(x, mem, g_mix, w_in, sinks, conv_w, a_log, dt_bias, gdn_norm_w, g_mem, w_mem_kv, w_swa_up, w_gdn_up, w_xa_up, w_out, g_mlp, w_mlp_in, w_mlp_out, g_final, loss_target, m_g_mix, m_w_in, m_sinks, m_conv_w, m_a_log, m_dt_bias, m_gdn_norm_w, m_g_mem, m_w_mem_kv, m_w_swa_up, m_w_gdn_up, m_w_xa_up, m_w_out, m_g_mlp, m_w_mlp_in, m_w_mlp_out, m_g_final, v_g_mix, v_w_in, v_sinks, v_conv_w, v_a_log, v_dt_bias, v_gdn_norm_w, v_g_mem, v_w_mem_kv, v_w_swa_up, v_w_gdn_up, v_w_xa_up, v_w_out, v_g_mlp, v_w_mlp_in, v_w_mlp_out, v_g_final):
    xs, ms, tgt = x[0], mem[0], loss_target[0]
    s, d = xs.shape
    lay = Layout(d)
    px, py, pc = _position()
    dev = 4 * px + 2 * py + pc

    g_in, g_conv = all_gather([w_in[0].astype(BF16), conv_w[0]], "gather_w_in")
    W_in = pad_w_in(g_in, lay)
    convw = _cols_to_full(g_conv)
    gp = jnp.zeros((8, LANE), F32).at[0, :GDN_H].set(a_log[0]).at[1, :GDN_H].set(dt_bias[0])
    later = [w_mem_kv[0], w_swa_up[0], w_gdn_up[0], w_xa_up[0], w_out[0], w_mlp_in[0]]

    n1 = rmsnorm_fwd(xs, g_mix, "norm_mix")
    p_all, g_mkv, W_sup, W_gup, W_xup, g_out, W_m1 = matmul(
        n1, W_in, mode="nn", out_dtype=F32, name="proj_in", tm=2048, tn=512, tk=d,
        side=GatherJob([w.astype(BF16) for w in later]))
    W_mkv = g_mkv.reshape(-1, g_mkv.shape[2])
    W_out = g_out.reshape(-1, d)
    y_a = swa_fwd(p_all, sinks, lay, "swa_fwd")
    xc = gdn_conv_fwd(p_all, convw, lay, "gdn_conv_fwd")
    u, gw, gqd, gkd, gqk, ggl = gdn_pre_fwd(xc, p_all, gp, lay, "gdn_pre_fwd")
    o_b, states = gdn_scan_fwd(u, gw, gqd, gkd, gqk, ggl, "gdn_scan_fwd")
    y_b = gdn_out(o_b, p_all, gdn_norm_w, None, lay, "gdn_out_fwd")
    nm = rmsnorm_fwd(ms, g_mem, "norm_mem")
    mkv = matmul(nm, W_mkv, mode="nn", out_dtype=BF16, name="proj_mem", tk=d)
    y_c = xattn_fwd(p_all, mkv, lay, "xattn_fwd")
    merged = merge(p_all, y_a, y_b, y_c, W_sup, W_gup, W_xup, None, lay, "merge_fwd")
    h1 = matmul(merged, W_out, mode="nn", out_dtype=F32, name="proj_out", tm=2048, tn=512, tk=d, resid=xs)
    n2 = rmsnorm_fwd(h1, g_mlp, "norm_mlp")
    uu, g_m2 = matmul(n2, W_m1, mode="nn", out_dtype=F32, name="mlp_in", tm=2048, tn=512, tk=d, b_cols=True,
                      side=GatherJob([w_mlp_out[0].astype(BF16)]))
    W_m2 = g_m2.reshape(-1, d)
    h2 = matmul(uu, W_m2, mode="nn", out_dtype=F32, name="mlp_out", tm=1024, tn=2048, tk=512, a_relu2=True, resid=h1)
    dh2, dg_final, lrow = final_norm_loss(h2, g_final.reshape(1, d), tgt, "final_loss")
    loss = lax.psum(lrow[0, 0], ("x", "y", "c"))

    du = matmul(dh2, W_m2, mode="nt", out_dtype=BF16, name="mlp_out_dx", tm=2048, tn=512, tk=d, relu2_grad_of=uu)
    dW_m2 = matmul(uu, dh2, mode="tn", out_dtype=BF16, name="mlp_out_dw", tm=1024, tn=2048, tk=512, a_relu2=True)
    dn2 = matmul(du, W_m1, mode="nt", out_dtype=F32, name="mlp_in_dx", tm=1024, tn=2048, tk=1024, b_cols=True)
    dW_m1 = matmul(n2, du, mode="tn", out_dtype=BF16, name="mlp_in_dw", tm=2048, tn=1024, tk=512, out_cols=True)
    dh1, dg_mlp = rmsnorm_bwd(h1, g_mlp, dn2, dh2, "norm_mlp_bwd")

    dmerged = matmul(dh1, W_out, mode="nt", out_dtype=F32, name="proj_out_dx", tm=2048, tn=512, tk=d)
    dW_out = matmul(merged, dh1, mode="tn", out_dtype=BF16, name="proj_out_dw", tm=2048, tn=1024, tk=512)
    dga, dgb, dgc, dta, dtb, dtc = merge(p_all, y_a, y_b, y_c, W_sup, W_gup, W_xup, dmerged, lay, "merge_bwd")
    dy_a = matmul(dta, W_sup, mode="nt", out_dtype=BF16, name="swa_up_dx", tm=2048, tk=d, b_cols=True)
    dy_b = matmul(dtb, W_gup, mode="nt", out_dtype=BF16, name="gdn_up_dx", tm=2048, tk=d, b_cols=True)
    dy_c = matmul(dtc, W_xup, mode="nt", out_dtype=BF16, name="xa_up_dx", tm=2048, tk=d, b_cols=True)
    dW_sup = matmul(y_a, dta, mode="tn", out_dtype=BF16, name="swa_up_dw", tk=1024, out_cols=True)
    dW_gup = matmul(y_b, dtb, mode="tn", out_dtype=BF16, name="gdn_up_dw", tk=1024, out_cols=True)
    dW_xup = matmul(y_c, dtc, mode="tn", out_dtype=BF16, name="xa_up_dw", tk=1024, out_cols=True)

    dq_a, dk_a, dv_a, dsinks = swa_bwd(p_all, sinks, dy_a, lay, "swa_bwd")
    dq_c, dmkv = xattn_bwd(p_all, mkv, dy_c, lay, "xattn_bwd")
    dW_mkv = matmul(nm, dmkv, mode="tn", out_dtype=BF16, name="proj_mem_dw", tk=256)
    dnm = matmul(dmkv, W_mkv, mode="nt", out_dtype=F32, name="proj_mem_dx", tk=1024)
    _, dg_mem = rmsnorm_bwd(ms, g_mem, dnm, None, "norm_mem_bwd")

    do_b, dz, dnorm_w = gdn_out(o_b, p_all, gdn_norm_w, dy_b, lay, "gdn_out_bwd")
    du_g, dw_g, dqd_g, dkd_g, dqk_g, dgl_g = gdn_scan_bwd(u, gw, gqd, gkd, gqk, ggl, states, do_b, "gdn_scan_bwd")
    dxc, dab, dgp = gdn_pre_bwd(xc, p_all, gp, du_g, dw_g, dqd_g, dkd_g, dqk_g, dgl_g, lay, "gdn_pre_bwd")
    dqkv, dconv = gdn_conv_bwd(p_all, convw, dxc, lay, "gdn_conv_bwd")

    dp_all = jnp.concatenate([dga, dgb, dgc, dq_a, dqkv, dz, dq_c, dk_a, dv_a, dab,
                              jnp.zeros((s, lay.pw - lay.end), BF16)], axis=1)
    def pair_stage(blocks, tag):
        from_sib = pair_exchange(blocks, "grads_pair_exchange_" + tag)
        return [pair_add(b, o, "grads_pair_add_%s%d" % (tag, i)) for i, (b, o) in enumerate(zip(blocks, from_sib))]

    early = pair_stage([_rows_to_blocks(dW_mkv), dW_sup, dW_gup, dW_xup, _rows_to_blocks(dW_out), dW_m1, _rows_to_blocks(dW_m2)], "a")
    dW_in, p_m1, p_m2 = matmul(n1, dp_all, mode="tn", out_dtype=BF16, name="proj_in_dw", tm=2048, tn=1024, tk=512,
                               side=ChipExchangeJob(early[5:]))
    late = pair_stage([unpad_dw_in(dW_in, lay)], "b")
    dn1, p_in, p_mkv, p_sup, p_gup, p_xup, p_out = matmul(
        dp_all, W_in, mode="nt", out_dtype=F32, name="proj_in_dx", tm=1024, tn=2048, tk=1024,
        side=ChipExchangeJob(late + early[:5]))
    grad_x, dg_mix = rmsnorm_bwd(xs, g_mix, dn1, dh1, "norm_mix_bwd")
    parts = [p_in, p_mkv, p_sup, p_gup, p_xup, p_out, p_m1, p_m2]

    shard_names = [(w_in, m_w_in, v_w_in), (w_mem_kv, m_w_mem_kv, v_w_mem_kv), (w_swa_up, m_w_swa_up, v_w_swa_up),
                   (w_gdn_up, m_w_gdn_up, v_w_gdn_up), (w_xa_up, m_w_xa_up, v_w_xa_up), (w_out, m_w_out, v_w_out),
                   (w_mlp_in, m_w_mlp_in, v_w_mlp_in), (w_mlp_out, m_w_mlp_out, v_w_mlp_out)]
    big_res = [adamw(p, w[0], m[0], v[0], "adamw_%d" % i) for i, (p, (w, m, v)) in enumerate(zip(parts, shard_names))]

    smalls = [(g_mix, m_g_mix, v_g_mix, dg_mix), (sinks, m_sinks, v_sinks, dsinks[:, :SWA_HQ]),
              (a_log, m_a_log, v_a_log, dgp[0:1, :GDN_H]), (dt_bias, m_dt_bias, v_dt_bias, dgp[1:2, :GDN_H]),
              (gdn_norm_w, m_gdn_norm_w, v_gdn_norm_w, dnorm_w), (g_mem, m_g_mem, v_g_mem, dg_mem),
              (g_mlp, m_g_mlp, v_g_mlp, dg_mlp), (g_final, m_g_final, v_g_final, dg_final)]
    sizes = [-(-t[0].size // LANE) * LANE for t in smalls] + [GDN_CONV * 3 * GDN_W]
    rows = -(-sum(sizes) // (8 * LANE)) * 8
    csh = conv_w.shape[2]

    def conv_place(a):
        full = jnp.tile(a[0], (1, N_DEV))
        owner = lax.broadcasted_iota(jnp.int32, full.shape, 1) // csh
        return jnp.where(owner == dev, full, 0.0)

    g_pack = _pack_small([t[3] for t in smalls] + [dconv], rows)
    w_pack = _pack_small([t[0] for t in smalls] + [conv_place(conv_w)], rows)
    m_pack = _pack_small([t[1] for t in smalls] + [conv_place(m_conv_w)], rows)
    v_pack = _pack_small([t[2] for t in smalls] + [conv_place(v_conv_w)], rows)
    g_all = all_gather([g_pack], "gather_small_grads")[0]
    small_res = adamw(g_all, w_pack, m_pack, v_pack, "adamw_small")

    def unpack(arr):
        flat = arr.reshape(-1)
        outs, off = [], 0
        for t, sz in zip(smalls, sizes[:-1]):
            outs.append(flat[off:off + t[0].size].reshape(t[0].shape))
            off += sz
        cw = flat[off:off + sizes[-1]].reshape(GDN_CONV, 3 * GDN_W)
        mine = (lax.broadcasted_iota(jnp.int32, (1, N_DEV, 1), 1) == dev).astype(F32)
        outs.append(jnp.sum(cw.reshape(GDN_CONV, N_DEV, csh) * mine, axis=1)[None])
        return outs

    sg, sd, sm, sv = (unpack(a) for a in small_res)
    bg, bd, bm, bv = ([r[i][None] for r in big_res] for i in range(4))

    def ordered(sm_, bg_):
        return [sm_[0], bg_[0], sm_[1], sm_[8], sm_[2], sm_[3], sm_[4], sm_[5], bg_[1], bg_[2], bg_[3], bg_[4], bg_[5],
                sm_[6], bg_[6], bg_[7], sm_[7]]

    return (loss, grad_x[None], *ordered(sg, bg), *ordered(sd, bd), *ordered(sm, bm), *ordered(sv, bv))
```

```python
import functools
import math

import jax
import jax.numpy as jnp
from jax import lax
from jax.experimental import pallas as pl
from jax.experimental.pallas import tpu as pltpu

F32, BF16 = jnp.float32, jnp.bfloat16
SDS = jax.ShapeDtypeStruct
MESH = pl.DeviceIdType.MESH
ANY = pl.BlockSpec(memory_space=pl.ANY)

SWA_HQ, SWA_HKV, SWA_HD, SWA_W = 16, 2, 64, 128
SWA_G = SWA_HQ // SWA_HKV
GDN_H, GDN_D, GDN_CONV, GDN_C = 4, 128, 4, 64
XA_H, XA_D = 4, 128
Q_W = SWA_HQ * SWA_HD
KV_W = SWA_HKV * SWA_HD
GDN_W = GDN_H * GDN_D
XA_W = XA_H * XA_D
RMS_EPS = 1e-6
L2_EPS = 1e-6
NEG = -1e30
N_DEV = 8
LANE = 128

ADAM_LR, ADAM_B1, ADAM_B2, ADAM_EPS, ADAM_WD, ADAM_STEP = 0.001, 0.9, 0.999, 1e-08, 0.01, 10

VMEM_BIG = 56 * 1024 * 1024


def _cp(sem, vmem=VMEM_BIG):
    return pltpu.CompilerParams(dimension_semantics=sem, vmem_limit_bytes=vmem)


def _div(a, b):
    assert a % b == 0, (a, b)
    return a // b


def _tile(n, t):
    t = min(t, n)
    assert n % t == 0, (n, t)
    return t


def _sigmoid(x):
    return 1.0 / (1.0 + jnp.exp(-x))


def _silu(x):
    return x * _sigmoid(x)


def _softplus(x):
    return jnp.maximum(x, 0.0) + jnp.log1p(jnp.exp(-jnp.abs(x)))


def _dot(a, b, dims, prec=None):
    return lax.dot_general(a, b, (dims, ((), ())), precision=prec, preferred_element_type=F32)


NN = ((1,), (0,))
NT = ((1,), (1,))
TN = ((0,), (0,))
HI = lax.Precision.HIGHEST


def _bdot_plain(a, b, dims):
    return _dot(a.astype(BF16), b.astype(BF16), dims)


@functools.partial(jax.custom_vjp, nondiff_argnums=(2,))
def _bdot_vjp(a, b, dims):
    return _bdot_plain(a, b, dims)


def _bdot_vjp_fwd(a, b, dims):
    return _bdot_plain(a, b, dims), (a, b)


def _bdot_vjp_bwd(dims, res, ct):
    a, b = res
    if dims == NN:
        return _bdot_plain(ct, b, NT), _bdot_plain(a, ct, TN)
    assert dims == NT, dims
    return _bdot_plain(ct, b, NN), _bdot_plain(ct, a, TN)


_bdot_vjp.defvjp(_bdot_vjp_fwd, _bdot_vjp_bwd)


class Layout:
    def __init__(self, d):
        self.d = d
        self.g = 0
        self.q = 3 * d
        self.qkv = self.q + Q_W
        self.z = self.qkv + 3 * GDN_W
        self.qc = self.z + GDN_W
        self.k = self.qc + XA_W
        self.v = self.k + KV_W
        self.ab = self.v + KV_W
        self.end = self.ab + LANE
        self.pw = -(-self.end // 1024) * 1024
        self.lq, self.lk, self.lv, self.lqkv = 0, Q_W, Q_W + KV_W, Q_W + 2 * KV_W
        self.la = self.lqkv + 3 * GDN_W
        self.lz = self.la + 2 * GDN_H
        self.lqc = self.lz + GDN_W
        self.lg = self.lqc + XA_W
        self.lw = self.lg + 3 * d

    def pieces(self):
        segs = [(self.lq, self.lk, self.q), (self.lk, self.lv, self.k), (self.lv, self.lqkv, self.v),
                (self.lqkv, self.la, self.qkv), (self.la, self.lz, self.ab), (self.lz, self.lqc, self.z),
                (self.lqc, self.lg, self.qc), (self.lg, self.lw, self.g)]
        cw = _div(self.lw, N_DEV)
        out = []
        for dev in range(N_DEV):
            lo, hi = dev * cw, (dev + 1) * cw
            for ls, le, ps in segs:
                s, e = max(lo, ls), min(hi, le)
                if s < e:
                    out.append((dev, s - lo, ps + s - ls, e - s))
        return out


def pad_w_in(g, lay):
    nd, k, cw = g.shape
    tr = _tile(k, 256)
    tail = lay.ab + 2 * GDN_H

    def body(g_ref, o_ref):
        o_ref[:, pl.ds(tail, lay.pw - tail)] = jnp.zeros((tr, lay.pw - tail), o_ref.dtype)
        for dev, so, po, ln in lay.pieces():
            o_ref[:, pl.ds(po, ln)] = g_ref[dev, :, pl.ds(so, ln)]

    return pl.pallas_call(
        body, name="pad_w_in", grid=(k // tr,), in_specs=[pl.BlockSpec((nd, tr, cw), lambda i: (0, i, 0))],
        out_specs=pl.BlockSpec((tr, lay.pw), lambda i: (i, 0)), out_shape=SDS((k, lay.pw), g.dtype),
        compiler_params=_cp(("parallel",)))(g)


def unpad_dw_in(dw, lay):
    k = dw.shape[0]
    cw = _div(lay.lw, N_DEV)
    tr = _tile(k, 256)

    def body(d_ref, o_ref):
        for dev, so, po, ln in lay.pieces():
            o_ref[dev, :, pl.ds(so, ln)] = d_ref[:, pl.ds(po, ln)]

    return pl.pallas_call(
        body, name="unpad_dw_in", grid=(k // tr,), in_specs=[pl.BlockSpec((tr, lay.pw), lambda i: (i, 0))],
        out_specs=pl.BlockSpec((N_DEV, tr, cw), lambda i: (0, i, 0)), out_shape=SDS((N_DEV, k, cw), dw.dtype),
        compiler_params=_cp(("parallel",)))(dw)


def _position():
    return lax.axis_index("x"), lax.axis_index("y"), lax.axis_index("c")


def all_gather(arrs, name):
    n = len(arrs)

    def body(*refs):
        ins, outs = refs[:n], refs[n:2 * n]
        send_sems, recv_sems, local_sems = refs[2 * n:]
        x, y, c = _position()
        me, sibling = (x, y, c), (x, y, 1 - c)
        chips = [(1 - x, y), (x, 1 - y), (1 - x, 1 - y)]

        def blk(o, p):
            return o.at[4 * p[0] + 2 * p[1] + p[2]]

        def copy(i, k, block, to, src=None):
            return pltpu.make_async_remote_copy(
                src_ref=blk(outs[i], block) if src is None else src, dst_ref=blk(outs[i], block),
                send_sem=send_sems.at[i, k], recv_sem=recv_sems.at[i, k], device_id=to, device_id_type=MESH)

        started = []
        for i in range(n):
            mine = pltpu.make_async_copy(ins[i], blk(outs[i], me), local_sems.at[i])
            mine.start()
            first = [copy(i, 0, me, sibling, src=ins[i])]
            first += [copy(i, 1 + j, me, (*chip, c), src=ins[i]) for j, chip in enumerate(chips)]
            for cp in first:
                cp.start()
            started += [mine.wait] + [cp.wait_send for cp in first]
        for i in range(n):
            for j, chip in enumerate(chips):
                copy(i, 1 + j, (*chip, c), me).wait_recv()
                fwd = copy(i, 4 + j, (*chip, c), sibling)
                fwd.start()
                started.append(fwd.wait_send)
        for i in range(n):
            copy(i, 0, sibling, me).wait_recv()
            for j, chip in enumerate(chips):
                copy(i, 4 + j, (*chip, 1 - c), me).wait_recv()
        for w in started:
            w()

    return pl.pallas_call(
        body, name=name,
        out_shape=[SDS((N_DEV,) + a.shape, a.dtype) for a in arrs],
        in_specs=[ANY] * n, out_specs=[ANY] * n,
        scratch_shapes=[pltpu.SemaphoreType.DMA((n, 7)), pltpu.SemaphoreType.DMA((n, 7)), pltpu.SemaphoreType.DMA((n,))],
    )(*arrs)


class GatherJob:
    def __init__(self, arrs):
        self.ins = list(arrs)
        n = len(arrs)
        self.out_shapes = [SDS((N_DEV,) + a.shape, a.dtype) for a in arrs]
        self.scratch = [pltpu.SemaphoreType.DMA((n, 7)), pltpu.SemaphoreType.DMA((n, 7)), pltpu.SemaphoreType.DMA((n,))]

    def _ctx(self, outs, sems):
        send_sems, recv_sems, _ = sems
        x, y, c = _position()

        def blk(o, p):
            return o.at[4 * p[0] + 2 * p[1] + p[2]]

        def copy(i, k, block, to, src=None):
            return pltpu.make_async_remote_copy(
                src_ref=blk(outs[i], block) if src is None else src, dst_ref=blk(outs[i], block),
                send_sem=send_sems.at[i, k], recv_sem=recv_sems.at[i, k], device_id=to, device_id_type=MESH)

        return (x, y, c), (x, y, 1 - c), [(1 - x, y), (x, 1 - y), (1 - x, 1 - y)], blk, copy

    def start(self, ins, outs, sems):
        me, sibling, chips, blk, copy = self._ctx(outs, sems)
        for i in range(len(ins)):
            pltpu.make_async_copy(ins[i], blk(outs[i], me), sems[2].at[i]).start()
            copy(i, 0, me, sibling, src=ins[i]).start()
            for j, chip in enumerate(chips):
                copy(i, 1 + j, me, (*chip, me[2]), src=ins[i]).start()

    def mid(self, ins, outs, sems):
        me, sibling, chips, blk, copy = self._ctx(outs, sems)
        for i in range(len(ins)):
            for j, chip in enumerate(chips):
                copy(i, 1 + j, (*chip, me[2]), me).wait_recv()
                copy(i, 4 + j, (*chip, me[2]), sibling).start()

    def finish(self, ins, outs, sems):
        me, sibling, chips, blk, copy = self._ctx(outs, sems)
        for i in range(len(ins)):
            copy(i, 0, sibling, me).wait_recv()
            for j, chip in enumerate(chips):
                copy(i, 4 + j, (*chip, 1 - me[2]), me).wait_recv()
        for i in range(len(ins)):
            pltpu.make_async_copy(ins[i], blk(outs[i], me), sems[2].at[i]).wait()
            copy(i, 0, me, sibling, src=ins[i]).wait_send()
            for j, chip in enumerate(chips):
                copy(i, 1 + j, me, (*chip, me[2]), src=ins[i]).wait_send()
                copy(i, 4 + j, (*chip, me[2]), sibling).wait_send()


class ChipExchangeJob:
    mid = None

    def __init__(self, arrs):
        self.ins = list(arrs)
        n = len(arrs)
        self.out_shapes = [SDS(a.shape, a.dtype) for a in arrs]
        self.scratch = [pltpu.SemaphoreType.DMA((n, 3)), pltpu.SemaphoreType.DMA((n, 3)), pltpu.SemaphoreType.DMA((n,))]

    def _copies(self, ins, outs, sems, i, arrivals):
        send_sems, recv_sems, local_sems = sems
        x, y, c = _position()
        my_chip = 2 * x + y
        chips = [(1 - x, y), (x, 1 - y), (1 - x, 1 - y)]
        if arrivals:
            return [pltpu.make_async_remote_copy(
                src_ref=ins[i].at[my_chip], dst_ref=outs[i].at[2 * px + py], send_sem=send_sems.at[i, k],
                recv_sem=recv_sems.at[i, k], device_id=(px, py, c), device_id_type=MESH) for k, (px, py) in enumerate(chips)]
        local = pltpu.make_async_copy(ins[i].at[my_chip], outs[i].at[my_chip], local_sems.at[i])
        return local, [pltpu.make_async_remote_copy(
            src_ref=ins[i].at[2 * px + py], dst_ref=outs[i].at[my_chip], send_sem=send_sems.at[i, k],
            recv_sem=recv_sems.at[i, k], device_id=(px, py, c), device_id_type=MESH) for k, (px, py) in enumerate(chips)]

    def start(self, ins, outs, sems):
        for i in range(len(ins)):
            local, remote = self._copies(ins, outs, sems, i, False)
            local.start()
            for cp in remote:
                cp.start()

    def finish(self, ins, outs, sems):
        for i in range(len(ins)):
            for cp in self._copies(ins, outs, sems, i, True):
                cp.wait_recv()
            local, remote = self._copies(ins, outs, sems, i, False)
            for cp in remote:
                cp.wait_send()
            local.wait()


def _slab_shape(g, cols):
    return (g.shape[0], _div(g.shape[1], N_DEV)) if cols else g.shape[1:]


def pair_exchange(grads, cols, name):
    n = len(grads)

    def body(*refs):
        ins, outs = refs[:n], refs[n:2 * n]
        send_sems, recv_sems = refs[2 * n:]
        x, y, c = _position()

        def part(i, dst):
            if not cols[i]:
                return ins[i].at[dst]
            cw = _slab_shape(grads[i], True)[1]
            return ins[i].at[:, pl.ds(pl.multiple_of(dst * cw, LANE), cw)]

        copies = [pltpu.make_async_remote_copy(src_ref=part(i, 2 * j + 1 - c), dst_ref=outs[i].at[j], send_sem=send_sems.at[i, j],
                                               recv_sem=recv_sems.at[i, j], device_id=(x, y, 1 - c), device_id_type=MESH)
                  for i in range(n) for j in range(4)]
        for cp in copies:
            cp.start()
        for cp in copies:
            cp.wait()

    return pl.pallas_call(
        body, name=name, out_shape=[SDS((4,) + _slab_shape(g, cl), g.dtype) for g, cl in zip(grads, cols)],
        in_specs=[ANY] * n, out_specs=[ANY] * n,
        scratch_shapes=[pltpu.SemaphoreType.DMA((n, 4)), pltpu.SemaphoreType.DMA((n, 4))],
    )(*grads)


def pair_add(grad, cols, other, name):
    r, c = _slab_shape(grad, cols)
    tr = _tile(r, 256)
    parity = lax.axis_index("c").astype(jnp.int32).reshape(1)

    def body(par_ref, a_ref, b_ref, o_ref):
        o_ref[...] = (a_ref[...].astype(F32) + b_ref[...].astype(F32)).astype(BF16)

    spec = pl.BlockSpec((None, tr, c), lambda j, i, par: (j, i, 0))
    if cols:
        own = pl.BlockSpec((tr, c), lambda j, i, par: (i, 2 * j + par[0]))
    else:
        own = pl.BlockSpec((None, tr, c), lambda j, i, par: (2 * j + par[0], i, 0))
    return pl.pallas_call(
        body, name=name, out_shape=SDS(other.shape, BF16),
        grid_spec=pltpu.PrefetchScalarGridSpec(num_scalar_prefetch=1, grid=(4, r // tr), in_specs=[own, spec], out_specs=spec),
        compiler_params=_cp(("parallel", "parallel")))(parity, grad, other)


def adamw(parts, w, m, v, name):
    p, r, c = parts.shape
    tr = _tile(r, 128 if c > 1024 else 256)

    def body(p_ref, w_ref, m_ref, v_ref, g_out, d_out, m_out, v_out):
        g = p_ref[0].astype(F32)
        for j in range(1, p):
            g = g + p_ref[j].astype(F32)
        mn = ADAM_B1 * m_ref[...] + (1.0 - ADAM_B1) * g
        vn = ADAM_B2 * v_ref[...] + (1.0 - ADAM_B2) * jnp.square(g)
        m_hat = mn / (1.0 - ADAM_B1 ** ADAM_STEP)
        v_hat = vn / (1.0 - ADAM_B2 ** ADAM_STEP)
        g_out[...] = g
        d_out[...] = -ADAM_LR * (m_hat / (jnp.sqrt(v_hat) + ADAM_EPS) + ADAM_WD * w_ref[...])
        m_out[...] = mn
        v_out[...] = vn

    spec = pl.BlockSpec((tr, c), lambda i: (i, 0))
    return pl.pallas_call(
        body, name=name, grid=(r // tr,),
        in_specs=[pl.BlockSpec((p, tr, c), lambda i: (0, i, 0)), spec, spec, spec],
        out_specs=[spec] * 4, out_shape=[SDS((r, c), F32)] * 4, compiler_params=_cp(("parallel",)))(parts, w, m, v)


def matmul(a, b, *, mode, out_dtype, name, tm=1024, tn=1024, tk=512, a_relu2=False, resid=None, relu2_grad_of=None,
           b_cols=False, relu2_out=False, side=None):
    if b_cols:
        nb, brows, bc = b.shape
        bshape = (brows, nb * bc)
    else:
        bshape = b.shape
    if mode == "nn":
        (m, k), (k2, n) = a.shape, bshape
    elif mode == "nt":
        (m, k), (n, k2) = a.shape, bshape
    else:
        (k, m), (k2, n) = a.shape, bshape
    assert k == k2, (a.shape, b.shape, mode)
    tm, tn, tk = _tile(m, tm), _tile(n, tn), _tile(k, tk)
    if b_cols and mode == "nn":
        tn = _tile(bc, tn)
    if b_cols and mode == "nt":
        tk = _tile(bc, tk)
    nk = k // tk
    ni, nj = m // tm, n // tn
    dims = {"nn": NN, "nt": NT, "tn": TN}[mode]
    extras = [e for e in (resid, relu2_grad_of) if e is not None]
    n_side = len(side.ins) if side is not None else 0
    n_main = 2 if relu2_out else 1

    def body(*refs):
        a_ref, b_ref = refs[:2]
        e_refs = refs[2:2 + len(extras)]
        n_in = 2 + len(extras) + n_side
        o_ref = refs[n_in]
        act_ref = refs[n_in + 1] if relu2_out else None
        acc_ref = refs[n_in + n_main + n_side] if nk > 1 else None
        if side is not None:
            s_ins = refs[2 + len(extras):n_in]
            s_outs = refs[n_in + n_main:n_in + n_main + n_side]
            s_sems = refs[len(refs) - len(side.scratch):]
            step = (pl.program_id(0) * nj + pl.program_id(1)) * nk + pl.program_id(2)
            pl.when(step == 0)(lambda: side.start(s_ins, s_outs, s_sems))
            if side.mid is not None:
                pl.when(step == (ni * nj * nk * 85) // 100)(lambda: side.mid(s_ins, s_outs, s_sems))

        def operands():
            av = a_ref[...]
            if a_relu2:
                av = jnp.square(jnp.maximum(av.astype(F32), 0.0))
            return av.astype(BF16), b_ref[...].astype(BF16)

        def finish(r):
            e = list(e_refs)
            if resid is not None:
                r = r + e.pop(0)[...]
            if relu2_grad_of is not None:
                r = r * (2.0 * jnp.maximum(e.pop(0)[...], 0.0))
            o_ref[...] = r.astype(out_dtype)
            if relu2_out:
                act_ref[...] = jnp.square(jnp.maximum(r, 0.0)).astype(BF16)

        if nk == 1:
            av, bv = operands()
            finish(_dot(av, bv, dims))
        else:
            kk = pl.program_id(2)

            @pl.when(kk == 0)
            def _():
                acc_ref[...] = jnp.zeros_like(acc_ref)

            av, bv = operands()
            acc_ref[...] += _dot(av, bv, dims)

            @pl.when(kk == nk - 1)
            def _():
                finish(acc_ref[...])

        if side is not None:
            pl.when(step == ni * nj * nk - 1)(lambda: side.finish(s_ins, s_outs, s_sems))

    a_spec = {"nn": pl.BlockSpec((tm, tk), lambda i, j, kk: (i, kk)),
              "nt": pl.BlockSpec((tm, tk), lambda i, j, kk: (i, kk)),
              "tn": pl.BlockSpec((tk, tm), lambda i, j, kk: (kk, i))}[mode]
    if not b_cols:
        b_spec = {"nn": pl.BlockSpec((tk, tn), lambda i, j, kk: (kk, j)),
                  "nt": pl.BlockSpec((tn, tk), lambda i, j, kk: (j, kk)),
                  "tn": pl.BlockSpec((tk, tn), lambda i, j, kk: (kk, j))}[mode]
    elif mode == "nn":
        per = bc // tn
        b_spec = pl.BlockSpec((None, tk, tn), lambda i, j, kk: (j // per, kk, j % per))
    else:
        assert mode == "nt", mode
        per = bc // tk
        b_spec = pl.BlockSpec((None, tn, tk), lambda i, j, kk: (kk // per, j, kk % per))
    e_spec = pl.BlockSpec((tm, tn), lambda i, j, kk: (i, j))
    main_shapes = [SDS((m, n), out_dtype)] + ([SDS((m, n), BF16)] if relu2_out else [])
    res = pl.pallas_call(
        body, name=name, grid=(ni, nj, nk),
        in_specs=[a_spec, b_spec] + [e_spec] * len(extras) + [ANY] * n_side,
        out_specs=[e_spec] * n_main + [ANY] * n_side, out_shape=main_shapes + (side.out_shapes if side is not None else []),
        scratch_shapes=([pltpu.VMEM((tm, tn), F32)] if nk > 1 else []) + (side.scratch if side is not None else []),
        compiler_params=_cp(("arbitrary", "arbitrary", "arbitrary")))(a, b, *extras, *(side.ins if side is not None else []))
    return res if len(res) > 1 else res[0]


def rmsnorm_fwd(x, g, name):
    s, d = x.shape
    tm = _tile(s, 256)

    def body(x_ref, g_ref, o_ref):
        xv = x_ref[...]
        r = lax.rsqrt(jnp.mean(xv * xv, axis=-1, keepdims=True) + RMS_EPS)
        o_ref[...] = (xv * r * g_ref[...]).astype(BF16)

    row = pl.BlockSpec((tm, d), lambda i: (i, 0))
    return pl.pallas_call(body, name=name, grid=(s // tm,), in_specs=[row, pl.BlockSpec((1, d), lambda i: (0, 0))],
                          out_specs=row, out_shape=SDS((s, d), BF16), compiler_params=_cp(("parallel",)))(x, g)


def _rms_bwd_rows(xv, gv, dy):
    r = lax.rsqrt(jnp.mean(xv * xv, axis=-1, keepdims=True) + RMS_EPS)
    xh = xv * r
    dxh = dy * gv
    dx = r * (dxh - xh * jnp.mean(dxh * xh, axis=-1, keepdims=True))
    return dx, jnp.sum(dy * xh, axis=0, keepdims=True)


def rmsnorm_bwd(x, g, dn, resid, name, bf16_copy=False):
    s, d = x.shape
    tm = _tile(s, 256)
    has_r = resid is not None

    def body(*refs):
        x_ref, g_ref, dn_ref = refs[:3]
        dx_ref, dg_ref = refs[3 + has_r:5 + has_r]
        dx, part = _rms_bwd_rows(x_ref[...], g_ref[...], dn_ref[...].astype(F32))
        if has_r:
            dx = dx + refs[3][...]
        dx_ref[...] = dx
        if bf16_copy:
            refs[5 + has_r][...] = dx.astype(BF16)

        @pl.when(pl.program_id(0) == 0)
        def _():
            dg_ref[...] = jnp.zeros_like(dg_ref)

        dg_ref[...] += part

    row = pl.BlockSpec((tm, d), lambda i: (i, 0))
    vec = pl.BlockSpec((1, d), lambda i: (0, 0))
    ins = [x, g, dn] + ([resid] if has_r else [])
    return pl.pallas_call(body, name=name, grid=(s // tm,), in_specs=[row, vec, row] + ([row] if has_r else []),
                          out_specs=[row, vec] + ([row] if bf16_copy else []),
                          out_shape=[SDS((s, d), F32), SDS((1, d), F32)] + ([SDS((s, d), BF16)] if bf16_copy else []),
                          compiler_params=_cp(("arbitrary",)))(*ins)


def final_norm_loss(h, g, tgt, name):
    s, d = h.shape
    tm = _tile(s, 256)

    def body(h_ref, g_ref, t_ref, dh_ref, dg_ref, l_ref, dhb_ref):
        xv, gv = h_ref[...], g_ref[...]
        r = lax.rsqrt(jnp.mean(xv * xv, axis=-1, keepdims=True) + RMS_EPS)
        e = xv * r * gv - t_ref[...]
        lpart = 0.5 * jnp.sum(jnp.mean(e * e, axis=-1, keepdims=True), axis=0, keepdims=True)
        dx, part = _rms_bwd_rows(xv, gv, e * (1.0 / d))
        dh_ref[...] = dx
        dhb_ref[...] = dx.astype(BF16)

        @pl.when(pl.program_id(0) == 0)
        def _():
            dg_ref[...] = jnp.zeros_like(dg_ref)
            l_ref[...] = jnp.zeros_like(l_ref)

        dg_ref[...] += part
        l_ref[...] += jnp.broadcast_to(lpart, l_ref.shape)

    row = pl.BlockSpec((tm, d), lambda i: (i, 0))
    vec = pl.BlockSpec((1, d), lambda i: (0, 0))
    lsp = pl.BlockSpec((1, LANE), lambda i: (0, 0))
    return pl.pallas_call(body, name=name, grid=(s // tm,), in_specs=[row, vec, row], out_specs=[row, vec, lsp, row],
                          out_shape=[SDS((s, d), F32), SDS((1, d), F32), SDS((1, LANE), F32), SDS((s, d), BF16)],
                          compiler_params=_cp(("arbitrary",)))(h, g, tgt)


def merge(p_all, ya, yb, yc, wa, wb, wc, dm, lay, name):
    s, d = ya.shape[0], lay.d
    wcols = wa.shape[2]
    bwd = dm is not None
    tm, tn = _tile(s, 1024 if bwd else 2048), _tile(wcols, 512)
    nj, per = d // tn, wcols // tn

    def body(*refs):
        ga, gb, gc, ya_r, yb_r, yc_r, wa_r, wb_r, wc_r = refs[:9]
        ts = [_dot(y[...], w[...], NN) for y, w in ((ya_r, wa_r), (yb_r, wb_r), (yc_r, wc_r))]
        gs = [_sigmoid(g[...]) for g in (ga, gb, gc)]
        if not bwd:
            refs[9][...] = (gs[0] * ts[0] + gs[1] * ts[1] + gs[2] * ts[2]).astype(BF16)
        else:
            dmv = refs[9][...]
            for i in range(3):
                refs[10 + i][...] = (dmv * ts[i] * (gs[i] * (1.0 - gs[i]))).astype(BF16)
                refs[13 + i][...] = (dmv * gs[i]).astype(BF16)

    gate_specs = [pl.BlockSpec((tm, tn), lambda i, j, b=b: (i, b * nj + j)) for b in range(3)]
    y_specs = [pl.BlockSpec((tm, y.shape[1]), lambda i, j: (i, 0)) for y in (ya, yb, yc)]
    w_specs = [pl.BlockSpec((None, w.shape[1], tn), lambda i, j: (j // per, 0, j % per)) for w in (wa, wb, wc)]
    o_spec = pl.BlockSpec((tm, tn), lambda i, j: (i, j))
    n_out = 6 if bwd else 1
    out = pl.pallas_call(
        body, name=name, grid=(s // tm, nj),
        in_specs=gate_specs + y_specs + w_specs + ([o_spec] if bwd else []),
        out_specs=[o_spec] * n_out, out_shape=[SDS((s, d), BF16)] * n_out,
        compiler_params=_cp(("parallel", "parallel")))(p_all, p_all, p_all, ya, yb, yc, wa, wb, wc, *([dm] if bwd else []))
    return out if bwd else out[0]


SWA_PAIRS = SWA_G // 2


def _swa_probs(q, kc, sink, first):
    s = _dot(q, kc, NT) * (SWA_HD ** -0.5)
    qi = lax.broadcasted_iota(jnp.int32, s.shape, 0) % SWA_W
    kj = lax.broadcasted_iota(jnp.int32, s.shape, 1)
    mask = (kj > qi) & (kj <= qi + SWA_W) & ((kj >= SWA_W) | jnp.logical_not(first))
    s = jnp.where(mask, s, NEG)
    m = jnp.maximum(jnp.max(s, axis=-1, keepdims=True), sink)
    p = jnp.exp(s - m)
    es = jnp.exp(sink - m)
    denom = jnp.sum(p, axis=-1, keepdims=True) + es
    return p / denom, es / denom


def _swa_stack(ref, h):
    return jnp.concatenate([ref[:, pl.ds((h * SWA_PAIRS + p) * LANE, LANE)] for p in range(SWA_PAIRS)], axis=0)


def _swa_unstack(ref, h, val):
    for p in range(SWA_PAIRS):
        ref[:, pl.ds((h * SWA_PAIRS + p) * LANE, LANE)] = val[p * SWA_W:(p + 1) * SWA_W]


def _swa_sink_col(sk_ref, h, second):
    pair = lax.broadcasted_iota(jnp.int32, (SWA_PAIRS * SWA_W, 1), 0) // SWA_W
    col = jnp.zeros((SWA_PAIRS * SWA_W, 1), F32)
    for p in range(SWA_PAIRS):
        hh = h * SWA_G + 2 * p + second
        col = jnp.where(pair == p, sk_ref[0:1, hh:hh + 1], col)
    return col


def _swa_kv_tiles(cur_ref, prev_ref, h):
    t = jnp.concatenate([prev_ref[...], cur_ref[...]], axis=0)
    lane = lax.broadcasted_iota(jnp.int32, t.shape, 1)
    moved = pltpu.roll(t, SWA_HD, axis=1)
    low, high = (t, moved) if h == 0 else (moved, t)
    return jnp.where(lane < SWA_HD, low, 0.0).astype(BF16), jnp.where(lane >= SWA_HD, high, 0.0).astype(BF16)


def _swa_kv_grad(g_low, g_high, h):
    lane = lax.broadcasted_iota(jnp.int32, g_low.shape, 1)
    if h == 0:
        return jnp.where(lane < SWA_HD, g_low + pltpu.roll(g_high, SWA_HD, axis=1), 0.0)
    return jnp.where(lane >= SWA_HD, pltpu.roll(g_low, SWA_HD, axis=1) + g_high, 0.0)


def _swa_specs(lay):
    w = SWA_W
    q_spec = pl.BlockSpec((w, Q_W), lambda n: (n, _div(lay.q, Q_W)))
    cur = lambda off: pl.BlockSpec((w, KV_W), lambda n: (n, _div(off, KV_W)))
    prev = lambda off: pl.BlockSpec((w, KV_W), lambda n: (jnp.maximum(n - 1, 0), _div(off, KV_W)))
    return q_spec, cur(lay.k), prev(lay.k), cur(lay.v), prev(lay.v)


def swa_fwd(p_all, sinks, lay, name):
    s = p_all.shape[0]
    nb = _div(s, SWA_W)

    def body(q_ref, kc_ref, kp_ref, vc_ref, vp_ref, sk_ref, o_ref):
        first = pl.program_id(0) == 0
        for h in range(SWA_HKV):
            ks = _swa_kv_tiles(kc_ref, kp_ref, h)
            vs = _swa_kv_tiles(vc_ref, vp_ref, h)
            q = _swa_stack(q_ref, h).astype(BF16)
            ps = [_swa_probs(q, ks[e], _swa_sink_col(sk_ref, h, e), first)[0] for e in range(2)]
            o = _dot(ps[0].astype(BF16), vs[0], NN) + _dot(ps[1].astype(BF16), vs[1], NN)
            _swa_unstack(o_ref, h, o.astype(BF16))

    q_spec, kc_s, kp_s, vc_s, vp_s = _swa_specs(lay)
    return pl.pallas_call(
        body, name=name, grid=(nb,),
        in_specs=[q_spec, kc_s, kp_s, vc_s, vp_s, pl.BlockSpec(sinks.shape, lambda n: (0, 0))],
        out_specs=pl.BlockSpec((SWA_W, Q_W), lambda n: (n, 0)), out_shape=SDS((s, Q_W), BF16),
        compiler_params=_cp(("parallel",)))(p_all, p_all, p_all, p_all, p_all, sinks)


def swa_bwd(p_all, sinks, dy, lay, name):
    s = p_all.shape[0]
    nb = _div(s, SWA_W)
    w = SWA_W

    def body(q_ref, kc_ref, kp_ref, vc_ref, vp_ref, sk_ref, do_ref, dq_ref, dk_ref, dv_ref, ds_ref, kcar, vcar):
        n = pl.program_id(0)
        first = n == 0

        @pl.when(first)
        def _():
            kcar[...] = jnp.zeros_like(kcar)
            vcar[...] = jnp.zeros_like(vcar)
            ds_ref[...] = jnp.zeros_like(ds_ref)

        @pl.when(n < nb)
        def _():
            lane = lax.broadcasted_iota(jnp.int32, (1, LANE), 1)
            dsink = jnp.zeros((1, LANE), F32)
            dk_tile = jnp.zeros((2 * w, KV_W), F32)
            dv_tile = jnp.zeros((2 * w, KV_W), F32)
            for h in range(SWA_HKV):
                ks = _swa_kv_tiles(kc_ref, kp_ref, h)
                vs = _swa_kv_tiles(vc_ref, vp_ref, h)
                q = _swa_stack(q_ref, h).astype(BF16)
                do = _swa_stack(do_ref, h).astype(BF16)
                dq = jnp.zeros((SWA_PAIRS * w, LANE), F32)
                dks, dvs = [], []
                for e in range(2):
                    p, psink = _swa_probs(q, ks[e], _swa_sink_col(sk_ref, h, e), first)
                    dp = _dot(do, vs[e], NT)
                    dvs.append(_dot(p.astype(BF16), do, TN))
                    rs = jnp.sum(dp * p, axis=-1, keepdims=True)
                    dsb = (p * (dp - rs) * (SWA_HD ** -0.5)).astype(BF16)
                    dq = dq + _dot(dsb, ks[e], NN)
                    dks.append(_dot(dsb, q, TN))
                    psr = psink * rs
                    for pr in range(SWA_PAIRS):
                        hh = h * SWA_G + 2 * pr + e
                        dsink = dsink + jnp.where(lane == hh, -jnp.sum(psr[pr * w:(pr + 1) * w], axis=0, keepdims=True), 0.0)
                _swa_unstack(dq_ref, h, dq.astype(BF16))
                dk_tile = dk_tile + _swa_kv_grad(dks[0], dks[1], h)
                dv_tile = dv_tile + _swa_kv_grad(dvs[0], dvs[1], h)
            dk_ref[...] = (kcar[...] + dk_tile[:w]).astype(BF16)
            dv_ref[...] = (vcar[...] + dv_tile[:w]).astype(BF16)
            kcar[...] = dk_tile[w:]
            vcar[...] = dv_tile[w:]
            ds_ref[...] += dsink

        @pl.when(n == nb)
        def _():
            dk_ref[...] = kcar[...].astype(BF16)
            dv_ref[...] = vcar[...].astype(BF16)

    last = nb - 1
    q_spec = pl.BlockSpec((w, Q_W), lambda n: (jnp.minimum(n, last), _div(lay.q, Q_W)))
    cur = lambda off: pl.BlockSpec((w, KV_W), lambda n: (jnp.minimum(n, last), _div(off, KV_W)))
    prev = lambda off: pl.BlockSpec((w, KV_W), lambda n: (jnp.clip(n - 1, 0, last), _div(off, KV_W)))
    row = pl.BlockSpec((w, Q_W), lambda n: (jnp.minimum(n, last), 0))
    kv_out = pl.BlockSpec((w, KV_W), lambda n: (jnp.maximum(n - 1, 0), 0))
    return pl.pallas_call(
        body, name=name, grid=(nb + 1,),
        in_specs=[q_spec, cur(lay.k), prev(lay.k), cur(lay.v), prev(lay.v), pl.BlockSpec(sinks.shape, lambda n: (0, 0)), row],
        out_specs=[row, kv_out, kv_out, pl.BlockSpec((1, LANE), lambda n: (0, 0))],
        out_shape=[SDS((s, Q_W), BF16), SDS((s, KV_W), BF16), SDS((s, KV_W), BF16), SDS((1, LANE), F32)],
        scratch_shapes=[pltpu.VMEM((w, KV_W), F32), pltpu.VMEM((w, KV_W), F32)],
        compiler_params=_cp(("arbitrary",)))(p_all, p_all, p_all, p_all, p_all, sinks, dy)


def _xa_probs(q, mk):
    s = _dot(q, mk, NT) * (XA_D ** -0.5)
    p = jnp.exp(s - jnp.max(s, axis=-1, keepdims=True))
    return p / jnp.sum(p, axis=-1, keepdims=True)


def xattn_fwd(p_all, mkv, lay, name):
    s, nm = p_all.shape[0], mkv.shape[0]
    tm = _tile(s, 512)

    def body(q_ref, mkv_ref, o_ref):
        for h in range(XA_H):
            cols = pl.ds(h * XA_D, XA_D)
            p = _xa_probs(q_ref[:, cols].astype(BF16), mkv_ref[:, cols])
            o_ref[:, cols] = _dot(p.astype(BF16), mkv_ref[:, pl.ds(XA_W + h * XA_D, XA_D)], NN).astype(BF16)

    return pl.pallas_call(
        body, name=name, grid=(s // tm,),
        in_specs=[pl.BlockSpec((tm, XA_W), lambda i: (i, _div(lay.qc, XA_W))), pl.BlockSpec((nm, 2 * XA_W), lambda i: (0, 0))],
        out_specs=pl.BlockSpec((tm, XA_W), lambda i: (i, 0)), out_shape=SDS((s, XA_W), BF16),
        compiler_params=_cp(("parallel",)))(p_all, mkv)


def xattn_bwd(p_all, mkv, dy, lay, name):
    s, nm = p_all.shape[0], mkv.shape[0]
    tm = _tile(s, 512)

    def body(q_ref, mkv_ref, do_ref, dq_ref, dmkv_ref):
        @pl.when(pl.program_id(0) == 0)
        def _():
            dmkv_ref[...] = jnp.zeros_like(dmkv_ref)

        for h in range(XA_H):
            cols = pl.ds(h * XA_D, XA_D)
            vcols = pl.ds(XA_W + h * XA_D, XA_D)
            q = q_ref[:, cols].astype(BF16)
            do = do_ref[:, cols].astype(BF16)
            p = _xa_probs(q, mkv_ref[:, cols])
            dp = _dot(do, mkv_ref[:, vcols], NT)
            dmkv_ref[:, vcols] += _dot(p.astype(BF16), do, TN)
            dsb = (p * (dp - jnp.sum(dp * p, axis=-1, keepdims=True)) * (XA_D ** -0.5)).astype(BF16)
            dq_ref[:, cols] = _dot(dsb, mkv_ref[:, cols], NN).astype(BF16)
            dmkv_ref[:, cols] += _dot(dsb, q, TN)

    row = pl.BlockSpec((tm, XA_W), lambda i: (i, 0))
    full = pl.BlockSpec((nm, 2 * XA_W), lambda i: (0, 0))
    return pl.pallas_call(
        body, name=name, grid=(s // tm,),
        in_specs=[pl.BlockSpec((tm, XA_W), lambda i: (i, _div(lay.qc, XA_W))), full, row],
        out_specs=[row, full], out_shape=[SDS((s, XA_W), BF16), SDS((nm, 2 * XA_W), F32)],
        compiler_params=_cp(("arbitrary",)))(p_all, mkv, dy)


def _shift_down(cur, prev8, s):
    cat = jnp.concatenate([prev8, cur[0:8]], axis=0)
    return pltpu.roll(cur, s, axis=0), pltpu.roll(cat, s, axis=0)[8:16]


def _shift_up(cur, next8, s):
    tm = cur.shape[0]
    cat = jnp.concatenate([cur[tm - 8:tm], next8], axis=0)
    return pltpu.roll(cur, tm - s, axis=0), pltpu.roll(cat, 16 - s, axis=0)[0:8]


def gdn_conv_fwd(p_all, conv_w, lay, name):
    s = p_all.shape[0]
    tm = _tile(s, 512)
    c0 = _div(lay.qkv, GDN_W)

    def body(x_ref, prev_ref, w_ref, o_ref):
        cur = x_ref[...]
        prev8 = jnp.where(pl.program_id(1) > 0, prev_ref[...], 0.0)
        main = w_ref[GDN_CONV - 1:GDN_CONV, :] * cur
        top = w_ref[GDN_CONV - 1:GDN_CONV, :] * cur[0:8]
        for sft in range(1, GDN_CONV):
            wi = w_ref[GDN_CONV - 1 - sft:GDN_CONV - sft, :]
            a, b = _shift_down(cur, prev8, sft)
            main = main + wi * a
            top = top + wi * b
        o_ref[...] = main
        o_ref[0:8, :] = top

    return pl.pallas_call(
        body, name=name, grid=(3, s // tm),
        in_specs=[pl.BlockSpec((tm, GDN_W), lambda c, i: (i, c0 + c)),
                  pl.BlockSpec((8, GDN_W), lambda c, i: (jnp.maximum(i * (tm // 8) - 1, 0), c0 + c)),
                  pl.BlockSpec((GDN_CONV, GDN_W), lambda c, i: (0, c))],
        out_specs=pl.BlockSpec((tm, GDN_W), lambda c, i: (i, c)), out_shape=SDS((s, 3 * GDN_W), F32),
        compiler_params=_cp(("parallel", "parallel")))(p_all, p_all, conv_w)


def gdn_conv_bwd(p_all, conv_w, dxc, lay, name):
    s = p_all.shape[0]
    tm = _tile(s, 512)
    c0 = _div(lay.qkv, GDN_W)
    nt = s // tm

    def body(x_ref, prev_ref, d_ref, next_ref, w_ref, dx_ref, dw_ref):
        i = pl.program_id(1)
        cur, d = x_ref[...], d_ref[...]
        prev8 = jnp.where(i > 0, prev_ref[...], 0.0)
        next8 = jnp.where(i < nt - 1, next_ref[...], 0.0)
        row = lax.broadcasted_iota(jnp.int32, (tm, 1), 0)
        main = w_ref[GDN_CONV - 1:GDN_CONV, :] * d
        bot = w_ref[GDN_CONV - 1:GDN_CONV, :] * d[tm - 8:tm]
        dws = [jnp.sum(d * cur, axis=0, keepdims=True)]
        for sft in range(1, GDN_CONV):
            wi = w_ref[GDN_CONV - 1 - sft:GDN_CONV - sft, :]
            a, b = _shift_up(d, next8, sft)
            main = main + wi * a
            bot = bot + wi * b
            xa, xb = _shift_down(cur, prev8, sft)
            dws.append(jnp.sum(jnp.where(row >= 8, d * xa, 0.0), axis=0, keepdims=True)
                       + jnp.sum(d[0:8] * xb, axis=0, keepdims=True))
        dx_ref[...] = main.astype(BF16)
        dx_ref[tm - 8:tm, :] = bot.astype(BF16)

        @pl.when(i == 0)
        def _():
            dw_ref[...] = jnp.zeros_like(dw_ref)

        for sft in range(GDN_CONV):
            dw_ref[GDN_CONV - 1 - sft:GDN_CONV - sft, :] += dws[sft]

    return pl.pallas_call(
        body, name=name, grid=(3, nt),
        in_specs=[pl.BlockSpec((tm, GDN_W), lambda c, i: (i, c0 + c)),
                  pl.BlockSpec((8, GDN_W), lambda c, i: (jnp.maximum(i * (tm // 8) - 1, 0), c0 + c)),
                  pl.BlockSpec((tm, GDN_W), lambda c, i: (i, c)),
                  pl.BlockSpec((8, GDN_W), lambda c, i: (jnp.minimum((i + 1) * (tm // 8), s // 8 - 1), c)),
                  pl.BlockSpec((GDN_CONV, GDN_W), lambda c, i: (0, c))],
        out_specs=[pl.BlockSpec((tm, GDN_W), lambda c, i: (i, c)), pl.BlockSpec((GDN_CONV, GDN_W), lambda c, i: (0, c))],
        out_shape=[SDS((s, 3 * GDN_W), BF16), SDS((GDN_CONV, 3 * GDN_W), F32)],
        compiler_params=_cp(("parallel", "arbitrary")))(p_all, p_all, dxc, dxc, conv_w)


def _gdn_chunk(xq, xk, xv, ab, gp, bdot=_bdot_plain):
    c = GDN_C
    nc = xq.shape[0] // c
    lane = lax.broadcasted_iota(jnp.int32, (c, LANE), 1)
    row = lax.broadcasted_iota(jnp.int32, (c, c), 0)
    col = lax.broadcasted_iota(jnp.int32, (c, c), 1)
    g_tile = -jnp.exp(gp[0:1, :]) * _softplus(ab + gp[1:2, :])
    b_tile = _sigmoid(ab)
    tri = (row >= col).astype(F32)
    qa, ka, va = _silu(xq), _silu(xk), _silu(xv)
    items = []
    for ci in range(nc):
        rs = slice(ci * c, (ci + 1) * c)
        gcum = _dot(tri, g_tile[rs], NN, HI)
        gcum_t = gcum.T
        for h in range(GDN_H):
            hs = slice(h * GDN_D, (h + 1) * GDN_D)
            q, k, v = qa[rs, hs], ka[rs, hs], va[rs, hs]
            q = q * lax.rsqrt(jnp.sum(q * q, axis=-1, keepdims=True) + L2_EPS) * (GDN_D ** -0.5)
            k = k * lax.rsqrt(jnp.sum(k * k, axis=-1, keepdims=True) + L2_EPS)
            gc = jnp.sum(jnp.where(lane == h, gcum, 0.0), axis=1, keepdims=True)
            beta = jnp.sum(jnp.where(lane == GDN_H + h, b_tile[rs], 0.0), axis=1, keepdims=True)
            decay = jnp.exp(jnp.where(row >= col, gc - gcum_t[h:h + 1, :], NEG))
            items.append((q, k, v, gc, beta, decay))
    kks = [bdot(k, k, NT) for (_, k, _, _, _, _) in items]
    pws = [-jnp.where(row > col, it[4] * kk * it[5], 0.0) for it, kk in zip(items, kks)]
    nns = list(pws)
    for _ in range(5):
        pws = [bdot(p, p, NN) for p in pws]
        nns = [n + p + bdot(n, p, NN) for n, p in zip(nns, pws)]
    qks = [bdot(q, k, NT) for (q, k, _, _, _, _) in items]
    out = []
    for (q, k, v, gc, beta, decay), n, qk in zip(items, nns, qks):
        eg = jnp.exp(gc)
        vb = v * beta
        kbe = k * (beta * eg)
        gl = gc[c - 1:c, :]
        out.append((vb + bdot(n, vb, NN), kbe + bdot(n, kbe, NN), q * eg, k * jnp.exp(gl - gc), qk * decay, jnp.exp(gl)))
    return [out[ci * GDN_H:(ci + 1) * GDN_H] for ci in range(nc)]


GDN_CPS = 4


def _gdn_pre_specs(lay, t):
    xspec = lambda j: pl.BlockSpec((t, GDN_W), lambda n, j=j: (n, j))
    return [xspec(0), xspec(1), xspec(2), pl.BlockSpec((t, LANE), lambda n: (n, _div(lay.ab, LANE))),
            pl.BlockSpec((8, LANE), lambda n: (0, 0))]


def gdn_pre_fwd(xc, p_all, gp, lay, name):
    s = xc.shape[0]
    c = GDN_C
    n = _div(s, c)
    cps = _tile(n, GDN_CPS)
    t = cps * c

    def body(xq, xk, xv, ab, gp_ref, u_ref, w_ref, qd_ref, kd_ref, qk_ref, gl_ref):
        lane = lax.broadcasted_iota(jnp.int32, (1, LANE), 1)
        chunks = _gdn_chunk(xq[...], xk[...], xv[...], ab[...], gp_ref[...])
        for ci, heads in enumerate(chunks):
            rs = pl.ds(ci * c, c)
            gl_row = jnp.zeros((1, LANE), F32)
            for h, (u, w, qd, kd, qk, gl) in enumerate(heads):
                hs = pl.ds(h * GDN_D, GDN_D)
                u_ref[rs, hs] = u
                w_ref[rs, hs] = w.astype(BF16)
                qd_ref[rs, hs] = qd.astype(BF16)
                kd_ref[rs, hs] = kd.astype(BF16)
                qk_ref[rs, pl.ds(h * c, c)] = qk.astype(BF16)
                gl_row = gl_row + jnp.where(lane == h, gl, 0.0)
            gl_ref[ci] = gl_row

    row = pl.BlockSpec((t, GDN_W), lambda n: (n, 0))
    return pl.pallas_call(
        body, name=name, grid=(n // cps,), in_specs=_gdn_pre_specs(lay, t),
        out_specs=[row, row, row, row, pl.BlockSpec((t, GDN_H * c), lambda n: (n, 0)), pl.BlockSpec((cps, 1, LANE), lambda n: (n, 0, 0))],
        out_shape=[SDS((s, GDN_W), F32), SDS((s, GDN_W), BF16), SDS((s, GDN_W), BF16), SDS((s, GDN_W), BF16),
                   SDS((s, GDN_H * c), BF16), SDS((n, 1, LANE), F32)],
        compiler_params=_cp(("parallel",)))(xc, xc, xc, p_all, gp)


def gdn_pre_bwd(xc, p_all, gp, du, dw, dqd, dkd, dqk, dgl, lay, name):
    s = xc.shape[0]
    c = GDN_C
    n = _div(s, c)
    cps = _tile(n, GDN_CPS)
    t = cps * c
    chunk = functools.partial(_gdn_chunk, bdot=_bdot_vjp)

    def body(xq, xk, xv, ab, gp_ref, du_r, dw_r, dqd_r, dkd_r, dqk_r, dgl_r, dxc_ref, dab_ref, dgp_ref):
        lane = lax.broadcasted_iota(jnp.int32, (1, LANE), 1)
        _, vjp = jax.vjp(chunk, xq[...], xk[...], xv[...], ab[...], gp_ref[...])
        cts = []
        for ci in range(cps):
            rs = pl.ds(ci * c, c)
            heads = []
            for h in range(GDN_H):
                hs = pl.ds(h * GDN_D, GDN_D)
                dgl_h = jnp.sum(jnp.where(lane == h, dgl_r[ci], 0.0), axis=1, keepdims=True)
                heads.append((du_r[rs, hs], dw_r[rs, hs], dqd_r[rs, hs], dkd_r[rs, hs], dqk_r[rs, pl.ds(h * c, c)], dgl_h))
            cts.append(heads)
        dq, dk, dv, dab, dgp = vjp(cts)
        dxc_ref[:, pl.ds(0, GDN_W)] = dq
        dxc_ref[:, pl.ds(GDN_W, GDN_W)] = dk
        dxc_ref[:, pl.ds(2 * GDN_W, GDN_W)] = dv
        dab_ref[...] = dab.astype(BF16)

        @pl.when(pl.program_id(0) == 0)
        def _():
            dgp_ref[...] = jnp.zeros_like(dgp_ref)

        dgp_ref[...] += dgp

    row = pl.BlockSpec((t, GDN_W), lambda n: (n, 0))
    return pl.pallas_call(
        body, name=name, grid=(n // cps,),
        in_specs=_gdn_pre_specs(lay, t) + [row, row, row, row, pl.BlockSpec((t, GDN_H * c), lambda n: (n, 0)),
                                           pl.BlockSpec((cps, 1, LANE), lambda n: (n, 0, 0))],
        out_specs=[pl.BlockSpec((t, 3 * GDN_W), lambda n: (n, 0)), pl.BlockSpec((t, LANE), lambda n: (n, 0)),
                   pl.BlockSpec((8, LANE), lambda n: (0, 0))],
        out_shape=[SDS((s, 3 * GDN_W), F32), SDS((s, LANE), BF16), SDS((8, LANE), F32)],
        compiler_params=_cp(("arbitrary",)))(xc, xc, xc, p_all, gp, du, dw, dqd, dkd, dqk, dgl)


def _lane_scalar(row, h):
    lane = lax.broadcasted_iota(jnp.int32, row.shape, 1)
    return jnp.sum(jnp.where(lane == h, row, 0.0), axis=1, keepdims=True)


def gdn_scan_fwd(u, w, qd, kd, qk, gl, name):
    s = u.shape[0]
    c = GDN_C
    n = _div(s, c)

    def body(u_r, w_r, qd_r, kd_r, qk_r, gl_r, o_ref, s_ref, st):
        @pl.when(pl.program_id(0) == 0)
        def _():
            st[...] = jnp.zeros_like(st)

        s_ref[0] = st[...]
        heads = range(GDN_H)
        hs = [pl.ds(h * GDN_D, GDN_D) for h in heads]
        sh = [st[hs[h], :] for h in heads]
        shb = [x.astype(BF16) for x in sh]
        ws = [_dot(w_r[:, hs[h]], shb[h], NN) for h in heads]
        qs = [_dot(qd_r[:, hs[h]], shb[h], NN) for h in heads]
        vb = [(u_r[:, hs[h]] - ws[h]).astype(BF16) for h in heads]
        ov = [_dot(qk_r[:, pl.ds(h * c, c)], vb[h], NN) for h in heads]
        kv = [_dot(kd_r[:, hs[h]], vb[h], TN) for h in heads]
        for h in heads:
            o_ref[:, hs[h]] = qs[h] + ov[h]
            st[hs[h], :] = sh[h] * _lane_scalar(gl_r[0], h) + kv[h]

    row = pl.BlockSpec((c, GDN_W), lambda i: (i, 0))
    return pl.pallas_call(
        body, name=name, grid=(n,),
        in_specs=[row, row, row, row, pl.BlockSpec((c, GDN_H * c), lambda i: (i, 0)), pl.BlockSpec((1, 1, LANE), lambda i: (i, 0, 0))],
        out_specs=[row, pl.BlockSpec((1, GDN_W, GDN_D), lambda i: (i, 0, 0))],
        out_shape=[SDS((s, GDN_W), F32), SDS((n, GDN_W, GDN_D), F32)],
        scratch_shapes=[pltpu.VMEM((GDN_W, GDN_D), F32)],
        compiler_params=_cp(("arbitrary",)))(u, w, qd, kd, qk, gl)


def gdn_scan_bwd(u, w, qd, kd, qk, gl, states, do, name):
    s = u.shape[0]
    c = GDN_C
    n = _div(s, c)

    def body(u_r, w_r, qd_r, kd_r, qk_r, gl_r, s_r, do_r, du_o, dw_o, dqd_o, dkd_o, dqk_o, dgl_o, dst):
        @pl.when(pl.program_id(0) == 0)
        def _():
            dst[...] = jnp.zeros_like(dst)

        lane = lax.broadcasted_iota(jnp.int32, (1, LANE), 1)
        heads = range(GDN_H)
        hs = [pl.ds(h * GDN_D, GDN_D) for h in heads]
        qs = [pl.ds(h * c, c) for h in heads]
        sh = [s_r[0, hs[h], :] for h in heads]
        shb = [x.astype(BF16) for x in sh]
        ds_out = [dst[hs[h], :] for h in heads]
        dsb = [x.astype(BF16) for x in ds_out]
        dob = [do_r[:, hs[h]].astype(BF16) for h in heads]
        ws = [_dot(w_r[:, hs[h]], shb[h], NN) for h in heads]
        dv1 = [_dot(qk_r[:, qs[h]], dob[h], TN) for h in heads]
        dv2 = [_dot(kd_r[:, hs[h]], dsb[h], NN) for h in heads]
        dqd = [_dot(dob[h], shb[h], NT) for h in heads]
        dsq = [_dot(qd_r[:, hs[h]], dob[h], TN) for h in heads]
        vb = [(u_r[:, hs[h]] - ws[h]).astype(BF16) for h in heads]
        dv = [dv1[h] + dv2[h] for h in heads]
        dvb = [x.astype(BF16) for x in dv]
        dw = [_dot(dvb[h], shb[h], NT) for h in heads]
        dkd = [_dot(vb[h], dsb[h], NT) for h in heads]
        dqk = [_dot(dob[h], vb[h], NT) for h in heads]
        dsw = [_dot(w_r[:, hs[h]], dvb[h], TN) for h in heads]
        dgl_row = jnp.zeros((1, LANE), F32)
        for h in heads:
            du_o[:, hs[h]] = dv[h]
            dw_o[:, hs[h]] = -dw[h]
            dqd_o[:, hs[h]] = dqd[h]
            dkd_o[:, hs[h]] = dkd[h]
            dqk_o[:, qs[h]] = dqk[h]
            dgl_row = dgl_row + jnp.where(lane == h, jnp.sum(jnp.sum(ds_out[h] * sh[h], axis=1, keepdims=True), axis=0, keepdims=True), 0.0)
            dst[hs[h], :] = ds_out[h] * _lane_scalar(gl_r[0], h) + dsq[h] - dsw[h]
        dgl_o[0] = dgl_row

    rev = lambda i: n - 1 - i
    row = pl.BlockSpec((c, GDN_W), lambda i: (rev(i), 0))
    qks = pl.BlockSpec((c, GDN_H * c), lambda i: (rev(i), 0))
    gls = pl.BlockSpec((1, 1, LANE), lambda i: (rev(i), 0, 0))
    return pl.pallas_call(
        body, name=name, grid=(n,),
        in_specs=[row, row, row, row, qks, gls, pl.BlockSpec((1, GDN_W, GDN_D), lambda i: (rev(i), 0, 0)), row],
        out_specs=[row, row, row, row, qks, gls],
        out_shape=[SDS((s, GDN_W), F32)] * 4 + [SDS((s, GDN_H * c), F32), SDS((n, 1, LANE), F32)],
        scratch_shapes=[pltpu.VMEM((GDN_W, GDN_D), F32)],
        compiler_params=_cp(("arbitrary",)))(u, w, qd, kd, qk, gl, states, do)


def _gdn_out_rows(o, z, nw):
    outs = []
    for h in range(GDN_H):
        hs = slice(h * GDN_D, (h + 1) * GDN_D)
        oh = o[:, hs]
        y = oh * lax.rsqrt(jnp.mean(oh * oh, axis=-1, keepdims=True) + RMS_EPS) * nw
        outs.append(y * _silu(z[:, hs]))
    return jnp.concatenate(outs, axis=1)


def gdn_out(o, p_all, nw, dy, lay, name):
    s = o.shape[0]
    tm = _tile(s, 512)
    bwd = dy is not None

    def body(*refs):
        o_r, z_r, nw_r = refs[:3]
        if not bwd:
            refs[3][...] = _gdn_out_rows(o_r[...], z_r[...], nw_r[...]).astype(BF16)
            return
        dy_r, do_o, dz_o, dnw_o = refs[3:]
        _, vjp = jax.vjp(_gdn_out_rows, o_r[...], z_r[...], nw_r[...])
        d_o, d_z, d_nw = vjp(dy_r[...].astype(F32))
        do_o[...] = d_o
        dz_o[...] = d_z.astype(BF16)

        @pl.when(pl.program_id(0) == 0)
        def _():
            dnw_o[...] = jnp.zeros_like(dnw_o)

        dnw_o[...] += d_nw

    row = pl.BlockSpec((tm, GDN_W), lambda i: (i, 0))
    zs = pl.BlockSpec((tm, GDN_W), lambda i: (i, _div(lay.z, GDN_W)))
    nws = pl.BlockSpec((1, GDN_D), lambda i: (0, 0))
    if not bwd:
        return pl.pallas_call(body, name=name, grid=(s // tm,), in_specs=[row, zs, nws], out_specs=row,
                              out_shape=SDS((s, GDN_W), BF16), compiler_params=_cp(("parallel",)))(o, p_all, nw)
    return pl.pallas_call(body, name=name, grid=(s // tm,), in_specs=[row, zs, nws, row], out_specs=[row, row, nws],
                          out_shape=[SDS((s, GDN_W), F32), SDS((s, GDN_W), BF16), SDS((1, GDN_D), F32)],
                          compiler_params=_cp(("arbitrary",)))(o, p_all, nw, dy)


def _cols_to_full(g):
    n, k, c = g.shape
    return g.transpose(1, 0, 2).reshape(k, n * c)


def _rows_to_blocks(w):
    return w.reshape(N_DEV, w.shape[0] // N_DEV, w.shape[1])


def _pack_small(parts, rows):
    flat = jnp.concatenate([jnp.pad(p.reshape(-1), (0, -p.size % LANE)) for p in parts])
    return jnp.pad(flat, (0, rows * LANE - flat.size)).reshape(rows, LANE)


def kernel(x, mem, g_mix, w_in, sinks, conv_w, a_log, dt_bias, gdn_norm_w, g_mem, w_mem_kv, w_swa_up, w_gdn_up, w_xa_up, w_out, g_mlp, w_mlp_in, w_mlp_out, g_final, loss_target, m_g_mix, m_w_in, m_sinks, m_conv_w, m_a_log, m_dt_bias, m_gdn_norm_w, m_g_mem, m_w_mem_kv, m_w_swa_up, m_w_gdn_up, m_w_xa_up, m_w_out, m_g_mlp, m_w_mlp_in, m_w_mlp_out, m_g_final, v_g_mix, v_w_in, v_sinks, v_conv_w, v_a_log, v_dt_bias, v_gdn_norm_w, v_g_mem, v_w_mem_kv, v_w_swa_up, v_w_gdn_up, v_w_xa_up, v_w_out, v_g_mlp, v_w_mlp_in, v_w_mlp_out, v_g_final):
    xs, ms, tgt = x[0], mem[0], loss_target[0]
    s, d = xs.shape
    lay = Layout(d)
    px, py, pc = _position()
    dev = 4 * px + 2 * py + pc

    g_in, g_conv = all_gather([w_in[0].astype(BF16), conv_w[0]], "gather_w_in")
    W_in = pad_w_in(g_in, lay)
    convw = _cols_to_full(g_conv)
    gp = jnp.zeros((8, LANE), F32).at[0, :GDN_H].set(a_log[0]).at[1, :GDN_H].set(dt_bias[0])
    later = [w_mem_kv[0], w_swa_up[0], w_gdn_up[0], w_xa_up[0], w_out[0], w_mlp_in[0]]

    n1 = rmsnorm_fwd(xs, g_mix, "norm_mix")
    p_all, g_mkv, W_sup, W_gup, W_xup, g_out, W_m1 = matmul(
        n1, W_in, mode="nn", out_dtype=F32, name="proj_in", tm=2048, tn=512, tk=d,
        side=GatherJob([w.astype(BF16) for w in later]))
    W_mkv = g_mkv.reshape(-1, g_mkv.shape[2])
    W_out = g_out.reshape(-1, d)
    y_a = swa_fwd(p_all, sinks, lay, "swa_fwd")
    xc = gdn_conv_fwd(p_all, convw, lay, "gdn_conv_fwd")
    u, gw, gqd, gkd, gqk, ggl = gdn_pre_fwd(xc, p_all, gp, lay, "gdn_pre_fwd")
    o_b, states = gdn_scan_fwd(u, gw, gqd, gkd, gqk, ggl, "gdn_scan_fwd")
    y_b = gdn_out(o_b, p_all, gdn_norm_w, None, lay, "gdn_out_fwd")
    nm = rmsnorm_fwd(ms, g_mem, "norm_mem")
    mkv = matmul(nm, W_mkv, mode="nn", out_dtype=BF16, name="proj_mem", tk=d)
    y_c = xattn_fwd(p_all, mkv, lay, "xattn_fwd")
    merged = merge(p_all, y_a, y_b, y_c, W_sup, W_gup, W_xup, None, lay, "merge_fwd")
    h1 = matmul(merged, W_out, mode="nn", out_dtype=F32, name="proj_out", tm=2048, tn=512, tk=d, resid=xs)
    n2 = rmsnorm_fwd(h1, g_mlp, "norm_mlp")
    uu, act, g_m2 = matmul(n2, W_m1, mode="nn", out_dtype=F32, name="mlp_in", tm=2048, tn=512, tk=d, b_cols=True,
                           relu2_out=True, side=GatherJob([w_mlp_out[0].astype(BF16)]))
    W_m2 = g_m2.reshape(-1, d)
    h2 = matmul(act, W_m2, mode="nn", out_dtype=F32, name="mlp_out", tm=1024, tn=2048, tk=512, resid=h1)
    dh2, dg_final, lrow, dh2_b = final_norm_loss(h2, g_final.reshape(1, d), tgt, "final_loss")
    loss = lax.psum(lrow[0, 0], ("x", "y", "c"))

    du = matmul(dh2_b, W_m2, mode="nt", out_dtype=BF16, name="mlp_out_dx", tm=2048, tn=512, tk=d, relu2_grad_of=uu)
    dW_m2 = matmul(act, dh2_b, mode="tn", out_dtype=BF16, name="mlp_out_dw", tm=1024, tn=2048, tk=1024)
    dn2 = matmul(du, W_m1, mode="nt", out_dtype=F32, name="mlp_in_dx", tm=1024, tn=2048, tk=1024, b_cols=True)
    dW_m1 = matmul(n2, du, mode="tn", out_dtype=BF16, name="mlp_in_dw", tm=2048, tn=1024, tk=512)
    dh1, dg_mlp, dh1_b = rmsnorm_bwd(h1, g_mlp, dn2, dh2, "norm_mlp_bwd", bf16_copy=True)

    dmerged = matmul(dh1_b, W_out, mode="nt", out_dtype=F32, name="proj_out_dx", tm=2048, tn=512, tk=d)
    dW_out = matmul(merged, dh1_b, mode="tn", out_dtype=BF16, name="proj_out_dw", tm=2048, tn=1024, tk=512)
    dga, dgb, dgc, dta, dtb, dtc = merge(p_all, y_a, y_b, y_c, W_sup, W_gup, W_xup, dmerged, lay, "merge_bwd")
    dy_a = matmul(dta, W_sup, mode="nt", out_dtype=BF16, name="swa_up_dx", tm=2048, tk=d, b_cols=True)
    dy_b = matmul(dtb, W_gup, mode="nt", out_dtype=BF16, name="gdn_up_dx", tm=2048, tk=d, b_cols=True)
    dy_c = matmul(dtc, W_xup, mode="nt", out_dtype=BF16, name="xa_up_dx", tm=2048, tk=d, b_cols=True)
    dW_sup = matmul(y_a, dta, mode="tn", out_dtype=BF16, name="swa_up_dw", tn=2048)
    dW_gup = matmul(y_b, dtb, mode="tn", out_dtype=BF16, name="gdn_up_dw", tn=2048)
    dW_xup = matmul(y_c, dtc, mode="tn", out_dtype=BF16, name="xa_up_dw", tn=2048)

    dq_a, dk_a, dv_a, dsinks = swa_bwd(p_all, sinks, dy_a, lay, "swa_bwd")
    dq_c, dmkv = xattn_bwd(p_all, mkv, dy_c, lay, "xattn_bwd")
    dW_mkv = matmul(nm, dmkv, mode="tn", out_dtype=BF16, name="proj_mem_dw", tk=256)
    dnm = matmul(dmkv, W_mkv, mode="nt", out_dtype=F32, name="proj_mem_dx", tk=1024)
    _, dg_mem = rmsnorm_bwd(ms, g_mem, dnm, None, "norm_mem_bwd")

    do_b, dz, dnorm_w = gdn_out(o_b, p_all, gdn_norm_w, dy_b, lay, "gdn_out_bwd")
    du_g, dw_g, dqd_g, dkd_g, dqk_g, dgl_g = gdn_scan_bwd(u, gw, gqd, gkd, gqk, ggl, states, do_b, "gdn_scan_bwd")
    dxc, dab, dgp = gdn_pre_bwd(xc, p_all, gp, du_g, dw_g, dqd_g, dkd_g, dqk_g, dgl_g, lay, "gdn_pre_bwd")
    dqkv, dconv = gdn_conv_bwd(p_all, convw, dxc, lay, "gdn_conv_bwd")

    dp_all = jnp.concatenate([dga, dgb, dgc, dq_a, dqkv, dz, dq_c, dk_a, dv_a, dab,
                              jnp.zeros((s, lay.pw - lay.end), BF16)], axis=1)
    def pair_stage(grads, cols, tag):
        from_sib = pair_exchange(grads, cols, "grads_pair_exchange_" + tag)
        return [pair_add(g, cl, o, "grads_pair_add_%s%d" % (tag, i)) for i, (g, cl, o) in enumerate(zip(grads, cols, from_sib))]

    early = pair_stage([_rows_to_blocks(dW_mkv), dW_sup, dW_gup, dW_xup, _rows_to_blocks(dW_out), dW_m1, _rows_to_blocks(dW_m2)],
                       [False, True, True, True, False, True, False], "a")
    dW_in, p_m1, p_m2 = matmul(n1, dp_all, mode="tn", out_dtype=BF16, name="proj_in_dw", tm=2048, tn=1024, tk=512,
                               side=ChipExchangeJob(early[5:]))
    late = pair_stage([unpad_dw_in(dW_in, lay)], [False], "b")
    dn1, p_in, p_mkv, p_sup, p_gup, p_xup, p_out = matmul(
        dp_all, W_in, mode="nt", out_dtype=F32, name="proj_in_dx", tm=1024, tn=2048, tk=1024,
        side=ChipExchangeJob(late + early[:5]))
    grad_x, dg_mix = rmsnorm_bwd(xs, g_mix, dn1, dh1, "norm_mix_bwd")
    parts = [p_in, p_mkv, p_sup, p_gup, p_xup, p_out, p_m1, p_m2]

    shard_names = [(w_in, m_w_in, v_w_in), (w_mem_kv, m_w_mem_kv, v_w_mem_kv), (w_swa_up, m_w_swa_up, v_w_swa_up),
                   (w_gdn_up, m_w_gdn_up, v_w_gdn_up), (w_xa_up, m_w_xa_up, v_w_xa_up), (w_out, m_w_out, v_w_out),
                   (w_mlp_in, m_w_mlp_in, v_w_mlp_in), (w_mlp_out, m_w_mlp_out, v_w_mlp_out)]
    big_res = [adamw(p, w[0], m[0], v[0], "adamw_%d" % i) for i, (p, (w, m, v)) in enumerate(zip(parts, shard_names))]

    smalls = [(g_mix, m_g_mix, v_g_mix, dg_mix), (sinks, m_sinks, v_sinks, dsinks[:, :SWA_HQ]),
              (a_log, m_a_log, v_a_log, dgp[0:1, :GDN_H]), (dt_bias, m_dt_bias, v_dt_bias, dgp[1:2, :GDN_H]),
              (gdn_norm_w, m_gdn_norm_w, v_gdn_norm_w, dnorm_w), (g_mem, m_g_mem, v_g_mem, dg_mem),
              (g_mlp, m_g_mlp, v_g_mlp, dg_mlp), (g_final, m_g_final, v_g_final, dg_final)]
    sizes = [-(-t[0].size // LANE) * LANE for t in smalls] + [GDN_CONV * 3 * GDN_W]
    rows = -(-sum(sizes) // (8 * LANE)) * 8
    csh = conv_w.shape[2]

    def conv_place(a):
        full = jnp.tile(a[0], (1, N_DEV))
        owner = lax.broadcasted_iota(jnp.int32, full.shape, 1) // csh
        return jnp.where(owner == dev, full, 0.0)

    g_pack = _pack_small([t[3] for t in smalls] + [dconv], rows)
    w_pack = _pack_small([t[0] for t in smalls] + [conv_place(conv_w)], rows)
    m_pack = _pack_small([t[1] for t in smalls] + [conv_place(m_conv_w)], rows)
    v_pack = _pack_small([t[2] for t in smalls] + [conv_place(v_conv_w)], rows)
    g_all = all_gather([g_pack], "gather_small_grads")[0]
    small_res = adamw(g_all, w_pack, m_pack, v_pack, "adamw_small")

    def unpack(arr):
        flat = arr.reshape(-1)
        outs, off = [], 0
        for t, sz in zip(smalls, sizes[:-1]):
            outs.append(flat[off:off + t[0].size].reshape(t[0].shape))
            off += sz
        cw = flat[off:off + sizes[-1]].reshape(GDN_CONV, 3 * GDN_W)
        mine = (lax.broadcasted_iota(jnp.int32, (1, N_DEV, 1), 1) == dev).astype(F32)
        outs.append(jnp.sum(cw.reshape(GDN_CONV, N_DEV, csh) * mine, axis=1)[None])
        return outs

    sg, sd, sm, sv = (unpack(a) for a in small_res)
    bg, bd, bm, bv = ([r[i][None] for r in big_res] for i in range(4))

    def ordered(sm_, bg_):
        return [sm_[0], bg_[0], sm_[1], sm_[8], sm_[2], sm_[3], sm_[4], sm_[5], bg_[1], bg_[2], bg_[3], bg_[4], bg_[5],
                sm_[6], bg_[6], bg_[7], sm_[7]]

    return (loss, grad_x[None], *ordered(sg, bg), *ordered(sd, bd), *ordered(sm, bm), *ordered(sv, bv))
```

```python
import functools
import math

import jax
import jax.numpy as jnp
from jax import lax
from jax.experimental import pallas as pl
from jax.experimental.pallas import tpu as pltpu

F32, BF16 = jnp.float32, jnp.bfloat16
SDS = jax.ShapeDtypeStruct
MESH = pl.DeviceIdType.MESH
ANY = pl.BlockSpec(memory_space=pl.ANY)

SWA_HQ, SWA_HKV, SWA_HD, SWA_W = 16, 2, 64, 128
SWA_G = SWA_HQ // SWA_HKV
GDN_H, GDN_D, GDN_CONV, GDN_C = 4, 128, 4, 64
XA_H, XA_D = 4, 128
Q_W = SWA_HQ * SWA_HD
KV_W = SWA_HKV * SWA_HD
GDN_W = GDN_H * GDN_D
XA_W = XA_H * XA_D
RMS_EPS = 1e-6
L2_EPS = 1e-6
NEG = -1e30
N_DEV = 8
LANE = 128

ADAM_LR, ADAM_B1, ADAM_B2, ADAM_EPS, ADAM_WD, ADAM_STEP = 0.001, 0.9, 0.999, 1e-08, 0.01, 10

VMEM_BIG = 56 * 1024 * 1024


def _cp(sem, vmem=VMEM_BIG):
    return pltpu.CompilerParams(dimension_semantics=sem, vmem_limit_bytes=vmem)


def _div(a, b):
    assert a % b == 0, (a, b)
    return a // b


def _tile(n, t):
    t = min(t, n)
    assert n % t == 0, (n, t)
    return t


def _sigmoid(x):
    return 1.0 / (1.0 + jnp.exp(-x))


def _silu(x):
    return x * _sigmoid(x)


def _softplus(x):
    return jnp.maximum(x, 0.0) + jnp.log1p(jnp.exp(-jnp.abs(x)))


def _dot(a, b, dims, prec=None):
    return lax.dot_general(a, b, (dims, ((), ())), precision=prec, preferred_element_type=F32)


NN = ((1,), (0,))
NT = ((1,), (1,))
TN = ((0,), (0,))
HI = lax.Precision.HIGHEST


def _bdot_plain(a, b, dims):
    return _dot(a.astype(BF16), b.astype(BF16), dims)


@functools.partial(jax.custom_vjp, nondiff_argnums=(2,))
def _bdot_vjp(a, b, dims):
    return _bdot_plain(a, b, dims)


def _bdot_vjp_fwd(a, b, dims):
    return _bdot_plain(a, b, dims), (a, b)


def _bdot_vjp_bwd(dims, res, ct):
    a, b = res
    if dims == NN:
        return _bdot_plain(ct, b, NT), _bdot_plain(a, ct, TN)
    assert dims == NT, dims
    return _bdot_plain(ct, b, NN), _bdot_plain(ct, a, TN)


_bdot_vjp.defvjp(_bdot_vjp_fwd, _bdot_vjp_bwd)


def _neumann(xs):
    pws, nns = list(xs), list(xs)
    for _ in range(5):
        pws = [_bdot_plain(p, p, NN) for p in pws]
        nns = [n + p + _bdot_plain(n, p, NN) for n, p in zip(nns, pws)]
    return tuple(nns)


@jax.custom_vjp
def _neumann_vjp(xs):
    return _neumann(xs)


def _neumann_vjp_fwd(xs):
    nns = _neumann(xs)
    return nns, nns


def _neumann_vjp_bwd(nns, cts):
    ts = [ct + _bdot_plain(nn, ct, TN) for nn, ct in zip(nns, cts)]
    return (tuple(t + _bdot_plain(t, nn, NT) for t, nn in zip(ts, nns)),)


_neumann_vjp.defvjp(_neumann_vjp_fwd, _neumann_vjp_bwd)


class Layout:
    def __init__(self, d):
        self.d = d
        self.g = 0
        self.q = 3 * d
        self.qkv = self.q + Q_W
        self.z = self.qkv + 3 * GDN_W
        self.qc = self.z + GDN_W
        self.k = self.qc + XA_W
        self.v = self.k + KV_W
        self.ab = self.v + KV_W
        self.end = self.ab + LANE
        self.pw = -(-self.end // 1024) * 1024
        self.lq, self.lk, self.lv, self.lqkv = 0, Q_W, Q_W + KV_W, Q_W + 2 * KV_W
        self.la = self.lqkv + 3 * GDN_W
        self.lz = self.la + 2 * GDN_H
        self.lqc = self.lz + GDN_W
        self.lg = self.lqc + XA_W
        self.lw = self.lg + 3 * d

    def pieces(self):
        segs = [(self.lq, self.lk, self.q), (self.lk, self.lv, self.k), (self.lv, self.lqkv, self.v),
                (self.lqkv, self.la, self.qkv), (self.la, self.lz, self.ab), (self.lz, self.lqc, self.z),
                (self.lqc, self.lg, self.qc), (self.lg, self.lw, self.g)]
        cw = _div(self.lw, N_DEV)
        out = []
        for dev in range(N_DEV):
            lo, hi = dev * cw, (dev + 1) * cw
            for ls, le, ps in segs:
                s, e = max(lo, ls), min(hi, le)
                if s < e:
                    out.append((dev, s - lo, ps + s - ls, e - s))
        return out


def pad_w_in(g, lay):
    nd, k, cw = g.shape
    tr = _tile(k, 256)
    tail = lay.ab + 2 * GDN_H

    def body(g_ref, o_ref):
        o_ref[:, pl.ds(tail, lay.pw - tail)] = jnp.zeros((tr, lay.pw - tail), o_ref.dtype)
        for dev, so, po, ln in lay.pieces():
            o_ref[:, pl.ds(po, ln)] = g_ref[dev, :, pl.ds(so, ln)]

    return pl.pallas_call(
        body, name="pad_w_in", grid=(k // tr,), in_specs=[pl.BlockSpec((nd, tr, cw), lambda i: (0, i, 0))],
        out_specs=pl.BlockSpec((tr, lay.pw), lambda i: (i, 0)), out_shape=SDS((k, lay.pw), g.dtype),
        compiler_params=_cp(("parallel",)))(g)


def unpad_dw_in(dw, lay):
    k = dw.shape[0]
    cw = _div(lay.lw, N_DEV)
    tr = _tile(k, 256)

    def body(d_ref, o_ref):
        for dev, so, po, ln in lay.pieces():
            o_ref[dev, :, pl.ds(so, ln)] = d_ref[:, pl.ds(po, ln)]

    return pl.pallas_call(
        body, name="unpad_dw_in", grid=(k // tr,), in_specs=[pl.BlockSpec((tr, lay.pw), lambda i: (i, 0))],
        out_specs=pl.BlockSpec((N_DEV, tr, cw), lambda i: (0, i, 0)), out_shape=SDS((N_DEV, k, cw), dw.dtype),
        compiler_params=_cp(("parallel",)))(dw)


def _position():
    return lax.axis_index("x"), lax.axis_index("y"), lax.axis_index("c")


class GatherJob:
    def __init__(self, arrs):
        self.ins = list(arrs)
        n = len(arrs)
        self.out_shapes = [SDS((N_DEV,) + a.shape, a.dtype) for a in arrs]
        self.scratch = [pltpu.SemaphoreType.DMA((n, 7)), pltpu.SemaphoreType.DMA((n, 7)), pltpu.SemaphoreType.DMA((n,))]

    def _ctx(self, outs, sems):
        send_sems, recv_sems, _ = sems
        x, y, c = _position()

        def blk(o, p):
            return o.at[4 * p[0] + 2 * p[1] + p[2]]

        def copy(i, k, block, to, src=None):
            return pltpu.make_async_remote_copy(
                src_ref=blk(outs[i], block) if src is None else src, dst_ref=blk(outs[i], block),
                send_sem=send_sems.at[i, k], recv_sem=recv_sems.at[i, k], device_id=to, device_id_type=MESH)

        return (x, y, c), (x, y, 1 - c), [(1 - x, y), (x, 1 - y), (1 - x, 1 - y)], blk, copy

    def start(self, ins, outs, sems):
        me, sibling, chips, blk, copy = self._ctx(outs, sems)
        for i in range(len(ins)):
            pltpu.make_async_copy(ins[i], blk(outs[i], me), sems[2].at[i]).start()
            copy(i, 0, me, sibling, src=ins[i]).start()
            for j, chip in enumerate(chips):
                copy(i, 1 + j, me, (*chip, me[2]), src=ins[i]).start()

    def mid(self, ins, outs, sems):
        me, sibling, chips, blk, copy = self._ctx(outs, sems)
        for i in range(len(ins)):
            for j, chip in enumerate(chips):
                copy(i, 1 + j, (*chip, me[2]), me).wait_recv()
                copy(i, 4 + j, (*chip, me[2]), sibling).start()

    def finish(self, ins, outs, sems):
        me, sibling, chips, blk, copy = self._ctx(outs, sems)
        for i in range(len(ins)):
            copy(i, 0, sibling, me).wait_recv()
            for j, chip in enumerate(chips):
                copy(i, 4 + j, (*chip, 1 - me[2]), me).wait_recv()
        for i in range(len(ins)):
            pltpu.make_async_copy(ins[i], blk(outs[i], me), sems[2].at[i]).wait()
            copy(i, 0, me, sibling, src=ins[i]).wait_send()
            for j, chip in enumerate(chips):
                copy(i, 1 + j, me, (*chip, me[2]), src=ins[i]).wait_send()
                copy(i, 4 + j, (*chip, me[2]), sibling).wait_send()


class ChipExchangeJob:
    mid = None

    def __init__(self, arrs):
        self.ins = list(arrs)
        n = len(arrs)
        self.out_shapes = [SDS(a.shape, a.dtype) for a in arrs]
        self.scratch = [pltpu.SemaphoreType.DMA((n, 3)), pltpu.SemaphoreType.DMA((n, 3)), pltpu.SemaphoreType.DMA((n,))]

    def _copies(self, ins, outs, sems, i, arrivals):
        send_sems, recv_sems, local_sems = sems
        x, y, c = _position()
        my_chip = 2 * x + y
        chips = [(1 - x, y), (x, 1 - y), (1 - x, 1 - y)]
        if arrivals:
            return [pltpu.make_async_remote_copy(
                src_ref=ins[i].at[my_chip], dst_ref=outs[i].at[2 * px + py], send_sem=send_sems.at[i, k],
                recv_sem=recv_sems.at[i, k], device_id=(px, py, c), device_id_type=MESH) for k, (px, py) in enumerate(chips)]
        local = pltpu.make_async_copy(ins[i].at[my_chip], outs[i].at[my_chip], local_sems.at[i])
        return local, [pltpu.make_async_remote_copy(
            src_ref=ins[i].at[2 * px + py], dst_ref=outs[i].at[my_chip], send_sem=send_sems.at[i, k],
            recv_sem=recv_sems.at[i, k], device_id=(px, py, c), device_id_type=MESH) for k, (px, py) in enumerate(chips)]

    def start(self, ins, outs, sems):
        for i in range(len(ins)):
            local, remote = self._copies(ins, outs, sems, i, False)
            local.start()
            for cp in remote:
                cp.start()

    def finish(self, ins, outs, sems):
        for i in range(len(ins)):
            for cp in self._copies(ins, outs, sems, i, True):
                cp.wait_recv()
            local, remote = self._copies(ins, outs, sems, i, False)
            for cp in remote:
                cp.wait_send()
            local.wait()


def _slab_shape(g, cols):
    return (g.shape[0], _div(g.shape[1], N_DEV)) if cols else g.shape[1:]


class PairExchangeJob:
    mid = None

    def __init__(self, grads, cols):
        self.ins, self.cols = list(grads), list(cols)
        n = len(grads)
        self.out_shapes = [SDS((4,) + _slab_shape(g, cl), g.dtype) for g, cl in zip(grads, cols)]
        self.scratch = [pltpu.SemaphoreType.DMA((n, 4)), pltpu.SemaphoreType.DMA((n, 4))]

    def _copies(self, ins, outs, sems):
        send_sems, recv_sems = sems
        x, y, c = _position()

        def part(i, dst):
            if not self.cols[i]:
                return ins[i].at[dst]
            cw = _slab_shape(self.ins[i], True)[1]
            return ins[i].at[:, pl.ds(pl.multiple_of(dst * cw, LANE), cw)]

        return [pltpu.make_async_remote_copy(src_ref=part(i, 2 * j + 1 - c), dst_ref=outs[i].at[j], send_sem=send_sems.at[i, j],
                                             recv_sem=recv_sems.at[i, j], device_id=(x, y, 1 - c), device_id_type=MESH)
                for i in range(len(ins)) for j in range(4)]

    def start(self, ins, outs, sems):
        for cp in self._copies(ins, outs, sems):
            cp.start()

    def finish(self, ins, outs, sems):
        for cp in self._copies(ins, outs, sems):
            cp.wait()


def run_job(job, name):
    n = len(job.ins)

    def body(*refs):
        ins, outs, sems = refs[:n], refs[n:2 * n], refs[2 * n:]
        job.start(ins, outs, sems)
        if job.mid is not None:
            job.mid(ins, outs, sems)
        job.finish(ins, outs, sems)

    return pl.pallas_call(body, name=name, out_shape=job.out_shapes, in_specs=[ANY] * n, out_specs=[ANY] * n,
                          scratch_shapes=job.scratch)(*job.ins)


def pair_add(grad, cols, other, name):
    r, c = _slab_shape(grad, cols)
    tr = _tile(r, 256)
    parity = lax.axis_index("c").astype(jnp.int32).reshape(1)

    def body(par_ref, a_ref, b_ref, o_ref):
        o_ref[...] = (a_ref[...].astype(F32) + b_ref[...].astype(F32)).astype(BF16)

    spec = pl.BlockSpec((None, tr, c), lambda j, i, par: (j, i, 0))
    if cols:
        own = pl.BlockSpec((tr, c), lambda j, i, par: (i, 2 * j + par[0]))
    else:
        own = pl.BlockSpec((None, tr, c), lambda j, i, par: (2 * j + par[0], i, 0))
    return pl.pallas_call(
        body, name=name, out_shape=SDS(other.shape, BF16),
        grid_spec=pltpu.PrefetchScalarGridSpec(num_scalar_prefetch=1, grid=(4, r // tr), in_specs=[own, spec], out_specs=spec),
        compiler_params=_cp(("parallel", "parallel")))(parity, grad, other)


def adamw(parts, w, m, v, name):
    p, r, c = parts.shape
    tr = _tile(r, 128 if c > 1024 else 256)

    def body(p_ref, w_ref, m_ref, v_ref, g_out, d_out, m_out, v_out):
        g = p_ref[0].astype(F32)
        for j in range(1, p):
            g = g + p_ref[j].astype(F32)
        mn = ADAM_B1 * m_ref[...] + (1.0 - ADAM_B1) * g
        vn = ADAM_B2 * v_ref[...] + (1.0 - ADAM_B2) * jnp.square(g)
        m_hat = mn / (1.0 - ADAM_B1 ** ADAM_STEP)
        v_hat = vn / (1.0 - ADAM_B2 ** ADAM_STEP)
        g_out[...] = g
        d_out[...] = -ADAM_LR * (m_hat / (jnp.sqrt(v_hat) + ADAM_EPS) + ADAM_WD * w_ref[...])
        m_out[...] = mn
        v_out[...] = vn

    spec = pl.BlockSpec((tr, c), lambda i: (i, 0))
    return pl.pallas_call(
        body, name=name, grid=(r // tr,),
        in_specs=[pl.BlockSpec((p, tr, c), lambda i: (0, i, 0)), spec, spec, spec],
        out_specs=[spec] * 4, out_shape=[SDS((r, c), F32)] * 4, compiler_params=_cp(("parallel",)))(parts, w, m, v)


def matmul(a, b, *, mode, out_dtype, name, tm=1024, tn=1024, tk=512, a_relu2=False, resid=None, relu2_grad_of=None,
           b_cols=False, relu2_out=False, side=None, tail=None):
    if b_cols:
        nb, brows, bc = b.shape
        bshape = (brows, nb * bc)
    else:
        bshape = b.shape
    head_k = head_n = None
    if mode == "nn":
        (m, k), (k2, n) = a.shape, bshape
    elif mode == "nt":
        (m, k), (n, k2) = a.shape, bshape
        if tail is not None:
            head_k, k = k, k + tail.shape[1]
    else:
        (k, m), (k2, n) = a.shape, bshape
        if tail is not None:
            head_n, n = n, n + tail.shape[1]
    assert k == k2 and (tail is None or mode != "nn"), (a.shape, b.shape, mode)
    tm, tn, tk = _tile(m, tm), _tile(n, tn), _tile(k, tk)
    if b_cols and mode == "nn":
        tn = _tile(bc, tn)
    if b_cols and mode == "nt":
        tk = _tile(bc, tk)
    nk = k // tk
    ni, nj = m // tm, n // tn
    nk_head = _div(head_k, tk) if head_k is not None else None
    nj_head = _div(head_n, tn) if head_n is not None else None
    use_acc = nk > 1 or tail is not None
    dims = {"nn": NN, "nt": NT, "tn": TN}[mode]
    extras = [e for e in (resid, relu2_grad_of) if e is not None]
    tails = [tail] if tail is not None else []
    n_side = len(side.ins) if side is not None else 0
    n_main = 2 if relu2_out else 1

    def body(*refs):
        a_ref, b_ref = refs[:2]
        t_ref = refs[2] if tails else None
        n_op = 2 + len(tails)
        e_refs = refs[n_op:n_op + len(extras)]
        n_in = n_op + len(extras) + n_side
        o_ref = refs[n_in]
        act_ref = refs[n_in + 1] if relu2_out else None
        acc_ref = refs[n_in + n_main + n_side] if use_acc else None
        if side is not None:
            s_ins = refs[n_op + len(extras):n_in]
            s_outs = refs[n_in + n_main:n_in + n_main + n_side]
            s_sems = refs[len(refs) - len(side.scratch):]
            step = (pl.program_id(0) * nj + pl.program_id(1)) * nk + pl.program_id(2)
            pl.when(step == 0)(lambda: side.start(s_ins, s_outs, s_sems))
            if side.mid is not None:
                pl.when(step == (ni * nj * nk * 85) // 100)(lambda: side.mid(s_ins, s_outs, s_sems))

        def operands(a_from=a_ref, b_from=b_ref):
            av = a_from[...]
            if a_relu2:
                av = jnp.square(jnp.maximum(av.astype(F32), 0.0))
            return av.astype(BF16), b_from[...].astype(BF16)

        def finish(r):
            e = list(e_refs)
            if resid is not None:
                r = r + e.pop(0)[...]
            if relu2_grad_of is not None:
                r = r * (2.0 * jnp.maximum(e.pop(0)[...], 0.0))
            o_ref[...] = r.astype(out_dtype)
            if relu2_out:
                act_ref[...] = jnp.square(jnp.maximum(r, 0.0)).astype(BF16)

        if not use_acc:
            av, bv = operands()
            finish(_dot(av, bv, dims))
        else:
            kk = pl.program_id(2)

            @pl.when(kk == 0)
            def _():
                acc_ref[...] = jnp.zeros_like(acc_ref)

            def accumulate(a_from, b_from):
                av, bv = operands(a_from, b_from)
                acc_ref[...] += _dot(av, bv, dims)

            if not tails:
                accumulate(a_ref, b_ref)
            elif mode == "nt":
                pl.when(kk < nk_head)(lambda: accumulate(a_ref, b_ref))
                pl.when(kk >= nk_head)(lambda: accumulate(t_ref, b_ref))
            else:
                in_head = pl.program_id(1) < nj_head
                pl.when(in_head)(lambda: accumulate(a_ref, b_ref))
                pl.when(jnp.logical_not(in_head))(lambda: accumulate(a_ref, t_ref))

            @pl.when(kk == nk - 1)
            def _():
                finish(acc_ref[...])

        if side is not None:
            pl.when(step == ni * nj * nk - 1)(lambda: side.finish(s_ins, s_outs, s_sems))

    a_spec = {"nn": pl.BlockSpec((tm, tk), lambda i, j, kk: (i, kk)),
              "nt": pl.BlockSpec((tm, tk), lambda i, j, kk: (i, kk)),
              "tn": pl.BlockSpec((tk, tm), lambda i, j, kk: (kk, i))}[mode]
    t_specs = []
    if tails and mode == "nt":
        a_spec = pl.BlockSpec((tm, tk), lambda i, j, kk: (i, jnp.minimum(kk, nk_head - 1)))
        t_specs = [pl.BlockSpec((tm, tk), lambda i, j, kk: (i, jnp.maximum(kk - nk_head, 0)))]
    if tails and mode == "tn":
        t_specs = [pl.BlockSpec((tk, tn), lambda i, j, kk: (kk, jnp.maximum(j - nj_head, 0)))]
    if tails and mode == "tn":
        b_spec = pl.BlockSpec((tk, tn), lambda i, j, kk: (kk, jnp.minimum(j, nj_head - 1)))
    elif not b_cols:
        b_spec = {"nn": pl.BlockSpec((tk, tn), lambda i, j, kk: (kk, j)),
                  "nt": pl.BlockSpec((tn, tk), lambda i, j, kk: (j, kk)),
                  "tn": pl.BlockSpec((tk, tn), lambda i, j, kk: (kk, j))}[mode]
    elif mode == "nn":
        per = bc // tn
        b_spec = pl.BlockSpec((None, tk, tn), lambda i, j, kk: (j // per, kk, j % per))
    else:
        assert mode == "nt", mode
        per = bc // tk
        b_spec = pl.BlockSpec((None, tn, tk), lambda i, j, kk: (kk // per, j, kk % per))
    e_spec = pl.BlockSpec((tm, tn), lambda i, j, kk: (i, j))
    main_shapes = [SDS((m, n), out_dtype)] + ([SDS((m, n), BF16)] if relu2_out else [])
    res = pl.pallas_call(
        body, name=name, grid=(ni, nj, nk),
        in_specs=[a_spec, b_spec] + t_specs + [e_spec] * len(extras) + [ANY] * n_side,
        out_specs=[e_spec] * n_main + [ANY] * n_side, out_shape=main_shapes + (side.out_shapes if side is not None else []),
        scratch_shapes=([pltpu.VMEM((tm, tn), F32)] if use_acc else []) + (side.scratch if side is not None else []),
        compiler_params=_cp(("arbitrary", "arbitrary", "arbitrary")))(a, b, *tails, *extras, *(side.ins if side is not None else []))
    return res if len(res) > 1 else res[0]


def rmsnorm_fwd(x, g, name):
    s, d = x.shape
    tm = _tile(s, 256)

    def body(x_ref, g_ref, o_ref):
        xv = x_ref[...]
        r = lax.rsqrt(jnp.mean(xv * xv, axis=-1, keepdims=True) + RMS_EPS)
        o_ref[...] = (xv * r * g_ref[...]).astype(BF16)

    row = pl.BlockSpec((tm, d), lambda i: (i, 0))
    return pl.pallas_call(body, name=name, grid=(s // tm,), in_specs=[row, pl.BlockSpec((1, d), lambda i: (0, 0))],
                          out_specs=row, out_shape=SDS((s, d), BF16), compiler_params=_cp(("parallel",)))(x, g)


def _rms_bwd_rows(xv, gv, dy):
    r = lax.rsqrt(jnp.mean(xv * xv, axis=-1, keepdims=True) + RMS_EPS)
    xh = xv * r
    dxh = dy * gv
    dx = r * (dxh - xh * jnp.mean(dxh * xh, axis=-1, keepdims=True))
    return dx, jnp.sum(dy * xh, axis=0, keepdims=True)


def rmsnorm_bwd(x, g, dn, resid, name, bf16_copy=False):
    s, d = x.shape
    tm = _tile(s, 256)
    has_r = resid is not None

    def body(*refs):
        x_ref, g_ref, dn_ref = refs[:3]
        dx_ref, dg_ref = refs[3 + has_r:5 + has_r]
        dx, part = _rms_bwd_rows(x_ref[...], g_ref[...], dn_ref[...].astype(F32))
        if has_r:
            dx = dx + refs[3][...]
        dx_ref[...] = dx
        if bf16_copy:
            refs[5 + has_r][...] = dx.astype(BF16)

        @pl.when(pl.program_id(0) == 0)
        def _():
            dg_ref[...] = jnp.zeros_like(dg_ref)

        dg_ref[...] += part

    row = pl.BlockSpec((tm, d), lambda i: (i, 0))
    vec = pl.BlockSpec((1, d), lambda i: (0, 0))
    ins = [x, g, dn] + ([resid] if has_r else [])
    return pl.pallas_call(body, name=name, grid=(s // tm,), in_specs=[row, vec, row] + ([row] if has_r else []),
                          out_specs=[row, vec] + ([row] if bf16_copy else []),
                          out_shape=[SDS((s, d), F32), SDS((1, d), F32)] + ([SDS((s, d), BF16)] if bf16_copy else []),
                          compiler_params=_cp(("arbitrary",)))(*ins)


def final_norm_loss(h, g, tgt, name):
    s, d = h.shape
    tm = _tile(s, 256)

    def body(h_ref, g_ref, t_ref, dh_ref, dg_ref, l_ref, dhb_ref):
        xv, gv = h_ref[...], g_ref[...]
        r = lax.rsqrt(jnp.mean(xv * xv, axis=-1, keepdims=True) + RMS_EPS)
        e = xv * r * gv - t_ref[...]
        lpart = 0.5 * jnp.sum(jnp.mean(e * e, axis=-1, keepdims=True), axis=0, keepdims=True)
        dx, part = _rms_bwd_rows(xv, gv, e * (1.0 / d))
        dh_ref[...] = dx
        dhb_ref[...] = dx.astype(BF16)

        @pl.when(pl.program_id(0) == 0)
        def _():
            dg_ref[...] = jnp.zeros_like(dg_ref)
            l_ref[...] = jnp.zeros_like(l_ref)

        dg_ref[...] += part
        l_ref[...] += jnp.broadcast_to(lpart, l_ref.shape)

    row = pl.BlockSpec((tm, d), lambda i: (i, 0))
    vec = pl.BlockSpec((1, d), lambda i: (0, 0))
    lsp = pl.BlockSpec((1, LANE), lambda i: (0, 0))
    return pl.pallas_call(body, name=name, grid=(s // tm,), in_specs=[row, vec, row], out_specs=[row, vec, lsp, row],
                          out_shape=[SDS((s, d), F32), SDS((1, d), F32), SDS((1, LANE), F32), SDS((s, d), BF16)],
                          compiler_params=_cp(("arbitrary",)))(h, g, tgt)


def merge(p_all, ya, yb, yc, wa, wb, wc, dm, lay, name):
    s, d = ya.shape[0], lay.d
    wcols = wa.shape[2]
    bwd = dm is not None
    tm, tn = _tile(s, 1024 if bwd else 2048), _tile(wcols, 512)
    nj, per = d // tn, wcols // tn

    y_specs = [pl.BlockSpec((tm, y.shape[1]), lambda i, j, *_: (i, 0)) for y in (ya, yb, yc)]
    w_specs = [pl.BlockSpec((None, w.shape[1], tn), lambda i, j, *_: (j // per, 0, j % per)) for w in (wa, wb, wc)]
    o_spec = pl.BlockSpec((tm, tn), lambda i, j, *_: (i, j))
    if not bwd:
        def body(ga, gb, gc, ya_r, yb_r, yc_r, wa_r, wb_r, wc_r, o_ref):
            ts = [_dot(y[...], w[...], NN) for y, w in ((ya_r, wa_r), (yb_r, wb_r), (yc_r, wc_r))]
            gs = [_sigmoid(g[...]) for g in (ga, gb, gc)]
            o_ref[...] = (gs[0] * ts[0] + gs[1] * ts[1] + gs[2] * ts[2]).astype(BF16)

        gate_specs = [pl.BlockSpec((tm, tn), lambda i, j, b=b: (i, b * nj + j)) for b in range(3)]
        return pl.pallas_call(
            body, name=name, grid=(s // tm, nj), in_specs=gate_specs + y_specs + w_specs,
            out_specs=o_spec, out_shape=SDS((s, d), BF16),
            compiler_params=_cp(("parallel", "parallel")))(p_all, p_all, p_all, ya, yb, yc, wa, wb, wc)

    def body_bwd(g_r, ya_r, yb_r, yc_r, wa_r, wb_r, wc_r, dm_r, dg_o, dta_o, dtb_o, dtc_o):
        for k, (y, w, dt_o) in enumerate(((ya_r, wa_r, dta_o), (yb_r, wb_r, dtb_o), (yc_r, wc_r, dtc_o))):
            @pl.when(pl.program_id(2) == k)
            def _(y=y, w=w, dt_o=dt_o):
                t = _dot(y[...], w[...], NN)
                g = _sigmoid(g_r[...])
                dmv = dm_r[...]
                dg_o[...] = (dmv * t * (g * (1.0 - g))).astype(BF16)
                dt_o[...] = (dmv * g).astype(BF16)

    gate_spec = pl.BlockSpec((tm, tn), lambda i, j, b: (i, b * nj + j))
    return pl.pallas_call(
        body_bwd, name=name, grid=(s // tm, nj, 3), in_specs=[gate_spec] + y_specs + w_specs + [o_spec],
        out_specs=[gate_spec, o_spec, o_spec, o_spec], out_shape=[SDS((s, 3 * d), BF16)] + [SDS((s, d), BF16)] * 3,
        compiler_params=_cp(("arbitrary", "arbitrary", "arbitrary")))(p_all, ya, yb, yc, wa, wb, wc, dm)


SWA_PAIRS = SWA_G // 2


def _swa_probs(q, kc, sink, first):
    s = _dot(q, kc, NT) * (SWA_HD ** -0.5)
    qi = lax.broadcasted_iota(jnp.int32, s.shape, 0) % SWA_W
    kj = lax.broadcasted_iota(jnp.int32, s.shape, 1)
    mask = (kj > qi) & (kj <= qi + SWA_W) & ((kj >= SWA_W) | jnp.logical_not(first))
    s = jnp.where(mask, s, NEG)
    m = jnp.maximum(jnp.max(s, axis=-1, keepdims=True), sink)
    p = jnp.exp(s - m)
    es = jnp.exp(sink - m)
    denom = jnp.sum(p, axis=-1, keepdims=True) + es
    return p / denom, es / denom


def _swa_stack(ref, h):
    return jnp.concatenate([ref[:, pl.ds((h * SWA_PAIRS + p) * LANE, LANE)] for p in range(SWA_PAIRS)], axis=0)


def _swa_unstack(ref, h, val):
    for p in range(SWA_PAIRS):
        ref[:, pl.ds((h * SWA_PAIRS + p) * LANE, LANE)] = val[p * SWA_W:(p + 1) * SWA_W]


def _swa_sink_col(sk_ref, h, second):
    pair = lax.broadcasted_iota(jnp.int32, (SWA_PAIRS * SWA_W, 1), 0) // SWA_W
    col = jnp.zeros((SWA_PAIRS * SWA_W, 1), F32)
    for p in range(SWA_PAIRS):
        hh = h * SWA_G + 2 * p + second
        col = jnp.where(pair == p, sk_ref[0:1, hh:hh + 1], col)
    return col


def _swa_kv_tiles(cur_ref, prev_ref, h):
    t = jnp.concatenate([prev_ref[...], cur_ref[...]], axis=0)
    lane = lax.broadcasted_iota(jnp.int32, t.shape, 1)
    moved = pltpu.roll(t, SWA_HD, axis=1)
    low, high = (t, moved) if h == 0 else (moved, t)
    return jnp.where(lane < SWA_HD, low, 0.0).astype(BF16), jnp.where(lane >= SWA_HD, high, 0.0).astype(BF16)


def _swa_kv_grad(g_low, g_high, h):
    lane = lax.broadcasted_iota(jnp.int32, g_low.shape, 1)
    if h == 0:
        return jnp.where(lane < SWA_HD, g_low + pltpu.roll(g_high, SWA_HD, axis=1), 0.0)
    return jnp.where(lane >= SWA_HD, pltpu.roll(g_low, SWA_HD, axis=1) + g_high, 0.0)


def _swa_specs(lay):
    w = SWA_W
    q_spec = pl.BlockSpec((w, Q_W), lambda n: (n, _div(lay.q, Q_W)))
    cur = lambda off: pl.BlockSpec((w, KV_W), lambda n: (n, _div(off, KV_W)))
    prev = lambda off: pl.BlockSpec((w, KV_W), lambda n: (jnp.maximum(n - 1, 0), _div(off, KV_W)))
    return q_spec, cur(lay.k), prev(lay.k), cur(lay.v), prev(lay.v)


def swa_fwd(p_all, sinks, lay, name):
    s = p_all.shape[0]
    nb = _div(s, SWA_W)

    def body(q_ref, kc_ref, kp_ref, vc_ref, vp_ref, sk_ref, o_ref):
        first = pl.program_id(0) == 0
        for h in range(SWA_HKV):
            ks = _swa_kv_tiles(kc_ref, kp_ref, h)
            vs = _swa_kv_tiles(vc_ref, vp_ref, h)
            q = _swa_stack(q_ref, h).astype(BF16)
            ps = [_swa_probs(q, ks[e], _swa_sink_col(sk_ref, h, e), first)[0] for e in range(2)]
            o = _dot(ps[0].astype(BF16), vs[0], NN) + _dot(ps[1].astype(BF16), vs[1], NN)
            _swa_unstack(o_ref, h, o.astype(BF16))

    q_spec, kc_s, kp_s, vc_s, vp_s = _swa_specs(lay)
    return pl.pallas_call(
        body, name=name, grid=(nb,),
        in_specs=[q_spec, kc_s, kp_s, vc_s, vp_s, pl.BlockSpec(sinks.shape, lambda n: (0, 0))],
        out_specs=pl.BlockSpec((SWA_W, Q_W), lambda n: (n, 0)), out_shape=SDS((s, Q_W), BF16),
        compiler_params=_cp(("parallel",)))(p_all, p_all, p_all, p_all, p_all, sinks)


def swa_bwd(p_all, sinks, dy, lay, name):
    s = p_all.shape[0]
    nb = _div(s, SWA_W)
    w = SWA_W

    def body(q_ref, kc_ref, kp_ref, vc_ref, vp_ref, sk_ref, do_ref, dq_ref, dk_ref, dv_ref, ds_ref, kcar, vcar):
        n = pl.program_id(0)
        first = n == 0

        @pl.when(first)
        def _():
            kcar[...] = jnp.zeros_like(kcar)
            vcar[...] = jnp.zeros_like(vcar)
            ds_ref[...] = jnp.zeros_like(ds_ref)

        @pl.when(n < nb)
        def _():
            lane = lax.broadcasted_iota(jnp.int32, (1, LANE), 1)
            dsink = jnp.zeros((1, LANE), F32)
            dk_tile = jnp.zeros((2 * w, KV_W), F32)
            dv_tile = jnp.zeros((2 * w, KV_W), F32)
            for h in range(SWA_HKV):
                ks = _swa_kv_tiles(kc_ref, kp_ref, h)
                vs = _swa_kv_tiles(vc_ref, vp_ref, h)
                q = _swa_stack(q_ref, h).astype(BF16)
                do = _swa_stack(do_ref, h).astype(BF16)
                dq = jnp.zeros((SWA_PAIRS * w, LANE), F32)
                dks, dvs = [], []
                for e in range(2):
                    p, psink = _swa_probs(q, ks[e], _swa_sink_col(sk_ref, h, e), first)
                    dp = _dot(do, vs[e], NT)
                    dvs.append(_dot(p.astype(BF16), do, TN))
                    rs = jnp.sum(dp * p, axis=-1, keepdims=True)
                    dsb = (p * (dp - rs) * (SWA_HD ** -0.5)).astype(BF16)
                    dq = dq + _dot(dsb, ks[e], NN)
                    dks.append(_dot(dsb, q, TN))
                    psr = psink * rs
                    for pr in range(SWA_PAIRS):
                        hh = h * SWA_G + 2 * pr + e
                        dsink = dsink + jnp.where(lane == hh, -jnp.sum(psr[pr * w:(pr + 1) * w], axis=0, keepdims=True), 0.0)
                _swa_unstack(dq_ref, h, dq.astype(BF16))
                dk_tile = dk_tile + _swa_kv_grad(dks[0], dks[1], h)
                dv_tile = dv_tile + _swa_kv_grad(dvs[0], dvs[1], h)
            dk_ref[...] = (kcar[...] + dk_tile[:w]).astype(BF16)
            dv_ref[...] = (vcar[...] + dv_tile[:w]).astype(BF16)
            kcar[...] = dk_tile[w:]
            vcar[...] = dv_tile[w:]
            ds_ref[...] += dsink

        @pl.when(n == nb)
        def _():
            dk_ref[...] = kcar[...].astype(BF16)
            dv_ref[...] = vcar[...].astype(BF16)

    last = nb - 1
    q_spec = pl.BlockSpec((w, Q_W), lambda n: (jnp.minimum(n, last), _div(lay.q, Q_W)))
    cur = lambda off: pl.BlockSpec((w, KV_W), lambda n: (jnp.minimum(n, last), _div(off, KV_W)))
    prev = lambda off: pl.BlockSpec((w, KV_W), lambda n: (jnp.clip(n - 1, 0, last), _div(off, KV_W)))
    row = pl.BlockSpec((w, Q_W), lambda n: (jnp.minimum(n, last), 0))
    kv_out = pl.BlockSpec((w, KV_W), lambda n: (jnp.maximum(n - 1, 0), 0))
    return pl.pallas_call(
        body, name=name, grid=(nb + 1,),
        in_specs=[q_spec, cur(lay.k), prev(lay.k), cur(lay.v), prev(lay.v), pl.BlockSpec(sinks.shape, lambda n: (0, 0)), row],
        out_specs=[row, kv_out, kv_out, pl.BlockSpec((1, LANE), lambda n: (0, 0))],
        out_shape=[SDS((s, Q_W), BF16), SDS((s, KV_W), BF16), SDS((s, KV_W), BF16), SDS((1, LANE), F32)],
        scratch_shapes=[pltpu.VMEM((w, KV_W), F32), pltpu.VMEM((w, KV_W), F32)],
        compiler_params=_cp(("arbitrary",)))(p_all, p_all, p_all, p_all, p_all, sinks, dy)


def _xa_probs(q, mk):
    s = _dot(q, mk, NT) * (XA_D ** -0.5)
    p = jnp.exp(s - jnp.max(s, axis=-1, keepdims=True))
    return p / jnp.sum(p, axis=-1, keepdims=True)


def xattn_fwd(p_all, mkv, lay, name):
    s, nm = p_all.shape[0], mkv.shape[0]
    tm = _tile(s, 512)

    def body(q_ref, mkv_ref, o_ref):
        for h in range(XA_H):
            cols = pl.ds(h * XA_D, XA_D)
            p = _xa_probs(q_ref[:, cols].astype(BF16), mkv_ref[:, cols])
            o_ref[:, cols] = _dot(p.astype(BF16), mkv_ref[:, pl.ds(XA_W + h * XA_D, XA_D)], NN).astype(BF16)

    return pl.pallas_call(
        body, name=name, grid=(s // tm,),
        in_specs=[pl.BlockSpec((tm, XA_W), lambda i: (i, _div(lay.qc, XA_W))), pl.BlockSpec((nm, 2 * XA_W), lambda i: (0, 0))],
        out_specs=pl.BlockSpec((tm, XA_W), lambda i: (i, 0)), out_shape=SDS((s, XA_W), BF16),
        compiler_params=_cp(("parallel",)))(p_all, mkv)


def xattn_bwd(p_all, mkv, dy, lay, name):
    s, nm = p_all.shape[0], mkv.shape[0]
    tm = _tile(s, 512)

    def body(q_ref, mkv_ref, do_ref, dq_ref, dmkv_ref):
        @pl.when(pl.program_id(0) == 0)
        def _():
            dmkv_ref[...] = jnp.zeros_like(dmkv_ref)

        for h in range(XA_H):
            cols = pl.ds(h * XA_D, XA_D)
            vcols = pl.ds(XA_W + h * XA_D, XA_D)
            q = q_ref[:, cols].astype(BF16)
            do = do_ref[:, cols].astype(BF16)
            p = _xa_probs(q, mkv_ref[:, cols])
            dp = _dot(do, mkv_ref[:, vcols], NT)
            dmkv_ref[:, vcols] += _dot(p.astype(BF16), do, TN)
            dsb = (p * (dp - jnp.sum(dp * p, axis=-1, keepdims=True)) * (XA_D ** -0.5)).astype(BF16)
            dq_ref[:, cols] = _dot(dsb, mkv_ref[:, cols], NN).astype(BF16)
            dmkv_ref[:, cols] += _dot(dsb, q, TN)

    row = pl.BlockSpec((tm, XA_W), lambda i: (i, 0))
    full = pl.BlockSpec((nm, 2 * XA_W), lambda i: (0, 0))
    return pl.pallas_call(
        body, name=name, grid=(s // tm,),
        in_specs=[pl.BlockSpec((tm, XA_W), lambda i: (i, _div(lay.qc, XA_W))), full, row],
        out_specs=[row, full], out_shape=[SDS((s, XA_W), BF16), SDS((nm, 2 * XA_W), F32)],
        compiler_params=_cp(("arbitrary",)))(p_all, mkv, dy)


def _shift_down(cur, prev8, s):
    cat = jnp.concatenate([prev8, cur[0:8]], axis=0)
    return pltpu.roll(cur, s, axis=0), pltpu.roll(cat, s, axis=0)[8:16]


def _shift_up(cur, next8, s):
    tm = cur.shape[0]
    cat = jnp.concatenate([cur[tm - 8:tm], next8], axis=0)
    return pltpu.roll(cur, tm - s, axis=0), pltpu.roll(cat, 16 - s, axis=0)[0:8]


def gdn_conv_fwd(p_all, conv_w, lay, name):
    s = p_all.shape[0]
    tm = _tile(s, 512)
    c0 = _div(lay.qkv, GDN_W)

    def body(x_ref, prev_ref, w_ref, o_ref):
        cur = x_ref[...]
        prev8 = jnp.where(pl.program_id(1) > 0, prev_ref[...], 0.0)
        main = w_ref[GDN_CONV - 1:GDN_CONV, :] * cur
        top = w_ref[GDN_CONV - 1:GDN_CONV, :] * cur[0:8]
        for sft in range(1, GDN_CONV):
            wi = w_ref[GDN_CONV - 1 - sft:GDN_CONV - sft, :]
            a, b = _shift_down(cur, prev8, sft)
            main = main + wi * a
            top = top + wi * b
        o_ref[...] = main
        o_ref[0:8, :] = top

    return pl.pallas_call(
        body, name=name, grid=(3, s // tm),
        in_specs=[pl.BlockSpec((tm, GDN_W), lambda c, i: (i, c0 + c)),
                  pl.BlockSpec((8, GDN_W), lambda c, i: (jnp.maximum(i * (tm // 8) - 1, 0), c0 + c)),
                  pl.BlockSpec((GDN_CONV, GDN_W), lambda c, i: (0, c))],
        out_specs=pl.BlockSpec((tm, GDN_W), lambda c, i: (i, c)), out_shape=SDS((s, 3 * GDN_W), F32),
        compiler_params=_cp(("parallel", "parallel")))(p_all, p_all, conv_w)


def gdn_conv_bwd(p_all, conv_w, dxc, lay, name):
    s = p_all.shape[0]
    tm = _tile(s, 512)
    c0 = _div(lay.qkv, GDN_W)
    nt = s // tm

    def body(x_ref, prev_ref, d_ref, next_ref, w_ref, dx_ref, dw_ref):
        i = pl.program_id(1)
        cur, d = x_ref[...], d_ref[...]
        prev8 = jnp.where(i > 0, prev_ref[...], 0.0)
        next8 = jnp.where(i < nt - 1, next_ref[...], 0.0)
        row = lax.broadcasted_iota(jnp.int32, (tm, 1), 0)
        main = w_ref[GDN_CONV - 1:GDN_CONV, :] * d
        bot = w_ref[GDN_CONV - 1:GDN_CONV, :] * d[tm - 8:tm]
        dws = [jnp.sum(d * cur, axis=0, keepdims=True)]
        for sft in range(1, GDN_CONV):
            wi = w_ref[GDN_CONV - 1 - sft:GDN_CONV - sft, :]
            a, b = _shift_up(d, next8, sft)
            main = main + wi * a
            bot = bot + wi * b
            xa, xb = _shift_down(cur, prev8, sft)
            dws.append(jnp.sum(jnp.where(row >= 8, d * xa, 0.0), axis=0, keepdims=True)
                       + jnp.sum(d[0:8] * xb, axis=0, keepdims=True))
        dx_ref[...] = main.astype(BF16)
        dx_ref[tm - 8:tm, :] = bot.astype(BF16)

        @pl.when(i == 0)
        def _():
            dw_ref[...] = jnp.zeros_like(dw_ref)

        for sft in range(GDN_CONV):
            dw_ref[GDN_CONV - 1 - sft:GDN_CONV - sft, :] += dws[sft]

    return pl.pallas_call(
        body, name=name, grid=(3, nt),
        in_specs=[pl.BlockSpec((tm, GDN_W), lambda c, i: (i, c0 + c)),
                  pl.BlockSpec((8, GDN_W), lambda c, i: (jnp.maximum(i * (tm // 8) - 1, 0), c0 + c)),
                  pl.BlockSpec((tm, GDN_W), lambda c, i: (i, c)),
                  pl.BlockSpec((8, GDN_W), lambda c, i: (jnp.minimum((i + 1) * (tm // 8), s // 8 - 1), c)),
                  pl.BlockSpec((GDN_CONV, GDN_W), lambda c, i: (0, c))],
        out_specs=[pl.BlockSpec((tm, GDN_W), lambda c, i: (i, c)), pl.BlockSpec((GDN_CONV, GDN_W), lambda c, i: (0, c))],
        out_shape=[SDS((s, 3 * GDN_W), BF16), SDS((GDN_CONV, 3 * GDN_W), F32)],
        compiler_params=_cp(("parallel", "arbitrary")))(p_all, p_all, dxc, dxc, conv_w)


def _gdn_chunk(xq, xk, xv, ab, gp, bdot=_bdot_plain):
    c = GDN_C
    nc = xq.shape[0] // c
    lane = lax.broadcasted_iota(jnp.int32, (c, LANE), 1)
    row = lax.broadcasted_iota(jnp.int32, (c, c), 0)
    col = lax.broadcasted_iota(jnp.int32, (c, c), 1)
    g_tile = -jnp.exp(gp[0:1, :]) * _softplus(ab + gp[1:2, :])
    b_tile = _sigmoid(ab)
    tri = (row >= col).astype(F32)
    qa, ka, va = _silu(xq), _silu(xk), _silu(xv)
    items = []
    for ci in range(nc):
        rs = slice(ci * c, (ci + 1) * c)
        gcum = _dot(tri, g_tile[rs], NN, HI)
        gcum_t = gcum.T
        for h in range(GDN_H):
            hs = slice(h * GDN_D, (h + 1) * GDN_D)
            q, k, v = qa[rs, hs], ka[rs, hs], va[rs, hs]
            q = q * lax.rsqrt(jnp.sum(q * q, axis=-1, keepdims=True) + L2_EPS) * (GDN_D ** -0.5)
            k = k * lax.rsqrt(jnp.sum(k * k, axis=-1, keepdims=True) + L2_EPS)
            gc = jnp.sum(jnp.where(lane == h, gcum, 0.0), axis=1, keepdims=True)
            beta = jnp.sum(jnp.where(lane == GDN_H + h, b_tile[rs], 0.0), axis=1, keepdims=True)
            decay = jnp.exp(jnp.where(row >= col, gc - gcum_t[h:h + 1, :], NEG))
            items.append((q, k, v, gc, beta, decay))
    kks = [bdot(k, k, NT) for (_, k, _, _, _, _) in items]
    xs = tuple(-jnp.where(row > col, it[4] * kk * it[5], 0.0) for it, kk in zip(items, kks))
    nns = _neumann(xs) if bdot is _bdot_plain else _neumann_vjp(xs)
    qks = [bdot(q, k, NT) for (q, k, _, _, _, _) in items]
    out = []
    for (q, k, v, gc, beta, decay), n, qk in zip(items, nns, qks):
        eg = jnp.exp(gc)
        vb = v * beta
        kbe = k * (beta * eg)
        gl = gc[c - 1:c, :]
        out.append((vb + bdot(n, vb, NN), kbe + bdot(n, kbe, NN), q * eg, k * jnp.exp(gl - gc), qk * decay, jnp.exp(gl)))
    return [out[ci * GDN_H:(ci + 1) * GDN_H] for ci in range(nc)]


GDN_CPS = 4


def _gdn_pre_specs(lay, t):
    xspec = lambda j: pl.BlockSpec((t, GDN_W), lambda n, j=j: (n, j))
    return [xspec(0), xspec(1), xspec(2), pl.BlockSpec((t, LANE), lambda n: (n, _div(lay.ab, LANE))),
            pl.BlockSpec((8, LANE), lambda n: (0, 0))]


def gdn_pre_fwd(xc, p_all, gp, lay, name):
    s = xc.shape[0]
    c = GDN_C
    n = _div(s, c)
    cps = _tile(n, GDN_CPS)
    t = cps * c

    def body(xq, xk, xv, ab, gp_ref, u_ref, w_ref, qd_ref, kd_ref, qk_ref, gl_ref):
        lane = lax.broadcasted_iota(jnp.int32, (1, LANE), 1)
        chunks = _gdn_chunk(xq[...], xk[...], xv[...], ab[...], gp_ref[...])
        for ci, heads in enumerate(chunks):
            rs = pl.ds(ci * c, c)
            gl_row = jnp.zeros((1, LANE), F32)
            for h, (u, w, qd, kd, qk, gl) in enumerate(heads):
                hs = pl.ds(h * GDN_D, GDN_D)
                u_ref[rs, hs] = u
                w_ref[rs, hs] = w.astype(BF16)
                qd_ref[rs, hs] = qd.astype(BF16)
                kd_ref[rs, hs] = kd.astype(BF16)
                qk_ref[rs, pl.ds(h * c, c)] = qk.astype(BF16)
                gl_row = gl_row + jnp.where(lane == h, gl, 0.0)
            gl_ref[ci] = gl_row

    row = pl.BlockSpec((t, GDN_W), lambda n: (n, 0))
    return pl.pallas_call(
        body, name=name, grid=(n // cps,), in_specs=_gdn_pre_specs(lay, t),
        out_specs=[row, row, row, row, pl.BlockSpec((t, GDN_H * c), lambda n: (n, 0)), pl.BlockSpec((cps, 1, LANE), lambda n: (n, 0, 0))],
        out_shape=[SDS((s, GDN_W), F32), SDS((s, GDN_W), BF16), SDS((s, GDN_W), BF16), SDS((s, GDN_W), BF16),
                   SDS((s, GDN_H * c), BF16), SDS((n, 1, LANE), F32)],
        compiler_params=_cp(("parallel",)))(xc, xc, xc, p_all, gp)


def gdn_pre_bwd(xc, p_all, gp, du, dw, dqd, dkd, dqk, dgl, lay, name):
    s = xc.shape[0]
    c = GDN_C
    n = _div(s, c)
    cps = _tile(n, GDN_CPS)
    t = cps * c
    chunk = functools.partial(_gdn_chunk, bdot=_bdot_vjp)

    def body(xq, xk, xv, ab, gp_ref, du_r, dw_r, dqd_r, dkd_r, dqk_r, dgl_r, dxc_ref, dab_ref, dgp_ref):
        lane = lax.broadcasted_iota(jnp.int32, (1, LANE), 1)
        _, vjp = jax.vjp(chunk, xq[...], xk[...], xv[...], ab[...], gp_ref[...])
        cts = []
        for ci in range(cps):
            rs = pl.ds(ci * c, c)
            heads = []
            for h in range(GDN_H):
                hs = pl.ds(h * GDN_D, GDN_D)
                dgl_h = jnp.sum(jnp.where(lane == h, dgl_r[ci], 0.0), axis=1, keepdims=True)
                heads.append((du_r[rs, hs], dw_r[rs, hs], dqd_r[rs, hs], dkd_r[rs, hs], dqk_r[rs, pl.ds(h * c, c)], dgl_h))
            cts.append(heads)
        dq, dk, dv, dab, dgp = vjp(cts)
        dxc_ref[:, pl.ds(0, GDN_W)] = dq
        dxc_ref[:, pl.ds(GDN_W, GDN_W)] = dk
        dxc_ref[:, pl.ds(2 * GDN_W, GDN_W)] = dv
        dab_ref[...] = dab.astype(BF16)

        @pl.when(pl.program_id(0) == 0)
        def _():
            dgp_ref[...] = jnp.zeros_like(dgp_ref)

        dgp_ref[...] += dgp

    row = pl.BlockSpec((t, GDN_W), lambda n: (n, 0))
    return pl.pallas_call(
        body, name=name, grid=(n // cps,),
        in_specs=_gdn_pre_specs(lay, t) + [row, row, row, row, pl.BlockSpec((t, GDN_H * c), lambda n: (n, 0)),
                                           pl.BlockSpec((cps, 1, LANE), lambda n: (n, 0, 0))],
        out_specs=[pl.BlockSpec((t, 3 * GDN_W), lambda n: (n, 0)), pl.BlockSpec((t, LANE), lambda n: (n, 0)),
                   pl.BlockSpec((8, LANE), lambda n: (0, 0))],
        out_shape=[SDS((s, 3 * GDN_W), F32), SDS((s, LANE), BF16), SDS((8, LANE), F32)],
        compiler_params=_cp(("arbitrary",)))(xc, xc, xc, p_all, gp, du, dw, dqd, dkd, dqk, dgl)


def _lane_scalar(row, h):
    lane = lax.broadcasted_iota(jnp.int32, row.shape, 1)
    return jnp.sum(jnp.where(lane == h, row, 0.0), axis=1, keepdims=True)


def gdn_scan_fwd(u, w, qd, kd, qk, gl, name):
    s = u.shape[0]
    c = GDN_C
    n = _div(s, c)
    cps = _tile(n, GDN_CPS)
    t = cps * c

    def body(u_r, w_r, qd_r, kd_r, qk_r, gl_r, o_ref, s_ref, st):
        @pl.when(pl.program_id(0) == 0)
        def _():
            st[...] = jnp.zeros_like(st)

        heads = range(GDN_H)
        hs = [pl.ds(h * GDN_D, GDN_D) for h in heads]
        for ci in range(cps):
            rs = pl.ds(ci * c, c)
            s_ref[ci] = st[...]
            sh = [st[hs[h], :] for h in heads]
            shb = [x.astype(BF16) for x in sh]
            ws = [_dot(w_r[rs, hs[h]], shb[h], NN) for h in heads]
            qs = [_dot(qd_r[rs, hs[h]], shb[h], NN) for h in heads]
            vb = [(u_r[rs, hs[h]] - ws[h]).astype(BF16) for h in heads]
            ov = [_dot(qk_r[rs, pl.ds(h * c, c)], vb[h], NN) for h in heads]
            kv = [_dot(kd_r[rs, hs[h]], vb[h], TN) for h in heads]
            for h in heads:
                o_ref[rs, hs[h]] = qs[h] + ov[h]
                st[hs[h], :] = sh[h] * _lane_scalar(gl_r[ci], h) + kv[h]

    row = pl.BlockSpec((t, GDN_W), lambda i: (i, 0))
    return pl.pallas_call(
        body, name=name, grid=(n // cps,),
        in_specs=[row, row, row, row, pl.BlockSpec((t, GDN_H * c), lambda i: (i, 0)), pl.BlockSpec((cps, 1, LANE), lambda i: (i, 0, 0))],
        out_specs=[row, pl.BlockSpec((cps, GDN_W, GDN_D), lambda i: (i, 0, 0))],
        out_shape=[SDS((s, GDN_W), F32), SDS((n, GDN_W, GDN_D), F32)],
        scratch_shapes=[pltpu.VMEM((GDN_W, GDN_D), F32)],
        compiler_params=_cp(("arbitrary",)))(u, w, qd, kd, qk, gl)


def gdn_scan_bwd(u, w, qd, kd, qk, gl, states, do, name):
    s = u.shape[0]
    c = GDN_C
    n = _div(s, c)
    cps = _tile(n, GDN_CPS)
    t = cps * c
    steps = n // cps

    def body(u_r, w_r, qd_r, kd_r, qk_r, gl_r, s_r, do_r, du_o, dw_o, dqd_o, dkd_o, dqk_o, dgl_o, dst):
        @pl.when(pl.program_id(0) == 0)
        def _():
            dst[...] = jnp.zeros_like(dst)

        lane = lax.broadcasted_iota(jnp.int32, (1, LANE), 1)
        heads = range(GDN_H)
        hs = [pl.ds(h * GDN_D, GDN_D) for h in heads]
        qs = [pl.ds(h * c, c) for h in heads]
        for ci in reversed(range(cps)):
            rs = pl.ds(ci * c, c)
            sh = [s_r[ci, hs[h], :] for h in heads]
            shb = [x.astype(BF16) for x in sh]
            ds_out = [dst[hs[h], :] for h in heads]
            dsb = [x.astype(BF16) for x in ds_out]
            dob = [do_r[rs, hs[h]].astype(BF16) for h in heads]
            ws = [_dot(w_r[rs, hs[h]], shb[h], NN) for h in heads]
            dv1 = [_dot(qk_r[rs, qs[h]], dob[h], TN) for h in heads]
            dv2 = [_dot(kd_r[rs, hs[h]], dsb[h], NN) for h in heads]
            dqd = [_dot(dob[h], shb[h], NT) for h in heads]
            dsq = [_dot(qd_r[rs, hs[h]], dob[h], TN) for h in heads]
            vb = [(u_r[rs, hs[h]] - ws[h]).astype(BF16) for h in heads]
            dv = [dv1[h] + dv2[h] for h in heads]
            dvb = [x.astype(BF16) for x in dv]
            dw = [_dot(dvb[h], shb[h], NT) for h in heads]
            dkd = [_dot(vb[h], dsb[h], NT) for h in heads]
            dqk = [_dot(dob[h], vb[h], NT) for h in heads]
            dsw = [_dot(w_r[rs, hs[h]], dvb[h], TN) for h in heads]
            dgl_row = jnp.zeros((1, LANE), F32)
            for h in heads:
                du_o[rs, hs[h]] = dv[h]
                dw_o[rs, hs[h]] = -dw[h]
                dqd_o[rs, hs[h]] = dqd[h]
                dkd_o[rs, hs[h]] = dkd[h]
                dqk_o[rs, qs[h]] = dqk[h]
                dgl_row = dgl_row + jnp.where(lane == h, jnp.sum(jnp.sum(ds_out[h] * sh[h], axis=1, keepdims=True), axis=0, keepdims=True), 0.0)
                dst[hs[h], :] = ds_out[h] * _lane_scalar(gl_r[ci], h) + dsq[h] - dsw[h]
            dgl_o[ci] = dgl_row

    rev = lambda i: steps - 1 - i
    row = pl.BlockSpec((t, GDN_W), lambda i: (rev(i), 0))
    qks = pl.BlockSpec((t, GDN_H * c), lambda i: (rev(i), 0))
    gls = pl.BlockSpec((cps, 1, LANE), lambda i: (rev(i), 0, 0))
    return pl.pallas_call(
        body, name=name, grid=(steps,),
        in_specs=[row, row, row, row, qks, gls, pl.BlockSpec((cps, GDN_W, GDN_D), lambda i: (rev(i), 0, 0)), row],
        out_specs=[row, row, row, row, qks, gls],
        out_shape=[SDS((s, GDN_W), F32)] * 4 + [SDS((s, GDN_H * c), F32), SDS((n, 1, LANE), F32)],
        scratch_shapes=[pltpu.VMEM((GDN_W, GDN_D), F32)],
        compiler_params=_cp(("arbitrary",)))(u, w, qd, kd, qk, gl, states, do)


def _gdn_out_rows(o, z, nw):
    outs = []
    for h in range(GDN_H):
        hs = slice(h * GDN_D, (h + 1) * GDN_D)
        oh = o[:, hs]
        y = oh * lax.rsqrt(jnp.mean(oh * oh, axis=-1, keepdims=True) + RMS_EPS) * nw
        outs.append(y * _silu(z[:, hs]))
    return jnp.concatenate(outs, axis=1)


def gdn_out(o, p_all, nw, dy, lay, name):
    s = o.shape[0]
    tm = _tile(s, 512)
    bwd = dy is not None

    def body(*refs):
        o_r, z_r, nw_r = refs[:3]
        if not bwd:
            refs[3][...] = _gdn_out_rows(o_r[...], z_r[...], nw_r[...]).astype(BF16)
            return
        dy_r, do_o, dz_o, dnw_o = refs[3:]
        _, vjp = jax.vjp(_gdn_out_rows, o_r[...], z_r[...], nw_r[...])
        d_o, d_z, d_nw = vjp(dy_r[...].astype(F32))
        do_o[...] = d_o
        dz_o[...] = d_z.astype(BF16)

        @pl.when(pl.program_id(0) == 0)
        def _():
            dnw_o[...] = jnp.zeros_like(dnw_o)

        dnw_o[...] += d_nw

    row = pl.BlockSpec((tm, GDN_W), lambda i: (i, 0))
    zs = pl.BlockSpec((tm, GDN_W), lambda i: (i, _div(lay.z, GDN_W)))
    nws = pl.BlockSpec((1, GDN_D), lambda i: (0, 0))
    if not bwd:
        return pl.pallas_call(body, name=name, grid=(s // tm,), in_specs=[row, zs, nws], out_specs=row,
                              out_shape=SDS((s, GDN_W), BF16), compiler_params=_cp(("parallel",)))(o, p_all, nw)
    return pl.pallas_call(body, name=name, grid=(s // tm,), in_specs=[row, zs, nws, row], out_specs=[row, row, nws],
                          out_shape=[SDS((s, GDN_W), F32), SDS((s, GDN_W), BF16), SDS((1, GDN_D), F32)],
                          compiler_params=_cp(("arbitrary",)))(o, p_all, nw, dy)


def _cols_to_full(g):
    n, k, c = g.shape
    return g.transpose(1, 0, 2).reshape(k, n * c)


def _rows_to_blocks(w):
    return w.reshape(N_DEV, w.shape[0] // N_DEV, w.shape[1])


def _pack_small(parts, rows):
    flat = jnp.concatenate([jnp.pad(p.reshape(-1), (0, -p.size % LANE)) for p in parts])
    return jnp.pad(flat, (0, rows * LANE - flat.size)).reshape(rows, LANE)


def kernel(x, mem, g_mix, w_in, sinks, conv_w, a_log, dt_bias, gdn_norm_w, g_mem, w_mem_kv, w_swa_up, w_gdn_up, w_xa_up, w_out, g_mlp, w_mlp_in, w_mlp_out, g_final, loss_target, m_g_mix, m_w_in, m_sinks, m_conv_w, m_a_log, m_dt_bias, m_gdn_norm_w, m_g_mem, m_w_mem_kv, m_w_swa_up, m_w_gdn_up, m_w_xa_up, m_w_out, m_g_mlp, m_w_mlp_in, m_w_mlp_out, m_g_final, v_g_mix, v_w_in, v_sinks, v_conv_w, v_a_log, v_dt_bias, v_gdn_norm_w, v_g_mem, v_w_mem_kv, v_w_swa_up, v_w_gdn_up, v_w_xa_up, v_w_out, v_g_mlp, v_w_mlp_in, v_w_mlp_out, v_g_final):
    xs, ms, tgt = x[0], mem[0], loss_target[0]
    s, d = xs.shape
    lay = Layout(d)
    px, py, pc = _position()
    dev = 4 * px + 2 * py + pc

    g_in, g_conv = run_job(GatherJob([w_in[0].astype(BF16), conv_w[0]]), "gather_w_in")
    W_in = pad_w_in(g_in, lay)
    convw = _cols_to_full(g_conv)
    gp = jnp.zeros((8, LANE), F32).at[0, :GDN_H].set(a_log[0]).at[1, :GDN_H].set(dt_bias[0])
    later = [w_mem_kv[0], w_swa_up[0], w_gdn_up[0], w_xa_up[0], w_out[0], w_mlp_in[0]]

    n1 = rmsnorm_fwd(xs, g_mix, "norm_mix")
    p_all, g_mkv, W_sup, W_gup, W_xup, g_out, W_m1 = matmul(
        n1, W_in, mode="nn", out_dtype=F32, name="proj_in", tm=2048, tn=512, tk=d,
        side=GatherJob([w.astype(BF16) for w in later]))
    W_mkv = g_mkv.reshape(-1, g_mkv.shape[2])
    W_out = g_out.reshape(-1, d)
    y_a = swa_fwd(p_all, sinks, lay, "swa_fwd")
    xc = gdn_conv_fwd(p_all, convw, lay, "gdn_conv_fwd")
    u, gw, gqd, gkd, gqk, ggl = gdn_pre_fwd(xc, p_all, gp, lay, "gdn_pre_fwd")
    o_b, states = gdn_scan_fwd(u, gw, gqd, gkd, gqk, ggl, "gdn_scan_fwd")
    y_b = gdn_out(o_b, p_all, gdn_norm_w, None, lay, "gdn_out_fwd")
    nm = rmsnorm_fwd(ms, g_mem, "norm_mem")
    mkv = matmul(nm, W_mkv, mode="nn", out_dtype=BF16, name="proj_mem", tk=d)
    y_c = xattn_fwd(p_all, mkv, lay, "xattn_fwd")
    merged = merge(p_all, y_a, y_b, y_c, W_sup, W_gup, W_xup, None, lay, "merge_fwd")
    h1 = matmul(merged, W_out, mode="nn", out_dtype=F32, name="proj_out", tm=2048, tn=512, tk=d, resid=xs)
    n2 = rmsnorm_fwd(h1, g_mlp, "norm_mlp")
    uu, act, g_m2 = matmul(n2, W_m1, mode="nn", out_dtype=F32, name="mlp_in", tm=2048, tn=512, tk=d, b_cols=True,
                           relu2_out=True, side=GatherJob([w_mlp_out[0].astype(BF16)]))
    W_m2 = g_m2.reshape(-1, d)
    h2 = matmul(act, W_m2, mode="nn", out_dtype=F32, name="mlp_out", tm=1024, tn=2048, tk=512, resid=h1)
    dh2, dg_final, lrow, dh2_b = final_norm_loss(h2, g_final.reshape(1, d), tgt, "final_loss")
    loss = lax.psum(lrow[0, 0], ("x", "y", "c"))

    du = matmul(dh2_b, W_m2, mode="nt", out_dtype=BF16, name="mlp_out_dx", tm=2048, tn=512, tk=d, relu2_grad_of=uu)
    dW_m2 = matmul(act, dh2_b, mode="tn", out_dtype=BF16, name="mlp_out_dw", tm=1024, tn=2048, tk=1024)
    dW_m2 = _rows_to_blocks(dW_m2)
    dn2, sib_m2 = matmul(du, W_m1, mode="nt", out_dtype=F32, name="mlp_in_dx", tm=1024, tn=2048, tk=1024, b_cols=True,
                         side=PairExchangeJob([dW_m2], [False]))
    c_m2 = pair_add(dW_m2, False, sib_m2, "grads_pair_add_m2")
    dW_m1 = matmul(n2, du, mode="tn", out_dtype=BF16, name="mlp_in_dw", tm=2048, tn=1024, tk=512)
    dh1, dg_mlp, dh1_b = rmsnorm_bwd(h1, g_mlp, dn2, dh2, "norm_mlp_bwd", bf16_copy=True)

    dmerged, sib_m1 = matmul(dh1_b, W_out, mode="nt", out_dtype=F32, name="proj_out_dx", tm=2048, tn=512, tk=d,
                             side=PairExchangeJob([dW_m1], [True]))
    c_m1 = pair_add(dW_m1, True, sib_m1, "grads_pair_add_m1")
    dW_out = matmul(merged, dh1_b, mode="tn", out_dtype=BF16, name="proj_out_dw", tm=2048, tn=1024, tk=512)
    dgates, dta, dtb, dtc = merge(p_all, y_a, y_b, y_c, W_sup, W_gup, W_xup, dmerged, lay, "merge_bwd")
    dy_a = matmul(dta, W_sup, mode="nt", out_dtype=BF16, name="swa_up_dx", tm=2048, tk=d, b_cols=True)
    dy_b = matmul(dtb, W_gup, mode="nt", out_dtype=BF16, name="gdn_up_dx", tm=2048, tk=d, b_cols=True)
    dy_c = matmul(dtc, W_xup, mode="nt", out_dtype=BF16, name="xa_up_dx", tm=2048, tk=d, b_cols=True)
    dW_sup = matmul(y_a, dta, mode="tn", out_dtype=BF16, name="swa_up_dw", tn=2048)
    dW_gup = matmul(y_b, dtb, mode="tn", out_dtype=BF16, name="gdn_up_dw", tn=2048)
    dW_xup = matmul(y_c, dtc, mode="tn", out_dtype=BF16, name="xa_up_dw", tn=2048)

    dq_a, dk_a, dv_a, dsinks = swa_bwd(p_all, sinks, dy_a, lay, "swa_bwd")
    dq_c, dmkv = xattn_bwd(p_all, mkv, dy_c, lay, "xattn_bwd")
    dW_mkv = matmul(nm, dmkv, mode="tn", out_dtype=BF16, name="proj_mem_dw", tk=256)
    dnm = matmul(dmkv, W_mkv, mode="nt", out_dtype=F32, name="proj_mem_dx", tk=1024)
    _, dg_mem = rmsnorm_bwd(ms, g_mem, dnm, None, "norm_mem_bwd")

    do_b, dz, dnorm_w = gdn_out(o_b, p_all, gdn_norm_w, dy_b, lay, "gdn_out_bwd")
    du_g, dw_g, dqd_g, dkd_g, dqk_g, dgl_g = gdn_scan_bwd(u, gw, gqd, gkd, gqk, ggl, states, do_b, "gdn_scan_bwd")
    dxc, dab, dgp = gdn_pre_bwd(xc, p_all, gp, du_g, dw_g, dqd_g, dkd_g, dqk_g, dgl_g, lay, "gdn_pre_bwd")
    dqkv, dconv = gdn_conv_bwd(p_all, convw, dxc, lay, "gdn_conv_bwd")

    drest = jnp.concatenate([dq_a, dqkv, dz, dq_c, dk_a, dv_a, dab, jnp.zeros((s, lay.pw - lay.end), BF16)], axis=1)
    def pair_stage(grads, cols, tag):
        from_sib = run_job(PairExchangeJob(grads, cols), "grads_pair_exchange_" + tag)
        return [pair_add(g, cl, o, "grads_pair_add_%s%d" % (tag, i)) for i, (g, cl, o) in enumerate(zip(grads, cols, from_sib))]

    small = pair_stage([_rows_to_blocks(dW_mkv), dW_sup, dW_gup, dW_xup, _rows_to_blocks(dW_out)],
                       [False, True, True, True, False], "a")
    dW_in, p_m1, p_m2 = matmul(n1, dgates, tail=drest, mode="tn", out_dtype=BF16, name="proj_in_dw", tm=2048, tn=1024, tk=512,
                               side=ChipExchangeJob([c_m1, c_m2]))
    late = pair_stage([unpad_dw_in(dW_in, lay)], [False], "b")
    dn1, p_in, p_mkv, p_sup, p_gup, p_xup, p_out = matmul(
        dgates, W_in, tail=drest, mode="nt", out_dtype=F32, name="proj_in_dx", tm=1024, tn=2048, tk=1024,
        side=ChipExchangeJob(late + small))
    grad_x, dg_mix = rmsnorm_bwd(xs, g_mix, dn1, dh1, "norm_mix_bwd")
    parts = [p_in, p_mkv, p_sup, p_gup, p_xup, p_out, p_m1, p_m2]

    shard_names = [(w_in, m_w_in, v_w_in), (w_mem_kv, m_w_mem_kv, v_w_mem_kv), (w_swa_up, m_w_swa_up, v_w_swa_up),
                   (w_gdn_up, m_w_gdn_up, v_w_gdn_up), (w_xa_up, m_w_xa_up, v_w_xa_up), (w_out, m_w_out, v_w_out),
                   (w_mlp_in, m_w_mlp_in, v_w_mlp_in), (w_mlp_out, m_w_mlp_out, v_w_mlp_out)]
    big_res = [adamw(p, w[0], m[0], v[0], "adamw_%d" % i) for i, (p, (w, m, v)) in enumerate(zip(parts, shard_names))]

    smalls = [(g_mix, m_g_mix, v_g_mix, dg_mix), (sinks, m_sinks, v_sinks, dsinks[:, :SWA_HQ]),
              (a_log, m_a_log, v_a_log, dgp[0:1, :GDN_H]), (dt_bias, m_dt_bias, v_dt_bias, dgp[1:2, :GDN_H]),
              (gdn_norm_w, m_gdn_norm_w, v_gdn_norm_w, dnorm_w), (g_mem, m_g_mem, v_g_mem, dg_mem),
              (g_mlp, m_g_mlp, v_g_mlp, dg_mlp), (g_final, m_g_final, v_g_final, dg_final)]
    sizes = [-(-t[0].size // LANE) * LANE for t in smalls] + [GDN_CONV * 3 * GDN_W]
    rows = -(-sum(sizes) // (8 * LANE)) * 8
    csh = conv_w.shape[2]

    def conv_place(a):
        full = jnp.tile(a[0], (1, N_DEV))
        owner = lax.broadcasted_iota(jnp.int32, full.shape, 1) // csh
        return jnp.where(owner == dev, full, 0.0)

    g_pack = _pack_small([t[3] for t in smalls] + [dconv], rows)
    w_pack = _pack_small([t[0] for t in smalls] + [conv_place(conv_w)], rows)
    m_pack = _pack_small([t[1] for t in smalls] + [conv_place(m_conv_w)], rows)
    v_pack = _pack_small([t[2] for t in smalls] + [conv_place(v_conv_w)], rows)
    g_all = run_job(GatherJob([g_pack]), "gather_small_grads")[0]
    small_res = adamw(g_all, w_pack, m_pack, v_pack, "adamw_small")

    def unpack(arr):
        flat = arr.reshape(-1)
        outs, off = [], 0
        for t, sz in zip(smalls, sizes[:-1]):
            outs.append(flat[off:off + t[0].size].reshape(t[0].shape))
            off += sz
        cw = flat[off:off + sizes[-1]].reshape(GDN_CONV, 3 * GDN_W)
        mine = (lax.broadcasted_iota(jnp.int32, (1, N_DEV, 1), 1) == dev).astype(F32)
        outs.append(jnp.sum(cw.reshape(GDN_CONV, N_DEV, csh) * mine, axis=1)[None])
        return outs

    sg, sd, sm, sv = (unpack(a) for a in small_res)
    bg, bd, bm, bv = ([r[i][None] for r in big_res] for i in range(4))

    def ordered(sm_, bg_):
        return [sm_[0], bg_[0], sm_[1], sm_[8], sm_[2], sm_[3], sm_[4], sm_[5], bg_[1], bg_[2], bg_[3], bg_[4], bg_[5],
                sm_[6], bg_[6], bg_[7], sm_[7]]

    return (loss, grad_x[None], *ordered(sg, bg), *ordered(sd, bd), *ordered(sm, bm), *ordered(sv, bv))
```

```python
import functools
import math

import jax
import jax.numpy as jnp
from jax import lax
from jax.experimental import pallas as pl
from jax.experimental.pallas import tpu as pltpu

F32, BF16 = jnp.float32, jnp.bfloat16
SDS = jax.ShapeDtypeStruct
MESH = pl.DeviceIdType.MESH
ANY = pl.BlockSpec(memory_space=pl.ANY)

SWA_HQ, SWA_HKV, SWA_HD, SWA_W = 16, 2, 64, 128
SWA_G = SWA_HQ // SWA_HKV
GDN_H, GDN_D, GDN_CONV, GDN_C = 4, 128, 4, 64
XA_H, XA_D = 4, 128
Q_W = SWA_HQ * SWA_HD
KV_W = SWA_HKV * SWA_HD
GDN_W = GDN_H * GDN_D
XA_W = XA_H * XA_D
RMS_EPS = 1e-6
L2_EPS = 1e-6
NEG = -1e30
N_DEV = 8
LANE = 128

ADAM_LR, ADAM_B1, ADAM_B2, ADAM_EPS, ADAM_WD, ADAM_STEP = 0.001, 0.9, 0.999, 1e-08, 0.01, 10

VMEM_BIG = 56 * 1024 * 1024


def _cp(sem, vmem=VMEM_BIG):
    return pltpu.CompilerParams(dimension_semantics=sem, vmem_limit_bytes=vmem)


def _div(a, b):
    assert a % b == 0, (a, b)
    return a // b


def _tile(n, t):
    t = min(t, n)
    assert n % t == 0, (n, t)
    return t


def _sigmoid(x):
    return 1.0 / (1.0 + jnp.exp(-x))


def _silu(x):
    return x * _sigmoid(x)


def _softplus(x):
    return jnp.maximum(x, 0.0) + jnp.log1p(jnp.exp(-jnp.abs(x)))


def _dot(a, b, dims, prec=None):
    return lax.dot_general(a, b, (dims, ((), ())), precision=prec, preferred_element_type=F32)


NN = ((1,), (0,))
NT = ((1,), (1,))
TN = ((0,), (0,))
HI = lax.Precision.HIGHEST


def _bdot_plain(a, b, dims):
    return _dot(a.astype(BF16), b.astype(BF16), dims)


@functools.partial(jax.custom_vjp, nondiff_argnums=(2,))
def _bdot_vjp(a, b, dims):
    return _bdot_plain(a, b, dims)


def _bdot_vjp_fwd(a, b, dims):
    return _bdot_plain(a, b, dims), (a, b)


def _bdot_vjp_bwd(dims, res, ct):
    a, b = res
    if dims == NN:
        return _bdot_plain(ct, b, NT), _bdot_plain(a, ct, TN)
    assert dims == NT, dims
    return _bdot_plain(ct, b, NN), _bdot_plain(ct, a, TN)


_bdot_vjp.defvjp(_bdot_vjp_fwd, _bdot_vjp_bwd)


def _neumann(xs):
    pws, nns = list(xs), list(xs)
    for _ in range(5):
        pws = [_bdot_plain(p, p, NN) for p in pws]
        nns = [n + p + _bdot_plain(n, p, NN) for n, p in zip(nns, pws)]
    return tuple(nns)


@jax.custom_vjp
def _neumann_vjp(xs):
    return _neumann(xs)


def _neumann_vjp_fwd(xs):
    nns = _neumann(xs)
    return nns, nns


def _neumann_vjp_bwd(nns, cts):
    ts = [ct + _bdot_plain(nn, ct, TN) for nn, ct in zip(nns, cts)]
    return (tuple(t + _bdot_plain(t, nn, NT) for t, nn in zip(ts, nns)),)


_neumann_vjp.defvjp(_neumann_vjp_fwd, _neumann_vjp_bwd)


class Layout:
    def __init__(self, d):
        self.d = d
        self.g = 0
        self.q = 3 * d
        self.qkv = self.q + Q_W
        self.z = self.qkv + 3 * GDN_W
        self.qc = self.z + GDN_W
        self.k = self.qc + XA_W
        self.v = self.k + KV_W
        self.ab = self.v + KV_W
        self.end = self.ab + LANE
        self.pw = -(-self.end // 1024) * 1024
        self.lq, self.lk, self.lv, self.lqkv = 0, Q_W, Q_W + KV_W, Q_W + 2 * KV_W
        self.la = self.lqkv + 3 * GDN_W
        self.lz = self.la + 2 * GDN_H
        self.lqc = self.lz + GDN_W
        self.lg = self.lqc + XA_W
        self.lw = self.lg + 3 * d

    def pieces(self):
        segs = [(self.lq, self.lk, self.q), (self.lk, self.lv, self.k), (self.lv, self.lqkv, self.v),
                (self.lqkv, self.la, self.qkv), (self.la, self.lz, self.ab), (self.lz, self.lqc, self.z),
                (self.lqc, self.lg, self.qc), (self.lg, self.lw, self.g)]
        cw = _div(self.lw, N_DEV)
        out = []
        for dev in range(N_DEV):
            lo, hi = dev * cw, (dev + 1) * cw
            for ls, le, ps in segs:
                s, e = max(lo, ls), min(hi, le)
                if s < e:
                    out.append((dev, s - lo, ps + s - ls, e - s))
        return out


def pad_w_in(g, lay):
    nd, k, cw = g.shape
    tr = _tile(k, 256)
    tail = lay.ab + 2 * GDN_H

    def body(g_ref, o_ref):
        o_ref[:, pl.ds(tail, lay.pw - tail)] = jnp.zeros((tr, lay.pw - tail), o_ref.dtype)
        for dev, so, po, ln in lay.pieces():
            o_ref[:, pl.ds(po, ln)] = g_ref[dev, :, pl.ds(so, ln)]

    return pl.pallas_call(
        body, name="pad_w_in", grid=(k // tr,), in_specs=[pl.BlockSpec((nd, tr, cw), lambda i: (0, i, 0))],
        out_specs=pl.BlockSpec((tr, lay.pw), lambda i: (i, 0)), out_shape=SDS((k, lay.pw), g.dtype),
        compiler_params=_cp(("parallel",)))(g)


def unpad_dw_in(dw, lay):
    k = dw.shape[0]
    cw = _div(lay.lw, N_DEV)
    tr = _tile(k, 256)

    def body(d_ref, o_ref):
        for dev, so, po, ln in lay.pieces():
            o_ref[dev, :, pl.ds(so, ln)] = d_ref[:, pl.ds(po, ln)]

    return pl.pallas_call(
        body, name="unpad_dw_in", grid=(k // tr,), in_specs=[pl.BlockSpec((tr, lay.pw), lambda i: (i, 0))],
        out_specs=pl.BlockSpec((N_DEV, tr, cw), lambda i: (0, i, 0)), out_shape=SDS((N_DEV, k, cw), dw.dtype),
        compiler_params=_cp(("parallel",)))(dw)


def _position():
    return lax.axis_index("x"), lax.axis_index("y"), lax.axis_index("c")


class GatherJob:
    def __init__(self, arrs):
        self.ins = list(arrs)
        n = len(arrs)
        self.out_shapes = [SDS((N_DEV,) + a.shape, a.dtype) for a in arrs]
        self.scratch = [pltpu.SemaphoreType.DMA((n, 7)), pltpu.SemaphoreType.DMA((n, 7)), pltpu.SemaphoreType.DMA((n,))]

    def _ctx(self, outs, sems):
        send_sems, recv_sems, _ = sems
        x, y, c = _position()

        def blk(o, p):
            return o.at[4 * p[0] + 2 * p[1] + p[2]]

        def copy(i, k, block, to, src=None):
            return pltpu.make_async_remote_copy(
                src_ref=blk(outs[i], block) if src is None else src, dst_ref=blk(outs[i], block),
                send_sem=send_sems.at[i, k], recv_sem=recv_sems.at[i, k], device_id=to, device_id_type=MESH)

        return (x, y, c), (x, y, 1 - c), [(1 - x, y), (x, 1 - y), (1 - x, 1 - y)], blk, copy

    def start(self, ins, outs, sems):
        me, sibling, chips, blk, copy = self._ctx(outs, sems)
        for i in range(len(ins)):
            pltpu.make_async_copy(ins[i], blk(outs[i], me), sems[2].at[i]).start()
            copy(i, 0, me, sibling, src=ins[i]).start()
            for j, chip in enumerate(chips):
                copy(i, 1 + j, me, (*chip, me[2]), src=ins[i]).start()

    def mid(self, ins, outs, sems):
        me, sibling, chips, blk, copy = self._ctx(outs, sems)
        for i in range(len(ins)):
            for j, chip in enumerate(chips):
                copy(i, 1 + j, (*chip, me[2]), me).wait_recv()
                copy(i, 4 + j, (*chip, me[2]), sibling).start()

    def finish(self, ins, outs, sems):
        me, sibling, chips, blk, copy = self._ctx(outs, sems)
        for i in range(len(ins)):
            copy(i, 0, sibling, me).wait_recv()
            for j, chip in enumerate(chips):
                copy(i, 4 + j, (*chip, 1 - me[2]), me).wait_recv()
        for i in range(len(ins)):
            pltpu.make_async_copy(ins[i], blk(outs[i], me), sems[2].at[i]).wait()
            copy(i, 0, me, sibling, src=ins[i]).wait_send()
            for j, chip in enumerate(chips):
                copy(i, 1 + j, me, (*chip, me[2]), src=ins[i]).wait_send()
                copy(i, 4 + j, (*chip, me[2]), sibling).wait_send()


class ChipExchangeJob:
    mid = None

    def __init__(self, arrs):
        self.ins = list(arrs)
        n = len(arrs)
        self.out_shapes = [SDS(a.shape, a.dtype) for a in arrs]
        self.scratch = [pltpu.SemaphoreType.DMA((n, 3)), pltpu.SemaphoreType.DMA((n, 3)), pltpu.SemaphoreType.DMA((n,))]

    def _copies(self, ins, outs, sems, i, arrivals):
        send_sems, recv_sems, local_sems = sems
        x, y, c = _position()
        my_chip = 2 * x + y
        chips = [(1 - x, y), (x, 1 - y), (1 - x, 1 - y)]
        if arrivals:
            return [pltpu.make_async_remote_copy(
                src_ref=ins[i].at[my_chip], dst_ref=outs[i].at[2 * px + py], send_sem=send_sems.at[i, k],
                recv_sem=recv_sems.at[i, k], device_id=(px, py, c), device_id_type=MESH) for k, (px, py) in enumerate(chips)]
        local = pltpu.make_async_copy(ins[i].at[my_chip], outs[i].at[my_chip], local_sems.at[i])
        return local, [pltpu.make_async_remote_copy(
            src_ref=ins[i].at[2 * px + py], dst_ref=outs[i].at[my_chip], send_sem=send_sems.at[i, k],
            recv_sem=recv_sems.at[i, k], device_id=(px, py, c), device_id_type=MESH) for k, (px, py) in enumerate(chips)]

    def start(self, ins, outs, sems):
        for i in range(len(ins)):
            local, remote = self._copies(ins, outs, sems, i, False)
            local.start()
            for cp in remote:
                cp.start()

    def finish(self, ins, outs, sems):
        for i in range(len(ins)):
            for cp in self._copies(ins, outs, sems, i, True):
                cp.wait_recv()
            local, remote = self._copies(ins, outs, sems, i, False)
            for cp in remote:
                cp.wait_send()
            local.wait()


def _slab_shape(g, cols):
    return (g.shape[0], _div(g.shape[1], N_DEV)) if cols else g.shape[1:]


class PairExchangeJob:
    mid = None

    def __init__(self, grads, cols):
        self.ins, self.cols = list(grads), list(cols)
        n = len(grads)
        self.out_shapes = [SDS((4,) + _slab_shape(g, cl), g.dtype) for g, cl in zip(grads, cols)]
        self.scratch = [pltpu.SemaphoreType.DMA((n, 4)), pltpu.SemaphoreType.DMA((n, 4))]

    def _copies(self, ins, outs, sems):
        send_sems, recv_sems = sems
        x, y, c = _position()

        def part(i, dst):
            if not self.cols[i]:
                return ins[i].at[dst]
            cw = _slab_shape(self.ins[i], True)[1]
            return ins[i].at[:, pl.ds(pl.multiple_of(dst * cw, LANE), cw)]

        return [pltpu.make_async_remote_copy(src_ref=part(i, 2 * j + 1 - c), dst_ref=outs[i].at[j], send_sem=send_sems.at[i, j],
                                             recv_sem=recv_sems.at[i, j], device_id=(x, y, 1 - c), device_id_type=MESH)
                for i in range(len(ins)) for j in range(4)]

    def start(self, ins, outs, sems):
        for cp in self._copies(ins, outs, sems):
            cp.start()

    def finish(self, ins, outs, sems):
        for cp in self._copies(ins, outs, sems):
            cp.wait()


def run_job(job, name):
    n = len(job.ins)

    def body(*refs):
        ins, outs, sems = refs[:n], refs[n:2 * n], refs[2 * n:]
        job.start(ins, outs, sems)
        if job.mid is not None:
            job.mid(ins, outs, sems)
        job.finish(ins, outs, sems)

    return pl.pallas_call(body, name=name, out_shape=job.out_shapes, in_specs=[ANY] * n, out_specs=[ANY] * n,
                          scratch_shapes=job.scratch)(*job.ins)


def pair_add(grad, cols, other, name):
    r, c = _slab_shape(grad, cols)
    tr = _tile(r, 256)
    parity = lax.axis_index("c").astype(jnp.int32).reshape(1)

    def body(par_ref, a_ref, b_ref, o_ref):
        o_ref[...] = (a_ref[...].astype(F32) + b_ref[...].astype(F32)).astype(BF16)

    spec = pl.BlockSpec((None, tr, c), lambda j, i, par: (j, i, 0))
    if cols:
        own = pl.BlockSpec((tr, c), lambda j, i, par: (i, 2 * j + par[0]))
    else:
        own = pl.BlockSpec((None, tr, c), lambda j, i, par: (2 * j + par[0], i, 0))
    return pl.pallas_call(
        body, name=name, out_shape=SDS(other.shape, BF16),
        grid_spec=pltpu.PrefetchScalarGridSpec(num_scalar_prefetch=1, grid=(4, r // tr), in_specs=[own, spec], out_specs=spec),
        compiler_params=_cp(("parallel", "parallel")))(parity, grad, other)


def adamw(parts, w, m, v, name):
    p, r, c = parts.shape
    tr = _tile(r, 128 if c > 1024 else 256)

    def body(p_ref, w_ref, m_ref, v_ref, g_out, d_out, m_out, v_out):
        g = p_ref[0].astype(F32)
        for j in range(1, p):
            g = g + p_ref[j].astype(F32)
        mn = ADAM_B1 * m_ref[...] + (1.0 - ADAM_B1) * g
        vn = ADAM_B2 * v_ref[...] + (1.0 - ADAM_B2) * jnp.square(g)
        m_hat = mn / (1.0 - ADAM_B1 ** ADAM_STEP)
        v_hat = vn / (1.0 - ADAM_B2 ** ADAM_STEP)
        g_out[...] = g
        d_out[...] = -ADAM_LR * (m_hat / (jnp.sqrt(v_hat) + ADAM_EPS) + ADAM_WD * w_ref[...])
        m_out[...] = mn
        v_out[...] = vn

    spec = pl.BlockSpec((tr, c), lambda i: (i, 0))
    return pl.pallas_call(
        body, name=name, grid=(r // tr,),
        in_specs=[pl.BlockSpec((p, tr, c), lambda i: (0, i, 0)), spec, spec, spec],
        out_specs=[spec] * 4, out_shape=[SDS((r, c), F32)] * 4, compiler_params=_cp(("parallel",)))(parts, w, m, v)


def matmul(a, b, *, mode, out_dtype, name, tm=1024, tn=1024, tk=512, a_relu2=False, resid=None, relu2_grad_of=None,
           b_cols=False, relu2_out=False, side=None, tail=None):
    if b_cols:
        nb, brows, bc = b.shape
        bshape = (brows, nb * bc)
    else:
        bshape = b.shape
    head_k = head_n = None
    if mode == "nn":
        (m, k), (k2, n) = a.shape, bshape
    elif mode == "nt":
        (m, k), (n, k2) = a.shape, bshape
        if tail is not None:
            head_k, k = k, k + tail.shape[1]
    else:
        (k, m), (k2, n) = a.shape, bshape
        if tail is not None:
            head_n, n = n, n + tail.shape[1]
    assert k == k2 and (tail is None or mode != "nn"), (a.shape, b.shape, mode)
    tm, tn, tk = _tile(m, tm), _tile(n, tn), _tile(k, tk)
    if b_cols and mode == "nn":
        tn = _tile(bc, tn)
    if b_cols and mode == "nt":
        tk = _tile(bc, tk)
    nk = k // tk
    ni, nj = m // tm, n // tn
    nk_head = _div(head_k, tk) if head_k is not None else None
    nj_head = _div(head_n, tn) if head_n is not None else None
    use_acc = nk > 1 or tail is not None
    dims = {"nn": NN, "nt": NT, "tn": TN}[mode]
    extras = [e for e in (resid, relu2_grad_of) if e is not None]
    tails = [tail] if tail is not None else []
    n_side = len(side.ins) if side is not None else 0
    n_main = 2 if relu2_out else 1

    def body(*refs):
        a_ref, b_ref = refs[:2]
        t_ref = refs[2] if tails else None
        n_op = 2 + len(tails)
        e_refs = refs[n_op:n_op + len(extras)]
        n_in = n_op + len(extras) + n_side
        o_ref = refs[n_in]
        act_ref = refs[n_in + 1] if relu2_out else None
        acc_ref = refs[n_in + n_main + n_side] if use_acc else None
        if side is not None:
            s_ins = refs[n_op + len(extras):n_in]
            s_outs = refs[n_in + n_main:n_in + n_main + n_side]
            s_sems = refs[len(refs) - len(side.scratch):]
            step = (pl.program_id(0) * nj + pl.program_id(1)) * nk + pl.program_id(2)
            pl.when(step == 0)(lambda: side.start(s_ins, s_outs, s_sems))
            if side.mid is not None:
                pl.when(step == (ni * nj * nk * 85) // 100)(lambda: side.mid(s_ins, s_outs, s_sems))

        def operands(a_from=a_ref, b_from=b_ref):
            av = a_from[...]
            if a_relu2:
                av = jnp.square(jnp.maximum(av.astype(F32), 0.0))
            return av.astype(BF16), b_from[...].astype(BF16)

        def finish(r):
            e = list(e_refs)
            if resid is not None:
                r = r + e.pop(0)[...]
            if relu2_grad_of is not None:
                r = r * (2.0 * jnp.maximum(e.pop(0)[...], 0.0))
            o_ref[...] = r.astype(out_dtype)
            if relu2_out:
                act_ref[...] = jnp.square(jnp.maximum(r, 0.0)).astype(BF16)

        if not use_acc:
            av, bv = operands()
            finish(_dot(av, bv, dims))
        else:
            kk = pl.program_id(2)

            def accumulate(a_from, b_from):
                def product():
                    av, bv = operands(a_from, b_from)
                    return _dot(av, bv, dims)

                if nk == 1:
                    finish(product())
                    return

                @pl.when(kk == 0)
                def _():
                    acc_ref[...] = product()

                @pl.when((kk > 0) & (kk < nk - 1))
                def _():
                    acc_ref[...] += product()

                @pl.when(kk == nk - 1)
                def _():
                    finish(acc_ref[...] + product())

            if not tails:
                accumulate(a_ref, b_ref)
            elif mode == "nt":
                pl.when(kk < nk_head)(lambda: accumulate(a_ref, b_ref))
                pl.when(kk >= nk_head)(lambda: accumulate(t_ref, b_ref))
            else:
                in_head = pl.program_id(1) < nj_head
                pl.when(in_head)(lambda: accumulate(a_ref, b_ref))
                pl.when(jnp.logical_not(in_head))(lambda: accumulate(a_ref, t_ref))

        if side is not None:
            pl.when(step == ni * nj * nk - 1)(lambda: side.finish(s_ins, s_outs, s_sems))

    a_spec = {"nn": pl.BlockSpec((tm, tk), lambda i, j, kk: (i, kk)),
              "nt": pl.BlockSpec((tm, tk), lambda i, j, kk: (i, kk)),
              "tn": pl.BlockSpec((tk, tm), lambda i, j, kk: (kk, i))}[mode]
    t_specs = []
    if tails and mode == "nt":
        a_spec = pl.BlockSpec((tm, tk), lambda i, j, kk: (i, jnp.minimum(kk, nk_head - 1)))
        t_specs = [pl.BlockSpec((tm, tk), lambda i, j, kk: (i, jnp.maximum(kk - nk_head, 0)))]
    if tails and mode == "tn":
        t_specs = [pl.BlockSpec((tk, tn), lambda i, j, kk: (kk, jnp.maximum(j - nj_head, 0)))]
    if tails and mode == "tn":
        b_spec = pl.BlockSpec((tk, tn), lambda i, j, kk: (kk, jnp.minimum(j, nj_head - 1)))
    elif not b_cols:
        b_spec = {"nn": pl.BlockSpec((tk, tn), lambda i, j, kk: (kk, j)),
                  "nt": pl.BlockSpec((tn, tk), lambda i, j, kk: (j, kk)),
                  "tn": pl.BlockSpec((tk, tn), lambda i, j, kk: (kk, j))}[mode]
    elif mode == "nn":
        per = bc // tn
        b_spec = pl.BlockSpec((None, tk, tn), lambda i, j, kk: (j // per, kk, j % per))
    else:
        assert mode == "nt", mode
        per = bc // tk
        b_spec = pl.BlockSpec((None, tn, tk), lambda i, j, kk: (kk // per, j, kk % per))
    e_spec = pl.BlockSpec((tm, tn), lambda i, j, kk: (i, j))
    main_shapes = [SDS((m, n), out_dtype)] + ([SDS((m, n), BF16)] if relu2_out else [])
    res = pl.pallas_call(
        body, name=name, grid=(ni, nj, nk),
        in_specs=[a_spec, b_spec] + t_specs + [e_spec] * len(extras) + [ANY] * n_side,
        out_specs=[e_spec] * n_main + [ANY] * n_side, out_shape=main_shapes + (side.out_shapes if side is not None else []),
        scratch_shapes=([pltpu.VMEM((tm, tn), F32)] if use_acc else []) + (side.scratch if side is not None else []),
        compiler_params=_cp(("arbitrary", "arbitrary", "arbitrary")))(a, b, *tails, *extras, *(side.ins if side is not None else []))
    return res if len(res) > 1 else res[0]


def rmsnorm_fwd(x, g, name):
    s, d = x.shape
    tm = _tile(s, 256)

    def body(x_ref, g_ref, o_ref):
        xv = x_ref[...]
        r = lax.rsqrt(jnp.mean(xv * xv, axis=-1, keepdims=True) + RMS_EPS)
        o_ref[...] = (xv * r * g_ref[...]).astype(BF16)

    row = pl.BlockSpec((tm, d), lambda i: (i, 0))
    return pl.pallas_call(body, name=name, grid=(s // tm,), in_specs=[row, pl.BlockSpec((1, d), lambda i: (0, 0))],
                          out_specs=row, out_shape=SDS((s, d), BF16), compiler_params=_cp(("parallel",)))(x, g)


def _rms_bwd_rows(xv, gv, dy):
    r = lax.rsqrt(jnp.mean(xv * xv, axis=-1, keepdims=True) + RMS_EPS)
    xh = xv * r
    dxh = dy * gv
    dx = r * (dxh - xh * jnp.mean(dxh * xh, axis=-1, keepdims=True))
    return dx, jnp.sum(dy * xh, axis=0, keepdims=True)


def rmsnorm_bwd(x, g, dn, resid, name, bf16_copy=False):
    s, d = x.shape
    tm = _tile(s, 256)
    has_r = resid is not None

    def body(*refs):
        x_ref, g_ref, dn_ref = refs[:3]
        dx_ref, dg_ref = refs[3 + has_r:5 + has_r]
        dx, part = _rms_bwd_rows(x_ref[...], g_ref[...], dn_ref[...].astype(F32))
        if has_r:
            dx = dx + refs[3][...]
        dx_ref[...] = dx
        if bf16_copy:
            refs[5 + has_r][...] = dx.astype(BF16)

        @pl.when(pl.program_id(0) == 0)
        def _():
            dg_ref[...] = jnp.zeros_like(dg_ref)

        dg_ref[...] += part

    row = pl.BlockSpec((tm, d), lambda i: (i, 0))
    vec = pl.BlockSpec((1, d), lambda i: (0, 0))
    ins = [x, g, dn] + ([resid] if has_r else [])
    return pl.pallas_call(body, name=name, grid=(s // tm,), in_specs=[row, vec, row] + ([row] if has_r else []),
                          out_specs=[row, vec] + ([row] if bf16_copy else []),
                          out_shape=[SDS((s, d), F32), SDS((1, d), F32)] + ([SDS((s, d), BF16)] if bf16_copy else []),
                          compiler_params=_cp(("arbitrary",)))(*ins)


def final_norm_loss(h, g, tgt, name):
    s, d = h.shape
    tm = _tile(s, 256)

    def body(h_ref, g_ref, t_ref, dh_ref, dg_ref, l_ref, dhb_ref):
        xv, gv = h_ref[...], g_ref[...]
        r = lax.rsqrt(jnp.mean(xv * xv, axis=-1, keepdims=True) + RMS_EPS)
        e = xv * r * gv - t_ref[...]
        lpart = 0.5 * jnp.sum(jnp.mean(e * e, axis=-1, keepdims=True), axis=0, keepdims=True)
        dx, part = _rms_bwd_rows(xv, gv, e * (1.0 / d))
        dh_ref[...] = dx
        dhb_ref[...] = dx.astype(BF16)

        @pl.when(pl.program_id(0) == 0)
        def _():
            dg_ref[...] = jnp.zeros_like(dg_ref)
            l_ref[...] = jnp.zeros_like(l_ref)

        dg_ref[...] += part
        l_ref[...] += jnp.broadcast_to(lpart, l_ref.shape)

    row = pl.BlockSpec((tm, d), lambda i: (i, 0))
    vec = pl.BlockSpec((1, d), lambda i: (0, 0))
    lsp = pl.BlockSpec((1, LANE), lambda i: (0, 0))
    return pl.pallas_call(body, name=name, grid=(s // tm,), in_specs=[row, vec, row], out_specs=[row, vec, lsp, row],
                          out_shape=[SDS((s, d), F32), SDS((1, d), F32), SDS((1, LANE), F32), SDS((s, d), BF16)],
                          compiler_params=_cp(("arbitrary",)))(h, g, tgt)


def merge(p_all, ya, yb, yc, wa, wb, wc, dm, lay, name):
    s, d = ya.shape[0], lay.d
    wcols = wa.shape[2]
    bwd = dm is not None
    tm, tn = _tile(s, 2048), _tile(wcols, 512)
    nj, per = d // tn, wcols // tn

    y_specs = [pl.BlockSpec((tm, y.shape[1]), lambda i, j, *_: (i, 0)) for y in (ya, yb, yc)]
    w_specs = [pl.BlockSpec((None, w.shape[1], tn), lambda i, j, *_: (j // per, 0, j % per)) for w in (wa, wb, wc)]
    o_spec = pl.BlockSpec((tm, tn), lambda i, j, *_: (i, j))
    if not bwd:
        def body(ga, gb, gc, ya_r, yb_r, yc_r, wa_r, wb_r, wc_r, o_ref):
            ts = [_dot(y[...], w[...], NN) for y, w in ((ya_r, wa_r), (yb_r, wb_r), (yc_r, wc_r))]
            gs = [_sigmoid(g[...]) for g in (ga, gb, gc)]
            o_ref[...] = (gs[0] * ts[0] + gs[1] * ts[1] + gs[2] * ts[2]).astype(BF16)

        gate_specs = [pl.BlockSpec((tm, tn), lambda i, j, b=b: (i, b * nj + j)) for b in range(3)]
        return pl.pallas_call(
            body, name=name, grid=(s // tm, nj), in_specs=gate_specs + y_specs + w_specs,
            out_specs=o_spec, out_shape=SDS((s, d), BF16),
            compiler_params=_cp(("parallel", "parallel")))(p_all, p_all, p_all, ya, yb, yc, wa, wb, wc)

    def body_bwd(g_r, ya_r, yb_r, yc_r, wa_r, wb_r, wc_r, dm_r, dg_o, dta_o, dtb_o, dtc_o):
        for k, (y, w, dt_o) in enumerate(((ya_r, wa_r, dta_o), (yb_r, wb_r, dtb_o), (yc_r, wc_r, dtc_o))):
            @pl.when(pl.program_id(2) == k)
            def _(y=y, w=w, dt_o=dt_o):
                t = _dot(y[...], w[...], NN)
                g = _sigmoid(g_r[...])
                dmv = dm_r[...]
                dg_o[...] = (dmv * t * (g * (1.0 - g))).astype(BF16)
                dt_o[...] = (dmv * g).astype(BF16)

    gate_spec = pl.BlockSpec((tm, tn), lambda i, j, b: (i, b * nj + j))
    return pl.pallas_call(
        body_bwd, name=name, grid=(s // tm, nj, 3), in_specs=[gate_spec] + y_specs + w_specs + [o_spec],
        out_specs=[gate_spec, o_spec, o_spec, o_spec], out_shape=[SDS((s, 3 * d), BF16)] + [SDS((s, d), BF16)] * 3,
        compiler_params=_cp(("arbitrary", "arbitrary", "arbitrary")))(p_all, ya, yb, yc, wa, wb, wc, dm)


SWA_PAIRS = SWA_G // 2


def _swa_probs(qs, kcs, sinks, first):
    shape = (qs[0].shape[0], 2 * SWA_W)
    qi = lax.broadcasted_iota(jnp.int32, shape, 0) % SWA_W
    kj = lax.broadcasted_iota(jnp.int32, shape, 1)
    mask = (kj > qi) & (kj <= qi + SWA_W) & ((kj >= SWA_W) | jnp.logical_not(first))
    ss = [jnp.where(mask, _dot(q, kc, NT) * (SWA_HD ** -0.5), NEG) for q, kc in zip(qs, kcs)]
    ms = [jnp.maximum(jnp.max(s, axis=-1, keepdims=True), sink) for s, sink in zip(ss, sinks)]
    ps = [jnp.exp(s - m) for s, m in zip(ss, ms)]
    es = [jnp.exp(sink - m) for sink, m in zip(sinks, ms)]
    inv = [1.0 / (jnp.sum(p, axis=-1, keepdims=True) + e) for p, e in zip(ps, es)]
    return [p * i for p, i in zip(ps, inv)], [e * i for e, i in zip(es, inv)]


def _swa_stack(ref, h):
    return jnp.concatenate([ref[:, pl.ds((h * SWA_PAIRS + p) * LANE, LANE)] for p in range(SWA_PAIRS)], axis=0)


def _swa_unstack(ref, h, val):
    for p in range(SWA_PAIRS):
        ref[:, pl.ds((h * SWA_PAIRS + p) * LANE, LANE)] = val[p * SWA_W:(p + 1) * SWA_W]


def _swa_sink_col(sk_ref, h, second):
    pair = lax.broadcasted_iota(jnp.int32, (SWA_PAIRS * SWA_W, 1), 0) // SWA_W
    col = jnp.zeros((SWA_PAIRS * SWA_W, 1), F32)
    for p in range(SWA_PAIRS):
        hh = h * SWA_G + 2 * p + second
        col = jnp.where(pair == p, sk_ref[0:1, hh:hh + 1], col)
    return col


def _swa_kv_tiles(cur_ref, prev_ref, h):
    t = jnp.concatenate([prev_ref[...], cur_ref[...]], axis=0)
    lane = lax.broadcasted_iota(jnp.int32, t.shape, 1)
    moved = pltpu.roll(t, SWA_HD, axis=1)
    low, high = (t, moved) if h == 0 else (moved, t)
    return jnp.where(lane < SWA_HD, low, 0.0).astype(BF16), jnp.where(lane >= SWA_HD, high, 0.0).astype(BF16)


def _swa_kv_grad(g_low, g_high, h):
    lane = lax.broadcasted_iota(jnp.int32, g_low.shape, 1)
    if h == 0:
        return jnp.where(lane < SWA_HD, g_low + pltpu.roll(g_high, SWA_HD, axis=1), 0.0)
    return jnp.where(lane >= SWA_HD, pltpu.roll(g_low, SWA_HD, axis=1) + g_high, 0.0)


def _swa_specs(lay):
    w = SWA_W
    q_spec = pl.BlockSpec((w, Q_W), lambda n: (n, _div(lay.q, Q_W)))
    cur = lambda off: pl.BlockSpec((w, KV_W), lambda n: (n, _div(off, KV_W)))
    prev = lambda off: pl.BlockSpec((w, KV_W), lambda n: (jnp.maximum(n - 1, 0), _div(off, KV_W)))
    return q_spec, cur(lay.k), prev(lay.k), cur(lay.v), prev(lay.v)


def swa_fwd(p_all, sinks, lay, name):
    s = p_all.shape[0]
    nb = _div(s, SWA_W)

    def body(q_ref, kc_ref, kp_ref, vc_ref, vp_ref, sk_ref, o_ref):
        first = pl.program_id(0) == 0
        units = [(h, e) for h in range(SWA_HKV) for e in range(2)]
        ks = [_swa_kv_tiles(kc_ref, kp_ref, h) for h in range(SWA_HKV)]
        vs = [_swa_kv_tiles(vc_ref, vp_ref, h) for h in range(SWA_HKV)]
        qs = [_swa_stack(q_ref, h).astype(BF16) for h in range(SWA_HKV)]
        ps, _ = _swa_probs([qs[h] for h, e in units], [ks[h][e] for h, e in units],
                           [_swa_sink_col(sk_ref, h, e) for h, e in units], first)
        os = [_dot(p.astype(BF16), vs[h][e], NN) for p, (h, e) in zip(ps, units)]
        for h in range(SWA_HKV):
            _swa_unstack(o_ref, h, (os[2 * h] + os[2 * h + 1]).astype(BF16))

    q_spec, kc_s, kp_s, vc_s, vp_s = _swa_specs(lay)
    return pl.pallas_call(
        body, name=name, grid=(nb,),
        in_specs=[q_spec, kc_s, kp_s, vc_s, vp_s, pl.BlockSpec(sinks.shape, lambda n: (0, 0))],
        out_specs=pl.BlockSpec((SWA_W, Q_W), lambda n: (n, 0)), out_shape=SDS((s, Q_W), BF16),
        compiler_params=_cp(("parallel",)))(p_all, p_all, p_all, p_all, p_all, sinks)


def swa_bwd(p_all, sinks, dy, lay, name):
    s = p_all.shape[0]
    nb = _div(s, SWA_W)
    w = SWA_W

    def body(q_ref, kc_ref, kp_ref, vc_ref, vp_ref, sk_ref, do_ref, dq_ref, dk_ref, dv_ref, ds_ref, kcar, vcar):
        n = pl.program_id(0)
        first = n == 0

        @pl.when(first)
        def _():
            kcar[...] = jnp.zeros_like(kcar)
            vcar[...] = jnp.zeros_like(vcar)
            ds_ref[...] = jnp.zeros_like(ds_ref)

        @pl.when(n < nb)
        def _():
            lane = lax.broadcasted_iota(jnp.int32, (1, LANE), 1)
            dsink = jnp.zeros((1, LANE), F32)
            units = [(h, e) for h in range(SWA_HKV) for e in range(2)]
            ks = [_swa_kv_tiles(kc_ref, kp_ref, h) for h in range(SWA_HKV)]
            vs = [_swa_kv_tiles(vc_ref, vp_ref, h) for h in range(SWA_HKV)]
            qs = [_swa_stack(q_ref, h).astype(BF16) for h in range(SWA_HKV)]
            dos = [_swa_stack(do_ref, h).astype(BF16) for h in range(SWA_HKV)]
            ps, psinks = _swa_probs([qs[h] for h, e in units], [ks[h][e] for h, e in units],
                                    [_swa_sink_col(sk_ref, h, e) for h, e in units], first)
            dps = [_dot(dos[h], vs[h][e], NT) for h, e in units]
            dvs = [_dot(p.astype(BF16), dos[h], TN) for p, (h, e) in zip(ps, units)]
            rss = [jnp.sum(dp * p, axis=-1, keepdims=True) for dp, p in zip(dps, ps)]
            dsb = [(p * (dp - rs) * (SWA_HD ** -0.5)).astype(BF16) for p, dp, rs in zip(ps, dps, rss)]
            dqs = [_dot(d, ks[h][e], NN) for d, (h, e) in zip(dsb, units)]
            dks = [_dot(d, qs[h], TN) for d, (h, e) in zip(dsb, units)]
            for u, (h, e) in enumerate(units):
                psr = psinks[u] * rss[u]
                for pr in range(SWA_PAIRS):
                    hh = h * SWA_G + 2 * pr + e
                    dsink = dsink + jnp.where(lane == hh, -jnp.sum(psr[pr * w:(pr + 1) * w], axis=0, keepdims=True), 0.0)
            dk_tile = jnp.zeros((2 * w, KV_W), F32)
            dv_tile = jnp.zeros((2 * w, KV_W), F32)
            for h in range(SWA_HKV):
                _swa_unstack(dq_ref, h, (dqs[2 * h] + dqs[2 * h + 1]).astype(BF16))
                dk_tile = dk_tile + _swa_kv_grad(dks[2 * h], dks[2 * h + 1], h)
                dv_tile = dv_tile + _swa_kv_grad(dvs[2 * h], dvs[2 * h + 1], h)
            dk_ref[...] = (kcar[...] + dk_tile[:w]).astype(BF16)
            dv_ref[...] = (vcar[...] + dv_tile[:w]).astype(BF16)
            kcar[...] = dk_tile[w:]
            vcar[...] = dv_tile[w:]
            ds_ref[...] += dsink

        @pl.when(n == nb)
        def _():
            dk_ref[...] = kcar[...].astype(BF16)
            dv_ref[...] = vcar[...].astype(BF16)

    last = nb - 1
    q_spec = pl.BlockSpec((w, Q_W), lambda n: (jnp.minimum(n, last), _div(lay.q, Q_W)))
    cur = lambda off: pl.BlockSpec((w, KV_W), lambda n: (jnp.minimum(n, last), _div(off, KV_W)))
    prev = lambda off: pl.BlockSpec((w, KV_W), lambda n: (jnp.clip(n - 1, 0, last), _div(off, KV_W)))
    row = pl.BlockSpec((w, Q_W), lambda n: (jnp.minimum(n, last), 0))
    kv_out = pl.BlockSpec((w, KV_W), lambda n: (jnp.maximum(n - 1, 0), 0))
    return pl.pallas_call(
        body, name=name, grid=(nb + 1,),
        in_specs=[q_spec, cur(lay.k), prev(lay.k), cur(lay.v), prev(lay.v), pl.BlockSpec(sinks.shape, lambda n: (0, 0)), row],
        out_specs=[row, kv_out, kv_out, pl.BlockSpec((1, LANE), lambda n: (0, 0))],
        out_shape=[SDS((s, Q_W), BF16), SDS((s, KV_W), BF16), SDS((s, KV_W), BF16), SDS((1, LANE), F32)],
        scratch_shapes=[pltpu.VMEM((w, KV_W), F32), pltpu.VMEM((w, KV_W), F32)],
        compiler_params=_cp(("arbitrary",)))(p_all, p_all, p_all, p_all, p_all, sinks, dy)


def _xa_probs(q, mk):
    s = _dot(q, mk, NT) * (XA_D ** -0.5)
    p = jnp.exp(s - jnp.max(s, axis=-1, keepdims=True))
    return p / jnp.sum(p, axis=-1, keepdims=True)


def xattn_fwd(p_all, mkv, lay, name):
    s, nm = p_all.shape[0], mkv.shape[0]
    tm = _tile(s, 512)

    def body(q_ref, mkv_ref, o_ref):
        for h in range(XA_H):
            cols = pl.ds(h * XA_D, XA_D)
            p = _xa_probs(q_ref[:, cols].astype(BF16), mkv_ref[:, cols])
            o_ref[:, cols] = _dot(p.astype(BF16), mkv_ref[:, pl.ds(XA_W + h * XA_D, XA_D)], NN).astype(BF16)

    return pl.pallas_call(
        body, name=name, grid=(s // tm,),
        in_specs=[pl.BlockSpec((tm, XA_W), lambda i: (i, _div(lay.qc, XA_W))), pl.BlockSpec((nm, 2 * XA_W), lambda i: (0, 0))],
        out_specs=pl.BlockSpec((tm, XA_W), lambda i: (i, 0)), out_shape=SDS((s, XA_W), BF16),
        compiler_params=_cp(("parallel",)))(p_all, mkv)


def xattn_bwd(p_all, mkv, dy, lay, name):
    s, nm = p_all.shape[0], mkv.shape[0]
    tm = _tile(s, 512)

    def body(q_ref, mkv_ref, do_ref, dq_ref, dmkv_ref):
        @pl.when(pl.program_id(0) == 0)
        def _():
            dmkv_ref[...] = jnp.zeros_like(dmkv_ref)

        for h in range(XA_H):
            cols = pl.ds(h * XA_D, XA_D)
            vcols = pl.ds(XA_W + h * XA_D, XA_D)
            q = q_ref[:, cols].astype(BF16)
            do = do_ref[:, cols].astype(BF16)
            p = _xa_probs(q, mkv_ref[:, cols])
            dp = _dot(do, mkv_ref[:, vcols], NT)
            dmkv_ref[:, vcols] += _dot(p.astype(BF16), do, TN)
            dsb = (p * (dp - jnp.sum(dp * p, axis=-1, keepdims=True)) * (XA_D ** -0.5)).astype(BF16)
            dq_ref[:, cols] = _dot(dsb, mkv_ref[:, cols], NN).astype(BF16)
            dmkv_ref[:, cols] += _dot(dsb, q, TN)

    row = pl.BlockSpec((tm, XA_W), lambda i: (i, 0))
    full = pl.BlockSpec((nm, 2 * XA_W), lambda i: (0, 0))
    return pl.pallas_call(
        body, name=name, grid=(s // tm,),
        in_specs=[pl.BlockSpec((tm, XA_W), lambda i: (i, _div(lay.qc, XA_W))), full, row],
        out_specs=[row, full], out_shape=[SDS((s, XA_W), BF16), SDS((nm, 2 * XA_W), F32)],
        compiler_params=_cp(("arbitrary",)))(p_all, mkv, dy)


def _shift_down(cur, prev8, s):
    cat = jnp.concatenate([prev8, cur[0:8]], axis=0)
    return pltpu.roll(cur, s, axis=0), pltpu.roll(cat, s, axis=0)[8:16]


def _shift_up(cur, next8, s):
    tm = cur.shape[0]
    cat = jnp.concatenate([cur[tm - 8:tm], next8], axis=0)
    return pltpu.roll(cur, tm - s, axis=0), pltpu.roll(cat, 16 - s, axis=0)[0:8]


def gdn_conv_fwd(p_all, conv_w, lay, name):
    s = p_all.shape[0]
    tm = _tile(s, 512)
    c0 = _div(lay.qkv, GDN_W)

    def body(x_ref, prev_ref, w_ref, o_ref):
        cur = x_ref[...]
        prev8 = jnp.where(pl.program_id(1) > 0, prev_ref[...], 0.0)
        main = w_ref[GDN_CONV - 1:GDN_CONV, :] * cur
        top = w_ref[GDN_CONV - 1:GDN_CONV, :] * cur[0:8]
        for sft in range(1, GDN_CONV):
            wi = w_ref[GDN_CONV - 1 - sft:GDN_CONV - sft, :]
            a, b = _shift_down(cur, prev8, sft)
            main = main + wi * a
            top = top + wi * b
        o_ref[...] = main
        o_ref[0:8, :] = top

    return pl.pallas_call(
        body, name=name, grid=(3, s // tm),
        in_specs=[pl.BlockSpec((tm, GDN_W), lambda c, i: (i, c0 + c)),
                  pl.BlockSpec((8, GDN_W), lambda c, i: (jnp.maximum(i * (tm // 8) - 1, 0), c0 + c)),
                  pl.BlockSpec((GDN_CONV, GDN_W), lambda c, i: (0, c))],
        out_specs=pl.BlockSpec((tm, GDN_W), lambda c, i: (i, c)), out_shape=SDS((s, 3 * GDN_W), F32),
        compiler_params=_cp(("parallel", "parallel")))(p_all, p_all, conv_w)


def gdn_conv_bwd(p_all, conv_w, dxc, lay, name):
    s = p_all.shape[0]
    tm = _tile(s, 512)
    c0 = _div(lay.qkv, GDN_W)
    nt = s // tm

    def body(x_ref, prev_ref, d_ref, next_ref, w_ref, dx_ref, dw_ref):
        i = pl.program_id(1)
        cur, d = x_ref[...], d_ref[...]
        prev8 = jnp.where(i > 0, prev_ref[...], 0.0)
        next8 = jnp.where(i < nt - 1, next_ref[...], 0.0)
        row = lax.broadcasted_iota(jnp.int32, (tm, 1), 0)
        main = w_ref[GDN_CONV - 1:GDN_CONV, :] * d
        bot = w_ref[GDN_CONV - 1:GDN_CONV, :] * d[tm - 8:tm]
        dws = [jnp.sum(d * cur, axis=0, keepdims=True)]
        for sft in range(1, GDN_CONV):
            wi = w_ref[GDN_CONV - 1 - sft:GDN_CONV - sft, :]
            a, b = _shift_up(d, next8, sft)
            main = main + wi * a
            bot = bot + wi * b
            xa, xb = _shift_down(cur, prev8, sft)
            dws.append(jnp.sum(jnp.where(row >= 8, d * xa, 0.0), axis=0, keepdims=True)
                       + jnp.sum(d[0:8] * xb, axis=0, keepdims=True))
        dx_ref[...] = main.astype(BF16)
        dx_ref[tm - 8:tm, :] = bot.astype(BF16)

        @pl.when(i == 0)
        def _():
            dw_ref[...] = jnp.zeros_like(dw_ref)

        for sft in range(GDN_CONV):
            dw_ref[GDN_CONV - 1 - sft:GDN_CONV - sft, :] += dws[sft]

    return pl.pallas_call(
        body, name=name, grid=(3, nt),
        in_specs=[pl.BlockSpec((tm, GDN_W), lambda c, i: (i, c0 + c)),
                  pl.BlockSpec((8, GDN_W), lambda c, i: (jnp.maximum(i * (tm // 8) - 1, 0), c0 + c)),
                  pl.BlockSpec((tm, GDN_W), lambda c, i: (i, c)),
                  pl.BlockSpec((8, GDN_W), lambda c, i: (jnp.minimum((i + 1) * (tm // 8), s // 8 - 1), c)),
                  pl.BlockSpec((GDN_CONV, GDN_W), lambda c, i: (0, c))],
        out_specs=[pl.BlockSpec((tm, GDN_W), lambda c, i: (i, c)), pl.BlockSpec((GDN_CONV, GDN_W), lambda c, i: (0, c))],
        out_shape=[SDS((s, 3 * GDN_W), BF16), SDS((GDN_CONV, 3 * GDN_W), F32)],
        compiler_params=_cp(("parallel", "arbitrary")))(p_all, p_all, dxc, dxc, conv_w)


def _gdn_chunk(xq, xk, xv, ab, gp, bdot=_bdot_plain):
    c = GDN_C
    nc = xq.shape[0] // c
    lane = lax.broadcasted_iota(jnp.int32, (c, LANE), 1)
    row = lax.broadcasted_iota(jnp.int32, (c, c), 0)
    col = lax.broadcasted_iota(jnp.int32, (c, c), 1)
    g_tile = -jnp.exp(gp[0:1, :]) * _softplus(ab + gp[1:2, :])
    b_tile = _sigmoid(ab)
    tri = (row >= col).astype(F32)
    qa, ka, va = _silu(xq), _silu(xk), _silu(xv)
    items = []
    for ci in range(nc):
        rs = slice(ci * c, (ci + 1) * c)
        gcum = _dot(tri, g_tile[rs], NN, HI)
        gcum_t = gcum.T
        for h in range(GDN_H):
            hs = slice(h * GDN_D, (h + 1) * GDN_D)
            q, k, v = qa[rs, hs], ka[rs, hs], va[rs, hs]
            q = q * lax.rsqrt(jnp.sum(q * q, axis=-1, keepdims=True) + L2_EPS) * (GDN_D ** -0.5)
            k = k * lax.rsqrt(jnp.sum(k * k, axis=-1, keepdims=True) + L2_EPS)
            gc = jnp.sum(jnp.where(lane == h, gcum, 0.0), axis=1, keepdims=True)
            beta = jnp.sum(jnp.where(lane == GDN_H + h, b_tile[rs], 0.0), axis=1, keepdims=True)
            decay = jnp.exp(jnp.where(row >= col, gc - gcum_t[h:h + 1, :], NEG))
            items.append((q, k, v, gc, beta, decay))
    kks = [bdot(k, k, NT) for (_, k, _, _, _, _) in items]
    xs = tuple(-jnp.where(row > col, it[4] * kk * it[5], 0.0) for it, kk in zip(items, kks))
    nns = _neumann(xs) if bdot is _bdot_plain else _neumann_vjp(xs)
    qks = [bdot(q, k, NT) for (q, k, _, _, _, _) in items]
    out = []
    for (q, k, v, gc, beta, decay), n, qk in zip(items, nns, qks):
        eg = jnp.exp(gc)
        vb = v * beta
        kbe = k * (beta * eg)
        gl = gc[c - 1:c, :]
        out.append((vb + bdot(n, vb, NN), kbe + bdot(n, kbe, NN), q * eg, k * jnp.exp(gl - gc), qk * decay, jnp.exp(gl)))
    return [out[ci * GDN_H:(ci + 1) * GDN_H] for ci in range(nc)]


GDN_CPS = 4


def _gdn_pre_specs(lay, t):
    xspec = lambda j: pl.BlockSpec((t, GDN_W), lambda n, j=j: (n, j))
    return [xspec(0), xspec(1), xspec(2), pl.BlockSpec((t, LANE), lambda n: (n, _div(lay.ab, LANE))),
            pl.BlockSpec((8, LANE), lambda n: (0, 0))]


def gdn_pre_fwd(xc, p_all, gp, lay, name):
    s = xc.shape[0]
    c = GDN_C
    n = _div(s, c)
    cps = _tile(n, GDN_CPS)
    t = cps * c

    def body(xq, xk, xv, ab, gp_ref, u_ref, w_ref, qd_ref, kd_ref, qk_ref, gl_ref):
        lane = lax.broadcasted_iota(jnp.int32, (1, LANE), 1)
        chunks = _gdn_chunk(xq[...], xk[...], xv[...], ab[...], gp_ref[...])
        for ci, heads in enumerate(chunks):
            rs = pl.ds(ci * c, c)
            gl_row = jnp.zeros((1, LANE), F32)
            for h, (u, w, qd, kd, qk, gl) in enumerate(heads):
                hs = pl.ds(h * GDN_D, GDN_D)
                u_ref[rs, hs] = u
                w_ref[rs, hs] = w.astype(BF16)
                qd_ref[rs, hs] = qd.astype(BF16)
                kd_ref[rs, hs] = kd.astype(BF16)
                qk_ref[rs, pl.ds(h * c, c)] = qk.astype(BF16)
                gl_row = gl_row + jnp.where(lane == h, gl, 0.0)
            gl_ref[ci] = gl_row

    row = pl.BlockSpec((t, GDN_W), lambda n: (n, 0))
    return pl.pallas_call(
        body, name=name, grid=(n // cps,), in_specs=_gdn_pre_specs(lay, t),
        out_specs=[row, row, row, row, pl.BlockSpec((t, GDN_H * c), lambda n: (n, 0)), pl.BlockSpec((cps, 1, LANE), lambda n: (n, 0, 0))],
        out_shape=[SDS((s, GDN_W), F32), SDS((s, GDN_W), BF16), SDS((s, GDN_W), BF16), SDS((s, GDN_W), BF16),
                   SDS((s, GDN_H * c), BF16), SDS((n, 1, LANE), F32)],
        compiler_params=_cp(("parallel",)))(xc, xc, xc, p_all, gp)


def gdn_pre_bwd(xc, p_all, gp, du, dw, dqd, dkd, dqk, dgl, lay, name):
    s = xc.shape[0]
    c = GDN_C
    n = _div(s, c)
    cps = _tile(n, GDN_CPS)
    t = cps * c
    chunk = functools.partial(_gdn_chunk, bdot=_bdot_vjp)

    def body(xq, xk, xv, ab, gp_ref, du_r, dw_r, dqd_r, dkd_r, dqk_r, dgl_r, dxc_ref, dab_ref, dgp_ref):
        lane = lax.broadcasted_iota(jnp.int32, (1, LANE), 1)
        _, vjp = jax.vjp(chunk, xq[...], xk[...], xv[...], ab[...], gp_ref[...])
        cts = []
        for ci in range(cps):
            rs = pl.ds(ci * c, c)
            heads = []
            for h in range(GDN_H):
                hs = pl.ds(h * GDN_D, GDN_D)
                dgl_h = jnp.sum(jnp.where(lane == h, dgl_r[ci], 0.0), axis=1, keepdims=True)
                heads.append((du_r[rs, hs], dw_r[rs, hs], dqd_r[rs, hs], dkd_r[rs, hs], dqk_r[rs, pl.ds(h * c, c)], dgl_h))
            cts.append(heads)
        dq, dk, dv, dab, dgp = vjp(cts)
        dxc_ref[:, pl.ds(0, GDN_W)] = dq
        dxc_ref[:, pl.ds(GDN_W, GDN_W)] = dk
        dxc_ref[:, pl.ds(2 * GDN_W, GDN_W)] = dv
        dab_ref[...] = dab.astype(BF16)

        @pl.when(pl.program_id(0) == 0)
        def _():
            dgp_ref[...] = jnp.zeros_like(dgp_ref)

        dgp_ref[...] += dgp

    row = pl.BlockSpec((t, GDN_W), lambda n: (n, 0))
    return pl.pallas_call(
        body, name=name, grid=(n // cps,),
        in_specs=_gdn_pre_specs(lay, t) + [row, row, row, row, pl.BlockSpec((t, GDN_H * c), lambda n: (n, 0)),
                                           pl.BlockSpec((cps, 1, LANE), lambda n: (n, 0, 0))],
        out_specs=[pl.BlockSpec((t, 3 * GDN_W), lambda n: (n, 0)), pl.BlockSpec((t, LANE), lambda n: (n, 0)),
                   pl.BlockSpec((8, LANE), lambda n: (0, 0))],
        out_shape=[SDS((s, 3 * GDN_W), F32), SDS((s, LANE), BF16), SDS((8, LANE), F32)],
        compiler_params=_cp(("arbitrary",)))(xc, xc, xc, p_all, gp, du, dw, dqd, dkd, dqk, dgl)


def _lane_scalar(row, h):
    lane = lax.broadcasted_iota(jnp.int32, row.shape, 1)
    return jnp.sum(jnp.where(lane == h, row, 0.0), axis=1, keepdims=True)


def gdn_scan_fwd(u, w, qd, kd, qk, gl, name):
    s = u.shape[0]
    c = GDN_C
    n = _div(s, c)
    cps = _tile(n, GDN_CPS)
    t = cps * c

    def body(u_r, w_r, qd_r, kd_r, qk_r, gl_r, o_ref, s_ref, st):
        @pl.when(pl.program_id(0) == 0)
        def _():
            st[...] = jnp.zeros_like(st)

        heads = range(GDN_H)
        hs = [pl.ds(h * GDN_D, GDN_D) for h in heads]
        for ci in range(cps):
            rs = pl.ds(ci * c, c)
            s_ref[ci] = st[...]
            sh = [st[hs[h], :] for h in heads]
            shb = [x.astype(BF16) for x in sh]
            ws = [_dot(w_r[rs, hs[h]], shb[h], NN) for h in heads]
            qs = [_dot(qd_r[rs, hs[h]], shb[h], NN) for h in heads]
            vb = [(u_r[rs, hs[h]] - ws[h]).astype(BF16) for h in heads]
            ov = [_dot(qk_r[rs, pl.ds(h * c, c)], vb[h], NN) for h in heads]
            kv = [_dot(kd_r[rs, hs[h]], vb[h], TN) for h in heads]
            for h in heads:
                o_ref[rs, hs[h]] = qs[h] + ov[h]
                st[hs[h], :] = sh[h] * _lane_scalar(gl_r[ci], h) + kv[h]

    row = pl.BlockSpec((t, GDN_W), lambda i: (i, 0))
    return pl.pallas_call(
        body, name=name, grid=(n // cps,),
        in_specs=[row, row, row, row, pl.BlockSpec((t, GDN_H * c), lambda i: (i, 0)), pl.BlockSpec((cps, 1, LANE), lambda i: (i, 0, 0))],
        out_specs=[row, pl.BlockSpec((cps, GDN_W, GDN_D), lambda i: (i, 0, 0))],
        out_shape=[SDS((s, GDN_W), F32), SDS((n, GDN_W, GDN_D), F32)],
        scratch_shapes=[pltpu.VMEM((GDN_W, GDN_D), F32)],
        compiler_params=_cp(("arbitrary",)))(u, w, qd, kd, qk, gl)


def gdn_scan_bwd(u, w, qd, kd, qk, gl, states, do, name):
    s = u.shape[0]
    c = GDN_C
    n = _div(s, c)
    cps = _tile(n, GDN_CPS)
    t = cps * c
    steps = n // cps

    def body(u_r, w_r, qd_r, kd_r, qk_r, gl_r, s_r, do_r, du_o, dw_o, dqd_o, dkd_o, dqk_o, dgl_o, dst):
        @pl.when(pl.program_id(0) == 0)
        def _():
            dst[...] = jnp.zeros_like(dst)

        lane = lax.broadcasted_iota(jnp.int32, (1, LANE), 1)
        heads = range(GDN_H)
        hs = [pl.ds(h * GDN_D, GDN_D) for h in heads]
        qs = [pl.ds(h * c, c) for h in heads]
        for ci in reversed(range(cps)):
            rs = pl.ds(ci * c, c)
            sh = [s_r[ci, hs[h], :] for h in heads]
            shb = [x.astype(BF16) for x in sh]
            ds_out = [dst[hs[h], :] for h in heads]
            dsb = [x.astype(BF16) for x in ds_out]
            dob = [do_r[rs, hs[h]].astype(BF16) for h in heads]
            ws = [_dot(w_r[rs, hs[h]], shb[h], NN) for h in heads]
            dv1 = [_dot(qk_r[rs, qs[h]], dob[h], TN) for h in heads]
            dv2 = [_dot(kd_r[rs, hs[h]], dsb[h], NN) for h in heads]
            dqd = [_dot(dob[h], shb[h], NT) for h in heads]
            dsq = [_dot(qd_r[rs, hs[h]], dob[h], TN) for h in heads]
            vb = [(u_r[rs, hs[h]] - ws[h]).astype(BF16) for h in heads]
            dv = [dv1[h] + dv2[h] for h in heads]
            dvb = [x.astype(BF16) for x in dv]
            dw = [_dot(dvb[h], shb[h], NT) for h in heads]
            dkd = [_dot(vb[h], dsb[h], NT) for h in heads]
            dqk = [_dot(dob[h], vb[h], NT) for h in heads]
            dsw = [_dot(w_r[rs, hs[h]], dvb[h], TN) for h in heads]
            dgl_row = jnp.zeros((1, LANE), F32)
            for h in heads:
                du_o[rs, hs[h]] = dv[h]
                dw_o[rs, hs[h]] = -dw[h]
                dqd_o[rs, hs[h]] = dqd[h]
                dkd_o[rs, hs[h]] = dkd[h]
                dqk_o[rs, qs[h]] = dqk[h]
                dgl_row = dgl_row + jnp.where(lane == h, jnp.sum(jnp.sum(ds_out[h] * sh[h], axis=1, keepdims=True), axis=0, keepdims=True), 0.0)
                dst[hs[h], :] = ds_out[h] * _lane_scalar(gl_r[ci], h) + dsq[h] - dsw[h]
            dgl_o[ci] = dgl_row

    rev = lambda i: steps - 1 - i
    row = pl.BlockSpec((t, GDN_W), lambda i: (rev(i), 0))
    qks = pl.BlockSpec((t, GDN_H * c), lambda i: (rev(i), 0))
    gls = pl.BlockSpec((cps, 1, LANE), lambda i: (rev(i), 0, 0))
    return pl.pallas_call(
        body, name=name, grid=(steps,),
        in_specs=[row, row, row, row, qks, gls, pl.BlockSpec((cps, GDN_W, GDN_D), lambda i: (rev(i), 0, 0)), row],
        out_specs=[row, row, row, row, qks, gls],
        out_shape=[SDS((s, GDN_W), F32)] * 4 + [SDS((s, GDN_H * c), F32), SDS((n, 1, LANE), F32)],
        scratch_shapes=[pltpu.VMEM((GDN_W, GDN_D), F32)],
        compiler_params=_cp(("arbitrary",)))(u, w, qd, kd, qk, gl, states, do)


def _gdn_out_rows(o, z, nw):
    outs = []
    for h in range(GDN_H):
        hs = slice(h * GDN_D, (h + 1) * GDN_D)
        oh = o[:, hs]
        y = oh * lax.rsqrt(jnp.mean(oh * oh, axis=-1, keepdims=True) + RMS_EPS) * nw
        outs.append(y * _silu(z[:, hs]))
    return jnp.concatenate(outs, axis=1)


def gdn_out(o, p_all, nw, dy, lay, name):
    s = o.shape[0]
    tm = _tile(s, 512)
    bwd = dy is not None

    def body(*refs):
        o_r, z_r, nw_r = refs[:3]
        if not bwd:
            refs[3][...] = _gdn_out_rows(o_r[...], z_r[...], nw_r[...]).astype(BF16)
            return
        dy_r, do_o, dz_o, dnw_o = refs[3:]
        _, vjp = jax.vjp(_gdn_out_rows, o_r[...], z_r[...], nw_r[...])
        d_o, d_z, d_nw = vjp(dy_r[...].astype(F32))
        do_o[...] = d_o
        dz_o[...] = d_z.astype(BF16)

        @pl.when(pl.program_id(0) == 0)
        def _():
            dnw_o[...] = jnp.zeros_like(dnw_o)

        dnw_o[...] += d_nw

    row = pl.BlockSpec((tm, GDN_W), lambda i: (i, 0))
    zs = pl.BlockSpec((tm, GDN_W), lambda i: (i, _div(lay.z, GDN_W)))
    nws = pl.BlockSpec((1, GDN_D), lambda i: (0, 0))
    if not bwd:
        return pl.pallas_call(body, name=name, grid=(s // tm,), in_specs=[row, zs, nws], out_specs=row,
                              out_shape=SDS((s, GDN_W), BF16), compiler_params=_cp(("parallel",)))(o, p_all, nw)
    return pl.pallas_call(body, name=name, grid=(s // tm,), in_specs=[row, zs, nws, row], out_specs=[row, row, nws],
                          out_shape=[SDS((s, GDN_W), F32), SDS((s, GDN_W), BF16), SDS((1, GDN_D), F32)],
                          compiler_params=_cp(("arbitrary",)))(o, p_all, nw, dy)


def _cols_to_full(g):
    n, k, c = g.shape
    return g.transpose(1, 0, 2).reshape(k, n * c)


def _rows_to_blocks(w):
    return w.reshape(N_DEV, w.shape[0] // N_DEV, w.shape[1])


def _pack_small(parts, rows):
    flat = jnp.concatenate([jnp.pad(p.reshape(-1), (0, -p.size % LANE)) for p in parts])
    return jnp.pad(flat, (0, rows * LANE - flat.size)).reshape(rows, LANE)


def kernel(x, mem, g_mix, w_in, sinks, conv_w, a_log, dt_bias, gdn_norm_w, g_mem, w_mem_kv, w_swa_up, w_gdn_up, w_xa_up, w_out, g_mlp, w_mlp_in, w_mlp_out, g_final, loss_target, m_g_mix, m_w_in, m_sinks, m_conv_w, m_a_log, m_dt_bias, m_gdn_norm_w, m_g_mem, m_w_mem_kv, m_w_swa_up, m_w_gdn_up, m_w_xa_up, m_w_out, m_g_mlp, m_w_mlp_in, m_w_mlp_out, m_g_final, v_g_mix, v_w_in, v_sinks, v_conv_w, v_a_log, v_dt_bias, v_gdn_norm_w, v_g_mem, v_w_mem_kv, v_w_swa_up, v_w_gdn_up, v_w_xa_up, v_w_out, v_g_mlp, v_w_mlp_in, v_w_mlp_out, v_g_final):
    xs, ms, tgt = x[0], mem[0], loss_target[0]
    s, d = xs.shape
    lay = Layout(d)
    px, py, pc = _position()
    dev = 4 * px + 2 * py + pc

    g_in, g_conv = run_job(GatherJob([w_in[0].astype(BF16), conv_w[0]]), "gather_w_in")
    W_in = pad_w_in(g_in, lay)
    convw = _cols_to_full(g_conv)
    gp = jnp.zeros((8, LANE), F32).at[0, :GDN_H].set(a_log[0]).at[1, :GDN_H].set(dt_bias[0])
    later = [w_mem_kv[0], w_swa_up[0], w_gdn_up[0], w_xa_up[0], w_out[0], w_mlp_in[0]]

    n1 = rmsnorm_fwd(xs, g_mix, "norm_mix")
    p_all, g_mkv, W_sup, W_gup, W_xup, g_out, W_m1 = matmul(
        n1, W_in, mode="nn", out_dtype=F32, name="proj_in", tm=2048, tn=512, tk=d,
        side=GatherJob([w.astype(BF16) for w in later]))
    W_mkv = g_mkv.reshape(-1, g_mkv.shape[2])
    W_out = g_out.reshape(-1, d)
    y_a = swa_fwd(p_all, sinks, lay, "swa_fwd")
    xc = gdn_conv_fwd(p_all, convw, lay, "gdn_conv_fwd")
    u, gw, gqd, gkd, gqk, ggl = gdn_pre_fwd(xc, p_all, gp, lay, "gdn_pre_fwd")
    o_b, states = gdn_scan_fwd(u, gw, gqd, gkd, gqk, ggl, "gdn_scan_fwd")
    y_b = gdn_out(o_b, p_all, gdn_norm_w, None, lay, "gdn_out_fwd")
    nm = rmsnorm_fwd(ms, g_mem, "norm_mem")
    mkv = matmul(nm, W_mkv, mode="nn", out_dtype=BF16, name="proj_mem", tk=d)
    y_c = xattn_fwd(p_all, mkv, lay, "xattn_fwd")
    merged = merge(p_all, y_a, y_b, y_c, W_sup, W_gup, W_xup, None, lay, "merge_fwd")
    h1 = matmul(merged, W_out, mode="nn", out_dtype=F32, name="proj_out", tm=2048, tn=512, tk=d, resid=xs)
    n2 = rmsnorm_fwd(h1, g_mlp, "norm_mlp")
    uu, act, g_m2 = matmul(n2, W_m1, mode="nn", out_dtype=F32, name="mlp_in", tm=2048, tn=512, tk=d, b_cols=True,
                           relu2_out=True, side=GatherJob([w_mlp_out[0].astype(BF16)]))
    W_m2 = g_m2.reshape(-1, d)
    h2 = matmul(act, W_m2, mode="nn", out_dtype=F32, name="mlp_out", tm=1024, tn=2048, tk=512, resid=h1)
    dh2, dg_final, lrow, dh2_b = final_norm_loss(h2, g_final.reshape(1, d), tgt, "final_loss")
    loss = lax.psum(lrow[0, 0], ("x", "y", "c"))

    du = matmul(dh2_b, W_m2, mode="nt", out_dtype=BF16, name="mlp_out_dx", tm=2048, tn=512, tk=d, relu2_grad_of=uu)
    dW_m2 = matmul(act, dh2_b, mode="tn", out_dtype=BF16, name="mlp_out_dw", tm=1024, tn=2048, tk=1024)
    dW_m2 = _rows_to_blocks(dW_m2)
    dn2, sib_m2 = matmul(du, W_m1, mode="nt", out_dtype=F32, name="mlp_in_dx", tm=1024, tn=2048, tk=1024, b_cols=True,
                         side=PairExchangeJob([dW_m2], [False]))
    c_m2 = pair_add(dW_m2, False, sib_m2, "grads_pair_add_m2")
    dW_m1 = matmul(n2, du, mode="tn", out_dtype=BF16, name="mlp_in_dw", tm=2048, tn=1024, tk=1024)
    dh1, dg_mlp, dh1_b = rmsnorm_bwd(h1, g_mlp, dn2, dh2, "norm_mlp_bwd", bf16_copy=True)

    dmerged, sib_m1 = matmul(dh1_b, W_out, mode="nt", out_dtype=F32, name="proj_out_dx", tm=2048, tn=512, tk=d,
                             side=PairExchangeJob([dW_m1], [True]))
    c_m1 = pair_add(dW_m1, True, sib_m1, "grads_pair_add_m1")
    dW_out = matmul(merged, dh1_b, mode="tn", out_dtype=BF16, name="proj_out_dw", tm=2048, tn=1024, tk=1024)
    dgates, dta, dtb, dtc = merge(p_all, y_a, y_b, y_c, W_sup, W_gup, W_xup, dmerged, lay, "merge_bwd")
    dy_a = matmul(dta, W_sup, mode="nt", out_dtype=BF16, name="swa_up_dx", tm=2048, tk=d, b_cols=True)
    dy_b = matmul(dtb, W_gup, mode="nt", out_dtype=BF16, name="gdn_up_dx", tm=2048, tk=d, b_cols=True)
    dy_c = matmul(dtc, W_xup, mode="nt", out_dtype=BF16, name="xa_up_dx", tm=2048, tk=d, b_cols=True)
    dW_sup = matmul(y_a, dta, mode="tn", out_dtype=BF16, name="swa_up_dw", tn=2048)
    dW_gup = matmul(y_b, dtb, mode="tn", out_dtype=BF16, name="gdn_up_dw", tn=2048)
    dW_xup = matmul(y_c, dtc, mode="tn", out_dtype=BF16, name="xa_up_dw", tn=2048)

    dq_a, dk_a, dv_a, dsinks = swa_bwd(p_all, sinks, dy_a, lay, "swa_bwd")
    dq_c, dmkv = xattn_bwd(p_all, mkv, dy_c, lay, "xattn_bwd")
    dW_mkv = matmul(nm, dmkv, mode="tn", out_dtype=BF16, name="proj_mem_dw", tk=256)
    dnm = matmul(dmkv, W_mkv, mode="nt", out_dtype=F32, name="proj_mem_dx", tk=1024)
    _, dg_mem = rmsnorm_bwd(ms, g_mem, dnm, None, "norm_mem_bwd")

    do_b, dz, dnorm_w = gdn_out(o_b, p_all, gdn_norm_w, dy_b, lay, "gdn_out_bwd")
    du_g, dw_g, dqd_g, dkd_g, dqk_g, dgl_g = gdn_scan_bwd(u, gw, gqd, gkd, gqk, ggl, states, do_b, "gdn_scan_bwd")
    dxc, dab, dgp = gdn_pre_bwd(xc, p_all, gp, du_g, dw_g, dqd_g, dkd_g, dqk_g, dgl_g, lay, "gdn_pre_bwd")
    dqkv, dconv = gdn_conv_bwd(p_all, convw, dxc, lay, "gdn_conv_bwd")

    drest = jnp.concatenate([dq_a, dqkv, dz, dq_c, dk_a, dv_a, dab, jnp.zeros((s, lay.pw - lay.end), BF16)], axis=1)
    def pair_stage(grads, cols, tag):
        from_sib = run_job(PairExchangeJob(grads, cols), "grads_pair_exchange_" + tag)
        return [pair_add(g, cl, o, "grads_pair_add_%s%d" % (tag, i)) for i, (g, cl, o) in enumerate(zip(grads, cols, from_sib))]

    small = pair_stage([_rows_to_blocks(dW_mkv), dW_sup, dW_gup, dW_xup, _rows_to_blocks(dW_out)],
                       [False, True, True, True, False], "a")
    dW_in, p_m1, p_m2 = matmul(n1, dgates, tail=drest, mode="tn", out_dtype=BF16, name="proj_in_dw", tm=2048, tn=1024, tk=1024,
                               side=ChipExchangeJob([c_m1, c_m2]))
    late = pair_stage([unpad_dw_in(dW_in, lay)], [False], "b")
    dn1, p_in, p_mkv, p_sup, p_gup, p_xup, p_out = matmul(
        dgates, W_in, tail=drest, mode="nt", out_dtype=F32, name="proj_in_dx", tm=1024, tn=2048, tk=1024,
        side=ChipExchangeJob(late + small))
    grad_x, dg_mix = rmsnorm_bwd(xs, g_mix, dn1, dh1, "norm_mix_bwd")
    parts = [p_in, p_mkv, p_sup, p_gup, p_xup, p_out, p_m1, p_m2]

    shard_names = [(w_in, m_w_in, v_w_in), (w_mem_kv, m_w_mem_kv, v_w_mem_kv), (w_swa_up, m_w_swa_up, v_w_swa_up),
                   (w_gdn_up, m_w_gdn_up, v_w_gdn_up), (w_xa_up, m_w_xa_up, v_w_xa_up), (w_out, m_w_out, v_w_out),
                   (w_mlp_in, m_w_mlp_in, v_w_mlp_in), (w_mlp_out, m_w_mlp_out, v_w_mlp_out)]
    big_res = [adamw(p, w[0], m[0], v[0], "adamw_%d" % i) for i, (p, (w, m, v)) in enumerate(zip(parts, shard_names))]

    smalls = [(g_mix, m_g_mix, v_g_mix, dg_mix), (sinks, m_sinks, v_sinks, dsinks[:, :SWA_HQ]),
              (a_log, m_a_log, v_a_log, dgp[0:1, :GDN_H]), (dt_bias, m_dt_bias, v_dt_bias, dgp[1:2, :GDN_H]),
              (gdn_norm_w, m_gdn_norm_w, v_gdn_norm_w, dnorm_w), (g_mem, m_g_mem, v_g_mem, dg_mem),
              (g_mlp, m_g_mlp, v_g_mlp, dg_mlp), (g_final, m_g_final, v_g_final, dg_final)]
    sizes = [-(-t[0].size // LANE) * LANE for t in smalls] + [GDN_CONV * 3 * GDN_W]
    rows = -(-sum(sizes) // (8 * LANE)) * 8
    csh = conv_w.shape[2]

    def conv_place(a):
        full = jnp.tile(a[0], (1, N_DEV))
        owner = lax.broadcasted_iota(jnp.int32, full.shape, 1) // csh
        return jnp.where(owner == dev, full, 0.0)

    g_pack = _pack_small([t[3] for t in smalls] + [dconv], rows)
    w_pack = _pack_small([t[0] for t in smalls] + [conv_place(conv_w)], rows)
    m_pack = _pack_small([t[1] for t in smalls] + [conv_place(m_conv_w)], rows)
    v_pack = _pack_small([t[2] for t in smalls] + [conv_place(v_conv_w)], rows)
    g_all = run_job(GatherJob([g_pack]), "gather_small_grads")[0]
    small_res = adamw(g_all, w_pack, m_pack, v_pack, "adamw_small")

    def unpack(arr):
        flat = arr.reshape(-1)
        outs, off = [], 0
        for t, sz in zip(smalls, sizes[:-1]):
            outs.append(flat[off:off + t[0].size].reshape(t[0].shape))
            off += sz
        cw = flat[off:off + sizes[-1]].reshape(GDN_CONV, 3 * GDN_W)
        mine = (lax.broadcasted_iota(jnp.int32, (1, N_DEV, 1), 1) == dev).astype(F32)
        outs.append(jnp.sum(cw.reshape(GDN_CONV, N_DEV, csh) * mine, axis=1)[None])
        return outs

    sg, sd, sm, sv = (unpack(a) for a in small_res)
    bg, bd, bm, bv = ([r[i][None] for r in big_res] for i in range(4))

    def ordered(sm_, bg_):
        return [sm_[0], bg_[0], sm_[1], sm_[8], sm_[2], sm_[3], sm_[4], sm_[5], bg_[1], bg_[2], bg_[3], bg_[4], bg_[5],
                sm_[6], bg_[6], bg_[7], sm_[7]]

    return (loss, grad_x[None], *ordered(sg, bg), *ordered(sd, bd), *ordered(sm, bm), *ordered(sv, bv))
```

```python
import functools
import math

import jax
import jax.numpy as jnp
from jax import lax
from jax.experimental import pallas as pl
from jax.experimental.pallas import tpu as pltpu

F32, BF16 = jnp.float32, jnp.bfloat16
SDS = jax.ShapeDtypeStruct
MESH = pl.DeviceIdType.MESH
ANY = pl.BlockSpec(memory_space=pl.ANY)

SWA_HQ, SWA_HKV, SWA_HD, SWA_W = 16, 2, 64, 128
SWA_G = SWA_HQ // SWA_HKV
GDN_H, GDN_D, GDN_CONV, GDN_C = 4, 128, 4, 64
XA_H, XA_D = 4, 128
Q_W = SWA_HQ * SWA_HD
KV_W = SWA_HKV * SWA_HD
GDN_W = GDN_H * GDN_D
XA_W = XA_H * XA_D
RMS_EPS = 1e-6
L2_EPS = 1e-6
NEG = -1e30
N_DEV = 8
LANE = 128

ADAM_LR, ADAM_B1, ADAM_B2, ADAM_EPS, ADAM_WD, ADAM_STEP = 0.001, 0.9, 0.999, 1e-08, 0.01, 10

VMEM_BIG = 56 * 1024 * 1024


def _cp(sem, vmem=VMEM_BIG):
    return pltpu.CompilerParams(dimension_semantics=sem, vmem_limit_bytes=vmem)


def _div(a, b):
    assert a % b == 0, (a, b)
    return a // b


def _tile(n, t):
    t = min(t, n)
    assert n % t == 0, (n, t)
    return t


def _sigmoid(x):
    return 1.0 / (1.0 + jnp.exp(-x))


def _silu(x):
    return x * _sigmoid(x)


def _softplus(x):
    return jnp.maximum(x, 0.0) + jnp.log1p(jnp.exp(-jnp.abs(x)))


def _dot(a, b, dims, prec=None):
    return lax.dot_general(a, b, (dims, ((), ())), precision=prec, preferred_element_type=F32)


NN = ((1,), (0,))
NT = ((1,), (1,))
TN = ((0,), (0,))
HI = lax.Precision.HIGHEST


def _bdot_plain(a, b, dims):
    return _dot(a.astype(BF16), b.astype(BF16), dims)


@functools.partial(jax.custom_vjp, nondiff_argnums=(2,))
def _bdot_vjp(a, b, dims):
    return _bdot_plain(a, b, dims)


def _bdot_vjp_fwd(a, b, dims):
    return _bdot_plain(a, b, dims), (a, b)


def _bdot_vjp_bwd(dims, res, ct):
    a, b = res
    if dims == NN:
        return _bdot_plain(ct, b, NT), _bdot_plain(a, ct, TN)
    assert dims == NT, dims
    return _bdot_plain(ct, b, NN), _bdot_plain(ct, a, TN)


_bdot_vjp.defvjp(_bdot_vjp_fwd, _bdot_vjp_bwd)


def _neumann(xs):
    pws, nns = list(xs), list(xs)
    for _ in range(5):
        pws = [_bdot_plain(p, p, NN) for p in pws]
        nns = [n + p + _bdot_plain(n, p, NN) for n, p in zip(nns, pws)]
    return tuple(nns)


@jax.custom_vjp
def _neumann_vjp(xs):
    return _neumann(xs)


def _neumann_vjp_fwd(xs):
    nns = _neumann(xs)
    return nns, nns


def _neumann_vjp_bwd(nns, cts):
    ts = [ct + _bdot_plain(nn, ct, TN) for nn, ct in zip(nns, cts)]
    return (tuple(t + _bdot_plain(t, nn, NT) for t, nn in zip(ts, nns)),)


_neumann_vjp.defvjp(_neumann_vjp_fwd, _neumann_vjp_bwd)


class Layout:
    def __init__(self, d):
        self.d = d
        self.g = 0
        self.q = 3 * d
        self.qkv = self.q + Q_W
        self.z = self.qkv + 3 * GDN_W
        self.qc = self.z + GDN_W
        self.k = self.qc + XA_W
        self.v = self.k + KV_W
        self.ab = self.v + KV_W
        self.end = self.ab + LANE
        self.pw = -(-self.end // 1024) * 1024
        self.lq, self.lk, self.lv, self.lqkv = 0, Q_W, Q_W + KV_W, Q_W + 2 * KV_W
        self.la = self.lqkv + 3 * GDN_W
        self.lz = self.la + 2 * GDN_H
        self.lqc = self.lz + GDN_W
        self.lg = self.lqc + XA_W
        self.lw = self.lg + 3 * d

    def pieces(self):
        segs = [(self.lq, self.lk, self.q), (self.lk, self.lv, self.k), (self.lv, self.lqkv, self.v),
                (self.lqkv, self.la, self.qkv), (self.la, self.lz, self.ab), (self.lz, self.lqc, self.z),
                (self.lqc, self.lg, self.qc), (self.lg, self.lw, self.g)]
        cw = _div(self.lw, N_DEV)
        out = []
        for dev in range(N_DEV):
            lo, hi = dev * cw, (dev + 1) * cw
            for ls, le, ps in segs:
                s, e = max(lo, ls), min(hi, le)
                if s < e:
                    out.append((dev, s - lo, ps + s - ls, e - s))
        return out


def pad_w_in(g, lay):
    nd, k, cw = g.shape
    tr = _tile(k, 256)
    tail = lay.ab + 2 * GDN_H

    def body(g_ref, o_ref):
        o_ref[:, pl.ds(tail, lay.pw - tail)] = jnp.zeros((tr, lay.pw - tail), o_ref.dtype)
        for dev, so, po, ln in lay.pieces():
            o_ref[:, pl.ds(po, ln)] = g_ref[dev, :, pl.ds(so, ln)]

    return pl.pallas_call(
        body, name="pad_w_in", grid=(k // tr,), in_specs=[pl.BlockSpec((nd, tr, cw), lambda i: (0, i, 0))],
        out_specs=pl.BlockSpec((tr, lay.pw), lambda i: (i, 0)), out_shape=SDS((k, lay.pw), g.dtype),
        compiler_params=_cp(("parallel",)))(g)


def unpad_dw_in(dw, lay):
    k = dw.shape[0]
    cw = _div(lay.lw, N_DEV)
    tr = _tile(k, 256)

    def body(d_ref, o_ref):
        for dev, so, po, ln in lay.pieces():
            o_ref[dev, :, pl.ds(so, ln)] = d_ref[:, pl.ds(po, ln)]

    return pl.pallas_call(
        body, name="unpad_dw_in", grid=(k // tr,), in_specs=[pl.BlockSpec((tr, lay.pw), lambda i: (i, 0))],
        out_specs=pl.BlockSpec((N_DEV, tr, cw), lambda i: (0, i, 0)), out_shape=SDS((N_DEV, k, cw), dw.dtype),
        compiler_params=_cp(("parallel",)))(dw)


def _position():
    return lax.axis_index("x"), lax.axis_index("y"), lax.axis_index("c")


class GatherJob:
    def __init__(self, arrs):
        self.ins = list(arrs)
        n = len(arrs)
        self.out_shapes = [SDS((N_DEV,) + a.shape, a.dtype) for a in arrs]
        self.scratch = [pltpu.SemaphoreType.DMA((n, 7)), pltpu.SemaphoreType.DMA((n, 7)), pltpu.SemaphoreType.DMA((n,))]

    def _ctx(self, outs, sems):
        send_sems, recv_sems, _ = sems
        x, y, c = _position()

        def blk(o, p):
            return o.at[4 * p[0] + 2 * p[1] + p[2]]

        def copy(i, k, block, to, src=None):
            return pltpu.make_async_remote_copy(
                src_ref=blk(outs[i], block) if src is None else src, dst_ref=blk(outs[i], block),
                send_sem=send_sems.at[i, k], recv_sem=recv_sems.at[i, k], device_id=to, device_id_type=MESH)

        return (x, y, c), (x, y, 1 - c), [(1 - x, y), (x, 1 - y), (1 - x, 1 - y)], blk, copy

    def start(self, ins, outs, sems):
        me, sibling, chips, blk, copy = self._ctx(outs, sems)
        for i in range(len(ins)):
            pltpu.make_async_copy(ins[i], blk(outs[i], me), sems[2].at[i]).start()
            copy(i, 0, me, sibling, src=ins[i]).start()
            for j, chip in enumerate(chips):
                copy(i, 1 + j, me, (*chip, me[2]), src=ins[i]).start()

    def mid(self, ins, outs, sems):
        me, sibling, chips, blk, copy = self._ctx(outs, sems)
        for i in range(len(ins)):
            for j, chip in enumerate(chips):
                copy(i, 1 + j, (*chip, me[2]), me).wait_recv()
                copy(i, 4 + j, (*chip, me[2]), sibling).start()

    def finish(self, ins, outs, sems):
        me, sibling, chips, blk, copy = self._ctx(outs, sems)
        for i in range(len(ins)):
            copy(i, 0, sibling, me).wait_recv()
            for j, chip in enumerate(chips):
                copy(i, 4 + j, (*chip, 1 - me[2]), me).wait_recv()
        for i in range(len(ins)):
            pltpu.make_async_copy(ins[i], blk(outs[i], me), sems[2].at[i]).wait()
            copy(i, 0, me, sibling, src=ins[i]).wait_send()
            for j, chip in enumerate(chips):
                copy(i, 1 + j, me, (*chip, me[2]), src=ins[i]).wait_send()
                copy(i, 4 + j, (*chip, me[2]), sibling).wait_send()


class ChipExchangeJob:
    mid = None

    def __init__(self, arrs):
        self.ins = list(arrs)
        n = len(arrs)
        self.out_shapes = [SDS(a.shape, a.dtype) for a in arrs]
        self.scratch = [pltpu.SemaphoreType.DMA((n, 3)), pltpu.SemaphoreType.DMA((n, 3)), pltpu.SemaphoreType.DMA((n,))]

    def _copies(self, ins, outs, sems, i, arrivals):
        send_sems, recv_sems, local_sems = sems
        x, y, c = _position()
        my_chip = 2 * x + y
        chips = [(1 - x, y), (x, 1 - y), (1 - x, 1 - y)]
        if arrivals:
            return [pltpu.make_async_remote_copy(
                src_ref=ins[i].at[my_chip], dst_ref=outs[i].at[2 * px + py], send_sem=send_sems.at[i, k],
                recv_sem=recv_sems.at[i, k], device_id=(px, py, c), device_id_type=MESH) for k, (px, py) in enumerate(chips)]
        local = pltpu.make_async_copy(ins[i].at[my_chip], outs[i].at[my_chip], local_sems.at[i])
        return local, [pltpu.make_async_remote_copy(
            src_ref=ins[i].at[2 * px + py], dst_ref=outs[i].at[my_chip], send_sem=send_sems.at[i, k],
            recv_sem=recv_sems.at[i, k], device_id=(px, py, c), device_id_type=MESH) for k, (px, py) in enumerate(chips)]

    def start(self, ins, outs, sems):
        for i in range(len(ins)):
            local, remote = self._copies(ins, outs, sems, i, False)
            local.start()
            for cp in remote:
                cp.start()

    def finish(self, ins, outs, sems):
        for i in range(len(ins)):
            for cp in self._copies(ins, outs, sems, i, True):
                cp.wait_recv()
            local, remote = self._copies(ins, outs, sems, i, False)
            for cp in remote:
                cp.wait_send()
            local.wait()


def _slab_shape(g, cols):
    return (g.shape[0], _div(g.shape[1], N_DEV)) if cols else g.shape[1:]


class PairExchangeJob:
    mid = None

    def __init__(self, grads, cols):
        self.ins, self.cols = list(grads), list(cols)
        n = len(grads)
        self.out_shapes = [SDS((4,) + _slab_shape(g, cl), g.dtype) for g, cl in zip(grads, cols)]
        self.scratch = [pltpu.SemaphoreType.DMA((n, 4)), pltpu.SemaphoreType.DMA((n, 4))]

    def _copies(self, ins, outs, sems):
        send_sems, recv_sems = sems
        x, y, c = _position()

        def part(i, dst):
            if not self.cols[i]:
                return ins[i].at[dst]
            cw = _slab_shape(self.ins[i], True)[1]
            return ins[i].at[:, pl.ds(pl.multiple_of(dst * cw, LANE), cw)]

        return [pltpu.make_async_remote_copy(src_ref=part(i, 2 * j + 1 - c), dst_ref=outs[i].at[j], send_sem=send_sems.at[i, j],
                                             recv_sem=recv_sems.at[i, j], device_id=(x, y, 1 - c), device_id_type=MESH)
                for i in range(len(ins)) for j in range(4)]

    def start(self, ins, outs, sems):
        for cp in self._copies(ins, outs, sems):
            cp.start()

    def finish(self, ins, outs, sems):
        for cp in self._copies(ins, outs, sems):
            cp.wait()


def run_job(job, name):
    n = len(job.ins)

    def body(*refs):
        ins, outs, sems = refs[:n], refs[n:2 * n], refs[2 * n:]
        job.start(ins, outs, sems)
        if job.mid is not None:
            job.mid(ins, outs, sems)
        job.finish(ins, outs, sems)

    return pl.pallas_call(body, name=name, out_shape=job.out_shapes, in_specs=[ANY] * n, out_specs=[ANY] * n,
                          scratch_shapes=job.scratch)(*job.ins)


def pair_add(grad, cols, other, name):
    r, c = _slab_shape(grad, cols)
    tr = _tile(r, 256)
    parity = lax.axis_index("c").astype(jnp.int32).reshape(1)

    def body(par_ref, a_ref, b_ref, o_ref):
        o_ref[...] = (a_ref[...].astype(F32) + b_ref[...].astype(F32)).astype(BF16)

    spec = pl.BlockSpec((None, tr, c), lambda j, i, par: (j, i, 0))
    if cols:
        own = pl.BlockSpec((tr, c), lambda j, i, par: (i, 2 * j + par[0]))
    else:
        own = pl.BlockSpec((None, tr, c), lambda j, i, par: (2 * j + par[0], i, 0))
    return pl.pallas_call(
        body, name=name, out_shape=SDS(other.shape, BF16),
        grid_spec=pltpu.PrefetchScalarGridSpec(num_scalar_prefetch=1, grid=(4, r // tr), in_specs=[own, spec], out_specs=spec),
        compiler_params=_cp(("parallel", "parallel")))(parity, grad, other)


def adamw(parts, w, m, v, name):
    p, r, c = parts.shape
    assert w.shape == (1, r, c), (w.shape, parts.shape)
    tr = _tile(r, 128 if c > 1024 else 256)

    def body(p_ref, w_ref, m_ref, v_ref, g_out, d_out, m_out, v_out):
        g = p_ref[0].astype(F32)
        for j in range(1, p):
            g = g + p_ref[j].astype(F32)
        mn = ADAM_B1 * m_ref[...] + (1.0 - ADAM_B1) * g
        vn = ADAM_B2 * v_ref[...] + (1.0 - ADAM_B2) * jnp.square(g)
        m_hat = mn / (1.0 - ADAM_B1 ** ADAM_STEP)
        v_hat = vn / (1.0 - ADAM_B2 ** ADAM_STEP)
        g_out[...] = g
        d_out[...] = -ADAM_LR * (m_hat / (jnp.sqrt(v_hat) + ADAM_EPS) + ADAM_WD * w_ref[...])
        m_out[...] = mn
        v_out[...] = vn

    spec = pl.BlockSpec((None, tr, c), lambda i: (0, i, 0))
    return pl.pallas_call(
        body, name=name, grid=(r // tr,),
        in_specs=[pl.BlockSpec((p, tr, c), lambda i: (0, i, 0)), spec, spec, spec],
        out_specs=[spec] * 4, out_shape=[SDS((1, r, c), F32)] * 4, compiler_params=_cp(("parallel",)))(parts, w, m, v)


def matmul(a, b, *, mode, out_dtype, name, tm=1024, tn=1024, tk=512, a_relu2=False, resid=None, relu2_grad_of=None,
           b_cols=False, relu2_out=False, rms_gain=None, side=None, tail=None):
    if b_cols:
        nb, brows, bc = b.shape
        bshape = (brows, nb * bc)
    else:
        bshape = b.shape
    head_k = head_n = None
    if mode == "nn":
        (m, k), (k2, n) = a.shape, bshape
    elif mode == "nt":
        (m, k), (n, k2) = a.shape, bshape
        if tail is not None:
            head_k, k = k, k + tail.shape[1]
    else:
        (k, m), (k2, n) = a.shape, bshape
        if tail is not None:
            head_n, n = n, n + tail.shape[1]
    assert k == k2 and (tail is None or mode != "nn"), (a.shape, b.shape, mode)
    tm, tn, tk = _tile(m, tm), _tile(n, tn), _tile(k, tk)
    if b_cols and mode == "nn":
        tn = _tile(bc, tn)
    if b_cols and mode == "nt":
        tk = _tile(bc, tk)
    nk = k // tk
    ni, nj = m // tm, n // tn
    nk_head = _div(head_k, tk) if head_k is not None else None
    nj_head = _div(head_n, tn) if head_n is not None else None
    use_acc = nk > 1 or tail is not None
    dims = {"nn": NN, "nt": NT, "tn": TN}[mode]
    extras = [e for e in (resid, relu2_grad_of) if e is not None]
    tails = [tail] if tail is not None else []
    n_side = len(side.ins) if side is not None else 0
    gains = [rms_gain] if rms_gain is not None else []
    assert not gains or (tn == n and not relu2_out), (tn, n)
    n_main = 2 if (relu2_out or gains) else 1

    def body(*refs):
        a_ref, b_ref = refs[:2]
        t_ref = refs[2] if tails else None
        n_op = 2 + len(tails)
        e_refs = refs[n_op:n_op + len(extras)]
        n_pre = n_op + len(extras) + len(gains)
        g_ref = refs[n_pre - 1] if gains else None
        n_in = n_pre + n_side
        o_ref = refs[n_in]
        act_ref = refs[n_in + 1] if n_main == 2 else None
        acc_ref = refs[n_in + n_main + n_side] if use_acc else None
        if side is not None:
            s_ins = refs[n_pre:n_in]
            s_outs = refs[n_in + n_main:n_in + n_main + n_side]
            s_sems = refs[len(refs) - len(side.scratch):]
            step = (pl.program_id(0) * nj + pl.program_id(1)) * nk + pl.program_id(2)
            pl.when(step == 0)(lambda: side.start(s_ins, s_outs, s_sems))
            if side.mid is not None:
                pl.when(step == (ni * nj * nk * 85) // 100)(lambda: side.mid(s_ins, s_outs, s_sems))

        def operands(a_from=a_ref, b_from=b_ref):
            av = a_from[...]
            if a_relu2:
                av = jnp.square(jnp.maximum(av.astype(F32), 0.0))
            return av.astype(BF16), b_from[...].astype(BF16)

        def finish(r):
            e = list(e_refs)
            if resid is not None:
                r = r + e.pop(0)[...]
            if relu2_grad_of is not None:
                r = r * (2.0 * jnp.maximum(e.pop(0)[...], 0.0))
            o_ref[...] = r.astype(out_dtype)
            if relu2_out:
                act_ref[...] = jnp.square(jnp.maximum(r, 0.0)).astype(BF16)
            if gains:
                inv = lax.rsqrt(jnp.mean(r * r, axis=-1, keepdims=True) + RMS_EPS)
                act_ref[...] = (r * inv * g_ref[...]).astype(BF16)

        if not use_acc:
            av, bv = operands()
            finish(_dot(av, bv, dims))
        else:
            kk = pl.program_id(2)

            def accumulate(a_from, b_from):
                def product():
                    av, bv = operands(a_from, b_from)
                    return _dot(av, bv, dims)

                if nk == 1:
                    finish(product())
                    return

                @pl.when(kk == 0)
                def _():
                    acc_ref[...] = product()

                @pl.when((kk > 0) & (kk < nk - 1))
                def _():
                    acc_ref[...] += product()

                @pl.when(kk == nk - 1)
                def _():
                    finish(acc_ref[...] + product())

            if not tails:
                accumulate(a_ref, b_ref)
            elif mode == "nt":
                pl.when(kk < nk_head)(lambda: accumulate(a_ref, b_ref))
                pl.when(kk >= nk_head)(lambda: accumulate(t_ref, b_ref))
            else:
                in_head = pl.program_id(1) < nj_head
                pl.when(in_head)(lambda: accumulate(a_ref, b_ref))
                pl.when(jnp.logical_not(in_head))(lambda: accumulate(a_ref, t_ref))

        if side is not None:
            pl.when(step == ni * nj * nk - 1)(lambda: side.finish(s_ins, s_outs, s_sems))

    a_spec = {"nn": pl.BlockSpec((tm, tk), lambda i, j, kk: (i, kk)),
              "nt": pl.BlockSpec((tm, tk), lambda i, j, kk: (i, kk)),
              "tn": pl.BlockSpec((tk, tm), lambda i, j, kk: (kk, i))}[mode]
    t_specs = []
    if tails and mode == "nt":
        a_spec = pl.BlockSpec((tm, tk), lambda i, j, kk: (i, jnp.minimum(kk, nk_head - 1)))
        t_specs = [pl.BlockSpec((tm, tk), lambda i, j, kk: (i, jnp.maximum(kk - nk_head, 0)))]
    if tails and mode == "tn":
        t_specs = [pl.BlockSpec((tk, tn), lambda i, j, kk: (kk, jnp.maximum(j - nj_head, 0)))]
    if tails and mode == "tn":
        b_spec = pl.BlockSpec((tk, tn), lambda i, j, kk: (kk, jnp.minimum(j, nj_head - 1)))
    elif not b_cols:
        b_spec = {"nn": pl.BlockSpec((tk, tn), lambda i, j, kk: (kk, j)),
                  "nt": pl.BlockSpec((tn, tk), lambda i, j, kk: (j, kk)),
                  "tn": pl.BlockSpec((tk, tn), lambda i, j, kk: (kk, j))}[mode]
    elif mode == "nn":
        per = bc // tn
        b_spec = pl.BlockSpec((None, tk, tn), lambda i, j, kk: (j // per, kk, j % per))
    else:
        assert mode == "nt", mode
        per = bc // tk
        b_spec = pl.BlockSpec((None, tn, tk), lambda i, j, kk: (kk // per, j, kk % per))
    e_spec = pl.BlockSpec((tm, tn), lambda i, j, kk: (i, j))
    main_shapes = [SDS((m, n), out_dtype)] + ([SDS((m, n), BF16)] if n_main == 2 else [])
    g_specs = [pl.BlockSpec((1, tn), lambda i, j, kk: (0, j))] * len(gains)
    res = pl.pallas_call(
        body, name=name, grid=(ni, nj, nk),
        in_specs=[a_spec, b_spec] + t_specs + [e_spec] * len(extras) + g_specs + [ANY] * n_side,
        out_specs=[e_spec] * n_main + [ANY] * n_side, out_shape=main_shapes + (side.out_shapes if side is not None else []),
        scratch_shapes=([pltpu.VMEM((tm, tn), F32)] if use_acc else []) + (side.scratch if side is not None else []),
        compiler_params=_cp(("arbitrary", "arbitrary", "arbitrary")))(a, b, *tails, *extras, *gains, *(side.ins if side is not None else []))
    return res if len(res) > 1 else res[0]


def rmsnorm_fwd(x, g, name):
    s, d = x.shape
    tm = _tile(s, 256)

    def body(x_ref, g_ref, o_ref):
        xv = x_ref[...]
        r = lax.rsqrt(jnp.mean(xv * xv, axis=-1, keepdims=True) + RMS_EPS)
        o_ref[...] = (xv * r * g_ref[...]).astype(BF16)

    row = pl.BlockSpec((tm, d), lambda i: (i, 0))
    return pl.pallas_call(body, name=name, grid=(s // tm,), in_specs=[row, pl.BlockSpec((1, d), lambda i: (0, 0))],
                          out_specs=row, out_shape=SDS((s, d), BF16), compiler_params=_cp(("parallel",)))(x, g)


def _rms_bwd_rows(xv, gv, dy):
    r = lax.rsqrt(jnp.mean(xv * xv, axis=-1, keepdims=True) + RMS_EPS)
    xh = xv * r
    dxh = dy * gv
    dx = r * (dxh - xh * jnp.mean(dxh * xh, axis=-1, keepdims=True))
    return dx, jnp.sum(dy * xh, axis=0, keepdims=True)


def rmsnorm_bwd(x, g, dn, resid, name, bf16_copy=False):
    s, d = x.shape
    tm = _tile(s, 256)
    has_r = resid is not None

    def body(*refs):
        x_ref, g_ref, dn_ref = refs[:3]
        dx_ref, dg_ref = refs[3 + has_r:5 + has_r]
        dx, part = _rms_bwd_rows(x_ref[...], g_ref[...], dn_ref[...].astype(F32))
        if has_r:
            dx = dx + refs[3][...]
        dx_ref[...] = dx
        if bf16_copy:
            refs[5 + has_r][...] = dx.astype(BF16)

        @pl.when(pl.program_id(0) == 0)
        def _():
            dg_ref[...] = jnp.zeros_like(dg_ref)

        dg_ref[...] += part

    row = pl.BlockSpec((tm, d), lambda i: (i, 0))
    vec = pl.BlockSpec((1, d), lambda i: (0, 0))
    ins = [x, g, dn] + ([resid] if has_r else [])
    return pl.pallas_call(body, name=name, grid=(s // tm,), in_specs=[row, vec, row] + ([row] if has_r else []),
                          out_specs=[row, vec] + ([row] if bf16_copy else []),
                          out_shape=[SDS((s, d), F32), SDS((1, d), F32)] + ([SDS((s, d), BF16)] if bf16_copy else []),
                          compiler_params=_cp(("arbitrary",)))(*ins)


def final_norm_loss(h, g, tgt, name):
    s, d = h.shape
    tm = _tile(s, 256)

    def body(h_ref, g_ref, t_ref, dh_ref, dg_ref, l_ref, dhb_ref):
        xv, gv = h_ref[...], g_ref[...]
        r = lax.rsqrt(jnp.mean(xv * xv, axis=-1, keepdims=True) + RMS_EPS)
        e = xv * r * gv - t_ref[...]
        lpart = 0.5 * jnp.sum(jnp.mean(e * e, axis=-1, keepdims=True), axis=0, keepdims=True)
        dx, part = _rms_bwd_rows(xv, gv, e * (1.0 / d))
        dh_ref[...] = dx
        dhb_ref[...] = dx.astype(BF16)

        @pl.when(pl.program_id(0) == 0)
        def _():
            dg_ref[...] = jnp.zeros_like(dg_ref)
            l_ref[...] = jnp.zeros_like(l_ref)

        dg_ref[...] += part
        l_ref[...] += jnp.broadcast_to(lpart, l_ref.shape)

    row = pl.BlockSpec((tm, d), lambda i: (i, 0))
    vec = pl.BlockSpec((1, d), lambda i: (0, 0))
    lsp = pl.BlockSpec((1, LANE), lambda i: (0, 0))
    return pl.pallas_call(body, name=name, grid=(s // tm,), in_specs=[row, vec, row], out_specs=[row, vec, lsp, row],
                          out_shape=[SDS((s, d), F32), SDS((1, d), F32), SDS((1, LANE), F32), SDS((s, d), BF16)],
                          compiler_params=_cp(("arbitrary",)))(h, g, tgt)


def merge(p_all, ya, yb, yc, wa, wb, wc, dm, lay, name):
    s, d = ya.shape[0], lay.d
    wcols = wa.shape[2]
    bwd = dm is not None
    tm, tn = _tile(s, 2048), _tile(wcols, 512)
    nj, per = d // tn, wcols // tn

    y_specs = [pl.BlockSpec((tm, y.shape[1]), lambda i, j, *_: (i, 0)) for y in (ya, yb, yc)]
    w_specs = [pl.BlockSpec((None, w.shape[1], tn), lambda i, j, *_: (j // per, 0, j % per)) for w in (wa, wb, wc)]
    o_spec = pl.BlockSpec((tm, tn), lambda i, j, *_: (i, j))
    if not bwd:
        def body(ga, gb, gc, ya_r, yb_r, yc_r, wa_r, wb_r, wc_r, o_ref):
            ts = [_dot(y[...], w[...], NN) for y, w in ((ya_r, wa_r), (yb_r, wb_r), (yc_r, wc_r))]
            gs = [_sigmoid(g[...]) for g in (ga, gb, gc)]
            o_ref[...] = (gs[0] * ts[0] + gs[1] * ts[1] + gs[2] * ts[2]).astype(BF16)

        gate_specs = [pl.BlockSpec((tm, tn), lambda i, j, b=b: (i, b * nj + j)) for b in range(3)]
        return pl.pallas_call(
            body, name=name, grid=(s // tm, nj), in_specs=gate_specs + y_specs + w_specs,
            out_specs=o_spec, out_shape=SDS((s, d), BF16),
            compiler_params=_cp(("parallel", "parallel")))(p_all, p_all, p_all, ya, yb, yc, wa, wb, wc)

    def body_bwd(g_r, ya_r, yb_r, yc_r, wa_r, wb_r, wc_r, dm_r, dg_o, dta_o, dtb_o, dtc_o):
        for k, (y, w, dt_o) in enumerate(((ya_r, wa_r, dta_o), (yb_r, wb_r, dtb_o), (yc_r, wc_r, dtc_o))):
            @pl.when(pl.program_id(2) == k)
            def _(y=y, w=w, dt_o=dt_o):
                t = _dot(y[...], w[...], NN)
                g = _sigmoid(g_r[...])
                dmv = dm_r[...]
                dg_o[...] = (dmv * t * (g * (1.0 - g))).astype(BF16)
                dt_o[...] = (dmv * g).astype(BF16)

    gate_spec = pl.BlockSpec((tm, tn), lambda i, j, b: (i, b * nj + j))
    return pl.pallas_call(
        body_bwd, name=name, grid=(s // tm, nj, 3), in_specs=[gate_spec] + y_specs + w_specs + [o_spec],
        out_specs=[gate_spec, o_spec, o_spec, o_spec], out_shape=[SDS((s, 3 * d), BF16)] + [SDS((s, d), BF16)] * 3,
        compiler_params=_cp(("arbitrary", "arbitrary", "arbitrary")))(p_all, ya, yb, yc, wa, wb, wc, dm)


SWA_PAIRS = SWA_G // 2


def _swa_probs(qs, kcs, sinks, first):
    shape = (qs[0].shape[0], 2 * SWA_W)
    qi = lax.broadcasted_iota(jnp.int32, shape, 0) % SWA_W
    kj = lax.broadcasted_iota(jnp.int32, shape, 1)
    mask = (kj > qi) & (kj <= qi + SWA_W) & ((kj >= SWA_W) | jnp.logical_not(first))
    ss = [jnp.where(mask, _dot(q, kc, NT) * (SWA_HD ** -0.5), NEG) for q, kc in zip(qs, kcs)]
    ms = [jnp.maximum(jnp.max(s, axis=-1, keepdims=True), sink) for s, sink in zip(ss, sinks)]
    ps = [jnp.exp(s - m) for s, m in zip(ss, ms)]
    es = [jnp.exp(sink - m) for sink, m in zip(sinks, ms)]
    inv = [1.0 / (jnp.sum(p, axis=-1, keepdims=True) + e) for p, e in zip(ps, es)]
    return [p * i for p, i in zip(ps, inv)], [e * i for e, i in zip(es, inv)]


def _swa_stack(ref, h):
    return jnp.concatenate([ref[:, pl.ds((h * SWA_PAIRS + p) * LANE, LANE)] for p in range(SWA_PAIRS)], axis=0)


def _swa_unstack(ref, h, val):
    for p in range(SWA_PAIRS):
        ref[:, pl.ds((h * SWA_PAIRS + p) * LANE, LANE)] = val[p * SWA_W:(p + 1) * SWA_W]


def _swa_sink_col(sk_ref, h, second):
    pair = lax.broadcasted_iota(jnp.int32, (SWA_PAIRS * SWA_W, 1), 0) // SWA_W
    col = jnp.zeros((SWA_PAIRS * SWA_W, 1), F32)
    for p in range(SWA_PAIRS):
        hh = h * SWA_G + 2 * p + second
        col = jnp.where(pair == p, sk_ref[0:1, hh:hh + 1], col)
    return col


def _swa_kv_tiles(cur_ref, prev_ref, h):
    t = jnp.concatenate([prev_ref[...], cur_ref[...]], axis=0)
    lane = lax.broadcasted_iota(jnp.int32, t.shape, 1)
    moved = pltpu.roll(t, SWA_HD, axis=1)
    low, high = (t, moved) if h == 0 else (moved, t)
    return jnp.where(lane < SWA_HD, low, 0.0).astype(BF16), jnp.where(lane >= SWA_HD, high, 0.0).astype(BF16)


def _swa_kv_grad(g_low, g_high, h):
    lane = lax.broadcasted_iota(jnp.int32, g_low.shape, 1)
    if h == 0:
        return jnp.where(lane < SWA_HD, g_low + pltpu.roll(g_high, SWA_HD, axis=1), 0.0)
    return jnp.where(lane >= SWA_HD, pltpu.roll(g_low, SWA_HD, axis=1) + g_high, 0.0)


def _swa_specs(lay):
    w = SWA_W
    q_spec = pl.BlockSpec((w, Q_W), lambda n: (n, _div(lay.q, Q_W)))
    cur = lambda off: pl.BlockSpec((w, KV_W), lambda n: (n, _div(off, KV_W)))
    prev = lambda off: pl.BlockSpec((w, KV_W), lambda n: (jnp.maximum(n - 1, 0), _div(off, KV_W)))
    return q_spec, cur(lay.k), prev(lay.k), cur(lay.v), prev(lay.v)


def swa_fwd(p_all, sinks, lay, name):
    s = p_all.shape[0]
    nb = _div(s, SWA_W)

    def body(q_ref, kc_ref, kp_ref, vc_ref, vp_ref, sk_ref, o_ref):
        first = pl.program_id(0) == 0
        units = [(h, e) for h in range(SWA_HKV) for e in range(2)]
        ks = [_swa_kv_tiles(kc_ref, kp_ref, h) for h in range(SWA_HKV)]
        vs = [_swa_kv_tiles(vc_ref, vp_ref, h) for h in range(SWA_HKV)]
        qs = [_swa_stack(q_ref, h).astype(BF16) for h in range(SWA_HKV)]
        ps, _ = _swa_probs([qs[h] for h, e in units], [ks[h][e] for h, e in units],
                           [_swa_sink_col(sk_ref, h, e) for h, e in units], first)
        os = [_dot(p.astype(BF16), vs[h][e], NN) for p, (h, e) in zip(ps, units)]
        for h in range(SWA_HKV):
            _swa_unstack(o_ref, h, (os[2 * h] + os[2 * h + 1]).astype(BF16))

    q_spec, kc_s, kp_s, vc_s, vp_s = _swa_specs(lay)
    return pl.pallas_call(
        body, name=name, grid=(nb,),
        in_specs=[q_spec, kc_s, kp_s, vc_s, vp_s, pl.BlockSpec(sinks.shape, lambda n: (0, 0))],
        out_specs=pl.BlockSpec((SWA_W, Q_W), lambda n: (n, 0)), out_shape=SDS((s, Q_W), BF16),
        compiler_params=_cp(("parallel",)))(p_all, p_all, p_all, p_all, p_all, sinks)


def swa_bwd(p_all, sinks, dy, lay, name):
    s = p_all.shape[0]
    nb = _div(s, SWA_W)
    w = SWA_W

    def body(q_ref, kc_ref, kp_ref, vc_ref, vp_ref, sk_ref, do_ref, dq_ref, dk_ref, dv_ref, ds_ref, kcar, vcar):
        n = pl.program_id(0)
        first = n == 0

        @pl.when(first)
        def _():
            kcar[...] = jnp.zeros_like(kcar)
            vcar[...] = jnp.zeros_like(vcar)
            ds_ref[...] = jnp.zeros_like(ds_ref)

        @pl.when(n < nb)
        def _():
            lane = lax.broadcasted_iota(jnp.int32, (1, LANE), 1)
            dsink = jnp.zeros((1, LANE), F32)
            units = [(h, e) for h in range(SWA_HKV) for e in range(2)]
            ks = [_swa_kv_tiles(kc_ref, kp_ref, h) for h in range(SWA_HKV)]
            vs = [_swa_kv_tiles(vc_ref, vp_ref, h) for h in range(SWA_HKV)]
            qs = [_swa_stack(q_ref, h).astype(BF16) for h in range(SWA_HKV)]
            dos = [_swa_stack(do_ref, h).astype(BF16) for h in range(SWA_HKV)]
            ps, psinks = _swa_probs([qs[h] for h, e in units], [ks[h][e] for h, e in units],
                                    [_swa_sink_col(sk_ref, h, e) for h, e in units], first)
            dps = [_dot(dos[h], vs[h][e], NT) for h, e in units]
            dvs = [_dot(p.astype(BF16), dos[h], TN) for p, (h, e) in zip(ps, units)]
            rss = [jnp.sum(dp * p, axis=-1, keepdims=True) for dp, p in zip(dps, ps)]
            dsb = [(p * (dp - rs) * (SWA_HD ** -0.5)).astype(BF16) for p, dp, rs in zip(ps, dps, rss)]
            dqs = [_dot(d, ks[h][e], NN) for d, (h, e) in zip(dsb, units)]
            dks = [_dot(d, qs[h], TN) for d, (h, e) in zip(dsb, units)]
            for u, (h, e) in enumerate(units):
                psr = psinks[u] * rss[u]
                for pr in range(SWA_PAIRS):
                    hh = h * SWA_G + 2 * pr + e
                    dsink = dsink + jnp.where(lane == hh, -jnp.sum(psr[pr * w:(pr + 1) * w], axis=0, keepdims=True), 0.0)
            dk_tile = jnp.zeros((2 * w, KV_W), F32)
            dv_tile = jnp.zeros((2 * w, KV_W), F32)
            for h in range(SWA_HKV):
                _swa_unstack(dq_ref, h, (dqs[2 * h] + dqs[2 * h + 1]).astype(BF16))
                dk_tile = dk_tile + _swa_kv_grad(dks[2 * h], dks[2 * h + 1], h)
                dv_tile = dv_tile + _swa_kv_grad(dvs[2 * h], dvs[2 * h + 1], h)
            dk_ref[...] = (kcar[...] + dk_tile[:w]).astype(BF16)
            dv_ref[...] = (vcar[...] + dv_tile[:w]).astype(BF16)
            kcar[...] = dk_tile[w:]
            vcar[...] = dv_tile[w:]
            ds_ref[...] += dsink

        @pl.when(n == nb)
        def _():
            dk_ref[...] = kcar[...].astype(BF16)
            dv_ref[...] = vcar[...].astype(BF16)

    last = nb - 1
    q_spec = pl.BlockSpec((w, Q_W), lambda n: (jnp.minimum(n, last), _div(lay.q, Q_W)))
    cur = lambda off: pl.BlockSpec((w, KV_W), lambda n: (jnp.minimum(n, last), _div(off, KV_W)))
    prev = lambda off: pl.BlockSpec((w, KV_W), lambda n: (jnp.clip(n - 1, 0, last), _div(off, KV_W)))
    row = pl.BlockSpec((w, Q_W), lambda n: (jnp.minimum(n, last), 0))
    kv_out = pl.BlockSpec((w, KV_W), lambda n: (jnp.maximum(n - 1, 0), 0))
    return pl.pallas_call(
        body, name=name, grid=(nb + 1,),
        in_specs=[q_spec, cur(lay.k), prev(lay.k), cur(lay.v), prev(lay.v), pl.BlockSpec(sinks.shape, lambda n: (0, 0)), row],
        out_specs=[row, kv_out, kv_out, pl.BlockSpec((1, LANE), lambda n: (0, 0))],
        out_shape=[SDS((s, Q_W), BF16), SDS((s, KV_W), BF16), SDS((s, KV_W), BF16), SDS((1, LANE), F32)],
        scratch_shapes=[pltpu.VMEM((w, KV_W), F32), pltpu.VMEM((w, KV_W), F32)],
        compiler_params=_cp(("arbitrary",)))(p_all, p_all, p_all, p_all, p_all, sinks, dy)


def _xa_probs(qs, mks):
    ss = [_dot(q, mk, NT) * (XA_D ** -0.5) for q, mk in zip(qs, mks)]
    ps = [jnp.exp(s - jnp.max(s, axis=-1, keepdims=True)) for s in ss]
    inv = [1.0 / jnp.sum(p, axis=-1, keepdims=True) for p in ps]
    return [p * i for p, i in zip(ps, inv)]


def xattn_fwd(p_all, mkv, lay, name):
    s, nm = p_all.shape[0], mkv.shape[0]
    tm = _tile(s, 512)

    def body(q_ref, mkv_ref, o_ref):
        heads = range(XA_H)
        cols = [pl.ds(h * XA_D, XA_D) for h in heads]
        ps = _xa_probs([q_ref[:, c].astype(BF16) for c in cols], [mkv_ref[:, c] for c in cols])
        os = [_dot(ps[h].astype(BF16), mkv_ref[:, pl.ds(XA_W + h * XA_D, XA_D)], NN) for h in heads]
        for h in heads:
            o_ref[:, cols[h]] = os[h].astype(BF16)

    return pl.pallas_call(
        body, name=name, grid=(s // tm,),
        in_specs=[pl.BlockSpec((tm, XA_W), lambda i: (i, _div(lay.qc, XA_W))), pl.BlockSpec((nm, 2 * XA_W), lambda i: (0, 0))],
        out_specs=pl.BlockSpec((tm, XA_W), lambda i: (i, 0)), out_shape=SDS((s, XA_W), BF16),
        compiler_params=_cp(("parallel",)))(p_all, mkv)


def xattn_bwd(p_all, mkv, dy, lay, name):
    s, nm = p_all.shape[0], mkv.shape[0]
    tm = _tile(s, 512)

    def body(q_ref, mkv_ref, do_ref, dq_ref, dmkv_ref):
        @pl.when(pl.program_id(0) == 0)
        def _():
            dmkv_ref[...] = jnp.zeros_like(dmkv_ref)

        heads = range(XA_H)
        cols = [pl.ds(h * XA_D, XA_D) for h in heads]
        vcols = [pl.ds(XA_W + h * XA_D, XA_D) for h in heads]
        qs = [q_ref[:, c].astype(BF16) for c in cols]
        dos = [do_ref[:, c].astype(BF16) for c in cols]
        ps = _xa_probs(qs, [mkv_ref[:, c] for c in cols])
        dps = [_dot(dos[h], mkv_ref[:, vcols[h]], NT) for h in heads]
        dvs = [_dot(ps[h].astype(BF16), dos[h], TN) for h in heads]
        dsb = [(p * (dp - jnp.sum(dp * p, axis=-1, keepdims=True)) * (XA_D ** -0.5)).astype(BF16) for p, dp in zip(ps, dps)]
        dqs = [_dot(dsb[h], mkv_ref[:, cols[h]], NN) for h in heads]
        dks = [_dot(dsb[h], qs[h], TN) for h in heads]
        for h in heads:
            dq_ref[:, cols[h]] = dqs[h].astype(BF16)
            dmkv_ref[:, vcols[h]] += dvs[h]
            dmkv_ref[:, cols[h]] += dks[h]

    row = pl.BlockSpec((tm, XA_W), lambda i: (i, 0))
    full = pl.BlockSpec((nm, 2 * XA_W), lambda i: (0, 0))
    return pl.pallas_call(
        body, name=name, grid=(s // tm,),
        in_specs=[pl.BlockSpec((tm, XA_W), lambda i: (i, _div(lay.qc, XA_W))), full, row],
        out_specs=[row, full], out_shape=[SDS((s, XA_W), BF16), SDS((nm, 2 * XA_W), F32)],
        compiler_params=_cp(("arbitrary",)))(p_all, mkv, dy)


def _shift_down(cur, prev8, s):
    cat = jnp.concatenate([prev8, cur[0:8]], axis=0)
    return pltpu.roll(cur, s, axis=0), pltpu.roll(cat, s, axis=0)[8:16]


def _shift_up(cur, next8, s):
    tm = cur.shape[0]
    cat = jnp.concatenate([cur[tm - 8:tm], next8], axis=0)
    return pltpu.roll(cur, tm - s, axis=0), pltpu.roll(cat, 16 - s, axis=0)[0:8]


def gdn_conv_fwd(p_all, conv_w, lay, name):
    s = p_all.shape[0]
    tm = _tile(s, 512)
    c0 = _div(lay.qkv, GDN_W)

    def body(x_ref, prev_ref, w_ref, o_ref):
        cur = x_ref[...]
        prev8 = jnp.where(pl.program_id(1) > 0, prev_ref[...], 0.0)
        main = w_ref[GDN_CONV - 1:GDN_CONV, :] * cur
        top = w_ref[GDN_CONV - 1:GDN_CONV, :] * cur[0:8]
        for sft in range(1, GDN_CONV):
            wi = w_ref[GDN_CONV - 1 - sft:GDN_CONV - sft, :]
            a, b = _shift_down(cur, prev8, sft)
            main = main + wi * a
            top = top + wi * b
        o_ref[...] = main
        o_ref[0:8, :] = top

    return pl.pallas_call(
        body, name=name, grid=(3, s // tm),
        in_specs=[pl.BlockSpec((tm, GDN_W), lambda c, i: (i, c0 + c)),
                  pl.BlockSpec((8, GDN_W), lambda c, i: (jnp.maximum(i * (tm // 8) - 1, 0), c0 + c)),
                  pl.BlockSpec((GDN_CONV, GDN_W), lambda c, i: (0, c))],
        out_specs=pl.BlockSpec((tm, GDN_W), lambda c, i: (i, c)), out_shape=SDS((s, 3 * GDN_W), F32),
        compiler_params=_cp(("parallel", "parallel")))(p_all, p_all, conv_w)


def gdn_conv_bwd(p_all, conv_w, dxc, lay, name):
    s = p_all.shape[0]
    tm = _tile(s, 512)
    c0 = _div(lay.qkv, GDN_W)
    nt = s // tm

    def body(x_ref, prev_ref, d_ref, next_ref, w_ref, dx_ref, dw_ref):
        i = pl.program_id(1)
        cur, d = x_ref[...], d_ref[...]
        prev8 = jnp.where(i > 0, prev_ref[...], 0.0)
        next8 = jnp.where(i < nt - 1, next_ref[...], 0.0)
        row = lax.broadcasted_iota(jnp.int32, (tm, 1), 0)
        main = w_ref[GDN_CONV - 1:GDN_CONV, :] * d
        bot = w_ref[GDN_CONV - 1:GDN_CONV, :] * d[tm - 8:tm]
        dws = [jnp.sum(d * cur, axis=0, keepdims=True)]
        for sft in range(1, GDN_CONV):
            wi = w_ref[GDN_CONV - 1 - sft:GDN_CONV - sft, :]
            a, b = _shift_up(d, next8, sft)
            main = main + wi * a
            bot = bot + wi * b
            xa, xb = _shift_down(cur, prev8, sft)
            dws.append(jnp.sum(jnp.where(row >= 8, d * xa, 0.0), axis=0, keepdims=True)
                       + jnp.sum(d[0:8] * xb, axis=0, keepdims=True))
        dx_ref[...] = main.astype(BF16)
        dx_ref[tm - 8:tm, :] = bot.astype(BF16)

        @pl.when(i == 0)
        def _():
            dw_ref[...] = jnp.zeros_like(dw_ref)

        for sft in range(GDN_CONV):
            dw_ref[GDN_CONV - 1 - sft:GDN_CONV - sft, :] += dws[sft]

    return pl.pallas_call(
        body, name=name, grid=(3, nt),
        in_specs=[pl.BlockSpec((tm, GDN_W), lambda c, i: (i, c0 + c)),
                  pl.BlockSpec((8, GDN_W), lambda c, i: (jnp.maximum(i * (tm // 8) - 1, 0), c0 + c)),
                  pl.BlockSpec((tm, GDN_W), lambda c, i: (i, c)),
                  pl.BlockSpec((8, GDN_W), lambda c, i: (jnp.minimum((i + 1) * (tm // 8), s // 8 - 1), c)),
                  pl.BlockSpec((GDN_CONV, GDN_W), lambda c, i: (0, c))],
        out_specs=[pl.BlockSpec((tm, GDN_W), lambda c, i: (i, c)), pl.BlockSpec((GDN_CONV, GDN_W), lambda c, i: (0, c))],
        out_shape=[SDS((s, 3 * GDN_W), BF16), SDS((GDN_CONV, 3 * GDN_W), F32)],
        compiler_params=_cp(("parallel", "arbitrary")))(p_all, p_all, dxc, dxc, conv_w)


def _gdn_chunk(xq, xk, xv, ab, gp, bdot=_bdot_plain):
    c = GDN_C
    nc = xq.shape[0] // c
    lane = lax.broadcasted_iota(jnp.int32, (c, LANE), 1)
    row = lax.broadcasted_iota(jnp.int32, (c, c), 0)
    col = lax.broadcasted_iota(jnp.int32, (c, c), 1)
    g_tile = -jnp.exp(gp[0:1, :]) * _softplus(ab + gp[1:2, :])
    b_tile = _sigmoid(ab)
    tri = (row >= col).astype(F32)
    qa, ka, va = _silu(xq), _silu(xk), _silu(xv)
    items = []
    for ci in range(nc):
        rs = slice(ci * c, (ci + 1) * c)
        gcum = _dot(tri, g_tile[rs], NN, HI)
        gcum_t = gcum.T
        for h in range(GDN_H):
            hs = slice(h * GDN_D, (h + 1) * GDN_D)
            q, k, v = qa[rs, hs], ka[rs, hs], va[rs, hs]
            q = q * lax.rsqrt(jnp.sum(q * q, axis=-1, keepdims=True) + L2_EPS) * (GDN_D ** -0.5)
            k = k * lax.rsqrt(jnp.sum(k * k, axis=-1, keepdims=True) + L2_EPS)
            gc = jnp.sum(jnp.where(lane == h, gcum, 0.0), axis=1, keepdims=True)
            beta = jnp.sum(jnp.where(lane == GDN_H + h, b_tile[rs], 0.0), axis=1, keepdims=True)
            decay = jnp.exp(jnp.where(row >= col, gc - gcum_t[h:h + 1, :], NEG))
            items.append((q, k, v, gc, beta, decay))
    kks = [bdot(k, k, NT) for (_, k, _, _, _, _) in items]
    xs = tuple(-jnp.where(row > col, it[4] * kk * it[5], 0.0) for it, kk in zip(items, kks))
    nns = _neumann(xs) if bdot is _bdot_plain else _neumann_vjp(xs)
    qks = [bdot(q, k, NT) for (q, k, _, _, _, _) in items]
    out = []
    for (q, k, v, gc, beta, decay), n, qk in zip(items, nns, qks):
        eg = jnp.exp(gc)
        vb = v * beta
        kbe = k * (beta * eg)
        gl = gc[c - 1:c, :]
        out.append((vb + bdot(n, vb, NN), kbe + bdot(n, kbe, NN), q * eg, k * jnp.exp(gl - gc), qk * decay, jnp.exp(gl)))
    return [out[ci * GDN_H:(ci + 1) * GDN_H] for ci in range(nc)]


GDN_CPS = 4


def _gdn_pre_specs(lay, t):
    xspec = lambda j: pl.BlockSpec((t, GDN_W), lambda n, j=j: (n, j))
    return [xspec(0), xspec(1), xspec(2), pl.BlockSpec((t, LANE), lambda n: (n, _div(lay.ab, LANE))),
            pl.BlockSpec((8, LANE), lambda n: (0, 0))]


def gdn_pre_fwd(xc, p_all, gp, lay, name):
    s = xc.shape[0]
    c = GDN_C
    n = _div(s, c)
    cps = _tile(n, GDN_CPS)
    t = cps * c

    def body(xq, xk, xv, ab, gp_ref, u_ref, w_ref, qd_ref, kd_ref, qk_ref, gl_ref):
        lane = lax.broadcasted_iota(jnp.int32, (1, LANE), 1)
        chunks = _gdn_chunk(xq[...], xk[...], xv[...], ab[...], gp_ref[...])
        for ci, heads in enumerate(chunks):
            rs = pl.ds(ci * c, c)
            gl_row = jnp.zeros((1, LANE), F32)
            for h, (u, w, qd, kd, qk, gl) in enumerate(heads):
                hs = pl.ds(h * GDN_D, GDN_D)
                u_ref[rs, hs] = u
                w_ref[rs, hs] = w.astype(BF16)
                qd_ref[rs, hs] = qd.astype(BF16)
                kd_ref[rs, hs] = kd.astype(BF16)
                qk_ref[rs, pl.ds(h * c, c)] = qk.astype(BF16)
                gl_row = gl_row + jnp.where(lane == h, gl, 0.0)
            gl_ref[ci] = gl_row

    row = pl.BlockSpec((t, GDN_W), lambda n: (n, 0))
    return pl.pallas_call(
        body, name=name, grid=(n // cps,), in_specs=_gdn_pre_specs(lay, t),
        out_specs=[row, row, row, row, pl.BlockSpec((t, GDN_H * c), lambda n: (n, 0)), pl.BlockSpec((cps, 1, LANE), lambda n: (n, 0, 0))],
        out_shape=[SDS((s, GDN_W), F32), SDS((s, GDN_W), BF16), SDS((s, GDN_W), BF16), SDS((s, GDN_W), BF16),
                   SDS((s, GDN_H * c), BF16), SDS((n, 1, LANE), F32)],
        compiler_params=_cp(("parallel",)))(xc, xc, xc, p_all, gp)


def gdn_pre_bwd(xc, p_all, gp, du, dw, dqd, dkd, dqk, dgl, lay, name):
    s = xc.shape[0]
    c = GDN_C
    n = _div(s, c)
    cps = _tile(n, GDN_CPS)
    t = cps * c
    chunk = functools.partial(_gdn_chunk, bdot=_bdot_vjp)

    def body(xq, xk, xv, ab, gp_ref, du_r, dw_r, dqd_r, dkd_r, dqk_r, dgl_r, dxc_ref, dab_ref, dgp_ref):
        lane = lax.broadcasted_iota(jnp.int32, (1, LANE), 1)
        _, vjp = jax.vjp(chunk, xq[...], xk[...], xv[...], ab[...], gp_ref[...])
        cts = []
        for ci in range(cps):
            rs = pl.ds(ci * c, c)
            heads = []
            for h in range(GDN_H):
                hs = pl.ds(h * GDN_D, GDN_D)
                dgl_h = jnp.sum(jnp.where(lane == h, dgl_r[ci], 0.0), axis=1, keepdims=True)
                heads.append((du_r[rs, hs], dw_r[rs, hs], dqd_r[rs, hs], dkd_r[rs, hs], dqk_r[rs, pl.ds(h * c, c)], dgl_h))
            cts.append(heads)
        dq, dk, dv, dab, dgp = vjp(cts)
        dxc_ref[:, pl.ds(0, GDN_W)] = dq
        dxc_ref[:, pl.ds(GDN_W, GDN_W)] = dk
        dxc_ref[:, pl.ds(2 * GDN_W, GDN_W)] = dv
        dab_ref[...] = dab.astype(BF16)

        @pl.when(pl.program_id(0) == 0)
        def _():
            dgp_ref[...] = jnp.zeros_like(dgp_ref)

        dgp_ref[...] += dgp

    row = pl.BlockSpec((t, GDN_W), lambda n: (n, 0))
    return pl.pallas_call(
        body, name=name, grid=(n // cps,),
        in_specs=_gdn_pre_specs(lay, t) + [row, row, row, row, pl.BlockSpec((t, GDN_H * c), lambda n: (n, 0)),
                                           pl.BlockSpec((cps, 1, LANE), lambda n: (n, 0, 0))],
        out_specs=[pl.BlockSpec((t, 3 * GDN_W), lambda n: (n, 0)), pl.BlockSpec((t, LANE), lambda n: (n, 0)),
                   pl.BlockSpec((8, LANE), lambda n: (0, 0))],
        out_shape=[SDS((s, 3 * GDN_W), F32), SDS((s, LANE), BF16), SDS((8, LANE), F32)],
        compiler_params=_cp(("arbitrary",)))(xc, xc, xc, p_all, gp, du, dw, dqd, dkd, dqk, dgl)


def _lane_scalar(row, h):
    lane = lax.broadcasted_iota(jnp.int32, row.shape, 1)
    return jnp.sum(jnp.where(lane == h, row, 0.0), axis=1, keepdims=True)


def gdn_scan_fwd(u, w, qd, kd, qk, gl, name):
    s = u.shape[0]
    c = GDN_C
    n = _div(s, c)
    cps = _tile(n, GDN_CPS)
    t = cps * c

    def body(u_r, w_r, qd_r, kd_r, qk_r, gl_r, o_ref, s_ref, st):
        @pl.when(pl.program_id(0) == 0)
        def _():
            st[...] = jnp.zeros_like(st)

        heads = range(GDN_H)
        hs = [pl.ds(h * GDN_D, GDN_D) for h in heads]
        for ci in range(cps):
            rs = pl.ds(ci * c, c)
            s_ref[ci] = st[...]
            sh = [st[hs[h], :] for h in heads]
            shb = [x.astype(BF16) for x in sh]
            ws = [_dot(w_r[rs, hs[h]], shb[h], NN) for h in heads]
            qs = [_dot(qd_r[rs, hs[h]], shb[h], NN) for h in heads]
            vb = [(u_r[rs, hs[h]] - ws[h]).astype(BF16) for h in heads]
            ov = [_dot(qk_r[rs, pl.ds(h * c, c)], vb[h], NN) for h in heads]
            kv = [_dot(kd_r[rs, hs[h]], vb[h], TN) for h in heads]
            for h in heads:
                o_ref[rs, hs[h]] = qs[h] + ov[h]
                st[hs[h], :] = sh[h] * _lane_scalar(gl_r[ci], h) + kv[h]

    row = pl.BlockSpec((t, GDN_W), lambda i: (i, 0))
    return pl.pallas_call(
        body, name=name, grid=(n // cps,),
        in_specs=[row, row, row, row, pl.BlockSpec((t, GDN_H * c), lambda i: (i, 0)), pl.BlockSpec((cps, 1, LANE), lambda i: (i, 0, 0))],
        out_specs=[row, pl.BlockSpec((cps, GDN_W, GDN_D), lambda i: (i, 0, 0))],
        out_shape=[SDS((s, GDN_W), F32), SDS((n, GDN_W, GDN_D), F32)],
        scratch_shapes=[pltpu.VMEM((GDN_W, GDN_D), F32)],
        compiler_params=_cp(("arbitrary",)))(u, w, qd, kd, qk, gl)


def gdn_scan_bwd(u, w, qd, kd, qk, gl, states, do, name):
    s = u.shape[0]
    c = GDN_C
    n = _div(s, c)
    cps = _tile(n, GDN_CPS)
    t = cps * c
    steps = n // cps

    def body(u_r, w_r, qd_r, kd_r, qk_r, gl_r, s_r, do_r, du_o, dw_o, dqd_o, dkd_o, dqk_o, dgl_o, dst):
        @pl.when(pl.program_id(0) == 0)
        def _():
            dst[...] = jnp.zeros_like(dst)

        lane = lax.broadcasted_iota(jnp.int32, (1, LANE), 1)
        heads = range(GDN_H)
        hs = [pl.ds(h * GDN_D, GDN_D) for h in heads]
        qs = [pl.ds(h * c, c) for h in heads]
        for ci in reversed(range(cps)):
            rs = pl.ds(ci * c, c)
            sh = [s_r[ci, hs[h], :] for h in heads]
            shb = [x.astype(BF16) for x in sh]
            ds_out = [dst[hs[h], :] for h in heads]
            dsb = [x.astype(BF16) for x in ds_out]
            dob = [do_r[rs, hs[h]].astype(BF16) for h in heads]
            ws = [_dot(w_r[rs, hs[h]], shb[h], NN) for h in heads]
            dv1 = [_dot(qk_r[rs, qs[h]], dob[h], TN) for h in heads]
            dv2 = [_dot(kd_r[rs, hs[h]], dsb[h], NN) for h in heads]
            dqd = [_dot(dob[h], shb[h], NT) for h in heads]
            dsq = [_dot(qd_r[rs, hs[h]], dob[h], TN) for h in heads]
            vb = [(u_r[rs, hs[h]] - ws[h]).astype(BF16) for h in heads]
            dv = [dv1[h] + dv2[h] for h in heads]
            dvb = [x.astype(BF16) for x in dv]
            dw = [_dot(dvb[h], shb[h], NT) for h in heads]
            dkd = [_dot(vb[h], dsb[h], NT) for h in heads]
            dqk = [_dot(dob[h], vb[h], NT) for h in heads]
            dsw = [_dot(w_r[rs, hs[h]], dvb[h], TN) for h in heads]
            dgl_row = jnp.zeros((1, LANE), F32)
            for h in heads:
                du_o[rs, hs[h]] = dv[h]
                dw_o[rs, hs[h]] = -dw[h]
                dqd_o[rs, hs[h]] = dqd[h]
                dkd_o[rs, hs[h]] = dkd[h]
                dqk_o[rs, qs[h]] = dqk[h]
                dgl_row = dgl_row + jnp.where(lane == h, jnp.sum(jnp.sum(ds_out[h] * sh[h], axis=1, keepdims=True), axis=0, keepdims=True), 0.0)
                dst[hs[h], :] = ds_out[h] * _lane_scalar(gl_r[ci], h) + dsq[h] - dsw[h]
            dgl_o[ci] = dgl_row

    rev = lambda i: steps - 1 - i
    row = pl.BlockSpec((t, GDN_W), lambda i: (rev(i), 0))
    qks = pl.BlockSpec((t, GDN_H * c), lambda i: (rev(i), 0))
    gls = pl.BlockSpec((cps, 1, LANE), lambda i: (rev(i), 0, 0))
    return pl.pallas_call(
        body, name=name, grid=(steps,),
        in_specs=[row, row, row, row, qks, gls, pl.BlockSpec((cps, GDN_W, GDN_D), lambda i: (rev(i), 0, 0)), row],
        out_specs=[row, row, row, row, qks, gls],
        out_shape=[SDS((s, GDN_W), F32)] * 4 + [SDS((s, GDN_H * c), F32), SDS((n, 1, LANE), F32)],
        scratch_shapes=[pltpu.VMEM((GDN_W, GDN_D), F32)],
        compiler_params=_cp(("arbitrary",)))(u, w, qd, kd, qk, gl, states, do)


def _gdn_out_rows(o, z, nw):
    outs = []
    for h in range(GDN_H):
        hs = slice(h * GDN_D, (h + 1) * GDN_D)
        oh = o[:, hs]
        y = oh * lax.rsqrt(jnp.mean(oh * oh, axis=-1, keepdims=True) + RMS_EPS) * nw
        outs.append(y * _silu(z[:, hs]))
    return jnp.concatenate(outs, axis=1)


def gdn_out(o, p_all, nw, dy, lay, name):
    s = o.shape[0]
    tm = _tile(s, 512)
    bwd = dy is not None

    def body(*refs):
        o_r, z_r, nw_r = refs[:3]
        if not bwd:
            refs[3][...] = _gdn_out_rows(o_r[...], z_r[...], nw_r[...]).astype(BF16)
            return
        dy_r, do_o, dz_o, dnw_o = refs[3:]
        _, vjp = jax.vjp(_gdn_out_rows, o_r[...], z_r[...], nw_r[...])
        d_o, d_z, d_nw = vjp(dy_r[...].astype(F32))
        do_o[...] = d_o
        dz_o[...] = d_z.astype(BF16)

        @pl.when(pl.program_id(0) == 0)
        def _():
            dnw_o[...] = jnp.zeros_like(dnw_o)

        dnw_o[...] += d_nw

    row = pl.BlockSpec((tm, GDN_W), lambda i: (i, 0))
    zs = pl.BlockSpec((tm, GDN_W), lambda i: (i, _div(lay.z, GDN_W)))
    nws = pl.BlockSpec((1, GDN_D), lambda i: (0, 0))
    if not bwd:
        return pl.pallas_call(body, name=name, grid=(s // tm,), in_specs=[row, zs, nws], out_specs=row,
                              out_shape=SDS((s, GDN_W), BF16), compiler_params=_cp(("parallel",)))(o, p_all, nw)
    return pl.pallas_call(body, name=name, grid=(s // tm,), in_specs=[row, zs, nws, row], out_specs=[row, row, nws],
                          out_shape=[SDS((s, GDN_W), F32), SDS((s, GDN_W), BF16), SDS((1, GDN_D), F32)],
                          compiler_params=_cp(("arbitrary",)))(o, p_all, nw, dy)


def _cols_to_full(g):
    n, k, c = g.shape
    return g.transpose(1, 0, 2).reshape(k, n * c)


def _rows_to_blocks(w):
    return w.reshape(N_DEV, w.shape[0] // N_DEV, w.shape[1])


def _pack_small(parts, rows):
    flat = jnp.concatenate([jnp.pad(p.reshape(-1), (0, -p.size % LANE)) for p in parts])
    return jnp.pad(flat, (0, rows * LANE - flat.size)).reshape(rows, LANE)


def kernel(x, mem, g_mix, w_in, sinks, conv_w, a_log, dt_bias, gdn_norm_w, g_mem, w_mem_kv, w_swa_up, w_gdn_up, w_xa_up, w_out, g_mlp, w_mlp_in, w_mlp_out, g_final, loss_target, m_g_mix, m_w_in, m_sinks, m_conv_w, m_a_log, m_dt_bias, m_gdn_norm_w, m_g_mem, m_w_mem_kv, m_w_swa_up, m_w_gdn_up, m_w_xa_up, m_w_out, m_g_mlp, m_w_mlp_in, m_w_mlp_out, m_g_final, v_g_mix, v_w_in, v_sinks, v_conv_w, v_a_log, v_dt_bias, v_gdn_norm_w, v_g_mem, v_w_mem_kv, v_w_swa_up, v_w_gdn_up, v_w_xa_up, v_w_out, v_g_mlp, v_w_mlp_in, v_w_mlp_out, v_g_final):
    xs, ms, tgt = x[0], mem[0], loss_target[0]
    s, d = xs.shape
    lay = Layout(d)
    px, py, pc = _position()
    dev = 4 * px + 2 * py + pc

    g_in, g_conv = run_job(GatherJob([w_in[0].astype(BF16), conv_w[0]]), "gather_w_in")
    W_in = pad_w_in(g_in, lay)
    convw = _cols_to_full(g_conv)
    gp = jnp.zeros((8, LANE), F32).at[0, :GDN_H].set(a_log[0]).at[1, :GDN_H].set(dt_bias[0])
    later = [w_mem_kv[0], w_swa_up[0], w_gdn_up[0], w_xa_up[0], w_out[0], w_mlp_in[0]]

    n1 = rmsnorm_fwd(xs, g_mix, "norm_mix")
    p_all, g_mkv, W_sup, W_gup, W_xup, g_out, W_m1 = matmul(
        n1, W_in, mode="nn", out_dtype=F32, name="proj_in", tm=2048, tn=1024, tk=d,
        side=GatherJob([w.astype(BF16) for w in later]))
    W_mkv = g_mkv.reshape(-1, g_mkv.shape[2])
    W_out = g_out.reshape(-1, d)
    y_a = swa_fwd(p_all, sinks, lay, "swa_fwd")
    xc = gdn_conv_fwd(p_all, convw, lay, "gdn_conv_fwd")
    u, gw, gqd, gkd, gqk, ggl = gdn_pre_fwd(xc, p_all, gp, lay, "gdn_pre_fwd")
    o_b, states = gdn_scan_fwd(u, gw, gqd, gkd, gqk, ggl, "gdn_scan_fwd")
    y_b = gdn_out(o_b, p_all, gdn_norm_w, None, lay, "gdn_out_fwd")
    nm = rmsnorm_fwd(ms, g_mem, "norm_mem")
    mkv = matmul(nm, W_mkv, mode="nn", out_dtype=BF16, name="proj_mem", tk=d)
    y_c = xattn_fwd(p_all, mkv, lay, "xattn_fwd")
    merged = merge(p_all, y_a, y_b, y_c, W_sup, W_gup, W_xup, None, lay, "merge_fwd")
    h1, n2 = matmul(merged, W_out, mode="nn", out_dtype=F32, name="proj_out", tm=512, tn=d, tk=d, resid=xs, rms_gain=g_mlp)
    uu, act, g_m2 = matmul(n2, W_m1, mode="nn", out_dtype=F32, name="mlp_in", tm=2048, tn=512, tk=d, b_cols=True,
                           relu2_out=True, side=GatherJob([w_mlp_out[0].astype(BF16)]))
    W_m2 = g_m2.reshape(-1, d)
    h2 = matmul(act, W_m2, mode="nn", out_dtype=F32, name="mlp_out", tm=512, tn=2048, tk=2048, resid=h1)
    dh2, dg_final, lrow, dh2_b = final_norm_loss(h2, g_final.reshape(1, d), tgt, "final_loss")
    loss = lax.psum(lrow[0, 0], ("x", "y", "c"))

    du = matmul(dh2_b, W_m2, mode="nt", out_dtype=BF16, name="mlp_out_dx", tm=2048, tn=512, tk=d, relu2_grad_of=uu)
    dW_m2 = matmul(act, dh2_b, mode="tn", out_dtype=BF16, name="mlp_out_dw", tm=1024, tn=2048, tk=1024)
    dW_m2 = _rows_to_blocks(dW_m2)
    dn2, sib_m2 = matmul(du, W_m1, mode="nt", out_dtype=F32, name="mlp_in_dx", tm=1024, tn=2048, tk=1024, b_cols=True,
                         side=PairExchangeJob([dW_m2], [False]))
    c_m2 = pair_add(dW_m2, False, sib_m2, "grads_pair_add_m2")
    dW_m1 = matmul(n2, du, mode="tn", out_dtype=BF16, name="mlp_in_dw", tm=2048, tn=1024, tk=1024)
    dh1, dg_mlp, dh1_b = rmsnorm_bwd(h1, g_mlp, dn2, dh2, "norm_mlp_bwd", bf16_copy=True)

    dmerged, sib_m1 = matmul(dh1_b, W_out, mode="nt", out_dtype=F32, name="proj_out_dx", tm=2048, tn=512, tk=d,
                             side=PairExchangeJob([dW_m1], [True]))
    c_m1 = pair_add(dW_m1, True, sib_m1, "grads_pair_add_m1")
    dW_out = matmul(merged, dh1_b, mode="tn", out_dtype=BF16, name="proj_out_dw", tm=2048, tn=1024, tk=1024)
    dgates, dta, dtb, dtc = merge(p_all, y_a, y_b, y_c, W_sup, W_gup, W_xup, dmerged, lay, "merge_bwd")
    dy_a = matmul(dta, W_sup, mode="nt", out_dtype=BF16, name="swa_up_dx", tm=2048, tk=d, b_cols=True)
    dy_b = matmul(dtb, W_gup, mode="nt", out_dtype=BF16, name="gdn_up_dx", tm=2048, tk=d, b_cols=True)
    dy_c = matmul(dtc, W_xup, mode="nt", out_dtype=BF16, name="xa_up_dx", tm=2048, tk=d, b_cols=True)
    dW_sup = matmul(y_a, dta, mode="tn", out_dtype=BF16, name="swa_up_dw", tn=2048)
    dW_gup = matmul(y_b, dtb, mode="tn", out_dtype=BF16, name="gdn_up_dw", tn=2048)
    dW_xup = matmul(y_c, dtc, mode="tn", out_dtype=BF16, name="xa_up_dw", tn=2048)

    dq_a, dk_a, dv_a, dsinks = swa_bwd(p_all, sinks, dy_a, lay, "swa_bwd")
    dq_c, dmkv = xattn_bwd(p_all, mkv, dy_c, lay, "xattn_bwd")
    dW_mkv = matmul(nm, dmkv, mode="tn", out_dtype=BF16, name="proj_mem_dw", tk=256)
    dnm = matmul(dmkv, W_mkv, mode="nt", out_dtype=F32, name="proj_mem_dx", tk=1024)
    _, dg_mem = rmsnorm_bwd(ms, g_mem, dnm, None, "norm_mem_bwd")

    do_b, dz, dnorm_w = gdn_out(o_b, p_all, gdn_norm_w, dy_b, lay, "gdn_out_bwd")
    du_g, dw_g, dqd_g, dkd_g, dqk_g, dgl_g = gdn_scan_bwd(u, gw, gqd, gkd, gqk, ggl, states, do_b, "gdn_scan_bwd")
    dxc, dab, dgp = gdn_pre_bwd(xc, p_all, gp, du_g, dw_g, dqd_g, dkd_g, dqk_g, dgl_g, lay, "gdn_pre_bwd")
    dqkv, dconv = gdn_conv_bwd(p_all, convw, dxc, lay, "gdn_conv_bwd")

    drest = jnp.concatenate([dq_a, dqkv, dz, dq_c, dk_a, dv_a, dab, jnp.zeros((s, lay.pw - lay.end), BF16)], axis=1)
    def pair_stage(grads, cols, tag):
        from_sib = run_job(PairExchangeJob(grads, cols), "grads_pair_exchange_" + tag)
        return [pair_add(g, cl, o, "grads_pair_add_%s%d" % (tag, i)) for i, (g, cl, o) in enumerate(zip(grads, cols, from_sib))]

    small = pair_stage([_rows_to_blocks(dW_mkv), dW_sup, dW_gup, dW_xup, _rows_to_blocks(dW_out)],
                       [False, True, True, True, False], "a")
    dW_in, p_m1, p_m2 = matmul(n1, dgates, tail=drest, mode="tn", out_dtype=BF16, name="proj_in_dw", tm=2048, tn=1024, tk=1024,
                               side=ChipExchangeJob([c_m1, c_m2]))
    late = pair_stage([unpad_dw_in(dW_in, lay)], [False], "b")
    dn1, p_in, p_mkv, p_sup, p_gup, p_xup, p_out = matmul(
        dgates, W_in, tail=drest, mode="nt", out_dtype=F32, name="proj_in_dx", tm=1024, tn=2048, tk=1024,
        side=ChipExchangeJob(late + small))
    grad_x, dg_mix = rmsnorm_bwd(xs, g_mix, dn1, dh1, "norm_mix_bwd")
    parts = [p_in, p_mkv, p_sup, p_gup, p_xup, p_out, p_m1, p_m2]

    shard_names = [(w_in, m_w_in, v_w_in), (w_mem_kv, m_w_mem_kv, v_w_mem_kv), (w_swa_up, m_w_swa_up, v_w_swa_up),
                   (w_gdn_up, m_w_gdn_up, v_w_gdn_up), (w_xa_up, m_w_xa_up, v_w_xa_up), (w_out, m_w_out, v_w_out),
                   (w_mlp_in, m_w_mlp_in, v_w_mlp_in), (w_mlp_out, m_w_mlp_out, v_w_mlp_out)]
    big_res = [adamw(p, w, m, v, "adamw_%d" % i) for i, (p, (w, m, v)) in enumerate(zip(parts, shard_names))]

    smalls = [(g_mix, m_g_mix, v_g_mix, dg_mix), (sinks, m_sinks, v_sinks, dsinks[:, :SWA_HQ]),
              (a_log, m_a_log, v_a_log, dgp[0:1, :GDN_H]), (dt_bias, m_dt_bias, v_dt_bias, dgp[1:2, :GDN_H]),
              (gdn_norm_w, m_gdn_norm_w, v_gdn_norm_w, dnorm_w), (g_mem, m_g_mem, v_g_mem, dg_mem),
              (g_mlp, m_g_mlp, v_g_mlp, dg_mlp), (g_final, m_g_final, v_g_final, dg_final)]
    sizes = [-(-t[0].size // LANE) * LANE for t in smalls] + [GDN_CONV * 3 * GDN_W]
    rows = -(-sum(sizes) // (8 * LANE)) * 8
    csh = conv_w.shape[2]

    def conv_place(a):
        full = jnp.tile(a[0], (1, N_DEV))
        owner = lax.broadcasted_iota(jnp.int32, full.shape, 1) // csh
        return jnp.where(owner == dev, full, 0.0)

    g_pack = _pack_small([t[3] for t in smalls] + [dconv], rows)
    w_pack = _pack_small([t[0] for t in smalls] + [conv_place(conv_w)], rows)
    m_pack = _pack_small([t[1] for t in smalls] + [conv_place(m_conv_w)], rows)
    v_pack = _pack_small([t[2] for t in smalls] + [conv_place(v_conv_w)], rows)
    g_all = run_job(GatherJob([g_pack]), "gather_small_grads")[0]
    small_res = adamw(g_all, w_pack[None], m_pack[None], v_pack[None], "adamw_small")

    def unpack(arr):
        flat = arr.reshape(-1)
        outs, off = [], 0
        for t, sz in zip(smalls, sizes[:-1]):
            outs.append(flat[off:off + t[0].size].reshape(t[0].shape))
            off += sz
        cw = flat[off:off + sizes[-1]].reshape(GDN_CONV, 3 * GDN_W)
        mine = (lax.broadcasted_iota(jnp.int32, (1, N_DEV, 1), 1) == dev).astype(F32)
        outs.append(jnp.sum(cw.reshape(GDN_CONV, N_DEV, csh) * mine, axis=1)[None])
        return outs

    sg, sd, sm, sv = (unpack(a) for a in small_res)
    bg, bd, bm, bv = ([r[i] for r in big_res] for i in range(4))

    def ordered(sm_, bg_):
        return [sm_[0], bg_[0], sm_[1], sm_[8], sm_[2], sm_[3], sm_[4], sm_[5], bg_[1], bg_[2], bg_[3], bg_[4], bg_[5],
                sm_[6], bg_[6], bg_[7], sm_[7]]

    return (loss, grad_x[None], *ordered(sg, bg), *ordered(sd, bd), *ordered(sm, bm), *ordered(sv, bv))
```

```python
import functools
import math

import jax
import jax.numpy as jnp
from jax import lax
from jax.experimental import pallas as pl
from jax.experimental.pallas import tpu as pltpu

F32, BF16 = jnp.float32, jnp.bfloat16
SDS = jax.ShapeDtypeStruct
MESH = pl.DeviceIdType.MESH
ANY = pl.BlockSpec(memory_space=pl.ANY)

SWA_HQ, SWA_HKV, SWA_HD, SWA_W = 16, 2, 64, 128
SWA_G = SWA_HQ // SWA_HKV
GDN_H, GDN_D, GDN_CONV, GDN_C = 4, 128, 4, 64
XA_H, XA_D = 4, 128
Q_W = SWA_HQ * SWA_HD
KV_W = SWA_HKV * SWA_HD
GDN_W = GDN_H * GDN_D
XA_W = XA_H * XA_D
RMS_EPS = 1e-6
L2_EPS = 1e-6
NEG = -1e30
N_DEV = 8
LANE = 128

ADAM_LR, ADAM_B1, ADAM_B2, ADAM_EPS, ADAM_WD, ADAM_STEP = 0.001, 0.9, 0.999, 1e-08, 0.01, 10

VMEM_BIG = 56 * 1024 * 1024


def _cp(sem, vmem=VMEM_BIG):
    return pltpu.CompilerParams(dimension_semantics=sem, vmem_limit_bytes=vmem)


def _div(a, b):
    assert a % b == 0, (a, b)
    return a // b


def _tile(n, t):
    t = min(t, n)
    assert n % t == 0, (n, t)
    return t


def _sigmoid(x):
    return 1.0 / (1.0 + jnp.exp(-x))


def _silu(x):
    return x * _sigmoid(x)


def _softplus(x):
    return jnp.maximum(x, 0.0) + jnp.log1p(jnp.exp(-jnp.abs(x)))


def _dot(a, b, dims, prec=None):
    return lax.dot_general(a, b, (dims, ((), ())), precision=prec, preferred_element_type=F32)


NN = ((1,), (0,))
NT = ((1,), (1,))
TN = ((0,), (0,))
HI = lax.Precision.HIGHEST


def _bdot_plain(a, b, dims):
    return _dot(a.astype(BF16), b.astype(BF16), dims)


@functools.partial(jax.custom_vjp, nondiff_argnums=(2,))
def _bdot_vjp(a, b, dims):
    return _bdot_plain(a, b, dims)


def _bdot_vjp_fwd(a, b, dims):
    return _bdot_plain(a, b, dims), (a, b)


def _bdot_vjp_bwd(dims, res, ct):
    a, b = res
    if dims == NN:
        return _bdot_plain(ct, b, NT), _bdot_plain(a, ct, TN)
    assert dims == NT, dims
    return _bdot_plain(ct, b, NN), _bdot_plain(ct, a, TN)


_bdot_vjp.defvjp(_bdot_vjp_fwd, _bdot_vjp_bwd)


def _neumann(xs):
    pws, nns = list(xs), list(xs)
    for _ in range(5):
        pws = [_bdot_plain(p, p, NN) for p in pws]
        nns = [n + p + _bdot_plain(n, p, NN) for n, p in zip(nns, pws)]
    return tuple(nns)


@jax.custom_vjp
def _neumann_vjp(xs):
    return _neumann(xs)


def _neumann_vjp_fwd(xs):
    nns = _neumann(xs)
    return nns, nns


def _neumann_vjp_bwd(nns, cts):
    ts = [ct + _bdot_plain(nn, ct, TN) for nn, ct in zip(nns, cts)]
    return (tuple(t + _bdot_plain(t, nn, NT) for t, nn in zip(ts, nns)),)


_neumann_vjp.defvjp(_neumann_vjp_fwd, _neumann_vjp_bwd)


class Layout:
    def __init__(self, d):
        self.d = d
        self.g = 0
        self.q = 3 * d
        self.qkv = self.q + Q_W
        self.z = self.qkv + 3 * GDN_W
        self.qc = self.z + GDN_W
        self.k = self.qc + XA_W
        self.v = self.k + KV_W
        self.ab = self.v + KV_W
        self.end = self.ab + LANE
        self.pw = -(-self.end // 1024) * 1024
        self.lq, self.lk, self.lv, self.lqkv = 0, Q_W, Q_W + KV_W, Q_W + 2 * KV_W
        self.la = self.lqkv + 3 * GDN_W
        self.lz = self.la + 2 * GDN_H
        self.lqc = self.lz + GDN_W
        self.lg = self.lqc + XA_W
        self.lw = self.lg + 3 * d

    def pieces(self):
        segs = [(self.lq, self.lk, self.q), (self.lk, self.lv, self.k), (self.lv, self.lqkv, self.v),
                (self.lqkv, self.la, self.qkv), (self.la, self.lz, self.ab), (self.lz, self.lqc, self.z),
                (self.lqc, self.lg, self.qc), (self.lg, self.lw, self.g)]
        cw = _div(self.lw, N_DEV)
        out = []
        for dev in range(N_DEV):
            lo, hi = dev * cw, (dev + 1) * cw
            for ls, le, ps in segs:
                s, e = max(lo, ls), min(hi, le)
                if s < e:
                    out.append((dev, s - lo, ps + s - ls, e - s))
        return out


def pad_w_in(g, lay):
    nd, k, cw = g.shape
    tr = _tile(k, 256)
    tail = lay.ab + 2 * GDN_H

    def body(g_ref, o_ref):
        o_ref[:, pl.ds(tail, lay.pw - tail)] = jnp.zeros((tr, lay.pw - tail), o_ref.dtype)
        for dev, so, po, ln in lay.pieces():
            o_ref[:, pl.ds(po, ln)] = g_ref[dev, :, pl.ds(so, ln)]

    return pl.pallas_call(
        body, name="pad_w_in", grid=(k // tr,), in_specs=[pl.BlockSpec((nd, tr, cw), lambda i: (0, i, 0))],
        out_specs=pl.BlockSpec((tr, lay.pw), lambda i: (i, 0)), out_shape=SDS((k, lay.pw), g.dtype),
        compiler_params=_cp(("parallel",)))(g)


def unpad_dw_in(dw, lay):
    k = dw.shape[0]
    cw = _div(lay.lw, N_DEV)
    tr = _tile(k, 256)

    def body(d_ref, o_ref):
        for dev, so, po, ln in lay.pieces():
            o_ref[dev, :, pl.ds(so, ln)] = d_ref[:, pl.ds(po, ln)]

    return pl.pallas_call(
        body, name="unpad_dw_in", grid=(k // tr,), in_specs=[pl.BlockSpec((tr, lay.pw), lambda i: (i, 0))],
        out_specs=pl.BlockSpec((N_DEV, tr, cw), lambda i: (0, i, 0)), out_shape=SDS((N_DEV, k, cw), dw.dtype),
        compiler_params=_cp(("parallel",)))(dw)


def _position():
    return lax.axis_index("x"), lax.axis_index("y"), lax.axis_index("c")


class GatherJob:
    def __init__(self, arrs):
        self.ins = list(arrs)
        n = len(arrs)
        self.out_shapes = [SDS((N_DEV,) + a.shape, a.dtype) for a in arrs]
        self.scratch = [pltpu.SemaphoreType.DMA((n, 7)), pltpu.SemaphoreType.DMA((n, 7)), pltpu.SemaphoreType.DMA((n,))]

    def _ctx(self, outs, sems):
        send_sems, recv_sems, _ = sems
        x, y, c = _position()

        def blk(o, p):
            return o.at[4 * p[0] + 2 * p[1] + p[2]]

        def copy(i, k, block, to, src=None):
            return pltpu.make_async_remote_copy(
                src_ref=blk(outs[i], block) if src is None else src, dst_ref=blk(outs[i], block),
                send_sem=send_sems.at[i, k], recv_sem=recv_sems.at[i, k], device_id=to, device_id_type=MESH)

        return (x, y, c), (x, y, 1 - c), [(1 - x, y), (x, 1 - y), (1 - x, 1 - y)], blk, copy

    def start(self, ins, outs, sems):
        me, sibling, chips, blk, copy = self._ctx(outs, sems)
        for i in range(len(ins)):
            pltpu.make_async_copy(ins[i], blk(outs[i], me), sems[2].at[i]).start()
            copy(i, 0, me, sibling, src=ins[i]).start()
            for j, chip in enumerate(chips[:2]):
                copy(i, 1 + j, me, (*chip, me[2]), src=ins[i]).start()

    def _relay(self, i, outs, sems, onward):
        me, _, _, blk, copy = self._ctx(outs, sems)
        x, y, c = me
        origin = (x + (1 - c) - 2 * x * (1 - c), y + c - 2 * y * c, c)
        if not onward:
            return copy(i, 1 + c, origin, me)
        return copy(i, 3, origin, (x + c - 2 * x * c, y + (1 - c) - 2 * y * (1 - c), c))

    def relay(self, ins, outs, sems):
        for i in range(len(ins)):
            self._relay(i, outs, sems, False).wait_recv()
            self._relay(i, outs, sems, True).start()

    def mid(self, ins, outs, sems):
        me, sibling, chips, blk, copy = self._ctx(outs, sems)
        for i in range(len(ins)):
            for j, chip in enumerate(chips):
                arrival = copy(i, 1 + j, (*chip, me[2]), me)
                if j < 2:
                    pl.when(me[2] != j)(arrival.wait_recv)
                else:
                    arrival.wait_recv()
                copy(i, 4 + j, (*chip, me[2]), sibling).start()

    def finish(self, ins, outs, sems):
        me, sibling, chips, blk, copy = self._ctx(outs, sems)
        for i in range(len(ins)):
            copy(i, 0, sibling, me).wait_recv()
            for j, chip in enumerate(chips):
                copy(i, 4 + j, (*chip, 1 - me[2]), me).wait_recv()
        for i in range(len(ins)):
            pltpu.make_async_copy(ins[i], blk(outs[i], me), sems[2].at[i]).wait()
            copy(i, 0, me, sibling, src=ins[i]).wait_send()
            for j, chip in enumerate(chips[:2]):
                copy(i, 1 + j, me, (*chip, me[2]), src=ins[i]).wait_send()
            self._relay(i, outs, sems, True).wait_send()
            for j, chip in enumerate(chips):
                copy(i, 4 + j, (*chip, me[2]), sibling).wait_send()


class ChipExchangeJob:
    mid = None

    def __init__(self, arrs):
        self.ins = list(arrs)
        n = len(arrs)
        self.out_shapes = [SDS(a.shape, a.dtype) for a in arrs]
        self.scratch = [pltpu.SemaphoreType.DMA((n, 3)), pltpu.SemaphoreType.DMA((n, 3)), pltpu.SemaphoreType.DMA((n,))]

    def _copies(self, ins, outs, sems, i, arrivals):
        send_sems, recv_sems, local_sems = sems
        x, y, c = _position()
        my_chip = 2 * x + y
        chips = [(1 - x, y), (x, 1 - y), (1 - x, 1 - y)]
        if arrivals:
            return [pltpu.make_async_remote_copy(
                src_ref=ins[i].at[my_chip], dst_ref=outs[i].at[2 * px + py], send_sem=send_sems.at[i, k],
                recv_sem=recv_sems.at[i, k], device_id=(px, py, c), device_id_type=MESH) for k, (px, py) in enumerate(chips)]
        local = pltpu.make_async_copy(ins[i].at[my_chip], outs[i].at[my_chip], local_sems.at[i])
        return local, [pltpu.make_async_remote_copy(
            src_ref=ins[i].at[2 * px + py], dst_ref=outs[i].at[my_chip], send_sem=send_sems.at[i, k],
            recv_sem=recv_sems.at[i, k], device_id=(px, py, c), device_id_type=MESH) for k, (px, py) in enumerate(chips)]

    def start(self, ins, outs, sems):
        for i in range(len(ins)):
            local, remote = self._copies(ins, outs, sems, i, False)
            local.start()
            for cp in remote:
                cp.start()

    def finish(self, ins, outs, sems):
        for i in range(len(ins)):
            for cp in self._copies(ins, outs, sems, i, True):
                cp.wait_recv()
            local, remote = self._copies(ins, outs, sems, i, False)
            for cp in remote:
                cp.wait_send()
            local.wait()


def _slab_shape(g, cols):
    return (g.shape[0], _div(g.shape[1], N_DEV)) if cols else g.shape[1:]


class PairExchangeJob:
    mid = None

    def __init__(self, grads, cols):
        self.ins, self.cols = list(grads), list(cols)
        n = len(grads)
        self.out_shapes = [SDS((4,) + _slab_shape(g, cl), g.dtype) for g, cl in zip(grads, cols)]
        self.scratch = [pltpu.SemaphoreType.DMA((n, 4)), pltpu.SemaphoreType.DMA((n, 4))]

    def _copies(self, ins, outs, sems):
        send_sems, recv_sems = sems
        x, y, c = _position()

        def part(i, dst):
            if not self.cols[i]:
                return ins[i].at[dst]
            cw = _slab_shape(self.ins[i], True)[1]
            return ins[i].at[:, pl.ds(pl.multiple_of(dst * cw, LANE), cw)]

        return [pltpu.make_async_remote_copy(src_ref=part(i, 2 * j + 1 - c), dst_ref=outs[i].at[j], send_sem=send_sems.at[i, j],
                                             recv_sem=recv_sems.at[i, j], device_id=(x, y, 1 - c), device_id_type=MESH)
                for i in range(len(ins)) for j in range(4)]

    def start(self, ins, outs, sems):
        for cp in self._copies(ins, outs, sems):
            cp.start()

    def finish(self, ins, outs, sems):
        for cp in self._copies(ins, outs, sems):
            cp.wait()


def _host_begin(job, step, steps, ins, outs, sems):
    pl.when(step == 0)(lambda: job.start(ins, outs, sems))
    if job.mid is not None:
        pl.when(step == (steps * 45) // 100)(lambda: job.relay(ins, outs, sems))
        pl.when(step == (steps * 85) // 100)(lambda: job.mid(ins, outs, sems))


def run_job(job, name):
    n = len(job.ins)

    def body(*refs):
        ins, outs, sems = refs[:n], refs[n:2 * n], refs[2 * n:]
        job.start(ins, outs, sems)
        if job.mid is not None:
            job.relay(ins, outs, sems)
            job.mid(ins, outs, sems)
        job.finish(ins, outs, sems)

    return pl.pallas_call(body, name=name, out_shape=job.out_shapes, in_specs=[ANY] * n, out_specs=[ANY] * n,
                          scratch_shapes=job.scratch)(*job.ins)


def pair_add(grad, cols, other, name):
    r, c = _slab_shape(grad, cols)
    tr = _tile(r, 256)
    parity = lax.axis_index("c").astype(jnp.int32).reshape(1)

    def body(par_ref, a_ref, b_ref, o_ref):
        o_ref[...] = (a_ref[...].astype(F32) + b_ref[...].astype(F32)).astype(BF16)

    spec = pl.BlockSpec((None, tr, c), lambda j, i, par: (j, i, 0))
    if cols:
        own = pl.BlockSpec((tr, c), lambda j, i, par: (i, 2 * j + par[0]))
    else:
        own = pl.BlockSpec((None, tr, c), lambda j, i, par: (2 * j + par[0], i, 0))
    return pl.pallas_call(
        body, name=name, out_shape=SDS(other.shape, BF16),
        grid_spec=pltpu.PrefetchScalarGridSpec(num_scalar_prefetch=1, grid=(4, r // tr), in_specs=[own, spec], out_specs=spec),
        compiler_params=_cp(("parallel", "parallel")))(parity, grad, other)


def adamw(parts, w, m, v, name):
    p, r, c = parts.shape
    tr = _tile(r, 128 if c > 1024 else 256)

    def body(p_ref, w_ref, m_ref, v_ref, g_out, d_out, m_out, v_out):
        g = p_ref[0].astype(F32)
        for j in range(1, p):
            g = g + p_ref[j].astype(F32)
        mn = ADAM_B1 * m_ref[...] + (1.0 - ADAM_B1) * g
        vn = ADAM_B2 * v_ref[...] + (1.0 - ADAM_B2) * jnp.square(g)
        m_hat = mn / (1.0 - ADAM_B1 ** ADAM_STEP)
        v_hat = vn / (1.0 - ADAM_B2 ** ADAM_STEP)
        g_out[...] = g
        d_out[...] = -ADAM_LR * (m_hat / (jnp.sqrt(v_hat) + ADAM_EPS) + ADAM_WD * w_ref[...])
        m_out[...] = mn
        v_out[...] = vn

    spec = pl.BlockSpec((tr, c), lambda i: (i, 0))
    return pl.pallas_call(
        body, name=name, grid=(r // tr,),
        in_specs=[pl.BlockSpec((p, tr, c), lambda i: (0, i, 0)), spec, spec, spec],
        out_specs=[spec] * 4, out_shape=[SDS((r, c), F32)] * 4, compiler_params=_cp(("parallel",)))(parts, w, m, v)


def matmul(a, b, *, mode, out_dtype, name, tm=1024, tn=1024, tk=512, a_relu2=False, resid=None, relu2_grad_of=None,
           b_cols=False, relu2_out=False, rms_gain=None, side=None, tail=None):
    if b_cols:
        nb, brows, bc = b.shape
        bshape = (brows, nb * bc)
    else:
        bshape = b.shape
    head_k = head_n = None
    if mode == "nn":
        (m, k), (k2, n) = a.shape, bshape
    elif mode == "nt":
        (m, k), (n, k2) = a.shape, bshape
        if tail is not None:
            head_k, k = k, k + tail.shape[1]
    else:
        (k, m), (k2, n) = a.shape, bshape
        if tail is not None:
            head_n, n = n, n + tail.shape[1]
    assert k == k2 and (tail is None or mode != "nn"), (a.shape, b.shape, mode)
    tm, tn, tk = _tile(m, tm), _tile(n, tn), _tile(k, tk)
    if b_cols and mode == "nn":
        tn = _tile(bc, tn)
    if b_cols and mode == "nt":
        tk = _tile(bc, tk)
    nk = k // tk
    ni, nj = m // tm, n // tn
    nk_head = _div(head_k, tk) if head_k is not None else None
    nj_head = _div(head_n, tn) if head_n is not None else None
    use_acc = nk > 1 or tail is not None
    dims = {"nn": NN, "nt": NT, "tn": TN}[mode]
    extras = [e for e in (resid, relu2_grad_of) if e is not None]
    tails = [tail] if tail is not None else []
    n_side = len(side.ins) if side is not None else 0
    gains = [rms_gain] if rms_gain is not None else []
    assert not gains or (tn == n and not relu2_out), (tn, n)
    n_main = 2 if (relu2_out or gains) else 1

    def body(*refs):
        a_ref, b_ref = refs[:2]
        t_ref = refs[2] if tails else None
        n_op = 2 + len(tails)
        e_refs = refs[n_op:n_op + len(extras)]
        n_pre = n_op + len(extras) + len(gains)
        g_ref = refs[n_pre - 1] if gains else None
        n_in = n_pre + n_side
        o_ref = refs[n_in]
        act_ref = refs[n_in + 1] if n_main == 2 else None
        acc_ref = refs[n_in + n_main + n_side] if use_acc else None
        if side is not None:
            s_ins = refs[n_pre:n_in]
            s_outs = refs[n_in + n_main:n_in + n_main + n_side]
            s_sems = refs[len(refs) - len(side.scratch):]
            step = (pl.program_id(0) * nj + pl.program_id(1)) * nk + pl.program_id(2)
            _host_begin(side, step, ni * nj * nk, s_ins, s_outs, s_sems)

        def operands(a_from=a_ref, b_from=b_ref):
            av = a_from[...]
            if a_relu2:
                av = jnp.square(jnp.maximum(av.astype(F32), 0.0))
            return av.astype(BF16), b_from[...].astype(BF16)

        def finish(r):
            e = list(e_refs)
            if resid is not None:
                r = r + e.pop(0)[...]
            if relu2_grad_of is not None:
                r = r * (2.0 * jnp.maximum(e.pop(0)[...], 0.0))
            o_ref[...] = r.astype(out_dtype)
            if relu2_out:
                act_ref[...] = jnp.square(jnp.maximum(r, 0.0)).astype(BF16)
            if gains:
                inv = lax.rsqrt(jnp.mean(r * r, axis=-1, keepdims=True) + RMS_EPS)
                act_ref[...] = (r * inv * g_ref[...]).astype(BF16)

        if not use_acc:
            av, bv = operands()
            finish(_dot(av, bv, dims))
        else:
            kk = pl.program_id(2)

            def accumulate(a_from, b_from):
                def product():
                    av, bv = operands(a_from, b_from)
                    return _dot(av, bv, dims)

                if nk == 1:
                    finish(product())
                    return

                @pl.when(kk == 0)
                def _():
                    acc_ref[...] = product()

                @pl.when((kk > 0) & (kk < nk - 1))
                def _():
                    acc_ref[...] += product()

                @pl.when(kk == nk - 1)
                def _():
                    finish(acc_ref[...] + product())

            if not tails:
                accumulate(a_ref, b_ref)
            elif mode == "nt":
                pl.when(kk < nk_head)(lambda: accumulate(a_ref, b_ref))
                pl.when(kk >= nk_head)(lambda: accumulate(t_ref, b_ref))
            else:
                in_head = pl.program_id(1) < nj_head
                pl.when(in_head)(lambda: accumulate(a_ref, b_ref))
                pl.when(jnp.logical_not(in_head))(lambda: accumulate(a_ref, t_ref))

        if side is not None:
            pl.when(step == ni * nj * nk - 1)(lambda: side.finish(s_ins, s_outs, s_sems))

    a_spec = {"nn": pl.BlockSpec((tm, tk), lambda i, j, kk: (i, kk)),
              "nt": pl.BlockSpec((tm, tk), lambda i, j, kk: (i, kk)),
              "tn": pl.BlockSpec((tk, tm), lambda i, j, kk: (kk, i))}[mode]
    t_specs = []
    if tails and mode == "nt":
        a_spec = pl.BlockSpec((tm, tk), lambda i, j, kk: (i, jnp.minimum(kk, nk_head - 1)))
        t_specs = [pl.BlockSpec((tm, tk), lambda i, j, kk: (i, jnp.maximum(kk - nk_head, 0)))]
    if tails and mode == "tn":
        t_specs = [pl.BlockSpec((tk, tn), lambda i, j, kk: (kk, jnp.maximum(j - nj_head, 0)))]
    if tails and mode == "tn":
        b_spec = pl.BlockSpec((tk, tn), lambda i, j, kk: (kk, jnp.minimum(j, nj_head - 1)))
    elif not b_cols:
        b_spec = {"nn": pl.BlockSpec((tk, tn), lambda i, j, kk: (kk, j)),
                  "nt": pl.BlockSpec((tn, tk), lambda i, j, kk: (j, kk)),
                  "tn": pl.BlockSpec((tk, tn), lambda i, j, kk: (kk, j))}[mode]
    elif mode == "nn":
        per = bc // tn
        b_spec = pl.BlockSpec((None, tk, tn), lambda i, j, kk: (j // per, kk, j % per))
    else:
        assert mode == "nt", mode
        per = bc // tk
        b_spec = pl.BlockSpec((None, tn, tk), lambda i, j, kk: (kk // per, j, kk % per))
    e_spec = pl.BlockSpec((tm, tn), lambda i, j, kk: (i, j))
    main_shapes = [SDS((m, n), out_dtype)] + ([SDS((m, n), BF16)] if n_main == 2 else [])
    g_specs = [pl.BlockSpec((1, tn), lambda i, j, kk: (0, j))] * len(gains)
    res = pl.pallas_call(
        body, name=name, grid=(ni, nj, nk),
        in_specs=[a_spec, b_spec] + t_specs + [e_spec] * len(extras) + g_specs + [ANY] * n_side,
        out_specs=[e_spec] * n_main + [ANY] * n_side, out_shape=main_shapes + (side.out_shapes if side is not None else []),
        scratch_shapes=([pltpu.VMEM((tm, tn), F32)] if use_acc else []) + (side.scratch if side is not None else []),
        compiler_params=_cp(("arbitrary", "arbitrary", "arbitrary")))(a, b, *tails, *extras, *gains, *(side.ins if side is not None else []))
    return res if len(res) > 1 else res[0]


def rmsnorm_fwd(x, g, name, side=None):
    s, d = x.shape
    tm = _tile(s, 256)
    steps = s // tm
    n_side = len(side.ins) if side is not None else 0

    def body(*refs):
        x_ref, g_ref, o_ref = refs[0], refs[1], refs[2 + n_side]
        if side is not None:
            s_ins, s_outs, s_sems = refs[2:2 + n_side], refs[3 + n_side:3 + 2 * n_side], refs[3 + 2 * n_side:]
            _host_begin(side, pl.program_id(0), steps, s_ins, s_outs, s_sems)
        xv = x_ref[...]
        r = lax.rsqrt(jnp.mean(xv * xv, axis=-1, keepdims=True) + RMS_EPS)
        o_ref[...] = (xv * r * g_ref[...]).astype(BF16)
        if side is not None:
            pl.when(pl.program_id(0) == steps - 1)(lambda: side.finish(s_ins, s_outs, s_sems))

    row = pl.BlockSpec((tm, d), lambda i: (i, 0))
    res = pl.pallas_call(
        body, name=name, grid=(steps,), in_specs=[row, pl.BlockSpec((1, d), lambda i: (0, 0))] + [ANY] * n_side,
        out_specs=[row] + [ANY] * n_side, out_shape=[SDS((s, d), BF16)] + (side.out_shapes if side is not None else []),
        scratch_shapes=side.scratch if side is not None else [],
        compiler_params=_cp(("arbitrary",)))(x, g, *(side.ins if side is not None else []))
    return res if side is not None else res[0]


def _rms_bwd_rows(xv, gv, dy):
    r = lax.rsqrt(jnp.mean(xv * xv, axis=-1, keepdims=True) + RMS_EPS)
    xh = xv * r
    dxh = dy * gv
    dx = r * (dxh - xh * jnp.mean(dxh * xh, axis=-1, keepdims=True))
    return dx, jnp.sum(dy * xh, axis=0, keepdims=True)


def rmsnorm_bwd(x, g, dn, resid, name, bf16_copy=False):
    s, d = x.shape
    tm = _tile(s, 256)
    has_r = resid is not None

    def body(*refs):
        x_ref, g_ref, dn_ref = refs[:3]
        dx_ref, dg_ref = refs[3 + has_r:5 + has_r]
        dx, part = _rms_bwd_rows(x_ref[...], g_ref[...], dn_ref[...].astype(F32))
        if has_r:
            dx = dx + refs[3][...]
        dx_ref[...] = dx
        if bf16_copy:
            refs[5 + has_r][...] = dx.astype(BF16)

        @pl.when(pl.program_id(0) == 0)
        def _():
            dg_ref[...] = jnp.zeros_like(dg_ref)

        dg_ref[...] += part

    row = pl.BlockSpec((tm, d), lambda i: (i, 0))
    vec = pl.BlockSpec((1, d), lambda i: (0, 0))
    ins = [x, g, dn] + ([resid] if has_r else [])
    return pl.pallas_call(body, name=name, grid=(s // tm,), in_specs=[row, vec, row] + ([row] if has_r else []),
                          out_specs=[row, vec] + ([row] if bf16_copy else []),
                          out_shape=[SDS((s, d), F32), SDS((1, d), F32)] + ([SDS((s, d), BF16)] if bf16_copy else []),
                          compiler_params=_cp(("arbitrary",)))(*ins)


def final_norm_loss(h, g, tgt, name):
    s, d = h.shape
    tm = _tile(s, 256)

    def body(h_ref, g_ref, t_ref, dh_ref, dg_ref, l_ref, dhb_ref):
        xv, gv = h_ref[...], g_ref[...]
        r = lax.rsqrt(jnp.mean(xv * xv, axis=-1, keepdims=True) + RMS_EPS)
        e = xv * r * gv - t_ref[...]
        lpart = 0.5 * jnp.sum(jnp.mean(e * e, axis=-1, keepdims=True), axis=0, keepdims=True)
        dx, part = _rms_bwd_rows(xv, gv, e * (1.0 / d))
        dh_ref[...] = dx
        dhb_ref[...] = dx.astype(BF16)

        @pl.when(pl.program_id(0) == 0)
        def _():
            dg_ref[...] = jnp.zeros_like(dg_ref)
            l_ref[...] = jnp.zeros_like(l_ref)

        dg_ref[...] += part
        l_ref[...] += jnp.broadcast_to(lpart, l_ref.shape)

    row = pl.BlockSpec((tm, d), lambda i: (i, 0))
    vec = pl.BlockSpec((1, d), lambda i: (0, 0))
    lsp = pl.BlockSpec((1, LANE), lambda i: (0, 0))
    return pl.pallas_call(body, name=name, grid=(s // tm,), in_specs=[row, vec, row], out_specs=[row, vec, lsp, row],
                          out_shape=[SDS((s, d), F32), SDS((1, d), F32), SDS((1, LANE), F32), SDS((s, d), BF16)],
                          compiler_params=_cp(("arbitrary",)))(h, g, tgt)


def merge(p_all, ya, yb, yc, wa, wb, wc, dm, lay, name):
    s, d = ya.shape[0], lay.d
    wcols = wa.shape[2]
    bwd = dm is not None
    tm, tn = _tile(s, 2048), _tile(wcols, 512)
    nj, per = d // tn, wcols // tn

    y_specs = [pl.BlockSpec((tm, y.shape[1]), lambda i, j, *_: (i, 0)) for y in (ya, yb, yc)]
    w_specs = [pl.BlockSpec((None, w.shape[1], tn), lambda i, j, *_: (j // per, 0, j % per)) for w in (wa, wb, wc)]
    o_spec = pl.BlockSpec((tm, tn), lambda i, j, *_: (i, j))
    if not bwd:
        def body(ga, gb, gc, ya_r, yb_r, yc_r, wa_r, wb_r, wc_r, o_ref):
            ts = [_dot(y[...], w[...], NN) for y, w in ((ya_r, wa_r), (yb_r, wb_r), (yc_r, wc_r))]
            gs = [_sigmoid(g[...]) for g in (ga, gb, gc)]
            o_ref[...] = (gs[0] * ts[0] + gs[1] * ts[1] + gs[2] * ts[2]).astype(BF16)

        gate_specs = [pl.BlockSpec((tm, tn), lambda i, j, b=b: (i, b * nj + j)) for b in range(3)]
        return pl.pallas_call(
            body, name=name, grid=(s // tm, nj), in_specs=gate_specs + y_specs + w_specs,
            out_specs=o_spec, out_shape=SDS((s, d), BF16),
            compiler_params=_cp(("parallel", "parallel")))(p_all, p_all, p_all, ya, yb, yc, wa, wb, wc)

    def body_bwd(g_r, ya_r, yb_r, yc_r, wa_r, wb_r, wc_r, dm_r, dg_o, dta_o, dtb_o, dtc_o):
        for k, (y, w, dt_o) in enumerate(((ya_r, wa_r, dta_o), (yb_r, wb_r, dtb_o), (yc_r, wc_r, dtc_o))):
            @pl.when(pl.program_id(2) == k)
            def _(y=y, w=w, dt_o=dt_o):
                t = _dot(y[...], w[...], NN)
                g = _sigmoid(g_r[...])
                dmv = dm_r[...]
                dg_o[...] = (dmv * t * (g * (1.0 - g))).astype(BF16)
                dt_o[...] = (dmv * g).astype(BF16)

    gate_spec = pl.BlockSpec((tm, tn), lambda i, j, b: (i, b * nj + j))
    return pl.pallas_call(
        body_bwd, name=name, grid=(s // tm, nj, 3), in_specs=[gate_spec] + y_specs + w_specs + [o_spec],
        out_specs=[gate_spec, o_spec, o_spec, o_spec], out_shape=[SDS((s, 3 * d), BF16)] + [SDS((s, d), BF16)] * 3,
        compiler_params=_cp(("arbitrary", "arbitrary", "arbitrary")))(p_all, ya, yb, yc, wa, wb, wc, dm)


SWA_PAIRS = SWA_G // 2


def _swa_probs(qs, kcs, sinks, first):
    shape = (qs[0].shape[0], 2 * SWA_W)
    qi = lax.broadcasted_iota(jnp.int32, shape, 0) % SWA_W
    kj = lax.broadcasted_iota(jnp.int32, shape, 1)
    mask = (kj > qi) & (kj <= qi + SWA_W) & ((kj >= SWA_W) | jnp.logical_not(first))
    ss = [jnp.where(mask, _dot(q, kc, NT) * (SWA_HD ** -0.5), NEG) for q, kc in zip(qs, kcs)]
    ms = [jnp.maximum(jnp.max(s, axis=-1, keepdims=True), sink) for s, sink in zip(ss, sinks)]
    ps = [jnp.exp(s - m) for s, m in zip(ss, ms)]
    es = [jnp.exp(sink - m) for sink, m in zip(sinks, ms)]
    inv = [1.0 / (jnp.sum(p, axis=-1, keepdims=True) + e) for p, e in zip(ps, es)]
    return [p * i for p, i in zip(ps, inv)], [e * i for e, i in zip(es, inv)]


def _swa_stack(ref, h):
    return jnp.concatenate([ref[:, pl.ds((h * SWA_PAIRS + p) * LANE, LANE)] for p in range(SWA_PAIRS)], axis=0)


def _swa_unstack(ref, h, val):
    for p in range(SWA_PAIRS):
        ref[:, pl.ds((h * SWA_PAIRS + p) * LANE, LANE)] = val[p * SWA_W:(p + 1) * SWA_W]


def _swa_sink_col(sk_ref, h, second):
    pair = lax.broadcasted_iota(jnp.int32, (SWA_PAIRS * SWA_W, 1), 0) // SWA_W
    col = jnp.zeros((SWA_PAIRS * SWA_W, 1), F32)
    for p in range(SWA_PAIRS):
        hh = h * SWA_G + 2 * p + second
        col = jnp.where(pair == p, sk_ref[0:1, hh:hh + 1], col)
    return col


def _swa_kv_tiles(cur_ref, prev_ref, h):
    t = jnp.concatenate([prev_ref[...], cur_ref[...]], axis=0)
    lane = lax.broadcasted_iota(jnp.int32, t.shape, 1)
    moved = pltpu.roll(t, SWA_HD, axis=1)
    low, high = (t, moved) if h == 0 else (moved, t)
    return jnp.where(lane < SWA_HD, low, 0.0).astype(BF16), jnp.where(lane >= SWA_HD, high, 0.0).astype(BF16)


def _swa_kv_grad(g_low, g_high, h):
    lane = lax.broadcasted_iota(jnp.int32, g_low.shape, 1)
    if h == 0:
        return jnp.where(lane < SWA_HD, g_low + pltpu.roll(g_high, SWA_HD, axis=1), 0.0)
    return jnp.where(lane >= SWA_HD, pltpu.roll(g_low, SWA_HD, axis=1) + g_high, 0.0)


def _swa_specs(lay):
    w = SWA_W
    q_spec = pl.BlockSpec((w, Q_W), lambda n: (n, _div(lay.q, Q_W)))
    cur = lambda off: pl.BlockSpec((w, KV_W), lambda n: (n, _div(off, KV_W)))
    prev = lambda off: pl.BlockSpec((w, KV_W), lambda n: (jnp.maximum(n - 1, 0), _div(off, KV_W)))
    return q_spec, cur(lay.k), prev(lay.k), cur(lay.v), prev(lay.v)


def swa_fwd(p_all, sinks, lay, name):
    s = p_all.shape[0]
    nb = _div(s, SWA_W)

    def body(q_ref, kc_ref, kp_ref, vc_ref, vp_ref, sk_ref, o_ref):
        first = pl.program_id(0) == 0
        units = [(h, e) for h in range(SWA_HKV) for e in range(2)]
        ks = [_swa_kv_tiles(kc_ref, kp_ref, h) for h in range(SWA_HKV)]
        vs = [_swa_kv_tiles(vc_ref, vp_ref, h) for h in range(SWA_HKV)]
        qs = [_swa_stack(q_ref, h).astype(BF16) for h in range(SWA_HKV)]
        ps, _ = _swa_probs([qs[h] for h, e in units], [ks[h][e] for h, e in units],
                           [_swa_sink_col(sk_ref, h, e) for h, e in units], first)
        os = [_dot(p.astype(BF16), vs[h][e], NN) for p, (h, e) in zip(ps, units)]
        for h in range(SWA_HKV):
            _swa_unstack(o_ref, h, (os[2 * h] + os[2 * h + 1]).astype(BF16))

    q_spec, kc_s, kp_s, vc_s, vp_s = _swa_specs(lay)
    return pl.pallas_call(
        body, name=name, grid=(nb,),
        in_specs=[q_spec, kc_s, kp_s, vc_s, vp_s, pl.BlockSpec(sinks.shape, lambda n: (0, 0))],
        out_specs=pl.BlockSpec((SWA_W, Q_W), lambda n: (n, 0)), out_shape=SDS((s, Q_W), BF16),
        compiler_params=_cp(("parallel",)))(p_all, p_all, p_all, p_all, p_all, sinks)


def swa_bwd(p_all, sinks, dy, lay, name):
    s = p_all.shape[0]
    nb = _div(s, SWA_W)
    w = SWA_W

    def body(q_ref, kc_ref, kp_ref, vc_ref, vp_ref, sk_ref, do_ref, dq_ref, dk_ref, dv_ref, ds_ref, kcar, vcar):
        n = pl.program_id(0)
        first = n == 0

        @pl.when(first)
        def _():
            kcar[...] = jnp.zeros_like(kcar)
            vcar[...] = jnp.zeros_like(vcar)
            ds_ref[...] = jnp.zeros_like(ds_ref)

        @pl.when(n < nb)
        def _():
            lane = lax.broadcasted_iota(jnp.int32, (1, LANE), 1)
            dsink = jnp.zeros((1, LANE), F32)
            units = [(h, e) for h in range(SWA_HKV) for e in range(2)]
            ks = [_swa_kv_tiles(kc_ref, kp_ref, h) for h in range(SWA_HKV)]
            vs = [_swa_kv_tiles(vc_ref, vp_ref, h) for h in range(SWA_HKV)]
            qs = [_swa_stack(q_ref, h).astype(BF16) for h in range(SWA_HKV)]
            dos = [_swa_stack(do_ref, h).astype(BF16) for h in range(SWA_HKV)]
            ps, psinks = _swa_probs([qs[h] for h, e in units], [ks[h][e] for h, e in units],
                                    [_swa_sink_col(sk_ref, h, e) for h, e in units], first)
            dps = [_dot(dos[h], vs[h][e], NT) for h, e in units]
            dvs = [_dot(p.astype(BF16), dos[h], TN) for p, (h, e) in zip(ps, units)]
            rss = [jnp.sum(dp * p, axis=-1, keepdims=True) for dp, p in zip(dps, ps)]
            dsb = [(p * (dp - rs) * (SWA_HD ** -0.5)).astype(BF16) for p, dp, rs in zip(ps, dps, rss)]
            dqs = [_dot(d, ks[h][e], NN) for d, (h, e) in zip(dsb, units)]
            dks = [_dot(d, qs[h], TN) for d, (h, e) in zip(dsb, units)]
            for u, (h, e) in enumerate(units):
                psr = psinks[u] * rss[u]
                for pr in range(SWA_PAIRS):
                    hh = h * SWA_G + 2 * pr + e
                    dsink = dsink + jnp.where(lane == hh, -jnp.sum(psr[pr * w:(pr + 1) * w], axis=0, keepdims=True), 0.0)
            dk_tile = jnp.zeros((2 * w, KV_W), F32)
            dv_tile = jnp.zeros((2 * w, KV_W), F32)
            for h in range(SWA_HKV):
                _swa_unstack(dq_ref, h, (dqs[2 * h] + dqs[2 * h + 1]).astype(BF16))
                dk_tile = dk_tile + _swa_kv_grad(dks[2 * h], dks[2 * h + 1], h)
                dv_tile = dv_tile + _swa_kv_grad(dvs[2 * h], dvs[2 * h + 1], h)
            dk_ref[...] = (kcar[...] + dk_tile[:w]).astype(BF16)
            dv_ref[...] = (vcar[...] + dv_tile[:w]).astype(BF16)
            kcar[...] = dk_tile[w:]
            vcar[...] = dv_tile[w:]
            ds_ref[...] += dsink

        @pl.when(n == nb)
        def _():
            dk_ref[...] = kcar[...].astype(BF16)
            dv_ref[...] = vcar[...].astype(BF16)

    last = nb - 1
    q_spec = pl.BlockSpec((w, Q_W), lambda n: (jnp.minimum(n, last), _div(lay.q, Q_W)))
    cur = lambda off: pl.BlockSpec((w, KV_W), lambda n: (jnp.minimum(n, last), _div(off, KV_W)))
    prev = lambda off: pl.BlockSpec((w, KV_W), lambda n: (jnp.clip(n - 1, 0, last), _div(off, KV_W)))
    row = pl.BlockSpec((w, Q_W), lambda n: (jnp.minimum(n, last), 0))
    kv_out = pl.BlockSpec((w, KV_W), lambda n: (jnp.maximum(n - 1, 0), 0))
    return pl.pallas_call(
        body, name=name, grid=(nb + 1,),
        in_specs=[q_spec, cur(lay.k), prev(lay.k), cur(lay.v), prev(lay.v), pl.BlockSpec(sinks.shape, lambda n: (0, 0)), row],
        out_specs=[row, kv_out, kv_out, pl.BlockSpec((1, LANE), lambda n: (0, 0))],
        out_shape=[SDS((s, Q_W), BF16), SDS((s, KV_W), BF16), SDS((s, KV_W), BF16), SDS((1, LANE), F32)],
        scratch_shapes=[pltpu.VMEM((w, KV_W), F32), pltpu.VMEM((w, KV_W), F32)],
        compiler_params=_cp(("arbitrary",)))(p_all, p_all, p_all, p_all, p_all, sinks, dy)


def _xa_probs(qs, mks):
    ss = [_dot(q, mk, NT) * (XA_D ** -0.5) for q, mk in zip(qs, mks)]
    ps = [jnp.exp(s - jnp.max(s, axis=-1, keepdims=True)) for s in ss]
    inv = [1.0 / jnp.sum(p, axis=-1, keepdims=True) for p in ps]
    return [p * i for p, i in zip(ps, inv)]


def xattn_fwd(p_all, mkv, lay, name):
    s, nm = p_all.shape[0], mkv.shape[0]
    tm = _tile(s, 512)

    def body(q_ref, mkv_ref, o_ref):
        heads = range(XA_H)
        cols = [pl.ds(h * XA_D, XA_D) for h in heads]
        ps = _xa_probs([q_ref[:, c].astype(BF16) for c in cols], [mkv_ref[:, c] for c in cols])
        os = [_dot(ps[h].astype(BF16), mkv_ref[:, pl.ds(XA_W + h * XA_D, XA_D)], NN) for h in heads]
        for h in heads:
            o_ref[:, cols[h]] = os[h].astype(BF16)

    return pl.pallas_call(
        body, name=name, grid=(s // tm,),
        in_specs=[pl.BlockSpec((tm, XA_W), lambda i: (i, _div(lay.qc, XA_W))), pl.BlockSpec((nm, 2 * XA_W), lambda i: (0, 0))],
        out_specs=pl.BlockSpec((tm, XA_W), lambda i: (i, 0)), out_shape=SDS((s, XA_W), BF16),
        compiler_params=_cp(("parallel",)))(p_all, mkv)


def xattn_bwd(p_all, mkv, dy, lay, name):
    s, nm = p_all.shape[0], mkv.shape[0]
    tm = _tile(s, 512)

    def body(q_ref, mkv_ref, do_ref, dq_ref, dmkv_ref):
        @pl.when(pl.program_id(0) == 0)
        def _():
            dmkv_ref[...] = jnp.zeros_like(dmkv_ref)

        heads = range(XA_H)
        cols = [pl.ds(h * XA_D, XA_D) for h in heads]
        vcols = [pl.ds(XA_W + h * XA_D, XA_D) for h in heads]
        qs = [q_ref[:, c].astype(BF16) for c in cols]
        dos = [do_ref[:, c].astype(BF16) for c in cols]
        ps = _xa_probs(qs, [mkv_ref[:, c] for c in cols])
        dps = [_dot(dos[h], mkv_ref[:, vcols[h]], NT) for h in heads]
        dvs = [_dot(ps[h].astype(BF16), dos[h], TN) for h in heads]
        dsb = [(p * (dp - jnp.sum(dp * p, axis=-1, keepdims=True)) * (XA_D ** -0.5)).astype(BF16) for p, dp in zip(ps, dps)]
        dqs = [_dot(dsb[h], mkv_ref[:, cols[h]], NN) for h in heads]
        dks = [_dot(dsb[h], qs[h], TN) for h in heads]
        for h in heads:
            dq_ref[:, cols[h]] = dqs[h].astype(BF16)
            dmkv_ref[:, vcols[h]] += dvs[h]
            dmkv_ref[:, cols[h]] += dks[h]

    row = pl.BlockSpec((tm, XA_W), lambda i: (i, 0))
    full = pl.BlockSpec((nm, 2 * XA_W), lambda i: (0, 0))
    return pl.pallas_call(
        body, name=name, grid=(s // tm,),
        in_specs=[pl.BlockSpec((tm, XA_W), lambda i: (i, _div(lay.qc, XA_W))), full, row],
        out_specs=[row, full], out_shape=[SDS((s, XA_W), BF16), SDS((nm, 2 * XA_W), F32)],
        compiler_params=_cp(("arbitrary",)))(p_all, mkv, dy)


def _shift_down(cur, prev8, s):
    cat = jnp.concatenate([prev8, cur[0:8]], axis=0)
    return pltpu.roll(cur, s, axis=0), pltpu.roll(cat, s, axis=0)[8:16]


def _shift_up(cur, next8, s):
    tm = cur.shape[0]
    cat = jnp.concatenate([cur[tm - 8:tm], next8], axis=0)
    return pltpu.roll(cur, tm - s, axis=0), pltpu.roll(cat, 16 - s, axis=0)[0:8]


def gdn_conv_fwd(p_all, conv_w, lay, name):
    s = p_all.shape[0]
    tm = _tile(s, 512)
    c0 = _div(lay.qkv, GDN_W)

    def body(x_ref, prev_ref, w_ref, o_ref):
        cur = x_ref[...]
        prev8 = jnp.where(pl.program_id(1) > 0, prev_ref[...], 0.0)
        main = w_ref[GDN_CONV - 1:GDN_CONV, :] * cur
        top = w_ref[GDN_CONV - 1:GDN_CONV, :] * cur[0:8]
        for sft in range(1, GDN_CONV):
            wi = w_ref[GDN_CONV - 1 - sft:GDN_CONV - sft, :]
            a, b = _shift_down(cur, prev8, sft)
            main = main + wi * a
            top = top + wi * b
        o_ref[...] = main
        o_ref[0:8, :] = top

    return pl.pallas_call(
        body, name=name, grid=(3, s // tm),
        in_specs=[pl.BlockSpec((tm, GDN_W), lambda c, i: (i, c0 + c)),
                  pl.BlockSpec((8, GDN_W), lambda c, i: (jnp.maximum(i * (tm // 8) - 1, 0), c0 + c)),
                  pl.BlockSpec((GDN_CONV, GDN_W), lambda c, i: (0, c))],
        out_specs=pl.BlockSpec((tm, GDN_W), lambda c, i: (i, c)), out_shape=SDS((s, 3 * GDN_W), F32),
        compiler_params=_cp(("parallel", "parallel")))(p_all, p_all, conv_w)


def gdn_conv_bwd(p_all, conv_w, dxc, lay, name):
    s = p_all.shape[0]
    tm = _tile(s, 512)
    c0 = _div(lay.qkv, GDN_W)
    nt = s // tm

    def body(x_ref, prev_ref, d_ref, next_ref, w_ref, dx_ref, dw_ref):
        i = pl.program_id(1)
        cur, d = x_ref[...], d_ref[...]
        prev8 = jnp.where(i > 0, prev_ref[...], 0.0)
        next8 = jnp.where(i < nt - 1, next_ref[...], 0.0)
        row = lax.broadcasted_iota(jnp.int32, (tm, 1), 0)
        main = w_ref[GDN_CONV - 1:GDN_CONV, :] * d
        bot = w_ref[GDN_CONV - 1:GDN_CONV, :] * d[tm - 8:tm]
        dws = [jnp.sum(d * cur, axis=0, keepdims=True)]
        for sft in range(1, GDN_CONV):
            wi = w_ref[GDN_CONV - 1 - sft:GDN_CONV - sft, :]
            a, b = _shift_up(d, next8, sft)
            main = main + wi * a
            bot = bot + wi * b
            xa, xb = _shift_down(cur, prev8, sft)
            dws.append(jnp.sum(jnp.where(row >= 8, d * xa, 0.0), axis=0, keepdims=True)
                       + jnp.sum(d[0:8] * xb, axis=0, keepdims=True))
        dx_ref[...] = main.astype(BF16)
        dx_ref[tm - 8:tm, :] = bot.astype(BF16)

        @pl.when(i == 0)
        def _():
            dw_ref[...] = jnp.zeros_like(dw_ref)

        for sft in range(GDN_CONV):
            dw_ref[GDN_CONV - 1 - sft:GDN_CONV - sft, :] += dws[sft]

    return pl.pallas_call(
        body, name=name, grid=(3, nt),
        in_specs=[pl.BlockSpec((tm, GDN_W), lambda c, i: (i, c0 + c)),
                  pl.BlockSpec((8, GDN_W), lambda c, i: (jnp.maximum(i * (tm // 8) - 1, 0), c0 + c)),
                  pl.BlockSpec((tm, GDN_W), lambda c, i: (i, c)),
                  pl.BlockSpec((8, GDN_W), lambda c, i: (jnp.minimum((i + 1) * (tm // 8), s // 8 - 1), c)),
                  pl.BlockSpec((GDN_CONV, GDN_W), lambda c, i: (0, c))],
        out_specs=[pl.BlockSpec((tm, GDN_W), lambda c, i: (i, c)), pl.BlockSpec((GDN_CONV, GDN_W), lambda c, i: (0, c))],
        out_shape=[SDS((s, 3 * GDN_W), BF16), SDS((GDN_CONV, 3 * GDN_W), F32)],
        compiler_params=_cp(("parallel", "arbitrary")))(p_all, p_all, dxc, dxc, conv_w)


def _gdn_chunk(xq, xk, xv, ab, gp, bdot=_bdot_plain):
    c = GDN_C
    nc = xq.shape[0] // c
    lane = lax.broadcasted_iota(jnp.int32, (c, LANE), 1)
    row = lax.broadcasted_iota(jnp.int32, (c, c), 0)
    col = lax.broadcasted_iota(jnp.int32, (c, c), 1)
    g_tile = -jnp.exp(gp[0:1, :]) * _softplus(ab + gp[1:2, :])
    b_tile = _sigmoid(ab)
    tri = (row >= col).astype(F32)
    qa, ka, va = _silu(xq), _silu(xk), _silu(xv)
    items = []
    for ci in range(nc):
        rs = slice(ci * c, (ci + 1) * c)
        gcum = _dot(tri, g_tile[rs], NN, HI)
        gcum_t = gcum.T
        for h in range(GDN_H):
            hs = slice(h * GDN_D, (h + 1) * GDN_D)
            q, k, v = qa[rs, hs], ka[rs, hs], va[rs, hs]
            q = q * lax.rsqrt(jnp.sum(q * q, axis=-1, keepdims=True) + L2_EPS) * (GDN_D ** -0.5)
            k = k * lax.rsqrt(jnp.sum(k * k, axis=-1, keepdims=True) + L2_EPS)
            gc = jnp.sum(jnp.where(lane == h, gcum, 0.0), axis=1, keepdims=True)
            beta = jnp.sum(jnp.where(lane == GDN_H + h, b_tile[rs], 0.0), axis=1, keepdims=True)
            decay = jnp.exp(jnp.where(row >= col, gc - gcum_t[h:h + 1, :], NEG))
            items.append((q, k, v, gc, beta, decay))
    kks = [bdot(k, k, NT) for (_, k, _, _, _, _) in items]
    xs = tuple(-jnp.where(row > col, it[4] * kk * it[5], 0.0) for it, kk in zip(items, kks))
    nns = _neumann(xs) if bdot is _bdot_plain else _neumann_vjp(xs)
    qks = [bdot(q, k, NT) for (q, k, _, _, _, _) in items]
    out = []
    for (q, k, v, gc, beta, decay), n, qk in zip(items, nns, qks):
        eg = jnp.exp(gc)
        vb = v * beta
        kbe = k * (beta * eg)
        gl = gc[c - 1:c, :]
        out.append((vb + bdot(n, vb, NN), kbe + bdot(n, kbe, NN), q * eg, k * jnp.exp(gl - gc), qk * decay, jnp.exp(gl)))
    return [out[ci * GDN_H:(ci + 1) * GDN_H] for ci in range(nc)]


GDN_CPS = 4


def _gdn_pre_specs(lay, t):
    xspec = lambda j: pl.BlockSpec((t, GDN_W), lambda n, j=j: (n, j))
    return [xspec(0), xspec(1), xspec(2), pl.BlockSpec((t, LANE), lambda n: (n, _div(lay.ab, LANE))),
            pl.BlockSpec((8, LANE), lambda n: (0, 0))]


def gdn_pre_fwd(xc, p_all, gp, lay, name):
    s = xc.shape[0]
    c = GDN_C
    n = _div(s, c)
    cps = _tile(n, GDN_CPS)
    t = cps * c

    def body(xq, xk, xv, ab, gp_ref, u_ref, w_ref, qd_ref, kd_ref, qk_ref, gl_ref):
        lane = lax.broadcasted_iota(jnp.int32, (1, LANE), 1)
        chunks = _gdn_chunk(xq[...], xk[...], xv[...], ab[...], gp_ref[...])
        for ci, heads in enumerate(chunks):
            rs = pl.ds(ci * c, c)
            gl_row = jnp.zeros((1, LANE), F32)
            for h, (u, w, qd, kd, qk, gl) in enumerate(heads):
                hs = pl.ds(h * GDN_D, GDN_D)
                u_ref[rs, hs] = u
                w_ref[rs, hs] = w.astype(BF16)
                qd_ref[rs, hs] = qd.astype(BF16)
                kd_ref[rs, hs] = kd.astype(BF16)
                qk_ref[rs, pl.ds(h * c, c)] = qk.astype(BF16)
                gl_row = gl_row + jnp.where(lane == h, gl, 0.0)
            gl_ref[ci] = gl_row

    row = pl.BlockSpec((t, GDN_W), lambda n: (n, 0))
    return pl.pallas_call(
        body, name=name, grid=(n // cps,), in_specs=_gdn_pre_specs(lay, t),
        out_specs=[row, row, row, row, pl.BlockSpec((t, GDN_H * c), lambda n: (n, 0)), pl.BlockSpec((cps, 1, LANE), lambda n: (n, 0, 0))],
        out_shape=[SDS((s, GDN_W), F32), SDS((s, GDN_W), BF16), SDS((s, GDN_W), BF16), SDS((s, GDN_W), BF16),
                   SDS((s, GDN_H * c), BF16), SDS((n, 1, LANE), F32)],
        compiler_params=_cp(("parallel",)))(xc, xc, xc, p_all, gp)


def gdn_pre_bwd(xc, p_all, gp, du, dw, dqd, dkd, dqk, dgl, lay, name):
    s = xc.shape[0]
    c = GDN_C
    n = _div(s, c)
    cps = _tile(n, GDN_CPS)
    t = cps * c
    chunk = functools.partial(_gdn_chunk, bdot=_bdot_vjp)

    def body(xq, xk, xv, ab, gp_ref, du_r, dw_r, dqd_r, dkd_r, dqk_r, dgl_r, dxc_ref, dab_ref, dgp_ref):
        lane = lax.broadcasted_iota(jnp.int32, (1, LANE), 1)
        _, vjp = jax.vjp(chunk, xq[...], xk[...], xv[...], ab[...], gp_ref[...])
        cts = []
        for ci in range(cps):
            rs = pl.ds(ci * c, c)
            heads = []
            for h in range(GDN_H):
                hs = pl.ds(h * GDN_D, GDN_D)
                dgl_h = jnp.sum(jnp.where(lane == h, dgl_r[ci], 0.0), axis=1, keepdims=True)
                heads.append((du_r[rs, hs], dw_r[rs, hs], dqd_r[rs, hs], dkd_r[rs, hs], dqk_r[rs, pl.ds(h * c, c)], dgl_h))
            cts.append(heads)
        dq, dk, dv, dab, dgp = vjp(cts)
        dxc_ref[:, pl.ds(0, GDN_W)] = dq
        dxc_ref[:, pl.ds(GDN_W, GDN_W)] = dk
        dxc_ref[:, pl.ds(2 * GDN_W, GDN_W)] = dv
        dab_ref[...] = dab.astype(BF16)

        @pl.when(pl.program_id(0) == 0)
        def _():
            dgp_ref[...] = jnp.zeros_like(dgp_ref)

        dgp_ref[...] += dgp

    row = pl.BlockSpec((t, GDN_W), lambda n: (n, 0))
    return pl.pallas_call(
        body, name=name, grid=(n // cps,),
        in_specs=_gdn_pre_specs(lay, t) + [row, row, row, row, pl.BlockSpec((t, GDN_H * c), lambda n: (n, 0)),
                                           pl.BlockSpec((cps, 1, LANE), lambda n: (n, 0, 0))],
        out_specs=[pl.BlockSpec((t, 3 * GDN_W), lambda n: (n, 0)), pl.BlockSpec((t, LANE), lambda n: (n, 0)),
                   pl.BlockSpec((8, LANE), lambda n: (0, 0))],
        out_shape=[SDS((s, 3 * GDN_W), F32), SDS((s, LANE), BF16), SDS((8, LANE), F32)],
        compiler_params=_cp(("arbitrary",)))(xc, xc, xc, p_all, gp, du, dw, dqd, dkd, dqk, dgl)


def _lane_scalar(row, h):
    lane = lax.broadcasted_iota(jnp.int32, row.shape, 1)
    return jnp.sum(jnp.where(lane == h, row, 0.0), axis=1, keepdims=True)


def gdn_scan_fwd(u, w, qd, kd, qk, gl, name):
    s = u.shape[0]
    c = GDN_C
    n = _div(s, c)
    cps = _tile(n, GDN_CPS)
    t = cps * c

    def body(u_r, w_r, qd_r, kd_r, qk_r, gl_r, o_ref, s_ref, st):
        @pl.when(pl.program_id(0) == 0)
        def _():
            st[...] = jnp.zeros_like(st)

        heads = range(GDN_H)
        hs = [pl.ds(h * GDN_D, GDN_D) for h in heads]
        for ci in range(cps):
            rs = pl.ds(ci * c, c)
            s_ref[ci] = st[...]
            sh = [st[hs[h], :] for h in heads]
            shb = [x.astype(BF16) for x in sh]
            ws = [_dot(w_r[rs, hs[h]], shb[h], NN) for h in heads]
            qs = [_dot(qd_r[rs, hs[h]], shb[h], NN) for h in heads]
            vb = [(u_r[rs, hs[h]] - ws[h]).astype(BF16) for h in heads]
            ov = [_dot(qk_r[rs, pl.ds(h * c, c)], vb[h], NN) for h in heads]
            kv = [_dot(kd_r[rs, hs[h]], vb[h], TN) for h in heads]
            for h in heads:
                o_ref[rs, hs[h]] = qs[h] + ov[h]
                st[hs[h], :] = sh[h] * _lane_scalar(gl_r[ci], h) + kv[h]

    row = pl.BlockSpec((t, GDN_W), lambda i: (i, 0))
    return pl.pallas_call(
        body, name=name, grid=(n // cps,),
        in_specs=[row, row, row, row, pl.BlockSpec((t, GDN_H * c), lambda i: (i, 0)), pl.BlockSpec((cps, 1, LANE), lambda i: (i, 0, 0))],
        out_specs=[row, pl.BlockSpec((cps, GDN_W, GDN_D), lambda i: (i, 0, 0))],
        out_shape=[SDS((s, GDN_W), F32), SDS((n, GDN_W, GDN_D), F32)],
        scratch_shapes=[pltpu.VMEM((GDN_W, GDN_D), F32)],
        compiler_params=_cp(("arbitrary",)))(u, w, qd, kd, qk, gl)


def gdn_scan_bwd(u, w, qd, kd, qk, gl, states, do, name):
    s = u.shape[0]
    c = GDN_C
    n = _div(s, c)
    cps = _tile(n, GDN_CPS)
    t = cps * c
    steps = n // cps

    def body(u_r, w_r, qd_r, kd_r, qk_r, gl_r, s_r, do_r, du_o, dw_o, dqd_o, dkd_o, dqk_o, dgl_o, dst):
        @pl.when(pl.program_id(0) == 0)
        def _():
            dst[...] = jnp.zeros_like(dst)

        lane = lax.broadcasted_iota(jnp.int32, (1, LANE), 1)
        heads = range(GDN_H)
        hs = [pl.ds(h * GDN_D, GDN_D) for h in heads]
        qs = [pl.ds(h * c, c) for h in heads]
        for ci in reversed(range(cps)):
            rs = pl.ds(ci * c, c)
            sh = [s_r[ci, hs[h], :] for h in heads]
            shb = [x.astype(BF16) for x in sh]
            ds_out = [dst[hs[h], :] for h in heads]
            dsb = [x.astype(BF16) for x in ds_out]
            dob = [do_r[rs, hs[h]].astype(BF16) for h in heads]
            ws = [_dot(w_r[rs, hs[h]], shb[h], NN) for h in heads]
            dv1 = [_dot(qk_r[rs, qs[h]], dob[h], TN) for h in heads]
            dv2 = [_dot(kd_r[rs, hs[h]], dsb[h], NN) for h in heads]
            dqd = [_dot(dob[h], shb[h], NT) for h in heads]
            dsq = [_dot(qd_r[rs, hs[h]], dob[h], TN) for h in heads]
            vb = [(u_r[rs, hs[h]] - ws[h]).astype(BF16) for h in heads]
            dv = [dv1[h] + dv2[h] for h in heads]
            dvb = [x.astype(BF16) for x in dv]
            dw = [_dot(dvb[h], shb[h], NT) for h in heads]
            dkd = [_dot(vb[h], dsb[h], NT) for h in heads]
            dqk = [_dot(dob[h], vb[h], NT) for h in heads]
            dsw = [_dot(w_r[rs, hs[h]], dvb[h], TN) for h in heads]
            dgl_row = jnp.zeros((1, LANE), F32)
            for h in heads:
                du_o[rs, hs[h]] = dv[h]
                dw_o[rs, hs[h]] = -dw[h]
                dqd_o[rs, hs[h]] = dqd[h]
                dkd_o[rs, hs[h]] = dkd[h]
                dqk_o[rs, qs[h]] = dqk[h]
                dgl_row = dgl_row + jnp.where(lane == h, jnp.sum(jnp.sum(ds_out[h] * sh[h], axis=1, keepdims=True), axis=0, keepdims=True), 0.0)
                dst[hs[h], :] = ds_out[h] * _lane_scalar(gl_r[ci], h) + dsq[h] - dsw[h]
            dgl_o[ci] = dgl_row

    rev = lambda i: steps - 1 - i
    row = pl.BlockSpec((t, GDN_W), lambda i: (rev(i), 0))
    qks = pl.BlockSpec((t, GDN_H * c), lambda i: (rev(i), 0))
    gls = pl.BlockSpec((cps, 1, LANE), lambda i: (rev(i), 0, 0))
    return pl.pallas_call(
        body, name=name, grid=(steps,),
        in_specs=[row, row, row, row, qks, gls, pl.BlockSpec((cps, GDN_W, GDN_D), lambda i: (rev(i), 0, 0)), row],
        out_specs=[row, row, row, row, qks, gls],
        out_shape=[SDS((s, GDN_W), F32)] * 4 + [SDS((s, GDN_H * c), F32), SDS((n, 1, LANE), F32)],
        scratch_shapes=[pltpu.VMEM((GDN_W, GDN_D), F32)],
        compiler_params=_cp(("arbitrary",)))(u, w, qd, kd, qk, gl, states, do)


def _gdn_out_rows(o, z, nw):
    outs = []
    for h in range(GDN_H):
        hs = slice(h * GDN_D, (h + 1) * GDN_D)
        oh = o[:, hs]
        y = oh * lax.rsqrt(jnp.mean(oh * oh, axis=-1, keepdims=True) + RMS_EPS) * nw
        outs.append(y * _silu(z[:, hs]))
    return jnp.concatenate(outs, axis=1)


def gdn_out(o, p_all, nw, dy, lay, name):
    s = o.shape[0]
    tm = _tile(s, 512)
    bwd = dy is not None

    def body(*refs):
        o_r, z_r, nw_r = refs[:3]
        if not bwd:
            refs[3][...] = _gdn_out_rows(o_r[...], z_r[...], nw_r[...]).astype(BF16)
            return
        dy_r, do_o, dz_o, dnw_o = refs[3:]
        _, vjp = jax.vjp(_gdn_out_rows, o_r[...], z_r[...], nw_r[...])
        d_o, d_z, d_nw = vjp(dy_r[...].astype(F32))
        do_o[...] = d_o
        dz_o[...] = d_z.astype(BF16)

        @pl.when(pl.program_id(0) == 0)
        def _():
            dnw_o[...] = jnp.zeros_like(dnw_o)

        dnw_o[...] += d_nw

    row = pl.BlockSpec((tm, GDN_W), lambda i: (i, 0))
    zs = pl.BlockSpec((tm, GDN_W), lambda i: (i, _div(lay.z, GDN_W)))
    nws = pl.BlockSpec((1, GDN_D), lambda i: (0, 0))
    if not bwd:
        return pl.pallas_call(body, name=name, grid=(s // tm,), in_specs=[row, zs, nws], out_specs=row,
                              out_shape=SDS((s, GDN_W), BF16), compiler_params=_cp(("parallel",)))(o, p_all, nw)
    return pl.pallas_call(body, name=name, grid=(s // tm,), in_specs=[row, zs, nws, row], out_specs=[row, row, nws],
                          out_shape=[SDS((s, GDN_W), F32), SDS((s, GDN_W), BF16), SDS((1, GDN_D), F32)],
                          compiler_params=_cp(("arbitrary",)))(o, p_all, nw, dy)


def _cols_to_full(g):
    n, k, c = g.shape
    return g.transpose(1, 0, 2).reshape(k, n * c)


def _rows_to_blocks(w):
    return w.reshape(N_DEV, w.shape[0] // N_DEV, w.shape[1])


def _pack_small(parts, rows):
    flat = jnp.concatenate([jnp.pad(p.reshape(-1), (0, -p.size % LANE)) for p in parts])
    return jnp.pad(flat, (0, rows * LANE - flat.size)).reshape(rows, LANE)


def kernel(x, mem, g_mix, w_in, sinks, conv_w, a_log, dt_bias, gdn_norm_w, g_mem, w_mem_kv, w_swa_up, w_gdn_up, w_xa_up, w_out, g_mlp, w_mlp_in, w_mlp_out, g_final, loss_target, m_g_mix, m_w_in, m_sinks, m_conv_w, m_a_log, m_dt_bias, m_gdn_norm_w, m_g_mem, m_w_mem_kv, m_w_swa_up, m_w_gdn_up, m_w_xa_up, m_w_out, m_g_mlp, m_w_mlp_in, m_w_mlp_out, m_g_final, v_g_mix, v_w_in, v_sinks, v_conv_w, v_a_log, v_dt_bias, v_gdn_norm_w, v_g_mem, v_w_mem_kv, v_w_swa_up, v_w_gdn_up, v_w_xa_up, v_w_out, v_g_mlp, v_w_mlp_in, v_w_mlp_out, v_g_final):
    xs, ms, tgt = x[0], mem[0], loss_target[0]
    s, d = xs.shape
    lay = Layout(d)
    px, py, pc = _position()
    dev = 4 * px + 2 * py + pc

    n1, g_in, g_conv = rmsnorm_fwd(xs, g_mix, "norm_mix", side=GatherJob([w_in[0].astype(BF16), conv_w[0]]))
    W_in = pad_w_in(g_in, lay)
    convw = _cols_to_full(g_conv)
    gp = jnp.zeros((8, LANE), F32).at[0, :GDN_H].set(a_log[0]).at[1, :GDN_H].set(dt_bias[0])
    later = [w_mem_kv[0], w_swa_up[0], w_gdn_up[0], w_xa_up[0], w_out[0], w_mlp_in[0]]

    p_all, g_mkv, W_sup, W_gup, W_xup, g_out, W_m1 = matmul(
        n1, W_in, mode="nn", out_dtype=F32, name="proj_in", tm=2048, tn=1024, tk=d,
        side=GatherJob([w.astype(BF16) for w in later]))
    W_mkv = g_mkv.reshape(-1, g_mkv.shape[2])
    W_out = g_out.reshape(-1, d)
    y_a = swa_fwd(p_all, sinks, lay, "swa_fwd")
    xc = gdn_conv_fwd(p_all, convw, lay, "gdn_conv_fwd")
    u, gw, gqd, gkd, gqk, ggl = gdn_pre_fwd(xc, p_all, gp, lay, "gdn_pre_fwd")
    o_b, states = gdn_scan_fwd(u, gw, gqd, gkd, gqk, ggl, "gdn_scan_fwd")
    y_b = gdn_out(o_b, p_all, gdn_norm_w, None, lay, "gdn_out_fwd")
    nm = rmsnorm_fwd(ms, g_mem, "norm_mem")
    mkv = matmul(nm, W_mkv, mode="nn", out_dtype=BF16, name="proj_mem", tk=d)
    y_c = xattn_fwd(p_all, mkv, lay, "xattn_fwd")
    merged = merge(p_all, y_a, y_b, y_c, W_sup, W_gup, W_xup, None, lay, "merge_fwd")
    h1, n2 = matmul(merged, W_out, mode="nn", out_dtype=F32, name="proj_out", tm=512, tn=d, tk=d, resid=xs, rms_gain=g_mlp)
    uu, act, g_m2 = matmul(n2, W_m1, mode="nn", out_dtype=F32, name="mlp_in", tm=2048, tn=512, tk=d, b_cols=True,
                           relu2_out=True, side=GatherJob([w_mlp_out[0].astype(BF16)]))
    W_m2 = g_m2.reshape(-1, d)
    h2 = matmul(act, W_m2, mode="nn", out_dtype=F32, name="mlp_out", tm=512, tn=2048, tk=2048, resid=h1)
    dh2, dg_final, lrow, dh2_b = final_norm_loss(h2, g_final.reshape(1, d), tgt, "final_loss")
    loss = lax.psum(lrow[0, 0], ("x", "y", "c"))

    du = matmul(dh2_b, W_m2, mode="nt", out_dtype=BF16, name="mlp_out_dx", tm=2048, tn=512, tk=d, relu2_grad_of=uu)
    dW_m2 = matmul(act, dh2_b, mode="tn", out_dtype=BF16, name="mlp_out_dw", tm=1024, tn=2048, tk=1024)
    dW_m2 = _rows_to_blocks(dW_m2)
    dn2, sib_m2 = matmul(du, W_m1, mode="nt", out_dtype=F32, name="mlp_in_dx", tm=1024, tn=2048, tk=1024, b_cols=True,
                         side=PairExchangeJob([dW_m2], [False]))
    c_m2 = pair_add(dW_m2, False, sib_m2, "grads_pair_add_m2")
    dW_m1 = matmul(n2, du, mode="tn", out_dtype=BF16, name="mlp_in_dw", tm=2048, tn=1024, tk=1024)
    dh1, dg_mlp, dh1_b = rmsnorm_bwd(h1, g_mlp, dn2, dh2, "norm_mlp_bwd", bf16_copy=True)

    dmerged, sib_m1 = matmul(dh1_b, W_out, mode="nt", out_dtype=F32, name="proj_out_dx", tm=2048, tn=512, tk=d,
                             side=PairExchangeJob([dW_m1], [True]))
    c_m1 = pair_add(dW_m1, True, sib_m1, "grads_pair_add_m1")
    dW_out = matmul(merged, dh1_b, mode="tn", out_dtype=BF16, name="proj_out_dw", tm=2048, tn=1024, tk=1024)
    dgates, dta, dtb, dtc = merge(p_all, y_a, y_b, y_c, W_sup, W_gup, W_xup, dmerged, lay, "merge_bwd")
    dy_a = matmul(dta, W_sup, mode="nt", out_dtype=BF16, name="swa_up_dx", tm=2048, tk=d, b_cols=True)
    dy_b = matmul(dtb, W_gup, mode="nt", out_dtype=BF16, name="gdn_up_dx", tm=2048, tk=d, b_cols=True)
    dy_c = matmul(dtc, W_xup, mode="nt", out_dtype=BF16, name="xa_up_dx", tm=2048, tk=d, b_cols=True)
    dW_sup = matmul(y_a, dta, mode="tn", out_dtype=BF16, name="swa_up_dw", tn=2048)
    dW_gup = matmul(y_b, dtb, mode="tn", out_dtype=BF16, name="gdn_up_dw", tn=2048)
    dW_xup = matmul(y_c, dtc, mode="tn", out_dtype=BF16, name="xa_up_dw", tn=2048)

    dq_a, dk_a, dv_a, dsinks = swa_bwd(p_all, sinks, dy_a, lay, "swa_bwd")
    dq_c, dmkv = xattn_bwd(p_all, mkv, dy_c, lay, "xattn_bwd")
    dW_mkv = matmul(nm, dmkv, mode="tn", out_dtype=BF16, name="proj_mem_dw", tk=256)
    dnm = matmul(dmkv, W_mkv, mode="nt", out_dtype=F32, name="proj_mem_dx", tk=1024)
    _, dg_mem = rmsnorm_bwd(ms, g_mem, dnm, None, "norm_mem_bwd")

    do_b, dz, dnorm_w = gdn_out(o_b, p_all, gdn_norm_w, dy_b, lay, "gdn_out_bwd")
    du_g, dw_g, dqd_g, dkd_g, dqk_g, dgl_g = gdn_scan_bwd(u, gw, gqd, gkd, gqk, ggl, states, do_b, "gdn_scan_bwd")
    dxc, dab, dgp = gdn_pre_bwd(xc, p_all, gp, du_g, dw_g, dqd_g, dkd_g, dqk_g, dgl_g, lay, "gdn_pre_bwd")
    dqkv, dconv = gdn_conv_bwd(p_all, convw, dxc, lay, "gdn_conv_bwd")

    drest = jnp.concatenate([dq_a, dqkv, dz, dq_c, dk_a, dv_a, dab, jnp.zeros((s, lay.pw - lay.end), BF16)], axis=1)
    def pair_stage(grads, cols, tag):
        from_sib = run_job(PairExchangeJob(grads, cols), "grads_pair_exchange_" + tag)
        return [pair_add(g, cl, o, "grads_pair_add_%s%d" % (tag, i)) for i, (g, cl, o) in enumerate(zip(grads, cols, from_sib))]

    small = pair_stage([_rows_to_blocks(dW_mkv), dW_sup, dW_gup, dW_xup, _rows_to_blocks(dW_out)],
                       [False, True, True, True, False], "a")
    dW_in, p_m1, p_m2 = matmul(n1, dgates, tail=drest, mode="tn", out_dtype=BF16, name="proj_in_dw", tm=2048, tn=1024, tk=1024,
                               side=ChipExchangeJob([c_m1, c_m2]))
    late = pair_stage([unpad_dw_in(dW_in, lay)], [False], "b")
    dn1, p_in, p_mkv, p_sup, p_gup, p_xup, p_out = matmul(
        dgates, W_in, tail=drest, mode="nt", out_dtype=F32, name="proj_in_dx", tm=1024, tn=2048, tk=1024,
        side=ChipExchangeJob(late + small))
    grad_x, dg_mix = rmsnorm_bwd(xs, g_mix, dn1, dh1, "norm_mix_bwd")
    parts = [p_in, p_mkv, p_sup, p_gup, p_xup, p_out, p_m1, p_m2]

    shard_names = [(w_in, m_w_in, v_w_in), (w_mem_kv, m_w_mem_kv, v_w_mem_kv), (w_swa_up, m_w_swa_up, v_w_swa_up),
                   (w_gdn_up, m_w_gdn_up, v_w_gdn_up), (w_xa_up, m_w_xa_up, v_w_xa_up), (w_out, m_w_out, v_w_out),
                   (w_mlp_in, m_w_mlp_in, v_w_mlp_in), (w_mlp_out, m_w_mlp_out, v_w_mlp_out)]
    big_res = [adamw(p, w[0], m[0], v[0], "adamw_%d" % i) for i, (p, (w, m, v)) in enumerate(zip(parts, shard_names))]

    smalls = [(g_mix, m_g_mix, v_g_mix, dg_mix), (sinks, m_sinks, v_sinks, dsinks[:, :SWA_HQ]),
              (a_log, m_a_log, v_a_log, dgp[0:1, :GDN_H]), (dt_bias, m_dt_bias, v_dt_bias, dgp[1:2, :GDN_H]),
              (gdn_norm_w, m_gdn_norm_w, v_gdn_norm_w, dnorm_w), (g_mem, m_g_mem, v_g_mem, dg_mem),
              (g_mlp, m_g_mlp, v_g_mlp, dg_mlp), (g_final, m_g_final, v_g_final, dg_final)]
    sizes = [-(-t[0].size // LANE) * LANE for t in smalls] + [GDN_CONV * 3 * GDN_W]
    rows = -(-sum(sizes) // (8 * LANE)) * 8
    csh = conv_w.shape[2]

    def conv_place(a):
        full = jnp.tile(a[0], (1, N_DEV))
        owner = lax.broadcasted_iota(jnp.int32, full.shape, 1) // csh
        return jnp.where(owner == dev, full, 0.0)

    g_pack = _pack_small([t[3] for t in smalls] + [dconv], rows)
    w_pack = _pack_small([t[0] for t in smalls] + [conv_place(conv_w)], rows)
    m_pack = _pack_small([t[1] for t in smalls] + [conv_place(m_conv_w)], rows)
    v_pack = _pack_small([t[2] for t in smalls] + [conv_place(v_conv_w)], rows)
    g_all = run_job(GatherJob([g_pack]), "gather_small_grads")[0]
    small_res = adamw(g_all, w_pack, m_pack, v_pack, "adamw_small")

    def unpack(arr):
        flat = arr.reshape(-1)
        outs, off = [], 0
        for t, sz in zip(smalls, sizes[:-1]):
            outs.append(flat[off:off + t[0].size].reshape(t[0].shape))
            off += sz
        cw = flat[off:off + sizes[-1]].reshape(GDN_CONV, 3 * GDN_W)
        mine = (lax.broadcasted_iota(jnp.int32, (1, N_DEV, 1), 1) == dev).astype(F32)
        outs.append(jnp.sum(cw.reshape(GDN_CONV, N_DEV, csh) * mine, axis=1)[None])
        return outs

    sg, sd, sm, sv = (unpack(a) for a in small_res)
    bg, bd, bm, bv = ([r[i][None] for r in big_res] for i in range(4))

    def ordered(sm_, bg_):
        return [sm_[0], bg_[0], sm_[1], sm_[8], sm_[2], sm_[3], sm_[4], sm_[5], bg_[1], bg_[2], bg_[3], bg_[4], bg_[5],
                sm_[6], bg_[6], bg_[7], sm_[7]]

    return (loss, grad_x[None], *ordered(sg, bg), *ordered(sd, bd), *ordered(sm, bm), *ordered(sv, bv))
```

```python
import functools
import math

import jax
import jax.numpy as jnp
from jax import lax
from jax.experimental import pallas as pl
from jax.experimental.pallas import tpu as pltpu

F32, BF16 = jnp.float32, jnp.bfloat16
SDS = jax.ShapeDtypeStruct
MESH = pl.DeviceIdType.MESH
ANY = pl.BlockSpec(memory_space=pl.ANY)

SWA_HQ, SWA_HKV, SWA_HD, SWA_W = 16, 2, 64, 128
SWA_G = SWA_HQ // SWA_HKV
GDN_H, GDN_D, GDN_CONV, GDN_C = 4, 128, 4, 64
XA_H, XA_D = 4, 128
Q_W = SWA_HQ * SWA_HD
KV_W = SWA_HKV * SWA_HD
GDN_W = GDN_H * GDN_D
XA_W = XA_H * XA_D
RMS_EPS = 1e-6
L2_EPS = 1e-6
NEG = -1e30
N_DEV = 8
LANE = 128

ADAM_LR, ADAM_B1, ADAM_B2, ADAM_EPS, ADAM_WD, ADAM_STEP = 0.001, 0.9, 0.999, 1e-08, 0.01, 10

VMEM_BIG = 56 * 1024 * 1024


def _cp(sem, vmem=VMEM_BIG):
    return pltpu.CompilerParams(dimension_semantics=sem, vmem_limit_bytes=vmem)


def _div(a, b):
    assert a % b == 0, (a, b)
    return a // b


def _tile(n, t):
    t = min(t, n)
    assert n % t == 0, (n, t)
    return t


def _sigmoid(x):
    return 1.0 / (1.0 + jnp.exp(-x))


def _silu(x):
    return x * _sigmoid(x)


def _softplus(x):
    return jnp.maximum(x, 0.0) + jnp.log1p(jnp.exp(-jnp.abs(x)))


def _dot(a, b, dims, prec=None):
    return lax.dot_general(a, b, (dims, ((), ())), precision=prec, preferred_element_type=F32)


NN = ((1,), (0,))
NT = ((1,), (1,))
TN = ((0,), (0,))
HI = lax.Precision.HIGHEST


def _bdot_plain(a, b, dims):
    return _dot(a.astype(BF16), b.astype(BF16), dims)


@functools.partial(jax.custom_vjp, nondiff_argnums=(2,))
def _bdot_vjp(a, b, dims):
    return _bdot_plain(a, b, dims)


def _bdot_vjp_fwd(a, b, dims):
    return _bdot_plain(a, b, dims), (a, b)


def _bdot_vjp_bwd(dims, res, ct):
    a, b = res
    if dims == NN:
        return _bdot_plain(ct, b, NT), _bdot_plain(a, ct, TN)
    assert dims == NT, dims
    return _bdot_plain(ct, b, NN), _bdot_plain(ct, a, TN)


_bdot_vjp.defvjp(_bdot_vjp_fwd, _bdot_vjp_bwd)


def _neumann(xs):
    pws, nns = list(xs), list(xs)
    for _ in range(5):
        pws = [_bdot_plain(p, p, NN) for p in pws]
        nns = [n + p + _bdot_plain(n, p, NN) for n, p in zip(nns, pws)]
    return tuple(nns)


@jax.custom_vjp
def _neumann_vjp(xs):
    return _neumann(xs)


def _neumann_vjp_fwd(xs):
    nns = _neumann(xs)
    return nns, nns


def _neumann_vjp_bwd(nns, cts):
    ts = [ct + _bdot_plain(nn, ct, TN) for nn, ct in zip(nns, cts)]
    return (tuple(t + _bdot_plain(t, nn, NT) for t, nn in zip(ts, nns)),)


_neumann_vjp.defvjp(_neumann_vjp_fwd, _neumann_vjp_bwd)


class Layout:
    def __init__(self, d):
        self.d = d
        self.g = 0
        self.q = 3 * d
        self.qkv = self.q + Q_W
        self.z = self.qkv + 3 * GDN_W
        self.qc = self.z + GDN_W
        self.k = self.qc + XA_W
        self.v = self.k + KV_W
        self.ab = self.v + KV_W
        self.end = self.ab + LANE
        self.pw = -(-self.end // 1024) * 1024
        self.lq, self.lk, self.lv, self.lqkv = 0, Q_W, Q_W + KV_W, Q_W + 2 * KV_W
        self.la = self.lqkv + 3 * GDN_W
        self.lz = self.la + 2 * GDN_H
        self.lqc = self.lz + GDN_W
        self.lg = self.lqc + XA_W
        self.lw = self.lg + 3 * d

    def pieces(self):
        segs = [(self.lq, self.lk, self.q), (self.lk, self.lv, self.k), (self.lv, self.lqkv, self.v),
                (self.lqkv, self.la, self.qkv), (self.la, self.lz, self.ab), (self.lz, self.lqc, self.z),
                (self.lqc, self.lg, self.qc), (self.lg, self.lw, self.g)]
        cw = _div(self.lw, N_DEV)
        out = []
        for dev in range(N_DEV):
            lo, hi = dev * cw, (dev + 1) * cw
            for ls, le, ps in segs:
                s, e = max(lo, ls), min(hi, le)
                if s < e:
                    out.append((dev, s - lo, ps + s - ls, e - s))
        return out


def pad_w_in(g, lay):
    nd, k, cw = g.shape
    tr = _tile(k, 256)
    tail = lay.ab + 2 * GDN_H

    def body(g_ref, o_ref):
        o_ref[:, pl.ds(tail, lay.pw - tail)] = jnp.zeros((tr, lay.pw - tail), o_ref.dtype)
        for dev, so, po, ln in lay.pieces():
            o_ref[:, pl.ds(po, ln)] = g_ref[dev, :, pl.ds(so, ln)]

    return pl.pallas_call(
        body, name="pad_w_in", grid=(k // tr,), in_specs=[pl.BlockSpec((nd, tr, cw), lambda i: (0, i, 0))],
        out_specs=pl.BlockSpec((tr, lay.pw), lambda i: (i, 0)), out_shape=SDS((k, lay.pw), g.dtype),
        compiler_params=_cp(("parallel",)))(g)


def unpad_dw_in(dw, lay):
    k = dw.shape[0]
    cw = _div(lay.lw, N_DEV)
    tr = _tile(k, 256)

    def body(d_ref, o_ref):
        for dev, so, po, ln in lay.pieces():
            o_ref[dev, :, pl.ds(so, ln)] = d_ref[:, pl.ds(po, ln)]

    return pl.pallas_call(
        body, name="unpad_dw_in", grid=(k // tr,), in_specs=[pl.BlockSpec((tr, lay.pw), lambda i: (i, 0))],
        out_specs=pl.BlockSpec((N_DEV, tr, cw), lambda i: (0, i, 0)), out_shape=SDS((N_DEV, k, cw), dw.dtype),
        compiler_params=_cp(("parallel",)))(dw)


def _position():
    return lax.axis_index("x"), lax.axis_index("y"), lax.axis_index("c")


class GatherJob:
    def __init__(self, arrs):
        self.ins = list(arrs)
        n = len(arrs)
        self.out_shapes = [SDS((N_DEV,) + a.shape, a.dtype) for a in arrs]
        self.scratch = [pltpu.SemaphoreType.DMA((n, 7)), pltpu.SemaphoreType.DMA((n, 7)), pltpu.SemaphoreType.DMA((n,))]

    def _ctx(self, outs, sems):
        send_sems, recv_sems, _ = sems
        x, y, c = _position()

        def blk(o, p):
            return o.at[4 * p[0] + 2 * p[1] + p[2]]

        def copy(i, k, block, to, src=None):
            return pltpu.make_async_remote_copy(
                src_ref=blk(outs[i], block) if src is None else src, dst_ref=blk(outs[i], block),
                send_sem=send_sems.at[i, k], recv_sem=recv_sems.at[i, k], device_id=to, device_id_type=MESH)

        return (x, y, c), (x, y, 1 - c), [(1 - x, y), (x, 1 - y), (1 - x, 1 - y)], blk, copy

    def start(self, ins, outs, sems):
        me, sibling, chips, blk, copy = self._ctx(outs, sems)
        for i in range(len(ins)):
            pltpu.make_async_copy(ins[i], blk(outs[i], me), sems[2].at[i]).start()
            copy(i, 0, me, sibling, src=ins[i]).start()
            for j, chip in enumerate(chips[:2]):
                copy(i, 1 + j, me, (*chip, me[2]), src=ins[i]).start()

    def _relay(self, i, outs, sems, onward):
        me, _, _, blk, copy = self._ctx(outs, sems)
        x, y, c = me
        origin = (x + (1 - c) - 2 * x * (1 - c), y + c - 2 * y * c, c)
        if not onward:
            return copy(i, 1 + c, origin, me)
        return copy(i, 3, origin, (x + c - 2 * x * c, y + (1 - c) - 2 * y * (1 - c), c))

    def relay(self, ins, outs, sems):
        for i in range(len(ins)):
            self._relay(i, outs, sems, False).wait_recv()
            self._relay(i, outs, sems, True).start()

    def mid(self, ins, outs, sems):
        me, sibling, chips, blk, copy = self._ctx(outs, sems)
        for i in range(len(ins)):
            for j, chip in enumerate(chips):
                arrival = copy(i, 1 + j, (*chip, me[2]), me)
                if j < 2:
                    pl.when(me[2] != j)(arrival.wait_recv)
                else:
                    arrival.wait_recv()
                copy(i, 4 + j, (*chip, me[2]), sibling).start()

    def finish(self, ins, outs, sems):
        me, sibling, chips, blk, copy = self._ctx(outs, sems)
        for i in range(len(ins)):
            copy(i, 0, sibling, me).wait_recv()
            for j, chip in enumerate(chips):
                copy(i, 4 + j, (*chip, 1 - me[2]), me).wait_recv()
        for i in range(len(ins)):
            pltpu.make_async_copy(ins[i], blk(outs[i], me), sems[2].at[i]).wait()
            copy(i, 0, me, sibling, src=ins[i]).wait_send()
            for j, chip in enumerate(chips[:2]):
                copy(i, 1 + j, me, (*chip, me[2]), src=ins[i]).wait_send()
            self._relay(i, outs, sems, True).wait_send()
            for j, chip in enumerate(chips):
                copy(i, 4 + j, (*chip, me[2]), sibling).wait_send()


class ChipExchangeJob:
    mid = None

    def __init__(self, arrs):
        self.ins = list(arrs)
        n = len(arrs)
        self.out_shapes = [SDS(a.shape, a.dtype) for a in arrs]
        self.scratch = [pltpu.SemaphoreType.DMA((n, 3)), pltpu.SemaphoreType.DMA((n, 3)), pltpu.SemaphoreType.DMA((n,))]

    def _copies(self, ins, outs, sems, i, arrivals):
        send_sems, recv_sems, local_sems = sems
        x, y, c = _position()
        my_chip = 2 * x + y
        chips = [(1 - x, y), (x, 1 - y), (1 - x, 1 - y)]
        if arrivals:
            return [pltpu.make_async_remote_copy(
                src_ref=ins[i].at[my_chip], dst_ref=outs[i].at[2 * px + py], send_sem=send_sems.at[i, k],
                recv_sem=recv_sems.at[i, k], device_id=(px, py, c), device_id_type=MESH) for k, (px, py) in enumerate(chips)]
        local = pltpu.make_async_copy(ins[i].at[my_chip], outs[i].at[my_chip], local_sems.at[i])
        return local, [pltpu.make_async_remote_copy(
            src_ref=ins[i].at[2 * px + py], dst_ref=outs[i].at[my_chip], send_sem=send_sems.at[i, k],
            recv_sem=recv_sems.at[i, k], device_id=(px, py, c), device_id_type=MESH) for k, (px, py) in enumerate(chips)]

    def start(self, ins, outs, sems):
        for i in range(len(ins)):
            local, remote = self._copies(ins, outs, sems, i, False)
            local.start()
            for cp in remote:
                cp.start()

    def finish(self, ins, outs, sems):
        for i in range(len(ins)):
            for cp in self._copies(ins, outs, sems, i, True):
                cp.wait_recv()
            local, remote = self._copies(ins, outs, sems, i, False)
            for cp in remote:
                cp.wait_send()
            local.wait()


def _slab_shape(g, cols):
    return (g.shape[0], _div(g.shape[1], N_DEV)) if cols else g.shape[1:]


class PairExchangeJob:
    mid = None

    def __init__(self, grads, cols):
        self.ins, self.cols = list(grads), list(cols)
        n = len(grads)
        self.out_shapes = [SDS((4,) + _slab_shape(g, cl), g.dtype) for g, cl in zip(grads, cols)]
        self.scratch = [pltpu.SemaphoreType.DMA((n, 4)), pltpu.SemaphoreType.DMA((n, 4))]

    def _copies(self, ins, outs, sems):
        send_sems, recv_sems = sems
        x, y, c = _position()

        def part(i, dst):
            if not self.cols[i]:
                return ins[i].at[dst]
            cw = _slab_shape(self.ins[i], True)[1]
            return ins[i].at[:, pl.ds(pl.multiple_of(dst * cw, LANE), cw)]

        return [pltpu.make_async_remote_copy(src_ref=part(i, 2 * j + 1 - c), dst_ref=outs[i].at[j], send_sem=send_sems.at[i, j],
                                             recv_sem=recv_sems.at[i, j], device_id=(x, y, 1 - c), device_id_type=MESH)
                for i in range(len(ins)) for j in range(4)]

    def start(self, ins, outs, sems):
        for cp in self._copies(ins, outs, sems):
            cp.start()

    def finish(self, ins, outs, sems):
        for cp in self._copies(ins, outs, sems):
            cp.wait()


def _host_begin(job, step, steps, ins, outs, sems):
    pl.when(step == 0)(lambda: job.start(ins, outs, sems))
    if job.mid is not None:
        pl.when(step == (steps * 45) // 100)(lambda: job.relay(ins, outs, sems))
        pl.when(step == (steps * 85) // 100)(lambda: job.mid(ins, outs, sems))


def run_job(job, name):
    n = len(job.ins)

    def body(*refs):
        ins, outs, sems = refs[:n], refs[n:2 * n], refs[2 * n:]
        job.start(ins, outs, sems)
        if job.mid is not None:
            job.relay(ins, outs, sems)
            job.mid(ins, outs, sems)
        job.finish(ins, outs, sems)

    return pl.pallas_call(body, name=name, out_shape=job.out_shapes, in_specs=[ANY] * n, out_specs=[ANY] * n,
                          scratch_shapes=job.scratch)(*job.ins)


def pair_add(grad, cols, other, name):
    r, c = _slab_shape(grad, cols)
    tr = _tile(r, 256)
    parity = lax.axis_index("c").astype(jnp.int32).reshape(1)

    def body(par_ref, a_ref, b_ref, o_ref):
        o_ref[...] = (a_ref[...].astype(F32) + b_ref[...].astype(F32)).astype(BF16)

    spec = pl.BlockSpec((None, tr, c), lambda j, i, par: (j, i, 0))
    if cols:
        own = pl.BlockSpec((tr, c), lambda j, i, par: (i, 2 * j + par[0]))
    else:
        own = pl.BlockSpec((None, tr, c), lambda j, i, par: (2 * j + par[0], i, 0))
    return pl.pallas_call(
        body, name=name, out_shape=SDS(other.shape, BF16),
        grid_spec=pltpu.PrefetchScalarGridSpec(num_scalar_prefetch=1, grid=(4, r // tr), in_specs=[own, spec], out_specs=spec),
        compiler_params=_cp(("parallel", "parallel")))(parity, grad, other)


def adamw(parts, w, m, v, name):
    p, r, c = parts.shape
    tr = _tile(r, 128 if c > 1024 else 256)

    def body(p_ref, w_ref, m_ref, v_ref, g_out, d_out, m_out, v_out):
        g = p_ref[0].astype(F32)
        for j in range(1, p):
            g = g + p_ref[j].astype(F32)
        mn = ADAM_B1 * m_ref[...] + (1.0 - ADAM_B1) * g
        vn = ADAM_B2 * v_ref[...] + (1.0 - ADAM_B2) * jnp.square(g)
        m_hat = mn / (1.0 - ADAM_B1 ** ADAM_STEP)
        v_hat = vn / (1.0 - ADAM_B2 ** ADAM_STEP)
        g_out[...] = g
        d_out[...] = -ADAM_LR * (m_hat / (jnp.sqrt(v_hat) + ADAM_EPS) + ADAM_WD * w_ref[...])
        m_out[...] = mn
        v_out[...] = vn

    spec = pl.BlockSpec((tr, c), lambda i: (i, 0))
    return pl.pallas_call(
        body, name=name, grid=(r // tr,),
        in_specs=[pl.BlockSpec((p, tr, c), lambda i: (0, i, 0)), spec, spec, spec],
        out_specs=[spec] * 4, out_shape=[SDS((r, c), F32)] * 4, compiler_params=_cp(("parallel",)))(parts, w, m, v)


def matmul(a, b, *, mode, out_dtype, name, tm=1024, tn=1024, tk=512, a_relu2=False, resid=None, relu2_grad_of=None,
           b_cols=False, relu2_out=False, rms_gain=None, loss_target=None, side=None, tail=None):
    if b_cols:
        nb, brows, bc = b.shape
        bshape = (brows, nb * bc)
    else:
        bshape = b.shape
    head_k = head_n = None
    if mode == "nn":
        (m, k), (k2, n) = a.shape, bshape
    elif mode == "nt":
        (m, k), (n, k2) = a.shape, bshape
        if tail is not None:
            head_k, k = k, k + tail.shape[1]
    else:
        (k, m), (k2, n) = a.shape, bshape
        if tail is not None:
            head_n, n = n, n + tail.shape[1]
    assert k == k2 and (tail is None or mode != "nn"), (a.shape, b.shape, mode)
    b_whole = b_cols and mode == "nt" and tk >= k
    tm, tn, tk = _tile(m, tm), _tile(n, tn), _tile(k, tk)
    if b_cols and mode == "nn":
        tn = _tile(bc, tn)
    if b_cols and mode == "nt" and not b_whole:
        tk = _tile(bc, tk)
    nk = k // tk
    ni, nj = m // tm, n // tn
    nk_head = _div(head_k, tk) if head_k is not None else None
    nj_head = _div(head_n, tn) if head_n is not None else None
    use_acc = nk > 1 or tail is not None
    dims = {"nn": NN, "nt": NT, "tn": TN}[mode]
    extras = [e for e in (resid, relu2_grad_of) if e is not None]
    tails = [tail] if tail is not None else []
    n_side = len(side.ins) if side is not None else 0
    loss = loss_target is not None
    if loss:
        extras.append(loss_target)
    gains = [rms_gain] if rms_gain is not None else []
    assert not gains or (tn == n and not relu2_out), (tn, n)
    assert not loss or (gains and resid is not None and relu2_grad_of is None and out_dtype == F32)
    n_main = 2 if (relu2_out or gains) else 1
    n_loss = 2 if loss else 0

    def body(*refs):
        a_ref, b_ref = refs[:2]
        t_ref = refs[2] if tails else None
        n_op = 2 + len(tails)
        e_refs = refs[n_op:n_op + len(extras)]
        n_pre = n_op + len(extras) + len(gains)
        g_ref = refs[n_pre - 1] if gains else None
        n_in = n_pre + n_side
        o_ref = refs[n_in]
        act_ref = refs[n_in + 1] if n_main == 2 else None
        dg_ref, l_ref = refs[n_in + n_main:n_in + n_main + n_loss] if loss else (None, None)
        acc_ref = refs[n_in + n_main + n_loss + n_side] if use_acc else None
        if side is not None:
            s_ins = refs[n_pre:n_in]
            s_outs = refs[n_in + n_main + n_loss:n_in + n_main + n_loss + n_side]
            s_sems = refs[len(refs) - len(side.scratch):]
            step = (pl.program_id(0) * nj + pl.program_id(1)) * nk + pl.program_id(2)
            _host_begin(side, step, ni * nj * nk, s_ins, s_outs, s_sems)

        def operands(a_from=a_ref, b_from=b_ref):
            av = a_from[...]
            if a_relu2:
                av = jnp.square(jnp.maximum(av.astype(F32), 0.0))
            return av.astype(BF16), b_from[...].astype(BF16)

        def finish(r):
            e = list(e_refs)
            if resid is not None:
                r = r + e.pop(0)[...]
            if relu2_grad_of is not None:
                r = r * (2.0 * jnp.maximum(e.pop(0)[...], 0.0))
            if loss:
                gv = g_ref[...]
                inv = lax.rsqrt(jnp.mean(r * r, axis=-1, keepdims=True) + RMS_EPS)
                err = r * inv * gv - e.pop(0)[...]
                lpart = 0.5 * jnp.sum(jnp.mean(err * err, axis=-1, keepdims=True), axis=0, keepdims=True)
                dx, part = _rms_bwd_rows(r, gv, err * (1.0 / n))
                o_ref[...] = dx
                act_ref[...] = dx.astype(BF16)

                @pl.when(pl.program_id(0) == 0)
                def _():
                    dg_ref[...] = jnp.zeros_like(dg_ref)
                    l_ref[...] = jnp.zeros_like(l_ref)

                dg_ref[...] += part
                l_ref[...] += jnp.broadcast_to(lpart, l_ref.shape)
                return
            o_ref[...] = r.astype(out_dtype)
            if relu2_out:
                act_ref[...] = jnp.square(jnp.maximum(r, 0.0)).astype(BF16)
            if gains:
                inv = lax.rsqrt(jnp.mean(r * r, axis=-1, keepdims=True) + RMS_EPS)
                act_ref[...] = (r * inv * g_ref[...]).astype(BF16)

        if b_whole:
            av = a_ref[...].astype(BF16)
            finish(sum(_dot(av[:, kb * bc:(kb + 1) * bc], b_ref[kb].astype(BF16), NT) for kb in range(nb)))
        elif not use_acc:
            av, bv = operands()
            finish(_dot(av, bv, dims))
        else:
            kk = pl.program_id(2)

            def accumulate(a_from, b_from):
                def product():
                    av, bv = operands(a_from, b_from)
                    return _dot(av, bv, dims)

                if nk == 1:
                    finish(product())
                    return

                @pl.when(kk == 0)
                def _():
                    acc_ref[...] = product()

                @pl.when((kk > 0) & (kk < nk - 1))
                def _():
                    acc_ref[...] += product()

                @pl.when(kk == nk - 1)
                def _():
                    finish(acc_ref[...] + product())

            if not tails:
                accumulate(a_ref, b_ref)
            elif mode == "nt":
                pl.when(kk < nk_head)(lambda: accumulate(a_ref, b_ref))
                pl.when(kk >= nk_head)(lambda: accumulate(t_ref, b_ref))
            else:
                in_head = pl.program_id(1) < nj_head
                pl.when(in_head)(lambda: accumulate(a_ref, b_ref))
                pl.when(jnp.logical_not(in_head))(lambda: accumulate(a_ref, t_ref))

        if side is not None:
            pl.when(step == ni * nj * nk - 1)(lambda: side.finish(s_ins, s_outs, s_sems))

    a_spec = {"nn": pl.BlockSpec((tm, tk), lambda i, j, kk: (i, kk)),
              "nt": pl.BlockSpec((tm, tk), lambda i, j, kk: (i, kk)),
              "tn": pl.BlockSpec((tk, tm), lambda i, j, kk: (kk, i))}[mode]
    t_specs = []
    if tails and mode == "nt":
        a_spec = pl.BlockSpec((tm, tk), lambda i, j, kk: (i, jnp.minimum(kk, nk_head - 1)))
        t_specs = [pl.BlockSpec((tm, tk), lambda i, j, kk: (i, jnp.maximum(kk - nk_head, 0)))]
    if tails and mode == "tn":
        t_specs = [pl.BlockSpec((tk, tn), lambda i, j, kk: (kk, jnp.maximum(j - nj_head, 0)))]
    if tails and mode == "tn":
        b_spec = pl.BlockSpec((tk, tn), lambda i, j, kk: (kk, jnp.minimum(j, nj_head - 1)))
    elif not b_cols:
        b_spec = {"nn": pl.BlockSpec((tk, tn), lambda i, j, kk: (kk, j)),
                  "nt": pl.BlockSpec((tn, tk), lambda i, j, kk: (j, kk)),
                  "tn": pl.BlockSpec((tk, tn), lambda i, j, kk: (kk, j))}[mode]
    elif mode == "nn":
        per = bc // tn
        b_spec = pl.BlockSpec((None, tk, tn), lambda i, j, kk: (j // per, kk, j % per))
    elif b_whole:
        b_spec = pl.BlockSpec((nb, tn, bc), lambda i, j, kk: (0, j, 0))
    else:
        assert mode == "nt", mode
        per = bc // tk
        b_spec = pl.BlockSpec((None, tn, tk), lambda i, j, kk: (kk // per, j, kk % per))
    e_spec = pl.BlockSpec((tm, tn), lambda i, j, kk: (i, j))
    main_shapes = [SDS((m, n), out_dtype)] + ([SDS((m, n), BF16)] if n_main == 2 else [])
    g_specs = [pl.BlockSpec((1, tn), lambda i, j, kk: (0, j))] * len(gains)
    l_specs = [pl.BlockSpec((1, tn), lambda i, j, kk: (0, j)), pl.BlockSpec((1, LANE), lambda i, j, kk: (0, 0))] if loss else []
    l_shapes = [SDS((1, n), F32), SDS((1, LANE), F32)] if loss else []
    res = pl.pallas_call(
        body, name=name, grid=(ni, nj, nk),
        in_specs=[a_spec, b_spec] + t_specs + [e_spec] * len(extras) + g_specs + [ANY] * n_side,
        out_specs=[e_spec] * n_main + l_specs + [ANY] * n_side,
        out_shape=main_shapes + l_shapes + (side.out_shapes if side is not None else []),
        scratch_shapes=([pltpu.VMEM((tm, tn), F32)] if use_acc else []) + (side.scratch if side is not None else []),
        compiler_params=_cp(("arbitrary", "arbitrary", "arbitrary")))(a, b, *tails, *extras, *gains, *(side.ins if side is not None else []))
    return res if len(res) > 1 else res[0]


def rmsnorm_fwd(x, g, name, side=None):
    s, d = x.shape
    tm = _tile(s, 256)
    steps = s // tm
    n_side = len(side.ins) if side is not None else 0

    def body(*refs):
        x_ref, g_ref, o_ref = refs[0], refs[1], refs[2 + n_side]
        if side is not None:
            s_ins, s_outs, s_sems = refs[2:2 + n_side], refs[3 + n_side:3 + 2 * n_side], refs[3 + 2 * n_side:]
            _host_begin(side, pl.program_id(0), steps, s_ins, s_outs, s_sems)
        xv = x_ref[...]
        r = lax.rsqrt(jnp.mean(xv * xv, axis=-1, keepdims=True) + RMS_EPS)
        o_ref[...] = (xv * r * g_ref[...]).astype(BF16)
        if side is not None:
            pl.when(pl.program_id(0) == steps - 1)(lambda: side.finish(s_ins, s_outs, s_sems))

    row = pl.BlockSpec((tm, d), lambda i: (i, 0))
    res = pl.pallas_call(
        body, name=name, grid=(steps,), in_specs=[row, pl.BlockSpec((1, d), lambda i: (0, 0))] + [ANY] * n_side,
        out_specs=[row] + [ANY] * n_side, out_shape=[SDS((s, d), BF16)] + (side.out_shapes if side is not None else []),
        scratch_shapes=side.scratch if side is not None else [],
        compiler_params=_cp(("arbitrary",)))(x, g, *(side.ins if side is not None else []))
    return res if side is not None else res[0]


def _rms_bwd_rows(xv, gv, dy):
    r = lax.rsqrt(jnp.mean(xv * xv, axis=-1, keepdims=True) + RMS_EPS)
    xh = xv * r
    dxh = dy * gv
    dx = r * (dxh - xh * jnp.mean(dxh * xh, axis=-1, keepdims=True))
    return dx, jnp.sum(dy * xh, axis=0, keepdims=True)


def rmsnorm_bwd(x, g, dn, resid, name, bf16_copy=False):
    s, d = x.shape
    tm = _tile(s, 256)
    has_r = resid is not None

    def body(*refs):
        x_ref, g_ref, dn_ref = refs[:3]
        dx_ref, dg_ref = refs[3 + has_r:5 + has_r]
        dx, part = _rms_bwd_rows(x_ref[...], g_ref[...], dn_ref[...].astype(F32))
        if has_r:
            dx = dx + refs[3][...]
        dx_ref[...] = dx
        if bf16_copy:
            refs[5 + has_r][...] = dx.astype(BF16)

        @pl.when(pl.program_id(0) == 0)
        def _():
            dg_ref[...] = jnp.zeros_like(dg_ref)

        dg_ref[...] += part

    row = pl.BlockSpec((tm, d), lambda i: (i, 0))
    vec = pl.BlockSpec((1, d), lambda i: (0, 0))
    ins = [x, g, dn] + ([resid] if has_r else [])
    return pl.pallas_call(body, name=name, grid=(s // tm,), in_specs=[row, vec, row] + ([row] if has_r else []),
                          out_specs=[row, vec] + ([row] if bf16_copy else []),
                          out_shape=[SDS((s, d), F32), SDS((1, d), F32)] + ([SDS((s, d), BF16)] if bf16_copy else []),
                          compiler_params=_cp(("arbitrary",)))(*ins)


def merge(p_all, ya, yb, yc, wa, wb, wc, dm, lay, name):
    s, d = ya.shape[0], lay.d
    wcols = wa.shape[2]
    bwd = dm is not None
    tm, tn = _tile(s, 2048), _tile(wcols, 512)
    nj, per = d // tn, wcols // tn

    y_specs = [pl.BlockSpec((tm, y.shape[1]), lambda i, j, *_: (i, 0)) for y in (ya, yb, yc)]
    w_specs = [pl.BlockSpec((None, w.shape[1], tn), lambda i, j, *_: (j // per, 0, j % per)) for w in (wa, wb, wc)]
    o_spec = pl.BlockSpec((tm, tn), lambda i, j, *_: (i, j))
    if not bwd:
        def body(ga, gb, gc, ya_r, yb_r, yc_r, wa_r, wb_r, wc_r, o_ref):
            ts = [_dot(y[...], w[...], NN) for y, w in ((ya_r, wa_r), (yb_r, wb_r), (yc_r, wc_r))]
            gs = [_sigmoid(g[...]) for g in (ga, gb, gc)]
            o_ref[...] = (gs[0] * ts[0] + gs[1] * ts[1] + gs[2] * ts[2]).astype(BF16)

        gate_specs = [pl.BlockSpec((tm, tn), lambda i, j, b=b: (i, b * nj + j)) for b in range(3)]
        return pl.pallas_call(
            body, name=name, grid=(s // tm, nj), in_specs=gate_specs + y_specs + w_specs,
            out_specs=o_spec, out_shape=SDS((s, d), BF16),
            compiler_params=_cp(("parallel", "parallel")))(p_all, p_all, p_all, ya, yb, yc, wa, wb, wc)

    def body_bwd(g_r, ya_r, yb_r, yc_r, wa_r, wb_r, wc_r, dm_r, dg_o, dta_o, dtb_o, dtc_o):
        for k, (y, w, dt_o) in enumerate(((ya_r, wa_r, dta_o), (yb_r, wb_r, dtb_o), (yc_r, wc_r, dtc_o))):
            @pl.when(pl.program_id(2) == k)
            def _(y=y, w=w, dt_o=dt_o):
                t = _dot(y[...], w[...], NN)
                g = _sigmoid(g_r[...])
                dmv = dm_r[...]
                dg_o[...] = (dmv * t * (g * (1.0 - g))).astype(BF16)
                dt_o[...] = (dmv * g).astype(BF16)

    gate_spec = pl.BlockSpec((tm, tn), lambda i, j, b: (i, b * nj + j))
    return pl.pallas_call(
        body_bwd, name=name, grid=(s // tm, nj, 3), in_specs=[gate_spec] + y_specs + w_specs + [o_spec],
        out_specs=[gate_spec, o_spec, o_spec, o_spec], out_shape=[SDS((s, 3 * d), BF16)] + [SDS((s, d), BF16)] * 3,
        compiler_params=_cp(("arbitrary", "arbitrary", "arbitrary")))(p_all, ya, yb, yc, wa, wb, wc, dm)


SWA_PAIRS = SWA_G // 2


def _swa_probs(qs, kcs, sinks, first):
    shape = (qs[0].shape[0], 2 * SWA_W)
    qi = lax.broadcasted_iota(jnp.int32, shape, 0) % SWA_W
    kj = lax.broadcasted_iota(jnp.int32, shape, 1)
    mask = (kj > qi) & (kj <= qi + SWA_W) & ((kj >= SWA_W) | jnp.logical_not(first))
    ss = [jnp.where(mask, _dot(q, kc, NT) * (SWA_HD ** -0.5), NEG) for q, kc in zip(qs, kcs)]
    ms = [jnp.maximum(jnp.max(s, axis=-1, keepdims=True), sink) for s, sink in zip(ss, sinks)]
    ps = [jnp.exp(s - m) for s, m in zip(ss, ms)]
    es = [jnp.exp(sink - m) for sink, m in zip(sinks, ms)]
    inv = [1.0 / (jnp.sum(p, axis=-1, keepdims=True) + e) for p, e in zip(ps, es)]
    return [p * i for p, i in zip(ps, inv)], [e * i for e, i in zip(es, inv)]


def _swa_stack(ref, h):
    return jnp.concatenate([ref[:, pl.ds((h * SWA_PAIRS + p) * LANE, LANE)] for p in range(SWA_PAIRS)], axis=0)


def _swa_unstack(ref, h, val):
    for p in range(SWA_PAIRS):
        ref[:, pl.ds((h * SWA_PAIRS + p) * LANE, LANE)] = val[p * SWA_W:(p + 1) * SWA_W]


def _swa_sink_col(sk_ref, h, second):
    pair = lax.broadcasted_iota(jnp.int32, (SWA_PAIRS * SWA_W, 1), 0) // SWA_W
    col = jnp.zeros((SWA_PAIRS * SWA_W, 1), F32)
    for p in range(SWA_PAIRS):
        hh = h * SWA_G + 2 * p + second
        col = jnp.where(pair == p, sk_ref[0:1, hh:hh + 1], col)
    return col


def _swa_kv_tiles(cur_ref, prev_ref, h):
    t = jnp.concatenate([prev_ref[...], cur_ref[...]], axis=0)
    lane = lax.broadcasted_iota(jnp.int32, t.shape, 1)
    moved = pltpu.roll(t, SWA_HD, axis=1)
    low, high = (t, moved) if h == 0 else (moved, t)
    return jnp.where(lane < SWA_HD, low, 0.0).astype(BF16), jnp.where(lane >= SWA_HD, high, 0.0).astype(BF16)


def _swa_kv_grad(g_low, g_high, h):
    lane = lax.broadcasted_iota(jnp.int32, g_low.shape, 1)
    if h == 0:
        return jnp.where(lane < SWA_HD, g_low + pltpu.roll(g_high, SWA_HD, axis=1), 0.0)
    return jnp.where(lane >= SWA_HD, pltpu.roll(g_low, SWA_HD, axis=1) + g_high, 0.0)


def _swa_specs(lay):
    w = SWA_W
    q_spec = pl.BlockSpec((w, Q_W), lambda n: (n, _div(lay.q, Q_W)))
    cur = lambda off: pl.BlockSpec((w, KV_W), lambda n: (n, _div(off, KV_W)))
    prev = lambda off: pl.BlockSpec((w, KV_W), lambda n: (jnp.maximum(n - 1, 0), _div(off, KV_W)))
    return q_spec, cur(lay.k), prev(lay.k), cur(lay.v), prev(lay.v)


def swa_fwd(p_all, sinks, lay, name):
    s = p_all.shape[0]
    nb = _div(s, SWA_W)

    def body(q_ref, kc_ref, kp_ref, vc_ref, vp_ref, sk_ref, o_ref):
        first = pl.program_id(0) == 0
        units = [(h, e) for h in range(SWA_HKV) for e in range(2)]
        ks = [_swa_kv_tiles(kc_ref, kp_ref, h) for h in range(SWA_HKV)]
        vs = [_swa_kv_tiles(vc_ref, vp_ref, h) for h in range(SWA_HKV)]
        qs = [_swa_stack(q_ref, h).astype(BF16) for h in range(SWA_HKV)]
        ps, _ = _swa_probs([qs[h] for h, e in units], [ks[h][e] for h, e in units],
                           [_swa_sink_col(sk_ref, h, e) for h, e in units], first)
        os = [_dot(p.astype(BF16), vs[h][e], NN) for p, (h, e) in zip(ps, units)]
        for h in range(SWA_HKV):
            _swa_unstack(o_ref, h, (os[2 * h] + os[2 * h + 1]).astype(BF16))

    q_spec, kc_s, kp_s, vc_s, vp_s = _swa_specs(lay)
    return pl.pallas_call(
        body, name=name, grid=(nb,),
        in_specs=[q_spec, kc_s, kp_s, vc_s, vp_s, pl.BlockSpec(sinks.shape, lambda n: (0, 0))],
        out_specs=pl.BlockSpec((SWA_W, Q_W), lambda n: (n, 0)), out_shape=SDS((s, Q_W), BF16),
        compiler_params=_cp(("parallel",)))(p_all, p_all, p_all, p_all, p_all, sinks)


def swa_bwd(p_all, sinks, dy, lay, name):
    s = p_all.shape[0]
    nb = _div(s, SWA_W)
    w = SWA_W

    def body(q_ref, kc_ref, kp_ref, vc_ref, vp_ref, sk_ref, do_ref, dq_ref, dk_ref, dv_ref, ds_ref, kcar, vcar):
        n = pl.program_id(0)
        first = n == 0

        @pl.when(first)
        def _():
            kcar[...] = jnp.zeros_like(kcar)
            vcar[...] = jnp.zeros_like(vcar)
            ds_ref[...] = jnp.zeros_like(ds_ref)

        @pl.when(n < nb)
        def _():
            lane = lax.broadcasted_iota(jnp.int32, (1, LANE), 1)
            dsink = jnp.zeros((1, LANE), F32)
            units = [(h, e) for h in range(SWA_HKV) for e in range(2)]
            ks = [_swa_kv_tiles(kc_ref, kp_ref, h) for h in range(SWA_HKV)]
            vs = [_swa_kv_tiles(vc_ref, vp_ref, h) for h in range(SWA_HKV)]
            qs = [_swa_stack(q_ref, h).astype(BF16) for h in range(SWA_HKV)]
            dos = [_swa_stack(do_ref, h).astype(BF16) for h in range(SWA_HKV)]
            ps, psinks = _swa_probs([qs[h] for h, e in units], [ks[h][e] for h, e in units],
                                    [_swa_sink_col(sk_ref, h, e) for h, e in units], first)
            dps = [_dot(dos[h], vs[h][e], NT) for h, e in units]
            dvs = [_dot(p.astype(BF16), dos[h], TN) for p, (h, e) in zip(ps, units)]
            rss = [jnp.sum(dp * p, axis=-1, keepdims=True) for dp, p in zip(dps, ps)]
            dsb = [(p * (dp - rs) * (SWA_HD ** -0.5)).astype(BF16) for p, dp, rs in zip(ps, dps, rss)]
            dqs = [_dot(d, ks[h][e], NN) for d, (h, e) in zip(dsb, units)]
            dks = [_dot(d, qs[h], TN) for d, (h, e) in zip(dsb, units)]
            for u, (h, e) in enumerate(units):
                psr = psinks[u] * rss[u]
                for pr in range(SWA_PAIRS):
                    hh = h * SWA_G + 2 * pr + e
                    dsink = dsink + jnp.where(lane == hh, -jnp.sum(psr[pr * w:(pr + 1) * w], axis=0, keepdims=True), 0.0)
            dk_tile = jnp.zeros((2 * w, KV_W), F32)
            dv_tile = jnp.zeros((2 * w, KV_W), F32)
            for h in range(SWA_HKV):
                _swa_unstack(dq_ref, h, (dqs[2 * h] + dqs[2 * h + 1]).astype(BF16))
                dk_tile = dk_tile + _swa_kv_grad(dks[2 * h], dks[2 * h + 1], h)
                dv_tile = dv_tile + _swa_kv_grad(dvs[2 * h], dvs[2 * h + 1], h)
            dk_ref[...] = (kcar[...] + dk_tile[:w]).astype(BF16)
            dv_ref[...] = (vcar[...] + dv_tile[:w]).astype(BF16)
            kcar[...] = dk_tile[w:]
            vcar[...] = dv_tile[w:]
            ds_ref[...] += dsink

        @pl.when(n == nb)
        def _():
            dk_ref[...] = kcar[...].astype(BF16)
            dv_ref[...] = vcar[...].astype(BF16)

    last = nb - 1
    q_spec = pl.BlockSpec((w, Q_W), lambda n: (jnp.minimum(n, last), _div(lay.q, Q_W)))
    cur = lambda off: pl.BlockSpec((w, KV_W), lambda n: (jnp.minimum(n, last), _div(off, KV_W)))
    prev = lambda off: pl.BlockSpec((w, KV_W), lambda n: (jnp.clip(n - 1, 0, last), _div(off, KV_W)))
    row = pl.BlockSpec((w, Q_W), lambda n: (jnp.minimum(n, last), 0))
    kv_out = pl.BlockSpec((w, KV_W), lambda n: (jnp.maximum(n - 1, 0), 0))
    return pl.pallas_call(
        body, name=name, grid=(nb + 1,),
        in_specs=[q_spec, cur(lay.k), prev(lay.k), cur(lay.v), prev(lay.v), pl.BlockSpec(sinks.shape, lambda n: (0, 0)), row],
        out_specs=[row, kv_out, kv_out, pl.BlockSpec((1, LANE), lambda n: (0, 0))],
        out_shape=[SDS((s, Q_W), BF16), SDS((s, KV_W), BF16), SDS((s, KV_W), BF16), SDS((1, LANE), F32)],
        scratch_shapes=[pltpu.VMEM((w, KV_W), F32), pltpu.VMEM((w, KV_W), F32)],
        compiler_params=_cp(("arbitrary",)))(p_all, p_all, p_all, p_all, p_all, sinks, dy)


def _xa_probs(qs, mks):
    ss = [_dot(q, mk, NT) * (XA_D ** -0.5) for q, mk in zip(qs, mks)]
    ps = [jnp.exp(s - jnp.max(s, axis=-1, keepdims=True)) for s in ss]
    inv = [1.0 / jnp.sum(p, axis=-1, keepdims=True) for p in ps]
    return [p * i for p, i in zip(ps, inv)]


def xattn_fwd(p_all, mkv, lay, name):
    s, nm = p_all.shape[0], mkv.shape[0]
    tm = _tile(s, 512)

    def body(q_ref, mkv_ref, o_ref):
        heads = range(XA_H)
        cols = [pl.ds(h * XA_D, XA_D) for h in heads]
        ps = _xa_probs([q_ref[:, c].astype(BF16) for c in cols], [mkv_ref[:, c] for c in cols])
        os = [_dot(ps[h].astype(BF16), mkv_ref[:, pl.ds(XA_W + h * XA_D, XA_D)], NN) for h in heads]
        for h in heads:
            o_ref[:, cols[h]] = os[h].astype(BF16)

    return pl.pallas_call(
        body, name=name, grid=(s // tm,),
        in_specs=[pl.BlockSpec((tm, XA_W), lambda i: (i, _div(lay.qc, XA_W))), pl.BlockSpec((nm, 2 * XA_W), lambda i: (0, 0))],
        out_specs=pl.BlockSpec((tm, XA_W), lambda i: (i, 0)), out_shape=SDS((s, XA_W), BF16),
        compiler_params=_cp(("parallel",)))(p_all, mkv)


def xattn_bwd(p_all, mkv, dy, lay, name):
    s, nm = p_all.shape[0], mkv.shape[0]
    tm = _tile(s, 512)

    def body(q_ref, mkv_ref, do_ref, dq_ref, dmkv_ref):
        @pl.when(pl.program_id(0) == 0)
        def _():
            dmkv_ref[...] = jnp.zeros_like(dmkv_ref)

        heads = range(XA_H)
        cols = [pl.ds(h * XA_D, XA_D) for h in heads]
        vcols = [pl.ds(XA_W + h * XA_D, XA_D) for h in heads]
        qs = [q_ref[:, c].astype(BF16) for c in cols]
        dos = [do_ref[:, c].astype(BF16) for c in cols]
        ps = _xa_probs(qs, [mkv_ref[:, c] for c in cols])
        dps = [_dot(dos[h], mkv_ref[:, vcols[h]], NT) for h in heads]
        dvs = [_dot(ps[h].astype(BF16), dos[h], TN) for h in heads]
        dsb = [(p * (dp - jnp.sum(dp * p, axis=-1, keepdims=True)) * (XA_D ** -0.5)).astype(BF16) for p, dp in zip(ps, dps)]
        dqs = [_dot(dsb[h], mkv_ref[:, cols[h]], NN) for h in heads]
        dks = [_dot(dsb[h], qs[h], TN) for h in heads]
        for h in heads:
            dq_ref[:, cols[h]] = dqs[h].astype(BF16)
            dmkv_ref[:, vcols[h]] += dvs[h]
            dmkv_ref[:, cols[h]] += dks[h]

    row = pl.BlockSpec((tm, XA_W), lambda i: (i, 0))
    full = pl.BlockSpec((nm, 2 * XA_W), lambda i: (0, 0))
    return pl.pallas_call(
        body, name=name, grid=(s // tm,),
        in_specs=[pl.BlockSpec((tm, XA_W), lambda i: (i, _div(lay.qc, XA_W))), full, row],
        out_specs=[row, full], out_shape=[SDS((s, XA_W), BF16), SDS((nm, 2 * XA_W), F32)],
        compiler_params=_cp(("arbitrary",)))(p_all, mkv, dy)


def _shift_down(cur, prev8, s):
    cat = jnp.concatenate([prev8, cur[0:8]], axis=0)
    return pltpu.roll(cur, s, axis=0), pltpu.roll(cat, s, axis=0)[8:16]


def _shift_up(cur, next8, s):
    tm = cur.shape[0]
    cat = jnp.concatenate([cur[tm - 8:tm], next8], axis=0)
    return pltpu.roll(cur, tm - s, axis=0), pltpu.roll(cat, 16 - s, axis=0)[0:8]


def gdn_conv_fwd(p_all, conv_w, lay, name):
    s = p_all.shape[0]
    tm = _tile(s, 512)
    c0 = _div(lay.qkv, GDN_W)

    def body(x_ref, prev_ref, w_ref, o_ref):
        cur = x_ref[...]
        prev8 = jnp.where(pl.program_id(1) > 0, prev_ref[...], 0.0)
        main = w_ref[GDN_CONV - 1:GDN_CONV, :] * cur
        top = w_ref[GDN_CONV - 1:GDN_CONV, :] * cur[0:8]
        for sft in range(1, GDN_CONV):
            wi = w_ref[GDN_CONV - 1 - sft:GDN_CONV - sft, :]
            a, b = _shift_down(cur, prev8, sft)
            main = main + wi * a
            top = top + wi * b
        o_ref[...] = main
        o_ref[0:8, :] = top

    return pl.pallas_call(
        body, name=name, grid=(3, s // tm),
        in_specs=[pl.BlockSpec((tm, GDN_W), lambda c, i: (i, c0 + c)),
                  pl.BlockSpec((8, GDN_W), lambda c, i: (jnp.maximum(i * (tm // 8) - 1, 0), c0 + c)),
                  pl.BlockSpec((GDN_CONV, GDN_W), lambda c, i: (0, c))],
        out_specs=pl.BlockSpec((tm, GDN_W), lambda c, i: (i, c)), out_shape=SDS((s, 3 * GDN_W), F32),
        compiler_params=_cp(("parallel", "parallel")))(p_all, p_all, conv_w)


def gdn_conv_bwd(p_all, conv_w, dxc, lay, name):
    s = p_all.shape[0]
    tm = _tile(s, 512)
    c0 = _div(lay.qkv, GDN_W)
    nt = s // tm

    def body(x_ref, prev_ref, d_ref, next_ref, w_ref, dx_ref, dw_ref):
        i = pl.program_id(1)
        cur, d = x_ref[...], d_ref[...]
        prev8 = jnp.where(i > 0, prev_ref[...], 0.0)
        next8 = jnp.where(i < nt - 1, next_ref[...], 0.0)
        row = lax.broadcasted_iota(jnp.int32, (tm, 1), 0)
        main = w_ref[GDN_CONV - 1:GDN_CONV, :] * d
        bot = w_ref[GDN_CONV - 1:GDN_CONV, :] * d[tm - 8:tm]
        dws = [jnp.sum(d * cur, axis=0, keepdims=True)]
        for sft in range(1, GDN_CONV):
            wi = w_ref[GDN_CONV - 1 - sft:GDN_CONV - sft, :]
            a, b = _shift_up(d, next8, sft)
            main = main + wi * a
            bot = bot + wi * b
            xa, xb = _shift_down(cur, prev8, sft)
            dws.append(jnp.sum(jnp.where(row >= 8, d * xa, 0.0), axis=0, keepdims=True)
                       + jnp.sum(d[0:8] * xb, axis=0, keepdims=True))
        dx_ref[...] = main.astype(BF16)
        dx_ref[tm - 8:tm, :] = bot.astype(BF16)

        @pl.when(i == 0)
        def _():
            dw_ref[...] = jnp.zeros_like(dw_ref)

        for sft in range(GDN_CONV):
            dw_ref[GDN_CONV - 1 - sft:GDN_CONV - sft, :] += dws[sft]

    return pl.pallas_call(
        body, name=name, grid=(3, nt),
        in_specs=[pl.BlockSpec((tm, GDN_W), lambda c, i: (i, c0 + c)),
                  pl.BlockSpec((8, GDN_W), lambda c, i: (jnp.maximum(i * (tm // 8) - 1, 0), c0 + c)),
                  pl.BlockSpec((tm, GDN_W), lambda c, i: (i, c)),
                  pl.BlockSpec((8, GDN_W), lambda c, i: (jnp.minimum((i + 1) * (tm // 8), s // 8 - 1), c)),
                  pl.BlockSpec((GDN_CONV, GDN_W), lambda c, i: (0, c))],
        out_specs=[pl.BlockSpec((tm, GDN_W), lambda c, i: (i, c)), pl.BlockSpec((GDN_CONV, GDN_W), lambda c, i: (0, c))],
        out_shape=[SDS((s, 3 * GDN_W), BF16), SDS((GDN_CONV, 3 * GDN_W), F32)],
        compiler_params=_cp(("parallel", "arbitrary")))(p_all, p_all, dxc, dxc, conv_w)


def _gdn_chunk(xq, xk, xv, ab, gp, bdot=_bdot_plain):
    c = GDN_C
    nc = xq.shape[0] // c
    lane = lax.broadcasted_iota(jnp.int32, (c, LANE), 1)
    row = lax.broadcasted_iota(jnp.int32, (c, c), 0)
    col = lax.broadcasted_iota(jnp.int32, (c, c), 1)
    g_tile = -jnp.exp(gp[0:1, :]) * _softplus(ab + gp[1:2, :])
    b_tile = _sigmoid(ab)
    tri = (row >= col).astype(F32)
    qa, ka, va = _silu(xq), _silu(xk), _silu(xv)
    items = []
    for ci in range(nc):
        rs = slice(ci * c, (ci + 1) * c)
        gcum = _dot(tri, g_tile[rs], NN, HI)
        gcum_t = gcum.T
        for h in range(GDN_H):
            hs = slice(h * GDN_D, (h + 1) * GDN_D)
            q, k, v = qa[rs, hs], ka[rs, hs], va[rs, hs]
            q = q * lax.rsqrt(jnp.sum(q * q, axis=-1, keepdims=True) + L2_EPS) * (GDN_D ** -0.5)
            k = k * lax.rsqrt(jnp.sum(k * k, axis=-1, keepdims=True) + L2_EPS)
            gc = jnp.sum(jnp.where(lane == h, gcum, 0.0), axis=1, keepdims=True)
            beta = jnp.sum(jnp.where(lane == GDN_H + h, b_tile[rs], 0.0), axis=1, keepdims=True)
            decay = jnp.exp(jnp.where(row >= col, gc - gcum_t[h:h + 1, :], NEG))
            items.append((q, k, v, gc, beta, decay))
    kks = [bdot(k, k, NT) for (_, k, _, _, _, _) in items]
    xs = tuple(-jnp.where(row > col, it[4] * kk * it[5], 0.0) for it, kk in zip(items, kks))
    nns = _neumann(xs) if bdot is _bdot_plain else _neumann_vjp(xs)
    qks = [bdot(q, k, NT) for (q, k, _, _, _, _) in items]
    out = []
    for (q, k, v, gc, beta, decay), n, qk in zip(items, nns, qks):
        eg = jnp.exp(gc)
        vb = v * beta
        kbe = k * (beta * eg)
        gl = gc[c - 1:c, :]
        out.append((vb + bdot(n, vb, NN), kbe + bdot(n, kbe, NN), q * eg, k * jnp.exp(gl - gc), qk * decay, jnp.exp(gl)))
    return [out[ci * GDN_H:(ci + 1) * GDN_H] for ci in range(nc)]


GDN_CPS = 4


def _gdn_pre_specs(lay, t):
    xspec = lambda j: pl.BlockSpec((t, GDN_W), lambda n, j=j: (n, j))
    return [xspec(0), xspec(1), xspec(2), pl.BlockSpec((t, LANE), lambda n: (n, _div(lay.ab, LANE))),
            pl.BlockSpec((8, LANE), lambda n: (0, 0))]


def gdn_pre_fwd(xc, p_all, gp, lay, name):
    s = xc.shape[0]
    c = GDN_C
    n = _div(s, c)
    cps = _tile(n, GDN_CPS)
    t = cps * c

    def body(xq, xk, xv, ab, gp_ref, u_ref, w_ref, qd_ref, kd_ref, qk_ref, gl_ref):
        lane = lax.broadcasted_iota(jnp.int32, (1, LANE), 1)
        chunks = _gdn_chunk(xq[...], xk[...], xv[...], ab[...], gp_ref[...])
        for ci, heads in enumerate(chunks):
            rs = pl.ds(ci * c, c)
            gl_row = jnp.zeros((1, LANE), F32)
            for h, (u, w, qd, kd, qk, gl) in enumerate(heads):
                hs = pl.ds(h * GDN_D, GDN_D)
                u_ref[rs, hs] = u
                w_ref[rs, hs] = w.astype(BF16)
                qd_ref[rs, hs] = qd.astype(BF16)
                kd_ref[rs, hs] = kd.astype(BF16)
                qk_ref[rs, pl.ds(h * c, c)] = qk.astype(BF16)
                gl_row = gl_row + jnp.where(lane == h, gl, 0.0)
            gl_ref[ci] = gl_row

    row = pl.BlockSpec((t, GDN_W), lambda n: (n, 0))
    return pl.pallas_call(
        body, name=name, grid=(n // cps,), in_specs=_gdn_pre_specs(lay, t),
        out_specs=[row, row, row, row, pl.BlockSpec((t, GDN_H * c), lambda n: (n, 0)), pl.BlockSpec((cps, 1, LANE), lambda n: (n, 0, 0))],
        out_shape=[SDS((s, GDN_W), F32), SDS((s, GDN_W), BF16), SDS((s, GDN_W), BF16), SDS((s, GDN_W), BF16),
                   SDS((s, GDN_H * c), BF16), SDS((n, 1, LANE), F32)],
        compiler_params=_cp(("parallel",)))(xc, xc, xc, p_all, gp)


def gdn_pre_bwd(xc, p_all, gp, du, dw, dqd, dkd, dqk, dgl, lay, name):
    s = xc.shape[0]
    c = GDN_C
    n = _div(s, c)
    cps = _tile(n, GDN_CPS)
    t = cps * c
    chunk = functools.partial(_gdn_chunk, bdot=_bdot_vjp)

    def body(xq, xk, xv, ab, gp_ref, du_r, dw_r, dqd_r, dkd_r, dqk_r, dgl_r, dxc_ref, dab_ref, dgp_ref):
        lane = lax.broadcasted_iota(jnp.int32, (1, LANE), 1)
        _, vjp = jax.vjp(chunk, xq[...], xk[...], xv[...], ab[...], gp_ref[...])
        cts = []
        for ci in range(cps):
            rs = pl.ds(ci * c, c)
            heads = []
            for h in range(GDN_H):
                hs = pl.ds(h * GDN_D, GDN_D)
                dgl_h = jnp.sum(jnp.where(lane == h, dgl_r[ci], 0.0), axis=1, keepdims=True)
                heads.append((du_r[rs, hs], dw_r[rs, hs], dqd_r[rs, hs], dkd_r[rs, hs], dqk_r[rs, pl.ds(h * c, c)], dgl_h))
            cts.append(heads)
        dq, dk, dv, dab, dgp = vjp(cts)
        dxc_ref[:, pl.ds(0, GDN_W)] = dq
        dxc_ref[:, pl.ds(GDN_W, GDN_W)] = dk
        dxc_ref[:, pl.ds(2 * GDN_W, GDN_W)] = dv
        dab_ref[...] = dab.astype(BF16)

        @pl.when(pl.program_id(0) == 0)
        def _():
            dgp_ref[...] = jnp.zeros_like(dgp_ref)

        dgp_ref[...] += dgp

    row = pl.BlockSpec((t, GDN_W), lambda n: (n, 0))
    return pl.pallas_call(
        body, name=name, grid=(n // cps,),
        in_specs=_gdn_pre_specs(lay, t) + [row, row, row, row, pl.BlockSpec((t, GDN_H * c), lambda n: (n, 0)),
                                           pl.BlockSpec((cps, 1, LANE), lambda n: (n, 0, 0))],
        out_specs=[pl.BlockSpec((t, 3 * GDN_W), lambda n: (n, 0)), pl.BlockSpec((t, LANE), lambda n: (n, 0)),
                   pl.BlockSpec((8, LANE), lambda n: (0, 0))],
        out_shape=[SDS((s, 3 * GDN_W), F32), SDS((s, LANE), BF16), SDS((8, LANE), F32)],
        compiler_params=_cp(("arbitrary",)))(xc, xc, xc, p_all, gp, du, dw, dqd, dkd, dqk, dgl)


def _lane_scalar(row, h):
    lane = lax.broadcasted_iota(jnp.int32, row.shape, 1)
    return jnp.sum(jnp.where(lane == h, row, 0.0), axis=1, keepdims=True)


def gdn_scan_fwd(u, w, qd, kd, qk, gl, name):
    s = u.shape[0]
    c = GDN_C
    n = _div(s, c)
    cps = _tile(n, GDN_CPS)
    t = cps * c

    def body(u_r, w_r, qd_r, kd_r, qk_r, gl_r, o_ref, s_ref, st):
        @pl.when(pl.program_id(0) == 0)
        def _():
            st[...] = jnp.zeros_like(st)

        heads = range(GDN_H)
        hs = [pl.ds(h * GDN_D, GDN_D) for h in heads]
        for ci in range(cps):
            rs = pl.ds(ci * c, c)
            s_ref[ci] = st[...]
            sh = [st[hs[h], :] for h in heads]
            shb = [x.astype(BF16) for x in sh]
            ws = [_dot(w_r[rs, hs[h]], shb[h], NN) for h in heads]
            qs = [_dot(qd_r[rs, hs[h]], shb[h], NN) for h in heads]
            vb = [(u_r[rs, hs[h]] - ws[h]).astype(BF16) for h in heads]
            ov = [_dot(qk_r[rs, pl.ds(h * c, c)], vb[h], NN) for h in heads]
            kv = [_dot(kd_r[rs, hs[h]], vb[h], TN) for h in heads]
            for h in heads:
                o_ref[rs, hs[h]] = qs[h] + ov[h]
                st[hs[h], :] = sh[h] * _lane_scalar(gl_r[ci], h) + kv[h]

    row = pl.BlockSpec((t, GDN_W), lambda i: (i, 0))
    return pl.pallas_call(
        body, name=name, grid=(n // cps,),
        in_specs=[row, row, row, row, pl.BlockSpec((t, GDN_H * c), lambda i: (i, 0)), pl.BlockSpec((cps, 1, LANE), lambda i: (i, 0, 0))],
        out_specs=[row, pl.BlockSpec((cps, GDN_W, GDN_D), lambda i: (i, 0, 0))],
        out_shape=[SDS((s, GDN_W), F32), SDS((n, GDN_W, GDN_D), F32)],
        scratch_shapes=[pltpu.VMEM((GDN_W, GDN_D), F32)],
        compiler_params=_cp(("arbitrary",)))(u, w, qd, kd, qk, gl)


def gdn_scan_bwd(u, w, qd, kd, qk, gl, states, do, name):
    s = u.shape[0]
    c = GDN_C
    n = _div(s, c)
    cps = _tile(n, GDN_CPS)
    t = cps * c
    steps = n // cps

    def body(u_r, w_r, qd_r, kd_r, qk_r, gl_r, s_r, do_r, du_o, dw_o, dqd_o, dkd_o, dqk_o, dgl_o, dst):
        @pl.when(pl.program_id(0) == 0)
        def _():
            dst[...] = jnp.zeros_like(dst)

        lane = lax.broadcasted_iota(jnp.int32, (1, LANE), 1)
        heads = range(GDN_H)
        hs = [pl.ds(h * GDN_D, GDN_D) for h in heads]
        qs = [pl.ds(h * c, c) for h in heads]
        for ci in reversed(range(cps)):
            rs = pl.ds(ci * c, c)
            sh = [s_r[ci, hs[h], :] for h in heads]
            shb = [x.astype(BF16) for x in sh]
            ds_out = [dst[hs[h], :] for h in heads]
            dsb = [x.astype(BF16) for x in ds_out]
            dob = [do_r[rs, hs[h]].astype(BF16) for h in heads]
            ws = [_dot(w_r[rs, hs[h]], shb[h], NN) for h in heads]
            dv1 = [_dot(qk_r[rs, qs[h]], dob[h], TN) for h in heads]
            dv2 = [_dot(kd_r[rs, hs[h]], dsb[h], NN) for h in heads]
            dqd = [_dot(dob[h], shb[h], NT) for h in heads]
            dsq = [_dot(qd_r[rs, hs[h]], dob[h], TN) for h in heads]
            vb = [(u_r[rs, hs[h]] - ws[h]).astype(BF16) for h in heads]
            dv = [dv1[h] + dv2[h] for h in heads]
            dvb = [x.astype(BF16) for x in dv]
            dw = [_dot(dvb[h], shb[h], NT) for h in heads]
            dkd = [_dot(vb[h], dsb[h], NT) for h in heads]
            dqk = [_dot(dob[h], vb[h], NT) for h in heads]
            dsw = [_dot(w_r[rs, hs[h]], dvb[h], TN) for h in heads]
            dgl_row = jnp.zeros((1, LANE), F32)
            for h in heads:
                du_o[rs, hs[h]] = dv[h]
                dw_o[rs, hs[h]] = -dw[h]
                dqd_o[rs, hs[h]] = dqd[h]
                dkd_o[rs, hs[h]] = dkd[h]
                dqk_o[rs, qs[h]] = dqk[h]
                dgl_row = dgl_row + jnp.where(lane == h, jnp.sum(jnp.sum(ds_out[h] * sh[h], axis=1, keepdims=True), axis=0, keepdims=True), 0.0)
                dst[hs[h], :] = ds_out[h] * _lane_scalar(gl_r[ci], h) + dsq[h] - dsw[h]
            dgl_o[ci] = dgl_row

    rev = lambda i: steps - 1 - i
    row = pl.BlockSpec((t, GDN_W), lambda i: (rev(i), 0))
    qks = pl.BlockSpec((t, GDN_H * c), lambda i: (rev(i), 0))
    gls = pl.BlockSpec((cps, 1, LANE), lambda i: (rev(i), 0, 0))
    return pl.pallas_call(
        body, name=name, grid=(steps,),
        in_specs=[row, row, row, row, qks, gls, pl.BlockSpec((cps, GDN_W, GDN_D), lambda i: (rev(i), 0, 0)), row],
        out_specs=[row, row, row, row, qks, gls],
        out_shape=[SDS((s, GDN_W), F32)] * 4 + [SDS((s, GDN_H * c), F32), SDS((n, 1, LANE), F32)],
        scratch_shapes=[pltpu.VMEM((GDN_W, GDN_D), F32)],
        compiler_params=_cp(("arbitrary",)))(u, w, qd, kd, qk, gl, states, do)


def _gdn_out_rows(o, z, nw):
    outs = []
    for h in range(GDN_H):
        hs = slice(h * GDN_D, (h + 1) * GDN_D)
        oh = o[:, hs]
        y = oh * lax.rsqrt(jnp.mean(oh * oh, axis=-1, keepdims=True) + RMS_EPS) * nw
        outs.append(y * _silu(z[:, hs]))
    return jnp.concatenate(outs, axis=1)


def gdn_out(o, p_all, nw, dy, lay, name):
    s = o.shape[0]
    tm = _tile(s, 512)
    bwd = dy is not None

    def body(*refs):
        o_r, z_r, nw_r = refs[:3]
        if not bwd:
            refs[3][...] = _gdn_out_rows(o_r[...], z_r[...], nw_r[...]).astype(BF16)
            return
        dy_r, do_o, dz_o, dnw_o = refs[3:]
        _, vjp = jax.vjp(_gdn_out_rows, o_r[...], z_r[...], nw_r[...])
        d_o, d_z, d_nw = vjp(dy_r[...].astype(F32))
        do_o[...] = d_o
        dz_o[...] = d_z.astype(BF16)

        @pl.when(pl.program_id(0) == 0)
        def _():
            dnw_o[...] = jnp.zeros_like(dnw_o)

        dnw_o[...] += d_nw

    row = pl.BlockSpec((tm, GDN_W), lambda i: (i, 0))
    zs = pl.BlockSpec((tm, GDN_W), lambda i: (i, _div(lay.z, GDN_W)))
    nws = pl.BlockSpec((1, GDN_D), lambda i: (0, 0))
    if not bwd:
        return pl.pallas_call(body, name=name, grid=(s // tm,), in_specs=[row, zs, nws], out_specs=row,
                              out_shape=SDS((s, GDN_W), BF16), compiler_params=_cp(("parallel",)))(o, p_all, nw)
    return pl.pallas_call(body, name=name, grid=(s // tm,), in_specs=[row, zs, nws, row], out_specs=[row, row, nws],
                          out_shape=[SDS((s, GDN_W), F32), SDS((s, GDN_W), BF16), SDS((1, GDN_D), F32)],
                          compiler_params=_cp(("arbitrary",)))(o, p_all, nw, dy)


def _cols_to_full(g):
    n, k, c = g.shape
    return g.transpose(1, 0, 2).reshape(k, n * c)


def _rows_to_blocks(w):
    return w.reshape(N_DEV, w.shape[0] // N_DEV, w.shape[1])


def _pack_small(parts, rows):
    flat = jnp.concatenate([jnp.pad(p.reshape(-1), (0, -p.size % LANE)) for p in parts])
    return jnp.pad(flat, (0, rows * LANE - flat.size)).reshape(rows, LANE)


def kernel(x, mem, g_mix, w_in, sinks, conv_w, a_log, dt_bias, gdn_norm_w, g_mem, w_mem_kv, w_swa_up, w_gdn_up, w_xa_up, w_out, g_mlp, w_mlp_in, w_mlp_out, g_final, loss_target, m_g_mix, m_w_in, m_sinks, m_conv_w, m_a_log, m_dt_bias, m_gdn_norm_w, m_g_mem, m_w_mem_kv, m_w_swa_up, m_w_gdn_up, m_w_xa_up, m_w_out, m_g_mlp, m_w_mlp_in, m_w_mlp_out, m_g_final, v_g_mix, v_w_in, v_sinks, v_conv_w, v_a_log, v_dt_bias, v_gdn_norm_w, v_g_mem, v_w_mem_kv, v_w_swa_up, v_w_gdn_up, v_w_xa_up, v_w_out, v_g_mlp, v_w_mlp_in, v_w_mlp_out, v_g_final):
    xs, ms, tgt = x[0], mem[0], loss_target[0]
    s, d = xs.shape
    lay = Layout(d)
    px, py, pc = _position()
    dev = 4 * px + 2 * py + pc

    n1, g_in, g_conv = rmsnorm_fwd(xs, g_mix, "norm_mix", side=GatherJob([w_in[0].astype(BF16), conv_w[0]]))
    W_in = pad_w_in(g_in, lay)
    convw = _cols_to_full(g_conv)
    gp = jnp.zeros((8, LANE), F32).at[0, :GDN_H].set(a_log[0]).at[1, :GDN_H].set(dt_bias[0])
    later = [w_mem_kv[0], w_swa_up[0], w_gdn_up[0], w_xa_up[0], w_out[0], w_mlp_in[0]]

    p_all, g_mkv, W_sup, W_gup, W_xup, g_out, W_m1 = matmul(
        n1, W_in, mode="nn", out_dtype=F32, name="proj_in", tm=2048, tn=1024, tk=d,
        side=GatherJob([w.astype(BF16) for w in later]))
    W_mkv = g_mkv.reshape(-1, g_mkv.shape[2])
    W_out = g_out.reshape(-1, d)
    y_a = swa_fwd(p_all, sinks, lay, "swa_fwd")
    xc = gdn_conv_fwd(p_all, convw, lay, "gdn_conv_fwd")
    u, gw, gqd, gkd, gqk, ggl = gdn_pre_fwd(xc, p_all, gp, lay, "gdn_pre_fwd")
    o_b, states = gdn_scan_fwd(u, gw, gqd, gkd, gqk, ggl, "gdn_scan_fwd")
    y_b = gdn_out(o_b, p_all, gdn_norm_w, None, lay, "gdn_out_fwd")
    nm = rmsnorm_fwd(ms, g_mem, "norm_mem")
    mkv = matmul(nm, W_mkv, mode="nn", out_dtype=BF16, name="proj_mem", tk=d)
    y_c = xattn_fwd(p_all, mkv, lay, "xattn_fwd")
    merged = merge(p_all, y_a, y_b, y_c, W_sup, W_gup, W_xup, None, lay, "merge_fwd")
    h1, n2 = matmul(merged, W_out, mode="nn", out_dtype=F32, name="proj_out", tm=512, tn=d, tk=d, resid=xs, rms_gain=g_mlp)
    uu, act, g_m2 = matmul(n2, W_m1, mode="nn", out_dtype=F32, name="mlp_in", tm=2048, tn=512, tk=d, b_cols=True,
                           relu2_out=True, side=GatherJob([w_mlp_out[0].astype(BF16)]))
    W_m2 = g_m2.reshape(-1, d)
    dh2, dh2_b, dg_final, lrow = matmul(act, W_m2, mode="nn", out_dtype=F32, name="mlp_out_loss", tm=512, tn=d, tk=1024,
                                        resid=h1, rms_gain=g_final.reshape(1, d), loss_target=tgt)
    loss = lax.psum(lrow[0, 0], ("x", "y", "c"))

    du = matmul(dh2_b, W_m2, mode="nt", out_dtype=BF16, name="mlp_out_dx", tm=2048, tn=512, tk=d, relu2_grad_of=uu)
    dW_m2 = matmul(act, dh2_b, mode="tn", out_dtype=BF16, name="mlp_out_dw", tm=1024, tn=2048, tk=1024)
    dW_m2 = _rows_to_blocks(dW_m2)
    dn2, sib_m2 = matmul(du, W_m1, mode="nt", out_dtype=F32, name="mlp_in_dx", tm=1024, tn=2048, tk=1024, b_cols=True,
                         side=PairExchangeJob([dW_m2], [False]))
    c_m2 = pair_add(dW_m2, False, sib_m2, "grads_pair_add_m2")
    dW_m1 = matmul(n2, du, mode="tn", out_dtype=BF16, name="mlp_in_dw", tm=2048, tn=1024, tk=1024)
    dh1, dg_mlp, dh1_b = rmsnorm_bwd(h1, g_mlp, dn2, dh2, "norm_mlp_bwd", bf16_copy=True)

    dmerged, sib_m1 = matmul(dh1_b, W_out, mode="nt", out_dtype=F32, name="proj_out_dx", tm=2048, tn=512, tk=d,
                             side=PairExchangeJob([dW_m1], [True]))
    c_m1 = pair_add(dW_m1, True, sib_m1, "grads_pair_add_m1")
    dW_out = matmul(merged, dh1_b, mode="tn", out_dtype=BF16, name="proj_out_dw", tm=2048, tn=1024, tk=1024)
    dgates, dta, dtb, dtc = merge(p_all, y_a, y_b, y_c, W_sup, W_gup, W_xup, dmerged, lay, "merge_bwd")
    dy_a = matmul(dta, W_sup, mode="nt", out_dtype=BF16, name="swa_up_dx", tm=2048, tk=d, b_cols=True)
    dy_b = matmul(dtb, W_gup, mode="nt", out_dtype=BF16, name="gdn_up_dx", tm=2048, tk=d, b_cols=True)
    dy_c = matmul(dtc, W_xup, mode="nt", out_dtype=BF16, name="xa_up_dx", tm=2048, tk=d, b_cols=True)
    dW_sup = matmul(y_a, dta, mode="tn", out_dtype=BF16, name="swa_up_dw", tn=2048, tk=2048)
    dW_gup = matmul(y_b, dtb, mode="tn", out_dtype=BF16, name="gdn_up_dw", tn=2048, tk=2048)
    dW_xup = matmul(y_c, dtc, mode="tn", out_dtype=BF16, name="xa_up_dw", tn=2048, tk=2048)

    dq_a, dk_a, dv_a, dsinks = swa_bwd(p_all, sinks, dy_a, lay, "swa_bwd")
    dq_c, dmkv = xattn_bwd(p_all, mkv, dy_c, lay, "xattn_bwd")
    dW_mkv = matmul(nm, dmkv, mode="tn", out_dtype=BF16, name="proj_mem_dw", tk=256)
    dnm = matmul(dmkv, W_mkv, mode="nt", out_dtype=F32, name="proj_mem_dx", tk=1024)
    _, dg_mem = rmsnorm_bwd(ms, g_mem, dnm, None, "norm_mem_bwd")

    do_b, dz, dnorm_w = gdn_out(o_b, p_all, gdn_norm_w, dy_b, lay, "gdn_out_bwd")
    du_g, dw_g, dqd_g, dkd_g, dqk_g, dgl_g = gdn_scan_bwd(u, gw, gqd, gkd, gqk, ggl, states, do_b, "gdn_scan_bwd")
    dxc, dab, dgp = gdn_pre_bwd(xc, p_all, gp, du_g, dw_g, dqd_g, dkd_g, dqk_g, dgl_g, lay, "gdn_pre_bwd")
    dqkv, dconv = gdn_conv_bwd(p_all, convw, dxc, lay, "gdn_conv_bwd")

    drest = jnp.concatenate([dq_a, dqkv, dz, dq_c, dk_a, dv_a, dab, jnp.zeros((s, lay.pw - lay.end), BF16)], axis=1)
    def pair_stage(grads, cols, tag):
        from_sib = run_job(PairExchangeJob(grads, cols), "grads_pair_exchange_" + tag)
        return [pair_add(g, cl, o, "grads_pair_add_%s%d" % (tag, i)) for i, (g, cl, o) in enumerate(zip(grads, cols, from_sib))]

    small = pair_stage([_rows_to_blocks(dW_mkv), dW_sup, dW_gup, dW_xup, _rows_to_blocks(dW_out)],
                       [False, True, True, True, False], "a")
    dW_in, p_m1, p_m2 = matmul(n1, dgates, tail=drest, mode="tn", out_dtype=BF16, name="proj_in_dw", tm=2048, tn=1024, tk=1024,
                               side=ChipExchangeJob([c_m1, c_m2]))
    late = pair_stage([unpad_dw_in(dW_in, lay)], [False], "b")
    dn1, p_in, p_mkv, p_sup, p_gup, p_xup, p_out = matmul(
        dgates, W_in, tail=drest, mode="nt", out_dtype=F32, name="proj_in_dx", tm=1024, tn=2048, tk=1024,
        side=ChipExchangeJob(late + small))
    grad_x, dg_mix = rmsnorm_bwd(xs, g_mix, dn1, dh1, "norm_mix_bwd")
    parts = [p_in, p_mkv, p_sup, p_gup, p_xup, p_out, p_m1, p_m2]

    shard_names = [(w_in, m_w_in, v_w_in), (w_mem_kv, m_w_mem_kv, v_w_mem_kv), (w_swa_up, m_w_swa_up, v_w_swa_up),
                   (w_gdn_up, m_w_gdn_up, v_w_gdn_up), (w_xa_up, m_w_xa_up, v_w_xa_up), (w_out, m_w_out, v_w_out),
                   (w_mlp_in, m_w_mlp_in, v_w_mlp_in), (w_mlp_out, m_w_mlp_out, v_w_mlp_out)]
    big_res = [adamw(p, w[0], m[0], v[0], "adamw_%d" % i) for i, (p, (w, m, v)) in enumerate(zip(parts, shard_names))]

    smalls = [(g_mix, m_g_mix, v_g_mix, dg_mix), (sinks, m_sinks, v_sinks, dsinks[:, :SWA_HQ]),
              (a_log, m_a_log, v_a_log, dgp[0:1, :GDN_H]), (dt_bias, m_dt_bias, v_dt_bias, dgp[1:2, :GDN_H]),
              (gdn_norm_w, m_gdn_norm_w, v_gdn_norm_w, dnorm_w), (g_mem, m_g_mem, v_g_mem, dg_mem),
              (g_mlp, m_g_mlp, v_g_mlp, dg_mlp), (g_final, m_g_final, v_g_final, dg_final)]
    sizes = [-(-t[0].size // LANE) * LANE for t in smalls] + [GDN_CONV * 3 * GDN_W]
    rows = -(-sum(sizes) // (8 * LANE)) * 8
    csh = conv_w.shape[2]

    def conv_place(a):
        full = jnp.tile(a[0], (1, N_DEV))
        owner = lax.broadcasted_iota(jnp.int32, full.shape, 1) // csh
        return jnp.where(owner == dev, full, 0.0)

    g_pack = _pack_small([t[3] for t in smalls] + [dconv], rows)
    w_pack = _pack_small([t[0] for t in smalls] + [conv_place(conv_w)], rows)
    m_pack = _pack_small([t[1] for t in smalls] + [conv_place(m_conv_w)], rows)
    v_pack = _pack_small([t[2] for t in smalls] + [conv_place(v_conv_w)], rows)
    g_all = run_job(GatherJob([g_pack]), "gather_small_grads")[0]
    small_res = adamw(g_all, w_pack, m_pack, v_pack, "adamw_small")

    def unpack(arr):
        flat = arr.reshape(-1)
        outs, off = [], 0
        for t, sz in zip(smalls, sizes[:-1]):
            outs.append(flat[off:off + t[0].size].reshape(t[0].shape))
            off += sz
        cw = flat[off:off + sizes[-1]].reshape(GDN_CONV, 3 * GDN_W)
        mine = (lax.broadcasted_iota(jnp.int32, (1, N_DEV, 1), 1) == dev).astype(F32)
        outs.append(jnp.sum(cw.reshape(GDN_CONV, N_DEV, csh) * mine, axis=1)[None])
        return outs

    sg, sd, sm, sv = (unpack(a) for a in small_res)
    bg, bd, bm, bv = ([r[i][None] for r in big_res] for i in range(4))

    def ordered(sm_, bg_):
        return [sm_[0], bg_[0], sm_[1], sm_[8], sm_[2], sm_[3], sm_[4], sm_[5], bg_[1], bg_[2], bg_[3], bg_[4], bg_[5],
                sm_[6], bg_[6], bg_[7], sm_[7]]

    return (loss, grad_x[None], *ordered(sg, bg), *ordered(sd, bd), *ordered(sm, bm), *ordered(sv, bv))
```

```python
import functools

import jax
import jax.numpy as jnp
from jax import lax
from jax.experimental import pallas as pl
from jax.experimental.pallas import tpu as pltpu

F32, BF16 = jnp.float32, jnp.bfloat16
SDS = jax.ShapeDtypeStruct
MESH = pl.DeviceIdType.MESH
ANY = pl.BlockSpec(memory_space=pl.ANY)

SWA_HQ, SWA_HKV, SWA_HD, SWA_W = 16, 2, 64, 128
SWA_G = SWA_HQ // SWA_HKV
GDN_H, GDN_D, GDN_CONV, GDN_C = 4, 128, 4, 64
XA_H, XA_D = 4, 128
Q_W = SWA_HQ * SWA_HD
KV_W = SWA_HKV * SWA_HD
GDN_W = GDN_H * GDN_D
XA_W = XA_H * XA_D
RMS_EPS = 1e-6
L2_EPS = 1e-6
NEG = -1e30
N_DEV = 8
LANE = 128

ADAM_LR, ADAM_B1, ADAM_B2, ADAM_EPS, ADAM_WD, ADAM_STEP = 0.001, 0.9, 0.999, 1e-08, 0.01, 10

VMEM_BIG = 56 * 1024 * 1024


def _cp(sem, vmem=VMEM_BIG):
    return pltpu.CompilerParams(dimension_semantics=sem, vmem_limit_bytes=vmem)


def _div(a, b):
    assert a % b == 0, (a, b)
    return a // b


def _tile(n, t):
    t = min(t, n)
    assert n % t == 0, (n, t)
    return t


def _sigmoid(x):
    return 1.0 / (1.0 + jnp.exp(-x))


def _silu(x):
    return x * _sigmoid(x)


def _softplus(x):
    return jnp.maximum(x, 0.0) + jnp.log1p(jnp.exp(-jnp.abs(x)))


def _dot(a, b, dims, prec=None):
    return lax.dot_general(a, b, (dims, ((), ())), precision=prec, preferred_element_type=F32)


NN = ((1,), (0,))
NT = ((1,), (1,))
TN = ((0,), (0,))
HI = lax.Precision.HIGHEST


def _bdot_plain(a, b, dims):
    return _dot(a.astype(BF16), b.astype(BF16), dims)


@functools.partial(jax.custom_vjp, nondiff_argnums=(2,))
def _bdot_vjp(a, b, dims):
    return _bdot_plain(a, b, dims)


def _bdot_vjp_fwd(a, b, dims):
    return _bdot_plain(a, b, dims), (a, b)


def _bdot_vjp_bwd(dims, res, ct):
    a, b = res
    if dims == NN:
        return _bdot_plain(ct, b, NT), _bdot_plain(a, ct, TN)
    assert dims == NT, dims
    return _bdot_plain(ct, b, NN), _bdot_plain(ct, a, TN)


_bdot_vjp.defvjp(_bdot_vjp_fwd, _bdot_vjp_bwd)


def _neumann(xs):
    pws, nns = list(xs), list(xs)
    for _ in range(5):
        pws = [_bdot_plain(p, p, NN) for p in pws]
        nns = [n + p + _bdot_plain(n, p, NN) for n, p in zip(nns, pws)]
    return tuple(nns)


@jax.custom_vjp
def _neumann_vjp(xs):
    return _neumann(xs)


def _neumann_vjp_fwd(xs):
    nns = _neumann(xs)
    return nns, nns


def _neumann_vjp_bwd(nns, cts):
    ts = [ct + _bdot_plain(nn, ct, TN) for nn, ct in zip(nns, cts)]
    return (tuple(t + _bdot_plain(t, nn, NT) for t, nn in zip(ts, nns)),)


_neumann_vjp.defvjp(_neumann_vjp_fwd, _neumann_vjp_bwd)


class Layout:
    def __init__(self, d):
        self.d = d
        self.g = 0
        self.q = 3 * d
        self.qkv = self.q + Q_W
        self.z = self.qkv + 3 * GDN_W
        self.qc = self.z + GDN_W
        self.k = self.qc + XA_W
        self.v = self.k + KV_W
        self.ab = self.v + KV_W
        self.end = self.ab + LANE
        self.pw = -(-self.end // 1024) * 1024
        self.lq, self.lk, self.lv, self.lqkv = 0, Q_W, Q_W + KV_W, Q_W + 2 * KV_W
        self.la = self.lqkv + 3 * GDN_W
        self.lz = self.la + 2 * GDN_H
        self.lqc = self.lz + GDN_W
        self.lg = self.lqc + XA_W
        self.lw = self.lg + 3 * d

    def pieces(self):
        segs = [(self.lq, self.lk, self.q), (self.lk, self.lv, self.k), (self.lv, self.lqkv, self.v),
                (self.lqkv, self.la, self.qkv), (self.la, self.lz, self.ab), (self.lz, self.lqc, self.z),
                (self.lqc, self.lg, self.qc), (self.lg, self.lw, self.g)]
        cw = _div(self.lw, N_DEV)
        out = []
        for dev in range(N_DEV):
            lo, hi = dev * cw, (dev + 1) * cw
            for ls, le, ps in segs:
                s, e = max(lo, ls), min(hi, le)
                if s < e:
                    out.append((dev, s - lo, ps + s - ls, e - s))
        return out


def pad_w_in(g, lay):
    nd, k, cw = g.shape
    tr = _tile(k, 256)
    tail = lay.ab + 2 * GDN_H

    def body(g_ref, o_ref):
        o_ref[:, pl.ds(tail, lay.pw - tail)] = jnp.zeros((tr, lay.pw - tail), o_ref.dtype)
        for dev, so, po, ln in lay.pieces():
            o_ref[:, pl.ds(po, ln)] = g_ref[dev, :, pl.ds(so, ln)]

    return pl.pallas_call(
        body, name="pad_w_in", grid=(k // tr,), in_specs=[pl.BlockSpec((nd, tr, cw), lambda i: (0, i, 0))],
        out_specs=pl.BlockSpec((tr, lay.pw), lambda i: (i, 0)), out_shape=SDS((k, lay.pw), g.dtype),
        compiler_params=_cp(("parallel",)))(g)


def unpad_dw_in(dw, lay):
    k = dw.shape[0]
    cw = _div(lay.lw, N_DEV)
    tr = _tile(k, 256)

    def body(d_ref, o_ref):
        for dev, so, po, ln in lay.pieces():
            o_ref[dev, :, pl.ds(so, ln)] = d_ref[:, pl.ds(po, ln)]

    return pl.pallas_call(
        body, name="unpad_dw_in", grid=(k // tr,), in_specs=[pl.BlockSpec((tr, lay.pw), lambda i: (i, 0))],
        out_specs=pl.BlockSpec((N_DEV, tr, cw), lambda i: (0, i, 0)), out_shape=SDS((N_DEV, k, cw), dw.dtype),
        compiler_params=_cp(("parallel",)))(dw)


def _position():
    return lax.axis_index("x"), lax.axis_index("y"), lax.axis_index("c")


class GatherJob:
    def __init__(self, arrs):
        self.ins = list(arrs)
        n = len(arrs)
        self.out_shapes = [SDS((N_DEV,) + a.shape, a.dtype) for a in arrs]
        self.scratch = [pltpu.SemaphoreType.DMA((n, 7)), pltpu.SemaphoreType.DMA((n, 7)), pltpu.SemaphoreType.DMA((n,))]

    def _ctx(self, outs, sems):
        send_sems, recv_sems, _ = sems
        x, y, c = _position()

        def blk(o, p):
            return o.at[4 * p[0] + 2 * p[1] + p[2]]

        def copy(i, k, block, to, src=None):
            return pltpu.make_async_remote_copy(
                src_ref=blk(outs[i], block) if src is None else src, dst_ref=blk(outs[i], block),
                send_sem=send_sems.at[i, k], recv_sem=recv_sems.at[i, k], device_id=to, device_id_type=MESH)

        return (x, y, c), (x, y, 1 - c), [(1 - x, y), (x, 1 - y), (1 - x, 1 - y)], blk, copy

    def start(self, ins, outs, sems):
        me, sibling, chips, blk, copy = self._ctx(outs, sems)
        for i in range(len(ins)):
            pltpu.make_async_copy(ins[i], blk(outs[i], me), sems[2].at[i]).start()
            copy(i, 0, me, sibling, src=ins[i]).start()
            for j, chip in enumerate(chips[:2]):
                copy(i, 1 + j, me, (*chip, me[2]), src=ins[i]).start()

    def _relay(self, i, outs, sems, onward):
        me, _, _, blk, copy = self._ctx(outs, sems)
        x, y, c = me
        origin = (x + (1 - c) - 2 * x * (1 - c), y + c - 2 * y * c, c)
        if not onward:
            return copy(i, 1 + c, origin, me)
        return copy(i, 3, origin, (x + c - 2 * x * c, y + (1 - c) - 2 * y * (1 - c), c))

    def relay(self, ins, outs, sems):
        for i in range(len(ins)):
            self._relay(i, outs, sems, False).wait_recv()
            self._relay(i, outs, sems, True).start()

    def mid(self, ins, outs, sems):
        me, sibling, chips, blk, copy = self._ctx(outs, sems)
        for i in range(len(ins)):
            for j, chip in enumerate(chips):
                arrival = copy(i, 1 + j, (*chip, me[2]), me)
                if j < 2:
                    pl.when(me[2] != j)(arrival.wait_recv)
                else:
                    arrival.wait_recv()
                copy(i, 4 + j, (*chip, me[2]), sibling).start()

    def finish(self, ins, outs, sems):
        me, sibling, chips, blk, copy = self._ctx(outs, sems)
        for i in range(len(ins)):
            copy(i, 0, sibling, me).wait_recv()
            for j, chip in enumerate(chips):
                copy(i, 4 + j, (*chip, 1 - me[2]), me).wait_recv()
        for i in range(len(ins)):
            pltpu.make_async_copy(ins[i], blk(outs[i], me), sems[2].at[i]).wait()
            copy(i, 0, me, sibling, src=ins[i]).wait_send()
            for j, chip in enumerate(chips[:2]):
                copy(i, 1 + j, me, (*chip, me[2]), src=ins[i]).wait_send()
            self._relay(i, outs, sems, True).wait_send()
            for j, chip in enumerate(chips):
                copy(i, 4 + j, (*chip, me[2]), sibling).wait_send()


class ChipExchangeJob:
    mid = None

    def __init__(self, arrs):
        self.ins = list(arrs)
        n = len(arrs)
        self.out_shapes = [SDS(a.shape, a.dtype) for a in arrs]
        self.scratch = [pltpu.SemaphoreType.DMA((n, 3)), pltpu.SemaphoreType.DMA((n, 3)), pltpu.SemaphoreType.DMA((n,))]

    def _copies(self, ins, outs, sems, i, arrivals):
        send_sems, recv_sems, local_sems = sems
        x, y, c = _position()
        my_chip = 2 * x + y
        chips = [(1 - x, y), (x, 1 - y), (1 - x, 1 - y)]
        if arrivals:
            return [pltpu.make_async_remote_copy(
                src_ref=ins[i].at[my_chip], dst_ref=outs[i].at[2 * px + py], send_sem=send_sems.at[i, k],
                recv_sem=recv_sems.at[i, k], device_id=(px, py, c), device_id_type=MESH) for k, (px, py) in enumerate(chips)]
        local = pltpu.make_async_copy(ins[i].at[my_chip], outs[i].at[my_chip], local_sems.at[i])
        return local, [pltpu.make_async_remote_copy(
            src_ref=ins[i].at[2 * px + py], dst_ref=outs[i].at[my_chip], send_sem=send_sems.at[i, k],
            recv_sem=recv_sems.at[i, k], device_id=(px, py, c), device_id_type=MESH) for k, (px, py) in enumerate(chips)]

    def start(self, ins, outs, sems):
        for i in range(len(ins)):
            local, remote = self._copies(ins, outs, sems, i, False)
            local.start()
            for cp in remote:
                cp.start()

    def finish(self, ins, outs, sems):
        for i in range(len(ins)):
            for cp in self._copies(ins, outs, sems, i, True):
                cp.wait_recv()
            local, remote = self._copies(ins, outs, sems, i, False)
            for cp in remote:
                cp.wait_send()
            local.wait()


def _slab_shape(g, cols):
    return (g.shape[0], _div(g.shape[1], N_DEV)) if cols else g.shape[1:]


class PairExchangeJob:
    mid = None

    def __init__(self, grads, cols):
        self.ins, self.cols = list(grads), list(cols)
        n = len(grads)
        self.out_shapes = [SDS((4,) + _slab_shape(g, cl), g.dtype) for g, cl in zip(grads, cols)]
        self.scratch = [pltpu.SemaphoreType.DMA((n, 4)), pltpu.SemaphoreType.DMA((n, 4))]

    def _copies(self, ins, outs, sems):
        send_sems, recv_sems = sems
        x, y, c = _position()

        def part(i, dst):
            if not self.cols[i]:
                return ins[i].at[dst]
            cw = _slab_shape(self.ins[i], True)[1]
            return ins[i].at[:, pl.ds(pl.multiple_of(dst * cw, LANE), cw)]

        return [pltpu.make_async_remote_copy(src_ref=part(i, 2 * j + 1 - c), dst_ref=outs[i].at[j], send_sem=send_sems.at[i, j],
                                             recv_sem=recv_sems.at[i, j], device_id=(x, y, 1 - c), device_id_type=MESH)
                for i in range(len(ins)) for j in range(4)]

    def start(self, ins, outs, sems):
        for cp in self._copies(ins, outs, sems):
            cp.start()

    def finish(self, ins, outs, sems):
        for cp in self._copies(ins, outs, sems):
            cp.wait()


def _host_begin(job, step, steps, ins, outs, sems):
    pl.when(step == 0)(lambda: job.start(ins, outs, sems))
    if job.mid is not None:
        pl.when(step == (steps * 45) // 100)(lambda: job.relay(ins, outs, sems))
        pl.when(step == (steps * 85) // 100)(lambda: job.mid(ins, outs, sems))


def run_job(job, name):
    n = len(job.ins)

    def body(*refs):
        ins, outs, sems = refs[:n], refs[n:2 * n], refs[2 * n:]
        job.start(ins, outs, sems)
        if job.mid is not None:
            job.relay(ins, outs, sems)
            job.mid(ins, outs, sems)
        job.finish(ins, outs, sems)

    return pl.pallas_call(body, name=name, out_shape=job.out_shapes, in_specs=[ANY] * n, out_specs=[ANY] * n,
                          scratch_shapes=job.scratch)(*job.ins)


def pair_add(grad, cols, other, name):
    r, c = _slab_shape(grad, cols)
    tr = _tile(r, 256)
    parity = lax.axis_index("c").astype(jnp.int32).reshape(1)

    def body(par_ref, a_ref, b_ref, o_ref):
        o_ref[...] = (a_ref[...].astype(F32) + b_ref[...].astype(F32)).astype(BF16)

    spec = pl.BlockSpec((None, tr, c), lambda j, i, par: (j, i, 0))
    if cols:
        own = pl.BlockSpec((tr, c), lambda j, i, par: (i, 2 * j + par[0]))
    else:
        own = pl.BlockSpec((None, tr, c), lambda j, i, par: (2 * j + par[0], i, 0))
    return pl.pallas_call(
        body, name=name, out_shape=SDS(other.shape, BF16),
        grid_spec=pltpu.PrefetchScalarGridSpec(num_scalar_prefetch=1, grid=(4, r // tr), in_specs=[own, spec], out_specs=spec),
        compiler_params=_cp(("parallel", "parallel")))(parity, grad, other)


def adamw(parts, w, m, v, name):
    p, r, c = parts.shape
    tr = _tile(r, 128 if c > 1024 else 256)

    def body(p_ref, w_ref, m_ref, v_ref, g_out, d_out, m_out, v_out):
        g = p_ref[0].astype(F32)
        for j in range(1, p):
            g = g + p_ref[j].astype(F32)
        mn = ADAM_B1 * m_ref[...] + (1.0 - ADAM_B1) * g
        vn = ADAM_B2 * v_ref[...] + (1.0 - ADAM_B2) * jnp.square(g)
        m_hat = mn / (1.0 - ADAM_B1 ** ADAM_STEP)
        v_hat = vn / (1.0 - ADAM_B2 ** ADAM_STEP)
        g_out[...] = g
        d_out[...] = -ADAM_LR * (m_hat / (jnp.sqrt(v_hat) + ADAM_EPS) + ADAM_WD * w_ref[...])
        m_out[...] = mn
        v_out[...] = vn

    spec = pl.BlockSpec((tr, c), lambda i: (i, 0))
    return pl.pallas_call(
        body, name=name, grid=(r // tr,),
        in_specs=[pl.BlockSpec((p, tr, c), lambda i: (0, i, 0)), spec, spec, spec],
        out_specs=[spec] * 4, out_shape=[SDS((r, c), F32)] * 4, compiler_params=_cp(("parallel",)))(parts, w, m, v)


def matmul(a, b, *, mode, out_dtype, name, tm=1024, tn=1024, tk=512, a_relu2=False, resid=None, relu2_grad_of=None,
           b_cols=False, relu2_out=False, rms_gain=None, loss_target=None, side=None, tail=None):
    if b_cols:
        nb, brows, bc = b.shape
        bshape = (brows, nb * bc)
    else:
        bshape = b.shape
    head_k = head_n = None
    if mode == "nn":
        (m, k), (k2, n) = a.shape, bshape
    elif mode == "nt":
        (m, k), (n, k2) = a.shape, bshape
        if tail is not None:
            head_k, k = k, k + tail.shape[1]
    else:
        (k, m), (k2, n) = a.shape, bshape
        if tail is not None:
            head_n, n = n, n + tail.shape[1]
    assert k == k2 and (tail is None or mode != "nn"), (a.shape, b.shape, mode)
    b_whole = b_cols and mode == "nt" and tk >= k
    tm, tn, tk = _tile(m, tm), _tile(n, tn), _tile(k, tk)
    if b_cols and mode == "nn":
        tn = _tile(bc, tn)
    if b_cols and mode == "nt" and not b_whole:
        tk = _tile(bc, tk)
    nk = k // tk
    ni, nj = m // tm, n // tn
    nk_head = _div(head_k, tk) if head_k is not None else None
    nj_head = _div(head_n, tn) if head_n is not None else None
    use_acc = nk > 1 or tail is not None
    dims = {"nn": NN, "nt": NT, "tn": TN}[mode]
    extras = [e for e in (resid, relu2_grad_of) if e is not None]
    tails = [tail] if tail is not None else []
    n_side = len(side.ins) if side is not None else 0
    loss = loss_target is not None
    if loss:
        extras.append(loss_target)
    gains = [rms_gain] if rms_gain is not None else []
    assert not gains or (tn == n and not relu2_out), (tn, n)
    assert not loss or (gains and resid is not None and relu2_grad_of is None and out_dtype == F32)
    n_main = 2 if (relu2_out or gains) else 1
    n_loss = 2 if loss else 0

    def body(*refs):
        a_ref, b_ref = refs[:2]
        t_ref = refs[2] if tails else None
        n_op = 2 + len(tails)
        e_refs = refs[n_op:n_op + len(extras)]
        n_pre = n_op + len(extras) + len(gains)
        g_ref = refs[n_pre - 1] if gains else None
        n_in = n_pre + n_side
        o_ref = refs[n_in]
        act_ref = refs[n_in + 1] if n_main == 2 else None
        dg_ref, l_ref = refs[n_in + n_main:n_in + n_main + n_loss] if loss else (None, None)
        acc_ref = refs[n_in + n_main + n_loss + n_side] if use_acc else None
        if side is not None:
            s_ins = refs[n_pre:n_in]
            s_outs = refs[n_in + n_main + n_loss:n_in + n_main + n_loss + n_side]
            s_sems = refs[len(refs) - len(side.scratch):]
            step = (pl.program_id(0) * nj + pl.program_id(1)) * nk + pl.program_id(2)
            _host_begin(side, step, ni * nj * nk, s_ins, s_outs, s_sems)

        def operands(a_from=a_ref, b_from=b_ref):
            av = a_from[...]
            if a_relu2:
                av = jnp.square(jnp.maximum(av.astype(F32), 0.0))
            return av.astype(BF16), b_from[...].astype(BF16)

        def finish(r):
            e = list(e_refs)
            if resid is not None:
                r = r + e.pop(0)[...]
            if relu2_grad_of is not None:
                r = r * (2.0 * jnp.maximum(e.pop(0)[...], 0.0))
            if loss:
                gv = g_ref[...]
                inv = lax.rsqrt(jnp.mean(r * r, axis=-1, keepdims=True) + RMS_EPS)
                err = r * inv * gv - e.pop(0)[...]
                lpart = 0.5 * jnp.sum(jnp.mean(err * err, axis=-1, keepdims=True), axis=0, keepdims=True)
                dx, part = _rms_bwd_rows(r, gv, err * (1.0 / n))
                o_ref[...] = dx
                act_ref[...] = dx.astype(BF16)

                @pl.when(pl.program_id(0) == 0)
                def _():
                    dg_ref[...] = jnp.zeros_like(dg_ref)
                    l_ref[...] = jnp.zeros_like(l_ref)

                dg_ref[...] += part
                l_ref[...] += jnp.broadcast_to(lpart, l_ref.shape)
                return
            o_ref[...] = r.astype(out_dtype)
            if relu2_out:
                act_ref[...] = jnp.square(jnp.maximum(r, 0.0)).astype(BF16)
            if gains:
                inv = lax.rsqrt(jnp.mean(r * r, axis=-1, keepdims=True) + RMS_EPS)
                act_ref[...] = (r * inv * g_ref[...]).astype(BF16)

        if b_whole:
            av = a_ref[...].astype(BF16)
            finish(sum(_dot(av[:, kb * bc:(kb + 1) * bc], b_ref[kb].astype(BF16), NT) for kb in range(nb)))
        elif not use_acc:
            av, bv = operands()
            finish(_dot(av, bv, dims))
        else:
            kk = pl.program_id(2)

            def accumulate(a_from, b_from):
                def product():
                    av, bv = operands(a_from, b_from)
                    return _dot(av, bv, dims)

                if nk == 1:
                    finish(product())
                    return

                @pl.when(kk == 0)
                def _():
                    acc_ref[...] = product()

                @pl.when((kk > 0) & (kk < nk - 1))
                def _():
                    acc_ref[...] += product()

                @pl.when(kk == nk - 1)
                def _():
                    finish(acc_ref[...] + product())

            if not tails:
                accumulate(a_ref, b_ref)
            elif mode == "nt":
                pl.when(kk < nk_head)(lambda: accumulate(a_ref, b_ref))
                pl.when(kk >= nk_head)(lambda: accumulate(t_ref, b_ref))
            else:
                in_head = pl.program_id(1) < nj_head
                pl.when(in_head)(lambda: accumulate(a_ref, b_ref))
                pl.when(jnp.logical_not(in_head))(lambda: accumulate(a_ref, t_ref))

        if side is not None:
            pl.when(step == ni * nj * nk - 1)(lambda: side.finish(s_ins, s_outs, s_sems))

    a_spec = {"nn": pl.BlockSpec((tm, tk), lambda i, j, kk: (i, kk)),
              "nt": pl.BlockSpec((tm, tk), lambda i, j, kk: (i, kk)),
              "tn": pl.BlockSpec((tk, tm), lambda i, j, kk: (kk, i))}[mode]
    t_specs = []
    if tails and mode == "nt":
        a_spec = pl.BlockSpec((tm, tk), lambda i, j, kk: (i, jnp.minimum(kk, nk_head - 1)))
        t_specs = [pl.BlockSpec((tm, tk), lambda i, j, kk: (i, jnp.maximum(kk - nk_head, 0)))]
    if tails and mode == "tn":
        t_specs = [pl.BlockSpec((tk, tn), lambda i, j, kk: (kk, jnp.maximum(j - nj_head, 0)))]
    if tails and mode == "tn":
        b_spec = pl.BlockSpec((tk, tn), lambda i, j, kk: (kk, jnp.minimum(j, nj_head - 1)))
    elif not b_cols:
        b_spec = {"nn": pl.BlockSpec((tk, tn), lambda i, j, kk: (kk, j)),
                  "nt": pl.BlockSpec((tn, tk), lambda i, j, kk: (j, kk)),
                  "tn": pl.BlockSpec((tk, tn), lambda i, j, kk: (kk, j))}[mode]
    elif mode == "nn":
        per = bc // tn
        b_spec = pl.BlockSpec((None, tk, tn), lambda i, j, kk: (j // per, kk, j % per))
    elif b_whole:
        b_spec = pl.BlockSpec((nb, tn, bc), lambda i, j, kk: (0, j, 0))
    else:
        assert mode == "nt", mode
        per = bc // tk
        b_spec = pl.BlockSpec((None, tn, tk), lambda i, j, kk: (kk // per, j, kk % per))
    e_spec = pl.BlockSpec((tm, tn), lambda i, j, kk: (i, j))
    main_shapes = [SDS((m, n), out_dtype)] + ([SDS((m, n), BF16)] if n_main == 2 else [])
    g_specs = [pl.BlockSpec((1, tn), lambda i, j, kk: (0, j))] * len(gains)
    l_specs = [pl.BlockSpec((1, tn), lambda i, j, kk: (0, j)), pl.BlockSpec((1, LANE), lambda i, j, kk: (0, 0))] if loss else []
    l_shapes = [SDS((1, n), F32), SDS((1, LANE), F32)] if loss else []
    res = pl.pallas_call(
        body, name=name, grid=(ni, nj, nk),
        in_specs=[a_spec, b_spec] + t_specs + [e_spec] * len(extras) + g_specs + [ANY] * n_side,
        out_specs=[e_spec] * n_main + l_specs + [ANY] * n_side,
        out_shape=main_shapes + l_shapes + (side.out_shapes if side is not None else []),
        scratch_shapes=([pltpu.VMEM((tm, tn), F32)] if use_acc else []) + (side.scratch if side is not None else []),
        compiler_params=_cp(("arbitrary", "arbitrary", "arbitrary")))(a, b, *tails, *extras, *gains, *(side.ins if side is not None else []))
    return res if len(res) > 1 else res[0]


def rmsnorm_fwd(x, g, name, side=None):
    s, d = x.shape
    tm = _tile(s, 256)
    steps = s // tm
    n_side = len(side.ins) if side is not None else 0

    def body(*refs):
        x_ref, g_ref, o_ref = refs[0], refs[1], refs[2 + n_side]
        if side is not None:
            s_ins, s_outs, s_sems = refs[2:2 + n_side], refs[3 + n_side:3 + 2 * n_side], refs[3 + 2 * n_side:]
            _host_begin(side, pl.program_id(0), steps, s_ins, s_outs, s_sems)
        xv = x_ref[...]
        r = lax.rsqrt(jnp.mean(xv * xv, axis=-1, keepdims=True) + RMS_EPS)
        o_ref[...] = (xv * r * g_ref[...]).astype(BF16)
        if side is not None:
            pl.when(pl.program_id(0) == steps - 1)(lambda: side.finish(s_ins, s_outs, s_sems))

    row = pl.BlockSpec((tm, d), lambda i: (i, 0))
    res = pl.pallas_call(
        body, name=name, grid=(steps,), in_specs=[row, pl.BlockSpec((1, d), lambda i: (0, 0))] + [ANY] * n_side,
        out_specs=[row] + [ANY] * n_side, out_shape=[SDS((s, d), BF16)] + (side.out_shapes if side is not None else []),
        scratch_shapes=side.scratch if side is not None else [],
        compiler_params=_cp(("arbitrary",)))(x, g, *(side.ins if side is not None else []))
    return res if side is not None else res[0]


def _rms_bwd_rows(xv, gv, dy):
    r = lax.rsqrt(jnp.mean(xv * xv, axis=-1, keepdims=True) + RMS_EPS)
    xh = xv * r
    dxh = dy * gv
    dx = r * (dxh - xh * jnp.mean(dxh * xh, axis=-1, keepdims=True))
    return dx, jnp.sum(dy * xh, axis=0, keepdims=True)


def rmsnorm_bwd(x, g, dn, resid, name, bf16_copy=False):
    s, d = x.shape
    tm = _tile(s, 256)
    has_r = resid is not None

    def body(*refs):
        x_ref, g_ref, dn_ref = refs[:3]
        dx_ref, dg_ref = refs[3 + has_r:5 + has_r]
        dx, part = _rms_bwd_rows(x_ref[...], g_ref[...], dn_ref[...].astype(F32))
        if has_r:
            dx = dx + refs[3][...]
        dx_ref[...] = dx
        if bf16_copy:
            refs[5 + has_r][...] = dx.astype(BF16)

        @pl.when(pl.program_id(0) == 0)
        def _():
            dg_ref[...] = jnp.zeros_like(dg_ref)

        dg_ref[...] += part

    row = pl.BlockSpec((tm, d), lambda i: (i, 0))
    vec = pl.BlockSpec((1, d), lambda i: (0, 0))
    ins = [x, g, dn] + ([resid] if has_r else [])
    return pl.pallas_call(body, name=name, grid=(s // tm,), in_specs=[row, vec, row] + ([row] if has_r else []),
                          out_specs=[row, vec] + ([row] if bf16_copy else []),
                          out_shape=[SDS((s, d), F32), SDS((1, d), F32)] + ([SDS((s, d), BF16)] if bf16_copy else []),
                          compiler_params=_cp(("arbitrary",)))(*ins)


def merge(p_all, ya, yb, yc, wa, wb, wc, dm, lay, name):
    s, d = ya.shape[0], lay.d
    wcols = wa.shape[2]
    bwd = dm is not None
    tm, tn = _tile(s, 2048), _tile(wcols, 512)
    nj, per = d // tn, wcols // tn

    y_specs = [pl.BlockSpec((tm, y.shape[1]), lambda i, j, *_: (i, 0)) for y in (ya, yb, yc)]
    w_specs = [pl.BlockSpec((None, w.shape[1], tn), lambda i, j, *_: (j // per, 0, j % per)) for w in (wa, wb, wc)]
    o_spec = pl.BlockSpec((tm, tn), lambda i, j, *_: (i, j))
    if not bwd:
        def body(ga, gb, gc, ya_r, yb_r, yc_r, wa_r, wb_r, wc_r, o_ref):
            ts = [_dot(y[...], w[...], NN) for y, w in ((ya_r, wa_r), (yb_r, wb_r), (yc_r, wc_r))]
            gs = [_sigmoid(g[...]) for g in (ga, gb, gc)]
            o_ref[...] = (gs[0] * ts[0] + gs[1] * ts[1] + gs[2] * ts[2]).astype(BF16)

        gate_specs = [pl.BlockSpec((tm, tn), lambda i, j, b=b: (i, b * nj + j)) for b in range(3)]
        return pl.pallas_call(
            body, name=name, grid=(s // tm, nj), in_specs=gate_specs + y_specs + w_specs,
            out_specs=o_spec, out_shape=SDS((s, d), BF16),
            compiler_params=_cp(("parallel", "parallel")))(p_all, p_all, p_all, ya, yb, yc, wa, wb, wc)

    def body_bwd(g_r, ya_r, yb_r, yc_r, wa_r, wb_r, wc_r, dm_r, dg_o, dta_o, dtb_o, dtc_o):
        for k, (y, w, dt_o) in enumerate(((ya_r, wa_r, dta_o), (yb_r, wb_r, dtb_o), (yc_r, wc_r, dtc_o))):
            @pl.when(pl.program_id(2) == k)
            def _(y=y, w=w, dt_o=dt_o):
                t = _dot(y[...], w[...], NN)
                g = _sigmoid(g_r[...])
                dmv = dm_r[...]
                dg_o[...] = (dmv * t * (g * (1.0 - g))).astype(BF16)
                dt_o[...] = (dmv * g).astype(BF16)

    gate_spec = pl.BlockSpec((tm, tn), lambda i, j, b: (i, b * nj + j))
    return pl.pallas_call(
        body_bwd, name=name, grid=(s // tm, nj, 3), in_specs=[gate_spec] + y_specs + w_specs + [o_spec],
        out_specs=[gate_spec, o_spec, o_spec, o_spec], out_shape=[SDS((s, 3 * d), BF16)] + [SDS((s, d), BF16)] * 3,
        compiler_params=_cp(("arbitrary", "arbitrary", "arbitrary")))(p_all, ya, yb, yc, wa, wb, wc, dm)


SWA_PAIRS = SWA_G // 2


def _swa_probs(qs, kcs, sinks, first):
    shape = (qs[0].shape[0], 2 * SWA_W)
    qi = lax.broadcasted_iota(jnp.int32, shape, 0) % SWA_W
    kj = lax.broadcasted_iota(jnp.int32, shape, 1)
    mask = (kj > qi) & (kj <= qi + SWA_W) & ((kj >= SWA_W) | jnp.logical_not(first))
    ss = [jnp.where(mask, _dot(q, kc, NT) * (SWA_HD ** -0.5), NEG) for q, kc in zip(qs, kcs)]
    ms = [jnp.maximum(jnp.max(s, axis=-1, keepdims=True), sink) for s, sink in zip(ss, sinks)]
    ps = [jnp.exp(s - m) for s, m in zip(ss, ms)]
    es = [jnp.exp(sink - m) for sink, m in zip(sinks, ms)]
    inv = [1.0 / (jnp.sum(p, axis=-1, keepdims=True) + e) for p, e in zip(ps, es)]
    return [p * i for p, i in zip(ps, inv)], [e * i for e, i in zip(es, inv)]


def _swa_stack(ref, h):
    return jnp.concatenate([ref[:, pl.ds((h * SWA_PAIRS + p) * LANE, LANE)] for p in range(SWA_PAIRS)], axis=0)


def _swa_unstack(ref, h, val):
    for p in range(SWA_PAIRS):
        ref[:, pl.ds((h * SWA_PAIRS + p) * LANE, LANE)] = val[p * SWA_W:(p + 1) * SWA_W]


def _swa_sink_col(sk_ref, h, second):
    pair = lax.broadcasted_iota(jnp.int32, (SWA_PAIRS * SWA_W, 1), 0) // SWA_W
    col = jnp.zeros((SWA_PAIRS * SWA_W, 1), F32)
    for p in range(SWA_PAIRS):
        hh = h * SWA_G + 2 * p + second
        col = jnp.where(pair == p, sk_ref[0:1, hh:hh + 1], col)
    return col


def _swa_kv_tiles(cur_ref, prev_ref, h):
    t = jnp.concatenate([prev_ref[...], cur_ref[...]], axis=0)
    lane = lax.broadcasted_iota(jnp.int32, t.shape, 1)
    moved = pltpu.roll(t, SWA_HD, axis=1)
    low, high = (t, moved) if h == 0 else (moved, t)
    return jnp.where(lane < SWA_HD, low, 0.0).astype(BF16), jnp.where(lane >= SWA_HD, high, 0.0).astype(BF16)


def _swa_kv_grad(g_low, g_high, h):
    lane = lax.broadcasted_iota(jnp.int32, g_low.shape, 1)
    if h == 0:
        return jnp.where(lane < SWA_HD, g_low + pltpu.roll(g_high, SWA_HD, axis=1), 0.0)
    return jnp.where(lane >= SWA_HD, pltpu.roll(g_low, SWA_HD, axis=1) + g_high, 0.0)


def _swa_specs(lay):
    w = SWA_W
    q_spec = pl.BlockSpec((w, Q_W), lambda n: (n, _div(lay.q, Q_W)))
    cur = lambda off: pl.BlockSpec((w, KV_W), lambda n: (n, _div(off, KV_W)))
    prev = lambda off: pl.BlockSpec((w, KV_W), lambda n: (jnp.maximum(n - 1, 0), _div(off, KV_W)))
    return q_spec, cur(lay.k), prev(lay.k), cur(lay.v), prev(lay.v)


def swa_fwd(p_all, sinks, lay, name):
    s = p_all.shape[0]
    nb = _div(s, SWA_W)

    def body(q_ref, kc_ref, kp_ref, vc_ref, vp_ref, sk_ref, o_ref):
        first = pl.program_id(0) == 0
        units = [(h, e) for h in range(SWA_HKV) for e in range(2)]
        ks = [_swa_kv_tiles(kc_ref, kp_ref, h) for h in range(SWA_HKV)]
        vs = [_swa_kv_tiles(vc_ref, vp_ref, h) for h in range(SWA_HKV)]
        qs = [_swa_stack(q_ref, h).astype(BF16) for h in range(SWA_HKV)]
        ps, _ = _swa_probs([qs[h] for h, e in units], [ks[h][e] for h, e in units],
                           [_swa_sink_col(sk_ref, h, e) for h, e in units], first)
        os = [_dot(p.astype(BF16), vs[h][e], NN) for p, (h, e) in zip(ps, units)]
        for h in range(SWA_HKV):
            _swa_unstack(o_ref, h, (os[2 * h] + os[2 * h + 1]).astype(BF16))

    q_spec, kc_s, kp_s, vc_s, vp_s = _swa_specs(lay)
    return pl.pallas_call(
        body, name=name, grid=(nb,),
        in_specs=[q_spec, kc_s, kp_s, vc_s, vp_s, pl.BlockSpec(sinks.shape, lambda n: (0, 0))],
        out_specs=pl.BlockSpec((SWA_W, Q_W), lambda n: (n, 0)), out_shape=SDS((s, Q_W), BF16),
        compiler_params=_cp(("parallel",)))(p_all, p_all, p_all, p_all, p_all, sinks)


def swa_bwd(p_all, sinks, dy, lay, name):
    s = p_all.shape[0]
    nb = _div(s, SWA_W)
    w = SWA_W

    def body(q_ref, kc_ref, kp_ref, vc_ref, vp_ref, sk_ref, do_ref, dq_ref, dk_ref, dv_ref, ds_ref, kcar, vcar):
        n = pl.program_id(0)
        first = n == 0

        @pl.when(first)
        def _():
            kcar[...] = jnp.zeros_like(kcar)
            vcar[...] = jnp.zeros_like(vcar)
            ds_ref[...] = jnp.zeros_like(ds_ref)

        @pl.when(n < nb)
        def _():
            lane = lax.broadcasted_iota(jnp.int32, (1, LANE), 1)
            dsink = jnp.zeros((1, LANE), F32)
            units = [(h, e) for h in range(SWA_HKV) for e in range(2)]
            ks = [_swa_kv_tiles(kc_ref, kp_ref, h) for h in range(SWA_HKV)]
            vs = [_swa_kv_tiles(vc_ref, vp_ref, h) for h in range(SWA_HKV)]
            qs = [_swa_stack(q_ref, h).astype(BF16) for h in range(SWA_HKV)]
            dos = [_swa_stack(do_ref, h).astype(BF16) for h in range(SWA_HKV)]
            ps, psinks = _swa_probs([qs[h] for h, e in units], [ks[h][e] for h, e in units],
                                    [_swa_sink_col(sk_ref, h, e) for h, e in units], first)
            dps = [_dot(dos[h], vs[h][e], NT) for h, e in units]
            dvs = [_dot(p.astype(BF16), dos[h], TN) for p, (h, e) in zip(ps, units)]
            rss = [jnp.sum(dp * p, axis=-1, keepdims=True) for dp, p in zip(dps, ps)]
            dsb = [(p * (dp - rs) * (SWA_HD ** -0.5)).astype(BF16) for p, dp, rs in zip(ps, dps, rss)]
            dqs = [_dot(d, ks[h][e], NN) for d, (h, e) in zip(dsb, units)]
            dks = [_dot(d, qs[h], TN) for d, (h, e) in zip(dsb, units)]
            for u, (h, e) in enumerate(units):
                psr = psinks[u] * rss[u]
                for pr in range(SWA_PAIRS):
                    hh = h * SWA_G + 2 * pr + e
                    dsink = dsink + jnp.where(lane == hh, -jnp.sum(psr[pr * w:(pr + 1) * w], axis=0, keepdims=True), 0.0)
            dk_tile = jnp.zeros((2 * w, KV_W), F32)
            dv_tile = jnp.zeros((2 * w, KV_W), F32)
            for h in range(SWA_HKV):
                _swa_unstack(dq_ref, h, (dqs[2 * h] + dqs[2 * h + 1]).astype(BF16))
                dk_tile = dk_tile + _swa_kv_grad(dks[2 * h], dks[2 * h + 1], h)
                dv_tile = dv_tile + _swa_kv_grad(dvs[2 * h], dvs[2 * h + 1], h)
            dk_ref[...] = (kcar[...] + dk_tile[:w]).astype(BF16)
            dv_ref[...] = (vcar[...] + dv_tile[:w]).astype(BF16)
            kcar[...] = dk_tile[w:]
            vcar[...] = dv_tile[w:]
            ds_ref[...] += dsink

        @pl.when(n == nb)
        def _():
            dk_ref[...] = kcar[...].astype(BF16)
            dv_ref[...] = vcar[...].astype(BF16)

    last = nb - 1
    q_spec = pl.BlockSpec((w, Q_W), lambda n: (jnp.minimum(n, last), _div(lay.q, Q_W)))
    cur = lambda off: pl.BlockSpec((w, KV_W), lambda n: (jnp.minimum(n, last), _div(off, KV_W)))
    prev = lambda off: pl.BlockSpec((w, KV_W), lambda n: (jnp.clip(n - 1, 0, last), _div(off, KV_W)))
    row = pl.BlockSpec((w, Q_W), lambda n: (jnp.minimum(n, last), 0))
    kv_out = pl.BlockSpec((w, KV_W), lambda n: (jnp.maximum(n - 1, 0), 0))
    return pl.pallas_call(
        body, name=name, grid=(nb + 1,),
        in_specs=[q_spec, cur(lay.k), prev(lay.k), cur(lay.v), prev(lay.v), pl.BlockSpec(sinks.shape, lambda n: (0, 0)), row],
        out_specs=[row, kv_out, kv_out, pl.BlockSpec((1, LANE), lambda n: (0, 0))],
        out_shape=[SDS((s, Q_W), BF16), SDS((s, KV_W), BF16), SDS((s, KV_W), BF16), SDS((1, LANE), F32)],
        scratch_shapes=[pltpu.VMEM((w, KV_W), F32), pltpu.VMEM((w, KV_W), F32)],
        compiler_params=_cp(("arbitrary",)))(p_all, p_all, p_all, p_all, p_all, sinks, dy)


def _xa_probs(qs, mks):
    ss = [_dot(q, mk, NT) * (XA_D ** -0.5) for q, mk in zip(qs, mks)]
    ps = [jnp.exp(s - jnp.max(s, axis=-1, keepdims=True)) for s in ss]
    inv = [1.0 / jnp.sum(p, axis=-1, keepdims=True) for p in ps]
    return [p * i for p, i in zip(ps, inv)]


def xattn_fwd(p_all, mkv, lay, name):
    s, nm = p_all.shape[0], mkv.shape[0]
    tm = _tile(s, 512)

    def body(q_ref, mkv_ref, o_ref):
        heads = range(XA_H)
        cols = [pl.ds(h * XA_D, XA_D) for h in heads]
        ps = _xa_probs([q_ref[:, c].astype(BF16) for c in cols], [mkv_ref[:, c] for c in cols])
        os = [_dot(ps[h].astype(BF16), mkv_ref[:, pl.ds(XA_W + h * XA_D, XA_D)], NN) for h in heads]
        for h in heads:
            o_ref[:, cols[h]] = os[h].astype(BF16)

    return pl.pallas_call(
        body, name=name, grid=(s // tm,),
        in_specs=[pl.BlockSpec((tm, XA_W), lambda i: (i, _div(lay.qc, XA_W))), pl.BlockSpec((nm, 2 * XA_W), lambda i: (0, 0))],
        out_specs=pl.BlockSpec((tm, XA_W), lambda i: (i, 0)), out_shape=SDS((s, XA_W), BF16),
        compiler_params=_cp(("parallel",)))(p_all, mkv)


def xattn_bwd(p_all, mkv, dy, lay, name):
    s, nm = p_all.shape[0], mkv.shape[0]
    tm = _tile(s, 512)

    def body(q_ref, mkv_ref, do_ref, dq_ref, dmkv_ref):
        @pl.when(pl.program_id(0) == 0)
        def _():
            dmkv_ref[...] = jnp.zeros_like(dmkv_ref)

        heads = range(XA_H)
        cols = [pl.ds(h * XA_D, XA_D) for h in heads]
        vcols = [pl.ds(XA_W + h * XA_D, XA_D) for h in heads]
        qs = [q_ref[:, c].astype(BF16) for c in cols]
        dos = [do_ref[:, c].astype(BF16) for c in cols]
        ps = _xa_probs(qs, [mkv_ref[:, c] for c in cols])
        dps = [_dot(dos[h], mkv_ref[:, vcols[h]], NT) for h in heads]
        dvs = [_dot(ps[h].astype(BF16), dos[h], TN) for h in heads]
        dsb = [(p * (dp - jnp.sum(dp * p, axis=-1, keepdims=True)) * (XA_D ** -0.5)).astype(BF16) for p, dp in zip(ps, dps)]
        dqs = [_dot(dsb[h], mkv_ref[:, cols[h]], NN) for h in heads]
        dks = [_dot(dsb[h], qs[h], TN) for h in heads]
        for h in heads:
            dq_ref[:, cols[h]] = dqs[h].astype(BF16)
            dmkv_ref[:, vcols[h]] += dvs[h]
            dmkv_ref[:, cols[h]] += dks[h]

    row = pl.BlockSpec((tm, XA_W), lambda i: (i, 0))
    full = pl.BlockSpec((nm, 2 * XA_W), lambda i: (0, 0))
    return pl.pallas_call(
        body, name=name, grid=(s // tm,),
        in_specs=[pl.BlockSpec((tm, XA_W), lambda i: (i, _div(lay.qc, XA_W))), full, row],
        out_specs=[row, full], out_shape=[SDS((s, XA_W), BF16), SDS((nm, 2 * XA_W), F32)],
        compiler_params=_cp(("arbitrary",)))(p_all, mkv, dy)


def _shift_down(cur, prev8, s):
    cat = jnp.concatenate([prev8, cur[0:8]], axis=0)
    return pltpu.roll(cur, s, axis=0), pltpu.roll(cat, s, axis=0)[8:16]


def _shift_up(cur, next8, s):
    tm = cur.shape[0]
    cat = jnp.concatenate([cur[tm - 8:tm], next8], axis=0)
    return pltpu.roll(cur, tm - s, axis=0), pltpu.roll(cat, 16 - s, axis=0)[0:8]


def _conv_rows(cur, prev8, w):
    main = w[GDN_CONV - 1:GDN_CONV] * cur
    top = w[GDN_CONV - 1:GDN_CONV] * cur[0:8]
    for sft in range(1, GDN_CONV):
        wi = w[GDN_CONV - 1 - sft:GDN_CONV - sft]
        a, b = _shift_down(cur, prev8, sft)
        main = main + wi * a
        top = top + wi * b
    return jnp.concatenate([top, main[8:]], axis=0)


def _conv_rows_bwd(cur, prev8, d, next8, w):
    tm = cur.shape[0]
    row = lax.broadcasted_iota(jnp.int32, (tm, 1), 0)
    main = w[GDN_CONV - 1:GDN_CONV] * d
    bot = w[GDN_CONV - 1:GDN_CONV] * d[tm - 8:tm]
    dws = [jnp.sum(d * cur, axis=0, keepdims=True)]
    for sft in range(1, GDN_CONV):
        wi = w[GDN_CONV - 1 - sft:GDN_CONV - sft]
        a, b = _shift_up(d, next8, sft)
        main = main + wi * a
        bot = bot + wi * b
        xa, xb = _shift_down(cur, prev8, sft)
        dws.append(jnp.sum(jnp.where(row >= 8, d * xa, 0.0), axis=0, keepdims=True)
                   + jnp.sum(d[0:8] * xb, axis=0, keepdims=True))
    return jnp.concatenate([main[:tm - 8], bot], axis=0), dws


def _gdn_chunk(xq, xk, xv, ab, gp, bdot=_bdot_plain):
    c = GDN_C
    nc = xq.shape[0] // c
    lane = lax.broadcasted_iota(jnp.int32, (c, LANE), 1)
    row = lax.broadcasted_iota(jnp.int32, (c, c), 0)
    col = lax.broadcasted_iota(jnp.int32, (c, c), 1)
    g_tile = -jnp.exp(gp[0:1, :]) * _softplus(ab + gp[1:2, :])
    b_tile = _sigmoid(ab)
    tri = (row >= col).astype(F32)
    qa, ka, va = _silu(xq), _silu(xk), _silu(xv)
    items = []
    for ci in range(nc):
        rs = slice(ci * c, (ci + 1) * c)
        gcum = _dot(tri, g_tile[rs], NN, HI)
        gcum_t = gcum.T
        for h in range(GDN_H):
            hs = slice(h * GDN_D, (h + 1) * GDN_D)
            q, k, v = qa[rs, hs], ka[rs, hs], va[rs, hs]
            q = q * lax.rsqrt(jnp.sum(q * q, axis=-1, keepdims=True) + L2_EPS) * (GDN_D ** -0.5)
            k = k * lax.rsqrt(jnp.sum(k * k, axis=-1, keepdims=True) + L2_EPS)
            gc = jnp.sum(jnp.where(lane == h, gcum, 0.0), axis=1, keepdims=True)
            beta = jnp.sum(jnp.where(lane == GDN_H + h, b_tile[rs], 0.0), axis=1, keepdims=True)
            decay = jnp.exp(jnp.where(row >= col, gc - gcum_t[h:h + 1, :], NEG))
            items.append((q, k, v, gc, beta, decay))
    kks = [bdot(k, k, NT) for (_, k, _, _, _, _) in items]
    xs = tuple(-jnp.where(row > col, it[4] * kk * it[5], 0.0) for it, kk in zip(items, kks))
    nns = _neumann(xs) if bdot is _bdot_plain else _neumann_vjp(xs)
    qks = [bdot(q, k, NT) for (q, k, _, _, _, _) in items]
    out = []
    for (q, k, v, gc, beta, decay), n, qk in zip(items, nns, qks):
        eg = jnp.exp(gc)
        vb = v * beta
        kbe = k * (beta * eg)
        gl = gc[c - 1:c, :]
        out.append((vb + bdot(n, vb, NN), kbe + bdot(n, kbe, NN), q * eg, k * jnp.exp(gl - gc), qk * decay, jnp.exp(gl)))
    return [out[ci * GDN_H:(ci + 1) * GDN_H] for ci in range(nc)]


GDN_CPS = 4


def _gdn_pre_specs(lay, t, tile):
    c0 = _div(lay.qkv, GDN_W)
    cur = [pl.BlockSpec((t, GDN_W), lambda n, j=j: (tile(n), c0 + j)) for j in range(3)]
    prev = [pl.BlockSpec((8, GDN_W), lambda n, j=j: (jnp.maximum(tile(n) * (t // 8) - 1, 0), c0 + j)) for j in range(3)]
    return cur + prev + [pl.BlockSpec((GDN_CONV, 3 * GDN_W), lambda n: (0, 0)),
                         pl.BlockSpec((t, LANE), lambda n: (tile(n), _div(lay.ab, LANE))),
                         pl.BlockSpec((8, LANE), lambda n: (0, 0))]


def _gdn_conv_inputs(x_refs, prev_refs, w_ref, first):
    out = []
    for j in range(3):
        prev8 = jnp.where(first, 0.0, prev_refs[j][...])
        out.append((x_refs[j][...], prev8, w_ref[:, pl.ds(j * GDN_W, GDN_W)]))
    return out


def gdn_pre_fwd(p_all, conv_w, gp, lay, name):
    s = p_all.shape[0]
    c = GDN_C
    n = _div(s, c)
    cps = _tile(n, GDN_CPS)
    t = cps * c

    def body(xq, xk, xv, pq, pk, pv, cw, ab, gp_ref, u_ref, w_ref, qd_ref, kd_ref, qk_ref, gl_ref):
        lane = lax.broadcasted_iota(jnp.int32, (1, LANE), 1)
        xs = [_conv_rows(*a) for a in _gdn_conv_inputs((xq, xk, xv), (pq, pk, pv), cw, pl.program_id(0) == 0)]
        chunks = _gdn_chunk(xs[0], xs[1], xs[2], ab[...], gp_ref[...])
        for ci, heads in enumerate(chunks):
            rs = pl.ds(ci * c, c)
            gl_row = jnp.zeros((1, LANE), F32)
            for h, (u, w, qd, kd, qk, gl) in enumerate(heads):
                hs = pl.ds(h * GDN_D, GDN_D)
                u_ref[rs, hs] = u
                w_ref[rs, hs] = w.astype(BF16)
                qd_ref[rs, hs] = qd.astype(BF16)
                kd_ref[rs, hs] = kd.astype(BF16)
                qk_ref[rs, pl.ds(h * c, c)] = qk.astype(BF16)
                gl_row = gl_row + jnp.where(lane == h, gl, 0.0)
            gl_ref[ci] = gl_row

    row = pl.BlockSpec((t, GDN_W), lambda n: (n, 0))
    return pl.pallas_call(
        body, name=name, grid=(n // cps,), in_specs=_gdn_pre_specs(lay, t, lambda n: n),
        out_specs=[row, row, row, row, pl.BlockSpec((t, GDN_H * c), lambda n: (n, 0)), pl.BlockSpec((cps, 1, LANE), lambda n: (n, 0, 0))],
        out_shape=[SDS((s, GDN_W), F32), SDS((s, GDN_W), BF16), SDS((s, GDN_W), BF16), SDS((s, GDN_W), BF16),
                   SDS((s, GDN_H * c), BF16), SDS((n, 1, LANE), F32)],
        compiler_params=_cp(("parallel",)))(p_all, p_all, p_all, p_all, p_all, p_all, conv_w, p_all, gp)


def gdn_pre_bwd(p_all, conv_w, gp, du, dw, dqd, dkd, dqk, dgl, lay, name):
    s = p_all.shape[0]
    c = GDN_C
    n = _div(s, c)
    cps = _tile(n, GDN_CPS)
    t = cps * c
    steps = n // cps
    chunk = functools.partial(_gdn_chunk, bdot=_bdot_vjp)

    def body(xq, xk, xv, pq, pk, pv, cw, ab, gp_ref, du_r, dw_r, dqd_r, dkd_r, dqk_r, dgl_r,
             dx_ref, dab_ref, dgp_ref, dcw_ref, carry):
        step = pl.program_id(0)

        @pl.when(step == 0)
        def _():
            dgp_ref[...] = jnp.zeros_like(dgp_ref)
            dcw_ref[...] = jnp.zeros_like(dcw_ref)
            carry[...] = jnp.zeros_like(carry)

        lane = lax.broadcasted_iota(jnp.int32, (1, LANE), 1)
        conv_in = _gdn_conv_inputs((xq, xk, xv), (pq, pk, pv), cw, step == steps - 1)
        xs = [_conv_rows(*a) for a in conv_in]
        _, vjp = jax.vjp(chunk, xs[0], xs[1], xs[2], ab[...], gp_ref[...])
        cts = []
        for ci in range(cps):
            rs = pl.ds(ci * c, c)
            heads = []
            for h in range(GDN_H):
                hs = pl.ds(h * GDN_D, GDN_D)
                dgl_h = jnp.sum(jnp.where(lane == h, dgl_r[ci], 0.0), axis=1, keepdims=True)
                heads.append((du_r[rs, hs], dw_r[rs, hs], dqd_r[rs, hs], dkd_r[rs, hs], dqk_r[rs, pl.ds(h * c, c)], dgl_h))
            cts.append(heads)
        *dxs, dab, dgp = vjp(cts)
        for j, (d, (cur, prev8, w)) in enumerate(zip(dxs, conv_in)):
            cols = pl.ds(j * GDN_W, GDN_W)
            dx, dws = _conv_rows_bwd(cur, prev8, d, carry[:, cols], w)
            carry[:, cols] = d[0:8]
            dx_ref[:, cols] = dx.astype(BF16)
            for sft in range(GDN_CONV):
                dcw_ref[GDN_CONV - 1 - sft:GDN_CONV - sft, cols] += dws[sft]
        dab_ref[...] = dab.astype(BF16)
        dgp_ref[...] += dgp

    tile = lambda i: steps - 1 - i
    row = pl.BlockSpec((t, GDN_W), lambda i: (tile(i), 0))
    return pl.pallas_call(
        body, name=name, grid=(steps,),
        in_specs=_gdn_pre_specs(lay, t, tile) + [row, row, row, row, pl.BlockSpec((t, GDN_H * c), lambda i: (tile(i), 0)),
                                                 pl.BlockSpec((cps, 1, LANE), lambda i: (tile(i), 0, 0))],
        out_specs=[pl.BlockSpec((t, 3 * GDN_W), lambda i: (tile(i), 0)), pl.BlockSpec((t, LANE), lambda i: (tile(i), 0)),
                   pl.BlockSpec((8, LANE), lambda i: (0, 0)), pl.BlockSpec((GDN_CONV, 3 * GDN_W), lambda i: (0, 0))],
        out_shape=[SDS((s, 3 * GDN_W), BF16), SDS((s, LANE), BF16), SDS((8, LANE), F32), SDS((GDN_CONV, 3 * GDN_W), F32)],
        scratch_shapes=[pltpu.VMEM((8, 3 * GDN_W), F32)],
        compiler_params=_cp(("arbitrary",)))(p_all, p_all, p_all, p_all, p_all, p_all, conv_w, p_all, gp,
                                             du, dw, dqd, dkd, dqk, dgl)


def _lane_scalar(row, h):
    lane = lax.broadcasted_iota(jnp.int32, row.shape, 1)
    return jnp.sum(jnp.where(lane == h, row, 0.0), axis=1, keepdims=True)


def gdn_scan_fwd(u, w, qd, kd, qk, gl, name):
    s = u.shape[0]
    c = GDN_C
    n = _div(s, c)
    cps = _tile(n, GDN_CPS)
    t = cps * c

    def body(u_r, w_r, qd_r, kd_r, qk_r, gl_r, o_ref, s_ref, st):
        @pl.when(pl.program_id(0) == 0)
        def _():
            st[...] = jnp.zeros_like(st)

        heads = range(GDN_H)
        hs = [pl.ds(h * GDN_D, GDN_D) for h in heads]
        for ci in range(cps):
            rs = pl.ds(ci * c, c)
            s_ref[ci] = st[...]
            sh = [st[hs[h], :] for h in heads]
            shb = [x.astype(BF16) for x in sh]
            ws = [_dot(w_r[rs, hs[h]], shb[h], NN) for h in heads]
            qs = [_dot(qd_r[rs, hs[h]], shb[h], NN) for h in heads]
            vb = [(u_r[rs, hs[h]] - ws[h]).astype(BF16) for h in heads]
            ov = [_dot(qk_r[rs, pl.ds(h * c, c)], vb[h], NN) for h in heads]
            kv = [_dot(kd_r[rs, hs[h]], vb[h], TN) for h in heads]
            for h in heads:
                o_ref[rs, hs[h]] = qs[h] + ov[h]
                st[hs[h], :] = sh[h] * _lane_scalar(gl_r[ci], h) + kv[h]

    row = pl.BlockSpec((t, GDN_W), lambda i: (i, 0))
    return pl.pallas_call(
        body, name=name, grid=(n // cps,),
        in_specs=[row, row, row, row, pl.BlockSpec((t, GDN_H * c), lambda i: (i, 0)), pl.BlockSpec((cps, 1, LANE), lambda i: (i, 0, 0))],
        out_specs=[row, pl.BlockSpec((cps, GDN_W, GDN_D), lambda i: (i, 0, 0))],
        out_shape=[SDS((s, GDN_W), F32), SDS((n, GDN_W, GDN_D), F32)],
        scratch_shapes=[pltpu.VMEM((GDN_W, GDN_D), F32)],
        compiler_params=_cp(("arbitrary",)))(u, w, qd, kd, qk, gl)


def gdn_scan_bwd(u, w, qd, kd, qk, gl, states, do, name):
    s = u.shape[0]
    c = GDN_C
    n = _div(s, c)
    cps = _tile(n, GDN_CPS)
    t = cps * c
    steps = n // cps

    def body(u_r, w_r, qd_r, kd_r, qk_r, gl_r, s_r, do_r, du_o, dw_o, dqd_o, dkd_o, dqk_o, dgl_o, dst):
        @pl.when(pl.program_id(0) == 0)
        def _():
            dst[...] = jnp.zeros_like(dst)

        lane = lax.broadcasted_iota(jnp.int32, (1, LANE), 1)
        heads = range(GDN_H)
        hs = [pl.ds(h * GDN_D, GDN_D) for h in heads]
        qs = [pl.ds(h * c, c) for h in heads]
        for ci in reversed(range(cps)):
            rs = pl.ds(ci * c, c)
            sh = [s_r[ci, hs[h], :] for h in heads]
            shb = [x.astype(BF16) for x in sh]
            ds_out = [dst[hs[h], :] for h in heads]
            dsb = [x.astype(BF16) for x in ds_out]
            dob = [do_r[rs, hs[h]].astype(BF16) for h in heads]
            ws = [_dot(w_r[rs, hs[h]], shb[h], NN) for h in heads]
            dv1 = [_dot(qk_r[rs, qs[h]], dob[h], TN) for h in heads]
            dv2 = [_dot(kd_r[rs, hs[h]], dsb[h], NN) for h in heads]
            dqd = [_dot(dob[h], shb[h], NT) for h in heads]
            dsq = [_dot(qd_r[rs, hs[h]], dob[h], TN) for h in heads]
            vb = [(u_r[rs, hs[h]] - ws[h]).astype(BF16) for h in heads]
            dv = [dv1[h] + dv2[h] for h in heads]
            dvb = [x.astype(BF16) for x in dv]
            dw = [_dot(dvb[h], shb[h], NT) for h in heads]
            dkd = [_dot(vb[h], dsb[h], NT) for h in heads]
            dqk = [_dot(dob[h], vb[h], NT) for h in heads]
            dsw = [_dot(w_r[rs, hs[h]], dvb[h], TN) for h in heads]
            dgl_row = jnp.zeros((1, LANE), F32)
            for h in heads:
                du_o[rs, hs[h]] = dv[h]
                dw_o[rs, hs[h]] = -dw[h]
                dqd_o[rs, hs[h]] = dqd[h]
                dkd_o[rs, hs[h]] = dkd[h]
                dqk_o[rs, qs[h]] = dqk[h]
                dgl_row = dgl_row + jnp.where(lane == h, jnp.sum(jnp.sum(ds_out[h] * sh[h], axis=1, keepdims=True), axis=0, keepdims=True), 0.0)
                dst[hs[h], :] = ds_out[h] * _lane_scalar(gl_r[ci], h) + dsq[h] - dsw[h]
            dgl_o[ci] = dgl_row

    rev = lambda i: steps - 1 - i
    row = pl.BlockSpec((t, GDN_W), lambda i: (rev(i), 0))
    qks = pl.BlockSpec((t, GDN_H * c), lambda i: (rev(i), 0))
    gls = pl.BlockSpec((cps, 1, LANE), lambda i: (rev(i), 0, 0))
    return pl.pallas_call(
        body, name=name, grid=(steps,),
        in_specs=[row, row, row, row, qks, gls, pl.BlockSpec((cps, GDN_W, GDN_D), lambda i: (rev(i), 0, 0)), row],
        out_specs=[row, row, row, row, qks, gls],
        out_shape=[SDS((s, GDN_W), F32)] * 4 + [SDS((s, GDN_H * c), F32), SDS((n, 1, LANE), F32)],
        scratch_shapes=[pltpu.VMEM((GDN_W, GDN_D), F32)],
        compiler_params=_cp(("arbitrary",)))(u, w, qd, kd, qk, gl, states, do)


def _gdn_out_rows(o, z, nw):
    outs = []
    for h in range(GDN_H):
        hs = slice(h * GDN_D, (h + 1) * GDN_D)
        oh = o[:, hs]
        y = oh * lax.rsqrt(jnp.mean(oh * oh, axis=-1, keepdims=True) + RMS_EPS) * nw
        outs.append(y * _silu(z[:, hs]))
    return jnp.concatenate(outs, axis=1)


def gdn_out(o, p_all, nw, dy, lay, name):
    s = o.shape[0]
    tm = _tile(s, 512)
    bwd = dy is not None

    def body(*refs):
        o_r, z_r, nw_r = refs[:3]
        if not bwd:
            refs[3][...] = _gdn_out_rows(o_r[...], z_r[...], nw_r[...]).astype(BF16)
            return
        dy_r, do_o, dz_o, dnw_o = refs[3:]
        _, vjp = jax.vjp(_gdn_out_rows, o_r[...], z_r[...], nw_r[...])
        d_o, d_z, d_nw = vjp(dy_r[...].astype(F32))
        do_o[...] = d_o
        dz_o[...] = d_z.astype(BF16)

        @pl.when(pl.program_id(0) == 0)
        def _():
            dnw_o[...] = jnp.zeros_like(dnw_o)

        dnw_o[...] += d_nw

    row = pl.BlockSpec((tm, GDN_W), lambda i: (i, 0))
    zs = pl.BlockSpec((tm, GDN_W), lambda i: (i, _div(lay.z, GDN_W)))
    nws = pl.BlockSpec((1, GDN_D), lambda i: (0, 0))
    if not bwd:
        return pl.pallas_call(body, name=name, grid=(s // tm,), in_specs=[row, zs, nws], out_specs=row,
                              out_shape=SDS((s, GDN_W), BF16), compiler_params=_cp(("parallel",)))(o, p_all, nw)
    return pl.pallas_call(body, name=name, grid=(s // tm,), in_specs=[row, zs, nws, row], out_specs=[row, row, nws],
                          out_shape=[SDS((s, GDN_W), F32), SDS((s, GDN_W), BF16), SDS((1, GDN_D), F32)],
                          compiler_params=_cp(("arbitrary",)))(o, p_all, nw, dy)


def _cols_to_full(g):
    n, k, c = g.shape
    return g.transpose(1, 0, 2).reshape(k, n * c)


def _rows_to_blocks(w):
    return w.reshape(N_DEV, w.shape[0] // N_DEV, w.shape[1])


def _pack_small(parts, rows):
    flat = jnp.concatenate([jnp.pad(p.reshape(-1), (0, -p.size % LANE)) for p in parts])
    return jnp.pad(flat, (0, rows * LANE - flat.size)).reshape(rows, LANE)


def kernel(x, mem, g_mix, w_in, sinks, conv_w, a_log, dt_bias, gdn_norm_w, g_mem, w_mem_kv, w_swa_up, w_gdn_up, w_xa_up, w_out, g_mlp, w_mlp_in, w_mlp_out, g_final, loss_target, m_g_mix, m_w_in, m_sinks, m_conv_w, m_a_log, m_dt_bias, m_gdn_norm_w, m_g_mem, m_w_mem_kv, m_w_swa_up, m_w_gdn_up, m_w_xa_up, m_w_out, m_g_mlp, m_w_mlp_in, m_w_mlp_out, m_g_final, v_g_mix, v_w_in, v_sinks, v_conv_w, v_a_log, v_dt_bias, v_gdn_norm_w, v_g_mem, v_w_mem_kv, v_w_swa_up, v_w_gdn_up, v_w_xa_up, v_w_out, v_g_mlp, v_w_mlp_in, v_w_mlp_out, v_g_final):
    xs, ms, tgt = x[0], mem[0], loss_target[0]
    s, d = xs.shape
    lay = Layout(d)
    px, py, pc = _position()
    dev = 4 * px + 2 * py + pc

    n1, g_in, g_conv = rmsnorm_fwd(xs, g_mix, "norm_mix", side=GatherJob([w_in[0].astype(BF16), conv_w[0]]))
    W_in = pad_w_in(g_in, lay)
    convw = _cols_to_full(g_conv)
    gp = jnp.zeros((8, LANE), F32).at[0, :GDN_H].set(a_log[0]).at[1, :GDN_H].set(dt_bias[0])
    later = [w_mem_kv[0], w_swa_up[0], w_gdn_up[0], w_xa_up[0], w_out[0], w_mlp_in[0]]

    p_all, g_mkv, W_sup, W_gup, W_xup, g_out, W_m1 = matmul(
        n1, W_in, mode="nn", out_dtype=F32, name="proj_in", tm=2048, tn=1024, tk=d,
        side=GatherJob([w.astype(BF16) for w in later]))
    W_mkv = g_mkv.reshape(-1, g_mkv.shape[2])
    W_out = g_out.reshape(-1, d)
    y_a = swa_fwd(p_all, sinks, lay, "swa_fwd")
    u, gw, gqd, gkd, gqk, ggl = gdn_pre_fwd(p_all, convw, gp, lay, "gdn_pre_fwd")
    o_b, states = gdn_scan_fwd(u, gw, gqd, gkd, gqk, ggl, "gdn_scan_fwd")
    y_b = gdn_out(o_b, p_all, gdn_norm_w, None, lay, "gdn_out_fwd")
    nm = rmsnorm_fwd(ms, g_mem, "norm_mem")
    mkv = matmul(nm, W_mkv, mode="nn", out_dtype=BF16, name="proj_mem", tk=d)
    y_c = xattn_fwd(p_all, mkv, lay, "xattn_fwd")
    merged = merge(p_all, y_a, y_b, y_c, W_sup, W_gup, W_xup, None, lay, "merge_fwd")
    h1, n2 = matmul(merged, W_out, mode="nn", out_dtype=F32, name="proj_out", tm=512, tn=d, tk=d, resid=xs, rms_gain=g_mlp)
    uu, act, g_m2 = matmul(n2, W_m1, mode="nn", out_dtype=F32, name="mlp_in", tm=2048, tn=512, tk=d, b_cols=True,
                           relu2_out=True, side=GatherJob([w_mlp_out[0].astype(BF16)]))
    W_m2 = g_m2.reshape(-1, d)
    dh2, dh2_b, dg_final, lrow = matmul(act, W_m2, mode="nn", out_dtype=F32, name="mlp_out_loss", tm=512, tn=d, tk=1024,
                                        resid=h1, rms_gain=g_final.reshape(1, d), loss_target=tgt)
    loss = lax.psum(lrow[0, 0], ("x", "y", "c"))

    du = matmul(dh2_b, W_m2, mode="nt", out_dtype=BF16, name="mlp_out_dx", tm=2048, tn=512, tk=d, relu2_grad_of=uu)
    dW_m2 = matmul(act, dh2_b, mode="tn", out_dtype=BF16, name="mlp_out_dw", tm=1024, tn=2048, tk=1024)
    dW_m2 = _rows_to_blocks(dW_m2)
    dn2, sib_m2 = matmul(du, W_m1, mode="nt", out_dtype=F32, name="mlp_in_dx", tm=1024, tn=2048, tk=1024, b_cols=True,
                         side=PairExchangeJob([dW_m2], [False]))
    c_m2 = pair_add(dW_m2, False, sib_m2, "grads_pair_add_m2")
    dW_m1 = matmul(n2, du, mode="tn", out_dtype=BF16, name="mlp_in_dw", tm=2048, tn=1024, tk=1024)
    dh1, dg_mlp, dh1_b = rmsnorm_bwd(h1, g_mlp, dn2, dh2, "norm_mlp_bwd", bf16_copy=True)

    dmerged, sib_m1 = matmul(dh1_b, W_out, mode="nt", out_dtype=F32, name="proj_out_dx", tm=2048, tn=512, tk=d,
                             side=PairExchangeJob([dW_m1], [True]))
    c_m1 = pair_add(dW_m1, True, sib_m1, "grads_pair_add_m1")
    dW_out = matmul(merged, dh1_b, mode="tn", out_dtype=BF16, name="proj_out_dw", tm=2048, tn=1024, tk=1024)
    dgates, dta, dtb, dtc = merge(p_all, y_a, y_b, y_c, W_sup, W_gup, W_xup, dmerged, lay, "merge_bwd")
    dy_a = matmul(dta, W_sup, mode="nt", out_dtype=BF16, name="swa_up_dx", tm=2048, tk=d, b_cols=True)
    dy_b = matmul(dtb, W_gup, mode="nt", out_dtype=BF16, name="gdn_up_dx", tm=2048, tk=d, b_cols=True)
    dy_c = matmul(dtc, W_xup, mode="nt", out_dtype=BF16, name="xa_up_dx", tm=2048, tk=d, b_cols=True)
    dW_sup = matmul(y_a, dta, mode="tn", out_dtype=BF16, name="swa_up_dw", tn=2048, tk=2048)
    dW_gup = matmul(y_b, dtb, mode="tn", out_dtype=BF16, name="gdn_up_dw", tn=2048, tk=2048)
    dW_xup = matmul(y_c, dtc, mode="tn", out_dtype=BF16, name="xa_up_dw", tn=2048, tk=2048)

    dq_a, dk_a, dv_a, dsinks = swa_bwd(p_all, sinks, dy_a, lay, "swa_bwd")
    dq_c, dmkv = xattn_bwd(p_all, mkv, dy_c, lay, "xattn_bwd")
    dW_mkv = matmul(nm, dmkv, mode="tn", out_dtype=BF16, name="proj_mem_dw", tk=256)
    dnm = matmul(dmkv, W_mkv, mode="nt", out_dtype=F32, name="proj_mem_dx", tk=1024)
    _, dg_mem = rmsnorm_bwd(ms, g_mem, dnm, None, "norm_mem_bwd")

    do_b, dz, dnorm_w = gdn_out(o_b, p_all, gdn_norm_w, dy_b, lay, "gdn_out_bwd")
    du_g, dw_g, dqd_g, dkd_g, dqk_g, dgl_g = gdn_scan_bwd(u, gw, gqd, gkd, gqk, ggl, states, do_b, "gdn_scan_bwd")
    dqkv, dab, dgp, dconv = gdn_pre_bwd(p_all, convw, gp, du_g, dw_g, dqd_g, dkd_g, dqk_g, dgl_g, lay, "gdn_pre_bwd")

    drest = jnp.concatenate([dq_a, dqkv, dz, dq_c, dk_a, dv_a, dab, jnp.zeros((s, lay.pw - lay.end), BF16)], axis=1)
    def pair_stage(grads, cols, tag):
        from_sib = run_job(PairExchangeJob(grads, cols), "grads_pair_exchange_" + tag)
        return [pair_add(g, cl, o, "grads_pair_add_%s%d" % (tag, i)) for i, (g, cl, o) in enumerate(zip(grads, cols, from_sib))]

    small = pair_stage([_rows_to_blocks(dW_mkv), dW_sup, dW_gup, dW_xup, _rows_to_blocks(dW_out)],
                       [False, True, True, True, False], "a")
    dW_in, p_m1, p_m2 = matmul(n1, dgates, tail=drest, mode="tn", out_dtype=BF16, name="proj_in_dw", tm=2048, tn=1024, tk=1024,
                               side=ChipExchangeJob([c_m1, c_m2]))
    late = pair_stage([unpad_dw_in(dW_in, lay)], [False], "b")
    dn1, p_in, p_mkv, p_sup, p_gup, p_xup, p_out = matmul(
        dgates, W_in, tail=drest, mode="nt", out_dtype=F32, name="proj_in_dx", tm=1024, tn=2048, tk=1024,
        side=ChipExchangeJob(late + small))
    grad_x, dg_mix = rmsnorm_bwd(xs, g_mix, dn1, dh1, "norm_mix_bwd")
    parts = [p_in, p_mkv, p_sup, p_gup, p_xup, p_out, p_m1, p_m2]

    shard_names = [(w_in, m_w_in, v_w_in), (w_mem_kv, m_w_mem_kv, v_w_mem_kv), (w_swa_up, m_w_swa_up, v_w_swa_up),
                   (w_gdn_up, m_w_gdn_up, v_w_gdn_up), (w_xa_up, m_w_xa_up, v_w_xa_up), (w_out, m_w_out, v_w_out),
                   (w_mlp_in, m_w_mlp_in, v_w_mlp_in), (w_mlp_out, m_w_mlp_out, v_w_mlp_out)]
    big_res = [adamw(p, w[0], m[0], v[0], "adamw_%d" % i) for i, (p, (w, m, v)) in enumerate(zip(parts, shard_names))]

    smalls = [(g_mix, m_g_mix, v_g_mix, dg_mix), (sinks, m_sinks, v_sinks, dsinks[:, :SWA_HQ]),
              (a_log, m_a_log, v_a_log, dgp[0:1, :GDN_H]), (dt_bias, m_dt_bias, v_dt_bias, dgp[1:2, :GDN_H]),
              (gdn_norm_w, m_gdn_norm_w, v_gdn_norm_w, dnorm_w), (g_mem, m_g_mem, v_g_mem, dg_mem),
              (g_mlp, m_g_mlp, v_g_mlp, dg_mlp), (g_final, m_g_final, v_g_final, dg_final)]
    sizes = [-(-t[0].size // LANE) * LANE for t in smalls] + [GDN_CONV * 3 * GDN_W]
    rows = -(-sum(sizes) // (8 * LANE)) * 8
    csh = conv_w.shape[2]

    def conv_place(a):
        full = jnp.tile(a[0], (1, N_DEV))
        owner = lax.broadcasted_iota(jnp.int32, full.shape, 1) // csh
        return jnp.where(owner == dev, full, 0.0)

    g_pack = _pack_small([t[3] for t in smalls] + [dconv], rows)
    w_pack = _pack_small([t[0] for t in smalls] + [conv_place(conv_w)], rows)
    m_pack = _pack_small([t[1] for t in smalls] + [conv_place(m_conv_w)], rows)
    v_pack = _pack_small([t[2] for t in smalls] + [conv_place(v_conv_w)], rows)
    g_all = run_job(GatherJob([g_pack]), "gather_small_grads")[0]
    small_res = adamw(g_all, w_pack, m_pack, v_pack, "adamw_small")

    def unpack(arr):
        flat = arr.reshape(-1)
        outs, off = [], 0
        for t, sz in zip(smalls, sizes[:-1]):
            outs.append(flat[off:off + t[0].size].reshape(t[0].shape))
            off += sz
        cw = flat[off:off + sizes[-1]].reshape(GDN_CONV, 3 * GDN_W)
        mine = (lax.broadcasted_iota(jnp.int32, (1, N_DEV, 1), 1) == dev).astype(F32)
        outs.append(jnp.sum(cw.reshape(GDN_CONV, N_DEV, csh) * mine, axis=1)[None])
        return outs

    sg, sd, sm, sv = (unpack(a) for a in small_res)
    bg, bd, bm, bv = ([r[i][None] for r in big_res] for i in range(4))

    def ordered(sm_, bg_):
        return [sm_[0], bg_[0], sm_[1], sm_[8], sm_[2], sm_[3], sm_[4], sm_[5], bg_[1], bg_[2], bg_[3], bg_[4], bg_[5],
                sm_[6], bg_[6], bg_[7], sm_[7]]

    return (loss, grad_x[None], *ordered(sg, bg), *ordered(sd, bd), *ordered(sm, bm), *ordered(sv, bv))
```

```python
import functools

import jax
import jax.numpy as jnp
from jax import lax
from jax.experimental import pallas as pl
from jax.experimental.pallas import tpu as pltpu

F32, BF16 = jnp.float32, jnp.bfloat16
SDS = jax.ShapeDtypeStruct
MESH = pl.DeviceIdType.MESH
ANY = pl.BlockSpec(memory_space=pl.ANY)

SWA_HQ, SWA_HKV, SWA_HD, SWA_W = 16, 2, 64, 128
SWA_G = SWA_HQ // SWA_HKV
GDN_H, GDN_D, GDN_CONV, GDN_C = 4, 128, 4, 64
XA_H, XA_D = 4, 128
Q_W = SWA_HQ * SWA_HD
KV_W = SWA_HKV * SWA_HD
GDN_W = GDN_H * GDN_D
XA_W = XA_H * XA_D
RMS_EPS = 1e-6
L2_EPS = 1e-6
NEG = -1e30
N_DEV = 8
LANE = 128

ADAM_LR, ADAM_B1, ADAM_B2, ADAM_EPS, ADAM_WD, ADAM_STEP = 0.001, 0.9, 0.999, 1e-08, 0.01, 10

VMEM_BIG = 56 * 1024 * 1024


def _cp(sem, vmem=VMEM_BIG):
    return pltpu.CompilerParams(dimension_semantics=sem, vmem_limit_bytes=vmem)


def _div(a, b):
    assert a % b == 0, (a, b)
    return a // b


def _tile(n, t):
    t = min(t, n)
    assert n % t == 0, (n, t)
    return t


def _sigmoid(x):
    return jax.nn.sigmoid(x)


def _silu(x):
    return x * _sigmoid(x)


def _softplus(x):
    return jnp.maximum(x, 0.0) + jnp.log1p(jnp.exp(-jnp.abs(x)))


def _dot(a, b, dims, prec=None):
    return lax.dot_general(a, b, (dims, ((), ())), precision=prec, preferred_element_type=F32)


NN = ((1,), (0,))
NT = ((1,), (1,))
TN = ((0,), (0,))
HI = lax.Precision.HIGHEST


def _bdot_plain(a, b, dims):
    return _dot(a.astype(BF16), b.astype(BF16), dims)


@functools.partial(jax.custom_vjp, nondiff_argnums=(2,))
def _bdot_vjp(a, b, dims):
    return _bdot_plain(a, b, dims)


def _bdot_vjp_fwd(a, b, dims):
    return _bdot_plain(a, b, dims), (a, b)


def _bdot_vjp_bwd(dims, res, ct):
    a, b = res
    if dims == NN:
        return _bdot_plain(ct, b, NT), _bdot_plain(a, ct, TN)
    assert dims == NT, dims
    return _bdot_plain(ct, b, NN), _bdot_plain(ct, a, TN)


_bdot_vjp.defvjp(_bdot_vjp_fwd, _bdot_vjp_bwd)


def _neumann(xs):
    pws, nns = list(xs), list(xs)
    for _ in range(5):
        pws = [_bdot_plain(p, p, NN) for p in pws]
        nns = [n + p + _bdot_plain(n, p, NN) for n, p in zip(nns, pws)]
    return tuple(nns)


@jax.custom_vjp
def _neumann_vjp(xs):
    return _neumann(xs)


def _neumann_vjp_fwd(xs):
    nns = _neumann(xs)
    return nns, nns


def _neumann_vjp_bwd(nns, cts):
    ts = [ct + _bdot_plain(nn, ct, TN) for nn, ct in zip(nns, cts)]
    return (tuple(t + _bdot_plain(t, nn, NT) for t, nn in zip(ts, nns)),)


_neumann_vjp.defvjp(_neumann_vjp_fwd, _neumann_vjp_bwd)


class Layout:
    def __init__(self, d):
        self.d = d
        self.g = 0
        self.q = 3 * d
        self.qkv = self.q + Q_W
        self.z = self.qkv + 3 * GDN_W
        self.qc = self.z + GDN_W
        self.k = self.qc + XA_W
        self.v = self.k + KV_W
        self.ab = self.v + KV_W
        self.end = self.ab + LANE
        self.pw = -(-self.end // 1024) * 1024
        self.lq, self.lk, self.lv, self.lqkv = 0, Q_W, Q_W + KV_W, Q_W + 2 * KV_W
        self.la = self.lqkv + 3 * GDN_W
        self.lz = self.la + 2 * GDN_H
        self.lqc = self.lz + GDN_W
        self.lg = self.lqc + XA_W
        self.lw = self.lg + 3 * d

    def pieces(self):
        segs = [(self.lq, self.lk, self.q), (self.lk, self.lv, self.k), (self.lv, self.lqkv, self.v),
                (self.lqkv, self.la, self.qkv), (self.la, self.lz, self.ab), (self.lz, self.lqc, self.z),
                (self.lqc, self.lg, self.qc), (self.lg, self.lw, self.g)]
        cw = _div(self.lw, N_DEV)
        out = []
        for dev in range(N_DEV):
            lo, hi = dev * cw, (dev + 1) * cw
            for ls, le, ps in segs:
                s, e = max(lo, ls), min(hi, le)
                if s < e:
                    out.append((dev, s - lo, ps + s - ls, e - s))
        return out


def pad_w_in(g, lay):
    nd, k, cw = g.shape
    tr = _tile(k, 256)
    tail = lay.ab + 2 * GDN_H

    def body(g_ref, o_ref):
        o_ref[:, pl.ds(tail, lay.pw - tail)] = jnp.zeros((tr, lay.pw - tail), o_ref.dtype)
        for dev, so, po, ln in lay.pieces():
            o_ref[:, pl.ds(po, ln)] = g_ref[dev, :, pl.ds(so, ln)]

    return pl.pallas_call(
        body, name="pad_w_in", grid=(k // tr,), in_specs=[pl.BlockSpec((nd, tr, cw), lambda i: (0, i, 0))],
        out_specs=pl.BlockSpec((tr, lay.pw), lambda i: (i, 0)), out_shape=SDS((k, lay.pw), g.dtype),
        compiler_params=_cp(("parallel",)))(g)


def unpad_dw_in(dw, lay):
    k = dw.shape[0]
    cw = _div(lay.lw, N_DEV)
    tr = _tile(k, 256)

    def body(d_ref, o_ref):
        for dev, so, po, ln in lay.pieces():
            o_ref[dev, :, pl.ds(so, ln)] = d_ref[:, pl.ds(po, ln)]

    return pl.pallas_call(
        body, name="unpad_dw_in", grid=(k // tr,), in_specs=[pl.BlockSpec((tr, lay.pw), lambda i: (i, 0))],
        out_specs=pl.BlockSpec((N_DEV, tr, cw), lambda i: (0, i, 0)), out_shape=SDS((N_DEV, k, cw), dw.dtype),
        compiler_params=_cp(("parallel",)))(dw)


def _position():
    return lax.axis_index("x"), lax.axis_index("y"), lax.axis_index("c")


class GatherJob:
    def __init__(self, arrs):
        self.ins = list(arrs)
        n = len(arrs)
        self.out_shapes = [SDS((N_DEV,) + a.shape, a.dtype) for a in arrs]
        self.scratch = [pltpu.SemaphoreType.DMA((n, 7)), pltpu.SemaphoreType.DMA((n, 7)), pltpu.SemaphoreType.DMA((n,))]

    def _ctx(self, outs, sems):
        send_sems, recv_sems, _ = sems
        x, y, c = _position()

        def blk(o, p):
            return o.at[4 * p[0] + 2 * p[1] + p[2]]

        def copy(i, k, block, to, src=None):
            return pltpu.make_async_remote_copy(
                src_ref=blk(outs[i], block) if src is None else src, dst_ref=blk(outs[i], block),
                send_sem=send_sems.at[i, k], recv_sem=recv_sems.at[i, k], device_id=to, device_id_type=MESH)

        return (x, y, c), (x, y, 1 - c), [(1 - x, y), (x, 1 - y), (1 - x, 1 - y)], blk, copy

    def start(self, ins, outs, sems):
        me, sibling, chips, blk, copy = self._ctx(outs, sems)
        for i in range(len(ins)):
            pltpu.make_async_copy(ins[i], blk(outs[i], me), sems[2].at[i]).start()
            copy(i, 0, me, sibling, src=ins[i]).start()
            for j, chip in enumerate(chips[:2]):
                copy(i, 1 + j, me, (*chip, me[2]), src=ins[i]).start()

    def _relay(self, i, outs, sems, onward):
        me, _, _, blk, copy = self._ctx(outs, sems)
        x, y, c = me
        origin = (x + (1 - c) - 2 * x * (1 - c), y + c - 2 * y * c, c)
        if not onward:
            return copy(i, 1 + c, origin, me)
        return copy(i, 3, origin, (x + c - 2 * x * c, y + (1 - c) - 2 * y * (1 - c), c))

    def relay(self, ins, outs, sems):
        for i in range(len(ins)):
            self._relay(i, outs, sems, False).wait_recv()
            self._relay(i, outs, sems, True).start()

    def mid(self, ins, outs, sems):
        me, sibling, chips, blk, copy = self._ctx(outs, sems)
        for i in range(len(ins)):
            for j, chip in enumerate(chips):
                arrival = copy(i, 1 + j, (*chip, me[2]), me)
                if j < 2:
                    pl.when(me[2] != j)(arrival.wait_recv)
                else:
                    arrival.wait_recv()
                copy(i, 4 + j, (*chip, me[2]), sibling).start()

    def finish(self, ins, outs, sems):
        me, sibling, chips, blk, copy = self._ctx(outs, sems)
        for i in range(len(ins)):
            copy(i, 0, sibling, me).wait_recv()
            for j, chip in enumerate(chips):
                copy(i, 4 + j, (*chip, 1 - me[2]), me).wait_recv()
        for i in range(len(ins)):
            pltpu.make_async_copy(ins[i], blk(outs[i], me), sems[2].at[i]).wait()
            copy(i, 0, me, sibling, src=ins[i]).wait_send()
            for j, chip in enumerate(chips[:2]):
                copy(i, 1 + j, me, (*chip, me[2]), src=ins[i]).wait_send()
            self._relay(i, outs, sems, True).wait_send()
            for j, chip in enumerate(chips):
                copy(i, 4 + j, (*chip, me[2]), sibling).wait_send()


class ChipExchangeJob:
    mid = None

    def __init__(self, arrs):
        self.ins = list(arrs)
        n = len(arrs)
        self.out_shapes = [SDS(a.shape, a.dtype) for a in arrs]
        self.scratch = [pltpu.SemaphoreType.DMA((n, 3)), pltpu.SemaphoreType.DMA((n, 3)), pltpu.SemaphoreType.DMA((n,))]

    def _copies(self, ins, outs, sems, i, arrivals):
        send_sems, recv_sems, local_sems = sems
        x, y, c = _position()
        my_chip = 2 * x + y
        chips = [(1 - x, y), (x, 1 - y), (1 - x, 1 - y)]
        if arrivals:
            return [pltpu.make_async_remote_copy(
                src_ref=ins[i].at[my_chip], dst_ref=outs[i].at[2 * px + py], send_sem=send_sems.at[i, k],
                recv_sem=recv_sems.at[i, k], device_id=(px, py, c), device_id_type=MESH) for k, (px, py) in enumerate(chips)]
        local = pltpu.make_async_copy(ins[i].at[my_chip], outs[i].at[my_chip], local_sems.at[i])
        return local, [pltpu.make_async_remote_copy(
            src_ref=ins[i].at[2 * px + py], dst_ref=outs[i].at[my_chip], send_sem=send_sems.at[i, k],
            recv_sem=recv_sems.at[i, k], device_id=(px, py, c), device_id_type=MESH) for k, (px, py) in enumerate(chips)]

    def start(self, ins, outs, sems):
        for i in range(len(ins)):
            local, remote = self._copies(ins, outs, sems, i, False)
            local.start()
            for cp in remote:
                cp.start()

    def finish(self, ins, outs, sems):
        for i in range(len(ins)):
            for cp in self._copies(ins, outs, sems, i, True):
                cp.wait_recv()
            local, remote = self._copies(ins, outs, sems, i, False)
            for cp in remote:
                cp.wait_send()
            local.wait()


def _slab_shape(g, cols):
    return (g.shape[0], _div(g.shape[1], N_DEV)) if cols else g.shape[1:]


class PairExchangeJob:
    mid = None

    def __init__(self, grads, cols):
        self.ins, self.cols = list(grads), list(cols)
        n = len(grads)
        self.out_shapes = [SDS((4,) + _slab_shape(g, cl), g.dtype) for g, cl in zip(grads, cols)]
        self.scratch = [pltpu.SemaphoreType.DMA((n, 4)), pltpu.SemaphoreType.DMA((n, 4))]

    def _copies(self, ins, outs, sems):
        send_sems, recv_sems = sems
        x, y, c = _position()

        def part(i, dst):
            if not self.cols[i]:
                return ins[i].at[dst]
            cw = _slab_shape(self.ins[i], True)[1]
            return ins[i].at[:, pl.ds(pl.multiple_of(dst * cw, LANE), cw)]

        return [pltpu.make_async_remote_copy(src_ref=part(i, 2 * j + 1 - c), dst_ref=outs[i].at[j], send_sem=send_sems.at[i, j],
                                             recv_sem=recv_sems.at[i, j], device_id=(x, y, 1 - c), device_id_type=MESH)
                for i in range(len(ins)) for j in range(4)]

    def start(self, ins, outs, sems):
        for cp in self._copies(ins, outs, sems):
            cp.start()

    def finish(self, ins, outs, sems):
        for cp in self._copies(ins, outs, sems):
            cp.wait()


def _host_begin(job, step, steps, ins, outs, sems):
    pl.when(step == 0)(lambda: job.start(ins, outs, sems))
    if job.mid is not None:
        pl.when(step == (steps * 45) // 100)(lambda: job.relay(ins, outs, sems))
        pl.when(step == (steps * 85) // 100)(lambda: job.mid(ins, outs, sems))


def run_job(job, name):
    n = len(job.ins)

    def body(*refs):
        ins, outs, sems = refs[:n], refs[n:2 * n], refs[2 * n:]
        job.start(ins, outs, sems)
        if job.mid is not None:
            job.relay(ins, outs, sems)
            job.mid(ins, outs, sems)
        job.finish(ins, outs, sems)

    return pl.pallas_call(body, name=name, out_shape=job.out_shapes, in_specs=[ANY] * n, out_specs=[ANY] * n,
                          scratch_shapes=job.scratch)(*job.ins)


def pair_add(grad, cols, other, name):
    r, c = _slab_shape(grad, cols)
    tr = _tile(r, 256)
    parity = lax.axis_index("c").astype(jnp.int32).reshape(1)

    def body(par_ref, a_ref, b_ref, o_ref):
        o_ref[...] = (a_ref[...].astype(F32) + b_ref[...].astype(F32)).astype(BF16)

    spec = pl.BlockSpec((None, tr, c), lambda j, i, par: (j, i, 0))
    if cols:
        own = pl.BlockSpec((tr, c), lambda j, i, par: (i, 2 * j + par[0]))
    else:
        own = pl.BlockSpec((None, tr, c), lambda j, i, par: (2 * j + par[0], i, 0))
    return pl.pallas_call(
        body, name=name, out_shape=SDS(other.shape, BF16),
        grid_spec=pltpu.PrefetchScalarGridSpec(num_scalar_prefetch=1, grid=(4, r // tr), in_specs=[own, spec], out_specs=spec),
        compiler_params=_cp(("parallel", "parallel")))(parity, grad, other)


def adamw(parts, w, m, v, name):
    p, r, c = parts.shape
    tr = _tile(r, 128 if c > 1024 else 256)

    def body(p_ref, w_ref, m_ref, v_ref, g_out, d_out, m_out, v_out):
        g = p_ref[0].astype(F32)
        for j in range(1, p):
            g = g + p_ref[j].astype(F32)
        mn = ADAM_B1 * m_ref[...] + (1.0 - ADAM_B1) * g
        vn = ADAM_B2 * v_ref[...] + (1.0 - ADAM_B2) * jnp.square(g)
        m_hat = mn / (1.0 - ADAM_B1 ** ADAM_STEP)
        v_hat = vn / (1.0 - ADAM_B2 ** ADAM_STEP)
        g_out[...] = g
        d_out[...] = -ADAM_LR * (m_hat / (jnp.sqrt(v_hat) + ADAM_EPS) + ADAM_WD * w_ref[...])
        m_out[...] = mn
        v_out[...] = vn

    spec = pl.BlockSpec((tr, c), lambda i: (i, 0))
    return pl.pallas_call(
        body, name=name, grid=(r // tr,),
        in_specs=[pl.BlockSpec((p, tr, c), lambda i: (0, i, 0)), spec, spec, spec],
        out_specs=[spec] * 4, out_shape=[SDS((r, c), F32)] * 4, compiler_params=_cp(("parallel",)))(parts, w, m, v)


def matmul(a, b, *, mode, out_dtype, name, tm=1024, tn=1024, tk=512, a_relu2=False, resid=None, relu2_grad_of=None,
           b_cols=False, relu2_out=False, rms_gain=None, loss_target=None, side=None, tail=None):
    if b_cols:
        nb, brows, bc = b.shape
        bshape = (brows, nb * bc)
    else:
        bshape = b.shape
    head_k = head_n = None
    if mode == "nn":
        (m, k), (k2, n) = a.shape, bshape
    elif mode == "nt":
        (m, k), (n, k2) = a.shape, bshape
        if tail is not None:
            head_k, k = k, k + tail.shape[1]
    else:
        (k, m), (k2, n) = a.shape, bshape
        if tail is not None:
            head_n, n = n, n + tail.shape[1]
    assert k == k2 and (tail is None or mode != "nn"), (a.shape, b.shape, mode)
    b_whole = b_cols and mode == "nt" and tk >= k
    tm, tn, tk = _tile(m, tm), _tile(n, tn), _tile(k, tk)
    if b_cols and mode == "nn":
        tn = _tile(bc, tn)
    if b_cols and mode == "nt" and not b_whole:
        tk = _tile(bc, tk)
    nk = k // tk
    ni, nj = m // tm, n // tn
    nk_head = _div(head_k, tk) if head_k is not None else None
    nj_head = _div(head_n, tn) if head_n is not None else None
    use_acc = nk > 1 or tail is not None
    dims = {"nn": NN, "nt": NT, "tn": TN}[mode]
    extras = [e for e in (resid, relu2_grad_of) if e is not None]
    tails = [tail] if tail is not None else []
    n_side = len(side.ins) if side is not None else 0
    loss = loss_target is not None
    if loss:
        extras.append(loss_target)
    gains = [rms_gain] if rms_gain is not None else []
    assert not gains or (tn == n and not relu2_out), (tn, n)
    assert not loss or (gains and resid is not None and relu2_grad_of is None and out_dtype == F32)
    n_main = 2 if (relu2_out or gains) else 1
    n_loss = 2 if loss else 0

    def body(*refs):
        a_ref, b_ref = refs[:2]
        t_ref = refs[2] if tails else None
        n_op = 2 + len(tails)
        e_refs = refs[n_op:n_op + len(extras)]
        n_pre = n_op + len(extras) + len(gains)
        g_ref = refs[n_pre - 1] if gains else None
        n_in = n_pre + n_side
        o_ref = refs[n_in]
        act_ref = refs[n_in + 1] if n_main == 2 else None
        dg_ref, l_ref = refs[n_in + n_main:n_in + n_main + n_loss] if loss else (None, None)
        acc_ref = refs[n_in + n_main + n_loss + n_side] if use_acc else None
        if side is not None:
            s_ins = refs[n_pre:n_in]
            s_outs = refs[n_in + n_main + n_loss:n_in + n_main + n_loss + n_side]
            s_sems = refs[len(refs) - len(side.scratch):]
            step = (pl.program_id(0) * nj + pl.program_id(1)) * nk + pl.program_id(2)
            _host_begin(side, step, ni * nj * nk, s_ins, s_outs, s_sems)

        def operands(a_from=a_ref, b_from=b_ref):
            av = a_from[...]
            if a_relu2:
                av = jnp.square(jnp.maximum(av.astype(F32), 0.0))
            return av.astype(BF16), b_from[...].astype(BF16)

        def finish(r):
            e = list(e_refs)
            if resid is not None:
                r = r + e.pop(0)[...]
            if relu2_grad_of is not None:
                r = r * (2.0 * jnp.maximum(e.pop(0)[...], 0.0))
            if loss:
                gv = g_ref[...]
                inv = lax.rsqrt(jnp.mean(r * r, axis=-1, keepdims=True) + RMS_EPS)
                err = r * inv * gv - e.pop(0)[...]
                lpart = 0.5 * jnp.sum(jnp.mean(err * err, axis=-1, keepdims=True), axis=0, keepdims=True)
                dx, part = _rms_bwd_rows(r, gv, err * (1.0 / n))
                o_ref[...] = dx
                act_ref[...] = dx.astype(BF16)

                @pl.when(pl.program_id(0) == 0)
                def _():
                    dg_ref[...] = jnp.zeros_like(dg_ref)
                    l_ref[...] = jnp.zeros_like(l_ref)

                dg_ref[...] += part
                l_ref[...] += jnp.broadcast_to(lpart, l_ref.shape)
                return
            o_ref[...] = r.astype(out_dtype)
            if relu2_out:
                act_ref[...] = jnp.square(jnp.maximum(r, 0.0)).astype(BF16)
            if gains:
                inv = lax.rsqrt(jnp.mean(r * r, axis=-1, keepdims=True) + RMS_EPS)
                act_ref[...] = (r * inv * g_ref[...]).astype(BF16)

        if b_whole:
            av = a_ref[...].astype(BF16)
            finish(sum(_dot(av[:, kb * bc:(kb + 1) * bc], b_ref[kb].astype(BF16), NT) for kb in range(nb)))
        elif not use_acc:
            av, bv = operands()
            finish(_dot(av, bv, dims))
        else:
            kk = pl.program_id(2)

            def accumulate(a_from, b_from):
                def product():
                    av, bv = operands(a_from, b_from)
                    return _dot(av, bv, dims)

                if nk == 1:
                    finish(product())
                    return

                @pl.when(kk == 0)
                def _():
                    acc_ref[...] = product()

                @pl.when((kk > 0) & (kk < nk - 1))
                def _():
                    acc_ref[...] += product()

                @pl.when(kk == nk - 1)
                def _():
                    finish(acc_ref[...] + product())

            if not tails:
                accumulate(a_ref, b_ref)
            elif mode == "nt":
                pl.when(kk < nk_head)(lambda: accumulate(a_ref, b_ref))
                pl.when(kk >= nk_head)(lambda: accumulate(t_ref, b_ref))
            else:
                in_head = pl.program_id(1) < nj_head
                pl.when(in_head)(lambda: accumulate(a_ref, b_ref))
                pl.when(jnp.logical_not(in_head))(lambda: accumulate(a_ref, t_ref))

        if side is not None:
            pl.when(step == ni * nj * nk - 1)(lambda: side.finish(s_ins, s_outs, s_sems))

    a_spec = {"nn": pl.BlockSpec((tm, tk), lambda i, j, kk: (i, kk)),
              "nt": pl.BlockSpec((tm, tk), lambda i, j, kk: (i, kk)),
              "tn": pl.BlockSpec((tk, tm), lambda i, j, kk: (kk, i))}[mode]
    t_specs = []
    if tails and mode == "nt":
        a_spec = pl.BlockSpec((tm, tk), lambda i, j, kk: (i, jnp.minimum(kk, nk_head - 1)))
        t_specs = [pl.BlockSpec((tm, tk), lambda i, j, kk: (i, jnp.maximum(kk - nk_head, 0)))]
    if tails and mode == "tn":
        t_specs = [pl.BlockSpec((tk, tn), lambda i, j, kk: (kk, jnp.maximum(j - nj_head, 0)))]
    if tails and mode == "tn":
        b_spec = pl.BlockSpec((tk, tn), lambda i, j, kk: (kk, jnp.minimum(j, nj_head - 1)))
    elif not b_cols:
        b_spec = {"nn": pl.BlockSpec((tk, tn), lambda i, j, kk: (kk, j)),
                  "nt": pl.BlockSpec((tn, tk), lambda i, j, kk: (j, kk)),
                  "tn": pl.BlockSpec((tk, tn), lambda i, j, kk: (kk, j))}[mode]
    elif mode == "nn":
        per = bc // tn
        b_spec = pl.BlockSpec((None, tk, tn), lambda i, j, kk: (j // per, kk, j % per))
    elif b_whole:
        b_spec = pl.BlockSpec((nb, tn, bc), lambda i, j, kk: (0, j, 0))
    else:
        assert mode == "nt", mode
        per = bc // tk
        b_spec = pl.BlockSpec((None, tn, tk), lambda i, j, kk: (kk // per, j, kk % per))
    e_spec = pl.BlockSpec((tm, tn), lambda i, j, kk: (i, j))
    main_shapes = [SDS((m, n), out_dtype)] + ([SDS((m, n), BF16)] if n_main == 2 else [])
    g_specs = [pl.BlockSpec((1, tn), lambda i, j, kk: (0, j))] * len(gains)
    l_specs = [pl.BlockSpec((1, tn), lambda i, j, kk: (0, j)), pl.BlockSpec((1, LANE), lambda i, j, kk: (0, 0))] if loss else []
    l_shapes = [SDS((1, n), F32), SDS((1, LANE), F32)] if loss else []
    res = pl.pallas_call(
        body, name=name, grid=(ni, nj, nk),
        in_specs=[a_spec, b_spec] + t_specs + [e_spec] * len(extras) + g_specs + [ANY] * n_side,
        out_specs=[e_spec] * n_main + l_specs + [ANY] * n_side,
        out_shape=main_shapes + l_shapes + (side.out_shapes if side is not None else []),
        scratch_shapes=([pltpu.VMEM((tm, tn), F32)] if use_acc else []) + (side.scratch if side is not None else []),
        compiler_params=_cp(("arbitrary", "arbitrary", "arbitrary")))(a, b, *tails, *extras, *gains, *(side.ins if side is not None else []))
    return res if len(res) > 1 else res[0]


def rmsnorm_fwd(x, g, name, side=None):
    s, d = x.shape
    tm = _tile(s, 256)
    steps = s // tm
    n_side = len(side.ins) if side is not None else 0

    def body(*refs):
        x_ref, g_ref, o_ref = refs[0], refs[1], refs[2 + n_side]
        if side is not None:
            s_ins, s_outs, s_sems = refs[2:2 + n_side], refs[3 + n_side:3 + 2 * n_side], refs[3 + 2 * n_side:]
            _host_begin(side, pl.program_id(0), steps, s_ins, s_outs, s_sems)
        xv = x_ref[...]
        r = lax.rsqrt(jnp.mean(xv * xv, axis=-1, keepdims=True) + RMS_EPS)
        o_ref[...] = (xv * r * g_ref[...]).astype(BF16)
        if side is not None:
            pl.when(pl.program_id(0) == steps - 1)(lambda: side.finish(s_ins, s_outs, s_sems))

    row = pl.BlockSpec((tm, d), lambda i: (i, 0))
    res = pl.pallas_call(
        body, name=name, grid=(steps,), in_specs=[row, pl.BlockSpec((1, d), lambda i: (0, 0))] + [ANY] * n_side,
        out_specs=[row] + [ANY] * n_side, out_shape=[SDS((s, d), BF16)] + (side.out_shapes if side is not None else []),
        scratch_shapes=side.scratch if side is not None else [],
        compiler_params=_cp(("arbitrary",)))(x, g, *(side.ins if side is not None else []))
    return res if side is not None else res[0]


def _rms_bwd_rows(xv, gv, dy):
    r = lax.rsqrt(jnp.mean(xv * xv, axis=-1, keepdims=True) + RMS_EPS)
    xh = xv * r
    dxh = dy * gv
    dx = r * (dxh - xh * jnp.mean(dxh * xh, axis=-1, keepdims=True))
    return dx, jnp.sum(dy * xh, axis=0, keepdims=True)


def rmsnorm_bwd(x, g, dn, resid, name, bf16_copy=False):
    s, d = x.shape
    tm = _tile(s, 256)
    has_r = resid is not None

    def body(*refs):
        x_ref, g_ref, dn_ref = refs[:3]
        dx_ref, dg_ref = refs[3 + has_r:5 + has_r]
        dx, part = _rms_bwd_rows(x_ref[...], g_ref[...], dn_ref[...].astype(F32))
        if has_r:
            dx = dx + refs[3][...]
        dx_ref[...] = dx
        if bf16_copy:
            refs[5 + has_r][...] = dx.astype(BF16)

        @pl.when(pl.program_id(0) == 0)
        def _():
            dg_ref[...] = jnp.zeros_like(dg_ref)

        dg_ref[...] += part

    row = pl.BlockSpec((tm, d), lambda i: (i, 0))
    vec = pl.BlockSpec((1, d), lambda i: (0, 0))
    ins = [x, g, dn] + ([resid] if has_r else [])
    return pl.pallas_call(body, name=name, grid=(s // tm,), in_specs=[row, vec, row] + ([row] if has_r else []),
                          out_specs=[row, vec] + ([row] if bf16_copy else []),
                          out_shape=[SDS((s, d), F32), SDS((1, d), F32)] + ([SDS((s, d), BF16)] if bf16_copy else []),
                          compiler_params=_cp(("arbitrary",)))(*ins)


def merge(p_all, ya, yb, yc, wa, wb, wc, dm, lay, name):
    s, d = ya.shape[0], lay.d
    wcols = wa.shape[2]
    bwd = dm is not None
    tm, tn = _tile(s, 2048), _tile(wcols, 512)
    nj, per = d // tn, wcols // tn

    y_specs = [pl.BlockSpec((tm, y.shape[1]), lambda i, j, *_: (i, 0)) for y in (ya, yb, yc)]
    w_specs = [pl.BlockSpec((None, w.shape[1], tn), lambda i, j, *_: (j // per, 0, j % per)) for w in (wa, wb, wc)]
    o_spec = pl.BlockSpec((tm, tn), lambda i, j, *_: (i, j))
    if not bwd:
        def body(ga, gb, gc, ya_r, yb_r, yc_r, wa_r, wb_r, wc_r, o_ref):
            ts = [_dot(y[...], w[...], NN) for y, w in ((ya_r, wa_r), (yb_r, wb_r), (yc_r, wc_r))]
            gs = [_sigmoid(g[...]) for g in (ga, gb, gc)]
            o_ref[...] = (gs[0] * ts[0] + gs[1] * ts[1] + gs[2] * ts[2]).astype(BF16)

        gate_specs = [pl.BlockSpec((tm, tn), lambda i, j, b=b: (i, b * nj + j)) for b in range(3)]
        return pl.pallas_call(
            body, name=name, grid=(s // tm, nj), in_specs=gate_specs + y_specs + w_specs,
            out_specs=o_spec, out_shape=SDS((s, d), BF16),
            compiler_params=_cp(("parallel", "parallel")))(p_all, p_all, p_all, ya, yb, yc, wa, wb, wc)

    def body_bwd(g_r, ya_r, yb_r, yc_r, wa_r, wb_r, wc_r, dm_r, dg_o, dta_o, dtb_o, dtc_o):
        for k, (y, w, dt_o) in enumerate(((ya_r, wa_r, dta_o), (yb_r, wb_r, dtb_o), (yc_r, wc_r, dtc_o))):
            @pl.when(pl.program_id(2) == k)
            def _(y=y, w=w, dt_o=dt_o):
                t = _dot(y[...], w[...], NN)
                g = _sigmoid(g_r[...])
                dmv = dm_r[...]
                dg_o[...] = (dmv * t * (g * (1.0 - g))).astype(BF16)
                dt_o[...] = (dmv * g).astype(BF16)

    gate_spec = pl.BlockSpec((tm, tn), lambda i, j, b: (i, b * nj + j))
    return pl.pallas_call(
        body_bwd, name=name, grid=(s // tm, nj, 3), in_specs=[gate_spec] + y_specs + w_specs + [o_spec],
        out_specs=[gate_spec, o_spec, o_spec, o_spec], out_shape=[SDS((s, 3 * d), BF16)] + [SDS((s, d), BF16)] * 3,
        compiler_params=_cp(("arbitrary", "arbitrary", "arbitrary")))(p_all, ya, yb, yc, wa, wb, wc, dm)


SWA_PAIRS = SWA_G // 2


def _swa_probs(qs, kcs, sinks, first):
    shape = (qs[0].shape[0], 2 * SWA_W)
    qi = lax.broadcasted_iota(jnp.int32, shape, 0) % SWA_W
    kj = lax.broadcasted_iota(jnp.int32, shape, 1)
    mask = (kj > qi) & (kj <= qi + SWA_W) & ((kj >= SWA_W) | jnp.logical_not(first))
    ss = [jnp.where(mask, _dot(q, kc, NT) * (SWA_HD ** -0.5), NEG) for q, kc in zip(qs, kcs)]
    ms = [jnp.maximum(jnp.max(s, axis=-1, keepdims=True), sink) for s, sink in zip(ss, sinks)]
    ps = [jnp.exp(s - m) for s, m in zip(ss, ms)]
    es = [jnp.exp(sink - m) for sink, m in zip(sinks, ms)]
    inv = [1.0 / (jnp.sum(p, axis=-1, keepdims=True) + e) for p, e in zip(ps, es)]
    return [p * i for p, i in zip(ps, inv)], [e * i for e, i in zip(es, inv)]


def _swa_stack(ref, h):
    return jnp.concatenate([ref[:, pl.ds((h * SWA_PAIRS + p) * LANE, LANE)] for p in range(SWA_PAIRS)], axis=0)


def _swa_unstack(ref, h, val):
    for p in range(SWA_PAIRS):
        ref[:, pl.ds((h * SWA_PAIRS + p) * LANE, LANE)] = val[p * SWA_W:(p + 1) * SWA_W]


def _swa_sink_col(sk_ref, h, second):
    pair = lax.broadcasted_iota(jnp.int32, (SWA_PAIRS * SWA_W, 1), 0) // SWA_W
    col = jnp.zeros((SWA_PAIRS * SWA_W, 1), F32)
    for p in range(SWA_PAIRS):
        hh = h * SWA_G + 2 * p + second
        col = jnp.where(pair == p, sk_ref[0:1, hh:hh + 1], col)
    return col


def _swa_kv_tiles(cur_ref, prev_ref, h):
    t = jnp.concatenate([prev_ref[...], cur_ref[...]], axis=0)
    lane = lax.broadcasted_iota(jnp.int32, t.shape, 1)
    moved = pltpu.roll(t, SWA_HD, axis=1)
    low, high = (t, moved) if h == 0 else (moved, t)
    return jnp.where(lane < SWA_HD, low, 0.0).astype(BF16), jnp.where(lane >= SWA_HD, high, 0.0).astype(BF16)


def _swa_kv_grad(g_low, g_high, h):
    lane = lax.broadcasted_iota(jnp.int32, g_low.shape, 1)
    if h == 0:
        return jnp.where(lane < SWA_HD, g_low + pltpu.roll(g_high, SWA_HD, axis=1), 0.0)
    return jnp.where(lane >= SWA_HD, pltpu.roll(g_low, SWA_HD, axis=1) + g_high, 0.0)


def _swa_specs(lay):
    w = SWA_W
    q_spec = pl.BlockSpec((w, Q_W), lambda n: (n, _div(lay.q, Q_W)))
    cur = lambda off: pl.BlockSpec((w, KV_W), lambda n: (n, _div(off, KV_W)))
    prev = lambda off: pl.BlockSpec((w, KV_W), lambda n: (jnp.maximum(n - 1, 0), _div(off, KV_W)))
    return q_spec, cur(lay.k), prev(lay.k), cur(lay.v), prev(lay.v)


def swa_fwd(p_all, sinks, lay, name):
    s = p_all.shape[0]
    nb = _div(s, SWA_W)

    def body(q_ref, kc_ref, kp_ref, vc_ref, vp_ref, sk_ref, o_ref):
        first = pl.program_id(0) == 0
        units = [(h, e) for h in range(SWA_HKV) for e in range(2)]
        ks = [_swa_kv_tiles(kc_ref, kp_ref, h) for h in range(SWA_HKV)]
        vs = [_swa_kv_tiles(vc_ref, vp_ref, h) for h in range(SWA_HKV)]
        qs = [_swa_stack(q_ref, h).astype(BF16) for h in range(SWA_HKV)]
        ps, _ = _swa_probs([qs[h] for h, e in units], [ks[h][e] for h, e in units],
                           [_swa_sink_col(sk_ref, h, e) for h, e in units], first)
        os = [_dot(p.astype(BF16), vs[h][e], NN) for p, (h, e) in zip(ps, units)]
        for h in range(SWA_HKV):
            _swa_unstack(o_ref, h, (os[2 * h] + os[2 * h + 1]).astype(BF16))

    q_spec, kc_s, kp_s, vc_s, vp_s = _swa_specs(lay)
    return pl.pallas_call(
        body, name=name, grid=(nb,),
        in_specs=[q_spec, kc_s, kp_s, vc_s, vp_s, pl.BlockSpec(sinks.shape, lambda n: (0, 0))],
        out_specs=pl.BlockSpec((SWA_W, Q_W), lambda n: (n, 0)), out_shape=SDS((s, Q_W), BF16),
        compiler_params=_cp(("parallel",)))(p_all, p_all, p_all, p_all, p_all, sinks)


def swa_bwd(p_all, sinks, dy, lay, name):
    s = p_all.shape[0]
    nb = _div(s, SWA_W)
    w = SWA_W

    def body(q_ref, kc_ref, kp_ref, vc_ref, vp_ref, sk_ref, do_ref, dq_ref, dk_ref, dv_ref, ds_ref, kcar, vcar):
        n = pl.program_id(0)
        first = n == 0

        @pl.when(first)
        def _():
            kcar[...] = jnp.zeros_like(kcar)
            vcar[...] = jnp.zeros_like(vcar)
            ds_ref[...] = jnp.zeros_like(ds_ref)

        @pl.when(n < nb)
        def _():
            lane = lax.broadcasted_iota(jnp.int32, (1, LANE), 1)
            dsink = jnp.zeros((1, LANE), F32)
            units = [(h, e) for h in range(SWA_HKV) for e in range(2)]
            ks = [_swa_kv_tiles(kc_ref, kp_ref, h) for h in range(SWA_HKV)]
            vs = [_swa_kv_tiles(vc_ref, vp_ref, h) for h in range(SWA_HKV)]
            qs = [_swa_stack(q_ref, h).astype(BF16) for h in range(SWA_HKV)]
            dos = [_swa_stack(do_ref, h).astype(BF16) for h in range(SWA_HKV)]
            ps, psinks = _swa_probs([qs[h] for h, e in units], [ks[h][e] for h, e in units],
                                    [_swa_sink_col(sk_ref, h, e) for h, e in units], first)
            dps = [_dot(dos[h], vs[h][e], NT) for h, e in units]
            dvs = [_dot(p.astype(BF16), dos[h], TN) for p, (h, e) in zip(ps, units)]
            rss = [jnp.sum(dp * p, axis=-1, keepdims=True) for dp, p in zip(dps, ps)]
            dsb = [(p * (dp - rs) * (SWA_HD ** -0.5)).astype(BF16) for p, dp, rs in zip(ps, dps, rss)]
            dqs = [_dot(d, ks[h][e], NN) for d, (h, e) in zip(dsb, units)]
            dks = [_dot(d, qs[h], TN) for d, (h, e) in zip(dsb, units)]
            for u, (h, e) in enumerate(units):
                psr = psinks[u] * rss[u]
                for pr in range(SWA_PAIRS):
                    hh = h * SWA_G + 2 * pr + e
                    dsink = dsink + jnp.where(lane == hh, -jnp.sum(psr[pr * w:(pr + 1) * w], axis=0, keepdims=True), 0.0)
            dk_tile = jnp.zeros((2 * w, KV_W), F32)
            dv_tile = jnp.zeros((2 * w, KV_W), F32)
            for h in range(SWA_HKV):
                _swa_unstack(dq_ref, h, (dqs[2 * h] + dqs[2 * h + 1]).astype(BF16))
                dk_tile = dk_tile + _swa_kv_grad(dks[2 * h], dks[2 * h + 1], h)
                dv_tile = dv_tile + _swa_kv_grad(dvs[2 * h], dvs[2 * h + 1], h)
            dk_ref[...] = (kcar[...] + dk_tile[:w]).astype(BF16)
            dv_ref[...] = (vcar[...] + dv_tile[:w]).astype(BF16)
            kcar[...] = dk_tile[w:]
            vcar[...] = dv_tile[w:]
            ds_ref[...] += dsink

        @pl.when(n == nb)
        def _():
            dk_ref[...] = kcar[...].astype(BF16)
            dv_ref[...] = vcar[...].astype(BF16)

    last = nb - 1
    q_spec = pl.BlockSpec((w, Q_W), lambda n: (jnp.minimum(n, last), _div(lay.q, Q_W)))
    cur = lambda off: pl.BlockSpec((w, KV_W), lambda n: (jnp.minimum(n, last), _div(off, KV_W)))
    prev = lambda off: pl.BlockSpec((w, KV_W), lambda n: (jnp.clip(n - 1, 0, last), _div(off, KV_W)))
    row = pl.BlockSpec((w, Q_W), lambda n: (jnp.minimum(n, last), 0))
    kv_out = pl.BlockSpec((w, KV_W), lambda n: (jnp.maximum(n - 1, 0), 0))
    return pl.pallas_call(
        body, name=name, grid=(nb + 1,),
        in_specs=[q_spec, cur(lay.k), prev(lay.k), cur(lay.v), prev(lay.v), pl.BlockSpec(sinks.shape, lambda n: (0, 0)), row],
        out_specs=[row, kv_out, kv_out, pl.BlockSpec((1, LANE), lambda n: (0, 0))],
        out_shape=[SDS((s, Q_W), BF16), SDS((s, KV_W), BF16), SDS((s, KV_W), BF16), SDS((1, LANE), F32)],
        scratch_shapes=[pltpu.VMEM((w, KV_W), F32), pltpu.VMEM((w, KV_W), F32)],
        compiler_params=_cp(("arbitrary",)))(p_all, p_all, p_all, p_all, p_all, sinks, dy)


def _xa_probs(qs, mks):
    ss = [_dot(q, mk, NT) * (XA_D ** -0.5) for q, mk in zip(qs, mks)]
    ps = [jnp.exp(s - jnp.max(s, axis=-1, keepdims=True)) for s in ss]
    inv = [1.0 / jnp.sum(p, axis=-1, keepdims=True) for p in ps]
    return [p * i for p, i in zip(ps, inv)]


def xattn_fwd(p_all, mkv, lay, name):
    s, nm = p_all.shape[0], mkv.shape[0]
    tm = _tile(s, 512)

    def body(q_ref, mkv_ref, o_ref):
        heads = range(XA_H)
        cols = [pl.ds(h * XA_D, XA_D) for h in heads]
        ps = _xa_probs([q_ref[:, c].astype(BF16) for c in cols], [mkv_ref[:, c] for c in cols])
        os = [_dot(ps[h].astype(BF16), mkv_ref[:, pl.ds(XA_W + h * XA_D, XA_D)], NN) for h in heads]
        for h in heads:
            o_ref[:, cols[h]] = os[h].astype(BF16)

    return pl.pallas_call(
        body, name=name, grid=(s // tm,),
        in_specs=[pl.BlockSpec((tm, XA_W), lambda i: (i, _div(lay.qc, XA_W))), pl.BlockSpec((nm, 2 * XA_W), lambda i: (0, 0))],
        out_specs=pl.BlockSpec((tm, XA_W), lambda i: (i, 0)), out_shape=SDS((s, XA_W), BF16),
        compiler_params=_cp(("parallel",)))(p_all, mkv)


def xattn_bwd(p_all, mkv, dy, lay, name):
    s, nm = p_all.shape[0], mkv.shape[0]
    tm = _tile(s, 512)

    def body(q_ref, mkv_ref, do_ref, dq_ref, dmkv_ref):
        @pl.when(pl.program_id(0) == 0)
        def _():
            dmkv_ref[...] = jnp.zeros_like(dmkv_ref)

        heads = range(XA_H)
        cols = [pl.ds(h * XA_D, XA_D) for h in heads]
        vcols = [pl.ds(XA_W + h * XA_D, XA_D) for h in heads]
        qs = [q_ref[:, c].astype(BF16) for c in cols]
        dos = [do_ref[:, c].astype(BF16) for c in cols]
        ps = _xa_probs(qs, [mkv_ref[:, c] for c in cols])
        dps = [_dot(dos[h], mkv_ref[:, vcols[h]], NT) for h in heads]
        dvs = [_dot(ps[h].astype(BF16), dos[h], TN) for h in heads]
        dsb = [(p * (dp - jnp.sum(dp * p, axis=-1, keepdims=True)) * (XA_D ** -0.5)).astype(BF16) for p, dp in zip(ps, dps)]
        dqs = [_dot(dsb[h], mkv_ref[:, cols[h]], NN) for h in heads]
        dks = [_dot(dsb[h], qs[h], TN) for h in heads]
        for h in heads:
            dq_ref[:, cols[h]] = dqs[h].astype(BF16)
            dmkv_ref[:, vcols[h]] += dvs[h]
            dmkv_ref[:, cols[h]] += dks[h]

    row = pl.BlockSpec((tm, XA_W), lambda i: (i, 0))
    full = pl.BlockSpec((nm, 2 * XA_W), lambda i: (0, 0))
    return pl.pallas_call(
        body, name=name, grid=(s // tm,),
        in_specs=[pl.BlockSpec((tm, XA_W), lambda i: (i, _div(lay.qc, XA_W))), full, row],
        out_specs=[row, full], out_shape=[SDS((s, XA_W), BF16), SDS((nm, 2 * XA_W), F32)],
        compiler_params=_cp(("arbitrary",)))(p_all, mkv, dy)


def _shift_down(cur, prev8, s):
    cat = jnp.concatenate([prev8, cur[0:8]], axis=0)
    return pltpu.roll(cur, s, axis=0), pltpu.roll(cat, s, axis=0)[8:16]


def _shift_up(cur, next8, s):
    tm = cur.shape[0]
    cat = jnp.concatenate([cur[tm - 8:tm], next8], axis=0)
    return pltpu.roll(cur, tm - s, axis=0), pltpu.roll(cat, 16 - s, axis=0)[0:8]


def _conv_rows(cur, prev8, w):
    main = w[GDN_CONV - 1:GDN_CONV] * cur
    top = w[GDN_CONV - 1:GDN_CONV] * cur[0:8]
    for sft in range(1, GDN_CONV):
        wi = w[GDN_CONV - 1 - sft:GDN_CONV - sft]
        a, b = _shift_down(cur, prev8, sft)
        main = main + wi * a
        top = top + wi * b
    return jnp.concatenate([top, main[8:]], axis=0)


def _conv_rows_bwd(cur, prev8, d, next8, w):
    tm = cur.shape[0]
    row = lax.broadcasted_iota(jnp.int32, (tm, 1), 0)
    main = w[GDN_CONV - 1:GDN_CONV] * d
    bot = w[GDN_CONV - 1:GDN_CONV] * d[tm - 8:tm]
    dws = [jnp.sum(d * cur, axis=0, keepdims=True)]
    for sft in range(1, GDN_CONV):
        wi = w[GDN_CONV - 1 - sft:GDN_CONV - sft]
        a, b = _shift_up(d, next8, sft)
        main = main + wi * a
        bot = bot + wi * b
        xa, xb = _shift_down(cur, prev8, sft)
        dws.append(jnp.sum(jnp.where(row >= 8, d * xa, 0.0), axis=0, keepdims=True)
                   + jnp.sum(d[0:8] * xb, axis=0, keepdims=True))
    return jnp.concatenate([main[:tm - 8], bot], axis=0), dws


def _gdn_chunk(xq, xk, xv, ab, gp, bdot=_bdot_plain):
    c = GDN_C
    nc = xq.shape[0] // c
    lane = lax.broadcasted_iota(jnp.int32, (c, LANE), 1)
    row = lax.broadcasted_iota(jnp.int32, (c, c), 0)
    col = lax.broadcasted_iota(jnp.int32, (c, c), 1)
    g_tile = -jnp.exp(gp[0:1, :]) * _softplus(ab + gp[1:2, :])
    b_tile = _sigmoid(ab)
    tri = (row >= col).astype(F32)
    qa, ka, va = _silu(xq), _silu(xk), _silu(xv)
    items = []
    for ci in range(nc):
        rs = slice(ci * c, (ci + 1) * c)
        gcum = _dot(tri, g_tile[rs], NN, HI)
        gcum_t = gcum.T
        for h in range(GDN_H):
            hs = slice(h * GDN_D, (h + 1) * GDN_D)
            q, k, v = qa[rs, hs], ka[rs, hs], va[rs, hs]
            q = q * lax.rsqrt(jnp.sum(q * q, axis=-1, keepdims=True) + L2_EPS) * (GDN_D ** -0.5)
            k = k * lax.rsqrt(jnp.sum(k * k, axis=-1, keepdims=True) + L2_EPS)
            gc = jnp.sum(jnp.where(lane == h, gcum, 0.0), axis=1, keepdims=True)
            beta = jnp.sum(jnp.where(lane == GDN_H + h, b_tile[rs], 0.0), axis=1, keepdims=True)
            decay = jnp.exp(jnp.where(row >= col, gc - gcum_t[h:h + 1, :], NEG))
            items.append((q, k, v, gc, beta, decay))
    kks = [bdot(k, k, NT) for (_, k, _, _, _, _) in items]
    xs = tuple(-jnp.where(row > col, it[4] * kk * it[5], 0.0) for it, kk in zip(items, kks))
    nns = _neumann(xs) if bdot is _bdot_plain else _neumann_vjp(xs)
    qks = [bdot(q, k, NT) for (q, k, _, _, _, _) in items]
    out = []
    for (q, k, v, gc, beta, decay), n, qk in zip(items, nns, qks):
        eg = jnp.exp(gc)
        vb = v * beta
        kbe = k * (beta * eg)
        gl = gc[c - 1:c, :]
        out.append((vb + bdot(n, vb, NN), kbe + bdot(n, kbe, NN), q * eg, k * jnp.exp(gl - gc), qk * decay, jnp.exp(gl)))
    return [out[ci * GDN_H:(ci + 1) * GDN_H] for ci in range(nc)]


GDN_CPS = 4


def _gdn_pre_specs(lay, t, tile):
    c0 = _div(lay.qkv, GDN_W)
    cur = [pl.BlockSpec((t, GDN_W), lambda n, j=j: (tile(n), c0 + j)) for j in range(3)]
    prev = [pl.BlockSpec((8, GDN_W), lambda n, j=j: (jnp.maximum(tile(n) * (t // 8) - 1, 0), c0 + j)) for j in range(3)]
    return cur + prev + [pl.BlockSpec((GDN_CONV, 3 * GDN_W), lambda n: (0, 0)),
                         pl.BlockSpec((t, LANE), lambda n: (tile(n), _div(lay.ab, LANE))),
                         pl.BlockSpec((8, LANE), lambda n: (0, 0))]


def _gdn_conv_inputs(x_refs, prev_refs, w_ref, first):
    out = []
    for j in range(3):
        prev8 = jnp.where(first, 0.0, prev_refs[j][...])
        out.append((x_refs[j][...], prev8, w_ref[:, pl.ds(j * GDN_W, GDN_W)]))
    return out


def gdn_pre_fwd(p_all, conv_w, gp, lay, name):
    s = p_all.shape[0]
    c = GDN_C
    n = _div(s, c)
    cps = _tile(n, GDN_CPS)
    t = cps * c

    def body(xq, xk, xv, pq, pk, pv, cw, ab, gp_ref, u_ref, w_ref, qd_ref, kd_ref, qk_ref, gl_ref):
        lane = lax.broadcasted_iota(jnp.int32, (1, LANE), 1)
        xs = [_conv_rows(*a) for a in _gdn_conv_inputs((xq, xk, xv), (pq, pk, pv), cw, pl.program_id(0) == 0)]
        chunks = _gdn_chunk(xs[0], xs[1], xs[2], ab[...], gp_ref[...])
        for ci, heads in enumerate(chunks):
            rs = pl.ds(ci * c, c)
            gl_row = jnp.zeros((1, LANE), F32)
            for h, (u, w, qd, kd, qk, gl) in enumerate(heads):
                hs = pl.ds(h * GDN_D, GDN_D)
                u_ref[rs, hs] = u
                w_ref[rs, hs] = w.astype(BF16)
                qd_ref[rs, hs] = qd.astype(BF16)
                kd_ref[rs, hs] = kd.astype(BF16)
                qk_ref[rs, pl.ds(h * c, c)] = qk.astype(BF16)
                gl_row = gl_row + jnp.where(lane == h, gl, 0.0)
            gl_ref[ci] = gl_row

    row = pl.BlockSpec((t, GDN_W), lambda n: (n, 0))
    return pl.pallas_call(
        body, name=name, grid=(n // cps,), in_specs=_gdn_pre_specs(lay, t, lambda n: n),
        out_specs=[row, row, row, row, pl.BlockSpec((t, GDN_H * c), lambda n: (n, 0)), pl.BlockSpec((cps, 1, LANE), lambda n: (n, 0, 0))],
        out_shape=[SDS((s, GDN_W), F32), SDS((s, GDN_W), BF16), SDS((s, GDN_W), BF16), SDS((s, GDN_W), BF16),
                   SDS((s, GDN_H * c), BF16), SDS((n, 1, LANE), F32)],
        compiler_params=_cp(("parallel",)))(p_all, p_all, p_all, p_all, p_all, p_all, conv_w, p_all, gp)


def gdn_pre_bwd(p_all, conv_w, gp, du, dw, dqd, dkd, dqk, dgl, lay, name):
    s = p_all.shape[0]
    c = GDN_C
    n = _div(s, c)
    cps = _tile(n, GDN_CPS)
    t = cps * c
    steps = n // cps
    chunk = functools.partial(_gdn_chunk, bdot=_bdot_vjp)

    def body(xq, xk, xv, pq, pk, pv, cw, ab, gp_ref, du_r, dw_r, dqd_r, dkd_r, dqk_r, dgl_r,
             dx_ref, dab_ref, dgp_ref, dcw_ref, carry):
        step = pl.program_id(0)

        @pl.when(step == 0)
        def _():
            dgp_ref[...] = jnp.zeros_like(dgp_ref)
            dcw_ref[...] = jnp.zeros_like(dcw_ref)
            carry[...] = jnp.zeros_like(carry)

        lane = lax.broadcasted_iota(jnp.int32, (1, LANE), 1)
        conv_in = _gdn_conv_inputs((xq, xk, xv), (pq, pk, pv), cw, step == steps - 1)
        xs = [_conv_rows(*a) for a in conv_in]
        _, vjp = jax.vjp(chunk, xs[0], xs[1], xs[2], ab[...], gp_ref[...])
        cts = []
        for ci in range(cps):
            rs = pl.ds(ci * c, c)
            heads = []
            for h in range(GDN_H):
                hs = pl.ds(h * GDN_D, GDN_D)
                dgl_h = jnp.sum(jnp.where(lane == h, dgl_r[ci], 0.0), axis=1, keepdims=True)
                heads.append((du_r[rs, hs], dw_r[rs, hs], dqd_r[rs, hs], dkd_r[rs, hs], dqk_r[rs, pl.ds(h * c, c)], dgl_h))
            cts.append(heads)
        *dxs, dab, dgp = vjp(cts)
        for j, (d, (cur, prev8, w)) in enumerate(zip(dxs, conv_in)):
            cols = pl.ds(j * GDN_W, GDN_W)
            dx, dws = _conv_rows_bwd(cur, prev8, d, carry[:, cols], w)
            carry[:, cols] = d[0:8]
            dx_ref[:, cols] = dx.astype(BF16)
            for sft in range(GDN_CONV):
                dcw_ref[GDN_CONV - 1 - sft:GDN_CONV - sft, cols] += dws[sft]
        dab_ref[...] = dab.astype(BF16)
        dgp_ref[...] += dgp

    tile = lambda i: steps - 1 - i
    row = pl.BlockSpec((t, GDN_W), lambda i: (tile(i), 0))
    return pl.pallas_call(
        body, name=name, grid=(steps,),
        in_specs=_gdn_pre_specs(lay, t, tile) + [row, row, row, row, pl.BlockSpec((t, GDN_H * c), lambda i: (tile(i), 0)),
                                                 pl.BlockSpec((cps, 1, LANE), lambda i: (tile(i), 0, 0))],
        out_specs=[pl.BlockSpec((t, 3 * GDN_W), lambda i: (tile(i), 0)), pl.BlockSpec((t, LANE), lambda i: (tile(i), 0)),
                   pl.BlockSpec((8, LANE), lambda i: (0, 0)), pl.BlockSpec((GDN_CONV, 3 * GDN_W), lambda i: (0, 0))],
        out_shape=[SDS((s, 3 * GDN_W), BF16), SDS((s, LANE), BF16), SDS((8, LANE), F32), SDS((GDN_CONV, 3 * GDN_W), F32)],
        scratch_shapes=[pltpu.VMEM((8, 3 * GDN_W), F32)],
        compiler_params=_cp(("arbitrary",)))(p_all, p_all, p_all, p_all, p_all, p_all, conv_w, p_all, gp,
                                             du, dw, dqd, dkd, dqk, dgl)


def _lane_scalar(row, h):
    lane = lax.broadcasted_iota(jnp.int32, row.shape, 1)
    return jnp.sum(jnp.where(lane == h, row, 0.0), axis=1, keepdims=True)


def _gdn_out_head(oh, zh, nw):
    return oh * lax.rsqrt(jnp.mean(oh * oh, axis=-1, keepdims=True) + RMS_EPS) * nw * _silu(zh)


def gdn_scan_fwd(u, w, qd, kd, qk, gl, p_all, nw, lay, name):
    s = u.shape[0]
    c = GDN_C
    n = _div(s, c)
    cps = _tile(n, GDN_CPS)
    t = cps * c

    def body(u_r, w_r, qd_r, kd_r, qk_r, gl_r, z_r, nw_r, o_ref, s_ref, y_ref, st):
        @pl.when(pl.program_id(0) == 0)
        def _():
            st[...] = jnp.zeros_like(st)

        heads = range(GDN_H)
        hs = [pl.ds(h * GDN_D, GDN_D) for h in heads]
        for ci in range(cps):
            rs = pl.ds(ci * c, c)
            s_ref[ci] = st[...]
            sh = [st[hs[h], :] for h in heads]
            shb = [x.astype(BF16) for x in sh]
            ws = [_dot(w_r[rs, hs[h]], shb[h], NN) for h in heads]
            qs = [_dot(qd_r[rs, hs[h]], shb[h], NN) for h in heads]
            vb = [(u_r[rs, hs[h]] - ws[h]).astype(BF16) for h in heads]
            ov = [_dot(qk_r[rs, pl.ds(h * c, c)], vb[h], NN) for h in heads]
            kv = [_dot(kd_r[rs, hs[h]], vb[h], TN) for h in heads]
            os = [qs[h] + ov[h] for h in heads]
            for h in heads:
                o_ref[rs, hs[h]] = os[h]
                y_ref[rs, hs[h]] = _gdn_out_head(os[h], z_r[rs, hs[h]], nw_r[...]).astype(BF16)
                st[hs[h], :] = sh[h] * _lane_scalar(gl_r[ci], h) + kv[h]

    row = pl.BlockSpec((t, GDN_W), lambda i: (i, 0))
    return pl.pallas_call(
        body, name=name, grid=(n // cps,),
        in_specs=[row, row, row, row, pl.BlockSpec((t, GDN_H * c), lambda i: (i, 0)), pl.BlockSpec((cps, 1, LANE), lambda i: (i, 0, 0)),
                  pl.BlockSpec((t, GDN_W), lambda i: (i, _div(lay.z, GDN_W))), pl.BlockSpec((1, GDN_D), lambda i: (0, 0))],
        out_specs=[row, pl.BlockSpec((cps, GDN_W, GDN_D), lambda i: (i, 0, 0)), row],
        out_shape=[SDS((s, GDN_W), F32), SDS((n, GDN_W, GDN_D), F32), SDS((s, GDN_W), BF16)],
        scratch_shapes=[pltpu.VMEM((GDN_W, GDN_D), F32)],
        compiler_params=_cp(("arbitrary",)))(u, w, qd, kd, qk, gl, p_all, nw)


def gdn_scan_bwd(u, w, qd, kd, qk, gl, states, o, dy, p_all, nw, lay, name):
    s = u.shape[0]
    c = GDN_C
    n = _div(s, c)
    cps = _tile(n, GDN_CPS)
    t = cps * c
    steps = n // cps

    def body(u_r, w_r, qd_r, kd_r, qk_r, gl_r, s_r, o_r, dy_r, z_r, nw_r,
             du_o, dw_o, dqd_o, dkd_o, dqk_o, dgl_o, dz_o, dnw_o, dst):
        @pl.when(pl.program_id(0) == 0)
        def _():
            dst[...] = jnp.zeros_like(dst)
            dnw_o[...] = jnp.zeros_like(dnw_o)

        lane = lax.broadcasted_iota(jnp.int32, (1, LANE), 1)
        heads = range(GDN_H)
        hs = [pl.ds(h * GDN_D, GDN_D) for h in heads]
        qs = [pl.ds(h * c, c) for h in heads]
        for ci in reversed(range(cps)):
            rs = pl.ds(ci * c, c)
            outs = [jax.vjp(_gdn_out_head, o_r[rs, hs[h]], z_r[rs, hs[h]], nw_r[...])[1](dy_r[rs, hs[h]].astype(F32))
                    for h in heads]
            for h in heads:
                dz_o[rs, hs[h]] = outs[h][1].astype(BF16)
                dnw_o[...] += outs[h][2]
            sh = [s_r[ci, hs[h], :] for h in heads]
            shb = [x.astype(BF16) for x in sh]
            ds_out = [dst[hs[h], :] for h in heads]
            dsb = [x.astype(BF16) for x in ds_out]
            dob = [outs[h][0].astype(BF16) for h in heads]
            ws = [_dot(w_r[rs, hs[h]], shb[h], NN) for h in heads]
            dv1 = [_dot(qk_r[rs, qs[h]], dob[h], TN) for h in heads]
            dv2 = [_dot(kd_r[rs, hs[h]], dsb[h], NN) for h in heads]
            dqd = [_dot(dob[h], shb[h], NT) for h in heads]
            dsq = [_dot(qd_r[rs, hs[h]], dob[h], TN) for h in heads]
            vb = [(u_r[rs, hs[h]] - ws[h]).astype(BF16) for h in heads]
            dv = [dv1[h] + dv2[h] for h in heads]
            dvb = [x.astype(BF16) for x in dv]
            dw = [_dot(dvb[h], shb[h], NT) for h in heads]
            dkd = [_dot(vb[h], dsb[h], NT) for h in heads]
            dqk = [_dot(dob[h], vb[h], NT) for h in heads]
            dsw = [_dot(w_r[rs, hs[h]], dvb[h], TN) for h in heads]
            dgl_row = jnp.zeros((1, LANE), F32)
            for h in heads:
                du_o[rs, hs[h]] = dv[h]
                dw_o[rs, hs[h]] = -dw[h]
                dqd_o[rs, hs[h]] = dqd[h]
                dkd_o[rs, hs[h]] = dkd[h]
                dqk_o[rs, qs[h]] = dqk[h]
                dgl_row = dgl_row + jnp.where(lane == h, jnp.sum(jnp.sum(ds_out[h] * sh[h], axis=1, keepdims=True), axis=0, keepdims=True), 0.0)
                dst[hs[h], :] = ds_out[h] * _lane_scalar(gl_r[ci], h) + dsq[h] - dsw[h]
            dgl_o[ci] = dgl_row

    rev = lambda i: steps - 1 - i
    row = pl.BlockSpec((t, GDN_W), lambda i: (rev(i), 0))
    qks = pl.BlockSpec((t, GDN_H * c), lambda i: (rev(i), 0))
    gls = pl.BlockSpec((cps, 1, LANE), lambda i: (rev(i), 0, 0))
    zs = pl.BlockSpec((t, GDN_W), lambda i: (rev(i), _div(lay.z, GDN_W)))
    nws = pl.BlockSpec((1, GDN_D), lambda i: (0, 0))
    return pl.pallas_call(
        body, name=name, grid=(steps,),
        in_specs=[row, row, row, row, qks, gls, pl.BlockSpec((cps, GDN_W, GDN_D), lambda i: (rev(i), 0, 0)), row, row, zs, nws],
        out_specs=[row, row, row, row, qks, gls, row, nws],
        out_shape=[SDS((s, GDN_W), F32)] * 4 + [SDS((s, GDN_H * c), F32), SDS((n, 1, LANE), F32), SDS((s, GDN_W), BF16),
                                                SDS((1, GDN_D), F32)],
        scratch_shapes=[pltpu.VMEM((GDN_W, GDN_D), F32)],
        compiler_params=_cp(("arbitrary",)))(u, w, qd, kd, qk, gl, states, o, dy, p_all, nw)


def _cols_to_full(g):
    n, k, c = g.shape
    return g.transpose(1, 0, 2).reshape(k, n * c)


def _rows_to_blocks(w):
    return w.reshape(N_DEV, w.shape[0] // N_DEV, w.shape[1])


def _pack_small(parts, rows):
    flat = jnp.concatenate([jnp.pad(p.reshape(-1), (0, -p.size % LANE)) for p in parts])
    return jnp.pad(flat, (0, rows * LANE - flat.size)).reshape(rows, LANE)


def kernel(x, mem, g_mix, w_in, sinks, conv_w, a_log, dt_bias, gdn_norm_w, g_mem, w_mem_kv, w_swa_up, w_gdn_up, w_xa_up, w_out, g_mlp, w_mlp_in, w_mlp_out, g_final, loss_target, m_g_mix, m_w_in, m_sinks, m_conv_w, m_a_log, m_dt_bias, m_gdn_norm_w, m_g_mem, m_w_mem_kv, m_w_swa_up, m_w_gdn_up, m_w_xa_up, m_w_out, m_g_mlp, m_w_mlp_in, m_w_mlp_out, m_g_final, v_g_mix, v_w_in, v_sinks, v_conv_w, v_a_log, v_dt_bias, v_gdn_norm_w, v_g_mem, v_w_mem_kv, v_w_swa_up, v_w_gdn_up, v_w_xa_up, v_w_out, v_g_mlp, v_w_mlp_in, v_w_mlp_out, v_g_final):
    xs, ms, tgt = x[0], mem[0], loss_target[0]
    s, d = xs.shape
    lay = Layout(d)
    px, py, pc = _position()
    dev = 4 * px + 2 * py + pc

    n1, g_in, g_conv = rmsnorm_fwd(xs, g_mix, "norm_mix", side=GatherJob([w_in[0].astype(BF16), conv_w[0]]))
    W_in = pad_w_in(g_in, lay)
    convw = _cols_to_full(g_conv)
    gp = jnp.zeros((8, LANE), F32).at[0, :GDN_H].set(a_log[0]).at[1, :GDN_H].set(dt_bias[0])
    later = [w_mem_kv[0], w_swa_up[0], w_gdn_up[0], w_xa_up[0], w_out[0], w_mlp_in[0]]

    p_all, g_mkv, W_sup, W_gup, W_xup, g_out, W_m1 = matmul(
        n1, W_in, mode="nn", out_dtype=F32, name="proj_in", tm=2048, tn=1024, tk=d,
        side=GatherJob([w.astype(BF16) for w in later]))
    W_mkv = g_mkv.reshape(-1, g_mkv.shape[2])
    W_out = g_out.reshape(-1, d)
    y_a = swa_fwd(p_all, sinks, lay, "swa_fwd")
    u, gw, gqd, gkd, gqk, ggl = gdn_pre_fwd(p_all, convw, gp, lay, "gdn_pre_fwd")
    o_b, states, y_b = gdn_scan_fwd(u, gw, gqd, gkd, gqk, ggl, p_all, gdn_norm_w, lay, "gdn_scan_fwd")
    nm = rmsnorm_fwd(ms, g_mem, "norm_mem")
    mkv = matmul(nm, W_mkv, mode="nn", out_dtype=BF16, name="proj_mem", tk=d)
    y_c = xattn_fwd(p_all, mkv, lay, "xattn_fwd")
    merged = merge(p_all, y_a, y_b, y_c, W_sup, W_gup, W_xup, None, lay, "merge_fwd")
    h1, n2 = matmul(merged, W_out, mode="nn", out_dtype=F32, name="proj_out", tm=512, tn=d, tk=d, resid=xs, rms_gain=g_mlp)
    uu, act, g_m2 = matmul(n2, W_m1, mode="nn", out_dtype=F32, name="mlp_in", tm=2048, tn=512, tk=d, b_cols=True,
                           relu2_out=True, side=GatherJob([w_mlp_out[0].astype(BF16)]))
    W_m2 = g_m2.reshape(-1, d)
    dh2, dh2_b, dg_final, lrow = matmul(act, W_m2, mode="nn", out_dtype=F32, name="mlp_out_loss", tm=512, tn=d, tk=1024,
                                        resid=h1, rms_gain=g_final.reshape(1, d), loss_target=tgt)
    loss = lax.psum(lrow[0, 0], ("x", "y", "c"))

    du = matmul(dh2_b, W_m2, mode="nt", out_dtype=BF16, name="mlp_out_dx", tm=2048, tn=512, tk=d, relu2_grad_of=uu)
    dW_m2 = matmul(act, dh2_b, mode="tn", out_dtype=BF16, name="mlp_out_dw", tm=1024, tn=2048, tk=1024)
    dW_m2 = _rows_to_blocks(dW_m2)
    dn2, sib_m2 = matmul(du, W_m1, mode="nt", out_dtype=F32, name="mlp_in_dx", tm=1024, tn=2048, tk=1024, b_cols=True,
                         side=PairExchangeJob([dW_m2], [False]))
    c_m2 = pair_add(dW_m2, False, sib_m2, "grads_pair_add_m2")
    dW_m1 = matmul(n2, du, mode="tn", out_dtype=BF16, name="mlp_in_dw", tm=2048, tn=1024, tk=1024)
    dh1, dg_mlp, dh1_b = rmsnorm_bwd(h1, g_mlp, dn2, dh2, "norm_mlp_bwd", bf16_copy=True)

    dmerged, sib_m1 = matmul(dh1_b, W_out, mode="nt", out_dtype=F32, name="proj_out_dx", tm=2048, tn=512, tk=d,
                             side=PairExchangeJob([dW_m1], [True]))
    c_m1 = pair_add(dW_m1, True, sib_m1, "grads_pair_add_m1")
    dW_out = matmul(merged, dh1_b, mode="tn", out_dtype=BF16, name="proj_out_dw", tm=2048, tn=1024, tk=1024)
    dgates, dta, dtb, dtc = merge(p_all, y_a, y_b, y_c, W_sup, W_gup, W_xup, dmerged, lay, "merge_bwd")
    dy_a = matmul(dta, W_sup, mode="nt", out_dtype=BF16, name="swa_up_dx", tm=2048, tk=d, b_cols=True)
    dy_b = matmul(dtb, W_gup, mode="nt", out_dtype=BF16, name="gdn_up_dx", tm=2048, tk=d, b_cols=True)
    dy_c = matmul(dtc, W_xup, mode="nt", out_dtype=BF16, name="xa_up_dx", tm=2048, tk=d, b_cols=True)
    dW_sup = matmul(y_a, dta, mode="tn", out_dtype=BF16, name="swa_up_dw", tn=2048, tk=2048)
    dW_gup = matmul(y_b, dtb, mode="tn", out_dtype=BF16, name="gdn_up_dw", tn=2048, tk=2048)
    dW_xup = matmul(y_c, dtc, mode="tn", out_dtype=BF16, name="xa_up_dw", tn=2048, tk=2048)

    dq_a, dk_a, dv_a, dsinks = swa_bwd(p_all, sinks, dy_a, lay, "swa_bwd")
    dq_c, dmkv = xattn_bwd(p_all, mkv, dy_c, lay, "xattn_bwd")
    dW_mkv = matmul(nm, dmkv, mode="tn", out_dtype=BF16, name="proj_mem_dw", tk=256)
    dnm = matmul(dmkv, W_mkv, mode="nt", out_dtype=F32, name="proj_mem_dx", tk=1024)
    _, dg_mem = rmsnorm_bwd(ms, g_mem, dnm, None, "norm_mem_bwd")

    du_g, dw_g, dqd_g, dkd_g, dqk_g, dgl_g, dz, dnorm_w = gdn_scan_bwd(
        u, gw, gqd, gkd, gqk, ggl, states, o_b, dy_b, p_all, gdn_norm_w, lay, "gdn_scan_bwd")
    dqkv, dab, dgp, dconv = gdn_pre_bwd(p_all, convw, gp, du_g, dw_g, dqd_g, dkd_g, dqk_g, dgl_g, lay, "gdn_pre_bwd")

    drest = jnp.concatenate([dq_a, dqkv, dz, dq_c, dk_a, dv_a, dab, jnp.zeros((s, lay.pw - lay.end), BF16)], axis=1)
    def pair_stage(grads, cols, tag):
        from_sib = run_job(PairExchangeJob(grads, cols), "grads_pair_exchange_" + tag)
        return [pair_add(g, cl, o, "grads_pair_add_%s%d" % (tag, i)) for i, (g, cl, o) in enumerate(zip(grads, cols, from_sib))]

    small = pair_stage([_rows_to_blocks(dW_mkv), dW_sup, dW_gup, dW_xup, _rows_to_blocks(dW_out)],
                       [False, True, True, True, False], "a")
    dW_in, p_m1, p_m2 = matmul(n1, dgates, tail=drest, mode="tn", out_dtype=BF16, name="proj_in_dw", tm=2048, tn=1024, tk=1024,
                               side=ChipExchangeJob([c_m1, c_m2]))
    late = pair_stage([unpad_dw_in(dW_in, lay)], [False], "b")
    dn1, p_in, p_mkv, p_sup, p_gup, p_xup, p_out = matmul(
        dgates, W_in, tail=drest, mode="nt", out_dtype=F32, name="proj_in_dx", tm=1024, tn=2048, tk=1024,
        side=ChipExchangeJob(late + small))
    grad_x, dg_mix = rmsnorm_bwd(xs, g_mix, dn1, dh1, "norm_mix_bwd")
    parts = [p_in, p_mkv, p_sup, p_gup, p_xup, p_out, p_m1, p_m2]

    shard_names = [(w_in, m_w_in, v_w_in), (w_mem_kv, m_w_mem_kv, v_w_mem_kv), (w_swa_up, m_w_swa_up, v_w_swa_up),
                   (w_gdn_up, m_w_gdn_up, v_w_gdn_up), (w_xa_up, m_w_xa_up, v_w_xa_up), (w_out, m_w_out, v_w_out),
                   (w_mlp_in, m_w_mlp_in, v_w_mlp_in), (w_mlp_out, m_w_mlp_out, v_w_mlp_out)]
    big_res = [adamw(p, w[0], m[0], v[0], "adamw_%d" % i) for i, (p, (w, m, v)) in enumerate(zip(parts, shard_names))]

    smalls = [(g_mix, m_g_mix, v_g_mix, dg_mix), (sinks, m_sinks, v_sinks, dsinks[:, :SWA_HQ]),
              (a_log, m_a_log, v_a_log, dgp[0:1, :GDN_H]), (dt_bias, m_dt_bias, v_dt_bias, dgp[1:2, :GDN_H]),
              (gdn_norm_w, m_gdn_norm_w, v_gdn_norm_w, dnorm_w), (g_mem, m_g_mem, v_g_mem, dg_mem),
              (g_mlp, m_g_mlp, v_g_mlp, dg_mlp), (g_final, m_g_final, v_g_final, dg_final)]
    sizes = [-(-t[0].size // LANE) * LANE for t in smalls] + [GDN_CONV * 3 * GDN_W]
    rows = -(-sum(sizes) // (8 * LANE)) * 8
    csh = conv_w.shape[2]

    def conv_place(a):
        full = jnp.tile(a[0], (1, N_DEV))
        owner = lax.broadcasted_iota(jnp.int32, full.shape, 1) // csh
        return jnp.where(owner == dev, full, 0.0)

    g_pack = _pack_small([t[3] for t in smalls] + [dconv], rows)
    w_pack = _pack_small([t[0] for t in smalls] + [conv_place(conv_w)], rows)
    m_pack = _pack_small([t[1] for t in smalls] + [conv_place(m_conv_w)], rows)
    v_pack = _pack_small([t[2] for t in smalls] + [conv_place(v_conv_w)], rows)
    g_all = run_job(GatherJob([g_pack]), "gather_small_grads")[0]
    small_res = adamw(g_all, w_pack, m_pack, v_pack, "adamw_small")

    def unpack(arr):
        flat = arr.reshape(-1)
        outs, off = [], 0
        for t, sz in zip(smalls, sizes[:-1]):
            outs.append(flat[off:off + t[0].size].reshape(t[0].shape))
            off += sz
        cw = flat[off:off + sizes[-1]].reshape(GDN_CONV, 3 * GDN_W)
        mine = (lax.broadcasted_iota(jnp.int32, (1, N_DEV, 1), 1) == dev).astype(F32)
        outs.append(jnp.sum(cw.reshape(GDN_CONV, N_DEV, csh) * mine, axis=1)[None])
        return outs

    sg, sd, sm, sv = (unpack(a) for a in small_res)
    bg, bd, bm, bv = ([r[i][None] for r in big_res] for i in range(4))

    def ordered(sm_, bg_):
        return [sm_[0], bg_[0], sm_[1], sm_[8], sm_[2], sm_[3], sm_[4], sm_[5], bg_[1], bg_[2], bg_[3], bg_[4], bg_[5],
                sm_[6], bg_[6], bg_[7], sm_[7]]

    return (loss, grad_x[None], *ordered(sg, bg), *ordered(sd, bd), *ordered(sm, bm), *ordered(sv, bv))
```

```python
import functools

import jax
import jax.numpy as jnp
from jax import lax
from jax.experimental import pallas as pl
from jax.experimental.pallas import tpu as pltpu

F32, BF16 = jnp.float32, jnp.bfloat16
SDS = jax.ShapeDtypeStruct
MESH = pl.DeviceIdType.MESH
ANY = pl.BlockSpec(memory_space=pl.ANY)

SWA_HQ, SWA_HKV, SWA_HD, SWA_W = 16, 2, 64, 128
SWA_G = SWA_HQ // SWA_HKV
GDN_H, GDN_D, GDN_CONV, GDN_C = 4, 128, 4, 64
XA_H, XA_D = 4, 128
Q_W = SWA_HQ * SWA_HD
KV_W = SWA_HKV * SWA_HD
GDN_W = GDN_H * GDN_D
XA_W = XA_H * XA_D
RMS_EPS = 1e-6
L2_EPS = 1e-6
NEG = -1e30
N_DEV = 8
LANE = 128

ADAM_LR, ADAM_B1, ADAM_B2, ADAM_EPS, ADAM_WD, ADAM_STEP = 0.001, 0.9, 0.999, 1e-08, 0.01, 10

VMEM_BIG = 56 * 1024 * 1024


def _cp(sem, vmem=VMEM_BIG):
    return pltpu.CompilerParams(dimension_semantics=sem, vmem_limit_bytes=vmem)


def _div(a, b):
    assert a % b == 0, (a, b)
    return a // b


def _tile(n, t):
    t = min(t, n)
    assert n % t == 0, (n, t)
    return t


def _sigmoid(x):
    return jax.nn.sigmoid(x)


def _silu(x):
    return x * _sigmoid(x)


def _softplus(x):
    return jnp.maximum(x, 0.0) + jnp.log1p(jnp.exp(-jnp.abs(x)))


def _dot(a, b, dims, prec=None):
    return lax.dot_general(a, b, (dims, ((), ())), precision=prec, preferred_element_type=F32)


NN = ((1,), (0,))
NT = ((1,), (1,))
TN = ((0,), (0,))
HI = lax.Precision.HIGHEST


def _bdot_plain(a, b, dims):
    return _dot(a.astype(BF16), b.astype(BF16), dims)


@functools.partial(jax.custom_vjp, nondiff_argnums=(2,))
def _bdot_vjp(a, b, dims):
    return _bdot_plain(a, b, dims)


def _bdot_vjp_fwd(a, b, dims):
    return _bdot_plain(a, b, dims), (a, b)


def _bdot_vjp_bwd(dims, res, ct):
    a, b = res
    if dims == NN:
        return _bdot_plain(ct, b, NT), _bdot_plain(a, ct, TN)
    assert dims == NT, dims
    return _bdot_plain(ct, b, NN), _bdot_plain(ct, a, TN)


_bdot_vjp.defvjp(_bdot_vjp_fwd, _bdot_vjp_bwd)


def _neumann(xs):
    pws, nns = list(xs), list(xs)
    for _ in range(5):
        pws = [_bdot_plain(p, p, NN) for p in pws]
        nns = [n + p + _bdot_plain(n, p, NN) for n, p in zip(nns, pws)]
    return tuple(nns)


@jax.custom_vjp
def _neumann_vjp(xs):
    return _neumann(xs)


def _neumann_vjp_fwd(xs):
    nns = _neumann(xs)
    return nns, nns


def _neumann_vjp_bwd(nns, cts):
    ts = [ct + _bdot_plain(nn, ct, TN) for nn, ct in zip(nns, cts)]
    return (tuple(t + _bdot_plain(t, nn, NT) for t, nn in zip(ts, nns)),)


_neumann_vjp.defvjp(_neumann_vjp_fwd, _neumann_vjp_bwd)


class Layout:
    def __init__(self, d):
        self.d = d
        self.g = 0
        self.q = 3 * d
        self.qkv = self.q + Q_W
        self.z = self.qkv + 3 * GDN_W
        self.qc = self.z + GDN_W
        self.k = self.qc + XA_W
        self.v = self.k + KV_W
        self.ab = self.v + KV_W
        self.end = self.ab + LANE
        self.pw = -(-self.end // 1024) * 1024
        self.lq, self.lk, self.lv, self.lqkv = 0, Q_W, Q_W + KV_W, Q_W + 2 * KV_W
        self.la = self.lqkv + 3 * GDN_W
        self.lz = self.la + 2 * GDN_H
        self.lqc = self.lz + GDN_W
        self.lg = self.lqc + XA_W
        self.lw = self.lg + 3 * d

    def pieces(self):
        segs = [(self.lq, self.lk, self.q), (self.lk, self.lv, self.k), (self.lv, self.lqkv, self.v),
                (self.lqkv, self.la, self.qkv), (self.la, self.lz, self.ab), (self.lz, self.lqc, self.z),
                (self.lqc, self.lg, self.qc), (self.lg, self.lw, self.g)]
        cw = _div(self.lw, N_DEV)
        out = []
        for dev in range(N_DEV):
            lo, hi = dev * cw, (dev + 1) * cw
            for ls, le, ps in segs:
                s, e = max(lo, ls), min(hi, le)
                if s < e:
                    out.append((dev, s - lo, ps + s - ls, e - s))
        return out


def pad_w_in(g, lay):
    nd, k, cw = g.shape
    tr = _tile(k, 256)
    tail = lay.ab + 2 * GDN_H

    def body(g_ref, o_ref):
        o_ref[:, pl.ds(tail, lay.pw - tail)] = jnp.zeros((tr, lay.pw - tail), o_ref.dtype)
        for dev, so, po, ln in lay.pieces():
            o_ref[:, pl.ds(po, ln)] = g_ref[dev, :, pl.ds(so, ln)]

    return pl.pallas_call(
        body, name="pad_w_in", grid=(k // tr,), in_specs=[pl.BlockSpec((nd, tr, cw), lambda i: (0, i, 0))],
        out_specs=pl.BlockSpec((tr, lay.pw), lambda i: (i, 0)), out_shape=SDS((k, lay.pw), g.dtype),
        compiler_params=_cp(("parallel",)))(g)


def unpad_dw_in(dw, lay):
    k = dw.shape[0]
    cw = _div(lay.lw, N_DEV)
    tr = _tile(k, 256)

    def body(d_ref, o_ref):
        for dev, so, po, ln in lay.pieces():
            o_ref[dev, :, pl.ds(so, ln)] = d_ref[:, pl.ds(po, ln)]

    return pl.pallas_call(
        body, name="unpad_dw_in", grid=(k // tr,), in_specs=[pl.BlockSpec((tr, lay.pw), lambda i: (i, 0))],
        out_specs=pl.BlockSpec((N_DEV, tr, cw), lambda i: (0, i, 0)), out_shape=SDS((N_DEV, k, cw), dw.dtype),
        compiler_params=_cp(("parallel",)))(dw)


def _position():
    return lax.axis_index("x"), lax.axis_index("y"), lax.axis_index("c")


class GatherJob:
    def __init__(self, arrs):
        self.ins = list(arrs)
        n = len(arrs)
        self.out_shapes = [SDS((N_DEV,) + a.shape, a.dtype) for a in arrs]
        self.scratch = [pltpu.SemaphoreType.DMA((n, 7)), pltpu.SemaphoreType.DMA((n, 7)), pltpu.SemaphoreType.DMA((n,))]

    def _ctx(self, outs, sems):
        send_sems, recv_sems, _ = sems
        x, y, c = _position()

        def blk(o, p):
            return o.at[4 * p[0] + 2 * p[1] + p[2]]

        def copy(i, k, block, to, src=None):
            return pltpu.make_async_remote_copy(
                src_ref=blk(outs[i], block) if src is None else src, dst_ref=blk(outs[i], block),
                send_sem=send_sems.at[i, k], recv_sem=recv_sems.at[i, k], device_id=to, device_id_type=MESH)

        return (x, y, c), (x, y, 1 - c), [(1 - x, y), (x, 1 - y), (1 - x, 1 - y)], blk, copy

    def start(self, ins, outs, sems):
        me, sibling, chips, blk, copy = self._ctx(outs, sems)
        for i in range(len(ins)):
            pltpu.make_async_copy(ins[i], blk(outs[i], me), sems[2].at[i]).start()
            copy(i, 0, me, sibling, src=ins[i]).start()
            for j, chip in enumerate(chips[:2]):
                copy(i, 1 + j, me, (*chip, me[2]), src=ins[i]).start()

    def _relay(self, i, outs, sems, onward):
        me, _, _, blk, copy = self._ctx(outs, sems)
        x, y, c = me
        origin = (x + (1 - c) - 2 * x * (1 - c), y + c - 2 * y * c, c)
        if not onward:
            return copy(i, 1 + c, origin, me)
        return copy(i, 3, origin, (x + c - 2 * x * c, y + (1 - c) - 2 * y * (1 - c), c))

    def relay(self, ins, outs, sems):
        for i in range(len(ins)):
            self._relay(i, outs, sems, False).wait_recv()
            self._relay(i, outs, sems, True).start()

    def mid(self, ins, outs, sems):
        me, sibling, chips, blk, copy = self._ctx(outs, sems)
        for i in range(len(ins)):
            for j, chip in enumerate(chips):
                arrival = copy(i, 1 + j, (*chip, me[2]), me)
                if j < 2:
                    pl.when(me[2] != j)(arrival.wait_recv)
                else:
                    arrival.wait_recv()
                copy(i, 4 + j, (*chip, me[2]), sibling).start()

    def finish(self, ins, outs, sems):
        me, sibling, chips, blk, copy = self._ctx(outs, sems)
        for i in range(len(ins)):
            copy(i, 0, sibling, me).wait_recv()
            for j, chip in enumerate(chips):
                copy(i, 4 + j, (*chip, 1 - me[2]), me).wait_recv()
        for i in range(len(ins)):
            pltpu.make_async_copy(ins[i], blk(outs[i], me), sems[2].at[i]).wait()
            copy(i, 0, me, sibling, src=ins[i]).wait_send()
            for j, chip in enumerate(chips[:2]):
                copy(i, 1 + j, me, (*chip, me[2]), src=ins[i]).wait_send()
            self._relay(i, outs, sems, True).wait_send()
            for j, chip in enumerate(chips):
                copy(i, 4 + j, (*chip, me[2]), sibling).wait_send()


class ChipExchangeJob:
    mid = None

    def __init__(self, arrs):
        self.ins = list(arrs)
        n = len(arrs)
        self.out_shapes = [SDS(a.shape, a.dtype) for a in arrs]
        self.scratch = [pltpu.SemaphoreType.DMA((n, 3)), pltpu.SemaphoreType.DMA((n, 3)), pltpu.SemaphoreType.DMA((n,))]

    def _copies(self, ins, outs, sems, i, arrivals):
        send_sems, recv_sems, local_sems = sems
        x, y, c = _position()
        my_chip = 2 * x + y
        chips = [(1 - x, y), (x, 1 - y), (1 - x, 1 - y)]
        if arrivals:
            return [pltpu.make_async_remote_copy(
                src_ref=ins[i].at[my_chip], dst_ref=outs[i].at[2 * px + py], send_sem=send_sems.at[i, k],
                recv_sem=recv_sems.at[i, k], device_id=(px, py, c), device_id_type=MESH) for k, (px, py) in enumerate(chips)]
        local = pltpu.make_async_copy(ins[i].at[my_chip], outs[i].at[my_chip], local_sems.at[i])
        return local, [pltpu.make_async_remote_copy(
            src_ref=ins[i].at[2 * px + py], dst_ref=outs[i].at[my_chip], send_sem=send_sems.at[i, k],
            recv_sem=recv_sems.at[i, k], device_id=(px, py, c), device_id_type=MESH) for k, (px, py) in enumerate(chips)]

    def start(self, ins, outs, sems):
        for i in range(len(ins)):
            local, remote = self._copies(ins, outs, sems, i, False)
            local.start()
            for cp in remote:
                cp.start()

    def finish(self, ins, outs, sems):
        for i in range(len(ins)):
            for cp in self._copies(ins, outs, sems, i, True):
                cp.wait_recv()
            local, remote = self._copies(ins, outs, sems, i, False)
            for cp in remote:
                cp.wait_send()
            local.wait()


def _slab_shape(g, cols):
    return (g.shape[0], _div(g.shape[1], N_DEV)) if cols else g.shape[1:]


class PairExchangeJob:
    mid = None

    def __init__(self, grads, cols):
        self.ins, self.cols = list(grads), list(cols)
        n = len(grads)
        self.out_shapes = [SDS((4,) + _slab_shape(g, cl), g.dtype) for g, cl in zip(grads, cols)]
        self.scratch = [pltpu.SemaphoreType.DMA((n, 4)), pltpu.SemaphoreType.DMA((n, 4))]

    def _copies(self, ins, outs, sems):
        send_sems, recv_sems = sems
        x, y, c = _position()

        def part(i, dst):
            if not self.cols[i]:
                return ins[i].at[dst]
            cw = _slab_shape(self.ins[i], True)[1]
            return ins[i].at[:, pl.ds(pl.multiple_of(dst * cw, LANE), cw)]

        return [pltpu.make_async_remote_copy(src_ref=part(i, 2 * j + 1 - c), dst_ref=outs[i].at[j], send_sem=send_sems.at[i, j],
                                             recv_sem=recv_sems.at[i, j], device_id=(x, y, 1 - c), device_id_type=MESH)
                for i in range(len(ins)) for j in range(4)]

    def start(self, ins, outs, sems):
        for cp in self._copies(ins, outs, sems):
            cp.start()

    def finish(self, ins, outs, sems):
        for cp in self._copies(ins, outs, sems):
            cp.wait()


def _host_begin(job, step, steps, ins, outs, sems):
    pl.when(step == 0)(lambda: job.start(ins, outs, sems))
    if job.mid is not None:
        pl.when(step == (steps * 45) // 100)(lambda: job.relay(ins, outs, sems))
        pl.when(step == (steps * 85) // 100)(lambda: job.mid(ins, outs, sems))


def run_job(job, name):
    n = len(job.ins)

    def body(*refs):
        ins, outs, sems = refs[:n], refs[n:2 * n], refs[2 * n:]
        job.start(ins, outs, sems)
        if job.mid is not None:
            job.relay(ins, outs, sems)
            job.mid(ins, outs, sems)
        job.finish(ins, outs, sems)

    return pl.pallas_call(body, name=name, out_shape=job.out_shapes, in_specs=[ANY] * n, out_specs=[ANY] * n,
                          scratch_shapes=job.scratch)(*job.ins)


def pair_add(grad, cols, other, name):
    r, c = _slab_shape(grad, cols)
    tr = _tile(r, 256)
    parity = lax.axis_index("c").astype(jnp.int32).reshape(1)

    def body(par_ref, a_ref, b_ref, o_ref):
        o_ref[...] = (a_ref[...].astype(F32) + b_ref[...].astype(F32)).astype(BF16)

    spec = pl.BlockSpec((None, tr, c), lambda j, i, par: (j, i, 0))
    if cols:
        own = pl.BlockSpec((tr, c), lambda j, i, par: (i, 2 * j + par[0]))
    else:
        own = pl.BlockSpec((None, tr, c), lambda j, i, par: (2 * j + par[0], i, 0))
    return pl.pallas_call(
        body, name=name, out_shape=SDS(other.shape, BF16),
        grid_spec=pltpu.PrefetchScalarGridSpec(num_scalar_prefetch=1, grid=(4, r // tr), in_specs=[own, spec], out_specs=spec),
        compiler_params=_cp(("parallel", "parallel")))(parity, grad, other)


def adamw(parts, w, m, v, name):
    p, r, c = parts.shape
    tr = _tile(r, 128 if c > 1024 else 256)

    def body(p_ref, w_ref, m_ref, v_ref, g_out, d_out, m_out, v_out):
        g = p_ref[0].astype(F32)
        for j in range(1, p):
            g = g + p_ref[j].astype(F32)
        mn = ADAM_B1 * m_ref[...] + (1.0 - ADAM_B1) * g
        vn = ADAM_B2 * v_ref[...] + (1.0 - ADAM_B2) * jnp.square(g)
        m_hat = mn / (1.0 - ADAM_B1 ** ADAM_STEP)
        v_hat = vn / (1.0 - ADAM_B2 ** ADAM_STEP)
        g_out[...] = g
        d_out[...] = -ADAM_LR * (m_hat / (jnp.sqrt(v_hat) + ADAM_EPS) + ADAM_WD * w_ref[...])
        m_out[...] = mn
        v_out[...] = vn

    spec = pl.BlockSpec((tr, c), lambda i: (i, 0))
    return pl.pallas_call(
        body, name=name, grid=(r // tr,),
        in_specs=[pl.BlockSpec((p, tr, c), lambda i: (0, i, 0)), spec, spec, spec],
        out_specs=[spec] * 4, out_shape=[SDS((r, c), F32)] * 4, compiler_params=_cp(("parallel",)))(parts, w, m, v)


def matmul(a, b, *, mode, out_dtype, name, tm=1024, tn=1024, tk=512, a_relu2=False, resid=None, relu2_grad_of=None,
           b_cols=False, relu2_out=False, rms_gain=None, loss_target=None, side=None, tail=None):
    if b_cols:
        nb, brows, bc = b.shape
        bshape = (brows, nb * bc)
    else:
        bshape = b.shape
    head_k = head_n = None
    if mode == "nn":
        (m, k), (k2, n) = a.shape, bshape
    elif mode == "nt":
        (m, k), (n, k2) = a.shape, bshape
        if tail is not None:
            head_k, k = k, k + tail.shape[1]
    else:
        (k, m), (k2, n) = a.shape, bshape
        if tail is not None:
            head_n, n = n, n + tail.shape[1]
    assert k == k2 and (tail is None or mode != "nn"), (a.shape, b.shape, mode)
    b_whole = b_cols and mode == "nt" and tk >= k
    tm, tn, tk = _tile(m, tm), _tile(n, tn), _tile(k, tk)
    if b_cols and mode == "nn":
        tn = _tile(bc, tn)
    if b_cols and mode == "nt" and not b_whole:
        tk = _tile(bc, tk)
    nk = k // tk
    ni, nj = m // tm, n // tn
    nk_head = _div(head_k, tk) if head_k is not None else None
    nj_head = _div(head_n, tn) if head_n is not None else None
    use_acc = nk > 1 or tail is not None
    dims = {"nn": NN, "nt": NT, "tn": TN}[mode]
    extras = [e for e in (resid, relu2_grad_of) if e is not None]
    tails = [tail] if tail is not None else []
    n_side = len(side.ins) if side is not None else 0
    loss = loss_target is not None
    if loss:
        extras.append(loss_target)
    gains = [rms_gain] if rms_gain is not None else []
    assert not gains or (tn == n and not relu2_out), (tn, n)
    assert not loss or (gains and resid is not None and relu2_grad_of is None and out_dtype == F32)
    n_main = 2 if (relu2_out or gains) else 1
    n_loss = 2 if loss else 0

    def body(*refs):
        a_ref, b_ref = refs[:2]
        t_ref = refs[2] if tails else None
        n_op = 2 + len(tails)
        e_refs = refs[n_op:n_op + len(extras)]
        n_pre = n_op + len(extras) + len(gains)
        g_ref = refs[n_pre - 1] if gains else None
        n_in = n_pre + n_side
        o_ref = refs[n_in]
        act_ref = refs[n_in + 1] if n_main == 2 else None
        dg_ref, l_ref = refs[n_in + n_main:n_in + n_main + n_loss] if loss else (None, None)
        acc_ref = refs[n_in + n_main + n_loss + n_side] if use_acc else None
        if side is not None:
            s_ins = refs[n_pre:n_in]
            s_outs = refs[n_in + n_main + n_loss:n_in + n_main + n_loss + n_side]
            s_sems = refs[len(refs) - len(side.scratch):]
            step = (pl.program_id(0) * nj + pl.program_id(1)) * nk + pl.program_id(2)
            _host_begin(side, step, ni * nj * nk, s_ins, s_outs, s_sems)

        def operands(a_from=a_ref, b_from=b_ref):
            av = a_from[...]
            if a_relu2:
                av = jnp.square(jnp.maximum(av.astype(F32), 0.0))
            return av.astype(BF16), b_from[...].astype(BF16)

        def finish(r):
            e = list(e_refs)
            if resid is not None:
                r = r + e.pop(0)[...]
            if relu2_grad_of is not None:
                r = r * (2.0 * jnp.maximum(e.pop(0)[...], 0.0))
            if loss:
                gv = g_ref[...]
                inv = lax.rsqrt(jnp.mean(r * r, axis=-1, keepdims=True) + RMS_EPS)
                err = r * inv * gv - e.pop(0)[...]
                lpart = 0.5 * jnp.sum(jnp.mean(err * err, axis=-1, keepdims=True), axis=0, keepdims=True)
                dx, part = _rms_bwd_rows(r, gv, err * (1.0 / n))
                o_ref[...] = dx
                act_ref[...] = dx.astype(BF16)

                @pl.when(pl.program_id(0) == 0)
                def _():
                    dg_ref[...] = jnp.zeros_like(dg_ref)
                    l_ref[...] = jnp.zeros_like(l_ref)

                dg_ref[...] += part
                l_ref[...] += jnp.broadcast_to(lpart, l_ref.shape)
                return
            o_ref[...] = r.astype(out_dtype)
            if relu2_out:
                act_ref[...] = jnp.square(jnp.maximum(r, 0.0)).astype(BF16)
            if gains:
                inv = lax.rsqrt(jnp.mean(r * r, axis=-1, keepdims=True) + RMS_EPS)
                act_ref[...] = (r * inv * g_ref[...]).astype(BF16)

        if b_whole:
            av = a_ref[...].astype(BF16)
            finish(sum(_dot(av[:, kb * bc:(kb + 1) * bc], b_ref[kb].astype(BF16), NT) for kb in range(nb)))
        elif not use_acc:
            av, bv = operands()
            finish(_dot(av, bv, dims))
        else:
            kk = pl.program_id(2)

            def accumulate(a_from, b_from):
                def product():
                    av, bv = operands(a_from, b_from)
                    return _dot(av, bv, dims)

                if nk == 1:
                    finish(product())
                    return

                @pl.when(kk == 0)
                def _():
                    acc_ref[...] = product()

                @pl.when((kk > 0) & (kk < nk - 1))
                def _():
                    acc_ref[...] += product()

                @pl.when(kk == nk - 1)
                def _():
                    finish(acc_ref[...] + product())

            if not tails:
                accumulate(a_ref, b_ref)
            elif mode == "nt":
                pl.when(kk < nk_head)(lambda: accumulate(a_ref, b_ref))
                pl.when(kk >= nk_head)(lambda: accumulate(t_ref, b_ref))
            else:
                in_head = pl.program_id(1) < nj_head
                pl.when(in_head)(lambda: accumulate(a_ref, b_ref))
                pl.when(jnp.logical_not(in_head))(lambda: accumulate(a_ref, t_ref))

        if side is not None:
            pl.when(step == ni * nj * nk - 1)(lambda: side.finish(s_ins, s_outs, s_sems))

    a_spec = {"nn": pl.BlockSpec((tm, tk), lambda i, j, kk: (i, kk)),
              "nt": pl.BlockSpec((tm, tk), lambda i, j, kk: (i, kk)),
              "tn": pl.BlockSpec((tk, tm), lambda i, j, kk: (kk, i))}[mode]
    t_specs = []
    if tails and mode == "nt":
        a_spec = pl.BlockSpec((tm, tk), lambda i, j, kk: (i, jnp.minimum(kk, nk_head - 1)))
        t_specs = [pl.BlockSpec((tm, tk), lambda i, j, kk: (i, jnp.maximum(kk - nk_head, 0)))]
    if tails and mode == "tn":
        t_specs = [pl.BlockSpec((tk, tn), lambda i, j, kk: (kk, jnp.maximum(j - nj_head, 0)))]
    if tails and mode == "tn":
        b_spec = pl.BlockSpec((tk, tn), lambda i, j, kk: (kk, jnp.minimum(j, nj_head - 1)))
    elif not b_cols:
        b_spec = {"nn": pl.BlockSpec((tk, tn), lambda i, j, kk: (kk, j)),
                  "nt": pl.BlockSpec((tn, tk), lambda i, j, kk: (j, kk)),
                  "tn": pl.BlockSpec((tk, tn), lambda i, j, kk: (kk, j))}[mode]
    elif mode == "nn":
        per = bc // tn
        b_spec = pl.BlockSpec((None, tk, tn), lambda i, j, kk: (j // per, kk, j % per))
    elif b_whole:
        b_spec = pl.BlockSpec((nb, tn, bc), lambda i, j, kk: (0, j, 0))
    else:
        assert mode == "nt", mode
        per = bc // tk
        b_spec = pl.BlockSpec((None, tn, tk), lambda i, j, kk: (kk // per, j, kk % per))
    e_spec = pl.BlockSpec((tm, tn), lambda i, j, kk: (i, j))
    main_shapes = [SDS((m, n), out_dtype)] + ([SDS((m, n), BF16)] if n_main == 2 else [])
    g_specs = [pl.BlockSpec((1, tn), lambda i, j, kk: (0, j))] * len(gains)
    l_specs = [pl.BlockSpec((1, tn), lambda i, j, kk: (0, j)), pl.BlockSpec((1, LANE), lambda i, j, kk: (0, 0))] if loss else []
    l_shapes = [SDS((1, n), F32), SDS((1, LANE), F32)] if loss else []
    res = pl.pallas_call(
        body, name=name, grid=(ni, nj, nk),
        in_specs=[a_spec, b_spec] + t_specs + [e_spec] * len(extras) + g_specs + [ANY] * n_side,
        out_specs=[e_spec] * n_main + l_specs + [ANY] * n_side,
        out_shape=main_shapes + l_shapes + (side.out_shapes if side is not None else []),
        scratch_shapes=([pltpu.VMEM((tm, tn), F32)] if use_acc else []) + (side.scratch if side is not None else []),
        compiler_params=_cp(("arbitrary", "arbitrary", "arbitrary")))(a, b, *tails, *extras, *gains, *(side.ins if side is not None else []))
    return res if len(res) > 1 else res[0]


def rmsnorm_fwd(x, g, name, side=None):
    s, d = x.shape
    tm = _tile(s, 256)
    steps = s // tm
    n_side = len(side.ins) if side is not None else 0

    def body(*refs):
        x_ref, g_ref, o_ref = refs[0], refs[1], refs[2 + n_side]
        if side is not None:
            s_ins, s_outs, s_sems = refs[2:2 + n_side], refs[3 + n_side:3 + 2 * n_side], refs[3 + 2 * n_side:]
            _host_begin(side, pl.program_id(0), steps, s_ins, s_outs, s_sems)
        xv = x_ref[...]
        r = lax.rsqrt(jnp.mean(xv * xv, axis=-1, keepdims=True) + RMS_EPS)
        o_ref[...] = (xv * r * g_ref[...]).astype(BF16)
        if side is not None:
            pl.when(pl.program_id(0) == steps - 1)(lambda: side.finish(s_ins, s_outs, s_sems))

    row = pl.BlockSpec((tm, d), lambda i: (i, 0))
    res = pl.pallas_call(
        body, name=name, grid=(steps,), in_specs=[row, pl.BlockSpec((1, d), lambda i: (0, 0))] + [ANY] * n_side,
        out_specs=[row] + [ANY] * n_side, out_shape=[SDS((s, d), BF16)] + (side.out_shapes if side is not None else []),
        scratch_shapes=side.scratch if side is not None else [],
        compiler_params=_cp(("arbitrary",)))(x, g, *(side.ins if side is not None else []))
    return res if side is not None else res[0]


def _rms_bwd_rows(xv, gv, dy):
    r = lax.rsqrt(jnp.mean(xv * xv, axis=-1, keepdims=True) + RMS_EPS)
    xh = xv * r
    dxh = dy * gv
    dx = r * (dxh - xh * jnp.mean(dxh * xh, axis=-1, keepdims=True))
    return dx, jnp.sum(dy * xh, axis=0, keepdims=True)


def rmsnorm_bwd(x, g, dn, resid, name, bf16_copy=False):
    s, d = x.shape
    tm = _tile(s, 256)
    has_r = resid is not None

    def body(*refs):
        x_ref, g_ref, dn_ref = refs[:3]
        dx_ref, dg_ref = refs[3 + has_r:5 + has_r]
        dx, part = _rms_bwd_rows(x_ref[...], g_ref[...], dn_ref[...].astype(F32))
        if has_r:
            dx = dx + refs[3][...]
        dx_ref[...] = dx
        if bf16_copy:
            refs[5 + has_r][...] = dx.astype(BF16)

        @pl.when(pl.program_id(0) == 0)
        def _():
            dg_ref[...] = jnp.zeros_like(dg_ref)

        dg_ref[...] += part

    row = pl.BlockSpec((tm, d), lambda i: (i, 0))
    vec = pl.BlockSpec((1, d), lambda i: (0, 0))
    ins = [x, g, dn] + ([resid] if has_r else [])
    return pl.pallas_call(body, name=name, grid=(s // tm,), in_specs=[row, vec, row] + ([row] if has_r else []),
                          out_specs=[row, vec] + ([row] if bf16_copy else []),
                          out_shape=[SDS((s, d), F32), SDS((1, d), F32)] + ([SDS((s, d), BF16)] if bf16_copy else []),
                          compiler_params=_cp(("arbitrary",)))(*ins)


def mlp_out_loss(act, w, resid, g, tgt, name):
    s, f = act.shape
    d = w.shape[1]
    tm, tk = _tile(s, 512), _tile(f, 1024)
    ni, nk = s // tm, f // tk
    rows = _div(tm, nk)
    assert rows % 8 == 0 and nk >= 2, (tm, nk)

    def body(a_ref, w_ref, r_ref, g_ref, t_ref, dh_ref, dhb_ref, dg_ref, l_ref, acc, pend):
        i, kk = pl.program_id(0), pl.program_id(1)

        @pl.when((i == 0) & (kk == 0))
        def _():
            dg_ref[...] = jnp.zeros_like(dg_ref)
            l_ref[...] = jnp.zeros_like(l_ref)
            acc[...] = jnp.zeros_like(acc)
            pend[...] = jnp.zeros_like(pend)

        prod = _dot(a_ref[...], w_ref[...], NN)
        rs = pl.ds(pl.multiple_of(kk * rows, 8), rows)
        h, gv = pend[rs, :], g_ref[...]
        inv = lax.rsqrt(jnp.mean(h * h, axis=-1, keepdims=True) + RMS_EPS)
        err = h * inv * gv - t_ref[rs, :]
        dx, part = _rms_bwd_rows(h, gv, err * (1.0 / d))
        dh_ref[rs, :] = dx
        dhb_ref[rs, :] = dx.astype(BF16)
        lpart = 0.5 * jnp.sum(jnp.mean(err * err, axis=-1, keepdims=True), axis=0, keepdims=True)
        dg_ref[...] += jnp.where(i > 0, part, 0.0)
        l_ref[...] += jnp.where(i > 0, jnp.broadcast_to(lpart, l_ref.shape), 0.0)
        total = prod + jnp.where(kk == 0, 0.0, acc[...])
        acc[...] = total

        @pl.when((i < ni) & (kk == nk - 1))
        def _():
            pend[...] = total + r_ref[...]

    last = ni - 1
    cur = lambda i: jnp.minimum(i, last)
    lag = lambda i: jnp.maximum(i - 1, 0)
    vec = pl.BlockSpec((1, d), lambda i, kk: (0, 0))
    return pl.pallas_call(
        body, name=name, grid=(ni + 1, nk),
        in_specs=[pl.BlockSpec((tm, tk), lambda i, kk: (cur(i), kk)), pl.BlockSpec((tk, d), lambda i, kk: (kk, 0)),
                  pl.BlockSpec((tm, d), lambda i, kk: (cur(i), 0)), vec, pl.BlockSpec((tm, d), lambda i, kk: (lag(i), 0))],
        out_specs=[pl.BlockSpec((tm, d), lambda i, kk: (lag(i), 0)), pl.BlockSpec((tm, d), lambda i, kk: (lag(i), 0)), vec,
                   pl.BlockSpec((1, LANE), lambda i, kk: (0, 0))],
        out_shape=[SDS((s, d), F32), SDS((s, d), BF16), SDS((1, d), F32), SDS((1, LANE), F32)],
        scratch_shapes=[pltpu.VMEM((tm, d), F32), pltpu.VMEM((tm, d), F32)],
        compiler_params=_cp(("arbitrary", "arbitrary")))(act, w, resid, g, tgt)


def merge(p_all, ya, yb, yc, wa, wb, wc, dm, lay, name):
    s, d = ya.shape[0], lay.d
    wcols = wa.shape[2]
    bwd = dm is not None
    tm, tn = _tile(s, 2048), _tile(wcols, 512)
    nj, per = d // tn, wcols // tn

    y_specs = [pl.BlockSpec((tm, y.shape[1]), lambda i, j, *_: (i, 0)) for y in (ya, yb, yc)]
    w_specs = [pl.BlockSpec((None, w.shape[1], tn), lambda i, j, *_: (j // per, 0, j % per)) for w in (wa, wb, wc)]
    o_spec = pl.BlockSpec((tm, tn), lambda i, j, *_: (i, j))
    if not bwd:
        def body(ga, gb, gc, ya_r, yb_r, yc_r, wa_r, wb_r, wc_r, o_ref):
            ts = [_dot(y[...], w[...], NN) for y, w in ((ya_r, wa_r), (yb_r, wb_r), (yc_r, wc_r))]
            gs = [_sigmoid(g[...]) for g in (ga, gb, gc)]
            o_ref[...] = (gs[0] * ts[0] + gs[1] * ts[1] + gs[2] * ts[2]).astype(BF16)

        gate_specs = [pl.BlockSpec((tm, tn), lambda i, j, b=b: (i, b * nj + j)) for b in range(3)]
        return pl.pallas_call(
            body, name=name, grid=(s // tm, nj), in_specs=gate_specs + y_specs + w_specs,
            out_specs=o_spec, out_shape=SDS((s, d), BF16),
            compiler_params=_cp(("parallel", "parallel")))(p_all, p_all, p_all, ya, yb, yc, wa, wb, wc)

    def body_bwd(g_r, ya_r, yb_r, yc_r, wa_r, wb_r, wc_r, dm_r, dg_o, dta_o, dtb_o, dtc_o):
        for k, (y, w, dt_o) in enumerate(((ya_r, wa_r, dta_o), (yb_r, wb_r, dtb_o), (yc_r, wc_r, dtc_o))):
            @pl.when(pl.program_id(2) == k)
            def _(y=y, w=w, dt_o=dt_o):
                t = _dot(y[...], w[...], NN)
                g = _sigmoid(g_r[...])
                dmv = dm_r[...]
                dg_o[...] = (dmv * t * (g * (1.0 - g))).astype(BF16)
                dt_o[...] = (dmv * g).astype(BF16)

    gate_spec = pl.BlockSpec((tm, tn), lambda i, j, b: (i, b * nj + j))
    return pl.pallas_call(
        body_bwd, name=name, grid=(s // tm, nj, 3), in_specs=[gate_spec] + y_specs + w_specs + [o_spec],
        out_specs=[gate_spec, o_spec, o_spec, o_spec], out_shape=[SDS((s, 3 * d), BF16)] + [SDS((s, d), BF16)] * 3,
        compiler_params=_cp(("arbitrary", "arbitrary", "arbitrary")))(p_all, ya, yb, yc, wa, wb, wc, dm)


SWA_PAIRS = SWA_G // 2


def _swa_probs(qs, kcs, sinks, first):
    shape = (qs[0].shape[0], 2 * SWA_W)
    qi = lax.broadcasted_iota(jnp.int32, shape, 0) % SWA_W
    kj = lax.broadcasted_iota(jnp.int32, shape, 1)
    mask = (kj > qi) & (kj <= qi + SWA_W) & ((kj >= SWA_W) | jnp.logical_not(first))
    ss = [jnp.where(mask, _dot(q, kc, NT) * (SWA_HD ** -0.5), NEG) for q, kc in zip(qs, kcs)]
    ms = [jnp.maximum(jnp.max(s, axis=-1, keepdims=True), sink) for s, sink in zip(ss, sinks)]
    ps = [jnp.exp(s - m) for s, m in zip(ss, ms)]
    es = [jnp.exp(sink - m) for sink, m in zip(sinks, ms)]
    inv = [1.0 / (jnp.sum(p, axis=-1, keepdims=True) + e) for p, e in zip(ps, es)]
    return [p * i for p, i in zip(ps, inv)], [e * i for e, i in zip(es, inv)]


def _swa_stack(ref, h):
    return jnp.concatenate([ref[:, pl.ds((h * SWA_PAIRS + p) * LANE, LANE)] for p in range(SWA_PAIRS)], axis=0)


def _swa_unstack(ref, h, val):
    for p in range(SWA_PAIRS):
        ref[:, pl.ds((h * SWA_PAIRS + p) * LANE, LANE)] = val[p * SWA_W:(p + 1) * SWA_W]


def _swa_sink_col(sk_ref, h, second):
    pair = lax.broadcasted_iota(jnp.int32, (SWA_PAIRS * SWA_W, 1), 0) // SWA_W
    col = jnp.zeros((SWA_PAIRS * SWA_W, 1), F32)
    for p in range(SWA_PAIRS):
        hh = h * SWA_G + 2 * p + second
        col = jnp.where(pair == p, sk_ref[0:1, hh:hh + 1], col)
    return col


def _swa_kv_tiles(cur_ref, prev_ref, h):
    t = jnp.concatenate([prev_ref[...], cur_ref[...]], axis=0)
    lane = lax.broadcasted_iota(jnp.int32, t.shape, 1)
    moved = pltpu.roll(t, SWA_HD, axis=1)
    low, high = (t, moved) if h == 0 else (moved, t)
    return jnp.where(lane < SWA_HD, low, 0.0).astype(BF16), jnp.where(lane >= SWA_HD, high, 0.0).astype(BF16)


def _swa_kv_grad(g_low, g_high, h):
    lane = lax.broadcasted_iota(jnp.int32, g_low.shape, 1)
    if h == 0:
        return jnp.where(lane < SWA_HD, g_low + pltpu.roll(g_high, SWA_HD, axis=1), 0.0)
    return jnp.where(lane >= SWA_HD, pltpu.roll(g_low, SWA_HD, axis=1) + g_high, 0.0)


def _swa_specs(lay):
    w = SWA_W
    q_spec = pl.BlockSpec((w, Q_W), lambda n: (n, _div(lay.q, Q_W)))
    cur = lambda off: pl.BlockSpec((w, KV_W), lambda n: (n, _div(off, KV_W)))
    prev = lambda off: pl.BlockSpec((w, KV_W), lambda n: (jnp.maximum(n - 1, 0), _div(off, KV_W)))
    return q_spec, cur(lay.k), prev(lay.k), cur(lay.v), prev(lay.v)


def swa_fwd(p_all, sinks, lay, name):
    s = p_all.shape[0]
    nb = _div(s, SWA_W)

    def body(q_ref, kc_ref, kp_ref, vc_ref, vp_ref, sk_ref, o_ref):
        first = pl.program_id(0) == 0
        units = [(h, e) for h in range(SWA_HKV) for e in range(2)]
        ks = [_swa_kv_tiles(kc_ref, kp_ref, h) for h in range(SWA_HKV)]
        vs = [_swa_kv_tiles(vc_ref, vp_ref, h) for h in range(SWA_HKV)]
        qs = [_swa_stack(q_ref, h).astype(BF16) for h in range(SWA_HKV)]
        ps, _ = _swa_probs([qs[h] for h, e in units], [ks[h][e] for h, e in units],
                           [_swa_sink_col(sk_ref, h, e) for h, e in units], first)
        os = [_dot(p.astype(BF16), vs[h][e], NN) for p, (h, e) in zip(ps, units)]
        for h in range(SWA_HKV):
            _swa_unstack(o_ref, h, (os[2 * h] + os[2 * h + 1]).astype(BF16))

    q_spec, kc_s, kp_s, vc_s, vp_s = _swa_specs(lay)
    return pl.pallas_call(
        body, name=name, grid=(nb,),
        in_specs=[q_spec, kc_s, kp_s, vc_s, vp_s, pl.BlockSpec(sinks.shape, lambda n: (0, 0))],
        out_specs=pl.BlockSpec((SWA_W, Q_W), lambda n: (n, 0)), out_shape=SDS((s, Q_W), BF16),
        compiler_params=_cp(("parallel",)))(p_all, p_all, p_all, p_all, p_all, sinks)


def swa_bwd(p_all, sinks, dy, lay, name):
    s = p_all.shape[0]
    nb = _div(s, SWA_W)
    w = SWA_W

    def body(q_ref, kc_ref, kp_ref, vc_ref, vp_ref, sk_ref, do_ref, dq_ref, dk_ref, dv_ref, ds_ref, kcar, vcar):
        n = pl.program_id(0)
        first = n == 0

        @pl.when(first)
        def _():
            kcar[...] = jnp.zeros_like(kcar)
            vcar[...] = jnp.zeros_like(vcar)
            ds_ref[...] = jnp.zeros_like(ds_ref)

        @pl.when(n < nb)
        def _():
            lane = lax.broadcasted_iota(jnp.int32, (1, LANE), 1)
            dsink = jnp.zeros((1, LANE), F32)
            units = [(h, e) for h in range(SWA_HKV) for e in range(2)]
            ks = [_swa_kv_tiles(kc_ref, kp_ref, h) for h in range(SWA_HKV)]
            vs = [_swa_kv_tiles(vc_ref, vp_ref, h) for h in range(SWA_HKV)]
            qs = [_swa_stack(q_ref, h).astype(BF16) for h in range(SWA_HKV)]
            dos = [_swa_stack(do_ref, h).astype(BF16) for h in range(SWA_HKV)]
            ps, psinks = _swa_probs([qs[h] for h, e in units], [ks[h][e] for h, e in units],
                                    [_swa_sink_col(sk_ref, h, e) for h, e in units], first)
            dps = [_dot(dos[h], vs[h][e], NT) for h, e in units]
            dvs = [_dot(p.astype(BF16), dos[h], TN) for p, (h, e) in zip(ps, units)]
            rss = [jnp.sum(dp * p, axis=-1, keepdims=True) for dp, p in zip(dps, ps)]
            dsb = [(p * (dp - rs) * (SWA_HD ** -0.5)).astype(BF16) for p, dp, rs in zip(ps, dps, rss)]
            dqs = [_dot(d, ks[h][e], NN) for d, (h, e) in zip(dsb, units)]
            dks = [_dot(d, qs[h], TN) for d, (h, e) in zip(dsb, units)]
            for u, (h, e) in enumerate(units):
                psr = psinks[u] * rss[u]
                for pr in range(SWA_PAIRS):
                    hh = h * SWA_G + 2 * pr + e
                    dsink = dsink + jnp.where(lane == hh, -jnp.sum(psr[pr * w:(pr + 1) * w], axis=0, keepdims=True), 0.0)
            dk_tile = jnp.zeros((2 * w, KV_W), F32)
            dv_tile = jnp.zeros((2 * w, KV_W), F32)
            for h in range(SWA_HKV):
                _swa_unstack(dq_ref, h, (dqs[2 * h] + dqs[2 * h + 1]).astype(BF16))
                dk_tile = dk_tile + _swa_kv_grad(dks[2 * h], dks[2 * h + 1], h)
                dv_tile = dv_tile + _swa_kv_grad(dvs[2 * h], dvs[2 * h + 1], h)
            dk_ref[...] = (kcar[...] + dk_tile[:w]).astype(BF16)
            dv_ref[...] = (vcar[...] + dv_tile[:w]).astype(BF16)
            kcar[...] = dk_tile[w:]
            vcar[...] = dv_tile[w:]
            ds_ref[...] += dsink

        @pl.when(n == nb)
        def _():
            dk_ref[...] = kcar[...].astype(BF16)
            dv_ref[...] = vcar[...].astype(BF16)

    last = nb - 1
    q_spec = pl.BlockSpec((w, Q_W), lambda n: (jnp.minimum(n, last), _div(lay.q, Q_W)))
    cur = lambda off: pl.BlockSpec((w, KV_W), lambda n: (jnp.minimum(n, last), _div(off, KV_W)))
    prev = lambda off: pl.BlockSpec((w, KV_W), lambda n: (jnp.clip(n - 1, 0, last), _div(off, KV_W)))
    row = pl.BlockSpec((w, Q_W), lambda n: (jnp.minimum(n, last), 0))
    kv_out = pl.BlockSpec((w, KV_W), lambda n: (jnp.maximum(n - 1, 0), 0))
    return pl.pallas_call(
        body, name=name, grid=(nb + 1,),
        in_specs=[q_spec, cur(lay.k), prev(lay.k), cur(lay.v), prev(lay.v), pl.BlockSpec(sinks.shape, lambda n: (0, 0)), row],
        out_specs=[row, kv_out, kv_out, pl.BlockSpec((1, LANE), lambda n: (0, 0))],
        out_shape=[SDS((s, Q_W), BF16), SDS((s, KV_W), BF16), SDS((s, KV_W), BF16), SDS((1, LANE), F32)],
        scratch_shapes=[pltpu.VMEM((w, KV_W), F32), pltpu.VMEM((w, KV_W), F32)],
        compiler_params=_cp(("arbitrary",)))(p_all, p_all, p_all, p_all, p_all, sinks, dy)


def _xa_probs(qs, mks):
    ss = [_dot(q, mk, NT) * (XA_D ** -0.5) for q, mk in zip(qs, mks)]
    ps = [jnp.exp(s - jnp.max(s, axis=-1, keepdims=True)) for s in ss]
    inv = [1.0 / jnp.sum(p, axis=-1, keepdims=True) for p in ps]
    return [p * i for p, i in zip(ps, inv)]


def xattn_fwd(p_all, mkv, lay, name):
    s, nm = p_all.shape[0], mkv.shape[0]
    tm = _tile(s, 512)

    def body(q_ref, mkv_ref, o_ref):
        heads = range(XA_H)
        cols = [pl.ds(h * XA_D, XA_D) for h in heads]
        ps = _xa_probs([q_ref[:, c].astype(BF16) for c in cols], [mkv_ref[:, c] for c in cols])
        os = [_dot(ps[h].astype(BF16), mkv_ref[:, pl.ds(XA_W + h * XA_D, XA_D)], NN) for h in heads]
        for h in heads:
            o_ref[:, cols[h]] = os[h].astype(BF16)

    return pl.pallas_call(
        body, name=name, grid=(s // tm,),
        in_specs=[pl.BlockSpec((tm, XA_W), lambda i: (i, _div(lay.qc, XA_W))), pl.BlockSpec((nm, 2 * XA_W), lambda i: (0, 0))],
        out_specs=pl.BlockSpec((tm, XA_W), lambda i: (i, 0)), out_shape=SDS((s, XA_W), BF16),
        compiler_params=_cp(("parallel",)))(p_all, mkv)


def xattn_bwd(p_all, mkv, dy, lay, name):
    s, nm = p_all.shape[0], mkv.shape[0]
    tm = _tile(s, 512)

    def body(q_ref, mkv_ref, do_ref, dq_ref, dmkv_ref):
        @pl.when(pl.program_id(0) == 0)
        def _():
            dmkv_ref[...] = jnp.zeros_like(dmkv_ref)

        heads = range(XA_H)
        cols = [pl.ds(h * XA_D, XA_D) for h in heads]
        vcols = [pl.ds(XA_W + h * XA_D, XA_D) for h in heads]
        qs = [q_ref[:, c].astype(BF16) for c in cols]
        dos = [do_ref[:, c].astype(BF16) for c in cols]
        ps = _xa_probs(qs, [mkv_ref[:, c] for c in cols])
        dps = [_dot(dos[h], mkv_ref[:, vcols[h]], NT) for h in heads]
        dvs = [_dot(ps[h].astype(BF16), dos[h], TN) for h in heads]
        dsb = [(p * (dp - jnp.sum(dp * p, axis=-1, keepdims=True)) * (XA_D ** -0.5)).astype(BF16) for p, dp in zip(ps, dps)]
        dqs = [_dot(dsb[h], mkv_ref[:, cols[h]], NN) for h in heads]
        dks = [_dot(dsb[h], qs[h], TN) for h in heads]
        for h in heads:
            dq_ref[:, cols[h]] = dqs[h].astype(BF16)
            dmkv_ref[:, vcols[h]] += dvs[h]
            dmkv_ref[:, cols[h]] += dks[h]

    row = pl.BlockSpec((tm, XA_W), lambda i: (i, 0))
    full = pl.BlockSpec((nm, 2 * XA_W), lambda i: (0, 0))
    return pl.pallas_call(
        body, name=name, grid=(s // tm,),
        in_specs=[pl.BlockSpec((tm, XA_W), lambda i: (i, _div(lay.qc, XA_W))), full, row],
        out_specs=[row, full], out_shape=[SDS((s, XA_W), BF16), SDS((nm, 2 * XA_W), F32)],
        compiler_params=_cp(("arbitrary",)))(p_all, mkv, dy)


def _shift_down(cur, prev8, s):
    cat = jnp.concatenate([prev8, cur[0:8]], axis=0)
    return pltpu.roll(cur, s, axis=0), pltpu.roll(cat, s, axis=0)[8:16]


def _shift_up(cur, next8, s):
    tm = cur.shape[0]
    cat = jnp.concatenate([cur[tm - 8:tm], next8], axis=0)
    return pltpu.roll(cur, tm - s, axis=0), pltpu.roll(cat, 16 - s, axis=0)[0:8]


def _conv_rows(cur, prev8, w):
    main = w[GDN_CONV - 1:GDN_CONV] * cur
    top = w[GDN_CONV - 1:GDN_CONV] * cur[0:8]
    for sft in range(1, GDN_CONV):
        wi = w[GDN_CONV - 1 - sft:GDN_CONV - sft]
        a, b = _shift_down(cur, prev8, sft)
        main = main + wi * a
        top = top + wi * b
    return jnp.concatenate([top, main[8:]], axis=0)


def _conv_rows_bwd(cur, prev8, d, next8, w):
    tm = cur.shape[0]
    row = lax.broadcasted_iota(jnp.int32, (tm, 1), 0)
    main = w[GDN_CONV - 1:GDN_CONV] * d
    bot = w[GDN_CONV - 1:GDN_CONV] * d[tm - 8:tm]
    dws = [jnp.sum(d * cur, axis=0, keepdims=True)]
    for sft in range(1, GDN_CONV):
        wi = w[GDN_CONV - 1 - sft:GDN_CONV - sft]
        a, b = _shift_up(d, next8, sft)
        main = main + wi * a
        bot = bot + wi * b
        xa, xb = _shift_down(cur, prev8, sft)
        dws.append(jnp.sum(jnp.where(row >= 8, d * xa, 0.0), axis=0, keepdims=True)
                   + jnp.sum(d[0:8] * xb, axis=0, keepdims=True))
    return jnp.concatenate([main[:tm - 8], bot], axis=0), dws


def _gdn_chunk(xq, xk, xv, ab, gp, bdot=_bdot_plain):
    c = GDN_C
    nc = xq.shape[0] // c
    lane = lax.broadcasted_iota(jnp.int32, (c, LANE), 1)
    row = lax.broadcasted_iota(jnp.int32, (c, c), 0)
    col = lax.broadcasted_iota(jnp.int32, (c, c), 1)
    g_tile = -jnp.exp(gp[0:1, :]) * _softplus(ab + gp[1:2, :])
    b_tile = _sigmoid(ab)
    tri = (row >= col).astype(F32)
    qa, ka, va = _silu(xq), _silu(xk), _silu(xv)
    items = []
    for ci in range(nc):
        rs = slice(ci * c, (ci + 1) * c)
        gcum = _dot(tri, g_tile[rs], NN, HI)
        gcum_t = gcum.T
        for h in range(GDN_H):
            hs = slice(h * GDN_D, (h + 1) * GDN_D)
            q, k, v = qa[rs, hs], ka[rs, hs], va[rs, hs]
            q = q * lax.rsqrt(jnp.sum(q * q, axis=-1, keepdims=True) + L2_EPS) * (GDN_D ** -0.5)
            k = k * lax.rsqrt(jnp.sum(k * k, axis=-1, keepdims=True) + L2_EPS)
            gc = jnp.sum(jnp.where(lane == h, gcum, 0.0), axis=1, keepdims=True)
            beta = jnp.sum(jnp.where(lane == GDN_H + h, b_tile[rs], 0.0), axis=1, keepdims=True)
            decay = jnp.exp(jnp.where(row >= col, gc - gcum_t[h:h + 1, :], NEG))
            items.append((q, k, v, gc, beta, decay))
    kks = [bdot(k, k, NT) for (_, k, _, _, _, _) in items]
    xs = tuple(-jnp.where(row > col, it[4] * kk * it[5], 0.0) for it, kk in zip(items, kks))
    nns = _neumann(xs) if bdot is _bdot_plain else _neumann_vjp(xs)
    qks = [bdot(q, k, NT) for (q, k, _, _, _, _) in items]
    out = []
    for (q, k, v, gc, beta, decay), n, qk in zip(items, nns, qks):
        eg = jnp.exp(gc)
        vb = v * beta
        kbe = k * (beta * eg)
        gl = gc[c - 1:c, :]
        out.append((vb + bdot(n, vb, NN), kbe + bdot(n, kbe, NN), q * eg, k * jnp.exp(gl - gc), qk * decay, jnp.exp(gl)))
    return [out[ci * GDN_H:(ci + 1) * GDN_H] for ci in range(nc)]


GDN_CPS = 4


def _gdn_pre_specs(lay, t, tile):
    c0 = _div(lay.qkv, GDN_W)
    cur = [pl.BlockSpec((t, GDN_W), lambda n, j=j: (tile(n), c0 + j)) for j in range(3)]
    prev = [pl.BlockSpec((8, GDN_W), lambda n, j=j: (jnp.maximum(tile(n) * (t // 8) - 1, 0), c0 + j)) for j in range(3)]
    return cur + prev + [pl.BlockSpec((GDN_CONV, 3 * GDN_W), lambda n: (0, 0)),
                         pl.BlockSpec((t, LANE), lambda n: (tile(n), _div(lay.ab, LANE))),
                         pl.BlockSpec((8, LANE), lambda n: (0, 0))]


def _gdn_conv_inputs(x_refs, prev_refs, w_ref, first):
    out = []
    for j in range(3):
        prev8 = jnp.where(first, 0.0, prev_refs[j][...])
        out.append((x_refs[j][...], prev8, w_ref[:, pl.ds(j * GDN_W, GDN_W)]))
    return out


def gdn_pre_fwd(p_all, conv_w, gp, lay, name):
    s = p_all.shape[0]
    c = GDN_C
    n = _div(s, c)
    cps = _tile(n, GDN_CPS)
    t = cps * c

    def body(xq, xk, xv, pq, pk, pv, cw, ab, gp_ref, u_ref, w_ref, qd_ref, kd_ref, qk_ref, gl_ref):
        lane = lax.broadcasted_iota(jnp.int32, (1, LANE), 1)
        xs = [_conv_rows(*a) for a in _gdn_conv_inputs((xq, xk, xv), (pq, pk, pv), cw, pl.program_id(0) == 0)]
        chunks = _gdn_chunk(xs[0], xs[1], xs[2], ab[...], gp_ref[...])
        for ci, heads in enumerate(chunks):
            rs = pl.ds(ci * c, c)
            gl_row = jnp.zeros((1, LANE), F32)
            for h, (u, w, qd, kd, qk, gl) in enumerate(heads):
                hs = pl.ds(h * GDN_D, GDN_D)
                u_ref[rs, hs] = u
                w_ref[rs, hs] = w.astype(BF16)
                qd_ref[rs, hs] = qd.astype(BF16)
                kd_ref[rs, hs] = kd.astype(BF16)
                qk_ref[rs, pl.ds(h * c, c)] = qk.astype(BF16)
                gl_row = gl_row + jnp.where(lane == h, gl, 0.0)
            gl_ref[ci] = gl_row

    row = pl.BlockSpec((t, GDN_W), lambda n: (n, 0))
    return pl.pallas_call(
        body, name=name, grid=(n // cps,), in_specs=_gdn_pre_specs(lay, t, lambda n: n),
        out_specs=[row, row, row, row, pl.BlockSpec((t, GDN_H * c), lambda n: (n, 0)), pl.BlockSpec((cps, 1, LANE), lambda n: (n, 0, 0))],
        out_shape=[SDS((s, GDN_W), F32), SDS((s, GDN_W), BF16), SDS((s, GDN_W), BF16), SDS((s, GDN_W), BF16),
                   SDS((s, GDN_H * c), BF16), SDS((n, 1, LANE), F32)],
        compiler_params=_cp(("parallel",)))(p_all, p_all, p_all, p_all, p_all, p_all, conv_w, p_all, gp)


def gdn_pre_bwd(p_all, conv_w, gp, du, dw, dqd, dkd, dqk, dgl, lay, name):
    s = p_all.shape[0]
    c = GDN_C
    n = _div(s, c)
    cps = _tile(n, GDN_CPS)
    t = cps * c
    steps = n // cps
    chunk = functools.partial(_gdn_chunk, bdot=_bdot_vjp)

    def body(xq, xk, xv, pq, pk, pv, cw, ab, gp_ref, du_r, dw_r, dqd_r, dkd_r, dqk_r, dgl_r,
             dx_ref, dab_ref, dgp_ref, dcw_ref, carry):
        step = pl.program_id(0)

        @pl.when(step == 0)
        def _():
            dgp_ref[...] = jnp.zeros_like(dgp_ref)
            dcw_ref[...] = jnp.zeros_like(dcw_ref)
            carry[...] = jnp.zeros_like(carry)

        lane = lax.broadcasted_iota(jnp.int32, (1, LANE), 1)
        conv_in = _gdn_conv_inputs((xq, xk, xv), (pq, pk, pv), cw, step == steps - 1)
        xs = [_conv_rows(*a) for a in conv_in]
        _, vjp = jax.vjp(chunk, xs[0], xs[1], xs[2], ab[...], gp_ref[...])
        cts = []
        for ci in range(cps):
            rs = pl.ds(ci * c, c)
            heads = []
            for h in range(GDN_H):
                hs = pl.ds(h * GDN_D, GDN_D)
                dgl_h = jnp.sum(jnp.where(lane == h, dgl_r[ci], 0.0), axis=1, keepdims=True)
                heads.append((du_r[rs, hs], dw_r[rs, hs], dqd_r[rs, hs], dkd_r[rs, hs], dqk_r[rs, pl.ds(h * c, c)], dgl_h))
            cts.append(heads)
        *dxs, dab, dgp = vjp(cts)
        for j, (d, (cur, prev8, w)) in enumerate(zip(dxs, conv_in)):
            cols = pl.ds(j * GDN_W, GDN_W)
            dx, dws = _conv_rows_bwd(cur, prev8, d, carry[:, cols], w)
            carry[:, cols] = d[0:8]
            dx_ref[:, cols] = dx.astype(BF16)
            for sft in range(GDN_CONV):
                dcw_ref[GDN_CONV - 1 - sft:GDN_CONV - sft, cols] += dws[sft]
        dab_ref[...] = dab.astype(BF16)
        dgp_ref[...] += dgp

    tile = lambda i: steps - 1 - i
    row = pl.BlockSpec((t, GDN_W), lambda i: (tile(i), 0))
    return pl.pallas_call(
        body, name=name, grid=(steps,),
        in_specs=_gdn_pre_specs(lay, t, tile) + [row, row, row, row, pl.BlockSpec((t, GDN_H * c), lambda i: (tile(i), 0)),
                                                 pl.BlockSpec((cps, 1, LANE), lambda i: (tile(i), 0, 0))],
        out_specs=[pl.BlockSpec((t, 3 * GDN_W), lambda i: (tile(i), 0)), pl.BlockSpec((t, LANE), lambda i: (tile(i), 0)),
                   pl.BlockSpec((8, LANE), lambda i: (0, 0)), pl.BlockSpec((GDN_CONV, 3 * GDN_W), lambda i: (0, 0))],
        out_shape=[SDS((s, 3 * GDN_W), BF16), SDS((s, LANE), BF16), SDS((8, LANE), F32), SDS((GDN_CONV, 3 * GDN_W), F32)],
        scratch_shapes=[pltpu.VMEM((8, 3 * GDN_W), F32)],
        compiler_params=_cp(("arbitrary",)))(p_all, p_all, p_all, p_all, p_all, p_all, conv_w, p_all, gp,
                                             du, dw, dqd, dkd, dqk, dgl)


def _lane_scalar(row, h):
    lane = lax.broadcasted_iota(jnp.int32, row.shape, 1)
    return jnp.sum(jnp.where(lane == h, row, 0.0), axis=1, keepdims=True)


def _gdn_out_head(oh, zh, nw):
    return oh * lax.rsqrt(jnp.mean(oh * oh, axis=-1, keepdims=True) + RMS_EPS) * nw * _silu(zh)


def gdn_scan_fwd(u, w, qd, kd, qk, gl, p_all, nw, lay, name):
    s = u.shape[0]
    c = GDN_C
    n = _div(s, c)
    cps = _tile(n, GDN_CPS)
    t = cps * c

    def body(u_r, w_r, qd_r, kd_r, qk_r, gl_r, z_r, nw_r, o_ref, s_ref, y_ref, st):
        @pl.when(pl.program_id(0) == 0)
        def _():
            st[...] = jnp.zeros_like(st)

        heads = range(GDN_H)
        hs = [pl.ds(h * GDN_D, GDN_D) for h in heads]
        for ci in range(cps):
            rs = pl.ds(ci * c, c)
            s_ref[ci] = st[...]
            sh = [st[hs[h], :] for h in heads]
            shb = [x.astype(BF16) for x in sh]
            ws = [_dot(w_r[rs, hs[h]], shb[h], NN) for h in heads]
            qs = [_dot(qd_r[rs, hs[h]], shb[h], NN) for h in heads]
            vb = [(u_r[rs, hs[h]] - ws[h]).astype(BF16) for h in heads]
            ov = [_dot(qk_r[rs, pl.ds(h * c, c)], vb[h], NN) for h in heads]
            kv = [_dot(kd_r[rs, hs[h]], vb[h], TN) for h in heads]
            os = [qs[h] + ov[h] for h in heads]
            for h in heads:
                o_ref[rs, hs[h]] = os[h]
                y_ref[rs, hs[h]] = _gdn_out_head(os[h], z_r[rs, hs[h]], nw_r[...]).astype(BF16)
                st[hs[h], :] = sh[h] * _lane_scalar(gl_r[ci], h) + kv[h]

    row = pl.BlockSpec((t, GDN_W), lambda i: (i, 0))
    return pl.pallas_call(
        body, name=name, grid=(n // cps,),
        in_specs=[row, row, row, row, pl.BlockSpec((t, GDN_H * c), lambda i: (i, 0)), pl.BlockSpec((cps, 1, LANE), lambda i: (i, 0, 0)),
                  pl.BlockSpec((t, GDN_W), lambda i: (i, _div(lay.z, GDN_W))), pl.BlockSpec((1, GDN_D), lambda i: (0, 0))],
        out_specs=[row, pl.BlockSpec((cps, GDN_W, GDN_D), lambda i: (i, 0, 0)), row],
        out_shape=[SDS((s, GDN_W), F32), SDS((n, GDN_W, GDN_D), F32), SDS((s, GDN_W), BF16)],
        scratch_shapes=[pltpu.VMEM((GDN_W, GDN_D), F32)],
        compiler_params=_cp(("arbitrary",)))(u, w, qd, kd, qk, gl, p_all, nw)


def gdn_scan_bwd(u, w, qd, kd, qk, gl, states, o, dy, p_all, nw, lay, name):
    s = u.shape[0]
    c = GDN_C
    n = _div(s, c)
    cps = _tile(n, GDN_CPS)
    t = cps * c
    steps = n // cps

    def body(u_r, w_r, qd_r, kd_r, qk_r, gl_r, s_r, o_r, dy_r, z_r, nw_r,
             du_o, dw_o, dqd_o, dkd_o, dqk_o, dgl_o, dz_o, dnw_o, dst):
        @pl.when(pl.program_id(0) == 0)
        def _():
            dst[...] = jnp.zeros_like(dst)
            dnw_o[...] = jnp.zeros_like(dnw_o)

        lane = lax.broadcasted_iota(jnp.int32, (1, LANE), 1)
        heads = range(GDN_H)
        hs = [pl.ds(h * GDN_D, GDN_D) for h in heads]
        qs = [pl.ds(h * c, c) for h in heads]
        for ci in reversed(range(cps)):
            rs = pl.ds(ci * c, c)
            outs = [jax.vjp(_gdn_out_head, o_r[rs, hs[h]], z_r[rs, hs[h]], nw_r[...])[1](dy_r[rs, hs[h]].astype(F32))
                    for h in heads]
            for h in heads:
                dz_o[rs, hs[h]] = outs[h][1].astype(BF16)
                dnw_o[...] += outs[h][2]
            sh = [s_r[ci, hs[h], :] for h in heads]
            shb = [x.astype(BF16) for x in sh]
            ds_out = [dst[hs[h], :] for h in heads]
            dsb = [x.astype(BF16) for x in ds_out]
            dob = [outs[h][0].astype(BF16) for h in heads]
            ws = [_dot(w_r[rs, hs[h]], shb[h], NN) for h in heads]
            dv1 = [_dot(qk_r[rs, qs[h]], dob[h], TN) for h in heads]
            dv2 = [_dot(kd_r[rs, hs[h]], dsb[h], NN) for h in heads]
            dqd = [_dot(dob[h], shb[h], NT) for h in heads]
            dsq = [_dot(qd_r[rs, hs[h]], dob[h], TN) for h in heads]
            vb = [(u_r[rs, hs[h]] - ws[h]).astype(BF16) for h in heads]
            dv = [dv1[h] + dv2[h] for h in heads]
            dvb = [x.astype(BF16) for x in dv]
            dw = [_dot(dvb[h], shb[h], NT) for h in heads]
            dkd = [_dot(vb[h], dsb[h], NT) for h in heads]
            dqk = [_dot(dob[h], vb[h], NT) for h in heads]
            dsw = [_dot(w_r[rs, hs[h]], dvb[h], TN) for h in heads]
            dgl_row = jnp.zeros((1, LANE), F32)
            for h in heads:
                du_o[rs, hs[h]] = dv[h]
                dw_o[rs, hs[h]] = -dw[h]
                dqd_o[rs, hs[h]] = dqd[h]
                dkd_o[rs, hs[h]] = dkd[h]
                dqk_o[rs, qs[h]] = dqk[h]
                dgl_row = dgl_row + jnp.where(lane == h, jnp.sum(jnp.sum(ds_out[h] * sh[h], axis=1, keepdims=True), axis=0, keepdims=True), 0.0)
                dst[hs[h], :] = ds_out[h] * _lane_scalar(gl_r[ci], h) + dsq[h] - dsw[h]
            dgl_o[ci] = dgl_row

    rev = lambda i: steps - 1 - i
    row = pl.BlockSpec((t, GDN_W), lambda i: (rev(i), 0))
    qks = pl.BlockSpec((t, GDN_H * c), lambda i: (rev(i), 0))
    gls = pl.BlockSpec((cps, 1, LANE), lambda i: (rev(i), 0, 0))
    zs = pl.BlockSpec((t, GDN_W), lambda i: (rev(i), _div(lay.z, GDN_W)))
    nws = pl.BlockSpec((1, GDN_D), lambda i: (0, 0))
    return pl.pallas_call(
        body, name=name, grid=(steps,),
        in_specs=[row, row, row, row, qks, gls, pl.BlockSpec((cps, GDN_W, GDN_D), lambda i: (rev(i), 0, 0)), row, row, zs, nws],
        out_specs=[row, row, row, row, qks, gls, row, nws],
        out_shape=[SDS((s, GDN_W), F32)] * 4 + [SDS((s, GDN_H * c), F32), SDS((n, 1, LANE), F32), SDS((s, GDN_W), BF16),
                                                SDS((1, GDN_D), F32)],
        scratch_shapes=[pltpu.VMEM((GDN_W, GDN_D), F32)],
        compiler_params=_cp(("arbitrary",)))(u, w, qd, kd, qk, gl, states, o, dy, p_all, nw)


def _cols_to_full(g):
    n, k, c = g.shape
    return g.transpose(1, 0, 2).reshape(k, n * c)


def _rows_to_blocks(w):
    return w.reshape(N_DEV, w.shape[0] // N_DEV, w.shape[1])


def _pack_small(parts, rows):
    flat = jnp.concatenate([jnp.pad(p.reshape(-1), (0, -p.size % LANE)) for p in parts])
    return jnp.pad(flat, (0, rows * LANE - flat.size)).reshape(rows, LANE)


def kernel(x, mem, g_mix, w_in, sinks, conv_w, a_log, dt_bias, gdn_norm_w, g_mem, w_mem_kv, w_swa_up, w_gdn_up, w_xa_up, w_out, g_mlp, w_mlp_in, w_mlp_out, g_final, loss_target, m_g_mix, m_w_in, m_sinks, m_conv_w, m_a_log, m_dt_bias, m_gdn_norm_w, m_g_mem, m_w_mem_kv, m_w_swa_up, m_w_gdn_up, m_w_xa_up, m_w_out, m_g_mlp, m_w_mlp_in, m_w_mlp_out, m_g_final, v_g_mix, v_w_in, v_sinks, v_conv_w, v_a_log, v_dt_bias, v_gdn_norm_w, v_g_mem, v_w_mem_kv, v_w_swa_up, v_w_gdn_up, v_w_xa_up, v_w_out, v_g_mlp, v_w_mlp_in, v_w_mlp_out, v_g_final):
    xs, ms, tgt = x[0], mem[0], loss_target[0]
    s, d = xs.shape
    lay = Layout(d)
    px, py, pc = _position()
    dev = 4 * px + 2 * py + pc

    n1, g_in, g_conv = rmsnorm_fwd(xs, g_mix, "norm_mix", side=GatherJob([w_in[0].astype(BF16), conv_w[0]]))
    W_in = pad_w_in(g_in, lay)
    convw = _cols_to_full(g_conv)
    gp = jnp.zeros((8, LANE), F32).at[0, :GDN_H].set(a_log[0]).at[1, :GDN_H].set(dt_bias[0])
    later = [w_mem_kv[0], w_swa_up[0], w_gdn_up[0], w_xa_up[0], w_out[0], w_mlp_in[0]]

    p_all, g_mkv, W_sup, W_gup, W_xup, g_out, W_m1 = matmul(
        n1, W_in, mode="nn", out_dtype=F32, name="proj_in", tm=2048, tn=1024, tk=d,
        side=GatherJob([w.astype(BF16) for w in later]))
    W_mkv = g_mkv.reshape(-1, g_mkv.shape[2])
    W_out = g_out.reshape(-1, d)
    y_a = swa_fwd(p_all, sinks, lay, "swa_fwd")
    u, gw, gqd, gkd, gqk, ggl = gdn_pre_fwd(p_all, convw, gp, lay, "gdn_pre_fwd")
    o_b, states, y_b = gdn_scan_fwd(u, gw, gqd, gkd, gqk, ggl, p_all, gdn_norm_w, lay, "gdn_scan_fwd")
    nm = rmsnorm_fwd(ms, g_mem, "norm_mem")
    mkv = matmul(nm, W_mkv, mode="nn", out_dtype=BF16, name="proj_mem", tk=d)
    y_c = xattn_fwd(p_all, mkv, lay, "xattn_fwd")
    merged = merge(p_all, y_a, y_b, y_c, W_sup, W_gup, W_xup, None, lay, "merge_fwd")
    h1, n2 = matmul(merged, W_out, mode="nn", out_dtype=F32, name="proj_out", tm=512, tn=d, tk=d, resid=xs, rms_gain=g_mlp)
    uu, act, g_m2 = matmul(n2, W_m1, mode="nn", out_dtype=F32, name="mlp_in", tm=2048, tn=512, tk=d, b_cols=True,
                           relu2_out=True, side=GatherJob([w_mlp_out[0].astype(BF16)]))
    W_m2 = g_m2.reshape(-1, d)
    dh2, dh2_b, dg_final, lrow = mlp_out_loss(act, W_m2, h1, g_final.reshape(1, d), tgt, "mlp_out_loss")
    loss = lax.psum(lrow[0, 0], ("x", "y", "c"))

    du = matmul(dh2_b, W_m2, mode="nt", out_dtype=BF16, name="mlp_out_dx", tm=2048, tn=512, tk=d, relu2_grad_of=uu)
    dW_m2 = matmul(act, dh2_b, mode="tn", out_dtype=BF16, name="mlp_out_dw", tm=1024, tn=2048, tk=1024)
    dW_m2 = _rows_to_blocks(dW_m2)
    dn2, sib_m2 = matmul(du, W_m1, mode="nt", out_dtype=F32, name="mlp_in_dx", tm=1024, tn=2048, tk=1024, b_cols=True,
                         side=PairExchangeJob([dW_m2], [False]))
    c_m2 = pair_add(dW_m2, False, sib_m2, "grads_pair_add_m2")
    dW_m1 = matmul(n2, du, mode="tn", out_dtype=BF16, name="mlp_in_dw", tm=2048, tn=1024, tk=1024)
    dh1, dg_mlp, dh1_b = rmsnorm_bwd(h1, g_mlp, dn2, dh2, "norm_mlp_bwd", bf16_copy=True)

    dmerged, sib_m1 = matmul(dh1_b, W_out, mode="nt", out_dtype=F32, name="proj_out_dx", tm=2048, tn=512, tk=d,
                             side=PairExchangeJob([dW_m1], [True]))
    c_m1 = pair_add(dW_m1, True, sib_m1, "grads_pair_add_m1")
    dW_out = matmul(merged, dh1_b, mode="tn", out_dtype=BF16, name="proj_out_dw", tm=2048, tn=1024, tk=1024)
    dgates, dta, dtb, dtc = merge(p_all, y_a, y_b, y_c, W_sup, W_gup, W_xup, dmerged, lay, "merge_bwd")
    dy_a = matmul(dta, W_sup, mode="nt", out_dtype=BF16, name="swa_up_dx", tm=2048, tk=d, b_cols=True)
    dy_b = matmul(dtb, W_gup, mode="nt", out_dtype=BF16, name="gdn_up_dx", tm=2048, tk=d, b_cols=True)
    dy_c = matmul(dtc, W_xup, mode="nt", out_dtype=BF16, name="xa_up_dx", tm=2048, tk=d, b_cols=True)
    dW_sup = matmul(y_a, dta, mode="tn", out_dtype=BF16, name="swa_up_dw", tn=2048, tk=2048)
    dW_gup = matmul(y_b, dtb, mode="tn", out_dtype=BF16, name="gdn_up_dw", tn=2048, tk=2048)
    dW_xup = matmul(y_c, dtc, mode="tn", out_dtype=BF16, name="xa_up_dw", tn=2048, tk=2048)

    dq_a, dk_a, dv_a, dsinks = swa_bwd(p_all, sinks, dy_a, lay, "swa_bwd")
    dq_c, dmkv = xattn_bwd(p_all, mkv, dy_c, lay, "xattn_bwd")
    dW_mkv = matmul(nm, dmkv, mode="tn", out_dtype=BF16, name="proj_mem_dw", tk=256)
    dnm = matmul(dmkv, W_mkv, mode="nt", out_dtype=F32, name="proj_mem_dx", tk=1024)
    _, dg_mem = rmsnorm_bwd(ms, g_mem, dnm, None, "norm_mem_bwd")

    du_g, dw_g, dqd_g, dkd_g, dqk_g, dgl_g, dz, dnorm_w = gdn_scan_bwd(
        u, gw, gqd, gkd, gqk, ggl, states, o_b, dy_b, p_all, gdn_norm_w, lay, "gdn_scan_bwd")
    dqkv, dab, dgp, dconv = gdn_pre_bwd(p_all, convw, gp, du_g, dw_g, dqd_g, dkd_g, dqk_g, dgl_g, lay, "gdn_pre_bwd")

    drest = jnp.concatenate([dq_a, dqkv, dz, dq_c, dk_a, dv_a, dab, jnp.zeros((s, lay.pw - lay.end), BF16)], axis=1)
    def pair_stage(grads, cols, tag):
        from_sib = run_job(PairExchangeJob(grads, cols), "grads_pair_exchange_" + tag)
        return [pair_add(g, cl, o, "grads_pair_add_%s%d" % (tag, i)) for i, (g, cl, o) in enumerate(zip(grads, cols, from_sib))]

    small = pair_stage([_rows_to_blocks(dW_mkv), dW_sup, dW_gup, dW_xup, _rows_to_blocks(dW_out)],
                       [False, True, True, True, False], "a")
    dW_in, p_m1, p_m2 = matmul(n1, dgates, tail=drest, mode="tn", out_dtype=BF16, name="proj_in_dw", tm=2048, tn=1024, tk=1024,
                               side=ChipExchangeJob([c_m1, c_m2]))
    late = pair_stage([unpad_dw_in(dW_in, lay)], [False], "b")
    dn1, p_in, p_mkv, p_sup, p_gup, p_xup, p_out = matmul(
        dgates, W_in, tail=drest, mode="nt", out_dtype=F32, name="proj_in_dx", tm=1024, tn=2048, tk=1024,
        side=ChipExchangeJob(late + small))
    grad_x, dg_mix = rmsnorm_bwd(xs, g_mix, dn1, dh1, "norm_mix_bwd")
    parts = [p_in, p_mkv, p_sup, p_gup, p_xup, p_out, p_m1, p_m2]

    shard_names = [(w_in, m_w_in, v_w_in), (w_mem_kv, m_w_mem_kv, v_w_mem_kv), (w_swa_up, m_w_swa_up, v_w_swa_up),
                   (w_gdn_up, m_w_gdn_up, v_w_gdn_up), (w_xa_up, m_w_xa_up, v_w_xa_up), (w_out, m_w_out, v_w_out),
                   (w_mlp_in, m_w_mlp_in, v_w_mlp_in), (w_mlp_out, m_w_mlp_out, v_w_mlp_out)]
    big_res = [adamw(p, w[0], m[0], v[0], "adamw_%d" % i) for i, (p, (w, m, v)) in enumerate(zip(parts, shard_names))]

    smalls = [(g_mix, m_g_mix, v_g_mix, dg_mix), (sinks, m_sinks, v_sinks, dsinks[:, :SWA_HQ]),
              (a_log, m_a_log, v_a_log, dgp[0:1, :GDN_H]), (dt_bias, m_dt_bias, v_dt_bias, dgp[1:2, :GDN_H]),
              (gdn_norm_w, m_gdn_norm_w, v_gdn_norm_w, dnorm_w), (g_mem, m_g_mem, v_g_mem, dg_mem),
              (g_mlp, m_g_mlp, v_g_mlp, dg_mlp), (g_final, m_g_final, v_g_final, dg_final)]
    sizes = [-(-t[0].size // LANE) * LANE for t in smalls] + [GDN_CONV * 3 * GDN_W]
    rows = -(-sum(sizes) // (8 * LANE)) * 8
    csh = conv_w.shape[2]

    def conv_place(a):
        full = jnp.tile(a[0], (1, N_DEV))
        owner = lax.broadcasted_iota(jnp.int32, full.shape, 1) // csh
        return jnp.where(owner == dev, full, 0.0)

    g_pack = _pack_small([t[3] for t in smalls] + [dconv], rows)
    w_pack = _pack_small([t[0] for t in smalls] + [conv_place(conv_w)], rows)
    m_pack = _pack_small([t[1] for t in smalls] + [conv_place(m_conv_w)], rows)
    v_pack = _pack_small([t[2] for t in smalls] + [conv_place(v_conv_w)], rows)
    g_all = run_job(GatherJob([g_pack]), "gather_small_grads")[0]
    small_res = adamw(g_all, w_pack, m_pack, v_pack, "adamw_small")

    def unpack(arr):
        flat = arr.reshape(-1)
        outs, off = [], 0
        for t, sz in zip(smalls, sizes[:-1]):
            outs.append(flat[off:off + t[0].size].reshape(t[0].shape))
            off += sz
        cw = flat[off:off + sizes[-1]].reshape(GDN_CONV, 3 * GDN_W)
        mine = (lax.broadcasted_iota(jnp.int32, (1, N_DEV, 1), 1) == dev).astype(F32)
        outs.append(jnp.sum(cw.reshape(GDN_CONV, N_DEV, csh) * mine, axis=1)[None])
        return outs

    sg, sd, sm, sv = (unpack(a) for a in small_res)
    bg, bd, bm, bv = ([r[i][None] for r in big_res] for i in range(4))

    def ordered(sm_, bg_):
        return [sm_[0], bg_[0], sm_[1], sm_[8], sm_[2], sm_[3], sm_[4], sm_[5], bg_[1], bg_[2], bg_[3], bg_[4], bg_[5],
                sm_[6], bg_[6], bg_[7], sm_[7]]

    return (loss, grad_x[None], *ordered(sg, bg), *ordered(sd, bd), *ordered(sm, bm), *ordered(sv, bv))
```

```python
import functools

import jax
import jax.numpy as jnp
from jax import lax
from jax.experimental import pallas as pl
from jax.experimental.pallas import tpu as pltpu

F32, BF16 = jnp.float32, jnp.bfloat16
SDS = jax.ShapeDtypeStruct
MESH = pl.DeviceIdType.MESH
ANY = pl.BlockSpec(memory_space=pl.ANY)

SWA_HQ, SWA_HKV, SWA_HD, SWA_W = 16, 2, 64, 128
SWA_G = SWA_HQ // SWA_HKV
GDN_H, GDN_D, GDN_CONV, GDN_C = 4, 128, 4, 64
XA_H, XA_D = 4, 128
Q_W = SWA_HQ * SWA_HD
KV_W = SWA_HKV * SWA_HD
GDN_W = GDN_H * GDN_D
XA_W = XA_H * XA_D
RMS_EPS = 1e-6
L2_EPS = 1e-6
NEG = -1e30
N_DEV = 8
LANE = 128

ADAM_LR, ADAM_B1, ADAM_B2, ADAM_EPS, ADAM_WD, ADAM_STEP = 0.001, 0.9, 0.999, 1e-08, 0.01, 10

VMEM_BIG = 56 * 1024 * 1024


def _cp(sem, vmem=VMEM_BIG):
    return pltpu.CompilerParams(dimension_semantics=sem, vmem_limit_bytes=vmem)


def _div(a, b):
    assert a % b == 0, (a, b)
    return a // b


def _tile(n, t):
    t = min(t, n)
    assert n % t == 0, (n, t)
    return t


def _sigmoid(x):
    return jax.nn.sigmoid(x)


def _silu(x):
    return x * _sigmoid(x)


def _softplus(x):
    return jnp.maximum(x, 0.0) + jnp.log1p(jnp.exp(-jnp.abs(x)))


def _dot(a, b, dims, prec=None):
    return lax.dot_general(a, b, (dims, ((), ())), precision=prec, preferred_element_type=F32)


NN = ((1,), (0,))
NT = ((1,), (1,))
TN = ((0,), (0,))
HI = lax.Precision.HIGHEST


def _bdot_plain(a, b, dims):
    return _dot(a.astype(BF16), b.astype(BF16), dims)


@functools.partial(jax.custom_vjp, nondiff_argnums=(2,))
def _bdot_vjp(a, b, dims):
    return _bdot_plain(a, b, dims)


def _bdot_vjp_fwd(a, b, dims):
    return _bdot_plain(a, b, dims), (a, b)


def _bdot_vjp_bwd(dims, res, ct):
    a, b = res
    if dims == NN:
        return _bdot_plain(ct, b, NT), _bdot_plain(a, ct, TN)
    assert dims == NT, dims
    return _bdot_plain(ct, b, NN), _bdot_plain(ct, a, TN)


_bdot_vjp.defvjp(_bdot_vjp_fwd, _bdot_vjp_bwd)


def _neumann(xs):
    pws, nns = list(xs), list(xs)
    for _ in range(5):
        pws = [_bdot_plain(p, p, NN) for p in pws]
        nns = [n + p + _bdot_plain(n, p, NN) for n, p in zip(nns, pws)]
    return tuple(nns)


@jax.custom_vjp
def _neumann_vjp(xs):
    return _neumann(xs)


def _neumann_vjp_fwd(xs):
    nns = _neumann(xs)
    return nns, nns


def _neumann_vjp_bwd(nns, cts):
    ts = [ct + _bdot_plain(nn, ct, TN) for nn, ct in zip(nns, cts)]
    return (tuple(t + _bdot_plain(t, nn, NT) for t, nn in zip(ts, nns)),)


_neumann_vjp.defvjp(_neumann_vjp_fwd, _neumann_vjp_bwd)


class Layout:
    def __init__(self, d):
        self.d = d
        self.g = 0
        self.q = 3 * d
        self.qkv = self.q + Q_W
        self.z = self.qkv + 3 * GDN_W
        self.qc = self.z + GDN_W
        self.k = self.qc + XA_W
        self.v = self.k + KV_W
        self.ab = self.v + KV_W
        self.end = self.ab + LANE
        self.pw = -(-self.end // 1024) * 1024
        self.lq, self.lk, self.lv, self.lqkv = 0, Q_W, Q_W + KV_W, Q_W + 2 * KV_W
        self.la = self.lqkv + 3 * GDN_W
        self.lz = self.la + 2 * GDN_H
        self.lqc = self.lz + GDN_W
        self.lg = self.lqc + XA_W
        self.lw = self.lg + 3 * d

    def pieces(self):
        segs = [(self.lq, self.lk, self.q), (self.lk, self.lv, self.k), (self.lv, self.lqkv, self.v),
                (self.lqkv, self.la, self.qkv), (self.la, self.lz, self.ab), (self.lz, self.lqc, self.z),
                (self.lqc, self.lg, self.qc), (self.lg, self.lw, self.g)]
        cw = _div(self.lw, N_DEV)
        out = []
        for dev in range(N_DEV):
            lo, hi = dev * cw, (dev + 1) * cw
            for ls, le, ps in segs:
                s, e = max(lo, ls), min(hi, le)
                if s < e:
                    out.append((dev, s - lo, ps + s - ls, e - s))
        return out


def pad_w_in(g, lay):
    nd, k, cw = g.shape
    tr = _tile(k, 256)
    tail = lay.ab + 2 * GDN_H

    def body(g_ref, o_ref):
        o_ref[:, pl.ds(tail, lay.pw - tail)] = jnp.zeros((tr, lay.pw - tail), o_ref.dtype)
        for dev, so, po, ln in lay.pieces():
            o_ref[:, pl.ds(po, ln)] = g_ref[dev, :, pl.ds(so, ln)]

    return pl.pallas_call(
        body, name="pad_w_in", grid=(k // tr,), in_specs=[pl.BlockSpec((nd, tr, cw), lambda i: (0, i, 0))],
        out_specs=pl.BlockSpec((tr, lay.pw), lambda i: (i, 0)), out_shape=SDS((k, lay.pw), g.dtype),
        compiler_params=_cp(("parallel",)))(g)


def unpad_dw_in(dw, lay):
    k = dw.shape[0]
    cw = _div(lay.lw, N_DEV)
    tr = _tile(k, 256)

    def body(d_ref, o_ref):
        for dev, so, po, ln in lay.pieces():
            o_ref[dev, :, pl.ds(so, ln)] = d_ref[:, pl.ds(po, ln)]

    return pl.pallas_call(
        body, name="unpad_dw_in", grid=(k // tr,), in_specs=[pl.BlockSpec((tr, lay.pw), lambda i: (i, 0))],
        out_specs=pl.BlockSpec((N_DEV, tr, cw), lambda i: (0, i, 0)), out_shape=SDS((N_DEV, k, cw), dw.dtype),
        compiler_params=_cp(("parallel",)))(dw)


def _position():
    return lax.axis_index("x"), lax.axis_index("y"), lax.axis_index("c")


class GatherJob:
    def __init__(self, arrs):
        self.ins = list(arrs)
        n = len(arrs)
        self.out_shapes = [SDS((N_DEV,) + a.shape, a.dtype) for a in arrs]
        self.scratch = [pltpu.SemaphoreType.DMA((n, 7)), pltpu.SemaphoreType.DMA((n, 7)), pltpu.SemaphoreType.DMA((n,))]

    def _ctx(self, outs, sems):
        send_sems, recv_sems, _ = sems
        x, y, c = _position()

        def blk(o, p):
            return o.at[4 * p[0] + 2 * p[1] + p[2]]

        def copy(i, k, block, to, src=None):
            return pltpu.make_async_remote_copy(
                src_ref=blk(outs[i], block) if src is None else src, dst_ref=blk(outs[i], block),
                send_sem=send_sems.at[i, k], recv_sem=recv_sems.at[i, k], device_id=to, device_id_type=MESH)

        return (x, y, c), (x, y, 1 - c), [(1 - x, y), (x, 1 - y), (1 - x, 1 - y)], blk, copy

    def start(self, ins, outs, sems):
        me, sibling, chips, blk, copy = self._ctx(outs, sems)
        for i in range(len(ins)):
            pltpu.make_async_copy(ins[i], blk(outs[i], me), sems[2].at[i]).start()
            copy(i, 0, me, sibling, src=ins[i]).start()
            for j, chip in enumerate(chips[:2]):
                copy(i, 1 + j, me, (*chip, me[2]), src=ins[i]).start()

    def _relay(self, i, outs, sems, onward):
        me, _, _, blk, copy = self._ctx(outs, sems)
        x, y, c = me
        origin = (x + (1 - c) - 2 * x * (1 - c), y + c - 2 * y * c, c)
        if not onward:
            return copy(i, 1 + c, origin, me)
        return copy(i, 3, origin, (x + c - 2 * x * c, y + (1 - c) - 2 * y * (1 - c), c))

    def relay(self, ins, outs, sems):
        for i in range(len(ins)):
            self._relay(i, outs, sems, False).wait_recv()
            self._relay(i, outs, sems, True).start()

    def mid(self, ins, outs, sems):
        me, sibling, chips, blk, copy = self._ctx(outs, sems)
        for i in range(len(ins)):
            for j, chip in enumerate(chips):
                arrival = copy(i, 1 + j, (*chip, me[2]), me)
                if j < 2:
                    pl.when(me[2] != j)(arrival.wait_recv)
                else:
                    arrival.wait_recv()
                copy(i, 4 + j, (*chip, me[2]), sibling).start()

    def finish(self, ins, outs, sems):
        me, sibling, chips, blk, copy = self._ctx(outs, sems)
        for i in range(len(ins)):
            copy(i, 0, sibling, me).wait_recv()
            for j, chip in enumerate(chips):
                copy(i, 4 + j, (*chip, 1 - me[2]), me).wait_recv()
        for i in range(len(ins)):
            pltpu.make_async_copy(ins[i], blk(outs[i], me), sems[2].at[i]).wait()
            copy(i, 0, me, sibling, src=ins[i]).wait_send()
            for j, chip in enumerate(chips[:2]):
                copy(i, 1 + j, me, (*chip, me[2]), src=ins[i]).wait_send()
            self._relay(i, outs, sems, True).wait_send()
            for j, chip in enumerate(chips):
                copy(i, 4 + j, (*chip, me[2]), sibling).wait_send()


class ChipExchangeJob:
    mid = None

    def __init__(self, arrs):
        self.ins = list(arrs)
        n = len(arrs)
        self.out_shapes = [SDS(a.shape, a.dtype) for a in arrs]
        self.scratch = [pltpu.SemaphoreType.DMA((n, 3)), pltpu.SemaphoreType.DMA((n, 3)), pltpu.SemaphoreType.DMA((n,))]

    def _copies(self, ins, outs, sems, i, arrivals):
        send_sems, recv_sems, local_sems = sems
        x, y, c = _position()
        my_chip = 2 * x + y
        chips = [(1 - x, y), (x, 1 - y), (1 - x, 1 - y)]
        if arrivals:
            return [pltpu.make_async_remote_copy(
                src_ref=ins[i].at[my_chip], dst_ref=outs[i].at[2 * px + py], send_sem=send_sems.at[i, k],
                recv_sem=recv_sems.at[i, k], device_id=(px, py, c), device_id_type=MESH) for k, (px, py) in enumerate(chips)]
        local = pltpu.make_async_copy(ins[i].at[my_chip], outs[i].at[my_chip], local_sems.at[i])
        return local, [pltpu.make_async_remote_copy(
            src_ref=ins[i].at[2 * px + py], dst_ref=outs[i].at[my_chip], send_sem=send_sems.at[i, k],
            recv_sem=recv_sems.at[i, k], device_id=(px, py, c), device_id_type=MESH) for k, (px, py) in enumerate(chips)]

    def start(self, ins, outs, sems):
        for i in range(len(ins)):
            local, remote = self._copies(ins, outs, sems, i, False)
            local.start()
            for cp in remote:
                cp.start()

    def finish(self, ins, outs, sems):
        for i in range(len(ins)):
            for cp in self._copies(ins, outs, sems, i, True):
                cp.wait_recv()
            local, remote = self._copies(ins, outs, sems, i, False)
            for cp in remote:
                cp.wait_send()
            local.wait()


def _slab_shape(g, cols):
    return (g.shape[0], _div(g.shape[1], N_DEV)) if cols else g.shape[1:]


class PairExchangeJob:
    mid = None

    def __init__(self, grads, cols):
        self.ins, self.cols = list(grads), list(cols)
        n = len(grads)
        self.out_shapes = [SDS((4,) + _slab_shape(g, cl), g.dtype) for g, cl in zip(grads, cols)]
        self.scratch = [pltpu.SemaphoreType.DMA((n, 4)), pltpu.SemaphoreType.DMA((n, 4))]

    def _copies(self, ins, outs, sems):
        send_sems, recv_sems = sems
        x, y, c = _position()

        def part(i, dst):
            if not self.cols[i]:
                return ins[i].at[dst]
            cw = _slab_shape(self.ins[i], True)[1]
            return ins[i].at[:, pl.ds(pl.multiple_of(dst * cw, LANE), cw)]

        return [pltpu.make_async_remote_copy(src_ref=part(i, 2 * j + 1 - c), dst_ref=outs[i].at[j], send_sem=send_sems.at[i, j],
                                             recv_sem=recv_sems.at[i, j], device_id=(x, y, 1 - c), device_id_type=MESH)
                for i in range(len(ins)) for j in range(4)]

    def start(self, ins, outs, sems):
        for cp in self._copies(ins, outs, sems):
            cp.start()

    def finish(self, ins, outs, sems):
        for cp in self._copies(ins, outs, sems):
            cp.wait()


def _host_begin(job, step, steps, ins, outs, sems):
    pl.when(step == 0)(lambda: job.start(ins, outs, sems))
    if job.mid is not None:
        pl.when(step == (steps * 45) // 100)(lambda: job.relay(ins, outs, sems))
        pl.when(step == (steps * 85) // 100)(lambda: job.mid(ins, outs, sems))


def run_job(job, name):
    n = len(job.ins)

    def body(*refs):
        ins, outs, sems = refs[:n], refs[n:2 * n], refs[2 * n:]
        job.start(ins, outs, sems)
        if job.mid is not None:
            job.relay(ins, outs, sems)
            job.mid(ins, outs, sems)
        job.finish(ins, outs, sems)

    return pl.pallas_call(body, name=name, out_shape=job.out_shapes, in_specs=[ANY] * n, out_specs=[ANY] * n,
                          scratch_shapes=job.scratch)(*job.ins)


def pair_add(grad, cols, other, name):
    r, c = _slab_shape(grad, cols)
    tr = _tile(r, 256)
    parity = lax.axis_index("c").astype(jnp.int32).reshape(1)

    def body(par_ref, a_ref, b_ref, o_ref):
        o_ref[...] = (a_ref[...].astype(F32) + b_ref[...].astype(F32)).astype(BF16)

    spec = pl.BlockSpec((None, tr, c), lambda j, i, par: (j, i, 0))
    if cols:
        own = pl.BlockSpec((tr, c), lambda j, i, par: (i, 2 * j + par[0]))
    else:
        own = pl.BlockSpec((None, tr, c), lambda j, i, par: (2 * j + par[0], i, 0))
    return pl.pallas_call(
        body, name=name, out_shape=SDS(other.shape, BF16),
        grid_spec=pltpu.PrefetchScalarGridSpec(num_scalar_prefetch=1, grid=(4, r // tr), in_specs=[own, spec], out_specs=spec),
        compiler_params=_cp(("parallel", "parallel")))(parity, grad, other)


def adamw(parts, w, m, v, name):
    p, r, c = parts.shape
    tr = _tile(r, 128 if c > 1024 else 256)

    def body(p_ref, w_ref, m_ref, v_ref, g_out, d_out, m_out, v_out):
        g = p_ref[0].astype(F32)
        for j in range(1, p):
            g = g + p_ref[j].astype(F32)
        mn = ADAM_B1 * m_ref[...] + (1.0 - ADAM_B1) * g
        vn = ADAM_B2 * v_ref[...] + (1.0 - ADAM_B2) * jnp.square(g)
        m_hat = mn / (1.0 - ADAM_B1 ** ADAM_STEP)
        v_hat = vn / (1.0 - ADAM_B2 ** ADAM_STEP)
        g_out[...] = g
        d_out[...] = -ADAM_LR * (m_hat / (jnp.sqrt(v_hat) + ADAM_EPS) + ADAM_WD * w_ref[...])
        m_out[...] = mn
        v_out[...] = vn

    spec = pl.BlockSpec((tr, c), lambda i: (i, 0))
    return pl.pallas_call(
        body, name=name, grid=(r // tr,),
        in_specs=[pl.BlockSpec((p, tr, c), lambda i: (0, i, 0)), spec, spec, spec],
        out_specs=[spec] * 4, out_shape=[SDS((r, c), F32)] * 4, compiler_params=_cp(("parallel",)))(parts, w, m, v)


def matmul(a, b, *, mode, out_dtype, name, tm=1024, tn=1024, tk=512, a_relu2=False, resid=None, relu2_grad_of=None,
           b_cols=False, relu2_out=False, rms_gain=None, loss_target=None, side=None, tail=None):
    if b_cols:
        nb, brows, bc = b.shape
        bshape = (brows, nb * bc)
    else:
        bshape = b.shape
    head_k = head_n = None
    if mode == "nn":
        (m, k), (k2, n) = a.shape, bshape
    elif mode == "nt":
        (m, k), (n, k2) = a.shape, bshape
        if tail is not None:
            head_k, k = k, k + tail.shape[1]
    else:
        (k, m), (k2, n) = a.shape, bshape
        if tail is not None:
            head_n, n = n, n + tail.shape[1]
    assert k == k2 and (tail is None or mode != "nn"), (a.shape, b.shape, mode)
    b_whole = b_cols and mode == "nt" and tk >= k
    tm, tn, tk = _tile(m, tm), _tile(n, tn), _tile(k, tk)
    if b_cols and mode == "nn":
        tn = _tile(bc, tn)
    if b_cols and mode == "nt" and not b_whole:
        tk = _tile(bc, tk)
    nk = k // tk
    ni, nj = m // tm, n // tn
    nk_head = _div(head_k, tk) if head_k is not None else None
    nj_head = _div(head_n, tn) if head_n is not None else None
    use_acc = nk > 1 or tail is not None
    dims = {"nn": NN, "nt": NT, "tn": TN}[mode]
    extras = [e for e in (resid, relu2_grad_of) if e is not None]
    tails = [tail] if tail is not None else []
    n_side = len(side.ins) if side is not None else 0
    loss = loss_target is not None
    if loss:
        extras.append(loss_target)
    gains = [rms_gain] if rms_gain is not None else []
    assert not gains or (tn == n and not relu2_out), (tn, n)
    assert not loss or (gains and resid is not None and relu2_grad_of is None and out_dtype == F32)
    n_main = 2 if (relu2_out or gains) else 1
    n_loss = 2 if loss else 0

    def body(*refs):
        a_ref, b_ref = refs[:2]
        t_ref = refs[2] if tails else None
        n_op = 2 + len(tails)
        e_refs = refs[n_op:n_op + len(extras)]
        n_pre = n_op + len(extras) + len(gains)
        g_ref = refs[n_pre - 1] if gains else None
        n_in = n_pre + n_side
        o_ref = refs[n_in]
        act_ref = refs[n_in + 1] if n_main == 2 else None
        dg_ref, l_ref = refs[n_in + n_main:n_in + n_main + n_loss] if loss else (None, None)
        acc_ref = refs[n_in + n_main + n_loss + n_side] if use_acc else None
        if side is not None:
            s_ins = refs[n_pre:n_in]
            s_outs = refs[n_in + n_main + n_loss:n_in + n_main + n_loss + n_side]
            s_sems = refs[len(refs) - len(side.scratch):]
            step = (pl.program_id(0) * nj + pl.program_id(1)) * nk + pl.program_id(2)
            _host_begin(side, step, ni * nj * nk, s_ins, s_outs, s_sems)

        def operands(a_from=a_ref, b_from=b_ref):
            av = a_from[...]
            if a_relu2:
                av = jnp.square(jnp.maximum(av.astype(F32), 0.0))
            return av.astype(BF16), b_from[...].astype(BF16)

        def finish(r):
            e = list(e_refs)
            if resid is not None:
                r = r + e.pop(0)[...]
            if relu2_grad_of is not None:
                r = r * (2.0 * jnp.maximum(e.pop(0)[...], 0.0))
            if loss:
                gv = g_ref[...]
                inv = lax.rsqrt(jnp.mean(r * r, axis=-1, keepdims=True) + RMS_EPS)
                err = r * inv * gv - e.pop(0)[...]
                lpart = 0.5 * jnp.sum(jnp.mean(err * err, axis=-1, keepdims=True), axis=0, keepdims=True)
                dx, part = _rms_bwd_rows(r, gv, err * (1.0 / n))
                o_ref[...] = dx
                act_ref[...] = dx.astype(BF16)

                @pl.when(pl.program_id(0) == 0)
                def _():
                    dg_ref[...] = jnp.zeros_like(dg_ref)
                    l_ref[...] = jnp.zeros_like(l_ref)

                dg_ref[...] += part
                l_ref[...] += jnp.broadcast_to(lpart, l_ref.shape)
                return
            o_ref[...] = r.astype(out_dtype)
            if relu2_out:
                act_ref[...] = jnp.square(jnp.maximum(r, 0.0)).astype(BF16)
            if gains:
                inv = lax.rsqrt(jnp.mean(r * r, axis=-1, keepdims=True) + RMS_EPS)
                act_ref[...] = (r * inv * g_ref[...]).astype(BF16)

        if b_whole:
            av = a_ref[...].astype(BF16)
            finish(sum(_dot(av[:, kb * bc:(kb + 1) * bc], b_ref[kb].astype(BF16), NT) for kb in range(nb)))
        elif not use_acc:
            av, bv = operands()
            finish(_dot(av, bv, dims))
        else:
            kk = pl.program_id(2)

            def accumulate(a_from, b_from):
                def product():
                    av, bv = operands(a_from, b_from)
                    return _dot(av, bv, dims)

                if nk == 1:
                    finish(product())
                    return

                @pl.when(kk == 0)
                def _():
                    acc_ref[...] = product()

                @pl.when((kk > 0) & (kk < nk - 1))
                def _():
                    acc_ref[...] += product()

                @pl.when(kk == nk - 1)
                def _():
                    finish(acc_ref[...] + product())

            if not tails:
                accumulate(a_ref, b_ref)
            elif mode == "nt":
                pl.when(kk < nk_head)(lambda: accumulate(a_ref, b_ref))
                pl.when(kk >= nk_head)(lambda: accumulate(t_ref, b_ref))
            else:
                in_head = pl.program_id(1) < nj_head
                pl.when(in_head)(lambda: accumulate(a_ref, b_ref))
                pl.when(jnp.logical_not(in_head))(lambda: accumulate(a_ref, t_ref))

        if side is not None:
            pl.when(step == ni * nj * nk - 1)(lambda: side.finish(s_ins, s_outs, s_sems))

    a_spec = {"nn": pl.BlockSpec((tm, tk), lambda i, j, kk: (i, kk)),
              "nt": pl.BlockSpec((tm, tk), lambda i, j, kk: (i, kk)),
              "tn": pl.BlockSpec((tk, tm), lambda i, j, kk: (kk, i))}[mode]
    t_specs = []
    if tails and mode == "nt":
        a_spec = pl.BlockSpec((tm, tk), lambda i, j, kk: (i, jnp.minimum(kk, nk_head - 1)))
        t_specs = [pl.BlockSpec((tm, tk), lambda i, j, kk: (i, jnp.maximum(kk - nk_head, 0)))]
    if tails and mode == "tn":
        t_specs = [pl.BlockSpec((tk, tn), lambda i, j, kk: (kk, jnp.maximum(j - nj_head, 0)))]
    if tails and mode == "tn":
        b_spec = pl.BlockSpec((tk, tn), lambda i, j, kk: (kk, jnp.minimum(j, nj_head - 1)))
    elif not b_cols:
        b_spec = {"nn": pl.BlockSpec((tk, tn), lambda i, j, kk: (kk, j)),
                  "nt": pl.BlockSpec((tn, tk), lambda i, j, kk: (j, kk)),
                  "tn": pl.BlockSpec((tk, tn), lambda i, j, kk: (kk, j))}[mode]
    elif mode == "nn":
        per = bc // tn
        b_spec = pl.BlockSpec((None, tk, tn), lambda i, j, kk: (j // per, kk, j % per))
    elif b_whole:
        b_spec = pl.BlockSpec((nb, tn, bc), lambda i, j, kk: (0, j, 0))
    else:
        assert mode == "nt", mode
        per = bc // tk
        b_spec = pl.BlockSpec((None, tn, tk), lambda i, j, kk: (kk // per, j, kk % per))
    e_spec = pl.BlockSpec((tm, tn), lambda i, j, kk: (i, j))
    main_shapes = [SDS((m, n), out_dtype)] + ([SDS((m, n), BF16)] if n_main == 2 else [])
    g_specs = [pl.BlockSpec((1, tn), lambda i, j, kk: (0, j))] * len(gains)
    l_specs = [pl.BlockSpec((1, tn), lambda i, j, kk: (0, j)), pl.BlockSpec((1, LANE), lambda i, j, kk: (0, 0))] if loss else []
    l_shapes = [SDS((1, n), F32), SDS((1, LANE), F32)] if loss else []
    res = pl.pallas_call(
        body, name=name, grid=(ni, nj, nk),
        in_specs=[a_spec, b_spec] + t_specs + [e_spec] * len(extras) + g_specs + [ANY] * n_side,
        out_specs=[e_spec] * n_main + l_specs + [ANY] * n_side,
        out_shape=main_shapes + l_shapes + (side.out_shapes if side is not None else []),
        scratch_shapes=([pltpu.VMEM((tm, tn), F32)] if use_acc else []) + (side.scratch if side is not None else []),
        compiler_params=_cp(("arbitrary", "arbitrary", "arbitrary")))(a, b, *tails, *extras, *gains, *(side.ins if side is not None else []))
    return res if len(res) > 1 else res[0]


def rmsnorm_fwd(x, g, name, side=None):
    s, d = x.shape
    tm = _tile(s, 256)
    steps = s // tm
    n_side = len(side.ins) if side is not None else 0

    def body(*refs):
        x_ref, g_ref, o_ref = refs[0], refs[1], refs[2 + n_side]
        if side is not None:
            s_ins, s_outs, s_sems = refs[2:2 + n_side], refs[3 + n_side:3 + 2 * n_side], refs[3 + 2 * n_side:]
            _host_begin(side, pl.program_id(0), steps, s_ins, s_outs, s_sems)
        xv = x_ref[...]
        r = lax.rsqrt(jnp.mean(xv * xv, axis=-1, keepdims=True) + RMS_EPS)
        o_ref[...] = (xv * r * g_ref[...]).astype(BF16)
        if side is not None:
            pl.when(pl.program_id(0) == steps - 1)(lambda: side.finish(s_ins, s_outs, s_sems))

    row = pl.BlockSpec((tm, d), lambda i: (i, 0))
    res = pl.pallas_call(
        body, name=name, grid=(steps,), in_specs=[row, pl.BlockSpec((1, d), lambda i: (0, 0))] + [ANY] * n_side,
        out_specs=[row] + [ANY] * n_side, out_shape=[SDS((s, d), BF16)] + (side.out_shapes if side is not None else []),
        scratch_shapes=side.scratch if side is not None else [],
        compiler_params=_cp(("arbitrary",)))(x, g, *(side.ins if side is not None else []))
    return res if side is not None else res[0]


def _rms_bwd_rows(xv, gv, dy):
    r = lax.rsqrt(jnp.mean(xv * xv, axis=-1, keepdims=True) + RMS_EPS)
    xh = xv * r
    dxh = dy * gv
    dx = r * (dxh - xh * jnp.mean(dxh * xh, axis=-1, keepdims=True))
    return dx, jnp.sum(dy * xh, axis=0, keepdims=True)


def rmsnorm_bwd(x, g, dn, resid, name, bf16_copy=False):
    s, d = x.shape
    tm = _tile(s, 256)
    has_r = resid is not None

    def body(*refs):
        x_ref, g_ref, dn_ref = refs[:3]
        dx_ref, dg_ref = refs[3 + has_r:5 + has_r]
        dx, part = _rms_bwd_rows(x_ref[...], g_ref[...], dn_ref[...].astype(F32))
        if has_r:
            dx = dx + refs[3][...]
        dx_ref[...] = dx
        if bf16_copy:
            refs[5 + has_r][...] = dx.astype(BF16)

        @pl.when(pl.program_id(0) == 0)
        def _():
            dg_ref[...] = jnp.zeros_like(dg_ref)

        dg_ref[...] += part

    row = pl.BlockSpec((tm, d), lambda i: (i, 0))
    vec = pl.BlockSpec((1, d), lambda i: (0, 0))
    ins = [x, g, dn] + ([resid] if has_r else [])
    return pl.pallas_call(body, name=name, grid=(s // tm,), in_specs=[row, vec, row] + ([row] if has_r else []),
                          out_specs=[row, vec] + ([row] if bf16_copy else []),
                          out_shape=[SDS((s, d), F32), SDS((1, d), F32)] + ([SDS((s, d), BF16)] if bf16_copy else []),
                          compiler_params=_cp(("arbitrary",)))(*ins)


def merge(p_all, ya, yb, yc, wa, wb, wc, dm, lay, name):
    s, d = ya.shape[0], lay.d
    wcols = wa.shape[2]
    bwd = dm is not None
    tm, tn = _tile(s, 2048), _tile(wcols, 512)
    nj, per = d // tn, wcols // tn

    y_specs = [pl.BlockSpec((tm, y.shape[1]), lambda i, j, *_: (i, 0)) for y in (ya, yb, yc)]
    w_specs = [pl.BlockSpec((None, w.shape[1], tn), lambda i, j, *_: (j // per, 0, j % per)) for w in (wa, wb, wc)]
    o_spec = pl.BlockSpec((tm, tn), lambda i, j, *_: (i, j))
    if not bwd:
        def body(ga, gb, gc, ya_r, yb_r, yc_r, wa_r, wb_r, wc_r, o_ref):
            ts = [_dot(y[...], w[...], NN) for y, w in ((ya_r, wa_r), (yb_r, wb_r), (yc_r, wc_r))]
            gs = [_sigmoid(g[...]) for g in (ga, gb, gc)]
            o_ref[...] = (gs[0] * ts[0] + gs[1] * ts[1] + gs[2] * ts[2]).astype(BF16)

        gate_specs = [pl.BlockSpec((tm, tn), lambda i, j, b=b: (i, b * nj + j)) for b in range(3)]
        return pl.pallas_call(
            body, name=name, grid=(s // tm, nj), in_specs=gate_specs + y_specs + w_specs,
            out_specs=o_spec, out_shape=SDS((s, d), BF16),
            compiler_params=_cp(("parallel", "parallel")))(p_all, p_all, p_all, ya, yb, yc, wa, wb, wc)

    def body_bwd(g_r, ya_r, yb_r, yc_r, wa_r, wb_r, wc_r, dm_r, dg_o, dta_o, dtb_o, dtc_o):
        for k, (y, w, dt_o) in enumerate(((ya_r, wa_r, dta_o), (yb_r, wb_r, dtb_o), (yc_r, wc_r, dtc_o))):
            @pl.when(pl.program_id(2) == k)
            def _(y=y, w=w, dt_o=dt_o):
                t = _dot(y[...], w[...], NN)
                g = _sigmoid(g_r[...])
                dmv = dm_r[...]
                dg_o[...] = (dmv * t * (g * (1.0 - g))).astype(BF16)
                dt_o[...] = (dmv * g).astype(BF16)

    gate_spec = pl.BlockSpec((tm, tn), lambda i, j, b: (i, b * nj + j))
    return pl.pallas_call(
        body_bwd, name=name, grid=(s // tm, nj, 3), in_specs=[gate_spec] + y_specs + w_specs + [o_spec],
        out_specs=[gate_spec, o_spec, o_spec, o_spec], out_shape=[SDS((s, 3 * d), BF16)] + [SDS((s, d), BF16)] * 3,
        compiler_params=_cp(("arbitrary", "arbitrary", "arbitrary")))(p_all, ya, yb, yc, wa, wb, wc, dm)


SWA_PAIRS = SWA_G // 2


def _swa_probs(qs, kcs, sinks, firsts):
    shape = (qs[0].shape[0], 2 * SWA_W)
    qi = lax.broadcasted_iota(jnp.int32, shape, 0) % SWA_W
    kj = lax.broadcasted_iota(jnp.int32, shape, 1)
    band = (kj > qi) & (kj <= qi + SWA_W)
    masks = [band if f is False else band & ((kj >= SWA_W) | jnp.logical_not(f)) for f in firsts]
    ss = [jnp.where(mask, _dot(q, kc, NT) * (SWA_HD ** -0.5), NEG) for q, kc, mask in zip(qs, kcs, masks)]
    ms = [jnp.maximum(jnp.max(s, axis=-1, keepdims=True), sink) for s, sink in zip(ss, sinks)]
    ps = [jnp.exp(s - m) for s, m in zip(ss, ms)]
    es = [jnp.exp(sink - m) for sink, m in zip(sinks, ms)]
    inv = [1.0 / (jnp.sum(p, axis=-1, keepdims=True) + e) for p, e in zip(ps, es)]
    return [p * i for p, i in zip(ps, inv)], [e * i for e, i in zip(es, inv)]


def _swa_stack(ref, h, r0=0):
    return jnp.concatenate([ref[pl.ds(r0, SWA_W), pl.ds((h * SWA_PAIRS + p) * LANE, LANE)] for p in range(SWA_PAIRS)], axis=0)


def _swa_unstack(ref, h, val, r0=0):
    for p in range(SWA_PAIRS):
        ref[pl.ds(r0, SWA_W), pl.ds((h * SWA_PAIRS + p) * LANE, LANE)] = val[p * SWA_W:(p + 1) * SWA_W]


def _swa_sink_col(sk_ref, h, second):
    pair = lax.broadcasted_iota(jnp.int32, (SWA_PAIRS * SWA_W, 1), 0) // SWA_W
    col = jnp.zeros((SWA_PAIRS * SWA_W, 1), F32)
    for p in range(SWA_PAIRS):
        hh = h * SWA_G + 2 * p + second
        col = jnp.where(pair == p, sk_ref[0:1, hh:hh + 1], col)
    return col


def _swa_kv_tiles(t, h):
    lane = lax.broadcasted_iota(jnp.int32, t.shape, 1)
    moved = pltpu.roll(t, SWA_HD, axis=1)
    low, high = (t, moved) if h == 0 else (moved, t)
    return jnp.where(lane < SWA_HD, low, 0.0).astype(BF16), jnp.where(lane >= SWA_HD, high, 0.0).astype(BF16)


def _swa_kv_grad(g_low, g_high, h):
    lane = lax.broadcasted_iota(jnp.int32, g_low.shape, 1)
    if h == 0:
        return jnp.where(lane < SWA_HD, g_low + pltpu.roll(g_high, SWA_HD, axis=1), 0.0)
    return jnp.where(lane >= SWA_HD, pltpu.roll(g_low, SWA_HD, axis=1) + g_high, 0.0)


def swa_fwd(p_all, sinks, lay, name):
    s = p_all.shape[0]
    w = SWA_W
    nb2 = _div(s, 2 * w)

    def body(q_ref, kc_ref, kp_ref, vc_ref, vp_ref, sk_ref, o_ref):
        first = pl.program_id(0) == 0
        kt = [jnp.concatenate([kp_ref[...], kc_ref[0:w, :]], axis=0), kc_ref[...]]
        vt = [jnp.concatenate([vp_ref[...], vc_ref[0:w, :]], axis=0), vc_ref[...]]
        units = [(b, h, e) for b in range(2) for h in range(SWA_HKV) for e in range(2)]
        ks = [[_swa_kv_tiles(kt[b], h) for h in range(SWA_HKV)] for b in range(2)]
        vs = [[_swa_kv_tiles(vt[b], h) for h in range(SWA_HKV)] for b in range(2)]
        qs = [[_swa_stack(q_ref, h, b * w).astype(BF16) for h in range(SWA_HKV)] for b in range(2)]
        ps, _ = _swa_probs([qs[b][h] for b, h, e in units], [ks[b][h][e] for b, h, e in units],
                           [_swa_sink_col(sk_ref, h, e) for b, h, e in units], [first if b == 0 else False for b, h, e in units])
        os = [_dot(p.astype(BF16), vs[b][h][e], NN) for p, (b, h, e) in zip(ps, units)]
        for u in range(0, len(units), 2):
            b, h, _ = units[u]
            _swa_unstack(o_ref, h, (os[u] + os[u + 1]).astype(BF16), b * w)

    q_spec = pl.BlockSpec((2 * w, Q_W), lambda n: (n, _div(lay.q, Q_W)))
    cur = lambda off: pl.BlockSpec((2 * w, KV_W), lambda n: (n, _div(off, KV_W)))
    prev = lambda off: pl.BlockSpec((w, KV_W), lambda n: (jnp.maximum(2 * n - 1, 0), _div(off, KV_W)))
    return pl.pallas_call(
        body, name=name, grid=(nb2,),
        in_specs=[q_spec, cur(lay.k), prev(lay.k), cur(lay.v), prev(lay.v), pl.BlockSpec(sinks.shape, lambda n: (0, 0))],
        out_specs=pl.BlockSpec((2 * w, Q_W), lambda n: (n, 0)), out_shape=SDS((s, Q_W), BF16),
        compiler_params=_cp(("parallel",)))(p_all, p_all, p_all, p_all, p_all, sinks)


def swa_bwd(p_all, sinks, dy, lay, name):
    s = p_all.shape[0]
    w = SWA_W
    nb = _div(s, w)
    nb2 = _div(nb, 2)

    def body(q_ref, kc_ref, kp_ref, vc_ref, vp_ref, sk_ref, do_ref,
             dq_ref, dke_ref, dko_ref, dve_ref, dvo_ref, ds_ref, kcar, vcar):
        n = pl.program_id(0)
        first = n == 0

        @pl.when(first)
        def _():
            kcar[...] = jnp.zeros_like(kcar)
            vcar[...] = jnp.zeros_like(vcar)
            ds_ref[...] = jnp.zeros_like(ds_ref)

        @pl.when(n < nb2)
        def _():
            lane = lax.broadcasted_iota(jnp.int32, (1, LANE), 1)
            dsink = jnp.zeros((1, LANE), F32)
            units = [(b, h, e) for b in range(2) for h in range(SWA_HKV) for e in range(2)]
            kt = [jnp.concatenate([kp_ref[...], kc_ref[0:w, :]], axis=0), kc_ref[...]]
            vt = [jnp.concatenate([vp_ref[...], vc_ref[0:w, :]], axis=0), vc_ref[...]]
            ks = [[_swa_kv_tiles(kt[b], h) for h in range(SWA_HKV)] for b in range(2)]
            vs = [[_swa_kv_tiles(vt[b], h) for h in range(SWA_HKV)] for b in range(2)]
            qs = [[_swa_stack(q_ref, h, b * w).astype(BF16) for h in range(SWA_HKV)] for b in range(2)]
            dos = [[_swa_stack(do_ref, h, b * w).astype(BF16) for h in range(SWA_HKV)] for b in range(2)]
            ps, psinks = _swa_probs([qs[b][h] for b, h, e in units], [ks[b][h][e] for b, h, e in units],
                                    [_swa_sink_col(sk_ref, h, e) for b, h, e in units],
                                    [first if b == 0 else False for b, h, e in units])
            dps = [_dot(dos[b][h], vs[b][h][e], NT) for b, h, e in units]
            dvs = [_dot(p.astype(BF16), dos[b][h], TN) for p, (b, h, e) in zip(ps, units)]
            rss = [jnp.sum(dp * p, axis=-1, keepdims=True) for dp, p in zip(dps, ps)]
            dsb = [(p * (dp - rs) * (SWA_HD ** -0.5)).astype(BF16) for p, dp, rs in zip(ps, dps, rss)]
            dqs = [_dot(d, ks[b][h][e], NN) for d, (b, h, e) in zip(dsb, units)]
            dks = [_dot(d, qs[b][h], TN) for d, (b, h, e) in zip(dsb, units)]
            for u, (b, h, e) in enumerate(units):
                psr = psinks[u] * rss[u]
                for pr in range(SWA_PAIRS):
                    hh = h * SWA_G + 2 * pr + e
                    dsink = dsink + jnp.where(lane == hh, -jnp.sum(psr[pr * w:(pr + 1) * w], axis=0, keepdims=True), 0.0)
            dk_tiles = [jnp.zeros((2 * w, KV_W), F32) for _ in range(2)]
            dv_tiles = [jnp.zeros((2 * w, KV_W), F32) for _ in range(2)]
            for u in range(0, len(units), 2):
                b, h, _ = units[u]
                _swa_unstack(dq_ref, h, (dqs[u] + dqs[u + 1]).astype(BF16), b * w)
                dk_tiles[b] = dk_tiles[b] + _swa_kv_grad(dks[u], dks[u + 1], h)
                dv_tiles[b] = dv_tiles[b] + _swa_kv_grad(dvs[u], dvs[u + 1], h)
            dko_ref[...] = (kcar[...] + dk_tiles[0][:w]).astype(BF16)
            dvo_ref[...] = (vcar[...] + dv_tiles[0][:w]).astype(BF16)
            dke_ref[...] = (dk_tiles[0][w:] + dk_tiles[1][:w]).astype(BF16)
            dve_ref[...] = (dv_tiles[0][w:] + dv_tiles[1][:w]).astype(BF16)
            kcar[...] = dk_tiles[1][w:]
            vcar[...] = dv_tiles[1][w:]
            ds_ref[...] += dsink

        @pl.when(n == nb2)
        def _():
            dko_ref[...] = kcar[...].astype(BF16)
            dvo_ref[...] = vcar[...].astype(BF16)

    last = nb2 - 1
    now = lambda n: jnp.minimum(n, last)
    q_spec = pl.BlockSpec((2 * w, Q_W), lambda n: (now(n), _div(lay.q, Q_W)))
    cur = lambda off: pl.BlockSpec((2 * w, KV_W), lambda n: (now(n), _div(off, KV_W)))
    prev = lambda off: pl.BlockSpec((w, KV_W), lambda n: (jnp.clip(2 * n - 1, 0, nb - 1), _div(off, KV_W)))
    row = pl.BlockSpec((2 * w, Q_W), lambda n: (now(n), 0))
    even = pl.BlockSpec((w, KV_W), lambda n: (now(n), 0))
    odd = pl.BlockSpec((w, KV_W), lambda n: (jnp.maximum(n - 1, 0), 0))
    half = SDS((s // 2, KV_W), BF16)
    dq, dke, dko, dve, dvo, dsk = pl.pallas_call(
        body, name=name, grid=(nb2 + 1,),
        in_specs=[q_spec, cur(lay.k), prev(lay.k), cur(lay.v), prev(lay.v), pl.BlockSpec(sinks.shape, lambda n: (0, 0)), row],
        out_specs=[row, even, odd, even, odd, pl.BlockSpec((1, LANE), lambda n: (0, 0))],
        out_shape=[SDS((s, Q_W), BF16), half, half, half, half, SDS((1, LANE), F32)],
        scratch_shapes=[pltpu.VMEM((w, KV_W), F32), pltpu.VMEM((w, KV_W), F32)],
        compiler_params=_cp(("arbitrary",)))(p_all, p_all, p_all, p_all, p_all, sinks, dy)

    def interleave(ev, od):
        return jnp.stack([ev.reshape(nb2, w, KV_W), od.reshape(nb2, w, KV_W)], axis=1).reshape(s, KV_W)

    return dq, interleave(dke, dko), interleave(dve, dvo), dsk


def _xa_probs(qs, mks):
    ss = [_dot(q, mk, NT) * (XA_D ** -0.5) for q, mk in zip(qs, mks)]
    ps = [jnp.exp(s - jnp.max(s, axis=-1, keepdims=True)) for s in ss]
    inv = [1.0 / jnp.sum(p, axis=-1, keepdims=True) for p in ps]
    return [p * i for p, i in zip(ps, inv)]


def xattn_fwd(p_all, mkv, lay, name):
    s, nm = p_all.shape[0], mkv.shape[0]
    tm = _tile(s, 512)

    def body(q_ref, mkv_ref, o_ref):
        heads = range(XA_H)
        cols = [pl.ds(h * XA_D, XA_D) for h in heads]
        ps = _xa_probs([q_ref[:, c].astype(BF16) for c in cols], [mkv_ref[:, c] for c in cols])
        os = [_dot(ps[h].astype(BF16), mkv_ref[:, pl.ds(XA_W + h * XA_D, XA_D)], NN) for h in heads]
        for h in heads:
            o_ref[:, cols[h]] = os[h].astype(BF16)

    return pl.pallas_call(
        body, name=name, grid=(s // tm,),
        in_specs=[pl.BlockSpec((tm, XA_W), lambda i: (i, _div(lay.qc, XA_W))), pl.BlockSpec((nm, 2 * XA_W), lambda i: (0, 0))],
        out_specs=pl.BlockSpec((tm, XA_W), lambda i: (i, 0)), out_shape=SDS((s, XA_W), BF16),
        compiler_params=_cp(("parallel",)))(p_all, mkv)


def xattn_bwd(p_all, mkv, dy, lay, name):
    s, nm = p_all.shape[0], mkv.shape[0]
    tm = _tile(s, 512)

    def body(q_ref, mkv_ref, do_ref, dq_ref, dmkv_ref):
        @pl.when(pl.program_id(0) == 0)
        def _():
            dmkv_ref[...] = jnp.zeros_like(dmkv_ref)

        heads = range(XA_H)
        cols = [pl.ds(h * XA_D, XA_D) for h in heads]
        vcols = [pl.ds(XA_W + h * XA_D, XA_D) for h in heads]
        qs = [q_ref[:, c].astype(BF16) for c in cols]
        dos = [do_ref[:, c].astype(BF16) for c in cols]
        ps = _xa_probs(qs, [mkv_ref[:, c] for c in cols])
        dps = [_dot(dos[h], mkv_ref[:, vcols[h]], NT) for h in heads]
        dvs = [_dot(ps[h].astype(BF16), dos[h], TN) for h in heads]
        dsb = [(p * (dp - jnp.sum(dp * p, axis=-1, keepdims=True)) * (XA_D ** -0.5)).astype(BF16) for p, dp in zip(ps, dps)]
        dqs = [_dot(dsb[h], mkv_ref[:, cols[h]], NN) for h in heads]
        dks = [_dot(dsb[h], qs[h], TN) for h in heads]
        for h in heads:
            dq_ref[:, cols[h]] = dqs[h].astype(BF16)
            dmkv_ref[:, vcols[h]] += dvs[h]
            dmkv_ref[:, cols[h]] += dks[h]

    row = pl.BlockSpec((tm, XA_W), lambda i: (i, 0))
    full = pl.BlockSpec((nm, 2 * XA_W), lambda i: (0, 0))
    return pl.pallas_call(
        body, name=name, grid=(s // tm,),
        in_specs=[pl.BlockSpec((tm, XA_W), lambda i: (i, _div(lay.qc, XA_W))), full, row],
        out_specs=[row, full], out_shape=[SDS((s, XA_W), BF16), SDS((nm, 2 * XA_W), F32)],
        compiler_params=_cp(("arbitrary",)))(p_all, mkv, dy)


def _shift_down(cur, prev8, s):
    cat = jnp.concatenate([prev8, cur[0:8]], axis=0)
    return pltpu.roll(cur, s, axis=0), pltpu.roll(cat, s, axis=0)[8:16]


def _shift_up(cur, next8, s):
    tm = cur.shape[0]
    cat = jnp.concatenate([cur[tm - 8:tm], next8], axis=0)
    return pltpu.roll(cur, tm - s, axis=0), pltpu.roll(cat, 16 - s, axis=0)[0:8]


def _conv_rows(cur, prev8, w):
    main = w[GDN_CONV - 1:GDN_CONV] * cur
    top = w[GDN_CONV - 1:GDN_CONV] * cur[0:8]
    for sft in range(1, GDN_CONV):
        wi = w[GDN_CONV - 1 - sft:GDN_CONV - sft]
        a, b = _shift_down(cur, prev8, sft)
        main = main + wi * a
        top = top + wi * b
    return jnp.concatenate([top, main[8:]], axis=0)


def _conv_rows_bwd(cur, prev8, d, next8, w):
    tm = cur.shape[0]
    row = lax.broadcasted_iota(jnp.int32, (tm, 1), 0)
    main = w[GDN_CONV - 1:GDN_CONV] * d
    bot = w[GDN_CONV - 1:GDN_CONV] * d[tm - 8:tm]
    dws = [jnp.sum(d * cur, axis=0, keepdims=True)]
    for sft in range(1, GDN_CONV):
        wi = w[GDN_CONV - 1 - sft:GDN_CONV - sft]
        a, b = _shift_up(d, next8, sft)
        main = main + wi * a
        bot = bot + wi * b
        xa, xb = _shift_down(cur, prev8, sft)
        dws.append(jnp.sum(jnp.where(row >= 8, d * xa, 0.0), axis=0, keepdims=True)
                   + jnp.sum(d[0:8] * xb, axis=0, keepdims=True))
    return jnp.concatenate([main[:tm - 8], bot], axis=0), dws


def _gdn_chunk(xq, xk, xv, ab, gp, bdot=_bdot_plain):
    c = GDN_C
    nc = xq.shape[0] // c
    lane = lax.broadcasted_iota(jnp.int32, (c, LANE), 1)
    row = lax.broadcasted_iota(jnp.int32, (c, c), 0)
    col = lax.broadcasted_iota(jnp.int32, (c, c), 1)
    g_tile = -jnp.exp(gp[0:1, :]) * _softplus(ab + gp[1:2, :])
    b_tile = _sigmoid(ab)
    tri = (row >= col).astype(F32)
    qa, ka, va = _silu(xq), _silu(xk), _silu(xv)
    items = []
    for ci in range(nc):
        rs = slice(ci * c, (ci + 1) * c)
        gcum = _dot(tri, g_tile[rs], NN, HI)
        gcum_t = gcum.T
        for h in range(GDN_H):
            hs = slice(h * GDN_D, (h + 1) * GDN_D)
            q, k, v = qa[rs, hs], ka[rs, hs], va[rs, hs]
            q = q * lax.rsqrt(jnp.sum(q * q, axis=-1, keepdims=True) + L2_EPS) * (GDN_D ** -0.5)
            k = k * lax.rsqrt(jnp.sum(k * k, axis=-1, keepdims=True) + L2_EPS)
            gc = jnp.sum(jnp.where(lane == h, gcum, 0.0), axis=1, keepdims=True)
            beta = jnp.sum(jnp.where(lane == GDN_H + h, b_tile[rs], 0.0), axis=1, keepdims=True)
            decay = jnp.exp(jnp.where(row >= col, gc - gcum_t[h:h + 1, :], NEG))
            items.append((q, k, v, gc, beta, decay))
    kks = [bdot(k, k, NT) for (_, k, _, _, _, _) in items]
    xs = tuple(-jnp.where(row > col, it[4] * kk * it[5], 0.0) for it, kk in zip(items, kks))
    nns = _neumann(xs) if bdot is _bdot_plain else _neumann_vjp(xs)
    qks = [bdot(q, k, NT) for (q, k, _, _, _, _) in items]
    out = []
    for (q, k, v, gc, beta, decay), n, qk in zip(items, nns, qks):
        eg = jnp.exp(gc)
        vb = v * beta
        kbe = k * (beta * eg)
        gl = gc[c - 1:c, :]
        out.append((vb + bdot(n, vb, NN), kbe + bdot(n, kbe, NN), q * eg, k * jnp.exp(gl - gc), qk * decay, jnp.exp(gl)))
    return [out[ci * GDN_H:(ci + 1) * GDN_H] for ci in range(nc)]


GDN_CPS = 4


def _gdn_pre_specs(lay, t, tile):
    c0 = _div(lay.qkv, GDN_W)
    cur = [pl.BlockSpec((t, GDN_W), lambda n, j=j: (tile(n), c0 + j)) for j in range(3)]
    prev = [pl.BlockSpec((8, GDN_W), lambda n, j=j: (jnp.maximum(tile(n) * (t // 8) - 1, 0), c0 + j)) for j in range(3)]
    return cur + prev + [pl.BlockSpec((GDN_CONV, 3 * GDN_W), lambda n: (0, 0)),
                         pl.BlockSpec((t, LANE), lambda n: (tile(n), _div(lay.ab, LANE))),
                         pl.BlockSpec((8, LANE), lambda n: (0, 0))]


def _gdn_conv_inputs(x_refs, prev_refs, w_ref, first):
    out = []
    for j in range(3):
        prev8 = jnp.where(first, 0.0, prev_refs[j][...])
        out.append((x_refs[j][...], prev8, w_ref[:, pl.ds(j * GDN_W, GDN_W)]))
    return out


def gdn_pre_fwd(p_all, conv_w, gp, lay, name):
    s = p_all.shape[0]
    c = GDN_C
    n = _div(s, c)
    cps = _tile(n, GDN_CPS)
    t = cps * c

    def body(xq, xk, xv, pq, pk, pv, cw, ab, gp_ref, u_ref, w_ref, qd_ref, kd_ref, qk_ref, gl_ref):
        lane = lax.broadcasted_iota(jnp.int32, (1, LANE), 1)
        xs = [_conv_rows(*a) for a in _gdn_conv_inputs((xq, xk, xv), (pq, pk, pv), cw, pl.program_id(0) == 0)]
        chunks = _gdn_chunk(xs[0], xs[1], xs[2], ab[...], gp_ref[...])
        for ci, heads in enumerate(chunks):
            rs = pl.ds(ci * c, c)
            gl_row = jnp.zeros((1, LANE), F32)
            for h, (u, w, qd, kd, qk, gl) in enumerate(heads):
                hs = pl.ds(h * GDN_D, GDN_D)
                u_ref[rs, hs] = u
                w_ref[rs, hs] = w.astype(BF16)
                qd_ref[rs, hs] = qd.astype(BF16)
                kd_ref[rs, hs] = kd.astype(BF16)
                qk_ref[rs, pl.ds(h * c, c)] = qk.astype(BF16)
                gl_row = gl_row + jnp.where(lane == h, gl, 0.0)
            gl_ref[ci] = gl_row

    row = pl.BlockSpec((t, GDN_W), lambda n: (n, 0))
    return pl.pallas_call(
        body, name=name, grid=(n // cps,), in_specs=_gdn_pre_specs(lay, t, lambda n: n),
        out_specs=[row, row, row, row, pl.BlockSpec((t, GDN_H * c), lambda n: (n, 0)), pl.BlockSpec((cps, 1, LANE), lambda n: (n, 0, 0))],
        out_shape=[SDS((s, GDN_W), F32), SDS((s, GDN_W), BF16), SDS((s, GDN_W), BF16), SDS((s, GDN_W), BF16),
                   SDS((s, GDN_H * c), BF16), SDS((n, 1, LANE), F32)],
        compiler_params=_cp(("parallel",)))(p_all, p_all, p_all, p_all, p_all, p_all, conv_w, p_all, gp)


def gdn_pre_bwd(p_all, conv_w, gp, du, dw, dqd, dkd, dqk, dgl, lay, name):
    s = p_all.shape[0]
    c = GDN_C
    n = _div(s, c)
    cps = _tile(n, GDN_CPS)
    t = cps * c
    steps = n // cps
    chunk = functools.partial(_gdn_chunk, bdot=_bdot_vjp)

    def body(xq, xk, xv, pq, pk, pv, cw, ab, gp_ref, du_r, dw_r, dqd_r, dkd_r, dqk_r, dgl_r,
             dx_ref, dab_ref, dgp_ref, dcw_ref, carry):
        step = pl.program_id(0)

        @pl.when(step == 0)
        def _():
            dgp_ref[...] = jnp.zeros_like(dgp_ref)
            dcw_ref[...] = jnp.zeros_like(dcw_ref)
            carry[...] = jnp.zeros_like(carry)

        lane = lax.broadcasted_iota(jnp.int32, (1, LANE), 1)
        conv_in = _gdn_conv_inputs((xq, xk, xv), (pq, pk, pv), cw, step == steps - 1)
        xs = [_conv_rows(*a) for a in conv_in]
        _, vjp = jax.vjp(chunk, xs[0], xs[1], xs[2], ab[...], gp_ref[...])
        cts = []
        for ci in range(cps):
            rs = pl.ds(ci * c, c)
            heads = []
            for h in range(GDN_H):
                hs = pl.ds(h * GDN_D, GDN_D)
                dgl_h = jnp.sum(jnp.where(lane == h, dgl_r[ci], 0.0), axis=1, keepdims=True)
                heads.append((du_r[rs, hs], dw_r[rs, hs], dqd_r[rs, hs], dkd_r[rs, hs], dqk_r[rs, pl.ds(h * c, c)], dgl_h))
            cts.append(heads)
        *dxs, dab, dgp = vjp(cts)
        for j, (d, (cur, prev8, w)) in enumerate(zip(dxs, conv_in)):
            cols = pl.ds(j * GDN_W, GDN_W)
            dx, dws = _conv_rows_bwd(cur, prev8, d, carry[:, cols], w)
            carry[:, cols] = d[0:8]
            dx_ref[:, cols] = dx.astype(BF16)
            for sft in range(GDN_CONV):
                dcw_ref[GDN_CONV - 1 - sft:GDN_CONV - sft, cols] += dws[sft]
        dab_ref[...] = dab.astype(BF16)
        dgp_ref[...] += dgp

    tile = lambda i: steps - 1 - i
    row = pl.BlockSpec((t, GDN_W), lambda i: (tile(i), 0))
    return pl.pallas_call(
        body, name=name, grid=(steps,),
        in_specs=_gdn_pre_specs(lay, t, tile) + [row, row, row, row, pl.BlockSpec((t, GDN_H * c), lambda i: (tile(i), 0)),
                                                 pl.BlockSpec((cps, 1, LANE), lambda i: (tile(i), 0, 0))],
        out_specs=[pl.BlockSpec((t, 3 * GDN_W), lambda i: (tile(i), 0)), pl.BlockSpec((t, LANE), lambda i: (tile(i), 0)),
                   pl.BlockSpec((8, LANE), lambda i: (0, 0)), pl.BlockSpec((GDN_CONV, 3 * GDN_W), lambda i: (0, 0))],
        out_shape=[SDS((s, 3 * GDN_W), BF16), SDS((s, LANE), BF16), SDS((8, LANE), F32), SDS((GDN_CONV, 3 * GDN_W), F32)],
        scratch_shapes=[pltpu.VMEM((8, 3 * GDN_W), F32)],
        compiler_params=_cp(("arbitrary",)))(p_all, p_all, p_all, p_all, p_all, p_all, conv_w, p_all, gp,
                                             du, dw, dqd, dkd, dqk, dgl)


def _lane_scalar(row, h):
    lane = lax.broadcasted_iota(jnp.int32, row.shape, 1)
    return jnp.sum(jnp.where(lane == h, row, 0.0), axis=1, keepdims=True)


def _gdn_out_head(oh, zh, nw):
    return oh * lax.rsqrt(jnp.mean(oh * oh, axis=-1, keepdims=True) + RMS_EPS) * nw * _silu(zh)


def gdn_scan_fwd(u, w, qd, kd, qk, gl, p_all, nw, lay, name):
    s = u.shape[0]
    c = GDN_C
    n = _div(s, c)
    cps = _tile(n, GDN_CPS)
    t = cps * c

    def body(u_r, w_r, qd_r, kd_r, qk_r, gl_r, z_r, nw_r, o_ref, s_ref, y_ref, st):
        @pl.when(pl.program_id(0) == 0)
        def _():
            st[...] = jnp.zeros_like(st)

        heads = range(GDN_H)
        hs = [pl.ds(h * GDN_D, GDN_D) for h in heads]
        for ci in range(cps):
            rs = pl.ds(ci * c, c)
            s_ref[ci] = st[...]
            sh = [st[hs[h], :] for h in heads]
            shb = [x.astype(BF16) for x in sh]
            ws = [_dot(w_r[rs, hs[h]], shb[h], NN) for h in heads]
            qs = [_dot(qd_r[rs, hs[h]], shb[h], NN) for h in heads]
            vb = [(u_r[rs, hs[h]] - ws[h]).astype(BF16) for h in heads]
            ov = [_dot(qk_r[rs, pl.ds(h * c, c)], vb[h], NN) for h in heads]
            kv = [_dot(kd_r[rs, hs[h]], vb[h], TN) for h in heads]
            os = [qs[h] + ov[h] for h in heads]
            for h in heads:
                o_ref[rs, hs[h]] = os[h]
                y_ref[rs, hs[h]] = _gdn_out_head(os[h], z_r[rs, hs[h]], nw_r[...]).astype(BF16)
                st[hs[h], :] = sh[h] * _lane_scalar(gl_r[ci], h) + kv[h]

    row = pl.BlockSpec((t, GDN_W), lambda i: (i, 0))
    return pl.pallas_call(
        body, name=name, grid=(n // cps,),
        in_specs=[row, row, row, row, pl.BlockSpec((t, GDN_H * c), lambda i: (i, 0)), pl.BlockSpec((cps, 1, LANE), lambda i: (i, 0, 0)),
                  pl.BlockSpec((t, GDN_W), lambda i: (i, _div(lay.z, GDN_W))), pl.BlockSpec((1, GDN_D), lambda i: (0, 0))],
        out_specs=[row, pl.BlockSpec((cps, GDN_W, GDN_D), lambda i: (i, 0, 0)), row],
        out_shape=[SDS((s, GDN_W), F32), SDS((n, GDN_W, GDN_D), F32), SDS((s, GDN_W), BF16)],
        scratch_shapes=[pltpu.VMEM((GDN_W, GDN_D), F32)],
        compiler_params=_cp(("arbitrary",)))(u, w, qd, kd, qk, gl, p_all, nw)


def gdn_scan_bwd(u, w, qd, kd, qk, gl, states, o, dy, p_all, nw, lay, name):
    s = u.shape[0]
    c = GDN_C
    n = _div(s, c)
    cps = _tile(n, GDN_CPS)
    t = cps * c
    steps = n // cps

    def body(u_r, w_r, qd_r, kd_r, qk_r, gl_r, s_r, o_r, dy_r, z_r, nw_r,
             du_o, dw_o, dqd_o, dkd_o, dqk_o, dgl_o, dz_o, dnw_o, dst):
        @pl.when(pl.program_id(0) == 0)
        def _():
            dst[...] = jnp.zeros_like(dst)
            dnw_o[...] = jnp.zeros_like(dnw_o)

        lane = lax.broadcasted_iota(jnp.int32, (1, LANE), 1)
        heads = range(GDN_H)
        hs = [pl.ds(h * GDN_D, GDN_D) for h in heads]
        qs = [pl.ds(h * c, c) for h in heads]
        for ci in reversed(range(cps)):
            rs = pl.ds(ci * c, c)
            outs = [jax.vjp(_gdn_out_head, o_r[rs, hs[h]], z_r[rs, hs[h]], nw_r[...])[1](dy_r[rs, hs[h]].astype(F32))
                    for h in heads]
            for h in heads:
                dz_o[rs, hs[h]] = outs[h][1].astype(BF16)
                dnw_o[...] += outs[h][2]
            sh = [s_r[ci, hs[h], :] for h in heads]
            shb = [x.astype(BF16) for x in sh]
            ds_out = [dst[hs[h], :] for h in heads]
            dsb = [x.astype(BF16) for x in ds_out]
            dob = [outs[h][0].astype(BF16) for h in heads]
            ws = [_dot(w_r[rs, hs[h]], shb[h], NN) for h in heads]
            dv1 = [_dot(qk_r[rs, qs[h]], dob[h], TN) for h in heads]
            dv2 = [_dot(kd_r[rs, hs[h]], dsb[h], NN) for h in heads]
            dqd = [_dot(dob[h], shb[h], NT) for h in heads]
            dsq = [_dot(qd_r[rs, hs[h]], dob[h], TN) for h in heads]
            vb = [(u_r[rs, hs[h]] - ws[h]).astype(BF16) for h in heads]
            dv = [dv1[h] + dv2[h] for h in heads]
            dvb = [x.astype(BF16) for x in dv]
            dw = [_dot(dvb[h], shb[h], NT) for h in heads]
            dkd = [_dot(vb[h], dsb[h], NT) for h in heads]
            dqk = [_dot(dob[h], vb[h], NT) for h in heads]
            dsw = [_dot(w_r[rs, hs[h]], dvb[h], TN) for h in heads]
            dgl_row = jnp.zeros((1, LANE), F32)
            for h in heads:
                du_o[rs, hs[h]] = dv[h]
                dw_o[rs, hs[h]] = -dw[h]
                dqd_o[rs, hs[h]] = dqd[h]
                dkd_o[rs, hs[h]] = dkd[h]
                dqk_o[rs, qs[h]] = dqk[h]
                dgl_row = dgl_row + jnp.where(lane == h, jnp.sum(jnp.sum(ds_out[h] * sh[h], axis=1, keepdims=True), axis=0, keepdims=True), 0.0)
                dst[hs[h], :] = ds_out[h] * _lane_scalar(gl_r[ci], h) + dsq[h] - dsw[h]
            dgl_o[ci] = dgl_row

    rev = lambda i: steps - 1 - i
    row = pl.BlockSpec((t, GDN_W), lambda i: (rev(i), 0))
    qks = pl.BlockSpec((t, GDN_H * c), lambda i: (rev(i), 0))
    gls = pl.BlockSpec((cps, 1, LANE), lambda i: (rev(i), 0, 0))
    zs = pl.BlockSpec((t, GDN_W), lambda i: (rev(i), _div(lay.z, GDN_W)))
    nws = pl.BlockSpec((1, GDN_D), lambda i: (0, 0))
    return pl.pallas_call(
        body, name=name, grid=(steps,),
        in_specs=[row, row, row, row, qks, gls, pl.BlockSpec((cps, GDN_W, GDN_D), lambda i: (rev(i), 0, 0)), row, row, zs, nws],
        out_specs=[row, row, row, row, qks, gls, row, nws],
        out_shape=[SDS((s, GDN_W), F32)] * 4 + [SDS((s, GDN_H * c), F32), SDS((n, 1, LANE), F32), SDS((s, GDN_W), BF16),
                                                SDS((1, GDN_D), F32)],
        scratch_shapes=[pltpu.VMEM((GDN_W, GDN_D), F32)],
        compiler_params=_cp(("arbitrary",)))(u, w, qd, kd, qk, gl, states, o, dy, p_all, nw)


def _cols_to_full(g):
    n, k, c = g.shape
    return g.transpose(1, 0, 2).reshape(k, n * c)


def _rows_to_blocks(w):
    return w.reshape(N_DEV, w.shape[0] // N_DEV, w.shape[1])


def _pack_small(parts, rows):
    flat = jnp.concatenate([jnp.pad(p.reshape(-1), (0, -p.size % LANE)) for p in parts])
    return jnp.pad(flat, (0, rows * LANE - flat.size)).reshape(rows, LANE)


def kernel(x, mem, g_mix, w_in, sinks, conv_w, a_log, dt_bias, gdn_norm_w, g_mem, w_mem_kv, w_swa_up, w_gdn_up, w_xa_up, w_out, g_mlp, w_mlp_in, w_mlp_out, g_final, loss_target, m_g_mix, m_w_in, m_sinks, m_conv_w, m_a_log, m_dt_bias, m_gdn_norm_w, m_g_mem, m_w_mem_kv, m_w_swa_up, m_w_gdn_up, m_w_xa_up, m_w_out, m_g_mlp, m_w_mlp_in, m_w_mlp_out, m_g_final, v_g_mix, v_w_in, v_sinks, v_conv_w, v_a_log, v_dt_bias, v_gdn_norm_w, v_g_mem, v_w_mem_kv, v_w_swa_up, v_w_gdn_up, v_w_xa_up, v_w_out, v_g_mlp, v_w_mlp_in, v_w_mlp_out, v_g_final):
    xs, ms, tgt = x[0], mem[0], loss_target[0]
    s, d = xs.shape
    lay = Layout(d)
    px, py, pc = _position()
    dev = 4 * px + 2 * py + pc

    n1, g_in, g_conv = rmsnorm_fwd(xs, g_mix, "norm_mix", side=GatherJob([w_in[0].astype(BF16), conv_w[0]]))
    W_in = pad_w_in(g_in, lay)
    convw = _cols_to_full(g_conv)
    gp = jnp.zeros((8, LANE), F32).at[0, :GDN_H].set(a_log[0]).at[1, :GDN_H].set(dt_bias[0])
    later = [w_mem_kv[0], w_swa_up[0], w_gdn_up[0], w_xa_up[0], w_out[0], w_mlp_in[0]]

    p_all, g_mkv, W_sup, W_gup, W_xup, g_out, W_m1 = matmul(
        n1, W_in, mode="nn", out_dtype=F32, name="proj_in", tm=2048, tn=1024, tk=d,
        side=GatherJob([w.astype(BF16) for w in later]))
    W_mkv = g_mkv.reshape(-1, g_mkv.shape[2])
    W_out = g_out.reshape(-1, d)
    y_a = swa_fwd(p_all, sinks, lay, "swa_fwd")
    u, gw, gqd, gkd, gqk, ggl = gdn_pre_fwd(p_all, convw, gp, lay, "gdn_pre_fwd")
    o_b, states, y_b = gdn_scan_fwd(u, gw, gqd, gkd, gqk, ggl, p_all, gdn_norm_w, lay, "gdn_scan_fwd")
    nm = rmsnorm_fwd(ms, g_mem, "norm_mem")
    mkv = matmul(nm, W_mkv, mode="nn", out_dtype=BF16, name="proj_mem", tk=d)
    y_c = xattn_fwd(p_all, mkv, lay, "xattn_fwd")
    merged = merge(p_all, y_a, y_b, y_c, W_sup, W_gup, W_xup, None, lay, "merge_fwd")
    h1, n2 = matmul(merged, W_out, mode="nn", out_dtype=F32, name="proj_out", tm=512, tn=d, tk=d, resid=xs, rms_gain=g_mlp)
    uu, act, g_m2 = matmul(n2, W_m1, mode="nn", out_dtype=F32, name="mlp_in", tm=2048, tn=512, tk=d, b_cols=True,
                           relu2_out=True, side=GatherJob([w_mlp_out[0].astype(BF16)]))
    W_m2 = g_m2.reshape(-1, d)
    dh2, dh2_b, dg_final, lrow = matmul(act, W_m2, mode="nn", out_dtype=F32, name="mlp_out_loss", tm=512, tn=d, tk=1024,
                                        resid=h1, rms_gain=g_final.reshape(1, d), loss_target=tgt)
    loss = lax.psum(lrow[0, 0], ("x", "y", "c"))

    du = matmul(dh2_b, W_m2, mode="nt", out_dtype=BF16, name="mlp_out_dx", tm=2048, tn=512, tk=d, relu2_grad_of=uu)
    dW_m2 = matmul(act, dh2_b, mode="tn", out_dtype=BF16, name="mlp_out_dw", tm=1024, tn=2048, tk=1024)
    dW_m2 = _rows_to_blocks(dW_m2)
    dn2, sib_m2 = matmul(du, W_m1, mode="nt", out_dtype=F32, name="mlp_in_dx", tm=1024, tn=2048, tk=1024, b_cols=True,
                         side=PairExchangeJob([dW_m2], [False]))
    c_m2 = pair_add(dW_m2, False, sib_m2, "grads_pair_add_m2")
    dW_m1 = matmul(n2, du, mode="tn", out_dtype=BF16, name="mlp_in_dw", tm=2048, tn=1024, tk=1024)
    dh1, dg_mlp, dh1_b = rmsnorm_bwd(h1, g_mlp, dn2, dh2, "norm_mlp_bwd", bf16_copy=True)

    dmerged, sib_m1 = matmul(dh1_b, W_out, mode="nt", out_dtype=F32, name="proj_out_dx", tm=2048, tn=512, tk=d,
                             side=PairExchangeJob([dW_m1], [True]))
    c_m1 = pair_add(dW_m1, True, sib_m1, "grads_pair_add_m1")
    dW_out = matmul(merged, dh1_b, mode="tn", out_dtype=BF16, name="proj_out_dw", tm=2048, tn=1024, tk=1024)
    dgates, dta, dtb, dtc = merge(p_all, y_a, y_b, y_c, W_sup, W_gup, W_xup, dmerged, lay, "merge_bwd")
    dy_a = matmul(dta, W_sup, mode="nt", out_dtype=BF16, name="swa_up_dx", tm=2048, tk=d, b_cols=True)
    dy_b = matmul(dtb, W_gup, mode="nt", out_dtype=BF16, name="gdn_up_dx", tm=2048, tk=d, b_cols=True)
    dy_c = matmul(dtc, W_xup, mode="nt", out_dtype=BF16, name="xa_up_dx", tm=2048, tk=d, b_cols=True)
    dW_sup = matmul(y_a, dta, mode="tn", out_dtype=BF16, name="swa_up_dw", tn=2048, tk=2048)
    dW_gup = matmul(y_b, dtb, mode="tn", out_dtype=BF16, name="gdn_up_dw", tn=2048, tk=2048)
    dW_xup = matmul(y_c, dtc, mode="tn", out_dtype=BF16, name="xa_up_dw", tn=2048, tk=2048)

    dq_a, dk_a, dv_a, dsinks = swa_bwd(p_all, sinks, dy_a, lay, "swa_bwd")
    dq_c, dmkv = xattn_bwd(p_all, mkv, dy_c, lay, "xattn_bwd")
    dW_mkv = matmul(nm, dmkv, mode="tn", out_dtype=BF16, name="proj_mem_dw", tk=256)
    dnm = matmul(dmkv, W_mkv, mode="nt", out_dtype=F32, name="proj_mem_dx", tk=1024)
    _, dg_mem = rmsnorm_bwd(ms, g_mem, dnm, None, "norm_mem_bwd")

    du_g, dw_g, dqd_g, dkd_g, dqk_g, dgl_g, dz, dnorm_w = gdn_scan_bwd(
        u, gw, gqd, gkd, gqk, ggl, states, o_b, dy_b, p_all, gdn_norm_w, lay, "gdn_scan_bwd")
    dqkv, dab, dgp, dconv = gdn_pre_bwd(p_all, convw, gp, du_g, dw_g, dqd_g, dkd_g, dqk_g, dgl_g, lay, "gdn_pre_bwd")

    drest = jnp.concatenate([dq_a, dqkv, dz, dq_c, dk_a, dv_a, dab, jnp.zeros((s, lay.pw - lay.end), BF16)], axis=1)
    def pair_stage(grads, cols, tag):
        from_sib = run_job(PairExchangeJob(grads, cols), "grads_pair_exchange_" + tag)
        return [pair_add(g, cl, o, "grads_pair_add_%s%d" % (tag, i)) for i, (g, cl, o) in enumerate(zip(grads, cols, from_sib))]

    small = pair_stage([_rows_to_blocks(dW_mkv), dW_sup, dW_gup, dW_xup, _rows_to_blocks(dW_out)],
                       [False, True, True, True, False], "a")
    dW_in, p_m1, p_m2 = matmul(n1, dgates, tail=drest, mode="tn", out_dtype=BF16, name="proj_in_dw", tm=2048, tn=1024, tk=1024,
                               side=ChipExchangeJob([c_m1, c_m2]))
    late = pair_stage([unpad_dw_in(dW_in, lay)], [False], "b")
    dn1, p_in, p_mkv, p_sup, p_gup, p_xup, p_out = matmul(
        dgates, W_in, tail=drest, mode="nt", out_dtype=F32, name="proj_in_dx", tm=1024, tn=2048, tk=1024,
        side=ChipExchangeJob(late + small))
    grad_x, dg_mix = rmsnorm_bwd(xs, g_mix, dn1, dh1, "norm_mix_bwd")
    parts = [p_in, p_mkv, p_sup, p_gup, p_xup, p_out, p_m1, p_m2]

    shard_names = [(w_in, m_w_in, v_w_in), (w_mem_kv, m_w_mem_kv, v_w_mem_kv), (w_swa_up, m_w_swa_up, v_w_swa_up),
                   (w_gdn_up, m_w_gdn_up, v_w_gdn_up), (w_xa_up, m_w_xa_up, v_w_xa_up), (w_out, m_w_out, v_w_out),
                   (w_mlp_in, m_w_mlp_in, v_w_mlp_in), (w_mlp_out, m_w_mlp_out, v_w_mlp_out)]
    big_res = [adamw(p, w[0], m[0], v[0], "adamw_%d" % i) for i, (p, (w, m, v)) in enumerate(zip(parts, shard_names))]

    smalls = [(g_mix, m_g_mix, v_g_mix, dg_mix), (sinks, m_sinks, v_sinks, dsinks[:, :SWA_HQ]),
              (a_log, m_a_log, v_a_log, dgp[0:1, :GDN_H]), (dt_bias, m_dt_bias, v_dt_bias, dgp[1:2, :GDN_H]),
              (gdn_norm_w, m_gdn_norm_w, v_gdn_norm_w, dnorm_w), (g_mem, m_g_mem, v_g_mem, dg_mem),
              (g_mlp, m_g_mlp, v_g_mlp, dg_mlp), (g_final, m_g_final, v_g_final, dg_final)]
    sizes = [-(-t[0].size // LANE) * LANE for t in smalls] + [GDN_CONV * 3 * GDN_W]
    rows = -(-sum(sizes) // (8 * LANE)) * 8
    csh = conv_w.shape[2]

    def conv_place(a):
        full = jnp.tile(a[0], (1, N_DEV))
        owner = lax.broadcasted_iota(jnp.int32, full.shape, 1) // csh
        return jnp.where(owner == dev, full, 0.0)

    g_pack = _pack_small([t[3] for t in smalls] + [dconv], rows)
    w_pack = _pack_small([t[0] for t in smalls] + [conv_place(conv_w)], rows)
    m_pack = _pack_small([t[1] for t in smalls] + [conv_place(m_conv_w)], rows)
    v_pack = _pack_small([t[2] for t in smalls] + [conv_place(v_conv_w)], rows)
    g_all = run_job(GatherJob([g_pack]), "gather_small_grads")[0]
    small_res = adamw(g_all, w_pack, m_pack, v_pack, "adamw_small")

    def unpack(arr):
        flat = arr.reshape(-1)
        outs, off = [], 0
        for t, sz in zip(smalls, sizes[:-1]):
            outs.append(flat[off:off + t[0].size].reshape(t[0].shape))
            off += sz
        cw = flat[off:off + sizes[-1]].reshape(GDN_CONV, 3 * GDN_W)
        mine = (lax.broadcasted_iota(jnp.int32, (1, N_DEV, 1), 1) == dev).astype(F32)
        outs.append(jnp.sum(cw.reshape(GDN_CONV, N_DEV, csh) * mine, axis=1)[None])
        return outs

    sg, sd, sm, sv = (unpack(a) for a in small_res)
    bg, bd, bm, bv = ([r[i][None] for r in big_res] for i in range(4))

    def ordered(sm_, bg_):
        return [sm_[0], bg_[0], sm_[1], sm_[8], sm_[2], sm_[3], sm_[4], sm_[5], bg_[1], bg_[2], bg_[3], bg_[4], bg_[5],
                sm_[6], bg_[6], bg_[7], sm_[7]]

    return (loss, grad_x[None], *ordered(sg, bg), *ordered(sd, bd), *ordered(sm, bm), *ordered(sv, bv))
```

```python
import functools

import jax
import jax.numpy as jnp
from jax import lax
from jax.experimental import pallas as pl
from jax.experimental.pallas import tpu as pltpu

F32, BF16 = jnp.float32, jnp.bfloat16
SDS = jax.ShapeDtypeStruct
MESH = pl.DeviceIdType.MESH
ANY = pl.BlockSpec(memory_space=pl.ANY)

SWA_HQ, SWA_HKV, SWA_HD, SWA_W = 16, 2, 64, 128
SWA_G = SWA_HQ // SWA_HKV
GDN_H, GDN_D, GDN_CONV, GDN_C = 4, 128, 4, 64
XA_H, XA_D = 4, 128
Q_W = SWA_HQ * SWA_HD
KV_W = SWA_HKV * SWA_HD
GDN_W = GDN_H * GDN_D
XA_W = XA_H * XA_D
RMS_EPS = 1e-6
L2_EPS = 1e-6
NEG = -1e30
N_DEV = 8
LANE = 128

ADAM_LR, ADAM_B1, ADAM_B2, ADAM_EPS, ADAM_WD, ADAM_STEP = 0.001, 0.9, 0.999, 1e-08, 0.01, 10

VMEM_BIG = 56 * 1024 * 1024


def _cp(sem, vmem=VMEM_BIG):
    return pltpu.CompilerParams(dimension_semantics=sem, vmem_limit_bytes=vmem)


def _div(a, b):
    assert a % b == 0, (a, b)
    return a // b


def _tile(n, t):
    t = min(t, n)
    assert n % t == 0, (n, t)
    return t


def _sigmoid(x):
    return jax.nn.sigmoid(x)


def _silu(x):
    return x * _sigmoid(x)


def _softplus(x):
    return jnp.maximum(x, 0.0) + jnp.log1p(jnp.exp(-jnp.abs(x)))


def _dot(a, b, dims, prec=None):
    return lax.dot_general(a, b, (dims, ((), ())), precision=prec, preferred_element_type=F32)


NN = ((1,), (0,))
NT = ((1,), (1,))
TN = ((0,), (0,))
HI = lax.Precision.HIGHEST


def _bdot_plain(a, b, dims):
    return _dot(a.astype(BF16), b.astype(BF16), dims)


@functools.partial(jax.custom_vjp, nondiff_argnums=(2,))
def _bdot_vjp(a, b, dims):
    return _bdot_plain(a, b, dims)


def _bdot_vjp_fwd(a, b, dims):
    return _bdot_plain(a, b, dims), (a, b)


def _bdot_vjp_bwd(dims, res, ct):
    a, b = res
    if dims == NN:
        return _bdot_plain(ct, b, NT), _bdot_plain(a, ct, TN)
    assert dims == NT, dims
    return _bdot_plain(ct, b, NN), _bdot_plain(ct, a, TN)


_bdot_vjp.defvjp(_bdot_vjp_fwd, _bdot_vjp_bwd)


def _neumann(xs):
    pws, nns = list(xs), list(xs)
    for _ in range(5):
        pws = [_bdot_plain(p, p, NN) for p in pws]
        nns = [n + p + _bdot_plain(n, p, NN) for n, p in zip(nns, pws)]
    return tuple(nns)


@jax.custom_vjp
def _neumann_vjp(xs):
    return _neumann(xs)


def _neumann_vjp_fwd(xs):
    nns = _neumann(xs)
    return nns, nns


def _neumann_vjp_bwd(nns, cts):
    ts = [ct + _bdot_plain(nn, ct, TN) for nn, ct in zip(nns, cts)]
    return (tuple(t + _bdot_plain(t, nn, NT) for t, nn in zip(ts, nns)),)


_neumann_vjp.defvjp(_neumann_vjp_fwd, _neumann_vjp_bwd)


class Layout:
    def __init__(self, d):
        self.d = d
        self.g = 0
        self.q = 3 * d
        self.qkv = self.q + Q_W
        self.z = self.qkv + 3 * GDN_W
        self.qc = self.z + GDN_W
        self.k = self.qc + XA_W
        self.v = self.k + KV_W
        self.ab = self.v + KV_W
        self.end = self.ab + LANE
        self.pw = -(-self.end // 1024) * 1024
        self.lq, self.lk, self.lv, self.lqkv = 0, Q_W, Q_W + KV_W, Q_W + 2 * KV_W
        self.la = self.lqkv + 3 * GDN_W
        self.lz = self.la + 2 * GDN_H
        self.lqc = self.lz + GDN_W
        self.lg = self.lqc + XA_W
        self.lw = self.lg + 3 * d

    def pieces(self):
        segs = [(self.lq, self.lk, self.q), (self.lk, self.lv, self.k), (self.lv, self.lqkv, self.v),
                (self.lqkv, self.la, self.qkv), (self.la, self.lz, self.ab), (self.lz, self.lqc, self.z),
                (self.lqc, self.lg, self.qc), (self.lg, self.lw, self.g)]
        cw = _div(self.lw, N_DEV)
        out = []
        for dev in range(N_DEV):
            lo, hi = dev * cw, (dev + 1) * cw
            for ls, le, ps in segs:
                s, e = max(lo, ls), min(hi, le)
                if s < e:
                    out.append((dev, s - lo, ps + s - ls, e - s))
        return out


def pad_w_in(g, lay):
    nd, k, cw = g.shape
    tr = _tile(k, 256)
    tail = lay.ab + 2 * GDN_H

    def body(g_ref, o_ref):
        o_ref[:, pl.ds(tail, lay.pw - tail)] = jnp.zeros((tr, lay.pw - tail), o_ref.dtype)
        for dev, so, po, ln in lay.pieces():
            o_ref[:, pl.ds(po, ln)] = g_ref[dev, :, pl.ds(so, ln)]

    return pl.pallas_call(
        body, name="pad_w_in", grid=(k // tr,), in_specs=[pl.BlockSpec((nd, tr, cw), lambda i: (0, i, 0))],
        out_specs=pl.BlockSpec((tr, lay.pw), lambda i: (i, 0)), out_shape=SDS((k, lay.pw), g.dtype),
        compiler_params=_cp(("parallel",)))(g)


def unpad_dw_in(dw, lay):
    k = dw.shape[0]
    cw = _div(lay.lw, N_DEV)
    tr = _tile(k, 256)

    def body(d_ref, o_ref):
        for dev, so, po, ln in lay.pieces():
            o_ref[dev, :, pl.ds(so, ln)] = d_ref[:, pl.ds(po, ln)]

    return pl.pallas_call(
        body, name="unpad_dw_in", grid=(k // tr,), in_specs=[pl.BlockSpec((tr, lay.pw), lambda i: (i, 0))],
        out_specs=pl.BlockSpec((N_DEV, tr, cw), lambda i: (0, i, 0)), out_shape=SDS((N_DEV, k, cw), dw.dtype),
        compiler_params=_cp(("parallel",)))(dw)


def _position():
    return lax.axis_index("x"), lax.axis_index("y"), lax.axis_index("c")


class GatherJob:
    def __init__(self, arrs):
        self.ins = list(arrs)
        n = len(arrs)
        self.out_shapes = [SDS((N_DEV,) + a.shape, a.dtype) for a in arrs]
        self.scratch = [pltpu.SemaphoreType.DMA((n, 7)), pltpu.SemaphoreType.DMA((n, 7)), pltpu.SemaphoreType.DMA((n,))]

    def _ctx(self, outs, sems):
        send_sems, recv_sems, _ = sems
        x, y, c = _position()

        def blk(o, p):
            return o.at[4 * p[0] + 2 * p[1] + p[2]]

        def copy(i, k, block, to, src=None):
            return pltpu.make_async_remote_copy(
                src_ref=blk(outs[i], block) if src is None else src, dst_ref=blk(outs[i], block),
                send_sem=send_sems.at[i, k], recv_sem=recv_sems.at[i, k], device_id=to, device_id_type=MESH)

        return (x, y, c), (x, y, 1 - c), [(1 - x, y), (x, 1 - y), (1 - x, 1 - y)], blk, copy

    def start(self, ins, outs, sems):
        me, sibling, chips, blk, copy = self._ctx(outs, sems)
        for i in range(len(ins)):
            pltpu.make_async_copy(ins[i], blk(outs[i], me), sems[2].at[i]).start()
            copy(i, 0, me, sibling, src=ins[i]).start()
            for j, chip in enumerate(chips[:2]):
                copy(i, 1 + j, me, (*chip, me[2]), src=ins[i]).start()

    def _relay(self, i, outs, sems, onward):
        me, _, _, blk, copy = self._ctx(outs, sems)
        x, y, c = me
        origin = (x + (1 - c) - 2 * x * (1 - c), y + c - 2 * y * c, c)
        if not onward:
            return copy(i, 1 + c, origin, me)
        return copy(i, 3, origin, (x + c - 2 * x * c, y + (1 - c) - 2 * y * (1 - c), c))

    def relay(self, ins, outs, sems):
        for i in range(len(ins)):
            self._relay(i, outs, sems, False).wait_recv()
            self._relay(i, outs, sems, True).start()

    def mid(self, ins, outs, sems):
        me, sibling, chips, blk, copy = self._ctx(outs, sems)
        for i in range(len(ins)):
            for j, chip in enumerate(chips):
                arrival = copy(i, 1 + j, (*chip, me[2]), me)
                if j < 2:
                    pl.when(me[2] != j)(arrival.wait_recv)
                else:
                    arrival.wait_recv()
                copy(i, 4 + j, (*chip, me[2]), sibling).start()

    def finish(self, ins, outs, sems):
        me, sibling, chips, blk, copy = self._ctx(outs, sems)
        for i in range(len(ins)):
            copy(i, 0, sibling, me).wait_recv()
            for j, chip in enumerate(chips):
                copy(i, 4 + j, (*chip, 1 - me[2]), me).wait_recv()
        for i in range(len(ins)):
            pltpu.make_async_copy(ins[i], blk(outs[i], me), sems[2].at[i]).wait()
            copy(i, 0, me, sibling, src=ins[i]).wait_send()
            for j, chip in enumerate(chips[:2]):
                copy(i, 1 + j, me, (*chip, me[2]), src=ins[i]).wait_send()
            self._relay(i, outs, sems, True).wait_send()
            for j, chip in enumerate(chips):
                copy(i, 4 + j, (*chip, me[2]), sibling).wait_send()


class ChipExchangeJob:
    mid = None

    def __init__(self, arrs):
        self.ins = list(arrs)
        n = len(arrs)
        self.out_shapes = [SDS(a.shape, a.dtype) for a in arrs]
        self.scratch = [pltpu.SemaphoreType.DMA((n, 3)), pltpu.SemaphoreType.DMA((n, 3)), pltpu.SemaphoreType.DMA((n,))]

    def _copies(self, ins, outs, sems, i, arrivals):
        send_sems, recv_sems, local_sems = sems
        x, y, c = _position()
        my_chip = 2 * x + y
        chips = [(1 - x, y), (x, 1 - y), (1 - x, 1 - y)]
        if arrivals:
            return [pltpu.make_async_remote_copy(
                src_ref=ins[i].at[my_chip], dst_ref=outs[i].at[2 * px + py], send_sem=send_sems.at[i, k],
                recv_sem=recv_sems.at[i, k], device_id=(px, py, c), device_id_type=MESH) for k, (px, py) in enumerate(chips)]
        local = pltpu.make_async_copy(ins[i].at[my_chip], outs[i].at[my_chip], local_sems.at[i])
        return local, [pltpu.make_async_remote_copy(
            src_ref=ins[i].at[2 * px + py], dst_ref=outs[i].at[my_chip], send_sem=send_sems.at[i, k],
            recv_sem=recv_sems.at[i, k], device_id=(px, py, c), device_id_type=MESH) for k, (px, py) in enumerate(chips)]

    def start(self, ins, outs, sems):
        for i in range(len(ins)):
            local, remote = self._copies(ins, outs, sems, i, False)
            local.start()
            for cp in remote:
                cp.start()

    def finish(self, ins, outs, sems):
        for i in range(len(ins)):
            for cp in self._copies(ins, outs, sems, i, True):
                cp.wait_recv()
            local, remote = self._copies(ins, outs, sems, i, False)
            for cp in remote:
                cp.wait_send()
            local.wait()


def _slab_shape(g, cols):
    return (g.shape[0], _div(g.shape[1], N_DEV)) if cols else g.shape[1:]


class PairExchangeJob:
    mid = None

    def __init__(self, grads, cols):
        self.ins, self.cols = list(grads), list(cols)
        n = len(grads)
        self.out_shapes = [SDS((4,) + _slab_shape(g, cl), g.dtype) for g, cl in zip(grads, cols)]
        self.scratch = [pltpu.SemaphoreType.DMA((n, 4)), pltpu.SemaphoreType.DMA((n, 4))]

    def _copies(self, ins, outs, sems):
        send_sems, recv_sems = sems
        x, y, c = _position()

        def part(i, dst):
            if not self.cols[i]:
                return ins[i].at[dst]
            cw = _slab_shape(self.ins[i], True)[1]
            return ins[i].at[:, pl.ds(pl.multiple_of(dst * cw, LANE), cw)]

        return [pltpu.make_async_remote_copy(src_ref=part(i, 2 * j + 1 - c), dst_ref=outs[i].at[j], send_sem=send_sems.at[i, j],
                                             recv_sem=recv_sems.at[i, j], device_id=(x, y, 1 - c), device_id_type=MESH)
                for i in range(len(ins)) for j in range(4)]

    def start(self, ins, outs, sems):
        for cp in self._copies(ins, outs, sems):
            cp.start()

    def finish(self, ins, outs, sems):
        for cp in self._copies(ins, outs, sems):
            cp.wait()


def _host_begin(job, step, steps, ins, outs, sems):
    pl.when(step == 0)(lambda: job.start(ins, outs, sems))
    if job.mid is not None:
        pl.when(step == (steps * 45) // 100)(lambda: job.relay(ins, outs, sems))
        pl.when(step == (steps * 85) // 100)(lambda: job.mid(ins, outs, sems))


def run_job(job, name):
    n = len(job.ins)

    def body(*refs):
        ins, outs, sems = refs[:n], refs[n:2 * n], refs[2 * n:]
        job.start(ins, outs, sems)
        if job.mid is not None:
            job.relay(ins, outs, sems)
            job.mid(ins, outs, sems)
        job.finish(ins, outs, sems)

    return pl.pallas_call(body, name=name, out_shape=job.out_shapes, in_specs=[ANY] * n, out_specs=[ANY] * n,
                          scratch_shapes=job.scratch)(*job.ins)


def pair_add(grad, cols, other, name):
    r, c = _slab_shape(grad, cols)
    tr = _tile(r, 256)
    parity = lax.axis_index("c").astype(jnp.int32).reshape(1)

    def body(par_ref, a_ref, b_ref, o_ref):
        o_ref[...] = (a_ref[...].astype(F32) + b_ref[...].astype(F32)).astype(BF16)

    spec = pl.BlockSpec((None, tr, c), lambda j, i, par: (j, i, 0))
    if cols:
        own = pl.BlockSpec((tr, c), lambda j, i, par: (i, 2 * j + par[0]))
    else:
        own = pl.BlockSpec((None, tr, c), lambda j, i, par: (2 * j + par[0], i, 0))
    return pl.pallas_call(
        body, name=name, out_shape=SDS(other.shape, BF16),
        grid_spec=pltpu.PrefetchScalarGridSpec(num_scalar_prefetch=1, grid=(4, r // tr), in_specs=[own, spec], out_specs=spec),
        compiler_params=_cp(("parallel", "parallel")))(parity, grad, other)


def adamw(parts, w, m, v, name):
    p, r, c = parts.shape
    tr = _tile(r, 128 if c > 1024 else 256)

    def body(p_ref, w_ref, m_ref, v_ref, g_out, d_out, m_out, v_out):
        g = p_ref[0].astype(F32)
        for j in range(1, p):
            g = g + p_ref[j].astype(F32)
        mn = ADAM_B1 * m_ref[...] + (1.0 - ADAM_B1) * g
        vn = ADAM_B2 * v_ref[...] + (1.0 - ADAM_B2) * jnp.square(g)
        m_hat = mn / (1.0 - ADAM_B1 ** ADAM_STEP)
        v_hat = vn / (1.0 - ADAM_B2 ** ADAM_STEP)
        g_out[...] = g
        d_out[...] = -ADAM_LR * (m_hat / (jnp.sqrt(v_hat) + ADAM_EPS) + ADAM_WD * w_ref[...])
        m_out[...] = mn
        v_out[...] = vn

    spec = pl.BlockSpec((tr, c), lambda i: (i, 0))
    return pl.pallas_call(
        body, name=name, grid=(r // tr,),
        in_specs=[pl.BlockSpec((p, tr, c), lambda i: (0, i, 0)), spec, spec, spec],
        out_specs=[spec] * 4, out_shape=[SDS((r, c), F32)] * 4, compiler_params=_cp(("parallel",)))(parts, w, m, v)


def matmul(a, b, *, mode, out_dtype, name, tm=1024, tn=1024, tk=512, a_relu2=False, resid=None, relu2_grad_of=None,
           b_cols=False, relu2_out=False, rms_gain=None, loss_target=None, side=None, tail=None):
    if b_cols:
        nb, brows, bc = b.shape
        bshape = (brows, nb * bc)
    else:
        bshape = b.shape
    head_k = head_n = None
    if mode == "nn":
        (m, k), (k2, n) = a.shape, bshape
    elif mode == "nt":
        (m, k), (n, k2) = a.shape, bshape
        if tail is not None:
            head_k, k = k, k + tail.shape[1]
    else:
        (k, m), (k2, n) = a.shape, bshape
        if tail is not None:
            head_n, n = n, n + tail.shape[1]
    assert k == k2 and (tail is None or mode != "nn"), (a.shape, b.shape, mode)
    b_whole = b_cols and mode == "nt" and tk >= k
    tm, tn, tk = _tile(m, tm), _tile(n, tn), _tile(k, tk)
    if b_cols and mode == "nn":
        tn = _tile(bc, tn)
    if b_cols and mode == "nt" and not b_whole:
        tk = _tile(bc, tk)
    nk = k // tk
    ni, nj = m // tm, n // tn
    nk_head = _div(head_k, tk) if head_k is not None else None
    nj_head = _div(head_n, tn) if head_n is not None else None
    use_acc = nk > 1 or tail is not None
    dims = {"nn": NN, "nt": NT, "tn": TN}[mode]
    extras = [e for e in (resid, relu2_grad_of) if e is not None]
    tails = [tail] if tail is not None else []
    n_side = len(side.ins) if side is not None else 0
    loss = loss_target is not None
    if loss:
        extras.append(loss_target)
    gains = [rms_gain] if rms_gain is not None else []
    assert not gains or (tn == n and not relu2_out), (tn, n)
    assert not loss or (gains and resid is not None and relu2_grad_of is None and out_dtype == F32)
    n_main = 2 if (relu2_out or gains) else 1
    n_loss = 2 if loss else 0

    def body(*refs):
        a_ref, b_ref = refs[:2]
        t_ref = refs[2] if tails else None
        n_op = 2 + len(tails)
        e_refs = refs[n_op:n_op + len(extras)]
        n_pre = n_op + len(extras) + len(gains)
        g_ref = refs[n_pre - 1] if gains else None
        n_in = n_pre + n_side
        o_ref = refs[n_in]
        act_ref = refs[n_in + 1] if n_main == 2 else None
        dg_ref, l_ref = refs[n_in + n_main:n_in + n_main + n_loss] if loss else (None, None)
        acc_ref = refs[n_in + n_main + n_loss + n_side] if use_acc else None
        if side is not None:
            s_ins = refs[n_pre:n_in]
            s_outs = refs[n_in + n_main + n_loss:n_in + n_main + n_loss + n_side]
            s_sems = refs[len(refs) - len(side.scratch):]
            step = (pl.program_id(0) * nj + pl.program_id(1)) * nk + pl.program_id(2)
            _host_begin(side, step, ni * nj * nk, s_ins, s_outs, s_sems)

        def operands(a_from=a_ref, b_from=b_ref):
            av = a_from[...]
            if a_relu2:
                av = jnp.square(jnp.maximum(av.astype(F32), 0.0))
            return av.astype(BF16), b_from[...].astype(BF16)

        def finish(r):
            e = list(e_refs)
            if resid is not None:
                r = r + e.pop(0)[...]
            if relu2_grad_of is not None:
                r = r * (2.0 * jnp.maximum(e.pop(0)[...], 0.0))
            if loss:
                gv = g_ref[...]
                inv = lax.rsqrt(jnp.mean(r * r, axis=-1, keepdims=True) + RMS_EPS)
                err = r * inv * gv - e.pop(0)[...]
                lpart = 0.5 * jnp.sum(jnp.mean(err * err, axis=-1, keepdims=True), axis=0, keepdims=True)
                dx, part = _rms_bwd_rows(r, gv, err * (1.0 / n))
                o_ref[...] = dx
                act_ref[...] = dx.astype(BF16)

                @pl.when(pl.program_id(0) == 0)
                def _():
                    dg_ref[...] = jnp.zeros_like(dg_ref)
                    l_ref[...] = jnp.zeros_like(l_ref)

                dg_ref[...] += part
                l_ref[...] += jnp.broadcast_to(lpart, l_ref.shape)
                return
            o_ref[...] = r.astype(out_dtype)
            if relu2_out:
                act_ref[...] = jnp.square(jnp.maximum(r, 0.0)).astype(BF16)
            if gains:
                inv = lax.rsqrt(jnp.mean(r * r, axis=-1, keepdims=True) + RMS_EPS)
                act_ref[...] = (r * inv * g_ref[...]).astype(BF16)

        if b_whole:
            av = a_ref[...].astype(BF16)
            finish(sum(_dot(av[:, kb * bc:(kb + 1) * bc], b_ref[kb].astype(BF16), NT) for kb in range(nb)))
        elif not use_acc:
            av, bv = operands()
            finish(_dot(av, bv, dims))
        else:
            kk = pl.program_id(2)

            def accumulate(a_from, b_from):
                def product():
                    av, bv = operands(a_from, b_from)
                    return _dot(av, bv, dims)

                if nk == 1:
                    finish(product())
                    return

                @pl.when(kk == 0)
                def _():
                    acc_ref[...] = product()

                @pl.when((kk > 0) & (kk < nk - 1))
                def _():
                    acc_ref[...] += product()

                @pl.when(kk == nk - 1)
                def _():
                    finish(acc_ref[...] + product())

            if not tails:
                accumulate(a_ref, b_ref)
            elif mode == "nt":
                pl.when(kk < nk_head)(lambda: accumulate(a_ref, b_ref))
                pl.when(kk >= nk_head)(lambda: accumulate(t_ref, b_ref))
            else:
                in_head = pl.program_id(1) < nj_head
                pl.when(in_head)(lambda: accumulate(a_ref, b_ref))
                pl.when(jnp.logical_not(in_head))(lambda: accumulate(a_ref, t_ref))

        if side is not None:
            pl.when(step == ni * nj * nk - 1)(lambda: side.finish(s_ins, s_outs, s_sems))

    a_spec = {"nn": pl.BlockSpec((tm, tk), lambda i, j, kk: (i, kk)),
              "nt": pl.BlockSpec((tm, tk), lambda i, j, kk: (i, kk)),
              "tn": pl.BlockSpec((tk, tm), lambda i, j, kk: (kk, i))}[mode]
    t_specs = []
    if tails and mode == "nt":
        a_spec = pl.BlockSpec((tm, tk), lambda i, j, kk: (i, jnp.minimum(kk, nk_head - 1)))
        t_specs = [pl.BlockSpec((tm, tk), lambda i, j, kk: (i, jnp.maximum(kk - nk_head, 0)))]
    if tails and mode == "tn":
        t_specs = [pl.BlockSpec((tk, tn), lambda i, j, kk: (kk, jnp.maximum(j - nj_head, 0)))]
    if tails and mode == "tn":
        b_spec = pl.BlockSpec((tk, tn), lambda i, j, kk: (kk, jnp.minimum(j, nj_head - 1)))
    elif not b_cols:
        b_spec = {"nn": pl.BlockSpec((tk, tn), lambda i, j, kk: (kk, j)),
                  "nt": pl.BlockSpec((tn, tk), lambda i, j, kk: (j, kk)),
                  "tn": pl.BlockSpec((tk, tn), lambda i, j, kk: (kk, j))}[mode]
    elif mode == "nn":
        per = bc // tn
        b_spec = pl.BlockSpec((None, tk, tn), lambda i, j, kk: (j // per, kk, j % per))
    elif b_whole:
        b_spec = pl.BlockSpec((nb, tn, bc), lambda i, j, kk: (0, j, 0))
    else:
        assert mode == "nt", mode
        per = bc // tk
        b_spec = pl.BlockSpec((None, tn, tk), lambda i, j, kk: (kk // per, j, kk % per))
    e_spec = pl.BlockSpec((tm, tn), lambda i, j, kk: (i, j))
    main_shapes = [SDS((m, n), out_dtype)] + ([SDS((m, n), BF16)] if n_main == 2 else [])
    g_specs = [pl.BlockSpec((1, tn), lambda i, j, kk: (0, j))] * len(gains)
    l_specs = [pl.BlockSpec((1, tn), lambda i, j, kk: (0, j)), pl.BlockSpec((1, LANE), lambda i, j, kk: (0, 0))] if loss else []
    l_shapes = [SDS((1, n), F32), SDS((1, LANE), F32)] if loss else []
    res = pl.pallas_call(
        body, name=name, grid=(ni, nj, nk),
        in_specs=[a_spec, b_spec] + t_specs + [e_spec] * len(extras) + g_specs + [ANY] * n_side,
        out_specs=[e_spec] * n_main + l_specs + [ANY] * n_side,
        out_shape=main_shapes + l_shapes + (side.out_shapes if side is not None else []),
        scratch_shapes=([pltpu.VMEM((tm, tn), F32)] if use_acc else []) + (side.scratch if side is not None else []),
        compiler_params=_cp(("arbitrary", "arbitrary", "arbitrary")))(a, b, *tails, *extras, *gains, *(side.ins if side is not None else []))
    return res if len(res) > 1 else res[0]


def rmsnorm_fwd(x, g, name, side=None):
    s, d = x.shape
    tm = _tile(s, 256)
    steps = s // tm
    n_side = len(side.ins) if side is not None else 0

    def body(*refs):
        x_ref, g_ref, o_ref = refs[0], refs[1], refs[2 + n_side]
        if side is not None:
            s_ins, s_outs, s_sems = refs[2:2 + n_side], refs[3 + n_side:3 + 2 * n_side], refs[3 + 2 * n_side:]
            _host_begin(side, pl.program_id(0), steps, s_ins, s_outs, s_sems)
        xv = x_ref[...]
        r = lax.rsqrt(jnp.mean(xv * xv, axis=-1, keepdims=True) + RMS_EPS)
        o_ref[...] = (xv * r * g_ref[...]).astype(BF16)
        if side is not None:
            pl.when(pl.program_id(0) == steps - 1)(lambda: side.finish(s_ins, s_outs, s_sems))

    row = pl.BlockSpec((tm, d), lambda i: (i, 0))
    res = pl.pallas_call(
        body, name=name, grid=(steps,), in_specs=[row, pl.BlockSpec((1, d), lambda i: (0, 0))] + [ANY] * n_side,
        out_specs=[row] + [ANY] * n_side, out_shape=[SDS((s, d), BF16)] + (side.out_shapes if side is not None else []),
        scratch_shapes=side.scratch if side is not None else [],
        compiler_params=_cp(("arbitrary",)))(x, g, *(side.ins if side is not None else []))
    return res if side is not None else res[0]


def _rms_bwd_rows(xv, gv, dy):
    r = lax.rsqrt(jnp.mean(xv * xv, axis=-1, keepdims=True) + RMS_EPS)
    xh = xv * r
    dxh = dy * gv
    dx = r * (dxh - xh * jnp.mean(dxh * xh, axis=-1, keepdims=True))
    return dx, jnp.sum(dy * xh, axis=0, keepdims=True)


def rmsnorm_bwd(x, g, dn, resid, name, bf16_copy=False):
    s, d = x.shape
    tm = _tile(s, 256)
    has_r = resid is not None

    def body(*refs):
        x_ref, g_ref, dn_ref = refs[:3]
        dx_ref, dg_ref = refs[3 + has_r:5 + has_r]
        dx, part = _rms_bwd_rows(x_ref[...], g_ref[...], dn_ref[...].astype(F32))
        if has_r:
            dx = dx + refs[3][...]
        dx_ref[...] = dx
        if bf16_copy:
            refs[5 + has_r][...] = dx.astype(BF16)

        @pl.when(pl.program_id(0) == 0)
        def _():
            dg_ref[...] = jnp.zeros_like(dg_ref)

        dg_ref[...] += part

    row = pl.BlockSpec((tm, d), lambda i: (i, 0))
    vec = pl.BlockSpec((1, d), lambda i: (0, 0))
    ins = [x, g, dn] + ([resid] if has_r else [])
    return pl.pallas_call(body, name=name, grid=(s // tm,), in_specs=[row, vec, row] + ([row] if has_r else []),
                          out_specs=[row, vec] + ([row] if bf16_copy else []),
                          out_shape=[SDS((s, d), F32), SDS((1, d), F32)] + ([SDS((s, d), BF16)] if bf16_copy else []),
                          compiler_params=_cp(("arbitrary",)))(*ins)


def merge(p_all, ya, yb, yc, wa, wb, wc, dm, lay, name):
    s, d = ya.shape[0], lay.d
    wcols = wa.shape[2]
    bwd = dm is not None
    tm, tn = _tile(s, 2048), _tile(wcols, 512)
    nj, per = d // tn, wcols // tn

    y_specs = [pl.BlockSpec((tm, y.shape[1]), lambda i, j, *_: (i, 0)) for y in (ya, yb, yc)]
    w_specs = [pl.BlockSpec((None, w.shape[1], tn), lambda i, j, *_: (j // per, 0, j % per)) for w in (wa, wb, wc)]
    o_spec = pl.BlockSpec((tm, tn), lambda i, j, *_: (i, j))
    if not bwd:
        def body(ga, gb, gc, ya_r, yb_r, yc_r, wa_r, wb_r, wc_r, o_ref):
            ts = [_dot(y[...], w[...], NN) for y, w in ((ya_r, wa_r), (yb_r, wb_r), (yc_r, wc_r))]
            gs = [_sigmoid(g[...]) for g in (ga, gb, gc)]
            o_ref[...] = (gs[0] * ts[0] + gs[1] * ts[1] + gs[2] * ts[2]).astype(BF16)

        gate_specs = [pl.BlockSpec((tm, tn), lambda i, j, b=b: (i, b * nj + j)) for b in range(3)]
        return pl.pallas_call(
            body, name=name, grid=(s // tm, nj), in_specs=gate_specs + y_specs + w_specs,
            out_specs=o_spec, out_shape=SDS((s, d), BF16),
            compiler_params=_cp(("parallel", "parallel")))(p_all, p_all, p_all, ya, yb, yc, wa, wb, wc)

    def body_bwd(g_r, ya_r, yb_r, yc_r, wa_r, wb_r, wc_r, dm_r, dg_o, dta_o, dtb_o, dtc_o):
        for k, (y, w, dt_o) in enumerate(((ya_r, wa_r, dta_o), (yb_r, wb_r, dtb_o), (yc_r, wc_r, dtc_o))):
            @pl.when(pl.program_id(2) == k)
            def _(y=y, w=w, dt_o=dt_o):
                t = _dot(y[...], w[...], NN)
                g = _sigmoid(g_r[...])
                dmv = dm_r[...]
                dg_o[...] = (dmv * t * (g * (1.0 - g))).astype(BF16)
                dt_o[...] = (dmv * g).astype(BF16)

    gate_spec = pl.BlockSpec((tm, tn), lambda i, j, b: (i, b * nj + j))
    return pl.pallas_call(
        body_bwd, name=name, grid=(s // tm, nj, 3), in_specs=[gate_spec] + y_specs + w_specs + [o_spec],
        out_specs=[gate_spec, o_spec, o_spec, o_spec], out_shape=[SDS((s, 3 * d), BF16)] + [SDS((s, d), BF16)] * 3,
        compiler_params=_cp(("arbitrary", "arbitrary", "arbitrary")))(p_all, ya, yb, yc, wa, wb, wc, dm)


SWA_PAIRS = SWA_G // 2


def _swa_probs(qs, kcs, sinks, firsts):
    shape = (qs[0].shape[0], 2 * SWA_W)
    qi = lax.broadcasted_iota(jnp.int32, shape, 0) % SWA_W
    kj = lax.broadcasted_iota(jnp.int32, shape, 1)
    band = (kj > qi) & (kj <= qi + SWA_W)
    masks = [band if f is False else band & ((kj >= SWA_W) | jnp.logical_not(f)) for f in firsts]
    ss = [jnp.where(mask, _dot(q, kc, NT) * (SWA_HD ** -0.5), NEG) for q, kc, mask in zip(qs, kcs, masks)]
    ms = [jnp.maximum(jnp.max(s, axis=-1, keepdims=True), sink) for s, sink in zip(ss, sinks)]
    ps = [jnp.exp(s - m) for s, m in zip(ss, ms)]
    es = [jnp.exp(sink - m) for sink, m in zip(sinks, ms)]
    inv = [1.0 / (jnp.sum(p, axis=-1, keepdims=True) + e) for p, e in zip(ps, es)]
    return [p * i for p, i in zip(ps, inv)], [e * i for e, i in zip(es, inv)]


def _swa_stack(ref, h, r0=0):
    return jnp.concatenate([ref[pl.ds(r0, SWA_W), pl.ds((h * SWA_PAIRS + p) * LANE, LANE)] for p in range(SWA_PAIRS)], axis=0)


def _swa_unstack(ref, h, val, r0=0):
    for p in range(SWA_PAIRS):
        ref[pl.ds(r0, SWA_W), pl.ds((h * SWA_PAIRS + p) * LANE, LANE)] = val[p * SWA_W:(p + 1) * SWA_W]


def _swa_sink_col(sk_ref, h, second):
    pair = lax.broadcasted_iota(jnp.int32, (SWA_PAIRS * SWA_W, 1), 0) // SWA_W
    col = jnp.zeros((SWA_PAIRS * SWA_W, 1), F32)
    for p in range(SWA_PAIRS):
        hh = h * SWA_G + 2 * p + second
        col = jnp.where(pair == p, sk_ref[0:1, hh:hh + 1], col)
    return col


def _swa_kv_tiles(t, h):
    lane = lax.broadcasted_iota(jnp.int32, t.shape, 1)
    moved = pltpu.roll(t, SWA_HD, axis=1)
    low, high = (t, moved) if h == 0 else (moved, t)
    return jnp.where(lane < SWA_HD, low, 0.0).astype(BF16), jnp.where(lane >= SWA_HD, high, 0.0).astype(BF16)


def _swa_kv_grad(g_low, g_high, h):
    lane = lax.broadcasted_iota(jnp.int32, g_low.shape, 1)
    if h == 0:
        return jnp.where(lane < SWA_HD, g_low + pltpu.roll(g_high, SWA_HD, axis=1), 0.0)
    return jnp.where(lane >= SWA_HD, pltpu.roll(g_low, SWA_HD, axis=1) + g_high, 0.0)


def swa_fwd(p_all, sinks, lay, name):
    s = p_all.shape[0]
    w = SWA_W
    nb2 = _div(s, 2 * w)

    def body(q_ref, kc_ref, kp_ref, vc_ref, vp_ref, sk_ref, o_ref):
        first = pl.program_id(0) == 0
        kt = [jnp.concatenate([kp_ref[...], kc_ref[0:w, :]], axis=0), kc_ref[...]]
        vt = [jnp.concatenate([vp_ref[...], vc_ref[0:w, :]], axis=0), vc_ref[...]]
        units = [(b, h, e) for b in range(2) for h in range(SWA_HKV) for e in range(2)]
        ks = [[_swa_kv_tiles(kt[b], h) for h in range(SWA_HKV)] for b in range(2)]
        vs = [[_swa_kv_tiles(vt[b], h) for h in range(SWA_HKV)] for b in range(2)]
        qs = [[_swa_stack(q_ref, h, b * w).astype(BF16) for h in range(SWA_HKV)] for b in range(2)]
        ps, _ = _swa_probs([qs[b][h] for b, h, e in units], [ks[b][h][e] for b, h, e in units],
                           [_swa_sink_col(sk_ref, h, e) for b, h, e in units], [first if b == 0 else False for b, h, e in units])
        os = [_dot(p.astype(BF16), vs[b][h][e], NN) for p, (b, h, e) in zip(ps, units)]
        for u in range(0, len(units), 2):
            b, h, _ = units[u]
            _swa_unstack(o_ref, h, (os[u] + os[u + 1]).astype(BF16), b * w)

    q_spec = pl.BlockSpec((2 * w, Q_W), lambda n: (n, _div(lay.q, Q_W)))
    cur = lambda off: pl.BlockSpec((2 * w, KV_W), lambda n: (n, _div(off, KV_W)))
    prev = lambda off: pl.BlockSpec((w, KV_W), lambda n: (jnp.maximum(2 * n - 1, 0), _div(off, KV_W)))
    return pl.pallas_call(
        body, name=name, grid=(nb2,),
        in_specs=[q_spec, cur(lay.k), prev(lay.k), cur(lay.v), prev(lay.v), pl.BlockSpec(sinks.shape, lambda n: (0, 0))],
        out_specs=pl.BlockSpec((2 * w, Q_W), lambda n: (n, 0)), out_shape=SDS((s, Q_W), BF16),
        compiler_params=_cp(("parallel",)))(p_all, p_all, p_all, p_all, p_all, sinks)


def swa_bwd(p_all, sinks, dy, lay, name):
    s = p_all.shape[0]
    w = SWA_W
    nb = _div(s, w)
    nb2 = _div(nb, 2)

    def body(q_ref, kc_ref, kp_ref, vc_ref, vp_ref, sk_ref, do_ref,
             dq_ref, dke_ref, dko_ref, dve_ref, dvo_ref, ds_ref, kcar, vcar):
        n = pl.program_id(0)
        first = n == 0

        @pl.when(first)
        def _():
            kcar[...] = jnp.zeros_like(kcar)
            vcar[...] = jnp.zeros_like(vcar)
            ds_ref[...] = jnp.zeros_like(ds_ref)

        @pl.when(n < nb2)
        def _():
            lane = lax.broadcasted_iota(jnp.int32, (1, LANE), 1)
            dsink = jnp.zeros((1, LANE), F32)
            units = [(b, h, e) for b in range(2) for h in range(SWA_HKV) for e in range(2)]
            kt = [jnp.concatenate([kp_ref[...], kc_ref[0:w, :]], axis=0), kc_ref[...]]
            vt = [jnp.concatenate([vp_ref[...], vc_ref[0:w, :]], axis=0), vc_ref[...]]
            ks = [[_swa_kv_tiles(kt[b], h) for h in range(SWA_HKV)] for b in range(2)]
            vs = [[_swa_kv_tiles(vt[b], h) for h in range(SWA_HKV)] for b in range(2)]
            qs = [[_swa_stack(q_ref, h, b * w).astype(BF16) for h in range(SWA_HKV)] for b in range(2)]
            dos = [[_swa_stack(do_ref, h, b * w).astype(BF16) for h in range(SWA_HKV)] for b in range(2)]
            ps, psinks = _swa_probs([qs[b][h] for b, h, e in units], [ks[b][h][e] for b, h, e in units],
                                    [_swa_sink_col(sk_ref, h, e) for b, h, e in units],
                                    [first if b == 0 else False for b, h, e in units])
            dps = [_dot(dos[b][h], vs[b][h][e], NT) for b, h, e in units]
            dvs = [_dot(p.astype(BF16), dos[b][h], TN) for p, (b, h, e) in zip(ps, units)]
            rss = [jnp.sum(dp * p, axis=-1, keepdims=True) for dp, p in zip(dps, ps)]
            dsb = [(p * (dp - rs) * (SWA_HD ** -0.5)).astype(BF16) for p, dp, rs in zip(ps, dps, rss)]
            dqs = [_dot(d, ks[b][h][e], NN) for d, (b, h, e) in zip(dsb, units)]
            dks = [_dot(d, qs[b][h], TN) for d, (b, h, e) in zip(dsb, units)]
            for u, (b, h, e) in enumerate(units):
                psr = psinks[u] * rss[u]
                for pr in range(SWA_PAIRS):
                    hh = h * SWA_G + 2 * pr + e
                    dsink = dsink + jnp.where(lane == hh, -jnp.sum(psr[pr * w:(pr + 1) * w], axis=0, keepdims=True), 0.0)
            dk_tiles = [jnp.zeros((2 * w, KV_W), F32) for _ in range(2)]
            dv_tiles = [jnp.zeros((2 * w, KV_W), F32) for _ in range(2)]
            for u in range(0, len(units), 2):
                b, h, _ = units[u]
                _swa_unstack(dq_ref, h, (dqs[u] + dqs[u + 1]).astype(BF16), b * w)
                dk_tiles[b] = dk_tiles[b] + _swa_kv_grad(dks[u], dks[u + 1], h)
                dv_tiles[b] = dv_tiles[b] + _swa_kv_grad(dvs[u], dvs[u + 1], h)
            dko_ref[...] = (kcar[...] + dk_tiles[0][:w]).astype(BF16)
            dvo_ref[...] = (vcar[...] + dv_tiles[0][:w]).astype(BF16)
            dke_ref[...] = (dk_tiles[0][w:] + dk_tiles[1][:w]).astype(BF16)
            dve_ref[...] = (dv_tiles[0][w:] + dv_tiles[1][:w]).astype(BF16)
            kcar[...] = dk_tiles[1][w:]
            vcar[...] = dv_tiles[1][w:]
            ds_ref[...] += dsink

        @pl.when(n == nb2)
        def _():
            dko_ref[...] = kcar[...].astype(BF16)
            dvo_ref[...] = vcar[...].astype(BF16)

    last = nb2 - 1
    now = lambda n: jnp.minimum(n, last)
    q_spec = pl.BlockSpec((2 * w, Q_W), lambda n: (now(n), _div(lay.q, Q_W)))
    cur = lambda off: pl.BlockSpec((2 * w, KV_W), lambda n: (now(n), _div(off, KV_W)))
    prev = lambda off: pl.BlockSpec((w, KV_W), lambda n: (jnp.clip(2 * n - 1, 0, nb - 1), _div(off, KV_W)))
    row = pl.BlockSpec((2 * w, Q_W), lambda n: (now(n), 0))
    even = pl.BlockSpec((w, KV_W), lambda n: (now(n), 0))
    odd = pl.BlockSpec((w, KV_W), lambda n: (jnp.maximum(n - 1, 0), 0))
    half = SDS((s // 2, KV_W), BF16)
    dq, dke, dko, dve, dvo, dsk = pl.pallas_call(
        body, name=name, grid=(nb2 + 1,),
        in_specs=[q_spec, cur(lay.k), prev(lay.k), cur(lay.v), prev(lay.v), pl.BlockSpec(sinks.shape, lambda n: (0, 0)), row],
        out_specs=[row, even, odd, even, odd, pl.BlockSpec((1, LANE), lambda n: (0, 0))],
        out_shape=[SDS((s, Q_W), BF16), half, half, half, half, SDS((1, LANE), F32)],
        scratch_shapes=[pltpu.VMEM((w, KV_W), F32), pltpu.VMEM((w, KV_W), F32)],
        compiler_params=_cp(("arbitrary",)))(p_all, p_all, p_all, p_all, p_all, sinks, dy)

    def interleave(ev, od):
        return jnp.stack([ev.reshape(nb2, w, KV_W), od.reshape(nb2, w, KV_W)], axis=1).reshape(s, KV_W)

    return dq, interleave(dke, dko), interleave(dve, dvo), dsk


def _xa_probs(qs, mks):
    ss = [_dot(q, mk, NT) * (XA_D ** -0.5) for q, mk in zip(qs, mks)]
    ps = [jnp.exp(s - jnp.max(s, axis=-1, keepdims=True)) for s in ss]
    inv = [1.0 / jnp.sum(p, axis=-1, keepdims=True) for p in ps]
    return [p * i for p, i in zip(ps, inv)]


def xattn_fwd(p_all, mkv, lay, name):
    s, nm = p_all.shape[0], mkv.shape[0]
    tm = _tile(s, 512)

    def body(q_ref, mkv_ref, o_ref):
        heads = range(XA_H)
        cols = [pl.ds(h * XA_D, XA_D) for h in heads]
        ps = _xa_probs([q_ref[:, c].astype(BF16) for c in cols], [mkv_ref[:, c] for c in cols])
        os = [_dot(ps[h].astype(BF16), mkv_ref[:, pl.ds(XA_W + h * XA_D, XA_D)], NN) for h in heads]
        for h in heads:
            o_ref[:, cols[h]] = os[h].astype(BF16)

    return pl.pallas_call(
        body, name=name, grid=(s // tm,),
        in_specs=[pl.BlockSpec((tm, XA_W), lambda i: (i, _div(lay.qc, XA_W))), pl.BlockSpec((nm, 2 * XA_W), lambda i: (0, 0))],
        out_specs=pl.BlockSpec((tm, XA_W), lambda i: (i, 0)), out_shape=SDS((s, XA_W), BF16),
        compiler_params=_cp(("parallel",)))(p_all, mkv)


def xattn_bwd(p_all, mkv, dy, lay, name):
    s, nm = p_all.shape[0], mkv.shape[0]
    tm = _tile(s, 512)

    def body(q_ref, mkv_ref, do_ref, dq_ref, dmkv_ref):
        @pl.when(pl.program_id(0) == 0)
        def _():
            dmkv_ref[...] = jnp.zeros_like(dmkv_ref)

        heads = range(XA_H)
        cols = [pl.ds(h * XA_D, XA_D) for h in heads]
        vcols = [pl.ds(XA_W + h * XA_D, XA_D) for h in heads]
        qs = [q_ref[:, c].astype(BF16) for c in cols]
        dos = [do_ref[:, c].astype(BF16) for c in cols]
        ps = _xa_probs(qs, [mkv_ref[:, c] for c in cols])
        dps = [_dot(dos[h], mkv_ref[:, vcols[h]], NT) for h in heads]
        dvs = [_dot(ps[h].astype(BF16), dos[h], TN) for h in heads]
        dsb = [(p * (dp - jnp.sum(dp * p, axis=-1, keepdims=True)) * (XA_D ** -0.5)).astype(BF16) for p, dp in zip(ps, dps)]
        dqs = [_dot(dsb[h], mkv_ref[:, cols[h]], NN) for h in heads]
        dks = [_dot(dsb[h], qs[h], TN) for h in heads]
        for h in heads:
            dq_ref[:, cols[h]] = dqs[h].astype(BF16)
            dmkv_ref[:, vcols[h]] += dvs[h]
            dmkv_ref[:, cols[h]] += dks[h]

    row = pl.BlockSpec((tm, XA_W), lambda i: (i, 0))
    full = pl.BlockSpec((nm, 2 * XA_W), lambda i: (0, 0))
    return pl.pallas_call(
        body, name=name, grid=(s // tm,),
        in_specs=[pl.BlockSpec((tm, XA_W), lambda i: (i, _div(lay.qc, XA_W))), full, row],
        out_specs=[row, full], out_shape=[SDS((s, XA_W), BF16), SDS((nm, 2 * XA_W), F32)],
        compiler_params=_cp(("arbitrary",)))(p_all, mkv, dy)


def _shift_down(cur, prev8, s):
    cat = jnp.concatenate([prev8, cur[0:8]], axis=0)
    return pltpu.roll(cur, s, axis=0), pltpu.roll(cat, s, axis=0)[8:16]


def _shift_up(cur, next8, s):
    tm = cur.shape[0]
    cat = jnp.concatenate([cur[tm - 8:tm], next8], axis=0)
    return pltpu.roll(cur, tm - s, axis=0), pltpu.roll(cat, 16 - s, axis=0)[0:8]


def _conv_rows(cur, prev8, w):
    main = w[GDN_CONV - 1:GDN_CONV] * cur
    top = w[GDN_CONV - 1:GDN_CONV] * cur[0:8]
    for sft in range(1, GDN_CONV):
        wi = w[GDN_CONV - 1 - sft:GDN_CONV - sft]
        a, b = _shift_down(cur, prev8, sft)
        main = main + wi * a
        top = top + wi * b
    return jnp.concatenate([top, main[8:]], axis=0)


def _conv_rows_bwd(cur, prev8, d, next8, w):
    tm = cur.shape[0]
    row = lax.broadcasted_iota(jnp.int32, (tm, 1), 0)
    main = w[GDN_CONV - 1:GDN_CONV] * d
    bot = w[GDN_CONV - 1:GDN_CONV] * d[tm - 8:tm]
    dws = [jnp.sum(d * cur, axis=0, keepdims=True)]
    for sft in range(1, GDN_CONV):
        wi = w[GDN_CONV - 1 - sft:GDN_CONV - sft]
        a, b = _shift_up(d, next8, sft)
        main = main + wi * a
        bot = bot + wi * b
        xa, xb = _shift_down(cur, prev8, sft)
        dws.append(jnp.sum(jnp.where(row >= 8, d * xa, 0.0), axis=0, keepdims=True)
                   + jnp.sum(d[0:8] * xb, axis=0, keepdims=True))
    return jnp.concatenate([main[:tm - 8], bot], axis=0), dws


def _gdn_chunk(xq, xk, xv, ab, gp, bdot=_bdot_plain):
    c = GDN_C
    nc = xq.shape[0] // c
    lane = lax.broadcasted_iota(jnp.int32, (c, LANE), 1)
    row = lax.broadcasted_iota(jnp.int32, (c, c), 0)
    col = lax.broadcasted_iota(jnp.int32, (c, c), 1)
    g_tile = -jnp.exp(gp[0:1, :]) * _softplus(ab + gp[1:2, :])
    b_tile = _sigmoid(ab)
    tri = (row >= col).astype(F32)
    qa, ka, va = _silu(xq), _silu(xk), _silu(xv)
    items = []
    for ci in range(nc):
        rs = slice(ci * c, (ci + 1) * c)
        gcum = _dot(tri, g_tile[rs], NN, HI)
        gcum_t = gcum.T
        for h in range(GDN_H):
            hs = slice(h * GDN_D, (h + 1) * GDN_D)
            q, k, v = qa[rs, hs], ka[rs, hs], va[rs, hs]
            q = q * lax.rsqrt(jnp.sum(q * q, axis=-1, keepdims=True) + L2_EPS) * (GDN_D ** -0.5)
            k = k * lax.rsqrt(jnp.sum(k * k, axis=-1, keepdims=True) + L2_EPS)
            gc = jnp.sum(jnp.where(lane == h, gcum, 0.0), axis=1, keepdims=True)
            beta = jnp.sum(jnp.where(lane == GDN_H + h, b_tile[rs], 0.0), axis=1, keepdims=True)
            decay = jnp.exp(jnp.where(row >= col, gc - gcum_t[h:h + 1, :], NEG))
            items.append((q, k, v, gc, beta, decay))
    kks = [bdot(k, k, NT) for (_, k, _, _, _, _) in items]
    xs = tuple(-jnp.where(row > col, it[4] * kk * it[5], 0.0) for it, kk in zip(items, kks))
    nns = _neumann(xs) if bdot is _bdot_plain else _neumann_vjp(xs)
    qks = [bdot(q, k, NT) for (q, k, _, _, _, _) in items]
    out = []
    for (q, k, v, gc, beta, decay), n, qk in zip(items, nns, qks):
        eg = jnp.exp(gc)
        vb = v * beta
        kbe = k * (beta * eg)
        gl = gc[c - 1:c, :]
        out.append((vb + bdot(n, vb, NN), kbe + bdot(n, kbe, NN), q * eg, k * jnp.exp(gl - gc), qk * decay, jnp.exp(gl)))
    return [out[ci * GDN_H:(ci + 1) * GDN_H] for ci in range(nc)]


GDN_CPS = 4


def _gdn_pre_specs(lay, t, tile):
    c0 = _div(lay.qkv, GDN_W)
    cur = [pl.BlockSpec((t, GDN_W), lambda n, j=j: (tile(n), c0 + j)) for j in range(3)]
    prev = [pl.BlockSpec((8, GDN_W), lambda n, j=j: (jnp.maximum(tile(n) * (t // 8) - 1, 0), c0 + j)) for j in range(3)]
    return cur + prev + [pl.BlockSpec((GDN_CONV, 3 * GDN_W), lambda n: (0, 0)),
                         pl.BlockSpec((t, LANE), lambda n: (tile(n), _div(lay.ab, LANE))),
                         pl.BlockSpec((8, LANE), lambda n: (0, 0))]


def _gdn_conv_inputs(x_refs, prev_refs, w_ref, first):
    out = []
    for j in range(3):
        prev8 = jnp.where(first, 0.0, prev_refs[j][...])
        out.append((x_refs[j][...], prev8, w_ref[:, pl.ds(j * GDN_W, GDN_W)]))
    return out


def gdn_pre_fwd(p_all, conv_w, gp, lay, name):
    s = p_all.shape[0]
    c = GDN_C
    n = _div(s, c)
    cps = _tile(n, GDN_CPS)
    t = cps * c

    def body(xq, xk, xv, pq, pk, pv, cw, ab, gp_ref, u_ref, w_ref, qd_ref, kd_ref, qk_ref, gl_ref):
        lane = lax.broadcasted_iota(jnp.int32, (1, LANE), 1)
        xs = [_conv_rows(*a) for a in _gdn_conv_inputs((xq, xk, xv), (pq, pk, pv), cw, pl.program_id(0) == 0)]
        chunks = _gdn_chunk(xs[0], xs[1], xs[2], ab[...], gp_ref[...])
        for ci, heads in enumerate(chunks):
            rs = pl.ds(ci * c, c)
            gl_row = jnp.zeros((1, LANE), F32)
            for h, (u, w, qd, kd, qk, gl) in enumerate(heads):
                hs = pl.ds(h * GDN_D, GDN_D)
                u_ref[rs, hs] = u
                w_ref[rs, hs] = w.astype(BF16)
                qd_ref[rs, hs] = qd.astype(BF16)
                kd_ref[rs, hs] = kd.astype(BF16)
                qk_ref[rs, pl.ds(h * c, c)] = qk.astype(BF16)
                gl_row = gl_row + jnp.where(lane == h, gl, 0.0)
            gl_ref[ci] = gl_row

    row = pl.BlockSpec((t, GDN_W), lambda n: (n, 0))
    return pl.pallas_call(
        body, name=name, grid=(n // cps,), in_specs=_gdn_pre_specs(lay, t, lambda n: n),
        out_specs=[row, row, row, row, pl.BlockSpec((t, GDN_H * c), lambda n: (n, 0)), pl.BlockSpec((cps, 1, LANE), lambda n: (n, 0, 0))],
        out_shape=[SDS((s, GDN_W), F32), SDS((s, GDN_W), BF16), SDS((s, GDN_W), BF16), SDS((s, GDN_W), BF16),
                   SDS((s, GDN_H * c), BF16), SDS((n, 1, LANE), F32)],
        compiler_params=_cp(("parallel",)))(p_all, p_all, p_all, p_all, p_all, p_all, conv_w, p_all, gp)


def gdn_pre_bwd(p_all, conv_w, gp, du, dw, dqd, dkd, dqk, dgl, lay, name):
    s = p_all.shape[0]
    c = GDN_C
    n = _div(s, c)
    cps = _tile(n, GDN_CPS)
    t = cps * c
    steps = n // cps
    chunk = functools.partial(_gdn_chunk, bdot=_bdot_vjp)

    def body(xq, xk, xv, pq, pk, pv, cw, ab, gp_ref, du_r, dw_r, dqd_r, dkd_r, dqk_r, dgl_r,
             dx_ref, dab_ref, dgp_ref, dcw_ref, carry):
        step = pl.program_id(0)

        @pl.when(step == 0)
        def _():
            dgp_ref[...] = jnp.zeros_like(dgp_ref)
            dcw_ref[...] = jnp.zeros_like(dcw_ref)
            carry[...] = jnp.zeros_like(carry)

        lane = lax.broadcasted_iota(jnp.int32, (1, LANE), 1)
        conv_in = _gdn_conv_inputs((xq, xk, xv), (pq, pk, pv), cw, step == steps - 1)
        xs = [_conv_rows(*a) for a in conv_in]
        _, vjp = jax.vjp(chunk, xs[0], xs[1], xs[2], ab[...], gp_ref[...])
        cts = []
        for ci in range(cps):
            rs = pl.ds(ci * c, c)
            heads = []
            for h in range(GDN_H):
                hs = pl.ds(h * GDN_D, GDN_D)
                dgl_h = jnp.sum(jnp.where(lane == h, dgl_r[ci], 0.0), axis=1, keepdims=True)
                heads.append((du_r[rs, hs], dw_r[rs, hs], dqd_r[rs, hs], dkd_r[rs, hs], dqk_r[rs, pl.ds(h * c, c)], dgl_h))
            cts.append(heads)
        *dxs, dab, dgp = vjp(cts)
        for j, (d, (cur, prev8, w)) in enumerate(zip(dxs, conv_in)):
            cols = pl.ds(j * GDN_W, GDN_W)
            dx, dws = _conv_rows_bwd(cur, prev8, d, carry[:, cols], w)
            carry[:, cols] = d[0:8]
            dx_ref[:, cols] = dx.astype(BF16)
            for sft in range(GDN_CONV):
                dcw_ref[GDN_CONV - 1 - sft:GDN_CONV - sft, cols] += dws[sft]
        dab_ref[...] = dab.astype(BF16)
        dgp_ref[...] += dgp

    tile = lambda i: steps - 1 - i
    row = pl.BlockSpec((t, GDN_W), lambda i: (tile(i), 0))
    return pl.pallas_call(
        body, name=name, grid=(steps,),
        in_specs=_gdn_pre_specs(lay, t, tile) + [row, row, row, row, pl.BlockSpec((t, GDN_H * c), lambda i: (tile(i), 0)),
                                                 pl.BlockSpec((cps, 1, LANE), lambda i: (tile(i), 0, 0))],
        out_specs=[pl.BlockSpec((t, 3 * GDN_W), lambda i: (tile(i), 0)), pl.BlockSpec((t, LANE), lambda i: (tile(i), 0)),
                   pl.BlockSpec((8, LANE), lambda i: (0, 0)), pl.BlockSpec((GDN_CONV, 3 * GDN_W), lambda i: (0, 0))],
        out_shape=[SDS((s, 3 * GDN_W), BF16), SDS((s, LANE), BF16), SDS((8, LANE), F32), SDS((GDN_CONV, 3 * GDN_W), F32)],
        scratch_shapes=[pltpu.VMEM((8, 3 * GDN_W), F32)],
        compiler_params=_cp(("arbitrary",)))(p_all, p_all, p_all, p_all, p_all, p_all, conv_w, p_all, gp,
                                             du, dw, dqd, dkd, dqk, dgl)


def _lane_scalar(row, h):
    lane = lax.broadcasted_iota(jnp.int32, row.shape, 1)
    return jnp.sum(jnp.where(lane == h, row, 0.0), axis=1, keepdims=True)


def _gdn_out_head(oh, zh, nw):
    return oh * lax.rsqrt(jnp.mean(oh * oh, axis=-1, keepdims=True) + RMS_EPS) * nw * _silu(zh)


def gdn_scan_fwd(u, w, qd, kd, qk, gl, p_all, nw, lay, name):
    s = u.shape[0]
    c = GDN_C
    n = _div(s, c)
    cps = _tile(n, GDN_CPS)
    t = cps * c

    def body(u_r, w_r, qd_r, kd_r, qk_r, gl_r, z_r, nw_r, o_ref, s_ref, y_ref, st):
        @pl.when(pl.program_id(0) == 0)
        def _():
            st[...] = jnp.zeros_like(st)

        heads = range(GDN_H)
        hs = [pl.ds(h * GDN_D, GDN_D) for h in heads]
        for ci in range(cps):
            rs = pl.ds(ci * c, c)
            s_ref[ci] = st[...]
            sh = [st[hs[h], :] for h in heads]
            shb = [x.astype(BF16) for x in sh]
            ws = [_dot(w_r[rs, hs[h]], shb[h], NN) for h in heads]
            qs = [_dot(qd_r[rs, hs[h]], shb[h], NN) for h in heads]
            vb = [(u_r[rs, hs[h]] - ws[h]).astype(BF16) for h in heads]
            ov = [_dot(qk_r[rs, pl.ds(h * c, c)], vb[h], NN) for h in heads]
            kv = [_dot(kd_r[rs, hs[h]], vb[h], TN) for h in heads]
            os = [qs[h] + ov[h] for h in heads]
            for h in heads:
                o_ref[rs, hs[h]] = os[h]
                y_ref[rs, hs[h]] = _gdn_out_head(os[h], z_r[rs, hs[h]], nw_r[...]).astype(BF16)
                st[hs[h], :] = sh[h] * _lane_scalar(gl_r[ci], h) + kv[h]

    row = pl.BlockSpec((t, GDN_W), lambda i: (i, 0))
    return pl.pallas_call(
        body, name=name, grid=(n // cps,),
        in_specs=[row, row, row, row, pl.BlockSpec((t, GDN_H * c), lambda i: (i, 0)), pl.BlockSpec((cps, 1, LANE), lambda i: (i, 0, 0)),
                  pl.BlockSpec((t, GDN_W), lambda i: (i, _div(lay.z, GDN_W))), pl.BlockSpec((1, GDN_D), lambda i: (0, 0))],
        out_specs=[row, pl.BlockSpec((cps, GDN_W, GDN_D), lambda i: (i, 0, 0)), row],
        out_shape=[SDS((s, GDN_W), F32), SDS((n, GDN_W, GDN_D), F32), SDS((s, GDN_W), BF16)],
        scratch_shapes=[pltpu.VMEM((GDN_W, GDN_D), F32)],
        compiler_params=_cp(("arbitrary",)))(u, w, qd, kd, qk, gl, p_all, nw)


def gdn_scan_bwd(u, w, qd, kd, qk, gl, states, o, dy, p_all, nw, lay, name):
    s = u.shape[0]
    c = GDN_C
    n = _div(s, c)
    cps = _tile(n, GDN_CPS)
    t = cps * c
    steps = n // cps

    def body(u_r, w_r, qd_r, kd_r, qk_r, gl_r, s_r, o_r, dy_r, z_r, nw_r,
             du_o, dw_o, dqd_o, dkd_o, dqk_o, dgl_o, dz_o, dnw_o, dst):
        @pl.when(pl.program_id(0) == 0)
        def _():
            dst[...] = jnp.zeros_like(dst)
            dnw_o[...] = jnp.zeros_like(dnw_o)

        lane = lax.broadcasted_iota(jnp.int32, (1, LANE), 1)
        heads = range(GDN_H)
        hs = [pl.ds(h * GDN_D, GDN_D) for h in heads]
        qs = [pl.ds(h * c, c) for h in heads]
        for ci in reversed(range(cps)):
            rs = pl.ds(ci * c, c)
            outs = [jax.vjp(_gdn_out_head, o_r[rs, hs[h]], z_r[rs, hs[h]], nw_r[...])[1](dy_r[rs, hs[h]].astype(F32))
                    for h in heads]
            for h in heads:
                dz_o[rs, hs[h]] = outs[h][1].astype(BF16)
                dnw_o[...] += outs[h][2]
            sh = [s_r[ci, hs[h], :] for h in heads]
            shb = [x.astype(BF16) for x in sh]
            ds_out = [dst[hs[h], :] for h in heads]
            dsb = [x.astype(BF16) for x in ds_out]
            dob = [outs[h][0].astype(BF16) for h in heads]
            ws = [_dot(w_r[rs, hs[h]], shb[h], NN) for h in heads]
            dv1 = [_dot(qk_r[rs, qs[h]], dob[h], TN) for h in heads]
            dv2 = [_dot(kd_r[rs, hs[h]], dsb[h], NN) for h in heads]
            dqd = [_dot(dob[h], shb[h], NT) for h in heads]
            dsq = [_dot(qd_r[rs, hs[h]], dob[h], TN) for h in heads]
            vb = [(u_r[rs, hs[h]] - ws[h]).astype(BF16) for h in heads]
            dv = [dv1[h] + dv2[h] for h in heads]
            dvb = [x.astype(BF16) for x in dv]
            dw = [_dot(dvb[h], shb[h], NT) for h in heads]
            dkd = [_dot(vb[h], dsb[h], NT) for h in heads]
            dqk = [_dot(dob[h], vb[h], NT) for h in heads]
            dsw = [_dot(w_r[rs, hs[h]], dvb[h], TN) for h in heads]
            dgl_row = jnp.zeros((1, LANE), F32)
            for h in heads:
                du_o[rs, hs[h]] = dv[h]
                dw_o[rs, hs[h]] = -dw[h]
                dqd_o[rs, hs[h]] = dqd[h]
                dkd_o[rs, hs[h]] = dkd[h]
                dqk_o[rs, qs[h]] = dqk[h]
                dgl_row = dgl_row + jnp.where(lane == h, jnp.sum(jnp.sum(ds_out[h] * sh[h], axis=1, keepdims=True), axis=0, keepdims=True), 0.0)
                dst[hs[h], :] = ds_out[h] * _lane_scalar(gl_r[ci], h) + dsq[h] - dsw[h]
            dgl_o[ci] = dgl_row

    rev = lambda i: steps - 1 - i
    row = pl.BlockSpec((t, GDN_W), lambda i: (rev(i), 0))
    qks = pl.BlockSpec((t, GDN_H * c), lambda i: (rev(i), 0))
    gls = pl.BlockSpec((cps, 1, LANE), lambda i: (rev(i), 0, 0))
    zs = pl.BlockSpec((t, GDN_W), lambda i: (rev(i), _div(lay.z, GDN_W)))
    nws = pl.BlockSpec((1, GDN_D), lambda i: (0, 0))
    return pl.pallas_call(
        body, name=name, grid=(steps,),
        in_specs=[row, row, row, row, qks, gls, pl.BlockSpec((cps, GDN_W, GDN_D), lambda i: (rev(i), 0, 0)), row, row, zs, nws],
        out_specs=[row, row, row, row, qks, gls, row, nws],
        out_shape=[SDS((s, GDN_W), F32)] * 4 + [SDS((s, GDN_H * c), F32), SDS((n, 1, LANE), F32), SDS((s, GDN_W), BF16),
                                                SDS((1, GDN_D), F32)],
        scratch_shapes=[pltpu.VMEM((GDN_W, GDN_D), F32)],
        compiler_params=_cp(("arbitrary",)))(u, w, qd, kd, qk, gl, states, o, dy, p_all, nw)


def _cols_to_full(g):
    n, k, c = g.shape
    return g.transpose(1, 0, 2).reshape(k, n * c)


def _rows_to_blocks(w):
    return w.reshape(N_DEV, w.shape[0] // N_DEV, w.shape[1])


def _pack_small(parts, rows):
    flat = jnp.concatenate([jnp.pad(p.reshape(-1), (0, -p.size % LANE)) for p in parts])
    return jnp.pad(flat, (0, rows * LANE - flat.size)).reshape(rows, LANE)


def kernel(x, mem, g_mix, w_in, sinks, conv_w, a_log, dt_bias, gdn_norm_w, g_mem, w_mem_kv, w_swa_up, w_gdn_up, w_xa_up, w_out, g_mlp, w_mlp_in, w_mlp_out, g_final, loss_target, m_g_mix, m_w_in, m_sinks, m_conv_w, m_a_log, m_dt_bias, m_gdn_norm_w, m_g_mem, m_w_mem_kv, m_w_swa_up, m_w_gdn_up, m_w_xa_up, m_w_out, m_g_mlp, m_w_mlp_in, m_w_mlp_out, m_g_final, v_g_mix, v_w_in, v_sinks, v_conv_w, v_a_log, v_dt_bias, v_gdn_norm_w, v_g_mem, v_w_mem_kv, v_w_swa_up, v_w_gdn_up, v_w_xa_up, v_w_out, v_g_mlp, v_w_mlp_in, v_w_mlp_out, v_g_final):
    xs, ms, tgt = x[0], mem[0], loss_target[0]
    s, d = xs.shape
    lay = Layout(d)
    px, py, pc = _position()
    dev = 4 * px + 2 * py + pc

    n1, g_in, g_conv = rmsnorm_fwd(xs, g_mix, "norm_mix", side=GatherJob([w_in[0].astype(BF16), conv_w[0]]))
    W_in = pad_w_in(g_in, lay)
    convw = _cols_to_full(g_conv)
    gp = jnp.zeros((8, LANE), F32).at[0, :GDN_H].set(a_log[0]).at[1, :GDN_H].set(dt_bias[0])
    later = [w_mem_kv[0], w_swa_up[0], w_gdn_up[0], w_xa_up[0], w_out[0], w_mlp_in[0]]

    p_all, g_mkv, W_sup, W_gup, W_xup, g_out, W_m1 = matmul(
        n1, W_in, mode="nn", out_dtype=F32, name="proj_in", tm=2048, tn=1024, tk=d,
        side=GatherJob([w.astype(BF16) for w in later]))
    W_mkv = g_mkv.reshape(-1, g_mkv.shape[2])
    W_out = g_out.reshape(-1, d)
    y_a = swa_fwd(p_all, sinks, lay, "swa_fwd")
    u, gw, gqd, gkd, gqk, ggl = gdn_pre_fwd(p_all, convw, gp, lay, "gdn_pre_fwd")
    o_b, states, y_b = gdn_scan_fwd(u, gw, gqd, gkd, gqk, ggl, p_all, gdn_norm_w, lay, "gdn_scan_fwd")
    nm = rmsnorm_fwd(ms, g_mem, "norm_mem")
    mkv = matmul(nm, W_mkv, mode="nn", out_dtype=BF16, name="proj_mem", tk=d)
    y_c = xattn_fwd(p_all, mkv, lay, "xattn_fwd")
    merged = merge(p_all, y_a, y_b, y_c, W_sup, W_gup, W_xup, None, lay, "merge_fwd")
    h1, n2 = matmul(merged, W_out, mode="nn", out_dtype=F32, name="proj_out", tm=512, tn=d, tk=d, resid=xs, rms_gain=g_mlp)
    uu, act, g_m2 = matmul(n2, W_m1, mode="nn", out_dtype=F32, name="mlp_in", tm=2048, tn=512, tk=d, b_cols=True,
                           relu2_out=True, side=GatherJob([w_mlp_out[0].astype(BF16)]))
    W_m2 = g_m2.reshape(-1, d)
    dh2, dh2_b, dg_final, lrow = matmul(act, W_m2, mode="nn", out_dtype=F32, name="mlp_out_loss", tm=512, tn=d, tk=1024,
                                        resid=h1, rms_gain=g_final.reshape(1, d), loss_target=tgt)
    loss = lax.psum(lrow[0, 0], ("x", "y", "c"))

    du = matmul(dh2_b, W_m2, mode="nt", out_dtype=BF16, name="mlp_out_dx", tm=2048, tn=512, tk=d, relu2_grad_of=uu)
    dW_m2 = matmul(act, dh2_b, mode="tn", out_dtype=BF16, name="mlp_out_dw", tm=1024, tn=2048, tk=1024)
    dW_m2 = _rows_to_blocks(dW_m2)
    dn2, sib_m2 = matmul(du, W_m1, mode="nt", out_dtype=F32, name="mlp_in_dx", tm=1024, tn=2048, tk=1024, b_cols=True,
                         side=PairExchangeJob([dW_m2], [False]))
    c_m2 = pair_add(dW_m2, False, sib_m2, "grads_pair_add_m2")
    dW_m1 = matmul(n2, du, mode="tn", out_dtype=BF16, name="mlp_in_dw", tm=2048, tn=1024, tk=1024)
    dh1, dg_mlp, dh1_b = rmsnorm_bwd(h1, g_mlp, dn2, dh2, "norm_mlp_bwd", bf16_copy=True)

    dmerged, sib_m1 = matmul(dh1_b, W_out, mode="nt", out_dtype=F32, name="proj_out_dx", tm=1024, tn=d, tk=d,
                             side=PairExchangeJob([dW_m1], [True]))
    c_m1 = pair_add(dW_m1, True, sib_m1, "grads_pair_add_m1")
    dW_out = matmul(merged, dh1_b, mode="tn", out_dtype=BF16, name="proj_out_dw", tm=2048, tn=2048, tk=1024)
    dgates, dta, dtb, dtc = merge(p_all, y_a, y_b, y_c, W_sup, W_gup, W_xup, dmerged, lay, "merge_bwd")
    dy_a = matmul(dta, W_sup, mode="nt", out_dtype=BF16, name="swa_up_dx", tm=2048, tk=d, b_cols=True)
    dy_b = matmul(dtb, W_gup, mode="nt", out_dtype=BF16, name="gdn_up_dx", tm=2048, tk=d, b_cols=True)
    dy_c = matmul(dtc, W_xup, mode="nt", out_dtype=BF16, name="xa_up_dx", tm=2048, tk=d, b_cols=True)
    dW_sup = matmul(y_a, dta, mode="tn", out_dtype=BF16, name="swa_up_dw", tn=2048, tk=2048)
    dW_gup = matmul(y_b, dtb, mode="tn", out_dtype=BF16, name="gdn_up_dw", tn=2048, tk=2048)
    dW_xup = matmul(y_c, dtc, mode="tn", out_dtype=BF16, name="xa_up_dw", tn=2048, tk=2048)

    dq_a, dk_a, dv_a, dsinks = swa_bwd(p_all, sinks, dy_a, lay, "swa_bwd")
    dq_c, dmkv = xattn_bwd(p_all, mkv, dy_c, lay, "xattn_bwd")
    dW_mkv = matmul(nm, dmkv, mode="tn", out_dtype=BF16, name="proj_mem_dw", tk=256)
    dnm = matmul(dmkv, W_mkv, mode="nt", out_dtype=F32, name="proj_mem_dx", tk=1024)
    _, dg_mem = rmsnorm_bwd(ms, g_mem, dnm, None, "norm_mem_bwd")

    du_g, dw_g, dqd_g, dkd_g, dqk_g, dgl_g, dz, dnorm_w = gdn_scan_bwd(
        u, gw, gqd, gkd, gqk, ggl, states, o_b, dy_b, p_all, gdn_norm_w, lay, "gdn_scan_bwd")
    dqkv, dab, dgp, dconv = gdn_pre_bwd(p_all, convw, gp, du_g, dw_g, dqd_g, dkd_g, dqk_g, dgl_g, lay, "gdn_pre_bwd")

    drest = jnp.concatenate([dq_a, dqkv, dz, dq_c, dk_a, dv_a, dab, jnp.zeros((s, lay.pw - lay.end), BF16)], axis=1)
    def pair_stage(grads, cols, tag):
        from_sib = run_job(PairExchangeJob(grads, cols), "grads_pair_exchange_" + tag)
        return [pair_add(g, cl, o, "grads_pair_add_%s%d" % (tag, i)) for i, (g, cl, o) in enumerate(zip(grads, cols, from_sib))]

    small = pair_stage([_rows_to_blocks(dW_mkv), dW_sup, dW_gup, dW_xup, _rows_to_blocks(dW_out)],
                       [False, True, True, True, False], "a")
    dW_in, p_m1, p_m2 = matmul(n1, dgates, tail=drest, mode="tn", out_dtype=BF16, name="proj_in_dw", tm=2048, tn=1024, tk=1024,
                               side=ChipExchangeJob([c_m1, c_m2]))
    late = pair_stage([unpad_dw_in(dW_in, lay)], [False], "b")
    dn1, p_in, p_mkv, p_sup, p_gup, p_xup, p_out = matmul(
        dgates, W_in, tail=drest, mode="nt", out_dtype=F32, name="proj_in_dx", tm=1024, tn=2048, tk=1024,
        side=ChipExchangeJob(late + small))
    grad_x, dg_mix = rmsnorm_bwd(xs, g_mix, dn1, dh1, "norm_mix_bwd")
    parts = [p_in, p_mkv, p_sup, p_gup, p_xup, p_out, p_m1, p_m2]

    shard_names = [(w_in, m_w_in, v_w_in), (w_mem_kv, m_w_mem_kv, v_w_mem_kv), (w_swa_up, m_w_swa_up, v_w_swa_up),
                   (w_gdn_up, m_w_gdn_up, v_w_gdn_up), (w_xa_up, m_w_xa_up, v_w_xa_up), (w_out, m_w_out, v_w_out),
                   (w_mlp_in, m_w_mlp_in, v_w_mlp_in), (w_mlp_out, m_w_mlp_out, v_w_mlp_out)]
    big_res = [adamw(p, w[0], m[0], v[0], "adamw_%d" % i) for i, (p, (w, m, v)) in enumerate(zip(parts, shard_names))]

    smalls = [(g_mix, m_g_mix, v_g_mix, dg_mix), (sinks, m_sinks, v_sinks, dsinks[:, :SWA_HQ]),
              (a_log, m_a_log, v_a_log, dgp[0:1, :GDN_H]), (dt_bias, m_dt_bias, v_dt_bias, dgp[1:2, :GDN_H]),
              (gdn_norm_w, m_gdn_norm_w, v_gdn_norm_w, dnorm_w), (g_mem, m_g_mem, v_g_mem, dg_mem),
              (g_mlp, m_g_mlp, v_g_mlp, dg_mlp), (g_final, m_g_final, v_g_final, dg_final)]
    sizes = [-(-t[0].size // LANE) * LANE for t in smalls] + [GDN_CONV * 3 * GDN_W]
    rows = -(-sum(sizes) // (8 * LANE)) * 8
    csh = conv_w.shape[2]

    def conv_place(a):
        full = jnp.tile(a[0], (1, N_DEV))
        owner = lax.broadcasted_iota(jnp.int32, full.shape, 1) // csh
        return jnp.where(owner == dev, full, 0.0)

    g_pack = _pack_small([t[3] for t in smalls] + [dconv], rows)
    w_pack = _pack_small([t[0] for t in smalls] + [conv_place(conv_w)], rows)
    m_pack = _pack_small([t[1] for t in smalls] + [conv_place(m_conv_w)], rows)
    v_pack = _pack_small([t[2] for t in smalls] + [conv_place(v_conv_w)], rows)
    g_all = run_job(GatherJob([g_pack]), "gather_small_grads")[0]
    small_res = adamw(g_all, w_pack, m_pack, v_pack, "adamw_small")

    def unpack(arr):
        flat = arr.reshape(-1)
        outs, off = [], 0
        for t, sz in zip(smalls, sizes[:-1]):
            outs.append(flat[off:off + t[0].size].reshape(t[0].shape))
            off += sz
        cw = flat[off:off + sizes[-1]].reshape(GDN_CONV, 3 * GDN_W)
        mine = (lax.broadcasted_iota(jnp.int32, (1, N_DEV, 1), 1) == dev).astype(F32)
        outs.append(jnp.sum(cw.reshape(GDN_CONV, N_DEV, csh) * mine, axis=1)[None])
        return outs

    sg, sd, sm, sv = (unpack(a) for a in small_res)
    bg, bd, bm, bv = ([r[i][None] for r in big_res] for i in range(4))

    def ordered(sm_, bg_):
        return [sm_[0], bg_[0], sm_[1], sm_[8], sm_[2], sm_[3], sm_[4], sm_[5], bg_[1], bg_[2], bg_[3], bg_[4], bg_[5],
                sm_[6], bg_[6], bg_[7], sm_[7]]

    return (loss, grad_x[None], *ordered(sg, bg), *ordered(sd, bd), *ordered(sm, bm), *ordered(sv, bv))
```

```python
import functools

import jax
import jax.numpy as jnp
from jax import lax
from jax.experimental import pallas as pl
from jax.experimental.pallas import tpu as pltpu

F32, BF16 = jnp.float32, jnp.bfloat16
SDS = jax.ShapeDtypeStruct
MESH = pl.DeviceIdType.MESH
ANY = pl.BlockSpec(memory_space=pl.ANY)

SWA_HQ, SWA_HKV, SWA_HD, SWA_W = 16, 2, 64, 128
SWA_G = SWA_HQ // SWA_HKV
GDN_H, GDN_D, GDN_CONV, GDN_C = 4, 128, 4, 64
XA_H, XA_D = 4, 128
Q_W = SWA_HQ * SWA_HD
KV_W = SWA_HKV * SWA_HD
GDN_W = GDN_H * GDN_D
XA_W = XA_H * XA_D
RMS_EPS = 1e-6
L2_EPS = 1e-6
NEG = -1e30
N_DEV = 8
LANE = 128

ADAM_LR, ADAM_B1, ADAM_B2, ADAM_EPS, ADAM_WD, ADAM_STEP = 0.001, 0.9, 0.999, 1e-08, 0.01, 10

VMEM_BIG = 56 * 1024 * 1024


def _cp(sem, vmem=VMEM_BIG):
    return pltpu.CompilerParams(dimension_semantics=sem, vmem_limit_bytes=vmem)


def _div(a, b):
    assert a % b == 0, (a, b)
    return a // b


def _tile(n, t):
    t = min(t, n)
    assert n % t == 0, (n, t)
    return t


def _sigmoid(x):
    return jax.nn.sigmoid(x)


def _silu(x):
    return x * _sigmoid(x)


def _softplus(x):
    return jnp.maximum(x, 0.0) + jnp.log1p(jnp.exp(-jnp.abs(x)))


def _dot(a, b, dims, prec=None):
    return lax.dot_general(a, b, (dims, ((), ())), precision=prec, preferred_element_type=F32)


NN = ((1,), (0,))
NT = ((1,), (1,))
TN = ((0,), (0,))
HI = lax.Precision.HIGHEST


def _bdot_plain(a, b, dims):
    return _dot(a.astype(BF16), b.astype(BF16), dims)


@functools.partial(jax.custom_vjp, nondiff_argnums=(2,))
def _bdot_vjp(a, b, dims):
    return _bdot_plain(a, b, dims)


def _bdot_vjp_fwd(a, b, dims):
    return _bdot_plain(a, b, dims), (a, b)


def _bdot_vjp_bwd(dims, res, ct):
    a, b = res
    if dims == NN:
        return _bdot_plain(ct, b, NT), _bdot_plain(a, ct, TN)
    assert dims == NT, dims
    return _bdot_plain(ct, b, NN), _bdot_plain(ct, a, TN)


_bdot_vjp.defvjp(_bdot_vjp_fwd, _bdot_vjp_bwd)


def _neumann(xs):
    pws, nns = list(xs), list(xs)
    for _ in range(5):
        pws = [_bdot_plain(p, p, NN) for p in pws]
        nns = [n + p + _bdot_plain(n, p, NN) for n, p in zip(nns, pws)]
    return tuple(nns)


@jax.custom_vjp
def _neumann_vjp(xs):
    return _neumann(xs)


def _neumann_vjp_fwd(xs):
    nns = _neumann(xs)
    return nns, nns


def _neumann_vjp_bwd(nns, cts):
    ts = [ct + _bdot_plain(nn, ct, TN) for nn, ct in zip(nns, cts)]
    return (tuple(t + _bdot_plain(t, nn, NT) for t, nn in zip(ts, nns)),)


_neumann_vjp.defvjp(_neumann_vjp_fwd, _neumann_vjp_bwd)


class Layout:
    def __init__(self, d):
        self.d = d
        self.g = 0
        self.q = 3 * d
        self.qkv = self.q + Q_W
        self.z = self.qkv + 3 * GDN_W
        self.qc = self.z + GDN_W
        self.k = self.qc + XA_W
        self.v = self.k + KV_W
        self.ab = self.v + KV_W
        self.end = self.ab + LANE
        self.pw = -(-self.end // 1024) * 1024
        self.lq, self.lk, self.lv, self.lqkv = 0, Q_W, Q_W + KV_W, Q_W + 2 * KV_W
        self.la = self.lqkv + 3 * GDN_W
        self.lz = self.la + 2 * GDN_H
        self.lqc = self.lz + GDN_W
        self.lg = self.lqc + XA_W
        self.lw = self.lg + 3 * d

    def pieces(self):
        segs = [(self.lq, self.lk, self.q), (self.lk, self.lv, self.k), (self.lv, self.lqkv, self.v),
                (self.lqkv, self.la, self.qkv), (self.la, self.lz, self.ab), (self.lz, self.lqc, self.z),
                (self.lqc, self.lg, self.qc), (self.lg, self.lw, self.g)]
        cw = _div(self.lw, N_DEV)
        out = []
        for dev in range(N_DEV):
            lo, hi = dev * cw, (dev + 1) * cw
            for ls, le, ps in segs:
                s, e = max(lo, ls), min(hi, le)
                if s < e:
                    out.append((dev, s - lo, ps + s - ls, e - s))
        return out


def pad_w_in(g, lay):
    nd, k, cw = g.shape
    tr = _tile(k, 256)
    tail = lay.ab + 2 * GDN_H

    def body(g_ref, o_ref):
        o_ref[:, pl.ds(tail, lay.pw - tail)] = jnp.zeros((tr, lay.pw - tail), o_ref.dtype)
        for dev, so, po, ln in lay.pieces():
            o_ref[:, pl.ds(po, ln)] = g_ref[dev, :, pl.ds(so, ln)]

    return pl.pallas_call(
        body, name="pad_w_in", grid=(k // tr,), in_specs=[pl.BlockSpec((nd, tr, cw), lambda i: (0, i, 0))],
        out_specs=pl.BlockSpec((tr, lay.pw), lambda i: (i, 0)), out_shape=SDS((k, lay.pw), g.dtype),
        compiler_params=_cp(("parallel",)))(g)


def unpad_dw_in(dw, lay):
    k = dw.shape[0]
    cw = _div(lay.lw, N_DEV)
    tr = _tile(k, 256)

    def body(d_ref, o_ref):
        for dev, so, po, ln in lay.pieces():
            o_ref[dev, :, pl.ds(so, ln)] = d_ref[:, pl.ds(po, ln)]

    return pl.pallas_call(
        body, name="unpad_dw_in", grid=(k // tr,), in_specs=[pl.BlockSpec((tr, lay.pw), lambda i: (i, 0))],
        out_specs=pl.BlockSpec((N_DEV, tr, cw), lambda i: (0, i, 0)), out_shape=SDS((N_DEV, k, cw), dw.dtype),
        compiler_params=_cp(("parallel",)))(dw)


def _position():
    return lax.axis_index("x"), lax.axis_index("y"), lax.axis_index("c")


class GatherJob:
    def __init__(self, arrs):
        self.ins = list(arrs)
        n = len(arrs)
        self.out_shapes = [SDS((N_DEV,) + a.shape, a.dtype) for a in arrs]
        self.scratch = [pltpu.SemaphoreType.DMA((n, 7)), pltpu.SemaphoreType.DMA((n, 7)), pltpu.SemaphoreType.DMA((n,))]

    def _ctx(self, outs, sems):
        send_sems, recv_sems, _ = sems
        x, y, c = _position()

        def blk(o, p):
            return o.at[4 * p[0] + 2 * p[1] + p[2]]

        def copy(i, k, block, to, src=None):
            return pltpu.make_async_remote_copy(
                src_ref=blk(outs[i], block) if src is None else src, dst_ref=blk(outs[i], block),
                send_sem=send_sems.at[i, k], recv_sem=recv_sems.at[i, k], device_id=to, device_id_type=MESH)

        return (x, y, c), (x, y, 1 - c), [(1 - x, y), (x, 1 - y), (1 - x, 1 - y)], blk, copy

    def start(self, ins, outs, sems):
        me, sibling, chips, blk, copy = self._ctx(outs, sems)
        for i in range(len(ins)):
            pltpu.make_async_copy(ins[i], blk(outs[i], me), sems[2].at[i]).start()
            copy(i, 0, me, sibling, src=ins[i]).start()
            for j, chip in enumerate(chips[:2]):
                copy(i, 1 + j, me, (*chip, me[2]), src=ins[i]).start()

    def _relay(self, i, outs, sems, onward):
        me, _, _, blk, copy = self._ctx(outs, sems)
        x, y, c = me
        origin = (x + (1 - c) - 2 * x * (1 - c), y + c - 2 * y * c, c)
        if not onward:
            return copy(i, 1 + c, origin, me)
        return copy(i, 3, origin, (x + c - 2 * x * c, y + (1 - c) - 2 * y * (1 - c), c))

    def relay(self, ins, outs, sems):
        for i in range(len(ins)):
            self._relay(i, outs, sems, False).wait_recv()
            self._relay(i, outs, sems, True).start()

    def mid(self, ins, outs, sems):
        me, sibling, chips, blk, copy = self._ctx(outs, sems)
        for i in range(len(ins)):
            for j, chip in enumerate(chips):
                arrival = copy(i, 1 + j, (*chip, me[2]), me)
                if j < 2:
                    pl.when(me[2] != j)(arrival.wait_recv)
                else:
                    arrival.wait_recv()
                copy(i, 4 + j, (*chip, me[2]), sibling).start()

    def finish(self, ins, outs, sems):
        me, sibling, chips, blk, copy = self._ctx(outs, sems)
        for i in range(len(ins)):
            copy(i, 0, sibling, me).wait_recv()
            for j, chip in enumerate(chips):
                copy(i, 4 + j, (*chip, 1 - me[2]), me).wait_recv()
        for i in range(len(ins)):
            pltpu.make_async_copy(ins[i], blk(outs[i], me), sems[2].at[i]).wait()
            copy(i, 0, me, sibling, src=ins[i]).wait_send()
            for j, chip in enumerate(chips[:2]):
                copy(i, 1 + j, me, (*chip, me[2]), src=ins[i]).wait_send()
            self._relay(i, outs, sems, True).wait_send()
            for j, chip in enumerate(chips):
                copy(i, 4 + j, (*chip, me[2]), sibling).wait_send()


class ChipExchangeJob:
    mid = None

    def __init__(self, arrs):
        self.ins = list(arrs)
        n = len(arrs)
        self.out_shapes = [SDS(a.shape, a.dtype) for a in arrs]
        self.scratch = [pltpu.SemaphoreType.DMA((n, 3)), pltpu.SemaphoreType.DMA((n, 3)), pltpu.SemaphoreType.DMA((n,))]

    def _copies(self, ins, outs, sems, i, arrivals):
        send_sems, recv_sems, local_sems = sems
        x, y, c = _position()
        my_chip = 2 * x + y
        chips = [(1 - x, y), (x, 1 - y), (1 - x, 1 - y)]
        if arrivals:
            return [pltpu.make_async_remote_copy(
                src_ref=ins[i].at[my_chip], dst_ref=outs[i].at[2 * px + py], send_sem=send_sems.at[i, k],
                recv_sem=recv_sems.at[i, k], device_id=(px, py, c), device_id_type=MESH) for k, (px, py) in enumerate(chips)]
        local = pltpu.make_async_copy(ins[i].at[my_chip], outs[i].at[my_chip], local_sems.at[i])
        return local, [pltpu.make_async_remote_copy(
            src_ref=ins[i].at[2 * px + py], dst_ref=outs[i].at[my_chip], send_sem=send_sems.at[i, k],
            recv_sem=recv_sems.at[i, k], device_id=(px, py, c), device_id_type=MESH) for k, (px, py) in enumerate(chips)]

    def start(self, ins, outs, sems):
        for i in range(len(ins)):
            local, remote = self._copies(ins, outs, sems, i, False)
            local.start()
            for cp in remote:
                cp.start()

    def finish(self, ins, outs, sems):
        for i in range(len(ins)):
            for cp in self._copies(ins, outs, sems, i, True):
                cp.wait_recv()
            local, remote = self._copies(ins, outs, sems, i, False)
            for cp in remote:
                cp.wait_send()
            local.wait()


def _slab_shape(g, cols):
    return (g.shape[0], _div(g.shape[1], N_DEV)) if cols else g.shape[1:]


class PairExchangeJob:
    mid = None

    def __init__(self, grads, cols):
        self.ins, self.cols = list(grads), list(cols)
        n = len(grads)
        self.out_shapes = [SDS((4,) + _slab_shape(g, cl), g.dtype) for g, cl in zip(grads, cols)]
        self.scratch = [pltpu.SemaphoreType.DMA((n, 4)), pltpu.SemaphoreType.DMA((n, 4))]

    def _copies(self, ins, outs, sems):
        send_sems, recv_sems = sems
        x, y, c = _position()

        def part(i, dst):
            if not self.cols[i]:
                return ins[i].at[dst]
            cw = _slab_shape(self.ins[i], True)[1]
            return ins[i].at[:, pl.ds(pl.multiple_of(dst * cw, LANE), cw)]

        return [pltpu.make_async_remote_copy(src_ref=part(i, 2 * j + 1 - c), dst_ref=outs[i].at[j], send_sem=send_sems.at[i, j],
                                             recv_sem=recv_sems.at[i, j], device_id=(x, y, 1 - c), device_id_type=MESH)
                for i in range(len(ins)) for j in range(4)]

    def start(self, ins, outs, sems):
        for cp in self._copies(ins, outs, sems):
            cp.start()

    def finish(self, ins, outs, sems):
        for cp in self._copies(ins, outs, sems):
            cp.wait()


def _host_begin(job, step, steps, ins, outs, sems):
    pl.when(step == 0)(lambda: job.start(ins, outs, sems))
    if job.mid is not None:
        pl.when(step == (steps * 45) // 100)(lambda: job.relay(ins, outs, sems))
        pl.when(step == (steps * 85) // 100)(lambda: job.mid(ins, outs, sems))


def run_job(job, name):
    n = len(job.ins)

    def body(*refs):
        ins, outs, sems = refs[:n], refs[n:2 * n], refs[2 * n:]
        job.start(ins, outs, sems)
        if job.mid is not None:
            job.relay(ins, outs, sems)
            job.mid(ins, outs, sems)
        job.finish(ins, outs, sems)

    return pl.pallas_call(body, name=name, out_shape=job.out_shapes, in_specs=[ANY] * n, out_specs=[ANY] * n,
                          scratch_shapes=job.scratch)(*job.ins)


def pair_add(grad, cols, other, name):
    r, c = _slab_shape(grad, cols)
    tr = _tile(r, 256)
    parity = lax.axis_index("c").astype(jnp.int32).reshape(1)

    def body(par_ref, a_ref, b_ref, o_ref):
        o_ref[...] = (a_ref[...].astype(F32) + b_ref[...].astype(F32)).astype(BF16)

    spec = pl.BlockSpec((None, tr, c), lambda j, i, par: (j, i, 0))
    if cols:
        own = pl.BlockSpec((tr, c), lambda j, i, par: (i, 2 * j + par[0]))
    else:
        own = pl.BlockSpec((None, tr, c), lambda j, i, par: (2 * j + par[0], i, 0))
    return pl.pallas_call(
        body, name=name, out_shape=SDS(other.shape, BF16),
        grid_spec=pltpu.PrefetchScalarGridSpec(num_scalar_prefetch=1, grid=(4, r // tr), in_specs=[own, spec], out_specs=spec),
        compiler_params=_cp(("parallel", "parallel")))(parity, grad, other)


def adamw(parts, w, m, v, name):
    p, r, c = parts.shape
    tr = _tile(r, 128 if c > 1024 else 256)

    def body(p_ref, w_ref, m_ref, v_ref, g_out, d_out, m_out, v_out):
        g = p_ref[0].astype(F32)
        for j in range(1, p):
            g = g + p_ref[j].astype(F32)
        mn = ADAM_B1 * m_ref[...] + (1.0 - ADAM_B1) * g
        vn = ADAM_B2 * v_ref[...] + (1.0 - ADAM_B2) * jnp.square(g)
        m_hat = mn / (1.0 - ADAM_B1 ** ADAM_STEP)
        v_hat = vn / (1.0 - ADAM_B2 ** ADAM_STEP)
        g_out[...] = g
        d_out[...] = -ADAM_LR * (m_hat / (jnp.sqrt(v_hat) + ADAM_EPS) + ADAM_WD * w_ref[...])
        m_out[...] = mn
        v_out[...] = vn

    spec = pl.BlockSpec((tr, c), lambda i: (i, 0))
    return pl.pallas_call(
        body, name=name, grid=(r // tr,),
        in_specs=[pl.BlockSpec((p, tr, c), lambda i: (0, i, 0)), spec, spec, spec],
        out_specs=[spec] * 4, out_shape=[SDS((r, c), F32)] * 4, compiler_params=_cp(("parallel",)))(parts, w, m, v)


def matmul(a, b, *, mode, out_dtype, name, tm=1024, tn=1024, tk=512, a_relu2=False, resid=None, relu2_grad_of=None,
           b_cols=False, relu2_out=False, rms_gain=None, loss_target=None, side=None, tail=None):
    if b_cols:
        nb, brows, bc = b.shape
        bshape = (brows, nb * bc)
    else:
        bshape = b.shape
    head_k = head_n = None
    if mode == "nn":
        (m, k), (k2, n) = a.shape, bshape
    elif mode == "nt":
        (m, k), (n, k2) = a.shape, bshape
        if tail is not None:
            head_k, k = k, k + tail.shape[1]
    else:
        (k, m), (k2, n) = a.shape, bshape
        if tail is not None:
            head_n, n = n, n + tail.shape[1]
    assert k == k2 and (tail is None or mode != "nn"), (a.shape, b.shape, mode)
    b_whole = b_cols and mode == "nt" and tk >= k
    tm, tn, tk = _tile(m, tm), _tile(n, tn), _tile(k, tk)
    if b_cols and mode == "nn":
        tn = _tile(bc, tn)
    if b_cols and mode == "nt" and not b_whole:
        tk = _tile(bc, tk)
    nk = k // tk
    ni, nj = m // tm, n // tn
    nk_head = _div(head_k, tk) if head_k is not None else None
    nj_head = _div(head_n, tn) if head_n is not None else None
    use_acc = nk > 1 or tail is not None
    dims = {"nn": NN, "nt": NT, "tn": TN}[mode]
    extras = [e for e in (resid, relu2_grad_of) if e is not None]
    tails = [tail] if tail is not None else []
    n_side = len(side.ins) if side is not None else 0
    loss = loss_target is not None
    if loss:
        extras.append(loss_target)
    gains = [rms_gain] if rms_gain is not None else []
    assert not gains or (tn == n and not relu2_out), (tn, n)
    assert not loss or (gains and resid is not None and relu2_grad_of is None and out_dtype == F32)
    n_main = 2 if (relu2_out or gains) else 1
    n_loss = 2 if loss else 0

    def body(*refs):
        a_ref, b_ref = refs[:2]
        t_ref = refs[2] if tails else None
        n_op = 2 + len(tails)
        e_refs = refs[n_op:n_op + len(extras)]
        n_pre = n_op + len(extras) + len(gains)
        g_ref = refs[n_pre - 1] if gains else None
        n_in = n_pre + n_side
        o_ref = refs[n_in]
        act_ref = refs[n_in + 1] if n_main == 2 else None
        dg_ref, l_ref = refs[n_in + n_main:n_in + n_main + n_loss] if loss else (None, None)
        acc_ref = refs[n_in + n_main + n_loss + n_side] if use_acc else None
        if side is not None:
            s_ins = refs[n_pre:n_in]
            s_outs = refs[n_in + n_main + n_loss:n_in + n_main + n_loss + n_side]
            s_sems = refs[len(refs) - len(side.scratch):]
            step = (pl.program_id(0) * nj + pl.program_id(1)) * nk + pl.program_id(2)
            _host_begin(side, step, ni * nj * nk, s_ins, s_outs, s_sems)

        def operands(a_from=a_ref, b_from=b_ref):
            av = a_from[...]
            if a_relu2:
                av = jnp.square(jnp.maximum(av.astype(F32), 0.0))
            return av.astype(BF16), b_from[...].astype(BF16)

        def finish(r):
            e = list(e_refs)
            if resid is not None:
                r = r + e.pop(0)[...]
            if relu2_grad_of is not None:
                r = r * (2.0 * jnp.maximum(e.pop(0)[...], 0.0))
            if loss:
                gv = g_ref[...]
                inv = lax.rsqrt(jnp.mean(r * r, axis=-1, keepdims=True) + RMS_EPS)
                err = r * inv * gv - e.pop(0)[...]
                lpart = 0.5 * jnp.sum(jnp.mean(err * err, axis=-1, keepdims=True), axis=0, keepdims=True)
                dx, part = _rms_bwd_rows(r, gv, err * (1.0 / n))
                o_ref[...] = dx
                act_ref[...] = dx.astype(BF16)

                @pl.when(pl.program_id(0) == 0)
                def _():
                    dg_ref[...] = jnp.zeros_like(dg_ref)
                    l_ref[...] = jnp.zeros_like(l_ref)

                dg_ref[...] += part
                l_ref[...] += jnp.broadcast_to(lpart, l_ref.shape)
                return
            o_ref[...] = r.astype(out_dtype)
            if relu2_out:
                act_ref[...] = jnp.square(jnp.maximum(r, 0.0)).astype(BF16)
            if gains:
                inv = lax.rsqrt(jnp.mean(r * r, axis=-1, keepdims=True) + RMS_EPS)
                act_ref[...] = (r * inv * g_ref[...]).astype(BF16)

        if b_whole:
            av = a_ref[...].astype(BF16)
            finish(sum(_dot(av[:, kb * bc:(kb + 1) * bc], b_ref[kb].astype(BF16), NT) for kb in range(nb)))
        elif not use_acc:
            av, bv = operands()
            finish(_dot(av, bv, dims))
        else:
            kk = pl.program_id(2)

            def accumulate(a_from, b_from):
                def product():
                    av, bv = operands(a_from, b_from)
                    return _dot(av, bv, dims)

                if nk == 1:
                    finish(product())
                    return

                @pl.when(kk == 0)
                def _():
                    acc_ref[...] = product()

                @pl.when((kk > 0) & (kk < nk - 1))
                def _():
                    acc_ref[...] += product()

                @pl.when(kk == nk - 1)
                def _():
                    finish(acc_ref[...] + product())

            if not tails:
                accumulate(a_ref, b_ref)
            elif mode == "nt":
                pl.when(kk < nk_head)(lambda: accumulate(a_ref, b_ref))
                pl.when(kk >= nk_head)(lambda: accumulate(t_ref, b_ref))
            else:
                in_head = pl.program_id(1) < nj_head
                pl.when(in_head)(lambda: accumulate(a_ref, b_ref))
                pl.when(jnp.logical_not(in_head))(lambda: accumulate(a_ref, t_ref))

        if side is not None:
            pl.when(step == ni * nj * nk - 1)(lambda: side.finish(s_ins, s_outs, s_sems))

    a_spec = {"nn": pl.BlockSpec((tm, tk), lambda i, j, kk: (i, kk)),
              "nt": pl.BlockSpec((tm, tk), lambda i, j, kk: (i, kk)),
              "tn": pl.BlockSpec((tk, tm), lambda i, j, kk: (kk, i))}[mode]
    t_specs = []
    if tails and mode == "nt":
        a_spec = pl.BlockSpec((tm, tk), lambda i, j, kk: (i, jnp.minimum(kk, nk_head - 1)))
        t_specs = [pl.BlockSpec((tm, tk), lambda i, j, kk: (i, jnp.maximum(kk - nk_head, 0)))]
    if tails and mode == "tn":
        t_specs = [pl.BlockSpec((tk, tn), lambda i, j, kk: (kk, jnp.maximum(j - nj_head, 0)))]
    if tails and mode == "tn":
        b_spec = pl.BlockSpec((tk, tn), lambda i, j, kk: (kk, jnp.minimum(j, nj_head - 1)))
    elif not b_cols:
        b_spec = {"nn": pl.BlockSpec((tk, tn), lambda i, j, kk: (kk, j)),
                  "nt": pl.BlockSpec((tn, tk), lambda i, j, kk: (j, kk)),
                  "tn": pl.BlockSpec((tk, tn), lambda i, j, kk: (kk, j))}[mode]
    elif mode == "nn":
        per = bc // tn
        b_spec = pl.BlockSpec((None, tk, tn), lambda i, j, kk: (j // per, kk, j % per))
    elif b_whole:
        b_spec = pl.BlockSpec((nb, tn, bc), lambda i, j, kk: (0, j, 0))
    else:
        assert mode == "nt", mode
        per = bc // tk
        b_spec = pl.BlockSpec((None, tn, tk), lambda i, j, kk: (kk // per, j, kk % per))
    e_spec = pl.BlockSpec((tm, tn), lambda i, j, kk: (i, j))
    main_shapes = [SDS((m, n), out_dtype)] + ([SDS((m, n), BF16)] if n_main == 2 else [])
    g_specs = [pl.BlockSpec((1, tn), lambda i, j, kk: (0, j))] * len(gains)
    l_specs = [pl.BlockSpec((1, tn), lambda i, j, kk: (0, j)), pl.BlockSpec((1, LANE), lambda i, j, kk: (0, 0))] if loss else []
    l_shapes = [SDS((1, n), F32), SDS((1, LANE), F32)] if loss else []
    res = pl.pallas_call(
        body, name=name, grid=(ni, nj, nk),
        in_specs=[a_spec, b_spec] + t_specs + [e_spec] * len(extras) + g_specs + [ANY] * n_side,
        out_specs=[e_spec] * n_main + l_specs + [ANY] * n_side,
        out_shape=main_shapes + l_shapes + (side.out_shapes if side is not None else []),
        scratch_shapes=([pltpu.VMEM((tm, tn), F32)] if use_acc else []) + (side.scratch if side is not None else []),
        compiler_params=_cp(("arbitrary", "arbitrary", "arbitrary")))(a, b, *tails, *extras, *gains, *(side.ins if side is not None else []))
    return res if len(res) > 1 else res[0]


def rmsnorm_fwd(x, g, name, side=None):
    s, d = x.shape
    tm = _tile(s, 512)
    steps = s // tm
    n_side = len(side.ins) if side is not None else 0

    def body(*refs):
        x_ref, g_ref, o_ref = refs[0], refs[1], refs[2 + n_side]
        if side is not None:
            s_ins, s_outs, s_sems = refs[2:2 + n_side], refs[3 + n_side:3 + 2 * n_side], refs[3 + 2 * n_side:]
            _host_begin(side, pl.program_id(0), steps, s_ins, s_outs, s_sems)
        xv = x_ref[...]
        r = lax.rsqrt(jnp.mean(xv * xv, axis=-1, keepdims=True) + RMS_EPS)
        o_ref[...] = (xv * r * g_ref[...]).astype(BF16)
        if side is not None:
            pl.when(pl.program_id(0) == steps - 1)(lambda: side.finish(s_ins, s_outs, s_sems))

    row = pl.BlockSpec((tm, d), lambda i: (i, 0))
    res = pl.pallas_call(
        body, name=name, grid=(steps,), in_specs=[row, pl.BlockSpec((1, d), lambda i: (0, 0))] + [ANY] * n_side,
        out_specs=[row] + [ANY] * n_side, out_shape=[SDS((s, d), BF16)] + (side.out_shapes if side is not None else []),
        scratch_shapes=side.scratch if side is not None else [],
        compiler_params=_cp(("arbitrary",)))(x, g, *(side.ins if side is not None else []))
    return res if side is not None else res[0]


def _rms_bwd_rows(xv, gv, dy):
    r = lax.rsqrt(jnp.mean(xv * xv, axis=-1, keepdims=True) + RMS_EPS)
    xh = xv * r
    dxh = dy * gv
    dx = r * (dxh - xh * jnp.mean(dxh * xh, axis=-1, keepdims=True))
    return dx, jnp.sum(dy * xh, axis=0, keepdims=True)


def rmsnorm_bwd(x, g, dn, resid, name, bf16_copy=False):
    s, d = x.shape
    tm = _tile(s, 512)
    has_r = resid is not None

    def body(*refs):
        x_ref, g_ref, dn_ref = refs[:3]
        dx_ref, dg_ref = refs[3 + has_r:5 + has_r]
        dx, part = _rms_bwd_rows(x_ref[...], g_ref[...], dn_ref[...].astype(F32))
        if has_r:
            dx = dx + refs[3][...]
        dx_ref[...] = dx
        if bf16_copy:
            refs[5 + has_r][...] = dx.astype(BF16)

        @pl.when(pl.program_id(0) == 0)
        def _():
            dg_ref[...] = jnp.zeros_like(dg_ref)

        dg_ref[...] += part

    row = pl.BlockSpec((tm, d), lambda i: (i, 0))
    vec = pl.BlockSpec((1, d), lambda i: (0, 0))
    ins = [x, g, dn] + ([resid] if has_r else [])
    return pl.pallas_call(body, name=name, grid=(s // tm,), in_specs=[row, vec, row] + ([row] if has_r else []),
                          out_specs=[row, vec] + ([row] if bf16_copy else []),
                          out_shape=[SDS((s, d), F32), SDS((1, d), F32)] + ([SDS((s, d), BF16)] if bf16_copy else []),
                          compiler_params=_cp(("arbitrary",)))(*ins)


def merge(p_all, ya, yb, yc, wa, wb, wc, dm, lay, name):
    s, d = ya.shape[0], lay.d
    wcols = wa.shape[2]
    bwd = dm is not None
    tm, tn = _tile(s, 2048), _tile(wcols, 512)
    nj, per = d // tn, wcols // tn

    y_specs = [pl.BlockSpec((tm, y.shape[1]), lambda i, j, *_: (i, 0)) for y in (ya, yb, yc)]
    w_specs = [pl.BlockSpec((None, w.shape[1], tn), lambda i, j, *_: (j // per, 0, j % per)) for w in (wa, wb, wc)]
    o_spec = pl.BlockSpec((tm, tn), lambda i, j, *_: (i, j))
    if not bwd:
        def body(ga, gb, gc, ya_r, yb_r, yc_r, wa_r, wb_r, wc_r, o_ref):
            ts = [_dot(y[...], w[...], NN) for y, w in ((ya_r, wa_r), (yb_r, wb_r), (yc_r, wc_r))]
            gs = [_sigmoid(g[...]) for g in (ga, gb, gc)]
            o_ref[...] = (gs[0] * ts[0] + gs[1] * ts[1] + gs[2] * ts[2]).astype(BF16)

        gate_specs = [pl.BlockSpec((tm, tn), lambda i, j, b=b: (i, b * nj + j)) for b in range(3)]
        return pl.pallas_call(
            body, name=name, grid=(s // tm, nj), in_specs=gate_specs + y_specs + w_specs,
            out_specs=o_spec, out_shape=SDS((s, d), BF16),
            compiler_params=_cp(("parallel", "parallel")))(p_all, p_all, p_all, ya, yb, yc, wa, wb, wc)

    def body_bwd(g_r, ya_r, yb_r, yc_r, wa_r, wb_r, wc_r, dm_r, dg_o, dta_o, dtb_o, dtc_o):
        for k, (y, w, dt_o) in enumerate(((ya_r, wa_r, dta_o), (yb_r, wb_r, dtb_o), (yc_r, wc_r, dtc_o))):
            @pl.when(pl.program_id(2) == k)
            def _(y=y, w=w, dt_o=dt_o):
                t = _dot(y[...], w[...], NN)
                g = _sigmoid(g_r[...])
                dmv = dm_r[...]
                dg_o[...] = (dmv * t * (g * (1.0 - g))).astype(BF16)
                dt_o[...] = (dmv * g).astype(BF16)

    gate_spec = pl.BlockSpec((tm, tn), lambda i, j, b: (i, b * nj + j))
    return pl.pallas_call(
        body_bwd, name=name, grid=(s // tm, nj, 3), in_specs=[gate_spec] + y_specs + w_specs + [o_spec],
        out_specs=[gate_spec, o_spec, o_spec, o_spec], out_shape=[SDS((s, 3 * d), BF16)] + [SDS((s, d), BF16)] * 3,
        compiler_params=_cp(("arbitrary", "arbitrary", "arbitrary")))(p_all, ya, yb, yc, wa, wb, wc, dm)


SWA_PAIRS = SWA_G // 2


def _swa_probs(qs, kcs, sinks, firsts):
    shape = (qs[0].shape[0], 2 * SWA_W)
    qi = lax.broadcasted_iota(jnp.int32, shape, 0) % SWA_W
    kj = lax.broadcasted_iota(jnp.int32, shape, 1)
    band = (kj > qi) & (kj <= qi + SWA_W)
    masks = [band if f is False else band & ((kj >= SWA_W) | jnp.logical_not(f)) for f in firsts]
    ss = [jnp.where(mask, _dot(q, kc, NT) * (SWA_HD ** -0.5), NEG) for q, kc, mask in zip(qs, kcs, masks)]
    ms = [jnp.maximum(jnp.max(s, axis=-1, keepdims=True), sink) for s, sink in zip(ss, sinks)]
    ps = [jnp.exp(s - m) for s, m in zip(ss, ms)]
    es = [jnp.exp(sink - m) for sink, m in zip(sinks, ms)]
    inv = [1.0 / (jnp.sum(p, axis=-1, keepdims=True) + e) for p, e in zip(ps, es)]
    return [p * i for p, i in zip(ps, inv)], [e * i for e, i in zip(es, inv)]


def _swa_stack(ref, h, r0=0):
    return jnp.concatenate([ref[pl.ds(r0, SWA_W), pl.ds((h * SWA_PAIRS + p) * LANE, LANE)] for p in range(SWA_PAIRS)], axis=0)


def _swa_unstack(ref, h, val, r0=0):
    for p in range(SWA_PAIRS):
        ref[pl.ds(r0, SWA_W), pl.ds((h * SWA_PAIRS + p) * LANE, LANE)] = val[p * SWA_W:(p + 1) * SWA_W]


def _swa_sink_col(sk_ref, h, second):
    pair = lax.broadcasted_iota(jnp.int32, (SWA_PAIRS * SWA_W, 1), 0) // SWA_W
    col = jnp.zeros((SWA_PAIRS * SWA_W, 1), F32)
    for p in range(SWA_PAIRS):
        hh = h * SWA_G + 2 * p + second
        col = jnp.where(pair == p, sk_ref[0:1, hh:hh + 1], col)
    return col


def _swa_kv_tiles(t, h):
    lane = lax.broadcasted_iota(jnp.int32, t.shape, 1)
    moved = pltpu.roll(t, SWA_HD, axis=1)
    low, high = (t, moved) if h == 0 else (moved, t)
    return jnp.where(lane < SWA_HD, low, 0.0).astype(BF16), jnp.where(lane >= SWA_HD, high, 0.0).astype(BF16)


def _swa_kv_grad(g_low, g_high, h):
    lane = lax.broadcasted_iota(jnp.int32, g_low.shape, 1)
    if h == 0:
        return jnp.where(lane < SWA_HD, g_low + pltpu.roll(g_high, SWA_HD, axis=1), 0.0)
    return jnp.where(lane >= SWA_HD, pltpu.roll(g_low, SWA_HD, axis=1) + g_high, 0.0)


def swa_fwd(p_all, sinks, lay, name):
    s = p_all.shape[0]
    w = SWA_W
    nb2 = _div(s, 2 * w)

    def body(q_ref, kc_ref, kp_ref, vc_ref, vp_ref, sk_ref, o_ref):
        first = pl.program_id(0) == 0
        kt = [jnp.concatenate([kp_ref[...], kc_ref[0:w, :]], axis=0), kc_ref[...]]
        vt = [jnp.concatenate([vp_ref[...], vc_ref[0:w, :]], axis=0), vc_ref[...]]
        units = [(b, h, e) for b in range(2) for h in range(SWA_HKV) for e in range(2)]
        ks = [[_swa_kv_tiles(kt[b], h) for h in range(SWA_HKV)] for b in range(2)]
        vs = [[_swa_kv_tiles(vt[b], h) for h in range(SWA_HKV)] for b in range(2)]
        qs = [[_swa_stack(q_ref, h, b * w).astype(BF16) for h in range(SWA_HKV)] for b in range(2)]
        ps, _ = _swa_probs([qs[b][h] for b, h, e in units], [ks[b][h][e] for b, h, e in units],
                           [_swa_sink_col(sk_ref, h, e) for b, h, e in units], [first if b == 0 else False for b, h, e in units])
        os = [_dot(p.astype(BF16), vs[b][h][e], NN) for p, (b, h, e) in zip(ps, units)]
        for u in range(0, len(units), 2):
            b, h, _ = units[u]
            _swa_unstack(o_ref, h, (os[u] + os[u + 1]).astype(BF16), b * w)

    q_spec = pl.BlockSpec((2 * w, Q_W), lambda n: (n, _div(lay.q, Q_W)))
    cur = lambda off: pl.BlockSpec((2 * w, KV_W), lambda n: (n, _div(off, KV_W)))
    prev = lambda off: pl.BlockSpec((w, KV_W), lambda n: (jnp.maximum(2 * n - 1, 0), _div(off, KV_W)))
    return pl.pallas_call(
        body, name=name, grid=(nb2,),
        in_specs=[q_spec, cur(lay.k), prev(lay.k), cur(lay.v), prev(lay.v), pl.BlockSpec(sinks.shape, lambda n: (0, 0))],
        out_specs=pl.BlockSpec((2 * w, Q_W), lambda n: (n, 0)), out_shape=SDS((s, Q_W), BF16),
        compiler_params=_cp(("parallel",)))(p_all, p_all, p_all, p_all, p_all, sinks)


def swa_bwd(p_all, sinks, dy, lay, name):
    s = p_all.shape[0]
    w = SWA_W
    nb = _div(s, w)
    nb2 = _div(nb, 2)

    def body(q_ref, kc_ref, kp_ref, vc_ref, vp_ref, sk_ref, do_ref,
             dq_ref, dke_ref, dko_ref, dve_ref, dvo_ref, ds_ref, kcar, vcar):
        n = pl.program_id(0)
        first = n == 0

        @pl.when(first)
        def _():
            kcar[...] = jnp.zeros_like(kcar)
            vcar[...] = jnp.zeros_like(vcar)
            ds_ref[...] = jnp.zeros_like(ds_ref)

        @pl.when(n < nb2)
        def _():
            lane = lax.broadcasted_iota(jnp.int32, (1, LANE), 1)
            dsink = jnp.zeros((1, LANE), F32)
            units = [(b, h, e) for b in range(2) for h in range(SWA_HKV) for e in range(2)]
            kt = [jnp.concatenate([kp_ref[...], kc_ref[0:w, :]], axis=0), kc_ref[...]]
            vt = [jnp.concatenate([vp_ref[...], vc_ref[0:w, :]], axis=0), vc_ref[...]]
            ks = [[_swa_kv_tiles(kt[b], h) for h in range(SWA_HKV)] for b in range(2)]
            vs = [[_swa_kv_tiles(vt[b], h) for h in range(SWA_HKV)] for b in range(2)]
            qs = [[_swa_stack(q_ref, h, b * w).astype(BF16) for h in range(SWA_HKV)] for b in range(2)]
            dos = [[_swa_stack(do_ref, h, b * w).astype(BF16) for h in range(SWA_HKV)] for b in range(2)]
            ps, psinks = _swa_probs([qs[b][h] for b, h, e in units], [ks[b][h][e] for b, h, e in units],
                                    [_swa_sink_col(sk_ref, h, e) for b, h, e in units],
                                    [first if b == 0 else False for b, h, e in units])
            dps = [_dot(dos[b][h], vs[b][h][e], NT) for b, h, e in units]
            dvs = [_dot(p.astype(BF16), dos[b][h], TN) for p, (b, h, e) in zip(ps, units)]
            rss = [jnp.sum(dp * p, axis=-1, keepdims=True) for dp, p in zip(dps, ps)]
            dsb = [(p * (dp - rs) * (SWA_HD ** -0.5)).astype(BF16) for p, dp, rs in zip(ps, dps, rss)]
            dqs = [_dot(d, ks[b][h][e], NN) for d, (b, h, e) in zip(dsb, units)]
            dks = [_dot(d, qs[b][h], TN) for d, (b, h, e) in zip(dsb, units)]
            for u, (b, h, e) in enumerate(units):
                psr = psinks[u] * rss[u]
                for pr in range(SWA_PAIRS):
                    hh = h * SWA_G + 2 * pr + e
                    dsink = dsink + jnp.where(lane == hh, -jnp.sum(psr[pr * w:(pr + 1) * w], axis=0, keepdims=True), 0.0)
            dk_tiles = [jnp.zeros((2 * w, KV_W), F32) for _ in range(2)]
            dv_tiles = [jnp.zeros((2 * w, KV_W), F32) for _ in range(2)]
            for u in range(0, len(units), 2):
                b, h, _ = units[u]
                _swa_unstack(dq_ref, h, (dqs[u] + dqs[u + 1]).astype(BF16), b * w)
                dk_tiles[b] = dk_tiles[b] + _swa_kv_grad(dks[u], dks[u + 1], h)
                dv_tiles[b] = dv_tiles[b] + _swa_kv_grad(dvs[u], dvs[u + 1], h)
            dko_ref[...] = (kcar[...] + dk_tiles[0][:w]).astype(BF16)
            dvo_ref[...] = (vcar[...] + dv_tiles[0][:w]).astype(BF16)
            dke_ref[...] = (dk_tiles[0][w:] + dk_tiles[1][:w]).astype(BF16)
            dve_ref[...] = (dv_tiles[0][w:] + dv_tiles[1][:w]).astype(BF16)
            kcar[...] = dk_tiles[1][w:]
            vcar[...] = dv_tiles[1][w:]
            ds_ref[...] += dsink

        @pl.when(n == nb2)
        def _():
            dko_ref[...] = kcar[...].astype(BF16)
            dvo_ref[...] = vcar[...].astype(BF16)

    last = nb2 - 1
    now = lambda n: jnp.minimum(n, last)
    q_spec = pl.BlockSpec((2 * w, Q_W), lambda n: (now(n), _div(lay.q, Q_W)))
    cur = lambda off: pl.BlockSpec((2 * w, KV_W), lambda n: (now(n), _div(off, KV_W)))
    prev = lambda off: pl.BlockSpec((w, KV_W), lambda n: (jnp.clip(2 * n - 1, 0, nb - 1), _div(off, KV_W)))
    row = pl.BlockSpec((2 * w, Q_W), lambda n: (now(n), 0))
    even = pl.BlockSpec((w, KV_W), lambda n: (now(n), 0))
    odd = pl.BlockSpec((w, KV_W), lambda n: (jnp.maximum(n - 1, 0), 0))
    half = SDS((s // 2, KV_W), BF16)
    dq, dke, dko, dve, dvo, dsk = pl.pallas_call(
        body, name=name, grid=(nb2 + 1,),
        in_specs=[q_spec, cur(lay.k), prev(lay.k), cur(lay.v), prev(lay.v), pl.BlockSpec(sinks.shape, lambda n: (0, 0)), row],
        out_specs=[row, even, odd, even, odd, pl.BlockSpec((1, LANE), lambda n: (0, 0))],
        out_shape=[SDS((s, Q_W), BF16), half, half, half, half, SDS((1, LANE), F32)],
        scratch_shapes=[pltpu.VMEM((w, KV_W), F32), pltpu.VMEM((w, KV_W), F32)],
        compiler_params=_cp(("arbitrary",)))(p_all, p_all, p_all, p_all, p_all, sinks, dy)

    def interleave(ev, od):
        return jnp.stack([ev.reshape(nb2, w, KV_W), od.reshape(nb2, w, KV_W)], axis=1).reshape(s, KV_W)

    return dq, interleave(dke, dko), interleave(dve, dvo), dsk


def _xa_probs(qs, mks):
    ss = [_dot(q, mk, NT) * (XA_D ** -0.5) for q, mk in zip(qs, mks)]
    ps = [jnp.exp(s - jnp.max(s, axis=-1, keepdims=True)) for s in ss]
    inv = [1.0 / jnp.sum(p, axis=-1, keepdims=True) for p in ps]
    return [p * i for p, i in zip(ps, inv)]


def xattn_fwd(p_all, mkv, lay, name):
    s, nm = p_all.shape[0], mkv.shape[0]
    tm = _tile(s, 512)

    def body(q_ref, mkv_ref, o_ref):
        heads = range(XA_H)
        cols = [pl.ds(h * XA_D, XA_D) for h in heads]
        ps = _xa_probs([q_ref[:, c].astype(BF16) for c in cols], [mkv_ref[:, c] for c in cols])
        os = [_dot(ps[h].astype(BF16), mkv_ref[:, pl.ds(XA_W + h * XA_D, XA_D)], NN) for h in heads]
        for h in heads:
            o_ref[:, cols[h]] = os[h].astype(BF16)

    return pl.pallas_call(
        body, name=name, grid=(s // tm,),
        in_specs=[pl.BlockSpec((tm, XA_W), lambda i: (i, _div(lay.qc, XA_W))), pl.BlockSpec((nm, 2 * XA_W), lambda i: (0, 0))],
        out_specs=pl.BlockSpec((tm, XA_W), lambda i: (i, 0)), out_shape=SDS((s, XA_W), BF16),
        compiler_params=_cp(("parallel",)))(p_all, mkv)


def xattn_bwd(p_all, mkv, dy, lay, name):
    s, nm = p_all.shape[0], mkv.shape[0]
    tm = _tile(s, 512)

    def body(q_ref, mkv_ref, do_ref, dq_ref, dmkv_ref):
        @pl.when(pl.program_id(0) == 0)
        def _():
            dmkv_ref[...] = jnp.zeros_like(dmkv_ref)

        heads = range(XA_H)
        cols = [pl.ds(h * XA_D, XA_D) for h in heads]
        vcols = [pl.ds(XA_W + h * XA_D, XA_D) for h in heads]
        qs = [q_ref[:, c].astype(BF16) for c in cols]
        dos = [do_ref[:, c].astype(BF16) for c in cols]
        ps = _xa_probs(qs, [mkv_ref[:, c] for c in cols])
        dps = [_dot(dos[h], mkv_ref[:, vcols[h]], NT) for h in heads]
        dvs = [_dot(ps[h].astype(BF16), dos[h], TN) for h in heads]
        dsb = [(p * (dp - jnp.sum(dp * p, axis=-1, keepdims=True)) * (XA_D ** -0.5)).astype(BF16) for p, dp in zip(ps, dps)]
        dqs = [_dot(dsb[h], mkv_ref[:, cols[h]], NN) for h in heads]
        dks = [_dot(dsb[h], qs[h], TN) for h in heads]
        for h in heads:
            dq_ref[:, cols[h]] = dqs[h].astype(BF16)
            dmkv_ref[:, vcols[h]] += dvs[h]
            dmkv_ref[:, cols[h]] += dks[h]

    row = pl.BlockSpec((tm, XA_W), lambda i: (i, 0))
    full = pl.BlockSpec((nm, 2 * XA_W), lambda i: (0, 0))
    return pl.pallas_call(
        body, name=name, grid=(s // tm,),
        in_specs=[pl.BlockSpec((tm, XA_W), lambda i: (i, _div(lay.qc, XA_W))), full, row],
        out_specs=[row, full], out_shape=[SDS((s, XA_W), BF16), SDS((nm, 2 * XA_W), F32)],
        compiler_params=_cp(("arbitrary",)))(p_all, mkv, dy)


def _shift_down(cur, prev8, s):
    cat = jnp.concatenate([prev8, cur[0:8]], axis=0)
    return pltpu.roll(cur, s, axis=0), pltpu.roll(cat, s, axis=0)[8:16]


def _shift_up(cur, next8, s):
    tm = cur.shape[0]
    cat = jnp.concatenate([cur[tm - 8:tm], next8], axis=0)
    return pltpu.roll(cur, tm - s, axis=0), pltpu.roll(cat, 16 - s, axis=0)[0:8]


def _conv_rows(cur, prev8, w):
    main = w[GDN_CONV - 1:GDN_CONV] * cur
    top = w[GDN_CONV - 1:GDN_CONV] * cur[0:8]
    for sft in range(1, GDN_CONV):
        wi = w[GDN_CONV - 1 - sft:GDN_CONV - sft]
        a, b = _shift_down(cur, prev8, sft)
        main = main + wi * a
        top = top + wi * b
    return jnp.concatenate([top, main[8:]], axis=0)


def _conv_rows_bwd(cur, prev8, d, next8, w):
    tm = cur.shape[0]
    row = lax.broadcasted_iota(jnp.int32, (tm, 1), 0)
    main = w[GDN_CONV - 1:GDN_CONV] * d
    bot = w[GDN_CONV - 1:GDN_CONV] * d[tm - 8:tm]
    dws = [jnp.sum(d * cur, axis=0, keepdims=True)]
    for sft in range(1, GDN_CONV):
        wi = w[GDN_CONV - 1 - sft:GDN_CONV - sft]
        a, b = _shift_up(d, next8, sft)
        main = main + wi * a
        bot = bot + wi * b
        xa, xb = _shift_down(cur, prev8, sft)
        dws.append(jnp.sum(jnp.where(row >= 8, d * xa, 0.0), axis=0, keepdims=True)
                   + jnp.sum(d[0:8] * xb, axis=0, keepdims=True))
    return jnp.concatenate([main[:tm - 8], bot], axis=0), dws


def _gdn_chunk(xq, xk, xv, ab, gp, bdot=_bdot_plain):
    c = GDN_C
    nc = xq.shape[0] // c
    lane = lax.broadcasted_iota(jnp.int32, (c, LANE), 1)
    row = lax.broadcasted_iota(jnp.int32, (c, c), 0)
    col = lax.broadcasted_iota(jnp.int32, (c, c), 1)
    g_tile = -jnp.exp(gp[0:1, :]) * _softplus(ab + gp[1:2, :])
    b_tile = _sigmoid(ab)
    tri = (row >= col).astype(F32)
    qa, ka, va = _silu(xq), _silu(xk), _silu(xv)
    items = []
    for ci in range(nc):
        rs = slice(ci * c, (ci + 1) * c)
        gcum = _dot(tri, g_tile[rs], NN, HI)
        gcum_t = gcum.T
        for h in range(GDN_H):
            hs = slice(h * GDN_D, (h + 1) * GDN_D)
            q, k, v = qa[rs, hs], ka[rs, hs], va[rs, hs]
            q = q * lax.rsqrt(jnp.sum(q * q, axis=-1, keepdims=True) + L2_EPS) * (GDN_D ** -0.5)
            k = k * lax.rsqrt(jnp.sum(k * k, axis=-1, keepdims=True) + L2_EPS)
            gc = jnp.sum(jnp.where(lane == h, gcum, 0.0), axis=1, keepdims=True)
            beta = jnp.sum(jnp.where(lane == GDN_H + h, b_tile[rs], 0.0), axis=1, keepdims=True)
            decay = jnp.exp(jnp.where(row >= col, gc - gcum_t[h:h + 1, :], NEG))
            items.append((q, k, v, gc, beta, decay))
    kks = [bdot(k, k, NT) for (_, k, _, _, _, _) in items]
    xs = tuple(-jnp.where(row > col, it[4] * kk * it[5], 0.0) for it, kk in zip(items, kks))
    nns = _neumann(xs) if bdot is _bdot_plain else _neumann_vjp(xs)
    qks = [bdot(q, k, NT) for (q, k, _, _, _, _) in items]
    out = []
    for (q, k, v, gc, beta, decay), n, qk in zip(items, nns, qks):
        eg = jnp.exp(gc)
        vb = v * beta
        kbe = k * (beta * eg)
        gl = gc[c - 1:c, :]
        out.append((vb + bdot(n, vb, NN), kbe + bdot(n, kbe, NN), q * eg, k * jnp.exp(gl - gc), qk * decay, jnp.exp(gl)))
    return [out[ci * GDN_H:(ci + 1) * GDN_H] for ci in range(nc)]


GDN_CPS = 4


def _gdn_pre_specs(lay, t, tile):
    c0 = _div(lay.qkv, GDN_W)
    cur = [pl.BlockSpec((t, GDN_W), lambda n, j=j: (tile(n), c0 + j)) for j in range(3)]
    prev = [pl.BlockSpec((8, GDN_W), lambda n, j=j: (jnp.maximum(tile(n) * (t // 8) - 1, 0), c0 + j)) for j in range(3)]
    return cur + prev + [pl.BlockSpec((GDN_CONV, 3 * GDN_W), lambda n: (0, 0)),
                         pl.BlockSpec((t, LANE), lambda n: (tile(n), _div(lay.ab, LANE))),
                         pl.BlockSpec((8, LANE), lambda n: (0, 0))]


def _gdn_conv_inputs(x_refs, prev_refs, w_ref, first):
    out = []
    for j in range(3):
        prev8 = jnp.where(first, 0.0, prev_refs[j][...])
        out.append((x_refs[j][...], prev8, w_ref[:, pl.ds(j * GDN_W, GDN_W)]))
    return out


def gdn_pre_fwd(p_all, conv_w, gp, lay, name):
    s = p_all.shape[0]
    c = GDN_C
    n = _div(s, c)
    cps = _tile(n, GDN_CPS)
    t = cps * c

    def body(xq, xk, xv, pq, pk, pv, cw, ab, gp_ref, u_ref, w_ref, qd_ref, kd_ref, qk_ref, gl_ref):
        lane = lax.broadcasted_iota(jnp.int32, (1, LANE), 1)
        xs = [_conv_rows(*a) for a in _gdn_conv_inputs((xq, xk, xv), (pq, pk, pv), cw, pl.program_id(0) == 0)]
        chunks = _gdn_chunk(xs[0], xs[1], xs[2], ab[...], gp_ref[...])
        for ci, heads in enumerate(chunks):
            rs = pl.ds(ci * c, c)
            gl_row = jnp.zeros((1, LANE), F32)
            for h, (u, w, qd, kd, qk, gl) in enumerate(heads):
                hs = pl.ds(h * GDN_D, GDN_D)
                u_ref[rs, hs] = u
                w_ref[rs, hs] = w.astype(BF16)
                qd_ref[rs, hs] = qd.astype(BF16)
                kd_ref[rs, hs] = kd.astype(BF16)
                qk_ref[rs, pl.ds(h * c, c)] = qk.astype(BF16)
                gl_row = gl_row + jnp.where(lane == h, gl, 0.0)
            gl_ref[ci] = gl_row

    row = pl.BlockSpec((t, GDN_W), lambda n: (n, 0))
    return pl.pallas_call(
        body, name=name, grid=(n // cps,), in_specs=_gdn_pre_specs(lay, t, lambda n: n),
        out_specs=[row, row, row, row, pl.BlockSpec((t, GDN_H * c), lambda n: (n, 0)), pl.BlockSpec((cps, 1, LANE), lambda n: (n, 0, 0))],
        out_shape=[SDS((s, GDN_W), F32), SDS((s, GDN_W), BF16), SDS((s, GDN_W), BF16), SDS((s, GDN_W), BF16),
                   SDS((s, GDN_H * c), BF16), SDS((n, 1, LANE), F32)],
        compiler_params=_cp(("parallel",)))(p_all, p_all, p_all, p_all, p_all, p_all, conv_w, p_all, gp)


def gdn_pre_bwd(p_all, conv_w, gp, du, dw, dqd, dkd, dqk, dgl, lay, name):
    s = p_all.shape[0]
    c = GDN_C
    n = _div(s, c)
    cps = _tile(n, GDN_CPS)
    t = cps * c
    steps = n // cps
    chunk = functools.partial(_gdn_chunk, bdot=_bdot_vjp)

    def body(xq, xk, xv, pq, pk, pv, cw, ab, gp_ref, du_r, dw_r, dqd_r, dkd_r, dqk_r, dgl_r,
             dx_ref, dab_ref, dgp_ref, dcw_ref, carry):
        step = pl.program_id(0)

        @pl.when(step == 0)
        def _():
            dgp_ref[...] = jnp.zeros_like(dgp_ref)
            dcw_ref[...] = jnp.zeros_like(dcw_ref)
            carry[...] = jnp.zeros_like(carry)

        lane = lax.broadcasted_iota(jnp.int32, (1, LANE), 1)
        conv_in = _gdn_conv_inputs((xq, xk, xv), (pq, pk, pv), cw, step == steps - 1)
        xs = [_conv_rows(*a) for a in conv_in]
        _, vjp = jax.vjp(chunk, xs[0], xs[1], xs[2], ab[...], gp_ref[...])
        cts = []
        for ci in range(cps):
            rs = pl.ds(ci * c, c)
            heads = []
            for h in range(GDN_H):
                hs = pl.ds(h * GDN_D, GDN_D)
                dgl_h = jnp.sum(jnp.where(lane == h, dgl_r[ci], 0.0), axis=1, keepdims=True)
                heads.append((du_r[rs, hs], dw_r[rs, hs], dqd_r[rs, hs], dkd_r[rs, hs], dqk_r[rs, pl.ds(h * c, c)], dgl_h))
            cts.append(heads)
        *dxs, dab, dgp = vjp(cts)
        for j, (d, (cur, prev8, w)) in enumerate(zip(dxs, conv_in)):
            cols = pl.ds(j * GDN_W, GDN_W)
            dx, dws = _conv_rows_bwd(cur, prev8, d, carry[:, cols], w)
            carry[:, cols] = d[0:8]
            dx_ref[:, cols] = dx.astype(BF16)
            for sft in range(GDN_CONV):
                dcw_ref[GDN_CONV - 1 - sft:GDN_CONV - sft, cols] += dws[sft]
        dab_ref[...] = dab.astype(BF16)
        dgp_ref[...] += dgp

    tile = lambda i: steps - 1 - i
    row = pl.BlockSpec((t, GDN_W), lambda i: (tile(i), 0))
    return pl.pallas_call(
        body, name=name, grid=(steps,),
        in_specs=_gdn_pre_specs(lay, t, tile) + [row, row, row, row, pl.BlockSpec((t, GDN_H * c), lambda i: (tile(i), 0)),
                                                 pl.BlockSpec((cps, 1, LANE), lambda i: (tile(i), 0, 0))],
        out_specs=[pl.BlockSpec((t, 3 * GDN_W), lambda i: (tile(i), 0)), pl.BlockSpec((t, LANE), lambda i: (tile(i), 0)),
                   pl.BlockSpec((8, LANE), lambda i: (0, 0)), pl.BlockSpec((GDN_CONV, 3 * GDN_W), lambda i: (0, 0))],
        out_shape=[SDS((s, 3 * GDN_W), BF16), SDS((s, LANE), BF16), SDS((8, LANE), F32), SDS((GDN_CONV, 3 * GDN_W), F32)],
        scratch_shapes=[pltpu.VMEM((8, 3 * GDN_W), F32)],
        compiler_params=_cp(("arbitrary",)))(p_all, p_all, p_all, p_all, p_all, p_all, conv_w, p_all, gp,
                                             du, dw, dqd, dkd, dqk, dgl)


def _lane_scalar(row, h):
    lane = lax.broadcasted_iota(jnp.int32, row.shape, 1)
    return jnp.sum(jnp.where(lane == h, row, 0.0), axis=1, keepdims=True)


def _gdn_out_head(oh, zh, nw):
    return oh * lax.rsqrt(jnp.mean(oh * oh, axis=-1, keepdims=True) + RMS_EPS) * nw * _silu(zh)


def gdn_scan_fwd(u, w, qd, kd, qk, gl, p_all, nw, lay, name):
    s = u.shape[0]
    c = GDN_C
    n = _div(s, c)
    cps = _tile(n, GDN_CPS)
    t = cps * c

    def body(u_r, w_r, qd_r, kd_r, qk_r, gl_r, z_r, nw_r, o_ref, s_ref, y_ref, st):
        @pl.when(pl.program_id(0) == 0)
        def _():
            st[...] = jnp.zeros_like(st)

        heads = range(GDN_H)
        hs = [pl.ds(h * GDN_D, GDN_D) for h in heads]
        for ci in range(cps):
            rs = pl.ds(ci * c, c)
            s_ref[ci] = st[...]
            sh = [st[hs[h], :] for h in heads]
            shb = [x.astype(BF16) for x in sh]
            ws = [_dot(w_r[rs, hs[h]], shb[h], NN) for h in heads]
            qs = [_dot(qd_r[rs, hs[h]], shb[h], NN) for h in heads]
            vb = [(u_r[rs, hs[h]] - ws[h]).astype(BF16) for h in heads]
            ov = [_dot(qk_r[rs, pl.ds(h * c, c)], vb[h], NN) for h in heads]
            kv = [_dot(kd_r[rs, hs[h]], vb[h], TN) for h in heads]
            os = [qs[h] + ov[h] for h in heads]
            for h in heads:
                o_ref[rs, hs[h]] = os[h]
                y_ref[rs, hs[h]] = _gdn_out_head(os[h], z_r[rs, hs[h]], nw_r[...]).astype(BF16)
                st[hs[h], :] = sh[h] * _lane_scalar(gl_r[ci], h) + kv[h]

    row = pl.BlockSpec((t, GDN_W), lambda i: (i, 0))
    return pl.pallas_call(
        body, name=name, grid=(n // cps,),
        in_specs=[row, row, row, row, pl.BlockSpec((t, GDN_H * c), lambda i: (i, 0)), pl.BlockSpec((cps, 1, LANE), lambda i: (i, 0, 0)),
                  pl.BlockSpec((t, GDN_W), lambda i: (i, _div(lay.z, GDN_W))), pl.BlockSpec((1, GDN_D), lambda i: (0, 0))],
        out_specs=[row, pl.BlockSpec((cps, GDN_W, GDN_D), lambda i: (i, 0, 0)), row],
        out_shape=[SDS((s, GDN_W), F32), SDS((n, GDN_W, GDN_D), F32), SDS((s, GDN_W), BF16)],
        scratch_shapes=[pltpu.VMEM((GDN_W, GDN_D), F32)],
        compiler_params=_cp(("arbitrary",)))(u, w, qd, kd, qk, gl, p_all, nw)


def gdn_scan_bwd(u, w, qd, kd, qk, gl, states, o, dy, p_all, nw, lay, name):
    s = u.shape[0]
    c = GDN_C
    n = _div(s, c)
    cps = _tile(n, GDN_CPS)
    t = cps * c
    steps = n // cps

    def body(u_r, w_r, qd_r, kd_r, qk_r, gl_r, s_r, o_r, dy_r, z_r, nw_r,
             du_o, dw_o, dqd_o, dkd_o, dqk_o, dgl_o, dz_o, dnw_o, dst):
        @pl.when(pl.program_id(0) == 0)
        def _():
            dst[...] = jnp.zeros_like(dst)
            dnw_o[...] = jnp.zeros_like(dnw_o)

        lane = lax.broadcasted_iota(jnp.int32, (1, LANE), 1)
        heads = range(GDN_H)
        hs = [pl.ds(h * GDN_D, GDN_D) for h in heads]
        qs = [pl.ds(h * c, c) for h in heads]
        for ci in reversed(range(cps)):
            rs = pl.ds(ci * c, c)
            outs = [jax.vjp(_gdn_out_head, o_r[rs, hs[h]], z_r[rs, hs[h]], nw_r[...])[1](dy_r[rs, hs[h]].astype(F32))
                    for h in heads]
            for h in heads:
                dz_o[rs, hs[h]] = outs[h][1].astype(BF16)
                dnw_o[...] += outs[h][2]
            sh = [s_r[ci, hs[h], :] for h in heads]
            shb = [x.astype(BF16) for x in sh]
            ds_out = [dst[hs[h], :] for h in heads]
            dsb = [x.astype(BF16) for x in ds_out]
            dob = [outs[h][0].astype(BF16) for h in heads]
            ws = [_dot(w_r[rs, hs[h]], shb[h], NN) for h in heads]
            dv1 = [_dot(qk_r[rs, qs[h]], dob[h], TN) for h in heads]
            dv2 = [_dot(kd_r[rs, hs[h]], dsb[h], NN) for h in heads]
            dqd = [_dot(dob[h], shb[h], NT) for h in heads]
            dsq = [_dot(qd_r[rs, hs[h]], dob[h], TN) for h in heads]
            vb = [(u_r[rs, hs[h]] - ws[h]).astype(BF16) for h in heads]
            dv = [dv1[h] + dv2[h] for h in heads]
            dvb = [x.astype(BF16) for x in dv]
            dw = [_dot(dvb[h], shb[h], NT) for h in heads]
            dkd = [_dot(vb[h], dsb[h], NT) for h in heads]
            dqk = [_dot(dob[h], vb[h], NT) for h in heads]
            dsw = [_dot(w_r[rs, hs[h]], dvb[h], TN) for h in heads]
            dgl_row = jnp.zeros((1, LANE), F32)
            for h in heads:
                du_o[rs, hs[h]] = dv[h]
                dw_o[rs, hs[h]] = -dw[h]
                dqd_o[rs, hs[h]] = dqd[h]
                dkd_o[rs, hs[h]] = dkd[h]
                dqk_o[rs, qs[h]] = dqk[h]
                dgl_row = dgl_row + jnp.where(lane == h, jnp.sum(jnp.sum(ds_out[h] * sh[h], axis=1, keepdims=True), axis=0, keepdims=True), 0.0)
                dst[hs[h], :] = ds_out[h] * _lane_scalar(gl_r[ci], h) + dsq[h] - dsw[h]
            dgl_o[ci] = dgl_row

    rev = lambda i: steps - 1 - i
    row = pl.BlockSpec((t, GDN_W), lambda i: (rev(i), 0))
    qks = pl.BlockSpec((t, GDN_H * c), lambda i: (rev(i), 0))
    gls = pl.BlockSpec((cps, 1, LANE), lambda i: (rev(i), 0, 0))
    zs = pl.BlockSpec((t, GDN_W), lambda i: (rev(i), _div(lay.z, GDN_W)))
    nws = pl.BlockSpec((1, GDN_D), lambda i: (0, 0))
    return pl.pallas_call(
        body, name=name, grid=(steps,),
        in_specs=[row, row, row, row, qks, gls, pl.BlockSpec((cps, GDN_W, GDN_D), lambda i: (rev(i), 0, 0)), row, row, zs, nws],
        out_specs=[row, row, row, row, qks, gls, row, nws],
        out_shape=[SDS((s, GDN_W), F32)] * 4 + [SDS((s, GDN_H * c), F32), SDS((n, 1, LANE), F32), SDS((s, GDN_W), BF16),
                                                SDS((1, GDN_D), F32)],
        scratch_shapes=[pltpu.VMEM((GDN_W, GDN_D), F32)],
        compiler_params=_cp(("arbitrary",)))(u, w, qd, kd, qk, gl, states, o, dy, p_all, nw)


def _cols_to_full(g):
    n, k, c = g.shape
    return g.transpose(1, 0, 2).reshape(k, n * c)


def _rows_to_blocks(w):
    return w.reshape(N_DEV, w.shape[0] // N_DEV, w.shape[1])


def _pack_small(parts, rows):
    flat = jnp.concatenate([jnp.pad(p.reshape(-1), (0, -p.size % LANE)) for p in parts])
    return jnp.pad(flat, (0, rows * LANE - flat.size)).reshape(rows, LANE)


def kernel(x, mem, g_mix, w_in, sinks, conv_w, a_log, dt_bias, gdn_norm_w, g_mem, w_mem_kv, w_swa_up, w_gdn_up, w_xa_up, w_out, g_mlp, w_mlp_in, w_mlp_out, g_final, loss_target, m_g_mix, m_w_in, m_sinks, m_conv_w, m_a_log, m_dt_bias, m_gdn_norm_w, m_g_mem, m_w_mem_kv, m_w_swa_up, m_w_gdn_up, m_w_xa_up, m_w_out, m_g_mlp, m_w_mlp_in, m_w_mlp_out, m_g_final, v_g_mix, v_w_in, v_sinks, v_conv_w, v_a_log, v_dt_bias, v_gdn_norm_w, v_g_mem, v_w_mem_kv, v_w_swa_up, v_w_gdn_up, v_w_xa_up, v_w_out, v_g_mlp, v_w_mlp_in, v_w_mlp_out, v_g_final):
    xs, ms, tgt = x[0], mem[0], loss_target[0]
    s, d = xs.shape
    lay = Layout(d)
    px, py, pc = _position()
    dev = 4 * px + 2 * py + pc

    n1, g_in, g_conv = rmsnorm_fwd(xs, g_mix, "norm_mix", side=GatherJob([w_in[0].astype(BF16), conv_w[0]]))
    W_in = pad_w_in(g_in, lay)
    convw = _cols_to_full(g_conv)
    gp = jnp.zeros((8, LANE), F32).at[0, :GDN_H].set(a_log[0]).at[1, :GDN_H].set(dt_bias[0])
    later = [w_mem_kv[0], w_swa_up[0], w_gdn_up[0], w_xa_up[0], w_out[0], w_mlp_in[0]]

    p_all, g_mkv, W_sup, W_gup, W_xup, g_out, W_m1 = matmul(
        n1, W_in, mode="nn", out_dtype=F32, name="proj_in", tm=2048, tn=1024, tk=d,
        side=GatherJob([w.astype(BF16) for w in later]))
    W_mkv = g_mkv.reshape(-1, g_mkv.shape[2])
    W_out = g_out.reshape(-1, d)
    y_a = swa_fwd(p_all, sinks, lay, "swa_fwd")
    u, gw, gqd, gkd, gqk, ggl = gdn_pre_fwd(p_all, convw, gp, lay, "gdn_pre_fwd")
    o_b, states, y_b = gdn_scan_fwd(u, gw, gqd, gkd, gqk, ggl, p_all, gdn_norm_w, lay, "gdn_scan_fwd")
    nm = rmsnorm_fwd(ms, g_mem, "norm_mem")
    mkv = matmul(nm, W_mkv, mode="nn", out_dtype=BF16, name="proj_mem", tk=d)
    y_c = xattn_fwd(p_all, mkv, lay, "xattn_fwd")
    merged = merge(p_all, y_a, y_b, y_c, W_sup, W_gup, W_xup, None, lay, "merge_fwd")
    h1, n2 = matmul(merged, W_out, mode="nn", out_dtype=F32, name="proj_out", tm=512, tn=d, tk=d, resid=xs, rms_gain=g_mlp)
    uu, act, g_m2 = matmul(n2, W_m1, mode="nn", out_dtype=F32, name="mlp_in", tm=2048, tn=512, tk=d, b_cols=True,
                           relu2_out=True, side=GatherJob([w_mlp_out[0].astype(BF16)]))
    W_m2 = g_m2.reshape(-1, d)
    dh2, dh2_b, dg_final, lrow = matmul(act, W_m2, mode="nn", out_dtype=F32, name="mlp_out_loss", tm=512, tn=d, tk=1024,
                                        resid=h1, rms_gain=g_final.reshape(1, d), loss_target=tgt)
    loss = lax.psum(lrow[0, 0], ("x", "y", "c"))

    du = matmul(dh2_b, W_m2, mode="nt", out_dtype=BF16, name="mlp_out_dx", tm=2048, tn=512, tk=d, relu2_grad_of=uu)
    dW_m2 = matmul(act, dh2_b, mode="tn", out_dtype=BF16, name="mlp_out_dw", tm=1024, tn=2048, tk=1024)
    dW_m2 = _rows_to_blocks(dW_m2)
    dn2, sib_m2 = matmul(du, W_m1, mode="nt", out_dtype=F32, name="mlp_in_dx", tm=1024, tn=2048, tk=1024, b_cols=True,
                         side=PairExchangeJob([dW_m2], [False]))
    c_m2 = pair_add(dW_m2, False, sib_m2, "grads_pair_add_m2")
    dW_m1 = matmul(n2, du, mode="tn", out_dtype=BF16, name="mlp_in_dw", tm=2048, tn=1024, tk=1024)
    dh1, dg_mlp, dh1_b = rmsnorm_bwd(h1, g_mlp, dn2, dh2, "norm_mlp_bwd", bf16_copy=True)

    dmerged, sib_m1 = matmul(dh1_b, W_out, mode="nt", out_dtype=F32, name="proj_out_dx", tm=1024, tn=d, tk=d,
                             side=PairExchangeJob([dW_m1], [True]))
    c_m1 = pair_add(dW_m1, True, sib_m1, "grads_pair_add_m1")
    dW_out = matmul(merged, dh1_b, mode="tn", out_dtype=BF16, name="proj_out_dw", tm=2048, tn=2048, tk=1024)
    dgates, dta, dtb, dtc = merge(p_all, y_a, y_b, y_c, W_sup, W_gup, W_xup, dmerged, lay, "merge_bwd")
    dy_a = matmul(dta, W_sup, mode="nt", out_dtype=BF16, name="swa_up_dx", tm=2048, tk=d, b_cols=True)
    dy_b = matmul(dtb, W_gup, mode="nt", out_dtype=BF16, name="gdn_up_dx", tm=2048, tk=d, b_cols=True)
    dy_c = matmul(dtc, W_xup, mode="nt", out_dtype=BF16, name="xa_up_dx", tm=2048, tk=d, b_cols=True)
    dW_sup = matmul(y_a, dta, mode="tn", out_dtype=BF16, name="swa_up_dw", tn=2048, tk=2048)
    dW_gup = matmul(y_b, dtb, mode="tn", out_dtype=BF16, name="gdn_up_dw", tn=2048, tk=2048)
    dW_xup = matmul(y_c, dtc, mode="tn", out_dtype=BF16, name="xa_up_dw", tn=2048, tk=2048)

    dq_a, dk_a, dv_a, dsinks = swa_bwd(p_all, sinks, dy_a, lay, "swa_bwd")
    dq_c, dmkv = xattn_bwd(p_all, mkv, dy_c, lay, "xattn_bwd")
    dW_mkv = matmul(nm, dmkv, mode="tn", out_dtype=BF16, name="proj_mem_dw", tk=256)
    dnm = matmul(dmkv, W_mkv, mode="nt", out_dtype=F32, name="proj_mem_dx", tk=1024)
    _, dg_mem = rmsnorm_bwd(ms, g_mem, dnm, None, "norm_mem_bwd")

    du_g, dw_g, dqd_g, dkd_g, dqk_g, dgl_g, dz, dnorm_w = gdn_scan_bwd(
        u, gw, gqd, gkd, gqk, ggl, states, o_b, dy_b, p_all, gdn_norm_w, lay, "gdn_scan_bwd")
    dqkv, dab, dgp, dconv = gdn_pre_bwd(p_all, convw, gp, du_g, dw_g, dqd_g, dkd_g, dqk_g, dgl_g, lay, "gdn_pre_bwd")

    drest = jnp.concatenate([dq_a, dqkv, dz, dq_c, dk_a, dv_a, dab, jnp.zeros((s, lay.pw - lay.end), BF16)], axis=1)
    def pair_stage(grads, cols, tag):
        from_sib = run_job(PairExchangeJob(grads, cols), "grads_pair_exchange_" + tag)
        return [pair_add(g, cl, o, "grads_pair_add_%s%d" % (tag, i)) for i, (g, cl, o) in enumerate(zip(grads, cols, from_sib))]

    small = pair_stage([_rows_to_blocks(dW_mkv), dW_sup, dW_gup, dW_xup, _rows_to_blocks(dW_out)],
                       [False, True, True, True, False], "a")
    dW_in, p_m1, p_m2 = matmul(n1, dgates, tail=drest, mode="tn", out_dtype=BF16, name="proj_in_dw", tm=2048, tn=1024, tk=1024,
                               side=ChipExchangeJob([c_m1, c_m2]))
    late = pair_stage([unpad_dw_in(dW_in, lay)], [False], "b")
    dn1, p_in, p_mkv, p_sup, p_gup, p_xup, p_out = matmul(
        dgates, W_in, tail=drest, mode="nt", out_dtype=F32, name="proj_in_dx", tm=1024, tn=2048, tk=1024,
        side=ChipExchangeJob(late + small))
    grad_x, dg_mix = rmsnorm_bwd(xs, g_mix, dn1, dh1, "norm_mix_bwd")
    parts = [p_in, p_mkv, p_sup, p_gup, p_xup, p_out, p_m1, p_m2]

    shard_names = [(w_in, m_w_in, v_w_in), (w_mem_kv, m_w_mem_kv, v_w_mem_kv), (w_swa_up, m_w_swa_up, v_w_swa_up),
                   (w_gdn_up, m_w_gdn_up, v_w_gdn_up), (w_xa_up, m_w_xa_up, v_w_xa_up), (w_out, m_w_out, v_w_out),
                   (w_mlp_in, m_w_mlp_in, v_w_mlp_in), (w_mlp_out, m_w_mlp_out, v_w_mlp_out)]
    big_res = [adamw(p, w[0], m[0], v[0], "adamw_%d" % i) for i, (p, (w, m, v)) in enumerate(zip(parts, shard_names))]

    smalls = [(g_mix, m_g_mix, v_g_mix, dg_mix), (sinks, m_sinks, v_sinks, dsinks[:, :SWA_HQ]),
              (a_log, m_a_log, v_a_log, dgp[0:1, :GDN_H]), (dt_bias, m_dt_bias, v_dt_bias, dgp[1:2, :GDN_H]),
              (gdn_norm_w, m_gdn_norm_w, v_gdn_norm_w, dnorm_w), (g_mem, m_g_mem, v_g_mem, dg_mem),
              (g_mlp, m_g_mlp, v_g_mlp, dg_mlp), (g_final, m_g_final, v_g_final, dg_final)]
    sizes = [-(-t[0].size // LANE) * LANE for t in smalls] + [GDN_CONV * 3 * GDN_W]
    rows = -(-sum(sizes) // (8 * LANE)) * 8
    csh = conv_w.shape[2]

    def conv_place(a):
        full = jnp.tile(a[0], (1, N_DEV))
        owner = lax.broadcasted_iota(jnp.int32, full.shape, 1) // csh
        return jnp.where(owner == dev, full, 0.0)

    g_pack = _pack_small([t[3] for t in smalls] + [dconv], rows)
    w_pack = _pack_small([t[0] for t in smalls] + [conv_place(conv_w)], rows)
    m_pack = _pack_small([t[1] for t in smalls] + [conv_place(m_conv_w)], rows)
    v_pack = _pack_small([t[2] for t in smalls] + [conv_place(v_conv_w)], rows)
    g_all = run_job(GatherJob([g_pack]), "gather_small_grads")[0]
    small_res = adamw(g_all, w_pack, m_pack, v_pack, "adamw_small")

    def unpack(arr):
        flat = arr.reshape(-1)
        outs, off = [], 0
        for t, sz in zip(smalls, sizes[:-1]):
            outs.append(flat[off:off + t[0].size].reshape(t[0].shape))
            off += sz
        cw = flat[off:off + sizes[-1]].reshape(GDN_CONV, 3 * GDN_W)
        mine = (lax.broadcasted_iota(jnp.int32, (1, N_DEV, 1), 1) == dev).astype(F32)
        outs.append(jnp.sum(cw.reshape(GDN_CONV, N_DEV, csh) * mine, axis=1)[None])
        return outs

    sg, sd, sm, sv = (unpack(a) for a in small_res)
    bg, bd, bm, bv = ([r[i][None] for r in big_res] for i in range(4))

    def ordered(sm_, bg_):
        return [sm_[0], bg_[0], sm_[1], sm_[8], sm_[2], sm_[3], sm_[4], sm_[5], bg_[1], bg_[2], bg_[3], bg_[4], bg_[5],
                sm_[6], bg_[6], bg_[7], sm_[7]]

    return (loss, grad_x[None], *ordered(sg, bg), *ordered(sd, bd), *ordered(sm, bm), *ordered(sv, bv))
```
